```python
import jax, jax.numpy as jnp
from jax import lax
import numpy as np

D_MODEL = 1024
BATCH = 16
SEQ = 2048
DEPTH = 1

N_META = 16
D_MIX = D_MODEL
D_CONV = D_MIX // 2
CONV_WIDTH = 31
GLA_HEADS = 4
GLA_DV = (D_MIX - D_CONV) // GLA_HEADS
GLA_DK = GLA_DV // 2
GLA_GATE_RANK = 16
GLA_TAU = 16.0
CHUNK = 64
D_FF = 4 * D_MODEL
LN_EPS = 1e-5
DEEPNORM_ALPHA = (2.0 * DEPTH) ** 0.25
DEEPNORM_BETA = (8.0 * DEPTH) ** -0.25

SPLIT_SIZES = (D_CONV, D_CONV,
               GLA_HEADS * GLA_DK, GLA_HEADS * GLA_DK,
               GLA_HEADS * GLA_DV, GLA_HEADS * GLA_DV,
               GLA_GATE_RANK)
D_IN = sum(SPLIT_SIZES)
SPLIT_IDX = tuple(int(i) for i in np.cumsum(SPLIT_SIZES)[:-1])

kernel_name = "hymba_conformer_gla_deepnorm"


def layer_norm(x, g, b):
    xf = x.astype(jnp.float32)
    mu = jnp.mean(xf, axis=-1, keepdims=True)
    var = jnp.mean(jnp.square(xf - mu), axis=-1, keepdims=True)
    y = (xf - mu) * lax.rsqrt(var + LN_EPS)
    return (y * g.astype(jnp.float32) + b.astype(jnp.float32)).astype(x.dtype)


def rms_norm(x, g):
    xf = x.astype(jnp.float32)
    y = xf * lax.rsqrt(jnp.mean(jnp.square(xf), axis=-1, keepdims=True) + LN_EPS)
    return y * g.astype(jnp.float32)


def conformer_conv(a, gate, conv_w, conv_b, ln_g, ln_b):
    h = a * jax.nn.sigmoid(gate)
    h = lax.conv_general_dilated(
        h, conv_w[:, None, :].astype(h.dtype),
        window_strides=(1,), padding=[(CONV_WIDTH - 1, 0)],
        dimension_numbers=("NWC", "WIO", "NWC"),
        feature_group_count=D_CONV) + conv_b
    return jax.nn.silu(layer_norm(h, ln_g, ln_b))


def gla_chunked(q, k, v, log_g):
    B, T = q.shape[0], q.shape[1]
    pad = (-N_META) % CHUNK
    padw = ((0, 0), (pad, 0), (0, 0), (0, 0))
    q, k, v, log_g = [jnp.pad(t.astype(jnp.float32), padw) for t in (q, k, v, log_g)]
    L = T + pad
    N = L // CHUNK

    def to_chunks(t):
        return t.reshape(B, N, CHUNK, GLA_HEADS, t.shape[-1]).transpose(0, 3, 1, 2, 4)

    q, k, v, log_g = map(to_chunks, (q, k, v, log_g))
    q = q * (GLA_DK ** -0.5)
    b = jnp.cumsum(log_g, axis=3)
    b_last = b[:, :, :, -1:, :]
    qe = q * jnp.exp(b)
    ke = k * jnp.exp(-b)
    kd = k * jnp.exp(b_last - b)

    mask = jnp.tril(jnp.ones((CHUNK, CHUNK), dtype=bool))
    A = jnp.einsum("bhncd,bhnsd->bhncs", qe, ke)
    A = jnp.where(mask, A, 0.0)
    o_intra = jnp.einsum("bhncs,bhnse->bhnce", A, v)

    dS = jnp.einsum("bhncd,bhnce->bhnde", kd, v)
    decay = jnp.exp(b_last[:, :, :, 0, :])

    def step(S, xs):
        dec, upd = xs
        return dec[..., None] * S + upd, S

    S0 = jnp.zeros((B, GLA_HEADS, GLA_DK, GLA_DV), jnp.float32)
    _, S_before = lax.scan(step, S0, (jnp.moveaxis(decay, 2, 0), jnp.moveaxis(dS, 2, 0)))
    S_before = jnp.moveaxis(S_before, 0, 2)
    o_inter = jnp.einsum("bhncd,bhnde->bhnce", qe, S_before)

    o = (o_intra + o_inter).transpose(0, 2, 3, 1, 4).reshape(B, L, GLA_HEADS, GLA_DV)
    return o[:, pad:]


def _fwd_setup_inputs(seed: int = 0) -> dict:
    key = jax.random.key(seed)
    ks = jax.random.split(key, 20)
    f32 = jnp.float32

    def nrm(k, shape, scale):
        return jax.random.normal(k, shape, f32) * scale

    def gain(k, shape):
        return 1.0 + 0.02 * jax.random.normal(k, shape, f32)

    return {
        "x": jax.random.normal(ks[0], (BATCH, SEQ, D_MODEL), f32),
        "meta_tokens": nrm(ks[1], (N_META, D_MODEL), 1.0),
        "ln_in_g": gain(ks[2], (D_MODEL,)),
        "ln_in_b": nrm(ks[3], (D_MODEL,), 0.02),
        "w_in": nrm(ks[4], (DEPTH, D_MODEL, D_IN), D_MODEL ** -0.5),
        "conv_w": nrm(ks[5], (DEPTH, CONV_WIDTH, D_CONV), CONV_WIDTH ** -0.5),
        "conv_b": nrm(ks[6], (DEPTH, D_CONV), 0.02),
        "conv_ln_g": gain(ks[7], (DEPTH, D_CONV)),
        "conv_ln_b": nrm(ks[8], (DEPTH, D_CONV), 0.02),
        "gate_up": nrm(ks[9], (DEPTH, GLA_GATE_RANK, GLA_HEADS * GLA_DK), GLA_GATE_RANK ** -0.5),
        "gate_bias": nrm(ks[10], (DEPTH, GLA_HEADS * GLA_DK), 0.02),
        "gla_norm_g": gain(ks[11], (DEPTH, GLA_DV)),
        "w_out": nrm(ks[12], (DEPTH, D_MIX, D_MODEL), DEEPNORM_BETA * D_MIX ** -0.5),
        "ln1_g": gain(ks[13], (DEPTH, D_MODEL)),
        "ln1_b": nrm(ks[14], (DEPTH, D_MODEL), 0.02),
        "w_ff1": nrm(ks[15], (DEPTH, D_MODEL, D_FF), D_MODEL ** -0.5),
        "w_ff2": nrm(ks[16], (DEPTH, D_FF, D_MODEL), DEEPNORM_BETA * D_FF ** -0.5),
        "ln2_g": gain(ks[17], (DEPTH, D_MODEL)),
        "ln2_b": nrm(ks[18], (DEPTH, D_MODEL), 0.02),
    }


def _fwd_reference(x, meta_tokens, ln_in_g, ln_in_b, w_in, conv_w, conv_b, conv_ln_g, conv_ln_b,
              gate_up, gate_bias, gla_norm_g, w_out, ln1_g, ln1_b, w_ff1, w_ff2, ln2_g, ln2_b):
    B = x.shape[0]
    meta = jnp.broadcast_to(meta_tokens.astype(x.dtype)[None], (B, N_META, D_MODEL))
    s = jnp.concatenate([meta, x], axis=1)
    s = layer_norm(s, ln_in_g, ln_in_b)
    T = s.shape[1]

    for l in range(DEPTH):
        u = s @ w_in[l]
        c_val, c_gate, q, k, v, r, g_down = jnp.split(u, SPLIT_IDX, axis=-1)

        conv_out = conformer_conv(c_val, c_gate, conv_w[l], conv_b[l], conv_ln_g[l], conv_ln_b[l])

        z = (g_down @ gate_up[l] + gate_bias[l]).astype(jnp.float32)
        log_g = jax.nn.log_sigmoid(z) / GLA_TAU
        hd = lambda t, d: t.reshape(B, T, GLA_HEADS, d)
        o = gla_chunked(hd(q, GLA_DK), hd(k, GLA_DK), hd(v, GLA_DV), hd(log_g, GLA_DK))
        o = rms_norm(o, gla_norm_g[l]) * jax.nn.silu(hd(r, GLA_DV).astype(jnp.float32))
        gla_out = o.reshape(B, T, GLA_HEADS * GLA_DV).astype(s.dtype)

        mix = jnp.concatenate([conv_out, gla_out], axis=-1) @ w_out[l]
        s = layer_norm(DEEPNORM_ALPHA * s + mix, ln1_g[l], ln1_b[l])

        f = jnp.square(jax.nn.relu(s @ w_ff1[l])) @ w_ff2[l]
        s = layer_norm(DEEPNORM_ALPHA * s + f, ln2_g[l], ln2_b[l])

    return s[:, N_META:]


import jax as _jax
import jax.numpy as _jnp

TWIN_FORMAT = 'train_step'
FWD_PARAMS = ['x', 'meta_tokens', 'ln_in_g', 'ln_in_b', 'w_in', 'conv_w', 'conv_b', 'conv_ln_g', 'conv_ln_b', 'gate_up', 'gate_bias', 'gla_norm_g', 'w_out', 'ln1_g', 'ln1_b', 'w_ff1', 'w_ff2', 'ln2_g', 'ln2_b']
TWIN_WEIGHTS = ['meta_tokens', 'ln_in_g', 'ln_in_b', 'w_in', 'conv_w', 'conv_b', 'conv_ln_g', 'conv_ln_b', 'gate_up', 'gate_bias', 'gla_norm_g', 'w_out', 'ln1_g', 'ln1_b', 'w_ff1', 'w_ff2', 'ln2_g', 'ln2_b']
TWIN_DIFF_INPUT = 'x'
TWIN_INPUTS = ['x', 'meta_tokens', 'ln_in_g', 'ln_in_b', 'w_in', 'conv_w', 'conv_b', 'conv_ln_g', 'conv_ln_b', 'gate_up', 'gate_bias', 'gla_norm_g', 'w_out', 'ln1_g', 'ln1_b', 'w_ff1', 'w_ff2', 'ln2_g', 'ln2_b', 'loss_target', 'm_meta_tokens', 'm_ln_in_g', 'm_ln_in_b', 'm_w_in', 'm_conv_w', 'm_conv_b', 'm_conv_ln_g', 'm_conv_ln_b', 'm_gate_up', 'm_gate_bias', 'm_gla_norm_g', 'm_w_out', 'm_ln1_g', 'm_ln1_b', 'm_w_ff1', 'm_w_ff2', 'm_ln2_g', 'm_ln2_b', 'v_meta_tokens', 'v_ln_in_g', 'v_ln_in_b', 'v_w_in', 'v_conv_w', 'v_conv_b', 'v_conv_ln_g', 'v_conv_ln_b', 'v_gate_up', 'v_gate_bias', 'v_gla_norm_g', 'v_w_out', 'v_ln1_g', 'v_ln1_b', 'v_w_ff1', 'v_w_ff2', 'v_ln2_g', 'v_ln2_b']
TWIN_OUTPUTS = ['loss', 'grad_x', 'grad_meta_tokens', 'grad_ln_in_g', 'grad_ln_in_b', 'grad_w_in', 'grad_conv_w', 'grad_conv_b', 'grad_conv_ln_g', 'grad_conv_ln_b', 'grad_gate_up', 'grad_gate_bias', 'grad_gla_norm_g', 'grad_w_out', 'grad_ln1_g', 'grad_ln1_b', 'grad_w_ff1', 'grad_w_ff2', 'grad_ln2_g', 'grad_ln2_b', 'delta_meta_tokens', 'delta_ln_in_g', 'delta_ln_in_b', 'delta_w_in', 'delta_conv_w', 'delta_conv_b', 'delta_conv_ln_g', 'delta_conv_ln_b', 'delta_gate_up', 'delta_gate_bias', 'delta_gla_norm_g', 'delta_w_out', 'delta_ln1_g', 'delta_ln1_b', 'delta_w_ff1', 'delta_w_ff2', 'delta_ln2_g', 'delta_ln2_b', 'new_m_meta_tokens', 'new_m_ln_in_g', 'new_m_ln_in_b', 'new_m_w_in', 'new_m_conv_w', 'new_m_conv_b', 'new_m_conv_ln_g', 'new_m_conv_ln_b', 'new_m_gate_up', 'new_m_gate_bias', 'new_m_gla_norm_g', 'new_m_w_out', 'new_m_ln1_g', 'new_m_ln1_b', 'new_m_w_ff1', 'new_m_w_ff2', 'new_m_ln2_g', 'new_m_ln2_b', 'new_v_meta_tokens', 'new_v_ln_in_g', 'new_v_ln_in_b', 'new_v_w_in', 'new_v_conv_w', 'new_v_conv_b', 'new_v_conv_ln_g', 'new_v_conv_ln_b', 'new_v_gate_up', 'new_v_gate_bias', 'new_v_gla_norm_g', 'new_v_w_out', 'new_v_ln1_g', 'new_v_ln1_b', 'new_v_w_ff1', 'new_v_w_ff2', 'new_v_ln2_g', 'new_v_ln2_b']
TWIN_LEAF_KINDS = {'loss': 'loss', 'grad_x': 'grad_x', 'grad_meta_tokens': 'grad_w', 'grad_ln_in_g': 'grad_w', 'grad_ln_in_b': 'grad_w', 'grad_w_in': 'grad_w', 'grad_conv_w': 'grad_w', 'grad_conv_b': 'grad_w', 'grad_conv_ln_g': 'grad_w', 'grad_conv_ln_b': 'grad_w', 'grad_gate_up': 'grad_w', 'grad_gate_bias': 'grad_w', 'grad_gla_norm_g': 'grad_w', 'grad_w_out': 'grad_w', 'grad_ln1_g': 'grad_w', 'grad_ln1_b': 'grad_w', 'grad_w_ff1': 'grad_w', 'grad_w_ff2': 'grad_w', 'grad_ln2_g': 'grad_w', 'grad_ln2_b': 'grad_w', 'delta_meta_tokens': 'delta_w', 'delta_ln_in_g': 'delta_w', 'delta_ln_in_b': 'delta_w', 'delta_w_in': 'delta_w', 'delta_conv_w': 'delta_w', 'delta_conv_b': 'delta_w', 'delta_conv_ln_g': 'delta_w', 'delta_conv_ln_b': 'delta_w', 'delta_gate_up': 'delta_w', 'delta_gate_bias': 'delta_w', 'delta_gla_norm_g': 'delta_w', 'delta_w_out': 'delta_w', 'delta_ln1_g': 'delta_w', 'delta_ln1_b': 'delta_w', 'delta_w_ff1': 'delta_w', 'delta_w_ff2': 'delta_w', 'delta_ln2_g': 'delta_w', 'delta_ln2_b': 'delta_w', 'new_m_meta_tokens': 'new_m', 'new_m_ln_in_g': 'new_m', 'new_m_ln_in_b': 'new_m', 'new_m_w_in': 'new_m', 'new_m_conv_w': 'new_m', 'new_m_conv_b': 'new_m', 'new_m_conv_ln_g': 'new_m', 'new_m_conv_ln_b': 'new_m', 'new_m_gate_up': 'new_m', 'new_m_gate_bias': 'new_m', 'new_m_gla_norm_g': 'new_m', 'new_m_w_out': 'new_m', 'new_m_ln1_g': 'new_m', 'new_m_ln1_b': 'new_m', 'new_m_w_ff1': 'new_m', 'new_m_w_ff2': 'new_m', 'new_m_ln2_g': 'new_m', 'new_m_ln2_b': 'new_m', 'new_v_meta_tokens': 'new_v', 'new_v_ln_in_g': 'new_v', 'new_v_ln_in_b': 'new_v', 'new_v_w_in': 'new_v', 'new_v_conv_w': 'new_v', 'new_v_conv_b': 'new_v', 'new_v_conv_ln_g': 'new_v', 'new_v_conv_ln_b': 'new_v', 'new_v_gate_up': 'new_v', 'new_v_gate_bias': 'new_v', 'new_v_gla_norm_g': 'new_v', 'new_v_w_out': 'new_v', 'new_v_ln1_g': 'new_v', 'new_v_ln1_b': 'new_v', 'new_v_w_ff1': 'new_v', 'new_v_w_ff2': 'new_v', 'new_v_ln2_g': 'new_v', 'new_v_ln2_b': 'new_v'}


def _forward(args):
    return _fwd_reference(*[args[k] for k in FWD_PARAMS])


def _output_shape():
    out = _jax.eval_shape(lambda: _forward(_fwd_setup_inputs(0)))
    return out.shape, out.dtype

N_MICROBATCH = 1
ADAM_LR = 0.001
ADAM_B1 = 0.9
ADAM_B2 = 0.999
ADAM_EPS = 1e-08
ADAM_WD = 0.01
ADAM_STEP = 10
PER_EXAMPLE_BATCH_AXIS = {'x': 0, 'loss_target': 0}
SHARED_INPUTS = []
_WEIGHT_DTYPES = {'meta_tokens': _jnp.float32, 'ln_in_g': _jnp.float32, 'ln_in_b': _jnp.float32, 'w_in': _jnp.float32, 'conv_w': _jnp.float32, 'conv_b': _jnp.float32, 'conv_ln_g': _jnp.float32, 'conv_ln_b': _jnp.float32, 'gate_up': _jnp.float32, 'gate_bias': _jnp.float32, 'gla_norm_g': _jnp.float32, 'w_out': _jnp.float32, 'ln1_g': _jnp.float32, 'ln1_b': _jnp.float32, 'w_ff1': _jnp.float32, 'w_ff2': _jnp.float32, 'ln2_g': _jnp.float32, 'ln2_b': _jnp.float32}
MOMENT_SCALE = {'meta_tokens': 3.527671e-03, 'ln_in_g': 6.340512e-01, 'ln_in_b': 4.326502e-01, 'w_in': 5.509862e-02, 'conv_w': 5.467505e-02, 'conv_b': 2.150425e-01, 'conv_ln_g': 8.590343e-02, 'conv_ln_b': 1.171183e-01, 'gate_up': 8.214866e-03, 'gate_bias': 3.878043e-02, 'gla_norm_g': 1.105911e-01, 'w_out': 9.683120e-02, 'ln1_g': 6.953166e-01, 'ln1_b': 4.524544e-01, 'w_ff1': 5.363777e-02, 'w_ff2': 2.091304e-01, 'ln2_g': 3.205855e+01, 'ln2_b': 7.172733e+00}


def _to_microbatches(a, axis):
    t = _jnp.moveaxis(a, axis, 0)
    t = t.reshape((N_MICROBATCH, t.shape[0] // N_MICROBATCH) + t.shape[1:])
    return _jnp.moveaxis(t, 1, axis + 1)


def setup_inputs(seed: int = 0) -> dict:
    inp = _fwd_setup_inputs(seed)
    key = _jax.random.fold_in(_jax.random.key(seed), 7919)
    shape, _ = _output_shape()
    out = dict(inp)
    out["loss_target"] = _jax.random.normal(_jax.random.fold_in(key, 0), shape, _jnp.float32)
    for i, name in enumerate(TWIN_WEIGHTS):
        w = inp[name].astype(_jnp.float32)
        if MOMENT_SCALE is None:
            s = _jnp.sqrt(_jnp.mean(_jnp.square(w)) + 1e-30)
        else:
            s = MOMENT_SCALE[name]
        km, kv = _jax.random.split(_jax.random.fold_in(key, i + 1))
        out[name] = w
        out["m_" + name] = s * _jax.random.normal(km, w.shape, _jnp.float32)
        out["v_" + name] = (s * s) * _jax.random.uniform(kv, w.shape, _jnp.float32, 0.5, 1.5)
    if N_MICROBATCH > 1:
        for name, axis in PER_EXAMPLE_BATCH_AXIS.items():
            out[name] = _to_microbatches(out[name], axis)
    return {'x': out['x'], 'meta_tokens': out['meta_tokens'], 'ln_in_g': out['ln_in_g'], 'ln_in_b': out['ln_in_b'], 'w_in': out['w_in'], 'conv_w': out['conv_w'], 'conv_b': out['conv_b'], 'conv_ln_g': out['conv_ln_g'], 'conv_ln_b': out['conv_ln_b'], 'gate_up': out['gate_up'], 'gate_bias': out['gate_bias'], 'gla_norm_g': out['gla_norm_g'], 'w_out': out['w_out'], 'ln1_g': out['ln1_g'], 'ln1_b': out['ln1_b'], 'w_ff1': out['w_ff1'], 'w_ff2': out['w_ff2'], 'ln2_g': out['ln2_g'], 'ln2_b': out['ln2_b'], 'loss_target': out['loss_target'], 'm_meta_tokens': out['m_meta_tokens'], 'm_ln_in_g': out['m_ln_in_g'], 'm_ln_in_b': out['m_ln_in_b'], 'm_w_in': out['m_w_in'], 'm_conv_w': out['m_conv_w'], 'm_conv_b': out['m_conv_b'], 'm_conv_ln_g': out['m_conv_ln_g'], 'm_conv_ln_b': out['m_conv_ln_b'], 'm_gate_up': out['m_gate_up'], 'm_gate_bias': out['m_gate_bias'], 'm_gla_norm_g': out['m_gla_norm_g'], 'm_w_out': out['m_w_out'], 'm_ln1_g': out['m_ln1_g'], 'm_ln1_b': out['m_ln1_b'], 'm_w_ff1': out['m_w_ff1'], 'm_w_ff2': out['m_w_ff2'], 'm_ln2_g': out['m_ln2_g'], 'm_ln2_b': out['m_ln2_b'], 'v_meta_tokens': out['v_meta_tokens'], 'v_ln_in_g': out['v_ln_in_g'], 'v_ln_in_b': out['v_ln_in_b'], 'v_w_in': out['v_w_in'], 'v_conv_w': out['v_conv_w'], 'v_conv_b': out['v_conv_b'], 'v_conv_ln_g': out['v_conv_ln_g'], 'v_conv_ln_b': out['v_conv_ln_b'], 'v_gate_up': out['v_gate_up'], 'v_gate_bias': out['v_gate_bias'], 'v_gla_norm_g': out['v_gla_norm_g'], 'v_w_out': out['v_w_out'], 'v_ln1_g': out['v_ln1_g'], 'v_ln1_b': out['v_ln1_b'], 'v_w_ff1': out['v_w_ff1'], 'v_w_ff2': out['v_w_ff2'], 'v_ln2_g': out['v_ln2_g'], 'v_ln2_b': out['v_ln2_b']}


def _loss(weights, diff, rest, loss_target):
    with _jax.named_scope("forward"):
        args = {**rest, TWIN_DIFF_INPUT: diff, **{k: w.astype(_WEIGHT_DTYPES[k]) for k, w in weights.items()}}
        y = _forward(args)
    with _jax.named_scope("loss_head"):
        err = _jnp.square(y.astype(_jnp.float32) - loss_target)
        return 0.5 * _jnp.sum(_jnp.mean(err, axis=-1)) if err.ndim else 0.5 * err


def _adamw(w, g, m, v):
    m = ADAM_B1 * m + (1.0 - ADAM_B1) * g
    v = ADAM_B2 * v + (1.0 - ADAM_B2) * _jnp.square(g)
    m_hat = m / (1.0 - ADAM_B1 ** ADAM_STEP)
    v_hat = v / (1.0 - ADAM_B2 ** ADAM_STEP)
    delta = -ADAM_LR * (m_hat / (_jnp.sqrt(v_hat) + ADAM_EPS) + ADAM_WD * w)
    return delta, m, v


def reference(x, meta_tokens, ln_in_g, ln_in_b, w_in, conv_w, conv_b, conv_ln_g, conv_ln_b, gate_up, gate_bias, gla_norm_g, w_out, ln1_g, ln1_b, w_ff1, w_ff2, ln2_g, ln2_b, loss_target, m_meta_tokens, m_ln_in_g, m_ln_in_b, m_w_in, m_conv_w, m_conv_b, m_conv_ln_g, m_conv_ln_b, m_gate_up, m_gate_bias, m_gla_norm_g, m_w_out, m_ln1_g, m_ln1_b, m_w_ff1, m_w_ff2, m_ln2_g, m_ln2_b, v_meta_tokens, v_ln_in_g, v_ln_in_b, v_w_in, v_conv_w, v_conv_b, v_conv_ln_g, v_conv_ln_b, v_gate_up, v_gate_bias, v_gla_norm_g, v_w_out, v_ln1_g, v_ln1_b, v_w_ff1, v_w_ff2, v_ln2_g, v_ln2_b):
    given = dict(x=x, meta_tokens=meta_tokens, ln_in_g=ln_in_g, ln_in_b=ln_in_b, w_in=w_in, conv_w=conv_w, conv_b=conv_b, conv_ln_g=conv_ln_g, conv_ln_b=conv_ln_b, gate_up=gate_up, gate_bias=gate_bias, gla_norm_g=gla_norm_g, w_out=w_out, ln1_g=ln1_g, ln1_b=ln1_b, w_ff1=w_ff1, w_ff2=w_ff2, ln2_g=ln2_g, ln2_b=ln2_b, loss_target=loss_target, m_meta_tokens=m_meta_tokens, m_ln_in_g=m_ln_in_g, m_ln_in_b=m_ln_in_b, m_w_in=m_w_in, m_conv_w=m_conv_w, m_conv_b=m_conv_b, m_conv_ln_g=m_conv_ln_g, m_conv_ln_b=m_conv_ln_b, m_gate_up=m_gate_up, m_gate_bias=m_gate_bias, m_gla_norm_g=m_gla_norm_g, m_w_out=m_w_out, m_ln1_g=m_ln1_g, m_ln1_b=m_ln1_b, m_w_ff1=m_w_ff1, m_w_ff2=m_w_ff2, m_ln2_g=m_ln2_g, m_ln2_b=m_ln2_b, v_meta_tokens=v_meta_tokens, v_ln_in_g=v_ln_in_g, v_ln_in_b=v_ln_in_b, v_w_in=v_w_in, v_conv_w=v_conv_w, v_conv_b=v_conv_b, v_conv_ln_g=v_conv_ln_g, v_conv_ln_b=v_conv_ln_b, v_gate_up=v_gate_up, v_gate_bias=v_gate_bias, v_gla_norm_g=v_gla_norm_g, v_w_out=v_w_out, v_ln1_g=v_ln1_g, v_ln1_b=v_ln1_b, v_w_ff1=v_w_ff1, v_w_ff2=v_w_ff2, v_ln2_g=v_ln2_g, v_ln2_b=v_ln2_b)
    weights = {n: given[n] for n in TWIN_WEIGHTS}
    shared = {n: given[n] for n in SHARED_INPUTS}
    per_example = {n: given[n] for n in ['x']}
    grad_fn = _jax.value_and_grad(_loss, argnums=(0, 1))

    def one_microbatch(ex, loss_target):
        ex = dict(ex)
        diff = ex.pop(TWIN_DIFF_INPUT)
        return grad_fn(weights, diff, {**shared, **ex}, loss_target)

    if N_MICROBATCH == 1:
        loss, (grad_w, grad_x) = one_microbatch(per_example, given["loss_target"])
    else:
        def body(carry, xs):
            loss_sum, grad_sum = carry
            l_k, (gw_k, gx_k) = one_microbatch(xs[0], xs[1])
            with _jax.named_scope("update"):
                return (loss_sum + l_k, _jax.tree.map(_jnp.add, grad_sum, gw_k)), gx_k

        init = (_jnp.zeros((), _jnp.float32), _jax.tree.map(_jnp.zeros_like, weights))
        (loss, grad_w), grad_x = _jax.lax.scan(body, init, (per_example, given["loss_target"]))
    with _jax.named_scope("update"):
        delta_w, new_m, new_v = {}, {}, {}
        for n in TWIN_WEIGHTS:
            delta_w[n], new_m[n], new_v[n] = _adamw(weights[n], grad_w[n], given["m_" + n], given["v_" + n])
    return (loss, grad_x, *[grad_w[n] for n in TWIN_WEIGHTS], *[delta_w[n] for n in TWIN_WEIGHTS],
            *[new_m[n] for n in TWIN_WEIGHTS], *[new_v[n] for n in TWIN_WEIGHTS])
```

```python
import functools

import jax
import jax.numpy as jnp
from jax import lax
from jax.experimental import pallas as pl
from jax.experimental.pallas import tpu as pltpu

F32 = jnp.float32
BF16 = jnp.bfloat16

N_META = 16
CHUNK = 64
PAD_FRONT = (-N_META) % CHUNK
X_OFF = PAD_FRONT + N_META
CONV_WIDTH = 31
CONV_HALO = 32
CONV_SUB = 64
CONV_WIN = CONV_SUB + CONV_HALO
GLA_HEADS = 4
GLA_DK = 64
GLA_DV = 128
GLA_RANK = 16
GLA_TAU = 16.0
QK_SCALE = GLA_DK ** -0.5
LN_EPS = 1e-5
ALPHA = 2.0 ** 0.25
LANES = 128
N_DEV = 8
ADAM_LR = 0.001
ADAM_B1 = 0.9
ADAM_B2 = 0.999
ADAM_EPS = 1e-08
ADAM_WD = 0.01
ADAM_STEP = 10
VMEM_LIMIT = 56 * 1024 * 1024
MESH_AXES = ("x", "y", "c")


def _sds(shape, dtype):
    return jax.ShapeDtypeStruct(shape, dtype)


def _mm(a, b):
    return jnp.dot(a, b, preferred_element_type=F32)


def _mm_nt(a, b):
    return lax.dot_general(a, b, (((1,), (1,)), ((), ())), preferred_element_type=F32)


def _mm_tn(a, b):
    return lax.dot_general(a, b, (((0,), (0,)), ((), ())), preferred_element_type=F32)


def _sigmoid(x):
    return 1.0 / (1.0 + jnp.exp(-x))


def _log_sigmoid(z):
    return jnp.minimum(z, 0.0) - jnp.log(1.0 + jnp.exp(-jnp.abs(z)))


def _ln(x):
    mu = jnp.mean(x, axis=-1, keepdims=True)
    xc = x - mu
    var = jnp.mean(xc * xc, axis=-1, keepdims=True)
    rstd = lax.rsqrt(var + LN_EPS)
    return xc * rstd, rstd


def _ln_bwd(dyg, xhat, rstd):
    m1 = jnp.mean(dyg, axis=-1, keepdims=True)
    m2 = jnp.mean(dyg * xhat, axis=-1, keepdims=True)
    return rstd * (dyg - m1 - xhat * m2)


def _rowsum(x):
    return jnp.sum(x, axis=0, keepdims=True)


def _row_in_seq(i, tm, tp):
    base = lax.rem(i * tm, tp)
    return base + lax.broadcasted_iota(jnp.int32, (tm, 1), 0)


def _split3(x):
    hi = x.astype(BF16)
    r1 = x - hi.astype(F32)
    mid = r1.astype(BF16)
    lo = (r1 - mid.astype(F32)).astype(BF16)
    return hi, mid, lo


def _tri_mm(tri, x):
    hi, mid, lo = _split3(x)
    return _mm(tri, hi) + _mm(tri, mid) + _mm(tri, lo)


def _params(sem):
    return pltpu.CompilerParams(dimension_semantics=sem, vmem_limit_bytes=VMEM_LIMIT)


def _pick_tile(n, prefs):
    for t in prefs:
        if n % t == 0:
            return t
    raise ValueError(f"no tile for {n}")


def _inproj_fwd(xcat, g, b, w_in, tp, tm):
    r, d = xcat.shape
    n = w_in.shape[1]

    def body(x_ref, g_ref, b_ref, w_ref, s0_ref, u_ref):
        i = pl.program_id(0)
        xhat, _ = _ln(x_ref[...])
        real = _row_in_seq(i, tm, tp) >= PAD_FRONT
        s = jnp.where(real, xhat * g_ref[...] + b_ref[...], 0.0)
        s0_ref[...] = s
        u_ref[...] = _mm(s.astype(BF16), w_ref[...])

    return pl.pallas_call(
        body, name="inproj_fwd", grid=(r // tm,),
        in_specs=[pl.BlockSpec((tm, d), lambda i: (i, 0)), pl.BlockSpec((1, d), lambda i: (0, 0)),
                  pl.BlockSpec((1, d), lambda i: (0, 0)), pl.BlockSpec((d, n), lambda i: (0, 0))],
        out_specs=[pl.BlockSpec((tm, d), lambda i: (i, 0)), pl.BlockSpec((tm, n), lambda i: (i, 0))],
        out_shape=[_sds((r, d), F32), _sds((r, n), F32)],
        compiler_params=_params(("parallel",)),
    )(xcat, g, b, w_in)


def _conv_taps(win, coef, lo):
    acc = None
    for rho in range(8):
        offs = [o for o in range(lo, lo + CONV_WIDTH) if o % 8 == rho]
        if not offs:
            continue
        rolled = win if rho == 0 else pltpu.roll(win, CONV_WIN - rho, 0)
        for o in offs:
            m8 = o - rho
            term = rolled[m8:m8 + CONV_SUB, :] * coef(o)
            acc = term if acc is None else acc + term
    return acc


def _conv_fwd(u, w32, cb, cg, cbe, tp, tc, dc):
    r = u.shape[0]
    hb = tc // CONV_HALO

    def body(a_ref, g_ref, ah_ref, gh_ref, w_ref, cb_ref, cg_ref, cbe_ref, c_ref, co_ref, hs_ref):
        t = pl.program_id(0)
        first = lax.rem(t * tc, tp) == 0
        hh = ah_ref[...] * _sigmoid(gh_ref[...])
        hs_ref[0:CONV_HALO, :] = jnp.where(first, 0.0, hh)
        hs_ref[CONV_HALO:CONV_HALO + tc, :] = a_ref[...] * _sigmoid(g_ref[...])

        def sub(k, carry):
            r0 = pl.multiple_of(k * CONV_SUB, CONV_SUB)
            win = hs_ref[pl.ds(r0, CONV_WIN), :]
            c = _conv_taps(win, lambda o: w_ref[o - 2:o - 1, :], 2) + cb_ref[...]
            c_ref[pl.ds(r0, CONV_SUB), :] = c
            xhat, _ = _ln(c)
            cn = xhat * cg_ref[...] + cbe_ref[...]
            co_ref[pl.ds(r0, CONV_SUB), :] = (cn * _sigmoid(cn)).astype(BF16)
            return carry

        lax.fori_loop(0, tc // CONV_SUB, sub, 0)

    vec = pl.BlockSpec((1, dc), lambda t: (0, 0))
    return pl.pallas_call(
        body, name="conv_fwd", grid=(r // tc,),
        in_specs=[pl.BlockSpec((tc, dc), lambda t: (t, 0)), pl.BlockSpec((tc, dc), lambda t: (t, 1)),
                  pl.BlockSpec((CONV_HALO, dc), lambda t: (jnp.maximum(t * hb - 1, 0), 0)),
                  pl.BlockSpec((CONV_HALO, dc), lambda t: (jnp.maximum(t * hb - 1, 0), 1)),
                  pl.BlockSpec((32, dc), lambda t: (0, 0)), vec, vec, vec],
        out_specs=[pl.BlockSpec((tc, dc), lambda t: (t, 0)), pl.BlockSpec((tc, dc), lambda t: (t, 0))],
        out_shape=[_sds((r, dc), F32), _sds((r, dc), BF16)],
        scratch_shapes=[pltpu.VMEM((CONV_HALO + tc, dc), F32)],
        compiler_params=_params(("parallel",)),
    )(u, u, u, u, w32, cb, cg, cbe)


def _gla_prep(qk, gd, gup, gb, n):
    z = _mm(gd.astype(BF16), gup) + gb
    lg = _log_sigmoid(z) * (1.0 / GLA_TAU)
    row = n * CHUNK + lax.broadcasted_iota(jnp.int32, (CHUNK, 1), 0)
    real = row >= PAD_FRONT
    lg = jnp.where(real, lg, 0.0)
    ri = lax.broadcasted_iota(jnp.int32, (CHUNK, CHUNK), 0)
    ci = lax.broadcasted_iota(jnp.int32, (CHUNK, CHUNK), 1)
    low = (ri >= ci).astype(BF16)
    b = _tri_mm(low, lg)
    bl = _rowsum(lg)
    hk = GLA_HEADS * GLA_DK
    q = qk[:, :hk] * QK_SCALE
    k = qk[:, hk:]
    eb = jnp.exp(b)
    enb = jnp.exp(-b)
    ebl = jnp.exp(bl - b)
    gam = jnp.exp(bl)
    return dict(z=z, real=real, ri=ri, ci=ci, eb=eb, enb=enb, ebl=ebl, gam=gam, k=k,
                qe=q * eb, ke=k * enb, kd=k * ebl)


def _head_mask(h2):
    lane = lax.broadcasted_iota(jnp.int32, (1, LANES), 1)
    return (lane < GLA_DK) if h2 == 0 else (lane >= GLA_DK)


def _gla_fwd(u, gup, gb, gn, bsz, nc):
    r = u.shape[0]
    hv = GLA_HEADS * GLA_DV

    def body(qk_ref, v_ref, r_ref, gd_ref, gup_ref, gb_ref, gn_ref, go_ref, sta_ref, st_ref):
        n = pl.program_id(1)

        @pl.when(n == 0)
        def _():
            st_ref[...] = jnp.zeros_like(st_ref)

        p = _gla_prep(qk_ref[...], gd_ref[...], gup_ref[...], gb_ref[...], n)
        tril = p["ri"] >= p["ci"]
        for h in range(GLA_HEADS):
            hp, h2 = divmod(h, 2)
            ls = slice(hp * LANES, (hp + 1) * LANES)
            m = _head_mask(h2)
            qeh = jnp.where(m, p["qe"][:, ls], 0.0).astype(BF16)
            kdh = jnp.where(m, p["kd"][:, ls], 0.0).astype(BF16)
            keh = p["ke"][:, ls].astype(BF16)
            vh = v_ref[:, h * GLA_DV:(h + 1) * GLA_DV].astype(BF16)
            st = st_ref[h]
            sta_ref[h] = st
            a = jnp.where(tril, _mm_nt(qeh, keh), 0.0)
            o = _mm(a.astype(BF16), vh) + _mm_nt(qeh, st.astype(BF16))
            st_ref[h] = st * p["gam"][:, ls] + _mm_tn(vh, kdh)
            rs = lax.rsqrt(jnp.mean(o * o, axis=-1, keepdims=True) + LN_EPS)
            rr = r_ref[:, h * GLA_DV:(h + 1) * GLA_DV]
            go = o * rs * gn_ref[...] * (rr * _sigmoid(rr))
            go_ref[:, h * GLA_DV:(h + 1) * GLA_DV] = go.astype(BF16)

    rowblk = lambda col: (lambda b, n: (b * nc + n, col))
    const = lambda b, n: (0, 0)
    return pl.pallas_call(
        body, name="gla_fwd", grid=(bsz, nc),
        in_specs=[pl.BlockSpec((CHUNK, 512), rowblk(2)), pl.BlockSpec((CHUNK, hv), rowblk(3)),
                  pl.BlockSpec((CHUNK, hv), rowblk(4)), pl.BlockSpec((CHUNK, LANES), rowblk(20)),
                  pl.BlockSpec((LANES, 256), const), pl.BlockSpec((1, 256), const), pl.BlockSpec((1, GLA_DV), const)],
        out_specs=[pl.BlockSpec((CHUNK, hv), rowblk(0)),
                   pl.BlockSpec((None, GLA_HEADS, LANES, LANES), lambda b, n: (b * nc + n, 0, 0, 0))],
        out_shape=[_sds((r, hv), BF16), _sds((bsz * nc, GLA_HEADS, LANES, LANES), F32)],
        scratch_shapes=[pltpu.VMEM((GLA_HEADS, LANES, LANES), F32)],
        compiler_params=_params(("parallel", "arbitrary")),
    )(u, u, u, u, gup, gb, gn)


def _outproj_fwd(s0, co, go, w_out, g1, b1, tm):
    r, d = s0.shape
    dc = co.shape[1]

    def body(s0_ref, co_ref, go_ref, w_ref, g_ref, b_ref, p1_ref, s1_ref):
        mix = _mm(co_ref[...], w_ref[0:dc, :]) + _mm(go_ref[...], w_ref[dc:2 * dc, :])
        p1 = ALPHA * s0_ref[...] + mix
        p1_ref[...] = p1
        xhat, _ = _ln(p1)
        s1_ref[...] = xhat * g_ref[...] + b_ref[...]

    row = lambda w: pl.BlockSpec((tm, w), lambda i: (i, 0))
    vec = pl.BlockSpec((1, d), lambda i: (0, 0))
    return pl.pallas_call(
        body, name="outproj_fwd", grid=(r // tm,),
        in_specs=[row(d), row(dc), row(dc), pl.BlockSpec((2 * dc, d), lambda i: (0, 0)), vec, vec],
        out_specs=[row(d), row(d)],
        out_shape=[_sds((r, d), F32), _sds((r, d), F32)],
        compiler_params=_params(("parallel",)),
    )(s0, co, go, w_out, g1, b1)


def _mlp_fwd(s1, w1g, w2, g2, b2, tgt, tp, tm):
    r, d = s1.shape
    nh, _, th = w1g.shape

    def body(s1_ref, w1_ref, w2_ref, g_ref, b_ref, t_ref, hm_ref, dp2_ref, loss_ref, dg_ref, db_ref, acc_ref, sb_ref):
        i = pl.program_id(0)
        j = pl.program_id(1)

        @pl.when(jnp.logical_and(i == 0, j == 0))
        def _():
            loss_ref[...] = jnp.zeros_like(loss_ref)
            dg_ref[...] = jnp.zeros_like(dg_ref)
            db_ref[...] = jnp.zeros_like(db_ref)

        @pl.when(j == 0)
        def _():
            acc_ref[...] = jnp.zeros_like(acc_ref)
            sb_ref[...] = s1_ref[...].astype(BF16)

        h = _mm(sb_ref[...], w1_ref[...])
        hm_ref[...] = h.astype(BF16)
        act = jnp.square(jnp.maximum(h, 0.0))
        acc_ref[...] += _mm(act.astype(BF16), w2_ref[...])

        @pl.when(j == nh - 1)
        def _():
            p2 = ALPHA * s1_ref[...] + acc_ref[...]
            xhat, rstd = _ln(p2)
            s2 = xhat * g_ref[...] + b_ref[...]
            isx = _row_in_seq(i, tm, tp) >= X_OFF
            err = jnp.where(isx, s2 - t_ref[...], 0.0)
            loss_ref[...] += 0.5 * jnp.sum(jnp.mean(err * err, axis=-1, keepdims=True))
            dy = err * (1.0 / d)
            dg_ref[...] += _rowsum(dy * xhat)
            db_ref[...] += _rowsum(dy)
            dp2_ref[...] = _ln_bwd(dy * g_ref[...], xhat, rstd)

    row = pl.BlockSpec((tm, d), lambda i, j: (i, 0))
    vec = pl.BlockSpec((1, d), lambda i, j: (0, 0))
    return pl.pallas_call(
        body, name="mlp_fwd", grid=(r // tm, nh),
        in_specs=[row, pl.BlockSpec((None, d, th), lambda i, j: (j, 0, 0)), pl.BlockSpec((th, d), lambda i, j: (j, 0)),
                  vec, vec, row],
        out_specs=[pl.BlockSpec((tm, th), lambda i, j: (i, j)), row,
                   pl.BlockSpec((8, LANES), lambda i, j: (0, 0)), vec, vec],
        out_shape=[_sds((r, nh * th), BF16), _sds((r, d), F32), _sds((8, LANES), F32), _sds((1, d), F32), _sds((1, d), F32)],
        scratch_shapes=[pltpu.VMEM((tm, d), F32), pltpu.VMEM((tm, d), BF16)],
        compiler_params=_params(("arbitrary", "arbitrary")),
    )(s1, w1g, w2, g2, b2, tgt)


def _mlp_bwd_act(dp2, hm, w1g, w2, p1, g1, tm):
    r, d = dp2.shape
    nh, _, th = w1g.shape

    def body(dp2_ref, hm_ref, w1_ref, w2_ref, p1_ref, g_ref, dh_ref, dp1_ref, dg_ref, db_ref, acc_ref, db16_ref):
        i = pl.program_id(0)
        j = pl.program_id(1)

        @pl.when(jnp.logical_and(i == 0, j == 0))
        def _():
            dg_ref[...] = jnp.zeros_like(dg_ref)
            db_ref[...] = jnp.zeros_like(db_ref)

        @pl.when(j == 0)
        def _():
            acc_ref[...] = jnp.zeros_like(acc_ref)
            db16_ref[...] = dp2_ref[...].astype(BF16)

        dact = _mm_nt(db16_ref[...], w2_ref[...])
        dh = (dact * (2.0 * jnp.maximum(hm_ref[...].astype(F32), 0.0))).astype(BF16)
        dh_ref[...] = dh
        acc_ref[...] += _mm_nt(dh, w1_ref[...])

        @pl.when(j == nh - 1)
        def _():
            ds1 = ALPHA * dp2_ref[...] + acc_ref[...]
            xhat, rstd = _ln(p1_ref[...])
            dg_ref[...] += _rowsum(ds1 * xhat)
            db_ref[...] += _rowsum(ds1)
            dp1_ref[...] = _ln_bwd(ds1 * g_ref[...], xhat, rstd)

    row = pl.BlockSpec((tm, d), lambda i, j: (i, 0))
    vec = pl.BlockSpec((1, d), lambda i, j: (0, 0))
    blk = pl.BlockSpec((tm, th), lambda i, j: (i, j))
    return pl.pallas_call(
        body, name="mlp_bwd_act", grid=(r // tm, nh),
        in_specs=[row, blk, pl.BlockSpec((None, d, th), lambda i, j: (j, 0, 0)), pl.BlockSpec((th, d), lambda i, j: (j, 0)),
                  row, vec],
        out_specs=[blk, row, vec, vec],
        out_shape=[_sds((r, nh * th), BF16), _sds((r, d), F32), _sds((1, d), F32), _sds((1, d), F32)],
        scratch_shapes=[pltpu.VMEM((tm, d), F32), pltpu.VMEM((tm, d), BF16)],
        compiler_params=_params(("arbitrary", "arbitrary")),
    )(dp2, hm, w1g, w2, p1, g1)


def _mlp_bwd_w(s1, hm, dh, dp2, nh, tm):
    r, d = s1.shape
    th = hm.shape[1] // nh

    def body(s1_ref, hm_ref, dh_ref, dp2_ref, dw1_ref, dw2_ref, a1_ref, a2_ref):
        i = pl.program_id(1)

        @pl.when(i == 0)
        def _():
            a1_ref[...] = jnp.zeros_like(a1_ref)
            a2_ref[...] = jnp.zeros_like(a2_ref)

        act = jnp.square(jnp.maximum(hm_ref[...].astype(F32), 0.0)).astype(BF16)
        a1_ref[...] += _mm_tn(s1_ref[...].astype(BF16), dh_ref[...])
        a2_ref[...] += _mm_tn(act, dp2_ref[...].astype(BF16))

        @pl.when(i == pl.num_programs(1) - 1)
        def _():
            dw1_ref[...] = a1_ref[...].astype(BF16)
            dw2_ref[...] = a2_ref[...].astype(BF16)

    row = pl.BlockSpec((tm, d), lambda j, i: (i, 0))
    blk = pl.BlockSpec((tm, th), lambda j, i: (i, j))
    return pl.pallas_call(
        body, name="mlp_bwd_w", grid=(nh, r // tm),
        in_specs=[row, blk, blk, row],
        out_specs=[pl.BlockSpec((None, d, th), lambda j, i: (j, 0, 0)), pl.BlockSpec((None, th, d), lambda j, i: (j, 0, 0))],
        out_shape=[_sds((nh, d, th), BF16), _sds((nh, th, d), BF16)],
        scratch_shapes=[pltpu.VMEM((d, th), F32), pltpu.VMEM((th, d), F32)],
        compiler_params=_params(("parallel", "arbitrary")),
    )(s1, hm, dh, dp2)


def _outproj_bwd(dp1, co, go, w_out, tm):
    r, d = dp1.shape
    dc = co.shape[1]

    def body(dp_ref, co_ref, go_ref, w_ref, dmi_ref, dw_ref, acc_ref):
        i = pl.program_id(0)

        @pl.when(i == 0)
        def _():
            acc_ref[...] = jnp.zeros_like(acc_ref)

        dpb = dp_ref[...].astype(BF16)
        dmi_ref[...] = _mm_nt(dpb, w_ref[...])
        acc_ref[0:dc, :] += _mm_tn(co_ref[...], dpb)
        acc_ref[dc:2 * dc, :] += _mm_tn(go_ref[...], dpb)

        @pl.when(i == pl.num_programs(0) - 1)
        def _():
            dw_ref[...] = acc_ref[...].astype(BF16)

    row = lambda w: pl.BlockSpec((tm, w), lambda i: (i, 0))
    full = pl.BlockSpec((2 * dc, d), lambda i: (0, 0))
    return pl.pallas_call(
        body, name="outproj_bwd", grid=(r // tm,),
        in_specs=[row(d), row(dc), row(dc), full],
        out_specs=[row(2 * dc), full],
        out_shape=[_sds((r, 2 * dc), F32), _sds((2 * dc, d), BF16)],
        scratch_shapes=[pltpu.VMEM((2 * dc, d), F32)],
        compiler_params=_params(("arbitrary",)),
    )(dp1, co, go, w_out)


def _gla_bwd(u, dmi, sta, gup, gb, gn, bsz, nc):
    r = u.shape[0]
    hv = GLA_HEADS * GLA_DV
    hk = GLA_HEADS * GLA_DK

    def body(qk_ref, v_ref, r_ref, gd_ref, dgo_ref, sta_ref, gup_ref, gb_ref, gn_ref,
             dqk_ref, dv_ref, dr_ref, dgd_ref, dgn_ref, dgb_ref, dgup_ref, dst_ref):
        bi = pl.program_id(0)
        t = pl.program_id(1)
        n = nc - 1 - t

        @pl.when(jnp.logical_and(bi == 0, t == 0))
        def _():
            dgn_ref[...] = jnp.zeros_like(dgn_ref)
            dgb_ref[...] = jnp.zeros_like(dgb_ref)
            dgup_ref[...] = jnp.zeros_like(dgup_ref)

        @pl.when(t == 0)
        def _():
            dst_ref[...] = jnp.zeros_like(dst_ref)

        gd = gd_ref[...]
        p = _gla_prep(qk_ref[...], gd, gup_ref[...], gb_ref[...], n)
        tril = p["ri"] >= p["ci"]
        dgn = jnp.zeros((1, GLA_DV), F32)
        dqe_t, dke_t, dkd_t, dgam_t = [], [], [], []
        for hp in range(GLA_HEADS // 2):
            ls = slice(hp * LANES, (hp + 1) * LANES)
            dqe = jnp.zeros((CHUNK, LANES), F32)
            dke = jnp.zeros((CHUNK, LANES), F32)
            dkd = jnp.zeros((CHUNK, LANES), F32)
            dgam = jnp.zeros((1, LANES), F32)
            for h2 in range(2):
                h = 2 * hp + h2
                vs = slice(h * GLA_DV, (h + 1) * GLA_DV)
                m = _head_mask(h2)
                qeh = jnp.where(m, p["qe"][:, ls], 0.0).astype(BF16)
                kdh = jnp.where(m, p["kd"][:, ls], 0.0).astype(BF16)
                keh = p["ke"][:, ls].astype(BF16)
                vh = v_ref[:, vs].astype(BF16)
                st = sta_ref[h]
                stb = st.astype(BF16)
                a = jnp.where(tril, _mm_nt(qeh, keh), 0.0)
                ab = a.astype(BF16)
                o = _mm(ab, vh) + _mm_nt(qeh, stb)
                rr = r_ref[:, vs]
                sr = _sigmoid(rr)
                rs = lax.rsqrt(jnp.mean(o * o, axis=-1, keepdims=True) + LN_EPS)
                y = o * rs
                dgo = dgo_ref[:, vs]
                don = dgo * (rr * sr)
                dr_ref[:, vs] = dgo * (y * gn_ref[...]) * (sr * (1.0 + rr * (1.0 - sr)))
                dgn = dgn + _rowsum(don * y)
                dxn = don * gn_ref[...]
                do = rs * (dxn - y * jnp.mean(dxn * y, axis=-1, keepdims=True))
                dob = do.astype(BF16)
                dst = dst_ref[h]
                dstb = dst.astype(BF16)
                da = jnp.where(tril, _mm_nt(dob, vh), 0.0).astype(BF16)
                dv_ref[:, vs] = _mm_tn(ab, dob) + _mm_nt(kdh, dstb)
                dqe = dqe + jnp.where(m, _mm(da, keh) + _mm(dob, stb), 0.0)
                dke = dke + _mm_tn(da, qeh)
                dkd = dkd + jnp.where(m, _mm(vh, dstb), 0.0)
                dgam = dgam + _rowsum(dst * st)
                dst_ref[h] = dst * p["gam"][:, ls] + _mm_tn(dob, qeh)
            dqe_t.append(dqe)
            dke_t.append(dke)
            dkd_t.append(dkd)
            dgam_t.append(dgam)
        dqe = jnp.concatenate(dqe_t, axis=1)
        dke = jnp.concatenate(dke_t, axis=1)
        dkd = jnp.concatenate(dkd_t, axis=1)
        dgam = jnp.concatenate(dgam_t, axis=1)
        dqk_ref[:, :hk] = dqe * p["eb"] * QK_SCALE
        dqk_ref[:, hk:] = dke * p["enb"] + dkd * p["ebl"]
        dkdkd = dkd * p["kd"]
        db = dqe * p["qe"] - dke * p["ke"] - dkdkd
        dbl = _rowsum(dkdkd) + dgam * p["gam"]
        upper = (p["ri"] <= p["ci"]).astype(BF16)
        dlg = _tri_mm(upper, db) + dbl
        dz = jnp.where(p["real"], dlg * (1.0 / GLA_TAU) * _sigmoid(-p["z"]), 0.0)
        dzb = dz.astype(BF16)
        dgb_ref[...] += _rowsum(dz)
        dgup_ref[...] += _mm_tn(gd.astype(BF16), dzb)
        dgd_ref[...] = _mm_nt(dzb, gup_ref[...])
        dgn_ref[...] += dgn

    rowblk = lambda col: (lambda b, t: (b * nc + nc - 1 - t, col))
    const = lambda b, t: (0, 0)
    return pl.pallas_call(
        body, name="gla_bwd", grid=(bsz, nc),
        in_specs=[pl.BlockSpec((CHUNK, 2 * hk), rowblk(2)), pl.BlockSpec((CHUNK, hv), rowblk(3)),
                  pl.BlockSpec((CHUNK, hv), rowblk(4)), pl.BlockSpec((CHUNK, LANES), rowblk(20)),
                  pl.BlockSpec((CHUNK, hv), rowblk(1)),
                  pl.BlockSpec((None, GLA_HEADS, LANES, LANES), lambda b, t: (b * nc + nc - 1 - t, 0, 0, 0)),
                  pl.BlockSpec((LANES, 256), const), pl.BlockSpec((1, 256), const), pl.BlockSpec((1, GLA_DV), const)],
        out_specs=[pl.BlockSpec((CHUNK, 2 * hk), rowblk(0)), pl.BlockSpec((CHUNK, hv), rowblk(0)),
                   pl.BlockSpec((CHUNK, hv), rowblk(0)), pl.BlockSpec((CHUNK, LANES), rowblk(0)),
                   pl.BlockSpec((1, GLA_DV), const), pl.BlockSpec((1, 256), const), pl.BlockSpec((LANES, 256), const)],
        out_shape=[_sds((r, 2 * hk), F32), _sds((r, hv), F32), _sds((r, hv), F32), _sds((r, LANES), F32),
                   _sds((1, GLA_DV), F32), _sds((1, 256), F32), _sds((LANES, 256), F32)],
        scratch_shapes=[pltpu.VMEM((GLA_HEADS, LANES, LANES), F32)],
        compiler_params=_params(("arbitrary", "arbitrary")),
    )(u, u, u, u, dmi, sta, gup, gb, gn)


def _conv_bwd(u, c, dmi, w32, cg, cbe, tp, tc, dc):
    r = u.shape[0]
    hb = tc // CONV_HALO
    nhalo = r // CONV_HALO

    def dconv(cv, dco, cg_ref, cbe_ref):
        xhat, rstd = _ln(cv)
        cn = xhat * cg_ref[...] + cbe_ref[...]
        sg = _sigmoid(cn)
        dcn = dco * (sg * (1.0 + cn * (1.0 - sg)))
        return _ln_bwd(dcn * cg_ref[...], xhat, rstd), dcn, xhat

    def body(a_ref, g_ref, ah_ref, gh_ref, c_ref, dco_ref, ch_ref, dcoh_ref, w_ref, cg_ref, cbe_ref,
             du_ref, dw_ref, dcb_ref, dcg_ref, dcbe_ref, hs_ref, dcs_ref):
        t = pl.program_id(0)

        @pl.when(t == 0)
        def _():
            dw_ref[...] = jnp.zeros_like(dw_ref)
            dcb_ref[...] = jnp.zeros_like(dcb_ref)
            dcg_ref[...] = jnp.zeros_like(dcg_ref)
            dcbe_ref[...] = jnp.zeros_like(dcbe_ref)

        first = lax.rem(t * tc, tp) == 0
        last = lax.rem((t + 1) * tc, tp) == 0
        hh = ah_ref[...] * _sigmoid(gh_ref[...])
        hs_ref[0:CONV_HALO, :] = jnp.where(first, 0.0, hh)
        hs_ref[CONV_HALO:CONV_HALO + tc, :] = a_ref[...] * _sigmoid(g_ref[...])
        dch, _, _ = dconv(ch_ref[...], dcoh_ref[...], cg_ref, cbe_ref)
        dcs_ref[tc:tc + CONV_HALO, :] = jnp.where(last, 0.0, dch)

        def sub1(k, carry):
            r0 = pl.multiple_of(k * CONV_SUB, CONV_SUB)
            dcv, dcn, xhat = dconv(c_ref[pl.ds(r0, CONV_SUB), :], dco_ref[pl.ds(r0, CONV_SUB), :], cg_ref, cbe_ref)
            dcs_ref[pl.ds(r0, CONV_SUB), :] = dcv
            dcb_ref[...] += _rowsum(dcv)
            dcg_ref[...] += _rowsum(dcn * xhat)
            dcbe_ref[...] += _rowsum(dcn)
            return carry

        lax.fori_loop(0, tc // CONV_SUB, sub1, 0)

        def sub2(k, carry):
            r0 = pl.multiple_of(k * CONV_SUB, CONV_SUB)
            dwin = dcs_ref[pl.ds(r0, CONV_WIN), :]
            dh = _conv_taps(dwin, lambda o: w_ref[CONV_WIDTH - 1 - o:CONV_WIDTH - o, :], 0)
            av = a_ref[pl.ds(r0, CONV_SUB), :]
            sg = _sigmoid(g_ref[pl.ds(r0, CONV_SUB), :])
            du_ref[pl.ds(r0, CONV_SUB), 0:dc] = dh * sg
            du_ref[pl.ds(r0, CONV_SUB), dc:2 * dc] = dh * av * sg * (1.0 - sg)
            hwin = hs_ref[pl.ds(r0, CONV_WIN), :]
            dcv = dwin[0:CONV_SUB, :]
            for rho in range(8):
                offs = [o for o in range(2, 2 + CONV_WIDTH) if o % 8 == rho]
                rolled = hwin if rho == 0 else pltpu.roll(hwin, CONV_WIN - rho, 0)
                for o in offs:
                    m8 = o - rho
                    dw_ref[o - 2:o - 1, :] += _rowsum(dcv * rolled[m8:m8 + CONV_SUB, :])
            return carry

        lax.fori_loop(0, tc // CONV_SUB, sub2, 0)

    vec = pl.BlockSpec((1, dc), lambda t: (0, 0))
    prev = lambda col: (lambda t: (jnp.maximum(t * hb - 1, 0), col))
    nxt = lambda col: (lambda t: (jnp.minimum((t + 1) * hb, nhalo - 1), col))
    return pl.pallas_call(
        body, name="conv_bwd", grid=(r // tc,),
        in_specs=[pl.BlockSpec((tc, dc), lambda t: (t, 0)), pl.BlockSpec((tc, dc), lambda t: (t, 1)),
                  pl.BlockSpec((CONV_HALO, dc), prev(0)), pl.BlockSpec((CONV_HALO, dc), prev(1)),
                  pl.BlockSpec((tc, dc), lambda t: (t, 0)), pl.BlockSpec((tc, dc), lambda t: (t, 0)),
                  pl.BlockSpec((CONV_HALO, dc), nxt(0)), pl.BlockSpec((CONV_HALO, dc), nxt(0)),
                  pl.BlockSpec((32, dc), lambda t: (0, 0)), vec, vec],
        out_specs=[pl.BlockSpec((tc, 2 * dc), lambda t: (t, 0)), pl.BlockSpec((32, dc), lambda t: (0, 0)), vec, vec, vec],
        out_shape=[_sds((r, 2 * dc), F32), _sds((32, dc), F32), _sds((1, dc), F32), _sds((1, dc), F32), _sds((1, dc), F32)],
        scratch_shapes=[pltpu.VMEM((CONV_HALO + tc, dc), F32), pltpu.VMEM((tc + CONV_HALO, dc), F32)],
        compiler_params=_params(("arbitrary",)),
    )(u, u, u, u, c, dmi, c, dmi, w32, cg, cbe)


def _inproj_bwd(dp1, dus, xcat, g_in, w_in, tp, tm):
    r, d = dp1.shape
    widths = [x.shape[1] for x in dus]
    offs = [sum(widths[:k]) for k in range(len(widths))]
    n = w_in.shape[1]
    nd = len(dus)
    tps = tp // tm

    def body(*refs):
        dp_ref = refs[0]
        du_refs = refs[1:1 + nd]
        x_ref, g_ref, w_ref, dx_ref, dmeta_ref, dg_ref, db_ref = refs[1 + nd:]
        i = pl.program_id(0)

        @pl.when(i == 0)
        def _():
            dmeta_ref[...] = jnp.zeros_like(dmeta_ref)
            dg_ref[...] = jnp.zeros_like(dg_ref)
            db_ref[...] = jnp.zeros_like(db_ref)

        ds0 = ALPHA * dp_ref[...]
        for k in range(nd):
            ds0 = ds0 + _mm_nt(du_refs[k][...].astype(BF16), w_ref[:, offs[k]:offs[k] + widths[k]])
        real = _row_in_seq(i, tm, tp) >= PAD_FRONT
        ds0 = jnp.where(real, ds0, 0.0)
        xhat, rstd = _ln(x_ref[...])
        dg_ref[...] += _rowsum(ds0 * xhat)
        db_ref[...] += _rowsum(ds0)
        dx = jnp.where(real, _ln_bwd(ds0 * g_ref[...], xhat, rstd), 0.0)
        dx_ref[...] = dx

        @pl.when(lax.rem(i, tps) == 0)
        def _():
            dmeta_ref[...] += dx[PAD_FRONT:X_OFF, :]

    row = lambda w: pl.BlockSpec((tm, w), lambda i: (i, 0))
    vec = pl.BlockSpec((1, d), lambda i: (0, 0))
    return pl.pallas_call(
        body, name="inproj_bwd", grid=(r // tm,),
        in_specs=[row(d)] + [row(w) for w in widths] + [row(d), vec, pl.BlockSpec((d, n), lambda i: (0, 0))],
        out_specs=[row(d), pl.BlockSpec((N_META, d), lambda i: (0, 0)), vec, vec],
        out_shape=[_sds((r, d), F32), _sds((N_META, d), F32), _sds((1, d), F32), _sds((1, d), F32)],
        compiler_params=_params(("arbitrary",)),
    )(dp1, *dus, xcat, g_in, w_in)


def _inproj_bwd_w(s0, dus, tm):
    r, d = s0.shape
    widths = [x.shape[1] for x in dus]
    nd = len(dus)

    def body(*refs):
        s_ref = refs[0]
        du_refs = refs[1:1 + nd]
        dw_refs = refs[1 + nd:]
        i = pl.program_id(0)

        @pl.when(i == 0)
        def _():
            for k in range(nd):
                dw_refs[k][...] = jnp.zeros_like(dw_refs[k])

        sb = s_ref[...].astype(BF16)
        for k in range(nd):
            dw_refs[k][...] += _mm_tn(sb, du_refs[k][...].astype(BF16))

    row = lambda w: pl.BlockSpec((tm, w), lambda i: (i, 0))
    return pl.pallas_call(
        body, name="inproj_bwd_w", grid=(r // tm,),
        in_specs=[row(d)] + [row(w) for w in widths],
        out_specs=[pl.BlockSpec((d, w), lambda i: (0, 0)) for w in widths],
        out_shape=[_sds((d, w), F32) for w in widths],
        compiler_params=_params(("arbitrary",)),
    )(s0, *dus)


def _local_step(x, tgt, meta, ln_in_g, ln_in_b, w_in, conv_w, conv_b, conv_ln_g, conv_ln_b, gate_up, gate_bias,
                gla_norm_g, w_out, ln1_g, ln1_b, w1g, w2, ln2_g, ln2_b):
    bsz, seq, d = x.shape
    tp = X_OFF + seq
    assert tp % CHUNK == 0
    nc = tp // CHUNK
    r = bsz * tp
    dc = conv_b.shape[1]
    tm = _pick_tile(tp, (352, 128, 64))
    tc = _pick_tile(tp, (704, 128, 64))
    nh = w1g.shape[0]

    xcat = jnp.concatenate([jnp.zeros((bsz, PAD_FRONT, d), F32), jnp.broadcast_to(meta[None], (bsz, N_META, d)), x],
                           axis=1).reshape(r, d)
    tgt_p = jnp.pad(tgt, ((0, 0), (X_OFF, 0), (0, 0))).reshape(r, d)
    w32 = jnp.pad(conv_w, ((0, 32 - CONV_WIDTH), (0, 0)))
    gup = jnp.pad(gate_up, ((0, LANES - GLA_RANK), (0, 0))).astype(BF16)

    s0, u = _inproj_fwd(xcat, ln_in_g, ln_in_b, w_in, tp, tm)
    c, co = _conv_fwd(u, w32, conv_b, conv_ln_g, conv_ln_b, tp, tc, dc)
    go, sta = _gla_fwd(u, gup, gate_bias, gla_norm_g, bsz, nc)
    p1, s1 = _outproj_fwd(s0, co, go, w_out, ln1_g, ln1_b, tm)
    hm, dp2, loss, dg2, db2 = _mlp_fwd(s1, w1g, w2, ln2_g, ln2_b, tgt_p, tp, tm)

    dh, dp1, dg1, db1 = _mlp_bwd_act(dp2, hm, w1g, w2, p1, ln1_g, tm)
    dw1, dw2 = _mlp_bwd_w(s1, hm, dh, dp2, nh, tm)
    dmi, dwo = _outproj_bwd(dp1, co, go, w_out, tm)
    dqk, dv, dr, dgd, dgn, dgb, dgup = _gla_bwd(u, dmi, sta, gup, gate_bias, gla_norm_g, bsz, nc)
    dcv, dcw, dcb, dcg, dcbe = _conv_bwd(u, c, dmi, w32, conv_ln_g, conv_ln_b, tp, tc, dc)
    dus = [dcv, dqk, dv, dr, dgd]
    dxcat, dmeta, dgi, dbi = _inproj_bwd(dp1, dus, xcat, ln_in_g, w_in, tp, tm)
    dwi = _inproj_bwd_w(s0, dus, tm)

    grad_x = dxcat.reshape(bsz, tp, d)[:, X_OFF:, :]
    return dict(loss=loss[0, 0], grad_x=grad_x, meta_tokens=dmeta, ln_in_g=dgi, ln_in_b=dbi,
                w_in=jnp.concatenate(dwi, axis=1), conv_w=dcw[:CONV_WIDTH], conv_b=dcb, conv_ln_g=dcg, conv_ln_b=dcbe,
                gate_up=dgup[:GLA_RANK], gate_bias=dgb, gla_norm_g=dgn, w_out=dwo, ln1_g=dg1, ln1_b=db1,
                w_ff1=dw1, w_ff2=dw2, ln2_g=dg2, ln2_b=db2)


def _exchange(arrays, scatter, name):
    na = len(arrays)
    npeer = N_DEV - 1

    def body(*refs):
        srcs = refs[:na]
        outs = refs[na:2 * na]
        send_sems, recv_sems, local_sems = refs[2 * na:]
        xi, yi, ci = (lax.axis_index(a) for a in MESH_AXES)
        me = 4 * xi + 2 * yi + ci
        copies = []
        for a in range(na):
            own = srcs[a].at[me] if scatter[a] else srcs[a]
            cp = pltpu.make_async_copy(own, outs[a].at[me], local_sems.at[a])
            cp.start()
            copies.append(cp)
        remote = []
        for k in range(1, N_DEV):
            px, py, pc = xi ^ (k >> 2), yi ^ ((k >> 1) & 1), ci ^ (k & 1)
            peer = 4 * px + 2 * py + pc
            for a in range(na):
                src = srcs[a].at[peer] if scatter[a] else srcs[a]
                cp = pltpu.make_async_remote_copy(
                    src_ref=src, dst_ref=outs[a].at[me],
                    send_sem=send_sems.at[a * npeer + k - 1], recv_sem=recv_sems.at[a * npeer + k - 1],
                    device_id=(px, py, pc), device_id_type=pl.DeviceIdType.MESH)
                cp.start()
                remote.append(cp)
        for cp in remote:
            cp.wait()
        for cp in copies:
            cp.wait()

    out_shape = [_sds(a.shape if scatter[i] else (N_DEV,) + a.shape, a.dtype) for i, a in enumerate(arrays)]
    anyspec = pl.BlockSpec(memory_space=pl.ANY)
    return pl.pallas_call(
        body, name=name,
        in_specs=[anyspec] * na, out_specs=[anyspec] * na, out_shape=out_shape,
        scratch_shapes=[pltpu.SemaphoreType.DMA((na * npeer,)), pltpu.SemaphoreType.DMA((na * npeer,)),
                        pltpu.SemaphoreType.DMA((na,))],
    )(*arrays)


def _adamw(w, g, m, v):
    m = ADAM_B1 * m + (1.0 - ADAM_B1) * g
    v = ADAM_B2 * v + (1.0 - ADAM_B2) * jnp.square(g)
    m_hat = m / (1.0 - ADAM_B1 ** ADAM_STEP)
    v_hat = v / (1.0 - ADAM_B2 ** ADAM_STEP)
    delta = -ADAM_LR * (m_hat / (jnp.sqrt(v_hat) + ADAM_EPS) + ADAM_WD * w)
    return delta, m, v


def _sum_devices(ref):
    g = ref[0].astype(F32)
    for k in range(1, N_DEV):
        g = g + ref[k].astype(F32)
    return g


def _update_big(parts, w, m, v, name):
    rows, cols = w.shape
    tr = _pick_tile(rows, (128, 64, 16))

    def body(p_ref, w_ref, m_ref, v_ref, g_ref, d_ref, nm_ref, nv_ref):
        g = _sum_devices(p_ref)
        g_ref[...] = g
        d_ref[...], nm_ref[...], nv_ref[...] = _adamw(w_ref[...], g, m_ref[...], v_ref[...])

    blk = pl.BlockSpec((tr, cols), lambda i: (i, 0))
    return pl.pallas_call(
        body, name=name, grid=(rows // tr,),
        in_specs=[pl.BlockSpec((N_DEV, tr, cols), lambda i: (0, i, 0)), blk, blk, blk],
        out_specs=[blk] * 4, out_shape=[_sds((rows, cols), F32)] * 4,
        compiler_params=_params(("parallel",)),
    )(parts, w, m, v)


_VEC_ORDER = ("ln_in_g", "ln_in_b", "conv_b", "conv_ln_g", "conv_ln_b", "gate_bias", "gla_norm_g",
              "ln1_g", "ln1_b", "ln2_g", "ln2_b")
_SHARDED_SMALL = (("meta_tokens", 0, N_META, LANES), ("conv_w", N_META, CONV_WIDTH, None), ("gate_up", N_META + 32, GLA_RANK, None))


def _update_small(parts_sh, parts_vec, wmv):
    names = [s[0] for s in _SHARDED_SMALL] + list(_VEC_ORDER)
    flat = [a for nme in names for a in wmv[nme]]

    def body(*refs):
        sh_ref, vec_ref = refs[0], refs[1]
        ins = refs[2:2 + len(flat)]
        outs = refs[2 + len(flat):2 + len(flat) + 4 * len(names)]
        gsh_ref, gvec_ref = refs[-2:]
        gsh_ref[...] = _sum_devices(sh_ref)
        gvec_ref[...] = _sum_devices(vec_ref)
        for idx, nme in enumerate(names):
            w_ref, m_ref, v_ref = ins[3 * idx:3 * idx + 3]
            rows, cols = w_ref.shape
            if idx < len(_SHARDED_SMALL):
                r0 = _SHARDED_SMALL[idx][1]
                g = gsh_ref[r0:r0 + rows, 0:cols]
            else:
                j = idx - len(_SHARDED_SMALL)
                g = gvec_ref[j:j + 1, 0:cols]
            o = outs[4 * idx:4 * idx + 4]
            o[0][...] = g
            o[1][...], o[2][...], o[3][...] = _adamw(w_ref[...], g, m_ref[...], v_ref[...])

    out_shape = [_sds(wmv[nme][0].shape, F32) for nme in names for _ in range(4)]
    vmem = pl.BlockSpec(memory_space=pltpu.VMEM)
    res = pl.pallas_call(
        body, name="update_small", out_shape=out_shape,
        in_specs=[vmem] * (2 + len(flat)), out_specs=[vmem] * len(out_shape),
        scratch_shapes=[pltpu.VMEM(parts_sh.shape[1:], F32), pltpu.VMEM(parts_vec.shape[1:], F32)],
    )(parts_sh, parts_vec, *flat)
    return {nme: res[4 * i:4 * i + 4] for i, nme in enumerate(names)}


_WEIGHTS = ("meta_tokens", "ln_in_g", "ln_in_b", "w_in", "conv_w", "conv_b", "conv_ln_g", "conv_ln_b", "gate_up",
            "gate_bias", "gla_norm_g", "w_out", "ln1_g", "ln1_b", "w_ff1", "w_ff2", "ln2_g", "ln2_b")


def kernel(x, meta_tokens, ln_in_g, ln_in_b, w_in, conv_w, conv_b, conv_ln_g, conv_ln_b, gate_up, gate_bias, gla_norm_g, w_out, ln1_g, ln1_b, w_ff1, w_ff2, ln2_g, ln2_b, loss_target, m_meta_tokens, m_ln_in_g, m_ln_in_b, m_w_in, m_conv_w, m_conv_b, m_conv_ln_g, m_conv_ln_b, m_gate_up, m_gate_bias, m_gla_norm_g, m_w_out, m_ln1_g, m_ln1_b, m_w_ff1, m_w_ff2, m_ln2_g, m_ln2_b, v_meta_tokens, v_ln_in_g, v_ln_in_b, v_w_in, v_conv_w, v_conv_b, v_conv_ln_g, v_conv_ln_b, v_gate_up, v_gate_bias, v_gla_norm_g, v_w_out, v_ln1_g, v_ln1_b, v_w_ff1, v_w_ff2, v_ln2_g, v_ln2_b):
    w = dict(meta_tokens=meta_tokens, ln_in_g=ln_in_g, ln_in_b=ln_in_b, w_in=w_in, conv_w=conv_w, conv_b=conv_b,
             conv_ln_g=conv_ln_g, conv_ln_b=conv_ln_b, gate_up=gate_up, gate_bias=gate_bias, gla_norm_g=gla_norm_g,
             w_out=w_out, ln1_g=ln1_g, ln1_b=ln1_b, w_ff1=w_ff1, w_ff2=w_ff2, ln2_g=ln2_g, ln2_b=ln2_b)
    mom = dict(meta_tokens=m_meta_tokens, ln_in_g=m_ln_in_g, ln_in_b=m_ln_in_b, w_in=m_w_in, conv_w=m_conv_w,
               conv_b=m_conv_b, conv_ln_g=m_conv_ln_g, conv_ln_b=m_conv_ln_b, gate_up=m_gate_up, gate_bias=m_gate_bias,
               gla_norm_g=m_gla_norm_g, w_out=m_w_out, ln1_g=m_ln1_g, ln1_b=m_ln1_b, w_ff1=m_w_ff1, w_ff2=m_w_ff2,
               ln2_g=m_ln2_g, ln2_b=m_ln2_b)
    var = dict(meta_tokens=v_meta_tokens, ln_in_g=v_ln_in_g, ln_in_b=v_ln_in_b, w_in=v_w_in, conv_w=v_conv_w,
               conv_b=v_conv_b, conv_ln_g=v_conv_ln_g, conv_ln_b=v_conv_ln_b, gate_up=v_gate_up, gate_bias=v_gate_bias,
               gla_norm_g=v_gla_norm_g, w_out=v_w_out, ln1_g=v_ln1_g, ln1_b=v_ln1_b, w_ff1=v_w_ff1, w_ff2=v_w_ff2,
               ln2_g=v_ln2_g, ln2_b=v_ln2_b)
    shapes = {k: a.shape for k, a in w.items()}

    def two_d(a):
        return a.reshape(1, -1) if a.ndim == 1 else a.reshape(a.shape[-2:])

    w2d = {k: two_d(a) for k, a in w.items()}
    m2d = {k: two_d(a) for k, a in mom.items()}
    v2d = {k: two_d(a) for k, a in var.items()}
    d = x.shape[-1]
    d_in = w2d["w_in"].shape[1] * N_DEV
    d_in_p = -(-d_in // LANES) * LANES

    g_in, g_out, g_ff1, g_ff2, g_meta, g_conv, g_gup = _exchange(
        [w2d["w_in"].astype(BF16), w2d["w_out"].astype(BF16), w2d["w_ff1"].astype(BF16), w2d["w_ff2"].astype(BF16),
         w2d["meta_tokens"], w2d["conv_w"], w2d["gate_up"]], [False] * 7, "gather_weights")
    w_in_full = jnp.pad(g_in.transpose(1, 0, 2).reshape(d, d_in), ((0, 0), (0, d_in_p - d_in)))
    w_out_full = g_out.reshape(-1, d)
    w2_full = g_ff2.reshape(-1, d)
    meta_full = g_meta.transpose(1, 0, 2).reshape(N_META, d)
    conv_w_full = g_conv.transpose(1, 0, 2).reshape(CONV_WIDTH, -1)
    gate_up_full = g_gup.transpose(1, 0, 2).reshape(GLA_RANK, -1)

    res = _local_step(x, loss_target, meta_full, w2d["ln_in_g"], w2d["ln_in_b"], w_in_full, conv_w_full, w2d["conv_b"],
                      w2d["conv_ln_g"], w2d["conv_ln_b"], gate_up_full, w2d["gate_bias"], w2d["gla_norm_g"], w_out_full,
                      w2d["ln1_g"], w2d["ln1_b"], g_ff1, w2_full, w2d["ln2_g"], w2d["ln2_b"])

    p_in = res["w_in"][:, :d_in].reshape(d, N_DEV, d_in // N_DEV).transpose(1, 0, 2).astype(BF16)
    p_out = res["w_out"].reshape(N_DEV, -1, d)
    dc = res["conv_w"].shape[1]
    hk = res["gate_up"].shape[1]
    sh_meta = res["meta_tokens"].reshape(N_META, N_DEV, LANES).transpose(1, 0, 2)
    sh_conv = jnp.pad(res["conv_w"].reshape(CONV_WIDTH, N_DEV, dc // N_DEV).transpose(1, 0, 2),
                      ((0, 0), (0, 32 - CONV_WIDTH), (0, LANES - dc // N_DEV)))
    sh_gup = jnp.pad(res["gate_up"].reshape(GLA_RANK, N_DEV, hk // N_DEV).transpose(1, 0, 2),
                     ((0, 0), (0, 0), (0, LANES - hk // N_DEV)))
    p_sh = jnp.concatenate([sh_meta, sh_conv, sh_gup], axis=1)
    p_vec = jnp.concatenate([jnp.pad(res[k], ((0, 0), (0, d - res[k].shape[1]))) for k in _VEC_ORDER]
                            + [jnp.zeros((16 - len(_VEC_ORDER), d), F32)], axis=0)

    r_in, r_out, r_ff1, r_ff2, r_sh, r_vec = _exchange(
        [p_in, p_out, res["w_ff1"], res["w_ff2"], p_sh, p_vec], [True, True, True, True, True, False], "scatter_grads")

    upd = {}
    upd["w_in"] = _update_big(r_in, w2d["w_in"], m2d["w_in"], v2d["w_in"], "update_w_in")
    upd["w_out"] = _update_big(r_out, w2d["w_out"], m2d["w_out"], v2d["w_out"], "update_w_out")
    upd["w_ff1"] = _update_big(r_ff1, w2d["w_ff1"], m2d["w_ff1"], v2d["w_ff1"], "update_w_ff1")
    upd["w_ff2"] = _update_big(r_ff2, w2d["w_ff2"], m2d["w_ff2"], v2d["w_ff2"], "update_w_ff2")
    small = [s[0] for s in _SHARDED_SMALL] + list(_VEC_ORDER)
    upd.update(_update_small(r_sh, r_vec, {k: (w2d[k], m2d[k], v2d[k]) for k in small}))

    loss = lax.psum(res["loss"], MESH_AXES)
    outs = [loss, res["grad_x"]]
    for j in range(4):
        outs += [upd[k][j].reshape(shapes[k]) for k in _WEIGHTS]
    return tuple(outs)
```

```python
import functools

import jax
import jax.numpy as jnp
from jax import lax
from jax.experimental import pallas as pl
from jax.experimental.pallas import tpu as pltpu

F32 = jnp.float32
BF16 = jnp.bfloat16

N_META = 16
CHUNK = 64
PAD_FRONT = (-N_META) % CHUNK
X_OFF = PAD_FRONT + N_META
CONV_WIDTH = 31
CONV_HALO = 32
CONV_SUB = 64
CONV_WIN = CONV_SUB + CONV_HALO
GLA_HEADS = 4
GLA_DK = 64
GLA_DV = 128
GLA_RANK = 16
GLA_TAU = 16.0
QK_SCALE = GLA_DK ** -0.5
LN_EPS = 1e-5
ALPHA = 2.0 ** 0.25
LANES = 128
N_DEV = 8
ADAM_LR = 0.001
ADAM_B1 = 0.9
ADAM_B2 = 0.999
ADAM_EPS = 1e-08
ADAM_WD = 0.01
ADAM_STEP = 10
VMEM_LIMIT = 56 * 1024 * 1024
MESH_AXES = ("x", "y", "c")


def _sds(shape, dtype):
    return jax.ShapeDtypeStruct(shape, dtype)


def _mm(a, b):
    return jnp.dot(a, b, preferred_element_type=F32)


def _mm_nt(a, b):
    return lax.dot_general(a, b, (((1,), (1,)), ((), ())), preferred_element_type=F32)


def _mm_tn(a, b):
    return lax.dot_general(a, b, (((0,), (0,)), ((), ())), preferred_element_type=F32)


def _sigmoid(x):
    return 1.0 / (1.0 + jnp.exp(-x))


def _log_sigmoid(z):
    return jnp.minimum(z, 0.0) - jnp.log(1.0 + jnp.exp(-jnp.abs(z)))


def _ln(x):
    mu = jnp.mean(x, axis=-1, keepdims=True)
    xc = x - mu
    var = jnp.mean(xc * xc, axis=-1, keepdims=True)
    rstd = lax.rsqrt(var + LN_EPS)
    return xc * rstd, rstd


def _ln_bwd(dyg, xhat, rstd):
    m1 = jnp.mean(dyg, axis=-1, keepdims=True)
    m2 = jnp.mean(dyg * xhat, axis=-1, keepdims=True)
    return rstd * (dyg - m1 - xhat * m2)


def _rowsum(x):
    return jnp.sum(x, axis=0, keepdims=True)


def _row_in_seq(i, tm, tp):
    base = lax.rem(i * tm, tp)
    return base + lax.broadcasted_iota(jnp.int32, (tm, 1), 0)


def _split3(x):
    hi = x.astype(BF16)
    r1 = x - hi.astype(F32)
    mid = r1.astype(BF16)
    lo = (r1 - mid.astype(F32)).astype(BF16)
    return hi, mid, lo


def _tri_mm(tri, x):
    hi, mid, lo = _split3(x)
    return _mm(tri, hi) + _mm(tri, mid) + _mm(tri, lo)


def _params(sem):
    return pltpu.CompilerParams(dimension_semantics=sem, vmem_limit_bytes=VMEM_LIMIT)


def _pick_tile(n, prefs):
    for t in prefs:
        if n % t == 0:
            return t
    raise ValueError(f"no tile for {n}")


def _inproj_fwd(xcat, g, b, w_in, tp, tm):
    r, d = xcat.shape
    n = w_in.shape[1]

    def body(x_ref, g_ref, b_ref, w_ref, s0_ref, u_ref):
        i = pl.program_id(0)
        xhat, _ = _ln(x_ref[...])
        real = _row_in_seq(i, tm, tp) >= PAD_FRONT
        s = jnp.where(real, xhat * g_ref[...] + b_ref[...], 0.0)
        s0_ref[...] = s
        u_ref[...] = _mm(s.astype(BF16), w_ref[...])

    return pl.pallas_call(
        body, name="inproj_fwd", grid=(r // tm,),
        in_specs=[pl.BlockSpec((tm, d), lambda i: (i, 0)), pl.BlockSpec((1, d), lambda i: (0, 0)),
                  pl.BlockSpec((1, d), lambda i: (0, 0)), pl.BlockSpec((d, n), lambda i: (0, 0))],
        out_specs=[pl.BlockSpec((tm, d), lambda i: (i, 0)), pl.BlockSpec((tm, n), lambda i: (i, 0))],
        out_shape=[_sds((r, d), F32), _sds((r, n), F32)],
        compiler_params=_params(("parallel",)),
    )(xcat, g, b, w_in)


def _conv_taps(win, coef, lo):
    acc = None
    for rho in range(8):
        offs = [o for o in range(lo, lo + CONV_WIDTH) if o % 8 == rho]
        if not offs:
            continue
        rolled = win if rho == 0 else pltpu.roll(win, CONV_WIN - rho, 0)
        for o in offs:
            m8 = o - rho
            term = rolled[m8:m8 + CONV_SUB, :] * coef(o)
            acc = term if acc is None else acc + term
    return acc


def _conv_fwd(u, w32, cb, cg, cbe, tp, tc, dc):
    r = u.shape[0]
    hb = tc // CONV_HALO

    def body(a_ref, g_ref, ah_ref, gh_ref, w_ref, cb_ref, cg_ref, cbe_ref, c_ref, co_ref, hs_ref):
        t = pl.program_id(0)
        first = lax.rem(t * tc, tp) == 0
        hh = ah_ref[...] * _sigmoid(gh_ref[...])
        hs_ref[0:CONV_HALO, :] = jnp.where(first, 0.0, hh)
        hs_ref[CONV_HALO:CONV_HALO + tc, :] = a_ref[...] * _sigmoid(g_ref[...])

        def sub(k, carry):
            r0 = pl.multiple_of(k * CONV_SUB, CONV_SUB)
            win = hs_ref[pl.ds(r0, CONV_WIN), :]
            c = _conv_taps(win, lambda o: w_ref[o - 2:o - 1, :], 2) + cb_ref[...]
            c_ref[pl.ds(r0, CONV_SUB), :] = c
            xhat, _ = _ln(c)
            cn = xhat * cg_ref[...] + cbe_ref[...]
            co_ref[pl.ds(r0, CONV_SUB), :] = (cn * _sigmoid(cn)).astype(BF16)
            return carry

        lax.fori_loop(0, tc // CONV_SUB, sub, 0)

    vec = pl.BlockSpec((1, dc), lambda t: (0, 0))
    return pl.pallas_call(
        body, name="conv_fwd", grid=(r // tc,),
        in_specs=[pl.BlockSpec((tc, dc), lambda t: (t, 0)), pl.BlockSpec((tc, dc), lambda t: (t, 1)),
                  pl.BlockSpec((CONV_HALO, dc), lambda t: (jnp.maximum(t * hb - 1, 0), 0)),
                  pl.BlockSpec((CONV_HALO, dc), lambda t: (jnp.maximum(t * hb - 1, 0), 1)),
                  pl.BlockSpec((32, dc), lambda t: (0, 0)), vec, vec, vec],
        out_specs=[pl.BlockSpec((tc, dc), lambda t: (t, 0)), pl.BlockSpec((tc, dc), lambda t: (t, 0))],
        out_shape=[_sds((r, dc), F32), _sds((r, dc), BF16)],
        scratch_shapes=[pltpu.VMEM((CONV_HALO + tc, dc), F32)],
        compiler_params=_params(("parallel",)),
    )(u, u, u, u, w32, cb, cg, cbe)


def _gla_prep(qk, gd, gup, gb, n):
    z = _mm(gd.astype(BF16), gup) + gb
    lg = _log_sigmoid(z) * (1.0 / GLA_TAU)
    row = n * CHUNK + lax.broadcasted_iota(jnp.int32, (CHUNK, 1), 0)
    real = row >= PAD_FRONT
    lg = jnp.where(real, lg, 0.0)
    ri = lax.broadcasted_iota(jnp.int32, (CHUNK, CHUNK), 0)
    ci = lax.broadcasted_iota(jnp.int32, (CHUNK, CHUNK), 1)
    low = (ri >= ci).astype(BF16)
    b = _tri_mm(low, lg)
    bl = _rowsum(lg)
    hk = GLA_HEADS * GLA_DK
    q = qk[:, :hk] * QK_SCALE
    k = qk[:, hk:]
    eb = jnp.exp(b)
    enb = jnp.exp(-b)
    ebl = jnp.exp(bl - b)
    gam = jnp.exp(bl)
    return dict(z=z, real=real, ri=ri, ci=ci, eb=eb, enb=enb, ebl=ebl, gam=gam, k=k,
                qe=q * eb, ke=k * enb, kd=k * ebl)


def _head_mask(h2):
    lane = lax.broadcasted_iota(jnp.int32, (1, LANES), 1)
    return (lane < GLA_DK) if h2 == 0 else (lane >= GLA_DK)


def _gla_fwd(u, gup, gb, gn, bsz, nc):
    r = u.shape[0]
    hv = GLA_HEADS * GLA_DV

    def body(qk_ref, v_ref, r_ref, gd_ref, gup_ref, gb_ref, gn_ref, go_ref, sta_ref, st_ref):
        n = pl.program_id(1)

        @pl.when(n == 0)
        def _():
            st_ref[...] = jnp.zeros_like(st_ref)

        p = _gla_prep(qk_ref[...], gd_ref[...], gup_ref[...], gb_ref[...], n)
        tril = p["ri"] >= p["ci"]
        for h in range(GLA_HEADS):
            hp, h2 = divmod(h, 2)
            ls = slice(hp * LANES, (hp + 1) * LANES)
            m = _head_mask(h2)
            qeh = jnp.where(m, p["qe"][:, ls], 0.0).astype(BF16)
            kdh = jnp.where(m, p["kd"][:, ls], 0.0).astype(BF16)
            keh = p["ke"][:, ls].astype(BF16)
            vh = v_ref[:, h * GLA_DV:(h + 1) * GLA_DV].astype(BF16)
            st = st_ref[h]
            sta_ref[h] = st
            a = jnp.where(tril, _mm_nt(qeh, keh), 0.0)
            o = _mm(a.astype(BF16), vh) + _mm_nt(qeh, st.astype(BF16))
            st_ref[h] = st * p["gam"][:, ls] + _mm_tn(vh, kdh)
            rs = lax.rsqrt(jnp.mean(o * o, axis=-1, keepdims=True) + LN_EPS)
            rr = r_ref[:, h * GLA_DV:(h + 1) * GLA_DV]
            go = o * rs * gn_ref[...] * (rr * _sigmoid(rr))
            go_ref[:, h * GLA_DV:(h + 1) * GLA_DV] = go.astype(BF16)

    rowblk = lambda col: (lambda b, n: (b * nc + n, col))
    const = lambda b, n: (0, 0)
    return pl.pallas_call(
        body, name="gla_fwd", grid=(bsz, nc),
        in_specs=[pl.BlockSpec((CHUNK, 512), rowblk(2)), pl.BlockSpec((CHUNK, hv), rowblk(3)),
                  pl.BlockSpec((CHUNK, hv), rowblk(4)), pl.BlockSpec((CHUNK, LANES), rowblk(20)),
                  pl.BlockSpec((LANES, 256), const), pl.BlockSpec((1, 256), const), pl.BlockSpec((1, GLA_DV), const)],
        out_specs=[pl.BlockSpec((CHUNK, hv), rowblk(0)),
                   pl.BlockSpec((None, GLA_HEADS, LANES, LANES), lambda b, n: (b * nc + n, 0, 0, 0))],
        out_shape=[_sds((r, hv), BF16), _sds((bsz * nc, GLA_HEADS, LANES, LANES), F32)],
        scratch_shapes=[pltpu.VMEM((GLA_HEADS, LANES, LANES), F32)],
        compiler_params=_params(("parallel", "arbitrary")),
    )(u, u, u, u, gup, gb, gn)


def _outproj_fwd(s0, co, go, w_out, g1, b1, tm):
    r, d = s0.shape
    dc = co.shape[1]

    def body(s0_ref, co_ref, go_ref, w_ref, g_ref, b_ref, p1_ref, s1_ref):
        mix = _mm(co_ref[...], w_ref[0:dc, :]) + _mm(go_ref[...], w_ref[dc:2 * dc, :])
        p1 = ALPHA * s0_ref[...] + mix
        p1_ref[...] = p1
        xhat, _ = _ln(p1)
        s1_ref[...] = xhat * g_ref[...] + b_ref[...]

    row = lambda w: pl.BlockSpec((tm, w), lambda i: (i, 0))
    vec = pl.BlockSpec((1, d), lambda i: (0, 0))
    return pl.pallas_call(
        body, name="outproj_fwd", grid=(r // tm,),
        in_specs=[row(d), row(dc), row(dc), pl.BlockSpec((2 * dc, d), lambda i: (0, 0)), vec, vec],
        out_specs=[row(d), row(d)],
        out_shape=[_sds((r, d), F32), _sds((r, d), F32)],
        compiler_params=_params(("parallel",)),
    )(s0, co, go, w_out, g1, b1)


def _mlp_fwd(s1, w1g, w2, g2, b2, tgt, tp, tm):
    r, d = s1.shape
    nh, _, th = w1g.shape

    def body(s1_ref, w1_ref, w2_ref, g_ref, b_ref, t_ref, hm_ref, dp2_ref, loss_ref, dg_ref, db_ref, acc_ref, sb_ref):
        i = pl.program_id(0)
        j = pl.program_id(1)

        @pl.when(jnp.logical_and(i == 0, j == 0))
        def _():
            loss_ref[...] = jnp.zeros_like(loss_ref)
            dg_ref[...] = jnp.zeros_like(dg_ref)
            db_ref[...] = jnp.zeros_like(db_ref)

        @pl.when(j == 0)
        def _():
            acc_ref[...] = jnp.zeros_like(acc_ref)
            sb_ref[...] = s1_ref[...].astype(BF16)

        h = _mm(sb_ref[...], w1_ref[...])
        hm_ref[...] = h.astype(BF16)
        act = jnp.square(jnp.maximum(h, 0.0))
        acc_ref[...] += _mm(act.astype(BF16), w2_ref[...])

        @pl.when(j == nh - 1)
        def _():
            p2 = ALPHA * s1_ref[...] + acc_ref[...]
            xhat, rstd = _ln(p2)
            s2 = xhat * g_ref[...] + b_ref[...]
            isx = _row_in_seq(i, tm, tp) >= X_OFF
            err = jnp.where(isx, s2 - t_ref[...], 0.0)
            loss_ref[...] += 0.5 * jnp.sum(jnp.mean(err * err, axis=-1, keepdims=True))
            dy = err * (1.0 / d)
            dg_ref[...] += _rowsum(dy * xhat)
            db_ref[...] += _rowsum(dy)
            dp2_ref[...] = _ln_bwd(dy * g_ref[...], xhat, rstd)

    row = pl.BlockSpec((tm, d), lambda i, j: (i, 0))
    vec = pl.BlockSpec((1, d), lambda i, j: (0, 0))
    return pl.pallas_call(
        body, name="mlp_fwd", grid=(r // tm, nh),
        in_specs=[row, pl.BlockSpec((None, d, th), lambda i, j: (j, 0, 0)), pl.BlockSpec((th, d), lambda i, j: (j, 0)),
                  vec, vec, row],
        out_specs=[pl.BlockSpec((tm, th), lambda i, j: (i, j)), row,
                   pl.BlockSpec((8, LANES), lambda i, j: (0, 0)), vec, vec],
        out_shape=[_sds((r, nh * th), BF16), _sds((r, d), F32), _sds((8, LANES), F32), _sds((1, d), F32), _sds((1, d), F32)],
        scratch_shapes=[pltpu.VMEM((tm, d), F32), pltpu.VMEM((tm, d), BF16)],
        compiler_params=_params(("arbitrary", "arbitrary")),
    )(s1, w1g, w2, g2, b2, tgt)


def _mlp_bwd_act(dp2, hm, w1g, w2, p1, g1, tm):
    r, d = dp2.shape
    nh, _, th = w1g.shape

    def body(dp2_ref, hm_ref, w1_ref, w2_ref, p1_ref, g_ref, dh_ref, dp1_ref, dg_ref, db_ref, acc_ref, db16_ref):
        i = pl.program_id(0)
        j = pl.program_id(1)

        @pl.when(jnp.logical_and(i == 0, j == 0))
        def _():
            dg_ref[...] = jnp.zeros_like(dg_ref)
            db_ref[...] = jnp.zeros_like(db_ref)

        @pl.when(j == 0)
        def _():
            acc_ref[...] = jnp.zeros_like(acc_ref)
            db16_ref[...] = dp2_ref[...].astype(BF16)

        dact = _mm_nt(db16_ref[...], w2_ref[...])
        dh = (dact * (2.0 * jnp.maximum(hm_ref[...].astype(F32), 0.0))).astype(BF16)
        dh_ref[...] = dh
        acc_ref[...] += _mm_nt(dh, w1_ref[...])

        @pl.when(j == nh - 1)
        def _():
            ds1 = ALPHA * dp2_ref[...] + acc_ref[...]
            xhat, rstd = _ln(p1_ref[...])
            dg_ref[...] += _rowsum(ds1 * xhat)
            db_ref[...] += _rowsum(ds1)
            dp1_ref[...] = _ln_bwd(ds1 * g_ref[...], xhat, rstd)

    row = pl.BlockSpec((tm, d), lambda i, j: (i, 0))
    vec = pl.BlockSpec((1, d), lambda i, j: (0, 0))
    blk = pl.BlockSpec((tm, th), lambda i, j: (i, j))
    return pl.pallas_call(
        body, name="mlp_bwd_act", grid=(r // tm, nh),
        in_specs=[row, blk, pl.BlockSpec((None, d, th), lambda i, j: (j, 0, 0)), pl.BlockSpec((th, d), lambda i, j: (j, 0)),
                  row, vec],
        out_specs=[blk, row, vec, vec],
        out_shape=[_sds((r, nh * th), BF16), _sds((r, d), F32), _sds((1, d), F32), _sds((1, d), F32)],
        scratch_shapes=[pltpu.VMEM((tm, d), F32), pltpu.VMEM((tm, d), BF16)],
        compiler_params=_params(("arbitrary", "arbitrary")),
    )(dp2, hm, w1g, w2, p1, g1)


def _mlp_bwd_w(s1, hm, dh, dp2, nh, tm):
    r, d = s1.shape
    th = hm.shape[1] // nh

    def body(s1_ref, hm_ref, dh_ref, dp2_ref, dw1_ref, dw2_ref, a1_ref, a2_ref):
        i = pl.program_id(1)

        @pl.when(i == 0)
        def _():
            a1_ref[...] = jnp.zeros_like(a1_ref)
            a2_ref[...] = jnp.zeros_like(a2_ref)

        act = jnp.square(jnp.maximum(hm_ref[...].astype(F32), 0.0)).astype(BF16)
        a1_ref[...] += _mm_tn(s1_ref[...].astype(BF16), dh_ref[...])
        a2_ref[...] += _mm_tn(act, dp2_ref[...].astype(BF16))

        @pl.when(i == pl.num_programs(1) - 1)
        def _():
            dw1_ref[...] = a1_ref[...].astype(BF16)
            dw2_ref[...] = a2_ref[...].astype(BF16)

    row = pl.BlockSpec((tm, d), lambda j, i: (i, 0))
    blk = pl.BlockSpec((tm, th), lambda j, i: (i, j))
    return pl.pallas_call(
        body, name="mlp_bwd_w", grid=(nh, r // tm),
        in_specs=[row, blk, blk, row],
        out_specs=[pl.BlockSpec((None, d, th), lambda j, i: (j, 0, 0)), pl.BlockSpec((None, th, d), lambda j, i: (j, 0, 0))],
        out_shape=[_sds((nh, d, th), BF16), _sds((nh, th, d), BF16)],
        scratch_shapes=[pltpu.VMEM((d, th), F32), pltpu.VMEM((th, d), F32)],
        compiler_params=_params(("parallel", "arbitrary")),
    )(s1, hm, dh, dp2)


def _outproj_bwd(dp1, co, go, w_out, dep, tm):
    r, d = dp1.shape
    dc = co.shape[1]

    def body(dp_ref, co_ref, go_ref, w_ref, dep_ref, dmi_ref, dw_ref, acc_ref):
        i = pl.program_id(0)

        @pl.when(i == 0)
        def _():
            acc_ref[...] = jnp.zeros_like(acc_ref)

        dpb = dp_ref[...].astype(BF16)
        dmi_ref[...] = _mm_nt(dpb, w_ref[...])
        acc_ref[0:dc, :] += _mm_tn(co_ref[...], dpb)
        acc_ref[dc:2 * dc, :] += _mm_tn(go_ref[...], dpb)

        @pl.when(i == pl.num_programs(0) - 1)
        def _():
            dw_ref[...] = acc_ref[...].astype(BF16)

    row = lambda w: pl.BlockSpec((tm, w), lambda i: (i, 0))
    full = pl.BlockSpec((2 * dc, d), lambda i: (0, 0))
    return pl.pallas_call(
        body, name="outproj_bwd", grid=(r // tm,),
        in_specs=[row(d), row(dc), row(dc), full, pl.BlockSpec(memory_space=pl.ANY)],
        out_specs=[row(2 * dc), full],
        out_shape=[_sds((r, 2 * dc), F32), _sds((2 * dc, d), BF16)],
        scratch_shapes=[pltpu.VMEM((2 * dc, d), F32)],
        compiler_params=_params(("arbitrary",)),
    )(dp1, co, go, w_out, dep)


def _gla_bwd(u, dmi, sta, gup, gb, gn, dep, bsz, nc):
    r = u.shape[0]
    hv = GLA_HEADS * GLA_DV
    hk = GLA_HEADS * GLA_DK

    def body(qk_ref, v_ref, r_ref, gd_ref, dgo_ref, sta_ref, gup_ref, gb_ref, gn_ref, dep_ref,
             dqk_ref, dv_ref, dr_ref, dgd_ref, dgn_ref, dgb_ref, dgup_ref, dst_ref):
        bi = pl.program_id(0)
        t = pl.program_id(1)
        n = nc - 1 - t

        @pl.when(jnp.logical_and(bi == 0, t == 0))
        def _():
            dgn_ref[...] = jnp.zeros_like(dgn_ref)
            dgb_ref[...] = jnp.zeros_like(dgb_ref)
            dgup_ref[...] = jnp.zeros_like(dgup_ref)

        @pl.when(t == 0)
        def _():
            dst_ref[...] = jnp.zeros_like(dst_ref)

        gd = gd_ref[...]
        p = _gla_prep(qk_ref[...], gd, gup_ref[...], gb_ref[...], n)
        tril = p["ri"] >= p["ci"]
        dgn = jnp.zeros((1, GLA_DV), F32)
        dqe_t, dke_t, dkd_t, dgam_t = [], [], [], []
        for hp in range(GLA_HEADS // 2):
            ls = slice(hp * LANES, (hp + 1) * LANES)
            dqe = jnp.zeros((CHUNK, LANES), F32)
            dke = jnp.zeros((CHUNK, LANES), F32)
            dkd = jnp.zeros((CHUNK, LANES), F32)
            dgam = jnp.zeros((1, LANES), F32)
            for h2 in range(2):
                h = 2 * hp + h2
                vs = slice(h * GLA_DV, (h + 1) * GLA_DV)
                m = _head_mask(h2)
                qeh = jnp.where(m, p["qe"][:, ls], 0.0).astype(BF16)
                kdh = jnp.where(m, p["kd"][:, ls], 0.0).astype(BF16)
                keh = p["ke"][:, ls].astype(BF16)
                vh = v_ref[:, vs].astype(BF16)
                st = sta_ref[h]
                stb = st.astype(BF16)
                a = jnp.where(tril, _mm_nt(qeh, keh), 0.0)
                ab = a.astype(BF16)
                o = _mm(ab, vh) + _mm_nt(qeh, stb)
                rr = r_ref[:, vs]
                sr = _sigmoid(rr)
                rs = lax.rsqrt(jnp.mean(o * o, axis=-1, keepdims=True) + LN_EPS)
                y = o * rs
                dgo = dgo_ref[:, vs]
                don = dgo * (rr * sr)
                dr_ref[:, vs] = dgo * (y * gn_ref[...]) * (sr * (1.0 + rr * (1.0 - sr)))
                dgn = dgn + _rowsum(don * y)
                dxn = don * gn_ref[...]
                do = rs * (dxn - y * jnp.mean(dxn * y, axis=-1, keepdims=True))
                dob = do.astype(BF16)
                dst = dst_ref[h]
                dstb = dst.astype(BF16)
                da = jnp.where(tril, _mm_nt(dob, vh), 0.0).astype(BF16)
                dv_ref[:, vs] = _mm_tn(ab, dob) + _mm_nt(kdh, dstb)
                dqe = dqe + jnp.where(m, _mm(da, keh) + _mm(dob, stb), 0.0)
                dke = dke + _mm_tn(da, qeh)
                dkd = dkd + jnp.where(m, _mm(vh, dstb), 0.0)
                dgam = dgam + _rowsum(dst * st)
                dst_ref[h] = dst * p["gam"][:, ls] + _mm_tn(dob, qeh)
            dqe_t.append(dqe)
            dke_t.append(dke)
            dkd_t.append(dkd)
            dgam_t.append(dgam)
        dqe = jnp.concatenate(dqe_t, axis=1)
        dke = jnp.concatenate(dke_t, axis=1)
        dkd = jnp.concatenate(dkd_t, axis=1)
        dgam = jnp.concatenate(dgam_t, axis=1)
        dqk_ref[:, :hk] = dqe * p["eb"] * QK_SCALE
        dqk_ref[:, hk:] = dke * p["enb"] + dkd * p["ebl"]
        dkdkd = dkd * p["kd"]
        db = dqe * p["qe"] - dke * p["ke"] - dkdkd
        dbl = _rowsum(dkdkd) + dgam * p["gam"]
        upper = (p["ri"] <= p["ci"]).astype(BF16)
        dlg = _tri_mm(upper, db) + dbl
        dz = jnp.where(p["real"], dlg * (1.0 / GLA_TAU) * _sigmoid(-p["z"]), 0.0)
        dzb = dz.astype(BF16)
        dgb_ref[...] += _rowsum(dz)
        dgup_ref[...] += _mm_tn(gd.astype(BF16), dzb)
        dgd_ref[...] = _mm_nt(dzb, gup_ref[...])
        dgn_ref[...] += dgn

    rowblk = lambda col: (lambda b, t: (b * nc + nc - 1 - t, col))
    const = lambda b, t: (0, 0)
    return pl.pallas_call(
        body, name="gla_bwd", grid=(bsz, nc),
        in_specs=[pl.BlockSpec((CHUNK, 2 * hk), rowblk(2)), pl.BlockSpec((CHUNK, hv), rowblk(3)),
                  pl.BlockSpec((CHUNK, hv), rowblk(4)), pl.BlockSpec((CHUNK, LANES), rowblk(20)),
                  pl.BlockSpec((CHUNK, hv), rowblk(1)),
                  pl.BlockSpec((None, GLA_HEADS, LANES, LANES), lambda b, t: (b * nc + nc - 1 - t, 0, 0, 0)),
                  pl.BlockSpec((LANES, 256), const), pl.BlockSpec((1, 256), const), pl.BlockSpec((1, GLA_DV), const),
                  pl.BlockSpec(memory_space=pl.ANY)],
        out_specs=[pl.BlockSpec((CHUNK, 2 * hk), rowblk(0)), pl.BlockSpec((CHUNK, hv), rowblk(0)),
                   pl.BlockSpec((CHUNK, hv), rowblk(0)), pl.BlockSpec((CHUNK, LANES), rowblk(0)),
                   pl.BlockSpec((1, GLA_DV), const), pl.BlockSpec((1, 256), const), pl.BlockSpec((LANES, 256), const)],
        out_shape=[_sds((r, 2 * hk), F32), _sds((r, hv), F32), _sds((r, hv), F32), _sds((r, LANES), F32),
                   _sds((1, GLA_DV), F32), _sds((1, 256), F32), _sds((LANES, 256), F32)],
        scratch_shapes=[pltpu.VMEM((GLA_HEADS, LANES, LANES), F32)],
        compiler_params=_params(("arbitrary", "arbitrary")),
    )(u, u, u, u, dmi, sta, gup, gb, gn, dep)


def _conv_bwd(u, c, dmi, w32, cg, cbe, tp, tc, dc):
    r = u.shape[0]
    hb = tc // CONV_HALO
    nhalo = r // CONV_HALO

    def dconv(cv, dco, cg_ref, cbe_ref):
        xhat, rstd = _ln(cv)
        cn = xhat * cg_ref[...] + cbe_ref[...]
        sg = _sigmoid(cn)
        dcn = dco * (sg * (1.0 + cn * (1.0 - sg)))
        return _ln_bwd(dcn * cg_ref[...], xhat, rstd), dcn, xhat

    def body(a_ref, g_ref, ah_ref, gh_ref, c_ref, dco_ref, ch_ref, dcoh_ref, w_ref, cg_ref, cbe_ref,
             du_ref, dw_ref, dcb_ref, dcg_ref, dcbe_ref, hs_ref, dcs_ref):
        t = pl.program_id(0)

        @pl.when(t == 0)
        def _():
            dw_ref[...] = jnp.zeros_like(dw_ref)
            dcb_ref[...] = jnp.zeros_like(dcb_ref)
            dcg_ref[...] = jnp.zeros_like(dcg_ref)
            dcbe_ref[...] = jnp.zeros_like(dcbe_ref)

        first = lax.rem(t * tc, tp) == 0
        last = lax.rem((t + 1) * tc, tp) == 0
        hh = ah_ref[...] * _sigmoid(gh_ref[...])
        hs_ref[0:CONV_HALO, :] = jnp.where(first, 0.0, hh)
        hs_ref[CONV_HALO:CONV_HALO + tc, :] = a_ref[...] * _sigmoid(g_ref[...])
        dch, _, _ = dconv(ch_ref[...], dcoh_ref[...], cg_ref, cbe_ref)
        dcs_ref[tc:tc + CONV_HALO, :] = jnp.where(last, 0.0, dch)

        def sub1(k, carry):
            r0 = pl.multiple_of(k * CONV_SUB, CONV_SUB)
            dcv, dcn, xhat = dconv(c_ref[pl.ds(r0, CONV_SUB), :], dco_ref[pl.ds(r0, CONV_SUB), :], cg_ref, cbe_ref)
            dcs_ref[pl.ds(r0, CONV_SUB), :] = dcv
            dcb_ref[...] += _rowsum(dcv)
            dcg_ref[...] += _rowsum(dcn * xhat)
            dcbe_ref[...] += _rowsum(dcn)
            return carry

        lax.fori_loop(0, tc // CONV_SUB, sub1, 0)

        def sub2(k, carry):
            r0 = pl.multiple_of(k * CONV_SUB, CONV_SUB)
            dwin = dcs_ref[pl.ds(r0, CONV_WIN), :]
            dh = _conv_taps(dwin, lambda o: w_ref[CONV_WIDTH - 1 - o:CONV_WIDTH - o, :], 0)
            av = a_ref[pl.ds(r0, CONV_SUB), :]
            sg = _sigmoid(g_ref[pl.ds(r0, CONV_SUB), :])
            du_ref[pl.ds(r0, CONV_SUB), 0:dc] = dh * sg
            du_ref[pl.ds(r0, CONV_SUB), dc:2 * dc] = dh * av * sg * (1.0 - sg)
            hwin = hs_ref[pl.ds(r0, CONV_WIN), :]
            dcv = dwin[0:CONV_SUB, :]
            for rho in range(8):
                offs = [o for o in range(2, 2 + CONV_WIDTH) if o % 8 == rho]
                rolled = hwin if rho == 0 else pltpu.roll(hwin, CONV_WIN - rho, 0)
                for o in offs:
                    m8 = o - rho
                    dw_ref[o - 2:o - 1, :] += _rowsum(dcv * rolled[m8:m8 + CONV_SUB, :])
            return carry

        lax.fori_loop(0, tc // CONV_SUB, sub2, 0)

    vec = pl.BlockSpec((1, dc), lambda t: (0, 0))
    prev = lambda col: (lambda t: (jnp.maximum(t * hb - 1, 0), col))
    nxt = lambda col: (lambda t: (jnp.minimum((t + 1) * hb, nhalo - 1), col))
    return pl.pallas_call(
        body, name="conv_bwd", grid=(r // tc,),
        in_specs=[pl.BlockSpec((tc, dc), lambda t: (t, 0)), pl.BlockSpec((tc, dc), lambda t: (t, 1)),
                  pl.BlockSpec((CONV_HALO, dc), prev(0)), pl.BlockSpec((CONV_HALO, dc), prev(1)),
                  pl.BlockSpec((tc, dc), lambda t: (t, 0)), pl.BlockSpec((tc, dc), lambda t: (t, 0)),
                  pl.BlockSpec((CONV_HALO, dc), nxt(0)), pl.BlockSpec((CONV_HALO, dc), nxt(0)),
                  pl.BlockSpec((32, dc), lambda t: (0, 0)), vec, vec],
        out_specs=[pl.BlockSpec((tc, 2 * dc), lambda t: (t, 0)), pl.BlockSpec((32, dc), lambda t: (0, 0)), vec, vec, vec],
        out_shape=[_sds((r, 2 * dc), F32), _sds((32, dc), F32), _sds((1, dc), F32), _sds((1, dc), F32), _sds((1, dc), F32)],
        scratch_shapes=[pltpu.VMEM((CONV_HALO + tc, dc), F32), pltpu.VMEM((tc + CONV_HALO, dc), F32)],
        compiler_params=_params(("arbitrary",)),
    )(u, u, u, u, c, dmi, c, dmi, w32, cg, cbe)


def _inproj_bwd(dp1, dus, xcat, g_in, w_in, dep, tp, tm):
    r, d = dp1.shape
    widths = [x.shape[1] for x in dus]
    offs = [sum(widths[:k]) for k in range(len(widths))]
    n = w_in.shape[1]
    nd = len(dus)
    tps = tp // tm

    def body(*refs):
        dp_ref = refs[0]
        du_refs = refs[1:1 + nd]
        x_ref, g_ref, w_ref, _, dx_ref, dmeta_ref, dg_ref, db_ref = refs[1 + nd:]
        i = pl.program_id(0)

        @pl.when(i == 0)
        def _():
            dmeta_ref[...] = jnp.zeros_like(dmeta_ref)
            dg_ref[...] = jnp.zeros_like(dg_ref)
            db_ref[...] = jnp.zeros_like(db_ref)

        ds0 = ALPHA * dp_ref[...]
        for k in range(nd):
            ds0 = ds0 + _mm_nt(du_refs[k][...].astype(BF16), w_ref[:, offs[k]:offs[k] + widths[k]])
        real = _row_in_seq(i, tm, tp) >= PAD_FRONT
        ds0 = jnp.where(real, ds0, 0.0)
        xhat, rstd = _ln(x_ref[...])
        dg_ref[...] += _rowsum(ds0 * xhat)
        db_ref[...] += _rowsum(ds0)
        dx = jnp.where(real, _ln_bwd(ds0 * g_ref[...], xhat, rstd), 0.0)
        dx_ref[...] = dx

        @pl.when(lax.rem(i, tps) == 0)
        def _():
            dmeta_ref[...] += dx[PAD_FRONT:X_OFF, :]

    row = lambda w: pl.BlockSpec((tm, w), lambda i: (i, 0))
    vec = pl.BlockSpec((1, d), lambda i: (0, 0))
    return pl.pallas_call(
        body, name="inproj_bwd", grid=(r // tm,),
        in_specs=[row(d)] + [row(w) for w in widths] + [row(d), vec, pl.BlockSpec((d, n), lambda i: (0, 0)),
                                                        pl.BlockSpec(memory_space=pl.ANY)],
        out_specs=[row(d), pl.BlockSpec((N_META, d), lambda i: (0, 0)), vec, vec],
        out_shape=[_sds((r, d), F32), _sds((N_META, d), F32), _sds((1, d), F32), _sds((1, d), F32)],
        compiler_params=_params(("arbitrary",)),
    )(dp1, *dus, xcat, g_in, w_in, dep)


def _inproj_bwd_w(s0, dus, tm):
    r, d = s0.shape
    widths = [x.shape[1] for x in dus]
    nd = len(dus)

    def body(*refs):
        s_ref = refs[0]
        du_refs = refs[1:1 + nd]
        dw_refs = refs[1 + nd:]
        i = pl.program_id(0)

        @pl.when(i == 0)
        def _():
            for k in range(nd):
                dw_refs[k][...] = jnp.zeros_like(dw_refs[k])

        sb = s_ref[...].astype(BF16)
        for k in range(nd):
            dw_refs[k][...] += _mm_tn(sb, du_refs[k][...].astype(BF16))

    row = lambda w: pl.BlockSpec((tm, w), lambda i: (i, 0))
    return pl.pallas_call(
        body, name="inproj_bwd_w", grid=(r // tm,),
        in_specs=[row(d)] + [row(w) for w in widths],
        out_specs=[pl.BlockSpec((d, w), lambda i: (0, 0)) for w in widths],
        out_shape=[_sds((d, w), F32) for w in widths],
        compiler_params=_params(("arbitrary",)),
    )(s0, *dus)


def _local_step(x, tgt, meta, ln_in_g, ln_in_b, w_in, conv_w, conv_b, conv_ln_g, conv_ln_b, gate_up, gate_bias,
                gla_norm_g, late_weights, ln1_g, ln1_b, ln2_g, ln2_b, push):
    bsz, seq, d = x.shape
    tp = X_OFF + seq
    assert tp % CHUNK == 0
    nc = tp // CHUNK
    r = bsz * tp
    dc = conv_b.shape[1]
    tm = _pick_tile(tp, (352, 128, 64))
    tc = _pick_tile(tp, (704, 128, 64))

    xcat = jnp.concatenate([jnp.zeros((bsz, PAD_FRONT, d), F32), jnp.broadcast_to(meta[None], (bsz, N_META, d)), x],
                           axis=1).reshape(r, d)
    tgt_p = jnp.pad(tgt, ((0, 0), (X_OFF, 0), (0, 0))).reshape(r, d)
    w32 = jnp.pad(conv_w, ((0, 32 - CONV_WIDTH), (0, 0)))
    gup = jnp.pad(gate_up, ((0, LANES - GLA_RANK), (0, 0))).astype(BF16)

    s0, u = _inproj_fwd(xcat, ln_in_g, ln_in_b, w_in, tp, tm)
    c, co = _conv_fwd(u, w32, conv_b, conv_ln_g, conv_ln_b, tp, tc, dc)
    go, sta = _gla_fwd(u, gup, gate_bias, gla_norm_g, bsz, nc)
    w_out, w1g, w2 = late_weights(go)
    nh = w1g.shape[0]
    p1, s1 = _outproj_fwd(s0, co, go, w_out, ln1_g, ln1_b, tm)
    hm, dp2, loss, dg2, db2 = _mlp_fwd(s1, w1g, w2, ln2_g, ln2_b, tgt_p, tp, tm)

    dh, dp1, dg1, db1 = _mlp_bwd_act(dp2, hm, w1g, w2, p1, ln1_g, tm)
    dw1, dw2 = _mlp_bwd_w(s1, hm, dh, dp2, nh, tm)
    tok = push("ff", (dw1, dw2))
    dmi, dwo = _outproj_bwd(dp1, co, go, w_out, tok, tm)
    tok = push("out", (dwo,))
    dqk, dv, dr, dgd, dgn, dgb, dgup = _gla_bwd(u, dmi, sta, gup, gate_bias, gla_norm_g, tok, bsz, nc)
    dcv, dcw, dcb, dcg, dcbe = _conv_bwd(u, c, dmi, w32, conv_ln_g, conv_ln_b, tp, tc, dc)
    dus = [dcv, dqk, dv, dr, dgd]
    dwi = _inproj_bwd_w(s0, dus, tm)
    tok = push("in", (jnp.concatenate(dwi, axis=1),))
    dxcat, dmeta, dgi, dbi = _inproj_bwd(dp1, dus, xcat, ln_in_g, w_in, tok, tp, tm)

    grad_x = dxcat.reshape(bsz, tp, d)[:, X_OFF:, :]
    return dict(loss=loss[0, 0], grad_x=grad_x, meta_tokens=dmeta, ln_in_g=dgi, ln_in_b=dbi,
                conv_w=dcw[:CONV_WIDTH], conv_b=dcb, conv_ln_g=dcg, conv_ln_b=dcbe,
                gate_up=dgup[:GLA_RANK], gate_bias=dgb, gla_norm_g=dgn, ln1_g=dg1, ln1_b=db1, ln2_g=dg2, ln2_b=db2)


def _exchange(arrays, scatter, name):
    na = len(arrays)
    npeer = N_DEV - 1

    def body(*refs):
        srcs = refs[:na]
        outs = refs[na:2 * na]
        send_sems, recv_sems, local_sems = refs[2 * na:]
        xi, yi, ci = (lax.axis_index(a) for a in MESH_AXES)
        me = 4 * xi + 2 * yi + ci
        copies = []
        for a in range(na):
            own = srcs[a].at[me] if scatter[a] else srcs[a]
            cp = pltpu.make_async_copy(own, outs[a].at[me], local_sems.at[a])
            cp.start()
            copies.append(cp)
        remote = []
        for k in range(1, N_DEV):
            px, py, pc = xi ^ (k >> 2), yi ^ ((k >> 1) & 1), ci ^ (k & 1)
            peer = 4 * px + 2 * py + pc
            for a in range(na):
                src = srcs[a].at[peer] if scatter[a] else srcs[a]
                cp = pltpu.make_async_remote_copy(
                    src_ref=src, dst_ref=outs[a].at[me],
                    send_sem=send_sems.at[a * npeer + k - 1], recv_sem=recv_sems.at[a * npeer + k - 1],
                    device_id=(px, py, pc), device_id_type=pl.DeviceIdType.MESH)
                cp.start()
                remote.append(cp)
        for cp in remote:
            cp.wait()
        for cp in copies:
            cp.wait()

    out_shape = [_sds(a.shape if scatter[i] else (N_DEV,) + a.shape, a.dtype) for i, a in enumerate(arrays)]
    anyspec = pl.BlockSpec(memory_space=pl.ANY)
    return pl.pallas_call(
        body, name=name,
        in_specs=[anyspec] * na, out_specs=[anyspec] * na, out_shape=out_shape,
        scratch_shapes=[pltpu.SemaphoreType.DMA((na * npeer,)), pltpu.SemaphoreType.DMA((na * npeer,)),
                        pltpu.SemaphoreType.DMA((na,))],
    )(*arrays)


def _peers(xi, yi, ci):
    for k in range(1, N_DEV):
        px, py, pc = xi ^ (k >> 2), yi ^ ((k >> 1) & 1), ci ^ (k & 1)
        yield (px, py, pc), 4 * px + 2 * py + pc


def _push_start(arrays, scatter, name, dep=None):
    na = len(arrays)
    shapes = [a.shape if scatter[i] else (N_DEV,) + a.shape for i, a in enumerate(arrays)]
    hbm = pl.BlockSpec(memory_space=pltpu.HBM)
    sem = pl.BlockSpec(memory_space=pltpu.SEMAPHORE)
    ndep = 0 if dep is None else 1

    def body(*refs):
        srcs = refs[:na]
        lands = refs[na:2 * na]
        send_sems, recv_sems = refs[2 * na + ndep:2 * na + ndep + 2]
        token = refs[4 * na + ndep + 2]
        own_sems = refs[4 * na + ndep + 3]
        xi, yi, ci = (lax.axis_index(a) for a in MESH_AXES)
        me = 4 * xi + 2 * yi + ci
        for a in range(na):
            for pos, peer in _peers(xi, yi, ci):
                pltpu.make_async_remote_copy(
                    src_ref=srcs[a].at[peer] if scatter[a] else srcs[a], dst_ref=lands[a].at[me],
                    send_sem=send_sems.at[a], recv_sem=recv_sems.at[a],
                    device_id=pos, device_id_type=pl.DeviceIdType.MESH).start()
        own = [pltpu.make_async_copy(srcs[a].at[me] if scatter[a] else srcs[a], lands[a].at[me], own_sems.at[a])
               for a in range(na)]
        for cp in own:
            cp.start()
        token[...] = jnp.zeros_like(token)
        for cp in own:
            cp.wait()

    ins = [pltpu.with_memory_space_constraint(a, pltpu.HBM) for a in arrays]
    ins += [pltpu.with_memory_space_constraint(lax.empty(s, a.dtype), pltpu.HBM) for s, a in zip(shapes, arrays)]
    res = pl.pallas_call(
        body, name=name,
        in_specs=[hbm] * (2 * na) + [pl.BlockSpec(memory_space=pl.ANY)] * ndep,
        out_specs=[sem, sem] + [hbm] * (2 * na) + [pl.BlockSpec(memory_space=pltpu.VMEM)],
        out_shape=[pltpu.SemaphoreType.DMA((na,)), pltpu.SemaphoreType.DMA((na,))]
                  + [pltpu.HBM(a.shape, a.dtype) for a in arrays] + [pltpu.HBM(s, a.dtype) for s, a in zip(shapes, arrays)]
                  + [_sds((8, LANES), F32)],
        input_output_aliases={i: 2 + i for i in range(2 * na)},
        scratch_shapes=[pltpu.SemaphoreType.DMA((na,))],
        compiler_params=pltpu.CompilerParams(has_side_effects=pltpu.SideEffectType.DATAFLOW_SIDE_EFFECTING),
    )(*ins, *([] if dep is None else [dep]))
    return (res[0], res[1], list(res[2:2 + na]), list(res[2 + na:2 + 2 * na])), res[-1]


def _push_wait(handle, after, name):
    send_sems, recv_sems, srcs, lands = handle
    na = len(srcs)
    hbm = pl.BlockSpec(memory_space=pltpu.HBM)
    sem = pl.BlockSpec(memory_space=pltpu.SEMAPHORE)

    def body(*refs):
        land_refs = refs[na:2 * na]
        send_ref, recv_ref = refs[2 * na:2 * na + 2]
        me = tuple(lax.axis_index(a) for a in MESH_AXES)
        for a in range(na):
            seven = land_refs[a].at[pl.ds(0, N_DEV - 1)]
            cp = pltpu.make_async_remote_copy(src_ref=seven, dst_ref=seven, send_sem=send_ref.at[a], recv_sem=recv_ref.at[a],
                                              device_id=me, device_id_type=pl.DeviceIdType.MESH)
            cp.wait_send()
            cp.wait_recv()

    res = pl.pallas_call(
        body, name=name,
        in_specs=[hbm] * (2 * na) + [sem, sem, pl.BlockSpec(memory_space=pl.ANY)],
        out_specs=[hbm] * (2 * na),
        out_shape=[pltpu.HBM(a.shape, a.dtype) for a in srcs] + [pltpu.HBM(a.shape, a.dtype) for a in lands],
        input_output_aliases={i: i for i in range(2 * na)},
        compiler_params=pltpu.CompilerParams(has_side_effects=pltpu.SideEffectType.DATAFLOW_SIDE_EFFECTING),
    )(*srcs, *lands, send_sems, recv_sems, after)
    return list(res[na:])


def _adamw(w, g, m, v):
    m = ADAM_B1 * m + (1.0 - ADAM_B1) * g
    v = ADAM_B2 * v + (1.0 - ADAM_B2) * jnp.square(g)
    m_hat = m / (1.0 - ADAM_B1 ** ADAM_STEP)
    v_hat = v / (1.0 - ADAM_B2 ** ADAM_STEP)
    delta = -ADAM_LR * (m_hat / (jnp.sqrt(v_hat) + ADAM_EPS) + ADAM_WD * w)
    return delta, m, v


def _sum_devices(ref):
    g = ref[0].astype(F32)
    for k in range(1, N_DEV):
        g = g + ref[k].astype(F32)
    return g


def _update_big(parts, w, m, v, name):
    rows, cols = w.shape
    tr = _pick_tile(rows, (128, 64, 16))

    def body(p_ref, w_ref, m_ref, v_ref, g_ref, d_ref, nm_ref, nv_ref):
        g = _sum_devices(p_ref)
        g_ref[...] = g
        d_ref[...], nm_ref[...], nv_ref[...] = _adamw(w_ref[...], g, m_ref[...], v_ref[...])

    blk = pl.BlockSpec((tr, cols), lambda i: (i, 0))
    return pl.pallas_call(
        body, name=name, grid=(rows // tr,),
        in_specs=[pl.BlockSpec((N_DEV, tr, cols), lambda i: (0, i, 0)), blk, blk, blk],
        out_specs=[blk] * 4, out_shape=[_sds((rows, cols), F32)] * 4,
        compiler_params=_params(("parallel",)),
    )(parts, w, m, v)


_VEC_ORDER = ("ln_in_g", "ln_in_b", "conv_b", "conv_ln_g", "conv_ln_b", "gate_bias", "gla_norm_g",
              "ln1_g", "ln1_b", "ln2_g", "ln2_b")
_SHARDED_SMALL = (("meta_tokens", 0, N_META, LANES), ("conv_w", N_META, CONV_WIDTH, None), ("gate_up", N_META + 32, GLA_RANK, None))


def _update_small(parts_sh, parts_vec, wmv):
    names = [s[0] for s in _SHARDED_SMALL] + list(_VEC_ORDER)
    flat = [a for nme in names for a in wmv[nme]]

    def body(*refs):
        sh_ref, vec_ref = refs[0], refs[1]
        ins = refs[2:2 + len(flat)]
        outs = refs[2 + len(flat):2 + len(flat) + 4 * len(names)]
        gsh_ref, gvec_ref = refs[-2:]
        gsh_ref[...] = _sum_devices(sh_ref)
        gvec_ref[...] = _sum_devices(vec_ref)
        for idx, nme in enumerate(names):
            w_ref, m_ref, v_ref = ins[3 * idx:3 * idx + 3]
            rows, cols = w_ref.shape
            if idx < len(_SHARDED_SMALL):
                r0 = _SHARDED_SMALL[idx][1]
                g = gsh_ref[r0:r0 + rows, 0:cols]
            else:
                j = idx - len(_SHARDED_SMALL)
                g = gvec_ref[j:j + 1, 0:cols]
            o = outs[4 * idx:4 * idx + 4]
            o[0][...] = g
            o[1][...], o[2][...], o[3][...] = _adamw(w_ref[...], g, m_ref[...], v_ref[...])

    out_shape = [_sds(wmv[nme][0].shape, F32) for nme in names for _ in range(4)]
    vmem = pl.BlockSpec(memory_space=pltpu.VMEM)
    res = pl.pallas_call(
        body, name="update_small", out_shape=out_shape,
        in_specs=[vmem] * (2 + len(flat)), out_specs=[vmem] * len(out_shape),
        scratch_shapes=[pltpu.VMEM(parts_sh.shape[1:], F32), pltpu.VMEM(parts_vec.shape[1:], F32)],
    )(parts_sh, parts_vec, *flat)
    return {nme: res[4 * i:4 * i + 4] for i, nme in enumerate(names)}


_WEIGHTS = ("meta_tokens", "ln_in_g", "ln_in_b", "w_in", "conv_w", "conv_b", "conv_ln_g", "conv_ln_b", "gate_up",
            "gate_bias", "gla_norm_g", "w_out", "ln1_g", "ln1_b", "w_ff1", "w_ff2", "ln2_g", "ln2_b")


def kernel(x, meta_tokens, ln_in_g, ln_in_b, w_in, conv_w, conv_b, conv_ln_g, conv_ln_b, gate_up, gate_bias, gla_norm_g, w_out, ln1_g, ln1_b, w_ff1, w_ff2, ln2_g, ln2_b, loss_target, m_meta_tokens, m_ln_in_g, m_ln_in_b, m_w_in, m_conv_w, m_conv_b, m_conv_ln_g, m_conv_ln_b, m_gate_up, m_gate_bias, m_gla_norm_g, m_w_out, m_ln1_g, m_ln1_b, m_w_ff1, m_w_ff2, m_ln2_g, m_ln2_b, v_meta_tokens, v_ln_in_g, v_ln_in_b, v_w_in, v_conv_w, v_conv_b, v_conv_ln_g, v_conv_ln_b, v_gate_up, v_gate_bias, v_gla_norm_g, v_w_out, v_ln1_g, v_ln1_b, v_w_ff1, v_w_ff2, v_ln2_g, v_ln2_b):
    w = dict(meta_tokens=meta_tokens, ln_in_g=ln_in_g, ln_in_b=ln_in_b, w_in=w_in, conv_w=conv_w, conv_b=conv_b,
             conv_ln_g=conv_ln_g, conv_ln_b=conv_ln_b, gate_up=gate_up, gate_bias=gate_bias, gla_norm_g=gla_norm_g,
             w_out=w_out, ln1_g=ln1_g, ln1_b=ln1_b, w_ff1=w_ff1, w_ff2=w_ff2, ln2_g=ln2_g, ln2_b=ln2_b)
    mom = dict(meta_tokens=m_meta_tokens, ln_in_g=m_ln_in_g, ln_in_b=m_ln_in_b, w_in=m_w_in, conv_w=m_conv_w,
               conv_b=m_conv_b, conv_ln_g=m_conv_ln_g, conv_ln_b=m_conv_ln_b, gate_up=m_gate_up, gate_bias=m_gate_bias,
               gla_norm_g=m_gla_norm_g, w_out=m_w_out, ln1_g=m_ln1_g, ln1_b=m_ln1_b, w_ff1=m_w_ff1, w_ff2=m_w_ff2,
               ln2_g=m_ln2_g, ln2_b=m_ln2_b)
    var = dict(meta_tokens=v_meta_tokens, ln_in_g=v_ln_in_g, ln_in_b=v_ln_in_b, w_in=v_w_in, conv_w=v_conv_w,
               conv_b=v_conv_b, conv_ln_g=v_conv_ln_g, conv_ln_b=v_conv_ln_b, gate_up=v_gate_up, gate_bias=v_gate_bias,
               gla_norm_g=v_gla_norm_g, w_out=v_w_out, ln1_g=v_ln1_g, ln1_b=v_ln1_b, w_ff1=v_w_ff1, w_ff2=v_w_ff2,
               ln2_g=v_ln2_g, ln2_b=v_ln2_b)
    shapes = {k: a.shape for k, a in w.items()}

    def two_d(a):
        return a.reshape(1, -1) if a.ndim == 1 else a.reshape(a.shape[-2:])

    w2d = {k: two_d(a) for k, a in w.items()}
    m2d = {k: two_d(a) for k, a in mom.items()}
    v2d = {k: two_d(a) for k, a in var.items()}
    d = x.shape[-1]
    d_in = w2d["w_in"].shape[1] * N_DEV
    d_in_p = -(-d_in // LANES) * LANES

    first, tok_first = _push_start(
        [w2d["w_in"].astype(BF16), w2d["meta_tokens"], w2d["conv_w"], w2d["gate_up"]], [False] * 4, "gather_first_start")
    late, tok_late = _push_start(
        [w2d["w_out"].astype(BF16), w2d["w_ff1"].astype(BF16), w2d["w_ff2"].astype(BF16)], [False] * 3,
        "gather_late_start", dep=tok_first)
    g_in, g_meta, g_conv, g_gup = _push_wait(first, tok_late, "gather_first_wait")
    w_in_full = jnp.pad(g_in.transpose(1, 0, 2).reshape(d, d_in), ((0, 0), (0, d_in_p - d_in)))
    meta_full = g_meta.transpose(1, 0, 2).reshape(N_META, d)
    conv_w_full = g_conv.transpose(1, 0, 2).reshape(CONV_WIDTH, -1)
    gate_up_full = g_gup.transpose(1, 0, 2).reshape(GLA_RANK, -1)

    def late_weights(after):
        g_out, g_ff1, g_ff2 = _push_wait(late, after, "gather_late_wait")
        return g_out.reshape(-1, d), g_ff1, g_ff2.reshape(-1, d)

    pushed = {}

    def push(tag, grads):
        if tag == "in":
            grads = (grads[0][:, :d_in].reshape(d, N_DEV, d_in // N_DEV).transpose(1, 0, 2).astype(BF16),)
        elif tag == "out":
            grads = (grads[0].reshape(N_DEV, -1, d),)
        pushed[tag], tok = _push_start(list(grads), [True] * len(grads), f"scatter_{tag}_start")
        return tok

    res = _local_step(x, loss_target, meta_full, w2d["ln_in_g"], w2d["ln_in_b"], w_in_full, conv_w_full, w2d["conv_b"],
                      w2d["conv_ln_g"], w2d["conv_ln_b"], gate_up_full, w2d["gate_bias"], w2d["gla_norm_g"], late_weights,
                      w2d["ln1_g"], w2d["ln1_b"], w2d["ln2_g"], w2d["ln2_b"], push)

    dc = res["conv_w"].shape[1]
    hk = res["gate_up"].shape[1]
    sh_meta = res["meta_tokens"].reshape(N_META, N_DEV, LANES).transpose(1, 0, 2)
    sh_conv = jnp.pad(res["conv_w"].reshape(CONV_WIDTH, N_DEV, dc // N_DEV).transpose(1, 0, 2),
                      ((0, 0), (0, 32 - CONV_WIDTH), (0, LANES - dc // N_DEV)))
    sh_gup = jnp.pad(res["gate_up"].reshape(GLA_RANK, N_DEV, hk // N_DEV).transpose(1, 0, 2),
                     ((0, 0), (0, 0), (0, LANES - hk // N_DEV)))
    p_sh = jnp.concatenate([sh_meta, sh_conv, sh_gup], axis=1)
    p_vec = jnp.concatenate([jnp.pad(res[k], ((0, 0), (0, d - res[k].shape[1]))) for k in _VEC_ORDER]
                            + [jnp.zeros((16 - len(_VEC_ORDER), d), F32)], axis=0)

    r_sh, r_vec = _exchange([p_sh, p_vec], [True, False], "scatter_small")
    r_ff1, r_ff2 = _push_wait(pushed["ff"], r_vec, "scatter_ff_wait")
    (r_out,) = _push_wait(pushed["out"], r_ff1, "scatter_out_wait")
    (r_in,) = _push_wait(pushed["in"], r_out, "scatter_in_wait")

    upd = {}
    upd["w_in"] = _update_big(r_in, w2d["w_in"], m2d["w_in"], v2d["w_in"], "update_w_in")
    upd["w_out"] = _update_big(r_out, w2d["w_out"], m2d["w_out"], v2d["w_out"], "update_w_out")
    upd["w_ff1"] = _update_big(r_ff1, w2d["w_ff1"], m2d["w_ff1"], v2d["w_ff1"], "update_w_ff1")
    upd["w_ff2"] = _update_big(r_ff2, w2d["w_ff2"], m2d["w_ff2"], v2d["w_ff2"], "update_w_ff2")
    small = [s[0] for s in _SHARDED_SMALL] + list(_VEC_ORDER)
    upd.update(_update_small(r_sh, r_vec, {k: (w2d[k], m2d[k], v2d[k]) for k in small}))

    loss = lax.psum(res["loss"], MESH_AXES)
    outs = [loss, res["grad_x"]]
    for j in range(4):
        outs += [upd[k][j].reshape(shapes[k]) for k in _WEIGHTS]
    return tuple(outs)
```

```python
import functools

import jax
import jax.numpy as jnp
from jax import lax
from jax.experimental import pallas as pl
from jax.experimental.pallas import tpu as pltpu

F32 = jnp.float32
BF16 = jnp.bfloat16

N_META = 16
CHUNK = 64
PAD_FRONT = (-N_META) % CHUNK
X_OFF = PAD_FRONT + N_META
CONV_WIDTH = 31
CONV_HALO = 32
CONV_SUB = 64
CONV_WIN = CONV_SUB + CONV_HALO
GLA_HEADS = 4
GLA_DK = 64
GLA_DV = 128
GLA_RANK = 16
GLA_TAU = 16.0
QK_SCALE = GLA_DK ** -0.5
LN_EPS = 1e-5
ALPHA = 2.0 ** 0.25
LANES = 128
N_DEV = 8
ADAM_LR = 0.001
ADAM_B1 = 0.9
ADAM_B2 = 0.999
ADAM_EPS = 1e-08
ADAM_WD = 0.01
ADAM_STEP = 10
VMEM_LIMIT = 56 * 1024 * 1024
MESH_AXES = ("x", "y", "c")


def _sds(shape, dtype):
    return jax.ShapeDtypeStruct(shape, dtype)


def _mm(a, b):
    return jnp.dot(a, b, preferred_element_type=F32)


def _mm_nt(a, b):
    return lax.dot_general(a, b, (((1,), (1,)), ((), ())), preferred_element_type=F32)


def _mm_tn(a, b):
    return lax.dot_general(a, b, (((0,), (0,)), ((), ())), preferred_element_type=F32)


def _sigmoid(x):
    return 1.0 / (1.0 + jnp.exp(-x))


def _log_sigmoid(z):
    return jnp.minimum(z, 0.0) - jnp.log(1.0 + jnp.exp(-jnp.abs(z)))


def _ln(x):
    mu = jnp.mean(x, axis=-1, keepdims=True)
    xc = x - mu
    var = jnp.mean(xc * xc, axis=-1, keepdims=True)
    rstd = lax.rsqrt(var + LN_EPS)
    return xc * rstd, rstd


def _ln_bwd(dyg, xhat, rstd):
    m1 = jnp.mean(dyg, axis=-1, keepdims=True)
    m2 = jnp.mean(dyg * xhat, axis=-1, keepdims=True)
    return rstd * (dyg - m1 - xhat * m2)


def _rowsum(x):
    return jnp.sum(x, axis=0, keepdims=True)


def _row_in_seq(i, tm, tp):
    base = lax.rem(i * tm, tp)
    return base + lax.broadcasted_iota(jnp.int32, (tm, 1), 0)


def _split3(x):
    hi = x.astype(BF16)
    r1 = x - hi.astype(F32)
    mid = r1.astype(BF16)
    lo = (r1 - mid.astype(F32)).astype(BF16)
    return hi, mid, lo


def _tri_mm(tri, x):
    hi, mid, lo = _split3(x)
    return _mm(tri, hi) + _mm(tri, mid) + _mm(tri, lo)


def _params(sem):
    return pltpu.CompilerParams(dimension_semantics=sem, vmem_limit_bytes=VMEM_LIMIT)


def _pick_tile(n, prefs):
    for t in prefs:
        if n % t == 0:
            return t
    raise ValueError(f"no tile for {n}")


def _inproj_fwd(xcat, g, b, w_in, tp, tm):
    r, d = xcat.shape
    n = w_in.shape[1]

    def body(x_ref, g_ref, b_ref, w_ref, s0_ref, u_ref):
        i = pl.program_id(0)
        xhat, _ = _ln(x_ref[...])
        real = _row_in_seq(i, tm, tp) >= PAD_FRONT
        s = jnp.where(real, xhat * g_ref[...] + b_ref[...], 0.0)
        s0_ref[...] = s
        u_ref[...] = _mm(s.astype(BF16), w_ref[...])

    return pl.pallas_call(
        body, name="inproj_fwd", grid=(r // tm,),
        in_specs=[pl.BlockSpec((tm, d), lambda i: (i, 0)), pl.BlockSpec((1, d), lambda i: (0, 0)),
                  pl.BlockSpec((1, d), lambda i: (0, 0)), pl.BlockSpec((d, n), lambda i: (0, 0))],
        out_specs=[pl.BlockSpec((tm, d), lambda i: (i, 0)), pl.BlockSpec((tm, n), lambda i: (i, 0))],
        out_shape=[_sds((r, d), F32), _sds((r, n), F32)],
        compiler_params=_params(("parallel",)),
    )(xcat, g, b, w_in)


def _conv_taps(win, coef, lo):
    acc = None
    for rho in range(8):
        offs = [o for o in range(lo, lo + CONV_WIDTH) if o % 8 == rho]
        if not offs:
            continue
        rolled = win if rho == 0 else pltpu.roll(win, CONV_WIN - rho, 0)
        for o in offs:
            m8 = o - rho
            term = rolled[m8:m8 + CONV_SUB, :] * coef(o)
            acc = term if acc is None else acc + term
    return acc


def _conv_fwd(u, w32, cb, cg, cbe, tp, tc, dc):
    r = u.shape[0]
    hb = tc // CONV_HALO

    def body(a_ref, g_ref, ah_ref, gh_ref, w_ref, cb_ref, cg_ref, cbe_ref, c_ref, co_ref, hs_ref):
        t = pl.program_id(0)
        first = lax.rem(t * tc, tp) == 0
        hh = ah_ref[...] * _sigmoid(gh_ref[...])
        hs_ref[0:CONV_HALO, :] = jnp.where(first, 0.0, hh)
        hs_ref[CONV_HALO:CONV_HALO + tc, :] = a_ref[...] * _sigmoid(g_ref[...])

        def sub(k, carry):
            r0 = pl.multiple_of(k * CONV_SUB, CONV_SUB)
            win = hs_ref[pl.ds(r0, CONV_WIN), :]
            c = _conv_taps(win, lambda o: w_ref[o - 2:o - 1, :], 2) + cb_ref[...]
            c_ref[pl.ds(r0, CONV_SUB), :] = c
            xhat, _ = _ln(c)
            cn = xhat * cg_ref[...] + cbe_ref[...]
            co_ref[pl.ds(r0, CONV_SUB), :] = (cn * _sigmoid(cn)).astype(BF16)
            return carry

        lax.fori_loop(0, tc // CONV_SUB, sub, 0)

    vec = pl.BlockSpec((1, dc), lambda t: (0, 0))
    return pl.pallas_call(
        body, name="conv_fwd", grid=(r // tc,),
        in_specs=[pl.BlockSpec((tc, dc), lambda t: (t, 0)), pl.BlockSpec((tc, dc), lambda t: (t, 1)),
                  pl.BlockSpec((CONV_HALO, dc), lambda t: (jnp.maximum(t * hb - 1, 0), 0)),
                  pl.BlockSpec((CONV_HALO, dc), lambda t: (jnp.maximum(t * hb - 1, 0), 1)),
                  pl.BlockSpec((32, dc), lambda t: (0, 0)), vec, vec, vec],
        out_specs=[pl.BlockSpec((tc, dc), lambda t: (t, 0)), pl.BlockSpec((tc, dc), lambda t: (t, 0))],
        out_shape=[_sds((r, dc), F32), _sds((r, dc), BF16)],
        scratch_shapes=[pltpu.VMEM((CONV_HALO + tc, dc), F32)],
        compiler_params=_params(("parallel",)),
    )(u, u, u, u, w32, cb, cg, cbe)


def _gla_prep(qk, gd, gup, gb, n):
    z = _mm(gd.astype(BF16), gup) + gb
    lg = _log_sigmoid(z) * (1.0 / GLA_TAU)
    row = n * CHUNK + lax.broadcasted_iota(jnp.int32, (CHUNK, 1), 0)
    real = row >= PAD_FRONT
    lg = jnp.where(real, lg, 0.0)
    ri = lax.broadcasted_iota(jnp.int32, (CHUNK, CHUNK), 0)
    ci = lax.broadcasted_iota(jnp.int32, (CHUNK, CHUNK), 1)
    low = (ri >= ci).astype(BF16)
    b = _tri_mm(low, lg)
    bl = _rowsum(lg)
    hk = GLA_HEADS * GLA_DK
    q = qk[:, :hk] * QK_SCALE
    k = qk[:, hk:]
    eb = jnp.exp(b)
    enb = jnp.exp(-b)
    ebl = jnp.exp(bl - b)
    gam = jnp.exp(bl)
    return dict(z=z, real=real, ri=ri, ci=ci, eb=eb, enb=enb, ebl=ebl, gam=gam, k=k,
                qe=q * eb, ke=k * enb, kd=k * ebl)


def _head_mask(h2):
    lane = lax.broadcasted_iota(jnp.int32, (1, LANES), 1)
    return (lane < GLA_DK) if h2 == 0 else (lane >= GLA_DK)


def _gla_fwd(u, gup, gb, gn, bsz, nc):
    r = u.shape[0]
    hv = GLA_HEADS * GLA_DV

    def body(qk_ref, v_ref, r_ref, gd_ref, gup_ref, gb_ref, gn_ref, go_ref, sta_ref, st_ref):
        n = pl.program_id(1)

        @pl.when(n == 0)
        def _():
            st_ref[...] = jnp.zeros_like(st_ref)

        p = _gla_prep(qk_ref[...], gd_ref[...], gup_ref[...], gb_ref[...], n)
        tril = p["ri"] >= p["ci"]
        for h in range(GLA_HEADS):
            hp, h2 = divmod(h, 2)
            ls = slice(hp * LANES, (hp + 1) * LANES)
            m = _head_mask(h2)
            qeh = jnp.where(m, p["qe"][:, ls], 0.0).astype(BF16)
            kdh = jnp.where(m, p["kd"][:, ls], 0.0).astype(BF16)
            keh = p["ke"][:, ls].astype(BF16)
            vh = v_ref[:, h * GLA_DV:(h + 1) * GLA_DV].astype(BF16)
            st = st_ref[h]
            sta_ref[h] = st
            a = jnp.where(tril, _mm_nt(qeh, keh), 0.0)
            o = _mm(a.astype(BF16), vh) + _mm_nt(qeh, st.astype(BF16))
            st_ref[h] = st * p["gam"][:, ls] + _mm_tn(vh, kdh)
            rs = lax.rsqrt(jnp.mean(o * o, axis=-1, keepdims=True) + LN_EPS)
            rr = r_ref[:, h * GLA_DV:(h + 1) * GLA_DV]
            go = o * rs * gn_ref[...] * (rr * _sigmoid(rr))
            go_ref[:, h * GLA_DV:(h + 1) * GLA_DV] = go.astype(BF16)

    rowblk = lambda col: (lambda b, n: (b * nc + n, col))
    const = lambda b, n: (0, 0)
    return pl.pallas_call(
        body, name="gla_fwd", grid=(bsz, nc),
        in_specs=[pl.BlockSpec((CHUNK, 512), rowblk(2)), pl.BlockSpec((CHUNK, hv), rowblk(3)),
                  pl.BlockSpec((CHUNK, hv), rowblk(4)), pl.BlockSpec((CHUNK, LANES), rowblk(20)),
                  pl.BlockSpec((LANES, 256), const), pl.BlockSpec((1, 256), const), pl.BlockSpec((1, GLA_DV), const)],
        out_specs=[pl.BlockSpec((CHUNK, hv), rowblk(0)),
                   pl.BlockSpec((None, GLA_HEADS, LANES, LANES), lambda b, n: (b * nc + n, 0, 0, 0))],
        out_shape=[_sds((r, hv), BF16), _sds((bsz * nc, GLA_HEADS, LANES, LANES), F32)],
        scratch_shapes=[pltpu.VMEM((GLA_HEADS, LANES, LANES), F32)],
        compiler_params=_params(("parallel", "arbitrary")),
    )(u, u, u, u, gup, gb, gn)


def _outproj_fwd(s0, co, go, w_out, g1, b1, tm):
    r, d = s0.shape
    dc = co.shape[1]

    def body(s0_ref, co_ref, go_ref, w_ref, g_ref, b_ref, p1_ref, s1_ref, s1b_ref):
        mix = _mm(co_ref[...], w_ref[0:dc, :]) + _mm(go_ref[...], w_ref[dc:2 * dc, :])
        p1 = ALPHA * s0_ref[...] + mix
        p1_ref[...] = p1
        xhat, _ = _ln(p1)
        s1 = xhat * g_ref[...] + b_ref[...]
        s1_ref[...] = s1
        s1b_ref[...] = s1.astype(BF16)

    row = lambda w: pl.BlockSpec((tm, w), lambda i: (i, 0))
    vec = pl.BlockSpec((1, d), lambda i: (0, 0))
    return pl.pallas_call(
        body, name="outproj_fwd", grid=(r // tm,),
        in_specs=[row(d), row(dc), row(dc), pl.BlockSpec((2 * dc, d), lambda i: (0, 0)), vec, vec],
        out_specs=[row(d), row(d), row(d)],
        out_shape=[_sds((r, d), F32), _sds((r, d), F32), _sds((r, d), BF16)],
        compiler_params=_params(("parallel",)),
    )(s0, co, go, w_out, g1, b1)


def _mlp_fwd(s1, s1b, w1g, w2, g2, b2, tgt, tp, tm, ns):
    r, d = s1.shape
    nh, _, th = w1g.shape
    nj = nh // ns

    def body(s1_ref, sb_ref, w1_ref, w2_ref, g_ref, b_ref, t_ref, hm_ref, dp2_ref, dpb_ref, loss_ref, dg_ref, db_ref, acc_ref):
        i = pl.program_id(0)
        j = pl.program_id(1)

        @pl.when(jnp.logical_and(i == 0, j == 0))
        def _():
            loss_ref[...] = jnp.zeros_like(loss_ref)
            dg_ref[...] = jnp.zeros_like(dg_ref)
            db_ref[...] = jnp.zeros_like(db_ref)

        @pl.when(j == 0)
        def _():
            acc_ref[...] = jnp.zeros_like(acc_ref)

        for s in range(ns):
            h = _mm(sb_ref[...], w1_ref[s])
            hm_ref[:, s * th:(s + 1) * th] = h.astype(BF16)
            act = jnp.square(jnp.maximum(h, 0.0))
            acc_ref[...] += _mm(act.astype(BF16), w2_ref[s * th:(s + 1) * th, :])

        @pl.when(j == nj - 1)
        def _():
            p2 = ALPHA * s1_ref[...] + acc_ref[...]
            xhat, rstd = _ln(p2)
            s2 = xhat * g_ref[...] + b_ref[...]
            isx = _row_in_seq(i, tm, tp) >= X_OFF
            err = jnp.where(isx, s2 - t_ref[...], 0.0)
            loss_ref[...] += 0.5 * jnp.sum(jnp.mean(err * err, axis=-1, keepdims=True))
            dy = err * (1.0 / d)
            dg_ref[...] += _rowsum(dy * xhat)
            db_ref[...] += _rowsum(dy)
            dp2 = _ln_bwd(dy * g_ref[...], xhat, rstd)
            dp2_ref[...] = dp2
            dpb_ref[...] = dp2.astype(BF16)

    row = pl.BlockSpec((tm, d), lambda i, j: (i, 0))
    vec = pl.BlockSpec((1, d), lambda i, j: (0, 0))
    return pl.pallas_call(
        body, name="mlp_fwd", grid=(r // tm, nj),
        in_specs=[row, row, pl.BlockSpec((ns, d, th), lambda i, j: (j, 0, 0)), pl.BlockSpec((ns * th, d), lambda i, j: (j, 0)),
                  vec, vec, row],
        out_specs=[pl.BlockSpec((tm, ns * th), lambda i, j: (i, j)), row, row,
                   pl.BlockSpec((8, LANES), lambda i, j: (0, 0)), vec, vec],
        out_shape=[_sds((r, nh * th), BF16), _sds((r, d), F32), _sds((r, d), BF16), _sds((8, LANES), F32),
                   _sds((1, d), F32), _sds((1, d), F32)],
        scratch_shapes=[pltpu.VMEM((tm, d), F32)],
        compiler_params=_params(("arbitrary", "arbitrary")),
    )(s1, s1b, w1g, w2, g2, b2, tgt)


def _mlp_bwd_act(dp2, dpb, hm, w1g, w2, p1, g1, tm, ns):
    r, d = dp2.shape
    nh, _, th = w1g.shape
    nj = nh // ns

    def body(dp2_ref, dpb_ref, hm_ref, w1_ref, w2_ref, p1_ref, g_ref, dh_ref, dp1_ref, dg_ref, db_ref, acc_ref):
        i = pl.program_id(0)
        j = pl.program_id(1)

        @pl.when(jnp.logical_and(i == 0, j == 0))
        def _():
            dg_ref[...] = jnp.zeros_like(dg_ref)
            db_ref[...] = jnp.zeros_like(db_ref)

        @pl.when(j == 0)
        def _():
            acc_ref[...] = jnp.zeros_like(acc_ref)

        for s in range(ns):
            cols = slice(s * th, (s + 1) * th)
            dact = _mm_nt(dpb_ref[...], w2_ref[cols, :])
            dh = (dact * (2.0 * jnp.maximum(hm_ref[:, cols].astype(F32), 0.0))).astype(BF16)
            dh_ref[:, cols] = dh
            acc_ref[...] += _mm_nt(dh, w1_ref[s])

        @pl.when(j == nj - 1)
        def _():
            ds1 = ALPHA * dp2_ref[...] + acc_ref[...]
            xhat, rstd = _ln(p1_ref[...])
            dg_ref[...] += _rowsum(ds1 * xhat)
            db_ref[...] += _rowsum(ds1)
            dp1_ref[...] = _ln_bwd(ds1 * g_ref[...], xhat, rstd)

    row = pl.BlockSpec((tm, d), lambda i, j: (i, 0))
    vec = pl.BlockSpec((1, d), lambda i, j: (0, 0))
    blk = pl.BlockSpec((tm, ns * th), lambda i, j: (i, j))
    return pl.pallas_call(
        body, name="mlp_bwd_act", grid=(r // tm, nj),
        in_specs=[row, row, blk, pl.BlockSpec((ns, d, th), lambda i, j: (j, 0, 0)),
                  pl.BlockSpec((ns * th, d), lambda i, j: (j, 0)), row, vec],
        out_specs=[blk, row, vec, vec],
        out_shape=[_sds((r, nh * th), BF16), _sds((r, d), F32), _sds((1, d), F32), _sds((1, d), F32)],
        scratch_shapes=[pltpu.VMEM((tm, d), F32)],
        compiler_params=_params(("arbitrary", "arbitrary")),
    )(dp2, dpb, hm, w1g, w2, p1, g1)


def _mlp_bwd_w(s1b, hm, dh, dpb, nh, tm, ns):
    r, d = s1b.shape
    th = hm.shape[1] // nh

    def body(s1_ref, hm_ref, dh_ref, dp2_ref, dw1_ref, dw2_ref, a1_ref, a2_ref):
        i = pl.program_id(1)

        @pl.when(i == 0)
        def _():
            a1_ref[...] = jnp.zeros_like(a1_ref)
            a2_ref[...] = jnp.zeros_like(a2_ref)

        for s in range(ns):
            cols = slice(s * th, (s + 1) * th)
            act = jnp.square(jnp.maximum(hm_ref[:, cols].astype(F32), 0.0)).astype(BF16)
            a1_ref[s] += _mm_tn(s1_ref[...], dh_ref[:, cols])
            a2_ref[s] += _mm_tn(act, dp2_ref[...])

        @pl.when(i == pl.num_programs(1) - 1)
        def _():
            dw1_ref[...] = a1_ref[...].astype(BF16)
            dw2_ref[...] = a2_ref[...].astype(BF16)

    row = pl.BlockSpec((tm, d), lambda j, i: (i, 0))
    blk = pl.BlockSpec((tm, ns * th), lambda j, i: (i, j))
    return pl.pallas_call(
        body, name="mlp_bwd_w", grid=(nh // ns, r // tm),
        in_specs=[row, blk, blk, row],
        out_specs=[pl.BlockSpec((ns, d, th), lambda j, i: (j, 0, 0)), pl.BlockSpec((ns, th, d), lambda j, i: (j, 0, 0))],
        out_shape=[_sds((nh, d, th), BF16), _sds((nh, th, d), BF16)],
        scratch_shapes=[pltpu.VMEM((ns, d, th), F32), pltpu.VMEM((ns, th, d), F32)],
        compiler_params=_params(("parallel", "arbitrary")),
    )(s1b, hm, dh, dpb)


def _outproj_bwd(dp1, co, go, w_out, dep, tm):
    r, d = dp1.shape
    dc = co.shape[1]

    def body(dp_ref, co_ref, go_ref, w_ref, dep_ref, dmi_ref, dw_ref, acc_ref):
        i = pl.program_id(0)

        @pl.when(i == 0)
        def _():
            acc_ref[...] = jnp.zeros_like(acc_ref)

        dpb = dp_ref[...].astype(BF16)
        dmi_ref[...] = _mm_nt(dpb, w_ref[...])
        acc_ref[0:dc, :] += _mm_tn(co_ref[...], dpb)
        acc_ref[dc:2 * dc, :] += _mm_tn(go_ref[...], dpb)

        @pl.when(i == pl.num_programs(0) - 1)
        def _():
            dw_ref[...] = acc_ref[...].astype(BF16)

    row = lambda w: pl.BlockSpec((tm, w), lambda i: (i, 0))
    full = pl.BlockSpec((2 * dc, d), lambda i: (0, 0))
    return pl.pallas_call(
        body, name="outproj_bwd", grid=(r // tm,),
        in_specs=[row(d), row(dc), row(dc), full, pl.BlockSpec(memory_space=pl.ANY)],
        out_specs=[row(2 * dc), full],
        out_shape=[_sds((r, 2 * dc), F32), _sds((2 * dc, d), BF16)],
        scratch_shapes=[pltpu.VMEM((2 * dc, d), F32)],
        compiler_params=_params(("arbitrary",)),
    )(dp1, co, go, w_out, dep)


def _gla_bwd(u, dmi, sta, gup, gb, gn, dep, bsz, nc):
    r = u.shape[0]
    hv = GLA_HEADS * GLA_DV
    hk = GLA_HEADS * GLA_DK

    def body(qk_ref, v_ref, r_ref, gd_ref, dgo_ref, sta_ref, gup_ref, gb_ref, gn_ref, dep_ref,
             dqk_ref, dv_ref, dr_ref, dgd_ref, dgn_ref, dgb_ref, dgup_ref, dst_ref):
        bi = pl.program_id(0)
        t = pl.program_id(1)
        n = nc - 1 - t

        @pl.when(jnp.logical_and(bi == 0, t == 0))
        def _():
            dgn_ref[...] = jnp.zeros_like(dgn_ref)
            dgb_ref[...] = jnp.zeros_like(dgb_ref)
            dgup_ref[...] = jnp.zeros_like(dgup_ref)

        @pl.when(t == 0)
        def _():
            dst_ref[...] = jnp.zeros_like(dst_ref)

        gd = gd_ref[...]
        p = _gla_prep(qk_ref[...], gd, gup_ref[...], gb_ref[...], n)
        tril = p["ri"] >= p["ci"]
        dgn = jnp.zeros((1, GLA_DV), F32)
        dqe_t, dke_t, dkd_t, dgam_t = [], [], [], []
        for hp in range(GLA_HEADS // 2):
            ls = slice(hp * LANES, (hp + 1) * LANES)
            dqe = jnp.zeros((CHUNK, LANES), F32)
            dke = jnp.zeros((CHUNK, LANES), F32)
            dkd = jnp.zeros((CHUNK, LANES), F32)
            dgam = jnp.zeros((1, LANES), F32)
            for h2 in range(2):
                h = 2 * hp + h2
                vs = slice(h * GLA_DV, (h + 1) * GLA_DV)
                m = _head_mask(h2)
                qeh = jnp.where(m, p["qe"][:, ls], 0.0).astype(BF16)
                kdh = jnp.where(m, p["kd"][:, ls], 0.0).astype(BF16)
                keh = p["ke"][:, ls].astype(BF16)
                vh = v_ref[:, vs].astype(BF16)
                st = sta_ref[h]
                stb = st.astype(BF16)
                a = jnp.where(tril, _mm_nt(qeh, keh), 0.0)
                ab = a.astype(BF16)
                o = _mm(ab, vh) + _mm_nt(qeh, stb)
                rr = r_ref[:, vs]
                sr = _sigmoid(rr)
                rs = lax.rsqrt(jnp.mean(o * o, axis=-1, keepdims=True) + LN_EPS)
                y = o * rs
                dgo = dgo_ref[:, vs]
                don = dgo * (rr * sr)
                dr_ref[:, vs] = dgo * (y * gn_ref[...]) * (sr * (1.0 + rr * (1.0 - sr)))
                dgn = dgn + _rowsum(don * y)
                dxn = don * gn_ref[...]
                do = rs * (dxn - y * jnp.mean(dxn * y, axis=-1, keepdims=True))
                dob = do.astype(BF16)
                dst = dst_ref[h]
                dstb = dst.astype(BF16)
                da = jnp.where(tril, _mm_nt(dob, vh), 0.0).astype(BF16)
                dv_ref[:, vs] = _mm_tn(ab, dob) + _mm_nt(kdh, dstb)
                dqe = dqe + jnp.where(m, _mm(da, keh) + _mm(dob, stb), 0.0)
                dke = dke + _mm_tn(da, qeh)
                dkd = dkd + jnp.where(m, _mm(vh, dstb), 0.0)
                dgam = dgam + _rowsum(dst * st)
                dst_ref[h] = dst * p["gam"][:, ls] + _mm_tn(dob, qeh)
            dqe_t.append(dqe)
            dke_t.append(dke)
            dkd_t.append(dkd)
            dgam_t.append(dgam)
        dqe = jnp.concatenate(dqe_t, axis=1)
        dke = jnp.concatenate(dke_t, axis=1)
        dkd = jnp.concatenate(dkd_t, axis=1)
        dgam = jnp.concatenate(dgam_t, axis=1)
        dqk_ref[:, :hk] = dqe * p["eb"] * QK_SCALE
        dqk_ref[:, hk:] = dke * p["enb"] + dkd * p["ebl"]
        dkdkd = dkd * p["kd"]
        db = dqe * p["qe"] - dke * p["ke"] - dkdkd
        dbl = _rowsum(dkdkd) + dgam * p["gam"]
        upper = (p["ri"] <= p["ci"]).astype(BF16)
        dlg = _tri_mm(upper, db) + dbl
        dz = jnp.where(p["real"], dlg * (1.0 / GLA_TAU) * _sigmoid(-p["z"]), 0.0)
        dzb = dz.astype(BF16)
        dgb_ref[...] += _rowsum(dz)
        dgup_ref[...] += _mm_tn(gd.astype(BF16), dzb)
        dgd_ref[...] = _mm_nt(dzb, gup_ref[...])
        dgn_ref[...] += dgn

    rowblk = lambda col: (lambda b, t: (b * nc + nc - 1 - t, col))
    const = lambda b, t: (0, 0)
    return pl.pallas_call(
        body, name="gla_bwd", grid=(bsz, nc),
        in_specs=[pl.BlockSpec((CHUNK, 2 * hk), rowblk(2)), pl.BlockSpec((CHUNK, hv), rowblk(3)),
                  pl.BlockSpec((CHUNK, hv), rowblk(4)), pl.BlockSpec((CHUNK, LANES), rowblk(20)),
                  pl.BlockSpec((CHUNK, hv), rowblk(1)),
                  pl.BlockSpec((None, GLA_HEADS, LANES, LANES), lambda b, t: (b * nc + nc - 1 - t, 0, 0, 0)),
                  pl.BlockSpec((LANES, 256), const), pl.BlockSpec((1, 256), const), pl.BlockSpec((1, GLA_DV), const),
                  pl.BlockSpec(memory_space=pl.ANY)],
        out_specs=[pl.BlockSpec((CHUNK, 2 * hk), rowblk(0)), pl.BlockSpec((CHUNK, hv), rowblk(0)),
                   pl.BlockSpec((CHUNK, hv), rowblk(0)), pl.BlockSpec((CHUNK, LANES), rowblk(0)),
                   pl.BlockSpec((1, GLA_DV), const), pl.BlockSpec((1, 256), const), pl.BlockSpec((LANES, 256), const)],
        out_shape=[_sds((r, 2 * hk), F32), _sds((r, hv), F32), _sds((r, hv), F32), _sds((r, LANES), F32),
                   _sds((1, GLA_DV), F32), _sds((1, 256), F32), _sds((LANES, 256), F32)],
        scratch_shapes=[pltpu.VMEM((GLA_HEADS, LANES, LANES), F32)],
        compiler_params=_params(("arbitrary", "arbitrary")),
    )(u, u, u, u, dmi, sta, gup, gb, gn, dep)


def _conv_bwd(u, c, dmi, w32, cg, cbe, tp, tc, dc):
    r = u.shape[0]
    hb = tc // CONV_HALO
    nhalo = r // CONV_HALO

    def dconv(cv, dco, cg_ref, cbe_ref):
        xhat, rstd = _ln(cv)
        cn = xhat * cg_ref[...] + cbe_ref[...]
        sg = _sigmoid(cn)
        dcn = dco * (sg * (1.0 + cn * (1.0 - sg)))
        return _ln_bwd(dcn * cg_ref[...], xhat, rstd), dcn, xhat

    def body(a_ref, g_ref, ah_ref, gh_ref, c_ref, dco_ref, ch_ref, dcoh_ref, w_ref, cg_ref, cbe_ref,
             du_ref, dw_ref, dcb_ref, dcg_ref, dcbe_ref, hs_ref, dcs_ref):
        t = pl.program_id(0)

        @pl.when(t == 0)
        def _():
            dw_ref[...] = jnp.zeros_like(dw_ref)
            dcb_ref[...] = jnp.zeros_like(dcb_ref)
            dcg_ref[...] = jnp.zeros_like(dcg_ref)
            dcbe_ref[...] = jnp.zeros_like(dcbe_ref)

        first = lax.rem(t * tc, tp) == 0
        last = lax.rem((t + 1) * tc, tp) == 0
        hh = ah_ref[...] * _sigmoid(gh_ref[...])
        hs_ref[0:CONV_HALO, :] = jnp.where(first, 0.0, hh)
        hs_ref[CONV_HALO:CONV_HALO + tc, :] = a_ref[...] * _sigmoid(g_ref[...])
        dch, _, _ = dconv(ch_ref[...], dcoh_ref[...], cg_ref, cbe_ref)
        dcs_ref[tc:tc + CONV_HALO, :] = jnp.where(last, 0.0, dch)

        def sub1(k, carry):
            r0 = pl.multiple_of(k * CONV_SUB, CONV_SUB)
            dcv, dcn, xhat = dconv(c_ref[pl.ds(r0, CONV_SUB), :], dco_ref[pl.ds(r0, CONV_SUB), :], cg_ref, cbe_ref)
            dcs_ref[pl.ds(r0, CONV_SUB), :] = dcv
            dcb_ref[...] += _rowsum(dcv)
            dcg_ref[...] += _rowsum(dcn * xhat)
            dcbe_ref[...] += _rowsum(dcn)
            return carry

        lax.fori_loop(0, tc // CONV_SUB, sub1, 0)

        def sub2(k, carry):
            r0 = pl.multiple_of(k * CONV_SUB, CONV_SUB)
            dwin = dcs_ref[pl.ds(r0, CONV_WIN), :]
            dh = _conv_taps(dwin, lambda o: w_ref[CONV_WIDTH - 1 - o:CONV_WIDTH - o, :], 0)
            av = a_ref[pl.ds(r0, CONV_SUB), :]
            sg = _sigmoid(g_ref[pl.ds(r0, CONV_SUB), :])
            du_ref[pl.ds(r0, CONV_SUB), 0:dc] = dh * sg
            du_ref[pl.ds(r0, CONV_SUB), dc:2 * dc] = dh * av * sg * (1.0 - sg)
            hwin = hs_ref[pl.ds(r0, CONV_WIN), :]
            dcv = dwin[0:CONV_SUB, :]
            for rho in range(8):
                offs = [o for o in range(2, 2 + CONV_WIDTH) if o % 8 == rho]
                rolled = hwin if rho == 0 else pltpu.roll(hwin, CONV_WIN - rho, 0)
                for o in offs:
                    m8 = o - rho
                    dw_ref[o - 2:o - 1, :] += _rowsum(dcv * rolled[m8:m8 + CONV_SUB, :])
            return carry

        lax.fori_loop(0, tc // CONV_SUB, sub2, 0)

    vec = pl.BlockSpec((1, dc), lambda t: (0, 0))
    prev = lambda col: (lambda t: (jnp.maximum(t * hb - 1, 0), col))
    nxt = lambda col: (lambda t: (jnp.minimum((t + 1) * hb, nhalo - 1), col))
    return pl.pallas_call(
        body, name="conv_bwd", grid=(r // tc,),
        in_specs=[pl.BlockSpec((tc, dc), lambda t: (t, 0)), pl.BlockSpec((tc, dc), lambda t: (t, 1)),
                  pl.BlockSpec((CONV_HALO, dc), prev(0)), pl.BlockSpec((CONV_HALO, dc), prev(1)),
                  pl.BlockSpec((tc, dc), lambda t: (t, 0)), pl.BlockSpec((tc, dc), lambda t: (t, 0)),
                  pl.BlockSpec((CONV_HALO, dc), nxt(0)), pl.BlockSpec((CONV_HALO, dc), nxt(0)),
                  pl.BlockSpec((32, dc), lambda t: (0, 0)), vec, vec],
        out_specs=[pl.BlockSpec((tc, 2 * dc), lambda t: (t, 0)), pl.BlockSpec((32, dc), lambda t: (0, 0)), vec, vec, vec],
        out_shape=[_sds((r, 2 * dc), F32), _sds((32, dc), F32), _sds((1, dc), F32), _sds((1, dc), F32), _sds((1, dc), F32)],
        scratch_shapes=[pltpu.VMEM((CONV_HALO + tc, dc), F32), pltpu.VMEM((tc + CONV_HALO, dc), F32)],
        compiler_params=_params(("arbitrary",)),
    )(u, u, u, u, c, dmi, c, dmi, w32, cg, cbe)


def _inproj_bwd(dp1, dus, xcat, g_in, w_in, dep, tp, tm):
    r, d = dp1.shape
    widths = [x.shape[1] for x in dus]
    offs = [sum(widths[:k]) for k in range(len(widths))]
    n = w_in.shape[1]
    nd = len(dus)
    tps = tp // tm

    def body(*refs):
        dp_ref = refs[0]
        du_refs = refs[1:1 + nd]
        x_ref, g_ref, w_ref, _, dx_ref, dmeta_ref, dg_ref, db_ref = refs[1 + nd:]
        i = pl.program_id(0)

        @pl.when(i == 0)
        def _():
            dmeta_ref[...] = jnp.zeros_like(dmeta_ref)
            dg_ref[...] = jnp.zeros_like(dg_ref)
            db_ref[...] = jnp.zeros_like(db_ref)

        ds0 = ALPHA * dp_ref[...]
        for k in range(nd):
            ds0 = ds0 + _mm_nt(du_refs[k][...].astype(BF16), w_ref[:, offs[k]:offs[k] + widths[k]])
        real = _row_in_seq(i, tm, tp) >= PAD_FRONT
        ds0 = jnp.where(real, ds0, 0.0)
        xhat, rstd = _ln(x_ref[...])
        dg_ref[...] += _rowsum(ds0 * xhat)
        db_ref[...] += _rowsum(ds0)
        dx = jnp.where(real, _ln_bwd(ds0 * g_ref[...], xhat, rstd), 0.0)
        dx_ref[...] = dx

        @pl.when(lax.rem(i, tps) == 0)
        def _():
            dmeta_ref[...] += dx[PAD_FRONT:X_OFF, :]

    row = lambda w: pl.BlockSpec((tm, w), lambda i: (i, 0))
    vec = pl.BlockSpec((1, d), lambda i: (0, 0))
    return pl.pallas_call(
        body, name="inproj_bwd", grid=(r // tm,),
        in_specs=[row(d)] + [row(w) for w in widths] + [row(d), vec, pl.BlockSpec((d, n), lambda i: (0, 0)),
                                                        pl.BlockSpec(memory_space=pl.ANY)],
        out_specs=[row(d), pl.BlockSpec((N_META, d), lambda i: (0, 0)), vec, vec],
        out_shape=[_sds((r, d), F32), _sds((N_META, d), F32), _sds((1, d), F32), _sds((1, d), F32)],
        compiler_params=_params(("arbitrary",)),
    )(dp1, *dus, xcat, g_in, w_in, dep)


def _inproj_bwd_w(s0, dus, tm):
    r, d = s0.shape
    widths = [x.shape[1] for x in dus]
    nd = len(dus)

    def body(*refs):
        s_ref = refs[0]
        du_refs = refs[1:1 + nd]
        dw_refs = refs[1 + nd:]
        i = pl.program_id(0)

        @pl.when(i == 0)
        def _():
            for k in range(nd):
                dw_refs[k][...] = jnp.zeros_like(dw_refs[k])

        sb = s_ref[...].astype(BF16)
        for k in range(nd):
            dw_refs[k][...] += _mm_tn(sb, du_refs[k][...].astype(BF16))

    row = lambda w: pl.BlockSpec((tm, w), lambda i: (i, 0))
    return pl.pallas_call(
        body, name="inproj_bwd_w", grid=(r // tm,),
        in_specs=[row(d)] + [row(w) for w in widths],
        out_specs=[pl.BlockSpec((d, w), lambda i: (0, 0)) for w in widths],
        out_shape=[_sds((d, w), F32) for w in widths],
        compiler_params=_params(("arbitrary",)),
    )(s0, *dus)


def _local_step(x, tgt, meta, ln_in_g, ln_in_b, w_in, conv_w, conv_b, conv_ln_g, conv_ln_b, gate_up, gate_bias,
                gla_norm_g, late_weights, ln1_g, ln1_b, ln2_g, ln2_b, push):
    bsz, seq, d = x.shape
    tp = X_OFF + seq
    assert tp % CHUNK == 0
    nc = tp // CHUNK
    r = bsz * tp
    dc = conv_b.shape[1]
    tm = _pick_tile(tp, (352, 128, 64))
    tc = _pick_tile(tp, (704, 128, 64))

    xcat = jnp.concatenate([jnp.zeros((bsz, PAD_FRONT, d), F32), jnp.broadcast_to(meta[None], (bsz, N_META, d)), x],
                           axis=1).reshape(r, d)
    tgt_p = jnp.pad(tgt, ((0, 0), (X_OFF, 0), (0, 0))).reshape(r, d)
    w32 = jnp.pad(conv_w, ((0, 32 - CONV_WIDTH), (0, 0)))
    gup = jnp.pad(gate_up, ((0, LANES - GLA_RANK), (0, 0))).astype(BF16)

    s0, u = _inproj_fwd(xcat, ln_in_g, ln_in_b, w_in, tp, tm)
    c, co = _conv_fwd(u, w32, conv_b, conv_ln_g, conv_ln_b, tp, tc, dc)
    go, sta = _gla_fwd(u, gup, gate_bias, gla_norm_g, bsz, nc)
    w_out, w1g, w2 = late_weights(go)
    nh = w1g.shape[0]
    tmm = _pick_tile(tp, (704, 128, 64))
    ns = 2
    p1, s1, s1b = _outproj_fwd(s0, co, go, w_out, ln1_g, ln1_b, tm)
    hm, dp2, dpb, loss, dg2, db2 = _mlp_fwd(s1, s1b, w1g, w2, ln2_g, ln2_b, tgt_p, tp, tmm, ns)

    dh, dp1, dg1, db1 = _mlp_bwd_act(dp2, dpb, hm, w1g, w2, p1, ln1_g, tmm, ns)
    dw1, dw2 = _mlp_bwd_w(s1b, hm, dh, dpb, nh, tmm, ns)
    tok = push("ff", (dw1, dw2))
    dmi, dwo = _outproj_bwd(dp1, co, go, w_out, tok, tm)
    tok = push("out", (dwo,))
    dqk, dv, dr, dgd, dgn, dgb, dgup = _gla_bwd(u, dmi, sta, gup, gate_bias, gla_norm_g, tok, bsz, nc)
    dcv, dcw, dcb, dcg, dcbe = _conv_bwd(u, c, dmi, w32, conv_ln_g, conv_ln_b, tp, tc, dc)
    dus = [dcv, dqk, dv, dr, dgd]
    dwi = _inproj_bwd_w(s0, dus, tm)
    tok = push("in", (jnp.concatenate(dwi, axis=1),))
    dxcat, dmeta, dgi, dbi = _inproj_bwd(dp1, dus, xcat, ln_in_g, w_in, tok, tp, tm)

    grad_x = dxcat.reshape(bsz, tp, d)[:, X_OFF:, :]
    return dict(loss=loss[0, 0], grad_x=grad_x, meta_tokens=dmeta, ln_in_g=dgi, ln_in_b=dbi,
                conv_w=dcw[:CONV_WIDTH], conv_b=dcb, conv_ln_g=dcg, conv_ln_b=dcbe,
                gate_up=dgup[:GLA_RANK], gate_bias=dgb, gla_norm_g=dgn, ln1_g=dg1, ln1_b=db1, ln2_g=dg2, ln2_b=db2)


def _exchange(arrays, scatter, name):
    na = len(arrays)
    npeer = N_DEV - 1

    def body(*refs):
        srcs = refs[:na]
        outs = refs[na:2 * na]
        send_sems, recv_sems, local_sems = refs[2 * na:]
        xi, yi, ci = (lax.axis_index(a) for a in MESH_AXES)
        me = 4 * xi + 2 * yi + ci
        copies = []
        for a in range(na):
            own = srcs[a].at[me] if scatter[a] else srcs[a]
            cp = pltpu.make_async_copy(own, outs[a].at[me], local_sems.at[a])
            cp.start()
            copies.append(cp)
        remote = []
        for k in range(1, N_DEV):
            px, py, pc = xi ^ (k >> 2), yi ^ ((k >> 1) & 1), ci ^ (k & 1)
            peer = 4 * px + 2 * py + pc
            for a in range(na):
                src = srcs[a].at[peer] if scatter[a] else srcs[a]
                cp = pltpu.make_async_remote_copy(
                    src_ref=src, dst_ref=outs[a].at[me],
                    send_sem=send_sems.at[a * npeer + k - 1], recv_sem=recv_sems.at[a * npeer + k - 1],
                    device_id=(px, py, pc), device_id_type=pl.DeviceIdType.MESH)
                cp.start()
                remote.append(cp)
        for cp in remote:
            cp.wait()
        for cp in copies:
            cp.wait()

    out_shape = [_sds(a.shape if scatter[i] else (N_DEV,) + a.shape, a.dtype) for i, a in enumerate(arrays)]
    anyspec = pl.BlockSpec(memory_space=pl.ANY)
    return pl.pallas_call(
        body, name=name,
        in_specs=[anyspec] * na, out_specs=[anyspec] * na, out_shape=out_shape,
        scratch_shapes=[pltpu.SemaphoreType.DMA((na * npeer,)), pltpu.SemaphoreType.DMA((na * npeer,)),
                        pltpu.SemaphoreType.DMA((na,))],
    )(*arrays)


def _peers(xi, yi, ci):
    for k in range(1, N_DEV):
        px, py, pc = xi ^ (k >> 2), yi ^ ((k >> 1) & 1), ci ^ (k & 1)
        yield (px, py, pc), 4 * px + 2 * py + pc


def _push_start(arrays, scatter, name, dep=None):
    na = len(arrays)
    shapes = [a.shape if scatter[i] else (N_DEV,) + a.shape for i, a in enumerate(arrays)]
    hbm = pl.BlockSpec(memory_space=pltpu.HBM)
    sem = pl.BlockSpec(memory_space=pltpu.SEMAPHORE)
    ndep = 0 if dep is None else 1

    def body(*refs):
        srcs = refs[:na]
        lands = refs[na:2 * na]
        send_sems, recv_sems = refs[2 * na + ndep:2 * na + ndep + 2]
        token = refs[4 * na + ndep + 2]
        own_sems = refs[4 * na + ndep + 3]
        xi, yi, ci = (lax.axis_index(a) for a in MESH_AXES)
        me = 4 * xi + 2 * yi + ci
        own = [pltpu.make_async_copy(srcs[a].at[me] if scatter[a] else srcs[a], lands[a].at[me], own_sems.at[a])
               for a in range(na)]
        for cp in own:
            cp.start()
        token[...] = jnp.zeros_like(token)
        for cp in own:
            cp.wait()
        for a in range(na):
            for pos, peer in _peers(xi, yi, ci):
                pltpu.make_async_remote_copy(
                    src_ref=srcs[a].at[peer] if scatter[a] else srcs[a], dst_ref=lands[a].at[me],
                    send_sem=send_sems.at[a], recv_sem=recv_sems.at[a],
                    device_id=pos, device_id_type=pl.DeviceIdType.MESH).start()

    ins = [pltpu.with_memory_space_constraint(a, pltpu.HBM) for a in arrays]
    ins += [pltpu.with_memory_space_constraint(lax.empty(s, a.dtype), pltpu.HBM) for s, a in zip(shapes, arrays)]
    res = pl.pallas_call(
        body, name=name,
        in_specs=[hbm] * (2 * na) + [pl.BlockSpec(memory_space=pl.ANY)] * ndep,
        out_specs=[sem, sem] + [hbm] * (2 * na) + [pl.BlockSpec(memory_space=pltpu.VMEM)],
        out_shape=[pltpu.SemaphoreType.DMA((na,)), pltpu.SemaphoreType.DMA((na,))]
                  + [pltpu.HBM(a.shape, a.dtype) for a in arrays] + [pltpu.HBM(s, a.dtype) for s, a in zip(shapes, arrays)]
                  + [_sds((8, LANES), F32)],
        input_output_aliases={i: 2 + i for i in range(2 * na)},
        scratch_shapes=[pltpu.SemaphoreType.DMA((na,))],
        compiler_params=pltpu.CompilerParams(has_side_effects=pltpu.SideEffectType.DATAFLOW_SIDE_EFFECTING),
    )(*ins, *([] if dep is None else [dep]))
    return (res[0], res[1], list(res[2:2 + na]), list(res[2 + na:2 + 2 * na])), res[-1]


def _push_wait(handle, after, name):
    send_sems, recv_sems, srcs, lands = handle
    na = len(srcs)
    hbm = pl.BlockSpec(memory_space=pltpu.HBM)
    sem = pl.BlockSpec(memory_space=pltpu.SEMAPHORE)

    def body(*refs):
        land_refs = refs[na:2 * na]
        send_ref, recv_ref = refs[2 * na:2 * na + 2]
        me = tuple(lax.axis_index(a) for a in MESH_AXES)
        for a in range(na):
            seven = land_refs[a].at[pl.ds(0, N_DEV - 1)]
            cp = pltpu.make_async_remote_copy(src_ref=seven, dst_ref=seven, send_sem=send_ref.at[a], recv_sem=recv_ref.at[a],
                                              device_id=me, device_id_type=pl.DeviceIdType.MESH)
            cp.wait_send()
            cp.wait_recv()

    res = pl.pallas_call(
        body, name=name,
        in_specs=[hbm] * (2 * na) + [sem, sem, pl.BlockSpec(memory_space=pl.ANY)],
        out_specs=[hbm] * (2 * na),
        out_shape=[pltpu.HBM(a.shape, a.dtype) for a in srcs] + [pltpu.HBM(a.shape, a.dtype) for a in lands],
        input_output_aliases={i: i for i in range(2 * na)},
        compiler_params=pltpu.CompilerParams(has_side_effects=pltpu.SideEffectType.DATAFLOW_SIDE_EFFECTING),
    )(*srcs, *lands, send_sems, recv_sems, after)
    return list(res[na:])


def _adamw(w, g, m, v):
    m = ADAM_B1 * m + (1.0 - ADAM_B1) * g
    v = ADAM_B2 * v + (1.0 - ADAM_B2) * jnp.square(g)
    m_hat = m / (1.0 - ADAM_B1 ** ADAM_STEP)
    v_hat = v / (1.0 - ADAM_B2 ** ADAM_STEP)
    delta = -ADAM_LR * (m_hat / (jnp.sqrt(v_hat) + ADAM_EPS) + ADAM_WD * w)
    return delta, m, v


def _sum_devices(ref):
    g = ref[0].astype(F32)
    for k in range(1, N_DEV):
        g = g + ref[k].astype(F32)
    return g


def _update_big(parts, w, m, v, name):
    rows, cols = w.shape
    tr = _pick_tile(rows, (128, 64, 16))

    def body(p_ref, w_ref, m_ref, v_ref, g_ref, d_ref, nm_ref, nv_ref):
        g = _sum_devices(p_ref)
        g_ref[...] = g
        d_ref[...], nm_ref[...], nv_ref[...] = _adamw(w_ref[...], g, m_ref[...], v_ref[...])

    blk = pl.BlockSpec((tr, cols), lambda i: (i, 0))
    return pl.pallas_call(
        body, name=name, grid=(rows // tr,),
        in_specs=[pl.BlockSpec((N_DEV, tr, cols), lambda i: (0, i, 0)), blk, blk, blk],
        out_specs=[blk] * 4, out_shape=[_sds((rows, cols), F32)] * 4,
        compiler_params=_params(("parallel",)),
    )(parts, w, m, v)


_VEC_ORDER = ("ln_in_g", "ln_in_b", "conv_b", "conv_ln_g", "conv_ln_b", "gate_bias", "gla_norm_g",
              "ln1_g", "ln1_b", "ln2_g", "ln2_b")
_SHARDED_SMALL = (("meta_tokens", 0, N_META, LANES), ("conv_w", N_META, CONV_WIDTH, None), ("gate_up", N_META + 32, GLA_RANK, None))


def _update_small(parts_sh, parts_vec, wmv):
    names = [s[0] for s in _SHARDED_SMALL] + list(_VEC_ORDER)
    flat = [a for nme in names for a in wmv[nme]]

    def body(*refs):
        sh_ref, vec_ref = refs[0], refs[1]
        ins = refs[2:2 + len(flat)]
        outs = refs[2 + len(flat):2 + len(flat) + 4 * len(names)]
        gsh_ref, gvec_ref = refs[-2:]
        gsh_ref[...] = _sum_devices(sh_ref)
        gvec_ref[...] = _sum_devices(vec_ref)
        for idx, nme in enumerate(names):
            w_ref, m_ref, v_ref = ins[3 * idx:3 * idx + 3]
            rows, cols = w_ref.shape
            if idx < len(_SHARDED_SMALL):
                r0 = _SHARDED_SMALL[idx][1]
                g = gsh_ref[r0:r0 + rows, 0:cols]
            else:
                j = idx - len(_SHARDED_SMALL)
                g = gvec_ref[j:j + 1, 0:cols]
            o = outs[4 * idx:4 * idx + 4]
            o[0][...] = g
            o[1][...], o[2][...], o[3][...] = _adamw(w_ref[...], g, m_ref[...], v_ref[...])

    out_shape = [_sds(wmv[nme][0].shape, F32) for nme in names for _ in range(4)]
    vmem = pl.BlockSpec(memory_space=pltpu.VMEM)
    res = pl.pallas_call(
        body, name="update_small", out_shape=out_shape,
        in_specs=[vmem] * (2 + len(flat)), out_specs=[vmem] * len(out_shape),
        scratch_shapes=[pltpu.VMEM(parts_sh.shape[1:], F32), pltpu.VMEM(parts_vec.shape[1:], F32)],
    )(parts_sh, parts_vec, *flat)
    return {nme: res[4 * i:4 * i + 4] for i, nme in enumerate(names)}


_WEIGHTS = ("meta_tokens", "ln_in_g", "ln_in_b", "w_in", "conv_w", "conv_b", "conv_ln_g", "conv_ln_b", "gate_up",
            "gate_bias", "gla_norm_g", "w_out", "ln1_g", "ln1_b", "w_ff1", "w_ff2", "ln2_g", "ln2_b")


def kernel(x, meta_tokens, ln_in_g, ln_in_b, w_in, conv_w, conv_b, conv_ln_g, conv_ln_b, gate_up, gate_bias, gla_norm_g, w_out, ln1_g, ln1_b, w_ff1, w_ff2, ln2_g, ln2_b, loss_target, m_meta_tokens, m_ln_in_g, m_ln_in_b, m_w_in, m_conv_w, m_conv_b, m_conv_ln_g, m_conv_ln_b, m_gate_up, m_gate_bias, m_gla_norm_g, m_w_out, m_ln1_g, m_ln1_b, m_w_ff1, m_w_ff2, m_ln2_g, m_ln2_b, v_meta_tokens, v_ln_in_g, v_ln_in_b, v_w_in, v_conv_w, v_conv_b, v_conv_ln_g, v_conv_ln_b, v_gate_up, v_gate_bias, v_gla_norm_g, v_w_out, v_ln1_g, v_ln1_b, v_w_ff1, v_w_ff2, v_ln2_g, v_ln2_b):
    w = dict(meta_tokens=meta_tokens, ln_in_g=ln_in_g, ln_in_b=ln_in_b, w_in=w_in, conv_w=conv_w, conv_b=conv_b,
             conv_ln_g=conv_ln_g, conv_ln_b=conv_ln_b, gate_up=gate_up, gate_bias=gate_bias, gla_norm_g=gla_norm_g,
             w_out=w_out, ln1_g=ln1_g, ln1_b=ln1_b, w_ff1=w_ff1, w_ff2=w_ff2, ln2_g=ln2_g, ln2_b=ln2_b)
    mom = dict(meta_tokens=m_meta_tokens, ln_in_g=m_ln_in_g, ln_in_b=m_ln_in_b, w_in=m_w_in, conv_w=m_conv_w,
               conv_b=m_conv_b, conv_ln_g=m_conv_ln_g, conv_ln_b=m_conv_ln_b, gate_up=m_gate_up, gate_bias=m_gate_bias,
               gla_norm_g=m_gla_norm_g, w_out=m_w_out, ln1_g=m_ln1_g, ln1_b=m_ln1_b, w_ff1=m_w_ff1, w_ff2=m_w_ff2,
               ln2_g=m_ln2_g, ln2_b=m_ln2_b)
    var = dict(meta_tokens=v_meta_tokens, ln_in_g=v_ln_in_g, ln_in_b=v_ln_in_b, w_in=v_w_in, conv_w=v_conv_w,
               conv_b=v_conv_b, conv_ln_g=v_conv_ln_g, conv_ln_b=v_conv_ln_b, gate_up=v_gate_up, gate_bias=v_gate_bias,
               gla_norm_g=v_gla_norm_g, w_out=v_w_out, ln1_g=v_ln1_g, ln1_b=v_ln1_b, w_ff1=v_w_ff1, w_ff2=v_w_ff2,
               ln2_g=v_ln2_g, ln2_b=v_ln2_b)
    shapes = {k: a.shape for k, a in w.items()}

    def two_d(a):
        return a.reshape(1, -1) if a.ndim == 1 else a.reshape(a.shape[-2:])

    w2d = {k: two_d(a) for k, a in w.items()}
    m2d = {k: two_d(a) for k, a in mom.items()}
    v2d = {k: two_d(a) for k, a in var.items()}
    d = x.shape[-1]
    d_in = w2d["w_in"].shape[1] * N_DEV
    d_in_p = -(-d_in // LANES) * LANES

    first, tok_first = _push_start(
        [w2d["w_in"].astype(BF16), w2d["meta_tokens"], w2d["conv_w"], w2d["gate_up"]], [False] * 4, "gather_first_start")
    late, tok_late = _push_start(
        [w2d["w_out"].astype(BF16), w2d["w_ff1"].astype(BF16), w2d["w_ff2"].astype(BF16)], [False] * 3,
        "gather_late_start", dep=tok_first)
    g_in, g_meta, g_conv, g_gup = _push_wait(first, tok_late, "gather_first_wait")
    w_in_full = jnp.pad(g_in.transpose(1, 0, 2).reshape(d, d_in), ((0, 0), (0, d_in_p - d_in)))
    meta_full = g_meta.transpose(1, 0, 2).reshape(N_META, d)
    conv_w_full = g_conv.transpose(1, 0, 2).reshape(CONV_WIDTH, -1)
    gate_up_full = g_gup.transpose(1, 0, 2).reshape(GLA_RANK, -1)

    def late_weights(after):
        g_out, g_ff1, g_ff2 = _push_wait(late, after, "gather_late_wait")
        return g_out.reshape(-1, d), g_ff1, g_ff2.reshape(-1, d)

    pushed = {}

    def push(tag, grads):
        if tag == "in":
            grads = (grads[0][:, :d_in].reshape(d, N_DEV, d_in // N_DEV).transpose(1, 0, 2).astype(BF16),)
        elif tag == "out":
            grads = (grads[0].reshape(N_DEV, -1, d),)
        pushed[tag], tok = _push_start(list(grads), [True] * len(grads), f"scatter_{tag}_start")
        return tok

    res = _local_step(x, loss_target, meta_full, w2d["ln_in_g"], w2d["ln_in_b"], w_in_full, conv_w_full, w2d["conv_b"],
                      w2d["conv_ln_g"], w2d["conv_ln_b"], gate_up_full, w2d["gate_bias"], w2d["gla_norm_g"], late_weights,
                      w2d["ln1_g"], w2d["ln1_b"], w2d["ln2_g"], w2d["ln2_b"], push)

    dc = res["conv_w"].shape[1]
    hk = res["gate_up"].shape[1]
    sh_meta = res["meta_tokens"].reshape(N_META, N_DEV, LANES).transpose(1, 0, 2)
    sh_conv = jnp.pad(res["conv_w"].reshape(CONV_WIDTH, N_DEV, dc // N_DEV).transpose(1, 0, 2),
                      ((0, 0), (0, 32 - CONV_WIDTH), (0, LANES - dc // N_DEV)))
    sh_gup = jnp.pad(res["gate_up"].reshape(GLA_RANK, N_DEV, hk // N_DEV).transpose(1, 0, 2),
                     ((0, 0), (0, 0), (0, LANES - hk // N_DEV)))
    p_sh = jnp.concatenate([sh_meta, sh_conv, sh_gup], axis=1)
    p_vec = jnp.concatenate([jnp.pad(res[k], ((0, 0), (0, d - res[k].shape[1]))) for k in _VEC_ORDER]
                            + [jnp.zeros((16 - len(_VEC_ORDER), d), F32)], axis=0)

    r_sh, r_vec = _exchange([p_sh, p_vec], [True, False], "scatter_small")
    r_ff1, r_ff2 = _push_wait(pushed["ff"], r_vec, "scatter_ff_wait")
    (r_out,) = _push_wait(pushed["out"], r_ff1, "scatter_out_wait")
    (r_in,) = _push_wait(pushed["in"], r_out, "scatter_in_wait")

    upd = {}
    upd["w_in"] = _update_big(r_in, w2d["w_in"], m2d["w_in"], v2d["w_in"], "update_w_in")
    upd["w_out"] = _update_big(r_out, w2d["w_out"], m2d["w_out"], v2d["w_out"], "update_w_out")
    upd["w_ff1"] = _update_big(r_ff1, w2d["w_ff1"], m2d["w_ff1"], v2d["w_ff1"], "update_w_ff1")
    upd["w_ff2"] = _update_big(r_ff2, w2d["w_ff2"], m2d["w_ff2"], v2d["w_ff2"], "update_w_ff2")
    small = [s[0] for s in _SHARDED_SMALL] + list(_VEC_ORDER)
    upd.update(_update_small(r_sh, r_vec, {k: (w2d[k], m2d[k], v2d[k]) for k in small}))

    loss = lax.psum(res["loss"], MESH_AXES)
    outs = [loss, res["grad_x"]]
    for j in range(4):
        outs += [upd[k][j].reshape(shapes[k]) for k in _WEIGHTS]
    return tuple(outs)
```

```python
import functools

import jax
import jax.numpy as jnp
from jax import lax
from jax.experimental import pallas as pl
from jax.experimental.pallas import tpu as pltpu

F32 = jnp.float32
BF16 = jnp.bfloat16

N_META = 16
CHUNK = 64
PAD_FRONT = (-N_META) % CHUNK
X_OFF = PAD_FRONT + N_META
CONV_WIDTH = 31
CONV_HALO = 32
CONV_SUB = 64
CONV_WIN = CONV_SUB + CONV_HALO
GLA_HEADS = 4
GLA_DK = 64
GLA_DV = 128
GLA_RANK = 16
GLA_TAU = 16.0
QK_SCALE = GLA_DK ** -0.5
LN_EPS = 1e-5
ALPHA = 2.0 ** 0.25
LANES = 128
N_DEV = 8
ADAM_LR = 0.001
ADAM_B1 = 0.9
ADAM_B2 = 0.999
ADAM_EPS = 1e-08
ADAM_WD = 0.01
ADAM_STEP = 10
VMEM_LIMIT = 56 * 1024 * 1024
MESH_AXES = ("x", "y", "c")


def _sds(shape, dtype):
    return jax.ShapeDtypeStruct(shape, dtype)


def _mm(a, b):
    return jnp.dot(a, b, preferred_element_type=F32)


def _mm_nt(a, b):
    return lax.dot_general(a, b, (((1,), (1,)), ((), ())), preferred_element_type=F32)


def _mm_tn(a, b):
    return lax.dot_general(a, b, (((0,), (0,)), ((), ())), preferred_element_type=F32)


def _sigmoid(x):
    return 1.0 / (1.0 + jnp.exp(-x))


def _log_sigmoid(z):
    return jnp.minimum(z, 0.0) - jnp.log(1.0 + jnp.exp(-jnp.abs(z)))


def _ln(x):
    mu = jnp.mean(x, axis=-1, keepdims=True)
    xc = x - mu
    var = jnp.mean(xc * xc, axis=-1, keepdims=True)
    rstd = lax.rsqrt(var + LN_EPS)
    return xc * rstd, rstd


def _ln_bwd(dyg, xhat, rstd):
    m1 = jnp.mean(dyg, axis=-1, keepdims=True)
    m2 = jnp.mean(dyg * xhat, axis=-1, keepdims=True)
    return rstd * (dyg - m1 - xhat * m2)


def _rowsum(x):
    return jnp.sum(x, axis=0, keepdims=True)


def _row_in_seq(i, tm, tp):
    base = lax.rem(i * tm, tp)
    return base + lax.broadcasted_iota(jnp.int32, (tm, 1), 0)


def _split3(x):
    hi = x.astype(BF16)
    r1 = x - hi.astype(F32)
    mid = r1.astype(BF16)
    lo = (r1 - mid.astype(F32)).astype(BF16)
    return hi, mid, lo


def _tri_mm(tri, x):
    hi, mid, lo = _split3(x)
    return _mm(tri, hi) + _mm(tri, mid) + _mm(tri, lo)


def _params(sem):
    return pltpu.CompilerParams(dimension_semantics=sem, vmem_limit_bytes=VMEM_LIMIT)


def _pick_tile(n, prefs):
    for t in prefs:
        if n % t == 0:
            return t
    raise ValueError(f"no tile for {n}")


def _inproj_fwd(xcat, g, b, w_in, tp, tm):
    r, d = xcat.shape
    n = w_in.shape[1]

    def body(x_ref, g_ref, b_ref, w_ref, s0_ref, u_ref):
        i = pl.program_id(0)
        xhat, _ = _ln(x_ref[...])
        real = _row_in_seq(i, tm, tp) >= PAD_FRONT
        s = jnp.where(real, xhat * g_ref[...] + b_ref[...], 0.0)
        s0_ref[...] = s
        u_ref[...] = _mm(s.astype(BF16), w_ref[...])

    return pl.pallas_call(
        body, name="inproj_fwd", grid=(r // tm,),
        in_specs=[pl.BlockSpec((tm, d), lambda i: (i, 0)), pl.BlockSpec((1, d), lambda i: (0, 0)),
                  pl.BlockSpec((1, d), lambda i: (0, 0)), pl.BlockSpec((d, n), lambda i: (0, 0))],
        out_specs=[pl.BlockSpec((tm, d), lambda i: (i, 0)), pl.BlockSpec((tm, n), lambda i: (i, 0))],
        out_shape=[_sds((r, d), F32), _sds((r, n), F32)],
        compiler_params=_params(("parallel",)),
    )(xcat, g, b, w_in)


def _conv_taps(win, coef, lo):
    acc = None
    for rho in range(8):
        offs = [o for o in range(lo, lo + CONV_WIDTH) if o % 8 == rho]
        if not offs:
            continue
        rolled = win if rho == 0 else pltpu.roll(win, CONV_WIN - rho, 0)
        for o in offs:
            m8 = o - rho
            term = rolled[m8:m8 + CONV_SUB, :] * coef(o)
            acc = term if acc is None else acc + term
    return acc


def _conv_fwd(u, w32, cb, cg, cbe, tp, tc, dc):
    r = u.shape[0]
    hb = tc // CONV_HALO

    def body(a_ref, g_ref, ah_ref, gh_ref, w_ref, cb_ref, cg_ref, cbe_ref, c_ref, co_ref, hs_ref):
        t = pl.program_id(0)
        first = lax.rem(t * tc, tp) == 0
        hh = ah_ref[...] * _sigmoid(gh_ref[...])
        hs_ref[0:CONV_HALO, :] = jnp.where(first, 0.0, hh)
        hs_ref[CONV_HALO:CONV_HALO + tc, :] = a_ref[...] * _sigmoid(g_ref[...])

        def sub(k, carry):
            r0 = pl.multiple_of(k * CONV_SUB, CONV_SUB)
            win = hs_ref[pl.ds(r0, CONV_WIN), :]
            c = _conv_taps(win, lambda o: w_ref[o - 2:o - 1, :], 2) + cb_ref[...]
            c_ref[pl.ds(r0, CONV_SUB), :] = c
            xhat, _ = _ln(c)
            cn = xhat * cg_ref[...] + cbe_ref[...]
            co_ref[pl.ds(r0, CONV_SUB), :] = (cn * _sigmoid(cn)).astype(BF16)
            return carry

        lax.fori_loop(0, tc // CONV_SUB, sub, 0)

    vec = pl.BlockSpec((1, dc), lambda t: (0, 0))
    return pl.pallas_call(
        body, name="conv_fwd", grid=(r // tc,),
        in_specs=[pl.BlockSpec((tc, dc), lambda t: (t, 0)), pl.BlockSpec((tc, dc), lambda t: (t, 1)),
                  pl.BlockSpec((CONV_HALO, dc), lambda t: (jnp.maximum(t * hb - 1, 0), 0)),
                  pl.BlockSpec((CONV_HALO, dc), lambda t: (jnp.maximum(t * hb - 1, 0), 1)),
                  pl.BlockSpec((32, dc), lambda t: (0, 0)), vec, vec, vec],
        out_specs=[pl.BlockSpec((tc, dc), lambda t: (t, 0)), pl.BlockSpec((tc, dc), lambda t: (t, 0))],
        out_shape=[_sds((r, dc), F32), _sds((r, dc), BF16)],
        scratch_shapes=[pltpu.VMEM((CONV_HALO + tc, dc), F32)],
        compiler_params=_params(("parallel",)),
    )(u, u, u, u, w32, cb, cg, cbe)


def _gla_prep(qk, gd, gup, gb, n):
    z = _mm(gd.astype(BF16), gup) + gb
    lg = _log_sigmoid(z) * (1.0 / GLA_TAU)
    row = n * CHUNK + lax.broadcasted_iota(jnp.int32, (CHUNK, 1), 0)
    real = row >= PAD_FRONT
    lg = jnp.where(real, lg, 0.0)
    ri = lax.broadcasted_iota(jnp.int32, (CHUNK, CHUNK), 0)
    ci = lax.broadcasted_iota(jnp.int32, (CHUNK, CHUNK), 1)
    low = (ri >= ci).astype(BF16)
    b = _tri_mm(low, lg)
    bl = _rowsum(lg)
    hk = GLA_HEADS * GLA_DK
    q = qk[:, :hk] * QK_SCALE
    k = qk[:, hk:]
    eb = jnp.exp(b)
    enb = jnp.exp(-b)
    ebl = jnp.exp(bl - b)
    gam = jnp.exp(bl)
    return dict(z=z, real=real, ri=ri, ci=ci, eb=eb, enb=enb, ebl=ebl, gam=gam, k=k,
                qe=q * eb, ke=k * enb, kd=k * ebl)


def _head_mask(h2):
    lane = lax.broadcasted_iota(jnp.int32, (1, LANES), 1)
    return (lane < GLA_DK) if h2 == 0 else (lane >= GLA_DK)


def _gla_fwd(u, gup, gb, gn, bsz, nc):
    r = u.shape[0]
    hv = GLA_HEADS * GLA_DV

    def body(qk_ref, v_ref, r_ref, gd_ref, gup_ref, gb_ref, gn_ref, go_ref, sta_ref, st_ref):
        n = pl.program_id(1)

        @pl.when(n == 0)
        def _():
            st_ref[...] = jnp.zeros_like(st_ref)

        p = _gla_prep(qk_ref[...], gd_ref[...], gup_ref[...], gb_ref[...], n)
        tril = p["ri"] >= p["ci"]
        for h in range(GLA_HEADS):
            hp, h2 = divmod(h, 2)
            ls = slice(hp * LANES, (hp + 1) * LANES)
            m = _head_mask(h2)
            qeh = jnp.where(m, p["qe"][:, ls], 0.0).astype(BF16)
            kdh = jnp.where(m, p["kd"][:, ls], 0.0).astype(BF16)
            keh = p["ke"][:, ls].astype(BF16)
            vh = v_ref[:, h * GLA_DV:(h + 1) * GLA_DV].astype(BF16)
            st = st_ref[h]
            sta_ref[h] = st
            a = jnp.where(tril, _mm_nt(qeh, keh), 0.0)
            o = _mm(a.astype(BF16), vh) + _mm_nt(qeh, st.astype(BF16))
            st_ref[h] = st * p["gam"][:, ls] + _mm_tn(vh, kdh)
            rs = lax.rsqrt(jnp.mean(o * o, axis=-1, keepdims=True) + LN_EPS)
            rr = r_ref[:, h * GLA_DV:(h + 1) * GLA_DV]
            go = o * rs * gn_ref[...] * (rr * _sigmoid(rr))
            go_ref[:, h * GLA_DV:(h + 1) * GLA_DV] = go.astype(BF16)

    rowblk = lambda col: (lambda b, n: (b * nc + n, col))
    const = lambda b, n: (0, 0)
    return pl.pallas_call(
        body, name="gla_fwd", grid=(bsz, nc),
        in_specs=[pl.BlockSpec((CHUNK, 512), rowblk(2)), pl.BlockSpec((CHUNK, hv), rowblk(3)),
                  pl.BlockSpec((CHUNK, hv), rowblk(4)), pl.BlockSpec((CHUNK, LANES), rowblk(20)),
                  pl.BlockSpec((LANES, 256), const), pl.BlockSpec((1, 256), const), pl.BlockSpec((1, GLA_DV), const)],
        out_specs=[pl.BlockSpec((CHUNK, hv), rowblk(0)),
                   pl.BlockSpec((None, GLA_HEADS, LANES, LANES), lambda b, n: (b * nc + n, 0, 0, 0))],
        out_shape=[_sds((r, hv), BF16), _sds((bsz * nc, GLA_HEADS, LANES, LANES), F32)],
        scratch_shapes=[pltpu.VMEM((GLA_HEADS, LANES, LANES), F32)],
        compiler_params=_params(("parallel", "arbitrary")),
    )(u, u, u, u, gup, gb, gn)


def _outproj_fwd(s0, co, go, w_out, g1, b1, tm):
    r, d = s0.shape
    dc = co.shape[1]

    def body(s0_ref, co_ref, go_ref, w_ref, g_ref, b_ref, p1_ref, s1_ref, s1b_ref):
        mix = _mm(co_ref[...], w_ref[0:dc, :]) + _mm(go_ref[...], w_ref[dc:2 * dc, :])
        p1 = ALPHA * s0_ref[...] + mix
        p1_ref[...] = p1
        xhat, _ = _ln(p1)
        s1 = xhat * g_ref[...] + b_ref[...]
        s1_ref[...] = s1
        s1b_ref[...] = s1.astype(BF16)

    row = lambda w: pl.BlockSpec((tm, w), lambda i: (i, 0))
    vec = pl.BlockSpec((1, d), lambda i: (0, 0))
    return pl.pallas_call(
        body, name="outproj_fwd", grid=(r // tm,),
        in_specs=[row(d), row(dc), row(dc), pl.BlockSpec((2 * dc, d), lambda i: (0, 0)), vec, vec],
        out_specs=[row(d), row(d), row(d)],
        out_shape=[_sds((r, d), F32), _sds((r, d), F32), _sds((r, d), BF16)],
        compiler_params=_params(("parallel",)),
    )(s0, co, go, w_out, g1, b1)


def _mlp_fwd(s1, s1b, w1g, w2, g2, b2, tgt, tp, tm, ns):
    r, d = s1.shape
    nh, _, th = w1g.shape
    nj = nh // ns

    def body(s1_ref, sb_ref, w1_ref, w2_ref, g_ref, b_ref, t_ref, hm_ref, dp2_ref, dpb_ref, loss_ref, dg_ref, db_ref, acc_ref):
        i = pl.program_id(0)
        j = pl.program_id(1)

        @pl.when(jnp.logical_and(i == 0, j == 0))
        def _():
            loss_ref[...] = jnp.zeros_like(loss_ref)
            dg_ref[...] = jnp.zeros_like(dg_ref)
            db_ref[...] = jnp.zeros_like(db_ref)

        @pl.when(j == 0)
        def _():
            acc_ref[...] = jnp.zeros_like(acc_ref)

        for s in range(ns):
            h = _mm(sb_ref[...], w1_ref[s])
            hm_ref[:, s * th:(s + 1) * th] = h.astype(BF16)
            act = jnp.square(jnp.maximum(h, 0.0))
            acc_ref[...] += _mm(act.astype(BF16), w2_ref[s * th:(s + 1) * th, :])

        @pl.when(j == nj - 1)
        def _():
            p2 = ALPHA * s1_ref[...] + acc_ref[...]
            xhat, rstd = _ln(p2)
            s2 = xhat * g_ref[...] + b_ref[...]
            isx = _row_in_seq(i, tm, tp) >= X_OFF
            err = jnp.where(isx, s2 - t_ref[...], 0.0)
            loss_ref[...] += 0.5 * jnp.sum(jnp.mean(err * err, axis=-1, keepdims=True))
            dy = err * (1.0 / d)
            dg_ref[...] += _rowsum(dy * xhat)
            db_ref[...] += _rowsum(dy)
            dp2 = _ln_bwd(dy * g_ref[...], xhat, rstd)
            dp2_ref[...] = dp2
            dpb_ref[...] = dp2.astype(BF16)

    row = pl.BlockSpec((tm, d), lambda i, j: (i, 0))
    vec = pl.BlockSpec((1, d), lambda i, j: (0, 0))
    return pl.pallas_call(
        body, name="mlp_fwd", grid=(r // tm, nj),
        in_specs=[row, row, pl.BlockSpec((ns, d, th), lambda i, j: (j, 0, 0)), pl.BlockSpec((ns * th, d), lambda i, j: (j, 0)),
                  vec, vec, row],
        out_specs=[pl.BlockSpec((tm, ns * th), lambda i, j: (i, j)), row, row,
                   pl.BlockSpec((8, LANES), lambda i, j: (0, 0)), vec, vec],
        out_shape=[_sds((r, nh * th), BF16), _sds((r, d), F32), _sds((r, d), BF16), _sds((8, LANES), F32),
                   _sds((1, d), F32), _sds((1, d), F32)],
        scratch_shapes=[pltpu.VMEM((tm, d), F32)],
        compiler_params=_params(("arbitrary", "arbitrary")),
    )(s1, s1b, w1g, w2, g2, b2, tgt)


def _mlp_bwd_act(dp2, dpb, hm, w1g, w2, p1, g1, tm, ns):
    r, d = dp2.shape
    nh, _, th = w1g.shape
    nj = nh // ns

    def body(dp2_ref, dpb_ref, hm_ref, w1_ref, w2_ref, p1_ref, g_ref, dh_ref, dp1_ref, dg_ref, db_ref, acc_ref):
        i = pl.program_id(0)
        j = pl.program_id(1)

        @pl.when(jnp.logical_and(i == 0, j == 0))
        def _():
            dg_ref[...] = jnp.zeros_like(dg_ref)
            db_ref[...] = jnp.zeros_like(db_ref)

        @pl.when(j == 0)
        def _():
            acc_ref[...] = jnp.zeros_like(acc_ref)

        for s in range(ns):
            cols = slice(s * th, (s + 1) * th)
            dact = _mm_nt(dpb_ref[...], w2_ref[cols, :])
            dh = (dact * (2.0 * jnp.maximum(hm_ref[:, cols].astype(F32), 0.0))).astype(BF16)
            dh_ref[:, cols] = dh
            acc_ref[...] += _mm_nt(dh, w1_ref[s])

        @pl.when(j == nj - 1)
        def _():
            ds1 = ALPHA * dp2_ref[...] + acc_ref[...]
            xhat, rstd = _ln(p1_ref[...])
            dg_ref[...] += _rowsum(ds1 * xhat)
            db_ref[...] += _rowsum(ds1)
            dp1_ref[...] = _ln_bwd(ds1 * g_ref[...], xhat, rstd)

    row = pl.BlockSpec((tm, d), lambda i, j: (i, 0))
    vec = pl.BlockSpec((1, d), lambda i, j: (0, 0))
    blk = pl.BlockSpec((tm, ns * th), lambda i, j: (i, j))
    return pl.pallas_call(
        body, name="mlp_bwd_act", grid=(r // tm, nj),
        in_specs=[row, row, blk, pl.BlockSpec((ns, d, th), lambda i, j: (j, 0, 0)),
                  pl.BlockSpec((ns * th, d), lambda i, j: (j, 0)), row, vec],
        out_specs=[blk, row, vec, vec],
        out_shape=[_sds((r, nh * th), BF16), _sds((r, d), F32), _sds((1, d), F32), _sds((1, d), F32)],
        scratch_shapes=[pltpu.VMEM((tm, d), F32)],
        compiler_params=_params(("arbitrary", "arbitrary")),
    )(dp2, dpb, hm, w1g, w2, p1, g1)


def _mlp_bwd_w(s1b, hm, dh, dpb, nh, tm, ns):
    r, d = s1b.shape
    th = hm.shape[1] // nh

    def body(s1_ref, hm_ref, dh_ref, dp2_ref, dw1_ref, dw2_ref, a1_ref, a2_ref):
        i = pl.program_id(1)

        @pl.when(i == 0)
        def _():
            a1_ref[...] = jnp.zeros_like(a1_ref)
            a2_ref[...] = jnp.zeros_like(a2_ref)

        for s in range(ns):
            cols = slice(s * th, (s + 1) * th)
            act = jnp.square(jnp.maximum(hm_ref[:, cols].astype(F32), 0.0)).astype(BF16)
            a1_ref[s] += _mm_tn(s1_ref[...], dh_ref[:, cols])
            a2_ref[s] += _mm_tn(act, dp2_ref[...])

        @pl.when(i == pl.num_programs(1) - 1)
        def _():
            dw1_ref[...] = a1_ref[...].astype(BF16)
            dw2_ref[...] = a2_ref[...].astype(BF16)

    row = pl.BlockSpec((tm, d), lambda j, i: (i, 0))
    blk = pl.BlockSpec((tm, ns * th), lambda j, i: (i, j))
    return pl.pallas_call(
        body, name="mlp_bwd_w", grid=(nh // ns, r // tm),
        in_specs=[row, blk, blk, row],
        out_specs=[pl.BlockSpec((ns, d, th), lambda j, i: (j, 0, 0)), pl.BlockSpec((ns, th, d), lambda j, i: (j, 0, 0))],
        out_shape=[_sds((nh, d, th), BF16), _sds((nh, th, d), BF16)],
        scratch_shapes=[pltpu.VMEM((ns, d, th), F32), pltpu.VMEM((ns, th, d), F32)],
        compiler_params=_params(("parallel", "arbitrary")),
    )(s1b, hm, dh, dpb)


def _outproj_bwd(dp1, co, go, w_out, dep, tm):
    r, d = dp1.shape
    dc = co.shape[1]

    def body(dp_ref, co_ref, go_ref, w_ref, dep_ref, dmi_ref, dw_ref, acc_ref):
        i = pl.program_id(0)

        @pl.when(i == 0)
        def _():
            acc_ref[...] = jnp.zeros_like(acc_ref)

        dpb = dp_ref[...].astype(BF16)
        dmi_ref[...] = _mm_nt(dpb, w_ref[...])
        acc_ref[0:dc, :] += _mm_tn(co_ref[...], dpb)
        acc_ref[dc:2 * dc, :] += _mm_tn(go_ref[...], dpb)

        @pl.when(i == pl.num_programs(0) - 1)
        def _():
            dw_ref[...] = acc_ref[...].astype(BF16)

    row = lambda w: pl.BlockSpec((tm, w), lambda i: (i, 0))
    full = pl.BlockSpec((2 * dc, d), lambda i: (0, 0))
    return pl.pallas_call(
        body, name="outproj_bwd", grid=(r // tm,),
        in_specs=[row(d), row(dc), row(dc), full, pl.BlockSpec(memory_space=pl.ANY)],
        out_specs=[row(2 * dc), full],
        out_shape=[_sds((r, 2 * dc), F32), _sds((2 * dc, d), BF16)],
        scratch_shapes=[pltpu.VMEM((2 * dc, d), F32)],
        compiler_params=_params(("arbitrary",)),
    )(dp1, co, go, w_out, dep)


def _gla_bwd(u, dmi, sta, gup, gb, gn, dep, bsz, nc):
    r = u.shape[0]
    hv = GLA_HEADS * GLA_DV
    hk = GLA_HEADS * GLA_DK

    def body(qk_ref, v_ref, r_ref, gd_ref, dgo_ref, sta_ref, gup_ref, gb_ref, gn_ref, dep_ref,
             dqk_ref, dv_ref, dr_ref, dgd_ref, dgn_ref, dgb_ref, dgup_ref, dst_ref):
        bi = pl.program_id(0)
        t = pl.program_id(1)
        n = nc - 1 - t

        @pl.when(jnp.logical_and(bi == 0, t == 0))
        def _():
            dgn_ref[...] = jnp.zeros_like(dgn_ref)
            dgb_ref[...] = jnp.zeros_like(dgb_ref)
            dgup_ref[...] = jnp.zeros_like(dgup_ref)

        @pl.when(t == 0)
        def _():
            dst_ref[...] = jnp.zeros_like(dst_ref)

        gd = gd_ref[...]
        p = _gla_prep(qk_ref[...], gd, gup_ref[...], gb_ref[...], n)
        tril = p["ri"] >= p["ci"]
        dgn = jnp.zeros((1, GLA_DV), F32)
        dqe_t, dke_t, dkd_t, dgam_t = [], [], [], []
        for hp in range(GLA_HEADS // 2):
            ls = slice(hp * LANES, (hp + 1) * LANES)
            dqe = jnp.zeros((CHUNK, LANES), F32)
            dke = jnp.zeros((CHUNK, LANES), F32)
            dkd = jnp.zeros((CHUNK, LANES), F32)
            dgam = jnp.zeros((1, LANES), F32)
            for h2 in range(2):
                h = 2 * hp + h2
                vs = slice(h * GLA_DV, (h + 1) * GLA_DV)
                m = _head_mask(h2)
                qeh = jnp.where(m, p["qe"][:, ls], 0.0).astype(BF16)
                kdh = jnp.where(m, p["kd"][:, ls], 0.0).astype(BF16)
                keh = p["ke"][:, ls].astype(BF16)
                vh = v_ref[:, vs].astype(BF16)
                st = sta_ref[h]
                stb = st.astype(BF16)
                a = jnp.where(tril, _mm_nt(qeh, keh), 0.0)
                ab = a.astype(BF16)
                o = _mm(ab, vh) + _mm_nt(qeh, stb)
                rr = r_ref[:, vs]
                sr = _sigmoid(rr)
                rs = lax.rsqrt(jnp.mean(o * o, axis=-1, keepdims=True) + LN_EPS)
                y = o * rs
                dgo = dgo_ref[:, vs]
                don = dgo * (rr * sr)
                dr_ref[:, vs] = dgo * (y * gn_ref[...]) * (sr * (1.0 + rr * (1.0 - sr)))
                dgn = dgn + _rowsum(don * y)
                dxn = don * gn_ref[...]
                do = rs * (dxn - y * jnp.mean(dxn * y, axis=-1, keepdims=True))
                dob = do.astype(BF16)
                dst = dst_ref[h]
                dstb = dst.astype(BF16)
                da = jnp.where(tril, _mm_nt(dob, vh), 0.0).astype(BF16)
                dv_ref[:, vs] = _mm_tn(ab, dob) + _mm_nt(kdh, dstb)
                dqe = dqe + jnp.where(m, _mm(da, keh) + _mm(dob, stb), 0.0)
                dke = dke + _mm_tn(da, qeh)
                dkd = dkd + jnp.where(m, _mm(vh, dstb), 0.0)
                dgam = dgam + _rowsum(dst * st)
                dst_ref[h] = dst * p["gam"][:, ls] + _mm_tn(dob, qeh)
            dqe_t.append(dqe)
            dke_t.append(dke)
            dkd_t.append(dkd)
            dgam_t.append(dgam)
        dqe = jnp.concatenate(dqe_t, axis=1)
        dke = jnp.concatenate(dke_t, axis=1)
        dkd = jnp.concatenate(dkd_t, axis=1)
        dgam = jnp.concatenate(dgam_t, axis=1)
        dqk_ref[:, :hk] = dqe * p["eb"] * QK_SCALE
        dqk_ref[:, hk:] = dke * p["enb"] + dkd * p["ebl"]
        dkdkd = dkd * p["kd"]
        db = dqe * p["qe"] - dke * p["ke"] - dkdkd
        dbl = _rowsum(dkdkd) + dgam * p["gam"]
        upper = (p["ri"] <= p["ci"]).astype(BF16)
        dlg = _tri_mm(upper, db) + dbl
        dz = jnp.where(p["real"], dlg * (1.0 / GLA_TAU) * _sigmoid(-p["z"]), 0.0)
        dzb = dz.astype(BF16)
        dgb_ref[...] += _rowsum(dz)
        dgup_ref[...] += _mm_tn(gd.astype(BF16), dzb)
        dgd_ref[...] = _mm_nt(dzb, gup_ref[...])
        dgn_ref[...] += dgn

    rowblk = lambda col: (lambda b, t: (b * nc + nc - 1 - t, col))
    const = lambda b, t: (0, 0)
    return pl.pallas_call(
        body, name="gla_bwd", grid=(bsz, nc),
        in_specs=[pl.BlockSpec((CHUNK, 2 * hk), rowblk(2)), pl.BlockSpec((CHUNK, hv), rowblk(3)),
                  pl.BlockSpec((CHUNK, hv), rowblk(4)), pl.BlockSpec((CHUNK, LANES), rowblk(20)),
                  pl.BlockSpec((CHUNK, hv), rowblk(1)),
                  pl.BlockSpec((None, GLA_HEADS, LANES, LANES), lambda b, t: (b * nc + nc - 1 - t, 0, 0, 0)),
                  pl.BlockSpec((LANES, 256), const), pl.BlockSpec((1, 256), const), pl.BlockSpec((1, GLA_DV), const),
                  pl.BlockSpec(memory_space=pl.ANY)],
        out_specs=[pl.BlockSpec((CHUNK, 2 * hk), rowblk(0)), pl.BlockSpec((CHUNK, hv), rowblk(0)),
                   pl.BlockSpec((CHUNK, hv), rowblk(0)), pl.BlockSpec((CHUNK, LANES), rowblk(0)),
                   pl.BlockSpec((1, GLA_DV), const), pl.BlockSpec((1, 256), const), pl.BlockSpec((LANES, 256), const)],
        out_shape=[_sds((r, 2 * hk), F32), _sds((r, hv), F32), _sds((r, hv), F32), _sds((r, LANES), F32),
                   _sds((1, GLA_DV), F32), _sds((1, 256), F32), _sds((LANES, 256), F32)],
        scratch_shapes=[pltpu.VMEM((GLA_HEADS, LANES, LANES), F32)],
        compiler_params=_params(("arbitrary", "arbitrary")),
    )(u, u, u, u, dmi, sta, gup, gb, gn, dep)


def _conv_bwd(u, c, dmi, w32, cg, cbe, tp, tc, dc):
    r = u.shape[0]
    hb = tc // CONV_HALO
    nhalo = r // CONV_HALO

    def dconv(cv, dco, cg_ref, cbe_ref):
        xhat, rstd = _ln(cv)
        cn = xhat * cg_ref[...] + cbe_ref[...]
        sg = _sigmoid(cn)
        dcn = dco * (sg * (1.0 + cn * (1.0 - sg)))
        return _ln_bwd(dcn * cg_ref[...], xhat, rstd), dcn, xhat

    def body(a_ref, g_ref, ah_ref, gh_ref, c_ref, dco_ref, ch_ref, dcoh_ref, w_ref, cg_ref, cbe_ref,
             du_ref, dw_ref, dcb_ref, dcg_ref, dcbe_ref, hs_ref, dcs_ref):
        t = pl.program_id(0)

        @pl.when(t == 0)
        def _():
            dw_ref[...] = jnp.zeros_like(dw_ref)
            dcb_ref[...] = jnp.zeros_like(dcb_ref)
            dcg_ref[...] = jnp.zeros_like(dcg_ref)
            dcbe_ref[...] = jnp.zeros_like(dcbe_ref)

        first = lax.rem(t * tc, tp) == 0
        last = lax.rem((t + 1) * tc, tp) == 0
        hh = ah_ref[...] * _sigmoid(gh_ref[...])
        hs_ref[0:CONV_HALO, :] = jnp.where(first, 0.0, hh)
        hs_ref[CONV_HALO:CONV_HALO + tc, :] = a_ref[...] * _sigmoid(g_ref[...])
        dch, _, _ = dconv(ch_ref[...], dcoh_ref[...], cg_ref, cbe_ref)
        dcs_ref[tc:tc + CONV_HALO, :] = jnp.where(last, 0.0, dch)

        def sub1(k, carry):
            r0 = pl.multiple_of(k * CONV_SUB, CONV_SUB)
            dcv, dcn, xhat = dconv(c_ref[pl.ds(r0, CONV_SUB), :], dco_ref[pl.ds(r0, CONV_SUB), :], cg_ref, cbe_ref)
            dcs_ref[pl.ds(r0, CONV_SUB), :] = dcv
            dcb_ref[...] += _rowsum(dcv)
            dcg_ref[...] += _rowsum(dcn * xhat)
            dcbe_ref[...] += _rowsum(dcn)
            return carry

        lax.fori_loop(0, tc // CONV_SUB, sub1, 0)

        def sub2(k, carry):
            r0 = pl.multiple_of(k * CONV_SUB, CONV_SUB)
            dwin = dcs_ref[pl.ds(r0, CONV_WIN), :]
            dh = _conv_taps(dwin, lambda o: w_ref[CONV_WIDTH - 1 - o:CONV_WIDTH - o, :], 0)
            av = a_ref[pl.ds(r0, CONV_SUB), :]
            sg = _sigmoid(g_ref[pl.ds(r0, CONV_SUB), :])
            du_ref[pl.ds(r0, CONV_SUB), 0:dc] = dh * sg
            du_ref[pl.ds(r0, CONV_SUB), dc:2 * dc] = dh * av * sg * (1.0 - sg)
            hwin = hs_ref[pl.ds(r0, CONV_WIN), :]
            dcv = dwin[0:CONV_SUB, :]
            for rho in range(8):
                offs = [o for o in range(2, 2 + CONV_WIDTH) if o % 8 == rho]
                rolled = hwin if rho == 0 else pltpu.roll(hwin, CONV_WIN - rho, 0)
                for o in offs:
                    m8 = o - rho
                    dw_ref[o - 2:o - 1, :] += _rowsum(dcv * rolled[m8:m8 + CONV_SUB, :])
            return carry

        lax.fori_loop(0, tc // CONV_SUB, sub2, 0)

    vec = pl.BlockSpec((1, dc), lambda t: (0, 0))
    prev = lambda col: (lambda t: (jnp.maximum(t * hb - 1, 0), col))
    nxt = lambda col: (lambda t: (jnp.minimum((t + 1) * hb, nhalo - 1), col))
    return pl.pallas_call(
        body, name="conv_bwd", grid=(r // tc,),
        in_specs=[pl.BlockSpec((tc, dc), lambda t: (t, 0)), pl.BlockSpec((tc, dc), lambda t: (t, 1)),
                  pl.BlockSpec((CONV_HALO, dc), prev(0)), pl.BlockSpec((CONV_HALO, dc), prev(1)),
                  pl.BlockSpec((tc, dc), lambda t: (t, 0)), pl.BlockSpec((tc, dc), lambda t: (t, 0)),
                  pl.BlockSpec((CONV_HALO, dc), nxt(0)), pl.BlockSpec((CONV_HALO, dc), nxt(0)),
                  pl.BlockSpec((32, dc), lambda t: (0, 0)), vec, vec],
        out_specs=[pl.BlockSpec((tc, 2 * dc), lambda t: (t, 0)), pl.BlockSpec((32, dc), lambda t: (0, 0)), vec, vec, vec],
        out_shape=[_sds((r, 2 * dc), F32), _sds((32, dc), F32), _sds((1, dc), F32), _sds((1, dc), F32), _sds((1, dc), F32)],
        scratch_shapes=[pltpu.VMEM((CONV_HALO + tc, dc), F32), pltpu.VMEM((tc + CONV_HALO, dc), F32)],
        compiler_params=_params(("arbitrary",)),
    )(u, u, u, u, c, dmi, c, dmi, w32, cg, cbe)


def _inproj_bwd(dp1, dus, xcat, g_in, w_in, dep, tp, tm):
    r, d = dp1.shape
    widths = [x.shape[1] for x in dus]
    offs = [sum(widths[:k]) for k in range(len(widths))]
    n = w_in.shape[1]
    nd = len(dus)
    tps = tp // tm

    def body(*refs):
        dp_ref = refs[0]
        du_refs = refs[1:1 + nd]
        x_ref, g_ref, w_ref, _, dx_ref, dmeta_ref, dg_ref, db_ref = refs[1 + nd:]
        i = pl.program_id(0)

        @pl.when(i == 0)
        def _():
            dmeta_ref[...] = jnp.zeros_like(dmeta_ref)
            dg_ref[...] = jnp.zeros_like(dg_ref)
            db_ref[...] = jnp.zeros_like(db_ref)

        ds0 = ALPHA * dp_ref[...]
        for k in range(nd):
            ds0 = ds0 + _mm_nt(du_refs[k][...].astype(BF16), w_ref[:, offs[k]:offs[k] + widths[k]])
        real = _row_in_seq(i, tm, tp) >= PAD_FRONT
        ds0 = jnp.where(real, ds0, 0.0)
        xhat, rstd = _ln(x_ref[...])
        dg_ref[...] += _rowsum(ds0 * xhat)
        db_ref[...] += _rowsum(ds0)
        dx = jnp.where(real, _ln_bwd(ds0 * g_ref[...], xhat, rstd), 0.0)
        dx_ref[...] = dx

        @pl.when(lax.rem(i, tps) == 0)
        def _():
            dmeta_ref[...] += dx[PAD_FRONT:X_OFF, :]

    row = lambda w: pl.BlockSpec((tm, w), lambda i: (i, 0))
    vec = pl.BlockSpec((1, d), lambda i: (0, 0))
    return pl.pallas_call(
        body, name="inproj_bwd", grid=(r // tm,),
        in_specs=[row(d)] + [row(w) for w in widths] + [row(d), vec, pl.BlockSpec((d, n), lambda i: (0, 0)),
                                                        pl.BlockSpec(memory_space=pl.ANY)],
        out_specs=[row(d), pl.BlockSpec((N_META, d), lambda i: (0, 0)), vec, vec],
        out_shape=[_sds((r, d), F32), _sds((N_META, d), F32), _sds((1, d), F32), _sds((1, d), F32)],
        compiler_params=_params(("arbitrary",)),
    )(dp1, *dus, xcat, g_in, w_in, dep)


def _inproj_bwd_w(s0, dus, tm):
    r, d = s0.shape
    widths = [x.shape[1] for x in dus]
    nd = len(dus)

    def body(*refs):
        s_ref = refs[0]
        du_refs = refs[1:1 + nd]
        dw_refs = refs[1 + nd:]
        i = pl.program_id(0)

        @pl.when(i == 0)
        def _():
            for k in range(nd):
                dw_refs[k][...] = jnp.zeros_like(dw_refs[k])

        sb = s_ref[...].astype(BF16)
        for k in range(nd):
            dw_refs[k][...] += _mm_tn(sb, du_refs[k][...].astype(BF16))

    row = lambda w: pl.BlockSpec((tm, w), lambda i: (i, 0))
    return pl.pallas_call(
        body, name="inproj_bwd_w", grid=(r // tm,),
        in_specs=[row(d)] + [row(w) for w in widths],
        out_specs=[pl.BlockSpec((d, w), lambda i: (0, 0)) for w in widths],
        out_shape=[_sds((d, w), F32) for w in widths],
        compiler_params=_params(("arbitrary",)),
    )(s0, *dus)


def _local_step(x, tgt, meta, ln_in_g, ln_in_b, w_in, conv_w, conv_b, conv_ln_g, conv_ln_b, gate_up, gate_bias,
                gla_norm_g, late_weights, ln1_g, ln1_b, ln2_g, ln2_b, push):
    bsz, seq, d = x.shape
    tp = X_OFF + seq
    assert tp % CHUNK == 0
    nc = tp // CHUNK
    r = bsz * tp
    dc = conv_b.shape[1]
    tm = _pick_tile(tp, (352, 128, 64))
    tc = _pick_tile(tp, (704, 128, 64))

    xcat = jnp.concatenate([jnp.zeros((bsz, PAD_FRONT, d), F32), jnp.broadcast_to(meta[None], (bsz, N_META, d)), x],
                           axis=1).reshape(r, d)
    tgt_p = jnp.pad(tgt, ((0, 0), (X_OFF, 0), (0, 0))).reshape(r, d)
    w32 = jnp.pad(conv_w, ((0, 32 - CONV_WIDTH), (0, 0)))
    gup = jnp.pad(gate_up, ((0, LANES - GLA_RANK), (0, 0))).astype(BF16)

    s0, u = _inproj_fwd(xcat, ln_in_g, ln_in_b, w_in, tp, tm)
    c, co = _conv_fwd(u, w32, conv_b, conv_ln_g, conv_ln_b, tp, tc, dc)
    go, sta = _gla_fwd(u, gup, gate_bias, gla_norm_g, bsz, nc)
    w_out, w1g, w2 = late_weights(go)
    nh = w1g.shape[0]
    tmm = _pick_tile(tp, (704, 128, 64))
    ns = 2
    p1, s1, s1b = _outproj_fwd(s0, co, go, w_out, ln1_g, ln1_b, tm)
    hm, dp2, dpb, loss, dg2, db2 = _mlp_fwd(s1, s1b, w1g, w2, ln2_g, ln2_b, tgt_p, tp, tmm, ns)

    dh, dp1, dg1, db1 = _mlp_bwd_act(dp2, dpb, hm, w1g, w2, p1, ln1_g, tmm, ns)
    dw1, dw2 = _mlp_bwd_w(s1b, hm, dh, dpb, nh, tmm, ns)
    tok = push("ff", (dw1, dw2))
    dmi, dwo = _outproj_bwd(dp1, co, go, w_out, tok, tm)
    tok = push("out", (dwo,))
    dqk, dv, dr, dgd, dgn, dgb, dgup = _gla_bwd(u, dmi, sta, gup, gate_bias, gla_norm_g, tok, bsz, nc)
    dcv, dcw, dcb, dcg, dcbe = _conv_bwd(u, c, dmi, w32, conv_ln_g, conv_ln_b, tp, tc, dc)
    dus = [dcv, dqk, dv, dr, dgd]
    dwi = _inproj_bwd_w(s0, dus, tm)
    tok = push("in", (jnp.concatenate(dwi, axis=1),))
    dxcat, dmeta, dgi, dbi = _inproj_bwd(dp1, dus, xcat, ln_in_g, w_in, tok, tp, tm)

    grad_x = dxcat.reshape(bsz, tp, d)[:, X_OFF:, :]
    return dict(loss=loss[0, 0], grad_x=grad_x, meta_tokens=dmeta, ln_in_g=dgi, ln_in_b=dbi,
                conv_w=dcw[:CONV_WIDTH], conv_b=dcb, conv_ln_g=dcg, conv_ln_b=dcbe,
                gate_up=dgup[:GLA_RANK], gate_bias=dgb, gla_norm_g=dgn, ln1_g=dg1, ln1_b=db1, ln2_g=dg2, ln2_b=db2)


def _exchange(arrays, scatter, name):
    na = len(arrays)
    npeer = N_DEV - 1

    def body(*refs):
        srcs = refs[:na]
        outs = refs[na:2 * na]
        send_sems, recv_sems, local_sems = refs[2 * na:]
        xi, yi, ci = (lax.axis_index(a) for a in MESH_AXES)
        me = 4 * xi + 2 * yi + ci
        copies = []
        for a in range(na):
            own = srcs[a].at[me] if scatter[a] else srcs[a]
            cp = pltpu.make_async_copy(own, outs[a].at[me], local_sems.at[a])
            cp.start()
            copies.append(cp)
        remote = []
        for k in range(1, N_DEV):
            px, py, pc = xi ^ (k >> 2), yi ^ ((k >> 1) & 1), ci ^ (k & 1)
            peer = 4 * px + 2 * py + pc
            for a in range(na):
                src = srcs[a].at[peer] if scatter[a] else srcs[a]
                cp = pltpu.make_async_remote_copy(
                    src_ref=src, dst_ref=outs[a].at[me],
                    send_sem=send_sems.at[a * npeer + k - 1], recv_sem=recv_sems.at[a * npeer + k - 1],
                    device_id=(px, py, pc), device_id_type=pl.DeviceIdType.MESH)
                cp.start()
                remote.append(cp)
        for cp in remote:
            cp.wait()
        for cp in copies:
            cp.wait()

    out_shape = [_sds(a.shape if scatter[i] else (N_DEV,) + a.shape, a.dtype) for i, a in enumerate(arrays)]
    anyspec = pl.BlockSpec(memory_space=pl.ANY)
    return pl.pallas_call(
        body, name=name,
        in_specs=[anyspec] * na, out_specs=[anyspec] * na, out_shape=out_shape,
        scratch_shapes=[pltpu.SemaphoreType.DMA((na * npeer,)), pltpu.SemaphoreType.DMA((na * npeer,)),
                        pltpu.SemaphoreType.DMA((na,))],
    )(*arrays)


def _peers(xi, yi, ci):
    for k in range(1, N_DEV):
        px, py, pc = xi ^ (k >> 2), yi ^ ((k >> 1) & 1), ci ^ (k & 1)
        yield (px, py, pc), 4 * px + 2 * py + pc


def _push_start(arrays, scatter, name, dep=None):
    na = len(arrays)
    shapes = [a.shape if scatter[i] else (N_DEV,) + a.shape for i, a in enumerate(arrays)]
    hbm = pl.BlockSpec(memory_space=pltpu.HBM)
    sem = pl.BlockSpec(memory_space=pltpu.SEMAPHORE)
    ndep = 0 if dep is None else 1

    def body(*refs):
        srcs = refs[:na]
        lands = refs[na:2 * na]
        send_sems, recv_sems = refs[2 * na + ndep:2 * na + ndep + 2]
        own_sems = refs[4 * na + ndep + 2]
        xi, yi, ci = (lax.axis_index(a) for a in MESH_AXES)
        me = 4 * xi + 2 * yi + ci
        own = [pltpu.make_async_copy(srcs[a].at[me] if scatter[a] else srcs[a], lands[a].at[me], own_sems.at[a])
               for a in range(na)]
        for cp in own:
            cp.start()
        for cp in own:
            cp.wait()
        for a in range(na):
            for pos, peer in _peers(xi, yi, ci):
                pltpu.make_async_remote_copy(
                    src_ref=srcs[a].at[peer] if scatter[a] else srcs[a], dst_ref=lands[a].at[me],
                    send_sem=send_sems.at[a], recv_sem=recv_sems.at[a],
                    device_id=pos, device_id_type=pl.DeviceIdType.MESH).start()

    ins = [pltpu.with_memory_space_constraint(a, pltpu.HBM) for a in arrays]
    ins += [pltpu.with_memory_space_constraint(lax.empty(s, a.dtype), pltpu.HBM) for s, a in zip(shapes, arrays)]
    res = pl.pallas_call(
        body, name=name,
        in_specs=[hbm] * (2 * na) + [pl.BlockSpec(memory_space=pl.ANY)] * ndep,
        out_specs=[sem, sem] + [hbm] * (2 * na),
        out_shape=[pltpu.SemaphoreType.DMA((na,)), pltpu.SemaphoreType.DMA((na,))]
                  + [pltpu.HBM(a.shape, a.dtype) for a in arrays] + [pltpu.HBM(s, a.dtype) for s, a in zip(shapes, arrays)],
        input_output_aliases={i: 2 + i for i in range(2 * na)},
        scratch_shapes=[pltpu.SemaphoreType.DMA((na,))],
        compiler_params=pltpu.CompilerParams(has_side_effects=pltpu.SideEffectType.DATAFLOW_SIDE_EFFECTING),
    )(*ins, *([] if dep is None else [dep]))
    return (res[0], res[1], list(res[2:2 + na]), list(res[2 + na:2 + 2 * na])), res[2]


def _push_wait(handle, after, name):
    send_sems, recv_sems, srcs, lands = handle
    na = len(srcs)
    hbm = pl.BlockSpec(memory_space=pltpu.HBM)
    sem = pl.BlockSpec(memory_space=pltpu.SEMAPHORE)

    def body(*refs):
        land_refs = refs[na:2 * na]
        send_ref, recv_ref = refs[2 * na:2 * na + 2]
        me = tuple(lax.axis_index(a) for a in MESH_AXES)
        for a in range(na):
            seven = land_refs[a].at[pl.ds(0, N_DEV - 1)]
            cp = pltpu.make_async_remote_copy(src_ref=seven, dst_ref=seven, send_sem=send_ref.at[a], recv_sem=recv_ref.at[a],
                                              device_id=me, device_id_type=pl.DeviceIdType.MESH)
            cp.wait_send()
            cp.wait_recv()

    res = pl.pallas_call(
        body, name=name,
        in_specs=[hbm] * (2 * na) + [sem, sem, pl.BlockSpec(memory_space=pl.ANY)],
        out_specs=[hbm] * (2 * na),
        out_shape=[pltpu.HBM(a.shape, a.dtype) for a in srcs] + [pltpu.HBM(a.shape, a.dtype) for a in lands],
        input_output_aliases={i: i for i in range(2 * na)},
        compiler_params=pltpu.CompilerParams(has_side_effects=pltpu.SideEffectType.DATAFLOW_SIDE_EFFECTING),
    )(*srcs, *lands, send_sems, recv_sems, after)
    return list(res[na:])


def _adamw(w, g, m, v):
    m = ADAM_B1 * m + (1.0 - ADAM_B1) * g
    v = ADAM_B2 * v + (1.0 - ADAM_B2) * jnp.square(g)
    m_hat = m / (1.0 - ADAM_B1 ** ADAM_STEP)
    v_hat = v / (1.0 - ADAM_B2 ** ADAM_STEP)
    delta = -ADAM_LR * (m_hat / (jnp.sqrt(v_hat) + ADAM_EPS) + ADAM_WD * w)
    return delta, m, v


def _sum_devices(ref):
    g = ref[0].astype(F32)
    for k in range(1, N_DEV):
        g = g + ref[k].astype(F32)
    return g


def _update_big(parts, w, m, v, name):
    rows, cols = w.shape
    tr = _pick_tile(rows, (128, 64, 16))

    def body(p_ref, w_ref, m_ref, v_ref, g_ref, d_ref, nm_ref, nv_ref):
        g = _sum_devices(p_ref)
        g_ref[...] = g
        d_ref[...], nm_ref[...], nv_ref[...] = _adamw(w_ref[...], g, m_ref[...], v_ref[...])

    blk = pl.BlockSpec((tr, cols), lambda i: (i, 0))
    return pl.pallas_call(
        body, name=name, grid=(rows // tr,),
        in_specs=[pl.BlockSpec((N_DEV, tr, cols), lambda i: (0, i, 0)), blk, blk, blk],
        out_specs=[blk] * 4, out_shape=[_sds((rows, cols), F32)] * 4,
        compiler_params=_params(("parallel",)),
    )(parts, w, m, v)


_VEC_ORDER = ("ln_in_g", "ln_in_b", "conv_b", "conv_ln_g", "conv_ln_b", "gate_bias", "gla_norm_g",
              "ln1_g", "ln1_b", "ln2_g", "ln2_b")
_SHARDED_SMALL = (("meta_tokens", 0, N_META, LANES), ("conv_w", N_META, CONV_WIDTH, None), ("gate_up", N_META + 32, GLA_RANK, None))


def _update_small(parts_sh, parts_vec, wmv):
    names = [s[0] for s in _SHARDED_SMALL] + list(_VEC_ORDER)
    flat = [a for nme in names for a in wmv[nme]]

    def body(*refs):
        sh_ref, vec_ref = refs[0], refs[1]
        ins = refs[2:2 + len(flat)]
        outs = refs[2 + len(flat):2 + len(flat) + 4 * len(names)]
        gsh_ref, gvec_ref = refs[-2:]
        gsh_ref[...] = _sum_devices(sh_ref)
        gvec_ref[...] = _sum_devices(vec_ref)
        for idx, nme in enumerate(names):
            w_ref, m_ref, v_ref = ins[3 * idx:3 * idx + 3]
            rows, cols = w_ref.shape
            if idx < len(_SHARDED_SMALL):
                r0 = _SHARDED_SMALL[idx][1]
                g = gsh_ref[r0:r0 + rows, 0:cols]
            else:
                j = idx - len(_SHARDED_SMALL)
                g = gvec_ref[j:j + 1, 0:cols]
            o = outs[4 * idx:4 * idx + 4]
            o[0][...] = g
            o[1][...], o[2][...], o[3][...] = _adamw(w_ref[...], g, m_ref[...], v_ref[...])

    out_shape = [_sds(wmv[nme][0].shape, F32) for nme in names for _ in range(4)]
    vmem = pl.BlockSpec(memory_space=pltpu.VMEM)
    res = pl.pallas_call(
        body, name="update_small", out_shape=out_shape,
        in_specs=[vmem] * (2 + len(flat)), out_specs=[vmem] * len(out_shape),
        scratch_shapes=[pltpu.VMEM(parts_sh.shape[1:], F32), pltpu.VMEM(parts_vec.shape[1:], F32)],
    )(parts_sh, parts_vec, *flat)
    return {nme: res[4 * i:4 * i + 4] for i, nme in enumerate(names)}


_WEIGHTS = ("meta_tokens", "ln_in_g", "ln_in_b", "w_in", "conv_w", "conv_b", "conv_ln_g", "conv_ln_b", "gate_up",
            "gate_bias", "gla_norm_g", "w_out", "ln1_g", "ln1_b", "w_ff1", "w_ff2", "ln2_g", "ln2_b")


def kernel(x, meta_tokens, ln_in_g, ln_in_b, w_in, conv_w, conv_b, conv_ln_g, conv_ln_b, gate_up, gate_bias, gla_norm_g, w_out, ln1_g, ln1_b, w_ff1, w_ff2, ln2_g, ln2_b, loss_target, m_meta_tokens, m_ln_in_g, m_ln_in_b, m_w_in, m_conv_w, m_conv_b, m_conv_ln_g, m_conv_ln_b, m_gate_up, m_gate_bias, m_gla_norm_g, m_w_out, m_ln1_g, m_ln1_b, m_w_ff1, m_w_ff2, m_ln2_g, m_ln2_b, v_meta_tokens, v_ln_in_g, v_ln_in_b, v_w_in, v_conv_w, v_conv_b, v_conv_ln_g, v_conv_ln_b, v_gate_up, v_gate_bias, v_gla_norm_g, v_w_out, v_ln1_g, v_ln1_b, v_w_ff1, v_w_ff2, v_ln2_g, v_ln2_b):
    w = dict(meta_tokens=meta_tokens, ln_in_g=ln_in_g, ln_in_b=ln_in_b, w_in=w_in, conv_w=conv_w, conv_b=conv_b,
             conv_ln_g=conv_ln_g, conv_ln_b=conv_ln_b, gate_up=gate_up, gate_bias=gate_bias, gla_norm_g=gla_norm_g,
             w_out=w_out, ln1_g=ln1_g, ln1_b=ln1_b, w_ff1=w_ff1, w_ff2=w_ff2, ln2_g=ln2_g, ln2_b=ln2_b)
    mom = dict(meta_tokens=m_meta_tokens, ln_in_g=m_ln_in_g, ln_in_b=m_ln_in_b, w_in=m_w_in, conv_w=m_conv_w,
               conv_b=m_conv_b, conv_ln_g=m_conv_ln_g, conv_ln_b=m_conv_ln_b, gate_up=m_gate_up, gate_bias=m_gate_bias,
               gla_norm_g=m_gla_norm_g, w_out=m_w_out, ln1_g=m_ln1_g, ln1_b=m_ln1_b, w_ff1=m_w_ff1, w_ff2=m_w_ff2,
               ln2_g=m_ln2_g, ln2_b=m_ln2_b)
    var = dict(meta_tokens=v_meta_tokens, ln_in_g=v_ln_in_g, ln_in_b=v_ln_in_b, w_in=v_w_in, conv_w=v_conv_w,
               conv_b=v_conv_b, conv_ln_g=v_conv_ln_g, conv_ln_b=v_conv_ln_b, gate_up=v_gate_up, gate_bias=v_gate_bias,
               gla_norm_g=v_gla_norm_g, w_out=v_w_out, ln1_g=v_ln1_g, ln1_b=v_ln1_b, w_ff1=v_w_ff1, w_ff2=v_w_ff2,
               ln2_g=v_ln2_g, ln2_b=v_ln2_b)
    shapes = {k: a.shape for k, a in w.items()}

    def two_d(a):
        return a.reshape(1, -1) if a.ndim == 1 else a.reshape(a.shape[-2:])

    w2d = {k: two_d(a) for k, a in w.items()}
    m2d = {k: two_d(a) for k, a in mom.items()}
    v2d = {k: two_d(a) for k, a in var.items()}
    d = x.shape[-1]
    d_in = w2d["w_in"].shape[1] * N_DEV
    d_in_p = -(-d_in // LANES) * LANES

    first, tok_first = _push_start(
        [w2d["w_in"].astype(BF16), w2d["meta_tokens"], w2d["conv_w"], w2d["gate_up"]], [False] * 4, "gather_first_start")
    late, tok_late = _push_start(
        [w2d["w_out"].astype(BF16), w2d["w_ff1"].astype(BF16), w2d["w_ff2"].astype(BF16)], [False] * 3,
        "gather_late_start", dep=tok_first)
    g_in, g_meta, g_conv, g_gup = _push_wait(first, tok_late, "gather_first_wait")
    w_in_full = jnp.pad(g_in.transpose(1, 0, 2).reshape(d, d_in), ((0, 0), (0, d_in_p - d_in)))
    meta_full = g_meta.transpose(1, 0, 2).reshape(N_META, d)
    conv_w_full = g_conv.transpose(1, 0, 2).reshape(CONV_WIDTH, -1)
    gate_up_full = g_gup.transpose(1, 0, 2).reshape(GLA_RANK, -1)

    def late_weights(after):
        g_out, g_ff1, g_ff2 = _push_wait(late, after, "gather_late_wait")
        return g_out.reshape(-1, d), g_ff1, g_ff2.reshape(-1, d)

    pushed = {}

    def push(tag, grads):
        if tag == "in":
            grads = (grads[0][:, :d_in].reshape(d, N_DEV, d_in // N_DEV).transpose(1, 0, 2).astype(BF16),)
        elif tag == "out":
            grads = (grads[0].reshape(N_DEV, -1, d),)
        pushed[tag], tok = _push_start(list(grads), [True] * len(grads), f"scatter_{tag}_start")
        return tok

    res = _local_step(x, loss_target, meta_full, w2d["ln_in_g"], w2d["ln_in_b"], w_in_full, conv_w_full, w2d["conv_b"],
                      w2d["conv_ln_g"], w2d["conv_ln_b"], gate_up_full, w2d["gate_bias"], w2d["gla_norm_g"], late_weights,
                      w2d["ln1_g"], w2d["ln1_b"], w2d["ln2_g"], w2d["ln2_b"], push)

    dc = res["conv_w"].shape[1]
    hk = res["gate_up"].shape[1]
    sh_meta = res["meta_tokens"].reshape(N_META, N_DEV, LANES).transpose(1, 0, 2)
    sh_conv = jnp.pad(res["conv_w"].reshape(CONV_WIDTH, N_DEV, dc // N_DEV).transpose(1, 0, 2),
                      ((0, 0), (0, 32 - CONV_WIDTH), (0, LANES - dc // N_DEV)))
    sh_gup = jnp.pad(res["gate_up"].reshape(GLA_RANK, N_DEV, hk // N_DEV).transpose(1, 0, 2),
                     ((0, 0), (0, 0), (0, LANES - hk // N_DEV)))
    p_sh = jnp.concatenate([sh_meta, sh_conv, sh_gup], axis=1)
    p_vec = jnp.concatenate([jnp.pad(res[k], ((0, 0), (0, d - res[k].shape[1]))) for k in _VEC_ORDER]
                            + [jnp.zeros((16 - len(_VEC_ORDER), d), F32)], axis=0)

    r_sh, r_vec = _exchange([p_sh, p_vec], [True, False], "scatter_small")
    r_ff1, r_ff2 = _push_wait(pushed["ff"], r_vec, "scatter_ff_wait")
    (r_out,) = _push_wait(pushed["out"], r_ff1, "scatter_out_wait")
    (r_in,) = _push_wait(pushed["in"], r_out, "scatter_in_wait")

    upd = {}
    upd["w_in"] = _update_big(r_in, w2d["w_in"], m2d["w_in"], v2d["w_in"], "update_w_in")
    upd["w_out"] = _update_big(r_out, w2d["w_out"], m2d["w_out"], v2d["w_out"], "update_w_out")
    upd["w_ff1"] = _update_big(r_ff1, w2d["w_ff1"], m2d["w_ff1"], v2d["w_ff1"], "update_w_ff1")
    upd["w_ff2"] = _update_big(r_ff2, w2d["w_ff2"], m2d["w_ff2"], v2d["w_ff2"], "update_w_ff2")
    small = [s[0] for s in _SHARDED_SMALL] + list(_VEC_ORDER)
    upd.update(_update_small(r_sh, r_vec, {k: (w2d[k], m2d[k], v2d[k]) for k in small}))

    loss = lax.psum(res["loss"], MESH_AXES)
    outs = [loss, res["grad_x"]]
    for j in range(4):
        outs += [upd[k][j].reshape(shapes[k]) for k in _WEIGHTS]
    return tuple(outs)
```

```python
import functools

import jax
import jax.numpy as jnp
from jax import lax
from jax.experimental import pallas as pl
from jax.experimental.pallas import tpu as pltpu
from jax.experimental.pallas import tpu_sc as plsc

F32 = jnp.float32
BF16 = jnp.bfloat16

N_META = 16
CHUNK = 64
PAD_FRONT = (-N_META) % CHUNK
X_OFF = PAD_FRONT + N_META
CONV_WIDTH = 31
CONV_HALO = 32
CONV_SUB = 64
CONV_WIN = CONV_SUB + CONV_HALO
GLA_HEADS = 4
GLA_DK = 64
GLA_DV = 128
GLA_RANK = 16
GLA_TAU = 16.0
QK_SCALE = GLA_DK ** -0.5
LN_EPS = 1e-5
ALPHA = 2.0 ** 0.25
LANES = 128
N_DEV = 8
ADAM_LR = 0.001
ADAM_B1 = 0.9
ADAM_B2 = 0.999
ADAM_EPS = 1e-08
ADAM_WD = 0.01
ADAM_STEP = 10
VMEM_LIMIT = 56 * 1024 * 1024
MESH_AXES = ("x", "y", "c")


def _sds(shape, dtype):
    return jax.ShapeDtypeStruct(shape, dtype)


def _mm(a, b):
    return jnp.dot(a, b, preferred_element_type=F32)


def _mm_nt(a, b):
    return lax.dot_general(a, b, (((1,), (1,)), ((), ())), preferred_element_type=F32)


def _mm_tn(a, b):
    return lax.dot_general(a, b, (((0,), (0,)), ((), ())), preferred_element_type=F32)


def _sigmoid(x):
    return 1.0 / (1.0 + jnp.exp(-x))


def _log_sigmoid(z):
    return jnp.minimum(z, 0.0) - jnp.log(1.0 + jnp.exp(-jnp.abs(z)))


def _ln(x):
    mu = jnp.mean(x, axis=-1, keepdims=True)
    xc = x - mu
    var = jnp.mean(xc * xc, axis=-1, keepdims=True)
    rstd = lax.rsqrt(var + LN_EPS)
    return xc * rstd, rstd


def _ln_bwd(dyg, xhat, rstd):
    m1 = jnp.mean(dyg, axis=-1, keepdims=True)
    m2 = jnp.mean(dyg * xhat, axis=-1, keepdims=True)
    return rstd * (dyg - m1 - xhat * m2)


def _rowsum(x):
    return jnp.sum(x, axis=0, keepdims=True)


def _row_in_seq(i, tm, tp):
    base = lax.rem(i * tm, tp)
    return base + lax.broadcasted_iota(jnp.int32, (tm, 1), 0)


def _split3(x):
    hi = x.astype(BF16)
    r1 = x - hi.astype(F32)
    mid = r1.astype(BF16)
    lo = (r1 - mid.astype(F32)).astype(BF16)
    return hi, mid, lo


def _tri_mm(tri, x):
    hi, mid, lo = _split3(x)
    return _mm(tri, hi) + _mm(tri, mid) + _mm(tri, lo)


def _params(sem):
    return pltpu.CompilerParams(dimension_semantics=sem, vmem_limit_bytes=VMEM_LIMIT)


def _pick_tile(n, prefs):
    for t in prefs:
        if n % t == 0:
            return t
    raise ValueError(f"no tile for {n}")


def _inproj_fwd(xcat, g, b, w_in, tp, tm):
    r, d = xcat.shape
    n = w_in.shape[1]

    def body(x_ref, g_ref, b_ref, w_ref, s0_ref, u_ref):
        i = pl.program_id(0)
        xhat, _ = _ln(x_ref[...])
        real = _row_in_seq(i, tm, tp) >= PAD_FRONT
        s = jnp.where(real, xhat * g_ref[...] + b_ref[...], 0.0)
        s0_ref[...] = s
        u_ref[...] = _mm(s.astype(BF16), w_ref[...])

    return pl.pallas_call(
        body, name="inproj_fwd", grid=(r // tm,),
        in_specs=[pl.BlockSpec((tm, d), lambda i: (i, 0)), pl.BlockSpec((1, d), lambda i: (0, 0)),
                  pl.BlockSpec((1, d), lambda i: (0, 0)), pl.BlockSpec((d, n), lambda i: (0, 0))],
        out_specs=[pl.BlockSpec((tm, d), lambda i: (i, 0)), pl.BlockSpec((tm, n), lambda i: (i, 0))],
        out_shape=[_sds((r, d), F32), _sds((r, n), F32)],
        compiler_params=_params(("parallel",)),
    )(xcat, g, b, w_in)


def _conv_taps(win, coef, lo):
    acc = None
    for rho in range(8):
        offs = [o for o in range(lo, lo + CONV_WIDTH) if o % 8 == rho]
        if not offs:
            continue
        rolled = win if rho == 0 else pltpu.roll(win, CONV_WIN - rho, 0)
        for o in offs:
            m8 = o - rho
            term = rolled[m8:m8 + CONV_SUB, :] * coef(o)
            acc = term if acc is None else acc + term
    return acc


def _conv_fwd(u, w32, cb, cg, cbe, tp, tc, dc):
    r = u.shape[0]
    hb = tc // CONV_HALO

    def body(a_ref, g_ref, ah_ref, gh_ref, w_ref, cb_ref, cg_ref, cbe_ref, c_ref, co_ref, hs_ref):
        t = pl.program_id(0)
        first = lax.rem(t * tc, tp) == 0
        hh = ah_ref[...] * _sigmoid(gh_ref[...])
        hs_ref[0:CONV_HALO, :] = jnp.where(first, 0.0, hh)
        hs_ref[CONV_HALO:CONV_HALO + tc, :] = a_ref[...] * _sigmoid(g_ref[...])

        def sub(k, carry):
            r0 = pl.multiple_of(k * CONV_SUB, CONV_SUB)
            win = hs_ref[pl.ds(r0, CONV_WIN), :]
            c = _conv_taps(win, lambda o: w_ref[o - 2:o - 1, :], 2) + cb_ref[...]
            c_ref[pl.ds(r0, CONV_SUB), :] = c
            xhat, _ = _ln(c)
            cn = xhat * cg_ref[...] + cbe_ref[...]
            co_ref[pl.ds(r0, CONV_SUB), :] = (cn * _sigmoid(cn)).astype(BF16)
            return carry

        lax.fori_loop(0, tc // CONV_SUB, sub, 0)

    vec = pl.BlockSpec((1, dc), lambda t: (0, 0))
    return pl.pallas_call(
        body, name="conv_fwd", grid=(r // tc,),
        in_specs=[pl.BlockSpec((tc, dc), lambda t: (t, 0)), pl.BlockSpec((tc, dc), lambda t: (t, 1)),
                  pl.BlockSpec((CONV_HALO, dc), lambda t: (jnp.maximum(t * hb - 1, 0), 0)),
                  pl.BlockSpec((CONV_HALO, dc), lambda t: (jnp.maximum(t * hb - 1, 0), 1)),
                  pl.BlockSpec((32, dc), lambda t: (0, 0)), vec, vec, vec],
        out_specs=[pl.BlockSpec((tc, dc), lambda t: (t, 0)), pl.BlockSpec((tc, dc), lambda t: (t, 0))],
        out_shape=[_sds((r, dc), F32), _sds((r, dc), BF16)],
        scratch_shapes=[pltpu.VMEM((CONV_HALO + tc, dc), F32)],
        compiler_params=_params(("parallel",)),
    )(u, u, u, u, w32, cb, cg, cbe)


def _gla_prep(qk, gd, gup, gb, n):
    z = _mm(gd.astype(BF16), gup) + gb
    lg = _log_sigmoid(z) * (1.0 / GLA_TAU)
    row = n * CHUNK + lax.broadcasted_iota(jnp.int32, (CHUNK, 1), 0)
    real = row >= PAD_FRONT
    lg = jnp.where(real, lg, 0.0)
    ri = lax.broadcasted_iota(jnp.int32, (CHUNK, CHUNK), 0)
    ci = lax.broadcasted_iota(jnp.int32, (CHUNK, CHUNK), 1)
    low = (ri >= ci).astype(BF16)
    b = _tri_mm(low, lg)
    bl = _rowsum(lg)
    hk = GLA_HEADS * GLA_DK
    q = qk[:, :hk] * QK_SCALE
    k = qk[:, hk:]
    eb = jnp.exp(b)
    enb = jnp.exp(-b)
    ebl = jnp.exp(bl - b)
    gam = jnp.exp(bl)
    return dict(z=z, real=real, ri=ri, ci=ci, eb=eb, enb=enb, ebl=ebl, gam=gam, k=k,
                qe=q * eb, ke=k * enb, kd=k * ebl)


def _head_mask(h2):
    lane = lax.broadcasted_iota(jnp.int32, (1, LANES), 1)
    return (lane < GLA_DK) if h2 == 0 else (lane >= GLA_DK)


def _gla_fwd(u, gup, gb, gn, bsz, nc):
    r = u.shape[0]
    hv = GLA_HEADS * GLA_DV

    def body(qk_ref, v_ref, r_ref, gd_ref, gup_ref, gb_ref, gn_ref, go_ref, sta_ref, st_ref):
        n = pl.program_id(1)

        @pl.when(n == 0)
        def _():
            st_ref[...] = jnp.zeros_like(st_ref)

        p = _gla_prep(qk_ref[...], gd_ref[...], gup_ref[...], gb_ref[...], n)
        tril = p["ri"] >= p["ci"]
        for h in range(GLA_HEADS):
            hp, h2 = divmod(h, 2)
            ls = slice(hp * LANES, (hp + 1) * LANES)
            m = _head_mask(h2)
            qeh = jnp.where(m, p["qe"][:, ls], 0.0).astype(BF16)
            kdh = jnp.where(m, p["kd"][:, ls], 0.0).astype(BF16)
            keh = p["ke"][:, ls].astype(BF16)
            vh = v_ref[:, h * GLA_DV:(h + 1) * GLA_DV].astype(BF16)
            st = st_ref[h]
            sta_ref[h] = st
            a = jnp.where(tril, _mm_nt(qeh, keh), 0.0)
            o = _mm(a.astype(BF16), vh) + _mm_nt(qeh, st.astype(BF16))
            st_ref[h] = st * p["gam"][:, ls] + _mm_tn(vh, kdh)
            rs = lax.rsqrt(jnp.mean(o * o, axis=-1, keepdims=True) + LN_EPS)
            rr = r_ref[:, h * GLA_DV:(h + 1) * GLA_DV]
            go = o * rs * gn_ref[...] * (rr * _sigmoid(rr))
            go_ref[:, h * GLA_DV:(h + 1) * GLA_DV] = go.astype(BF16)

    rowblk = lambda col: (lambda b, n: (b * nc + n, col))
    const = lambda b, n: (0, 0)
    return pl.pallas_call(
        body, name="gla_fwd", grid=(bsz, nc),
        in_specs=[pl.BlockSpec((CHUNK, 512), rowblk(2)), pl.BlockSpec((CHUNK, hv), rowblk(3)),
                  pl.BlockSpec((CHUNK, hv), rowblk(4)), pl.BlockSpec((CHUNK, LANES), rowblk(20)),
                  pl.BlockSpec((LANES, 256), const), pl.BlockSpec((1, 256), const), pl.BlockSpec((1, GLA_DV), const)],
        out_specs=[pl.BlockSpec((CHUNK, hv), rowblk(0)),
                   pl.BlockSpec((None, GLA_HEADS, LANES, LANES), lambda b, n: (b * nc + n, 0, 0, 0))],
        out_shape=[_sds((r, hv), BF16), _sds((bsz * nc, GLA_HEADS, LANES, LANES), F32)],
        scratch_shapes=[pltpu.VMEM((GLA_HEADS, LANES, LANES), F32)],
        compiler_params=_params(("parallel", "arbitrary")),
    )(u, u, u, u, gup, gb, gn)


def _outproj_fwd(s0, co, go, w_out, g1, b1, tm):
    r, d = s0.shape
    dc = co.shape[1]

    def body(s0_ref, co_ref, go_ref, w_ref, g_ref, b_ref, p1_ref, s1_ref, s1b_ref):
        mix = _mm(co_ref[...], w_ref[0:dc, :]) + _mm(go_ref[...], w_ref[dc:2 * dc, :])
        p1 = ALPHA * s0_ref[...] + mix
        p1_ref[...] = p1
        xhat, _ = _ln(p1)
        s1 = xhat * g_ref[...] + b_ref[...]
        s1_ref[...] = s1
        s1b_ref[...] = s1.astype(BF16)

    row = lambda w: pl.BlockSpec((tm, w), lambda i: (i, 0))
    vec = pl.BlockSpec((1, d), lambda i: (0, 0))
    return pl.pallas_call(
        body, name="outproj_fwd", grid=(r // tm,),
        in_specs=[row(d), row(dc), row(dc), pl.BlockSpec((2 * dc, d), lambda i: (0, 0)), vec, vec],
        out_specs=[row(d), row(d), row(d)],
        out_shape=[_sds((r, d), F32), _sds((r, d), F32), _sds((r, d), BF16)],
        compiler_params=_params(("parallel",)),
    )(s0, co, go, w_out, g1, b1)


def _mlp_fwd(s1, s1b, w1g, w2, g2, b2, tgt, tp, tm, ns):
    r, d = s1.shape
    nh, _, th = w1g.shape
    nj = nh // ns

    def body(s1_ref, sb_ref, w1_ref, w2_ref, g_ref, b_ref, t_ref, hm_ref, dp2_ref, dpb_ref, loss_ref, dg_ref, db_ref, acc_ref):
        i = pl.program_id(0)
        j = pl.program_id(1)

        @pl.when(jnp.logical_and(i == 0, j == 0))
        def _():
            loss_ref[...] = jnp.zeros_like(loss_ref)
            dg_ref[...] = jnp.zeros_like(dg_ref)
            db_ref[...] = jnp.zeros_like(db_ref)

        @pl.when(j == 0)
        def _():
            acc_ref[...] = jnp.zeros_like(acc_ref)

        for s in range(ns):
            h = _mm(sb_ref[...], w1_ref[s])
            hm_ref[:, s * th:(s + 1) * th] = h.astype(BF16)
            act = jnp.square(jnp.maximum(h, 0.0))
            acc_ref[...] += _mm(act.astype(BF16), w2_ref[s * th:(s + 1) * th, :])

        @pl.when(j == nj - 1)
        def _():
            p2 = ALPHA * s1_ref[...] + acc_ref[...]
            xhat, rstd = _ln(p2)
            s2 = xhat * g_ref[...] + b_ref[...]
            isx = _row_in_seq(i, tm, tp) >= X_OFF
            err = jnp.where(isx, s2 - t_ref[...], 0.0)
            loss_ref[...] += 0.5 * jnp.sum(jnp.mean(err * err, axis=-1, keepdims=True))
            dy = err * (1.0 / d)
            dg_ref[...] += _rowsum(dy * xhat)
            db_ref[...] += _rowsum(dy)
            dp2 = _ln_bwd(dy * g_ref[...], xhat, rstd)
            dp2_ref[...] = dp2
            dpb_ref[...] = dp2.astype(BF16)

    row = pl.BlockSpec((tm, d), lambda i, j: (i, 0))
    vec = pl.BlockSpec((1, d), lambda i, j: (0, 0))
    return pl.pallas_call(
        body, name="mlp_fwd", grid=(r // tm, nj),
        in_specs=[row, row, pl.BlockSpec((ns, d, th), lambda i, j: (j, 0, 0)), pl.BlockSpec((ns * th, d), lambda i, j: (j, 0)),
                  vec, vec, row],
        out_specs=[pl.BlockSpec((tm, ns * th), lambda i, j: (i, j)), row, row,
                   pl.BlockSpec((8, LANES), lambda i, j: (0, 0)), vec, vec],
        out_shape=[_sds((r, nh * th), BF16), _sds((r, d), F32), _sds((r, d), BF16), _sds((8, LANES), F32),
                   _sds((1, d), F32), _sds((1, d), F32)],
        scratch_shapes=[pltpu.VMEM((tm, d), F32)],
        compiler_params=_params(("arbitrary", "arbitrary")),
    )(s1, s1b, w1g, w2, g2, b2, tgt)


def _mlp_bwd_act(dp2, dpb, hm, w1g, w2, p1, g1, tm, ns):
    r, d = dp2.shape
    nh, _, th = w1g.shape
    nj = nh // ns

    def body(dp2_ref, dpb_ref, hm_ref, w1_ref, w2_ref, p1_ref, g_ref, dh_ref, dp1_ref, dg_ref, db_ref, acc_ref):
        i = pl.program_id(0)
        j = pl.program_id(1)

        @pl.when(jnp.logical_and(i == 0, j == 0))
        def _():
            dg_ref[...] = jnp.zeros_like(dg_ref)
            db_ref[...] = jnp.zeros_like(db_ref)

        @pl.when(j == 0)
        def _():
            acc_ref[...] = jnp.zeros_like(acc_ref)

        for s in range(ns):
            cols = slice(s * th, (s + 1) * th)
            dact = _mm_nt(dpb_ref[...], w2_ref[cols, :])
            dh = (dact * (2.0 * jnp.maximum(hm_ref[:, cols].astype(F32), 0.0))).astype(BF16)
            dh_ref[:, cols] = dh
            acc_ref[...] += _mm_nt(dh, w1_ref[s])

        @pl.when(j == nj - 1)
        def _():
            ds1 = ALPHA * dp2_ref[...] + acc_ref[...]
            xhat, rstd = _ln(p1_ref[...])
            dg_ref[...] += _rowsum(ds1 * xhat)
            db_ref[...] += _rowsum(ds1)
            dp1_ref[...] = _ln_bwd(ds1 * g_ref[...], xhat, rstd)

    row = pl.BlockSpec((tm, d), lambda i, j: (i, 0))
    vec = pl.BlockSpec((1, d), lambda i, j: (0, 0))
    blk = pl.BlockSpec((tm, ns * th), lambda i, j: (i, j))
    return pl.pallas_call(
        body, name="mlp_bwd_act", grid=(r // tm, nj),
        in_specs=[row, row, blk, pl.BlockSpec((ns, d, th), lambda i, j: (j, 0, 0)),
                  pl.BlockSpec((ns * th, d), lambda i, j: (j, 0)), row, vec],
        out_specs=[blk, row, vec, vec],
        out_shape=[_sds((r, nh * th), BF16), _sds((r, d), F32), _sds((1, d), F32), _sds((1, d), F32)],
        scratch_shapes=[pltpu.VMEM((tm, d), F32)],
        compiler_params=_params(("arbitrary", "arbitrary")),
    )(dp2, dpb, hm, w1g, w2, p1, g1)


def _mlp_bwd_w(s1b, hm, dh, dpb, nh, tm, ns):
    r, d = s1b.shape
    th = hm.shape[1] // nh

    def body(s1_ref, hm_ref, dh_ref, dp2_ref, dw1_ref, dw2_ref, a1_ref, a2_ref):
        i = pl.program_id(1)

        @pl.when(i == 0)
        def _():
            a1_ref[...] = jnp.zeros_like(a1_ref)
            a2_ref[...] = jnp.zeros_like(a2_ref)

        for s in range(ns):
            cols = slice(s * th, (s + 1) * th)
            act = jnp.square(jnp.maximum(hm_ref[:, cols].astype(F32), 0.0)).astype(BF16)
            a1_ref[s] += _mm_tn(s1_ref[...], dh_ref[:, cols])
            a2_ref[s] += _mm_tn(act, dp2_ref[...])

        @pl.when(i == pl.num_programs(1) - 1)
        def _():
            dw1_ref[...] = a1_ref[...].astype(BF16)
            dw2_ref[...] = a2_ref[...].astype(BF16)

    row = pl.BlockSpec((tm, d), lambda j, i: (i, 0))
    blk = pl.BlockSpec((tm, ns * th), lambda j, i: (i, j))
    return pl.pallas_call(
        body, name="mlp_bwd_w", grid=(nh // ns, r // tm),
        in_specs=[row, blk, blk, row],
        out_specs=[pl.BlockSpec((ns, d, th), lambda j, i: (j, 0, 0)), pl.BlockSpec((ns, th, d), lambda j, i: (j, 0, 0))],
        out_shape=[_sds((nh, d, th), BF16), _sds((nh, th, d), BF16)],
        scratch_shapes=[pltpu.VMEM((ns, d, th), F32), pltpu.VMEM((ns, th, d), F32)],
        compiler_params=_params(("parallel", "arbitrary")),
    )(s1b, hm, dh, dpb)


def _outproj_bwd(dp1, co, go, w_out, dep, tm):
    r, d = dp1.shape
    dc = co.shape[1]

    def body(dp_ref, co_ref, go_ref, w_ref, dep_ref, dmi_ref, dw_ref, acc_ref):
        i = pl.program_id(0)

        @pl.when(i == 0)
        def _():
            acc_ref[...] = jnp.zeros_like(acc_ref)

        dpb = dp_ref[...].astype(BF16)
        dmi_ref[...] = _mm_nt(dpb, w_ref[...])
        acc_ref[0:dc, :] += _mm_tn(co_ref[...], dpb)
        acc_ref[dc:2 * dc, :] += _mm_tn(go_ref[...], dpb)

        @pl.when(i == pl.num_programs(0) - 1)
        def _():
            dw_ref[...] = acc_ref[...].astype(BF16)

    row = lambda w: pl.BlockSpec((tm, w), lambda i: (i, 0))
    full = pl.BlockSpec((2 * dc, d), lambda i: (0, 0))
    return pl.pallas_call(
        body, name="outproj_bwd", grid=(r // tm,),
        in_specs=[row(d), row(dc), row(dc), full, pl.BlockSpec(memory_space=pl.ANY)],
        out_specs=[row(2 * dc), full],
        out_shape=[_sds((r, 2 * dc), F32), _sds((2 * dc, d), BF16)],
        scratch_shapes=[pltpu.VMEM((2 * dc, d), F32)],
        compiler_params=_params(("arbitrary",)),
    )(dp1, co, go, w_out, dep)


def _gla_bwd(u, dmi, sta, gup, gb, gn, dep, bsz, nc):
    r = u.shape[0]
    hv = GLA_HEADS * GLA_DV
    hk = GLA_HEADS * GLA_DK

    def body(qk_ref, v_ref, r_ref, gd_ref, dgo_ref, sta_ref, gup_ref, gb_ref, gn_ref, dep_ref,
             dqk_ref, dv_ref, dr_ref, dgd_ref, dgn_ref, dgb_ref, dgup_ref, dst_ref):
        bi = pl.program_id(0)
        t = pl.program_id(1)
        n = nc - 1 - t

        @pl.when(jnp.logical_and(bi == 0, t == 0))
        def _():
            dgn_ref[...] = jnp.zeros_like(dgn_ref)
            dgb_ref[...] = jnp.zeros_like(dgb_ref)
            dgup_ref[...] = jnp.zeros_like(dgup_ref)

        @pl.when(t == 0)
        def _():
            dst_ref[...] = jnp.zeros_like(dst_ref)

        gd = gd_ref[...]
        p = _gla_prep(qk_ref[...], gd, gup_ref[...], gb_ref[...], n)
        tril = p["ri"] >= p["ci"]
        dgn = jnp.zeros((1, GLA_DV), F32)
        dqe_t, dke_t, dkd_t, dgam_t = [], [], [], []
        for hp in range(GLA_HEADS // 2):
            ls = slice(hp * LANES, (hp + 1) * LANES)
            dqe = jnp.zeros((CHUNK, LANES), F32)
            dke = jnp.zeros((CHUNK, LANES), F32)
            dkd = jnp.zeros((CHUNK, LANES), F32)
            dgam = jnp.zeros((1, LANES), F32)
            for h2 in range(2):
                h = 2 * hp + h2
                vs = slice(h * GLA_DV, (h + 1) * GLA_DV)
                m = _head_mask(h2)
                qeh = jnp.where(m, p["qe"][:, ls], 0.0).astype(BF16)
                kdh = jnp.where(m, p["kd"][:, ls], 0.0).astype(BF16)
                keh = p["ke"][:, ls].astype(BF16)
                vh = v_ref[:, vs].astype(BF16)
                st = sta_ref[h]
                stb = st.astype(BF16)
                a = jnp.where(tril, _mm_nt(qeh, keh), 0.0)
                ab = a.astype(BF16)
                o = _mm(ab, vh) + _mm_nt(qeh, stb)
                rr = r_ref[:, vs]
                sr = _sigmoid(rr)
                rs = lax.rsqrt(jnp.mean(o * o, axis=-1, keepdims=True) + LN_EPS)
                y = o * rs
                dgo = dgo_ref[:, vs]
                don = dgo * (rr * sr)
                dr_ref[:, vs] = dgo * (y * gn_ref[...]) * (sr * (1.0 + rr * (1.0 - sr)))
                dgn = dgn + _rowsum(don * y)
                dxn = don * gn_ref[...]
                do = rs * (dxn - y * jnp.mean(dxn * y, axis=-1, keepdims=True))
                dob = do.astype(BF16)
                dst = dst_ref[h]
                dstb = dst.astype(BF16)
                da = jnp.where(tril, _mm_nt(dob, vh), 0.0).astype(BF16)
                dv_ref[:, vs] = _mm_tn(ab, dob) + _mm_nt(kdh, dstb)
                dqe = dqe + jnp.where(m, _mm(da, keh) + _mm(dob, stb), 0.0)
                dke = dke + _mm_tn(da, qeh)
                dkd = dkd + jnp.where(m, _mm(vh, dstb), 0.0)
                dgam = dgam + _rowsum(dst * st)
                dst_ref[h] = dst * p["gam"][:, ls] + _mm_tn(dob, qeh)
            dqe_t.append(dqe)
            dke_t.append(dke)
            dkd_t.append(dkd)
            dgam_t.append(dgam)
        dqe = jnp.concatenate(dqe_t, axis=1)
        dke = jnp.concatenate(dke_t, axis=1)
        dkd = jnp.concatenate(dkd_t, axis=1)
        dgam = jnp.concatenate(dgam_t, axis=1)
        dqk_ref[:, :hk] = dqe * p["eb"] * QK_SCALE
        dqk_ref[:, hk:] = dke * p["enb"] + dkd * p["ebl"]
        dkdkd = dkd * p["kd"]
        db = dqe * p["qe"] - dke * p["ke"] - dkdkd
        dbl = _rowsum(dkdkd) + dgam * p["gam"]
        upper = (p["ri"] <= p["ci"]).astype(BF16)
        dlg = _tri_mm(upper, db) + dbl
        dz = jnp.where(p["real"], dlg * (1.0 / GLA_TAU) * _sigmoid(-p["z"]), 0.0)
        dzb = dz.astype(BF16)
        dgb_ref[...] += _rowsum(dz)
        dgup_ref[...] += _mm_tn(gd.astype(BF16), dzb)
        dgd_ref[...] = _mm_nt(dzb, gup_ref[...])
        dgn_ref[...] += dgn

    rowblk = lambda col: (lambda b, t: (b * nc + nc - 1 - t, col))
    const = lambda b, t: (0, 0)
    return pl.pallas_call(
        body, name="gla_bwd", grid=(bsz, nc),
        in_specs=[pl.BlockSpec((CHUNK, 2 * hk), rowblk(2)), pl.BlockSpec((CHUNK, hv), rowblk(3)),
                  pl.BlockSpec((CHUNK, hv), rowblk(4)), pl.BlockSpec((CHUNK, LANES), rowblk(20)),
                  pl.BlockSpec((CHUNK, hv), rowblk(1)),
                  pl.BlockSpec((None, GLA_HEADS, LANES, LANES), lambda b, t: (b * nc + nc - 1 - t, 0, 0, 0)),
                  pl.BlockSpec((LANES, 256), const), pl.BlockSpec((1, 256), const), pl.BlockSpec((1, GLA_DV), const),
                  pl.BlockSpec(memory_space=pl.ANY)],
        out_specs=[pl.BlockSpec((CHUNK, 2 * hk), rowblk(0)), pl.BlockSpec((CHUNK, hv), rowblk(0)),
                   pl.BlockSpec((CHUNK, hv), rowblk(0)), pl.BlockSpec((CHUNK, LANES), rowblk(0)),
                   pl.BlockSpec((1, GLA_DV), const), pl.BlockSpec((1, 256), const), pl.BlockSpec((LANES, 256), const)],
        out_shape=[_sds((r, 2 * hk), F32), _sds((r, hv), F32), _sds((r, hv), F32), _sds((r, LANES), F32),
                   _sds((1, GLA_DV), F32), _sds((1, 256), F32), _sds((LANES, 256), F32)],
        scratch_shapes=[pltpu.VMEM((GLA_HEADS, LANES, LANES), F32)],
        compiler_params=_params(("arbitrary", "arbitrary")),
    )(u, u, u, u, dmi, sta, gup, gb, gn, dep)


def _conv_bwd(u, c, dmi, w32, cg, cbe, tp, tc, dc):
    r = u.shape[0]
    hb = tc // CONV_HALO
    nhalo = r // CONV_HALO

    def dconv(cv, dco, cg_ref, cbe_ref):
        xhat, rstd = _ln(cv)
        cn = xhat * cg_ref[...] + cbe_ref[...]
        sg = _sigmoid(cn)
        dcn = dco * (sg * (1.0 + cn * (1.0 - sg)))
        return _ln_bwd(dcn * cg_ref[...], xhat, rstd), dcn, xhat

    def body(a_ref, g_ref, ah_ref, gh_ref, c_ref, dco_ref, ch_ref, dcoh_ref, w_ref, cg_ref, cbe_ref,
             du_ref, dw_ref, dcb_ref, dcg_ref, dcbe_ref, hs_ref, dcs_ref):
        t = pl.program_id(0)

        @pl.when(t == 0)
        def _():
            dw_ref[...] = jnp.zeros_like(dw_ref)
            dcb_ref[...] = jnp.zeros_like(dcb_ref)
            dcg_ref[...] = jnp.zeros_like(dcg_ref)
            dcbe_ref[...] = jnp.zeros_like(dcbe_ref)

        first = lax.rem(t * tc, tp) == 0
        last = lax.rem((t + 1) * tc, tp) == 0
        hh = ah_ref[...] * _sigmoid(gh_ref[...])
        hs_ref[0:CONV_HALO, :] = jnp.where(first, 0.0, hh)
        hs_ref[CONV_HALO:CONV_HALO + tc, :] = a_ref[...] * _sigmoid(g_ref[...])
        dch, _, _ = dconv(ch_ref[...], dcoh_ref[...], cg_ref, cbe_ref)
        dcs_ref[tc:tc + CONV_HALO, :] = jnp.where(last, 0.0, dch)

        def sub1(k, carry):
            r0 = pl.multiple_of(k * CONV_SUB, CONV_SUB)
            dcv, dcn, xhat = dconv(c_ref[pl.ds(r0, CONV_SUB), :], dco_ref[pl.ds(r0, CONV_SUB), :], cg_ref, cbe_ref)
            dcs_ref[pl.ds(r0, CONV_SUB), :] = dcv
            dcb_ref[...] += _rowsum(dcv)
            dcg_ref[...] += _rowsum(dcn * xhat)
            dcbe_ref[...] += _rowsum(dcn)
            return carry

        lax.fori_loop(0, tc // CONV_SUB, sub1, 0)

        def sub2(k, carry):
            r0 = pl.multiple_of(k * CONV_SUB, CONV_SUB)
            dwin = dcs_ref[pl.ds(r0, CONV_WIN), :]
            dh = _conv_taps(dwin, lambda o: w_ref[CONV_WIDTH - 1 - o:CONV_WIDTH - o, :], 0)
            av = a_ref[pl.ds(r0, CONV_SUB), :]
            sg = _sigmoid(g_ref[pl.ds(r0, CONV_SUB), :])
            du_ref[pl.ds(r0, CONV_SUB), 0:dc] = dh * sg
            du_ref[pl.ds(r0, CONV_SUB), dc:2 * dc] = dh * av * sg * (1.0 - sg)
            hwin = hs_ref[pl.ds(r0, CONV_WIN), :]
            dcv = dwin[0:CONV_SUB, :]
            for rho in range(8):
                offs = [o for o in range(2, 2 + CONV_WIDTH) if o % 8 == rho]
                rolled = hwin if rho == 0 else pltpu.roll(hwin, CONV_WIN - rho, 0)
                for o in offs:
                    m8 = o - rho
                    dw_ref[o - 2:o - 1, :] += _rowsum(dcv * rolled[m8:m8 + CONV_SUB, :])
            return carry

        lax.fori_loop(0, tc // CONV_SUB, sub2, 0)

    vec = pl.BlockSpec((1, dc), lambda t: (0, 0))
    prev = lambda col: (lambda t: (jnp.maximum(t * hb - 1, 0), col))
    nxt = lambda col: (lambda t: (jnp.minimum((t + 1) * hb, nhalo - 1), col))
    return pl.pallas_call(
        body, name="conv_bwd", grid=(r // tc,),
        in_specs=[pl.BlockSpec((tc, dc), lambda t: (t, 0)), pl.BlockSpec((tc, dc), lambda t: (t, 1)),
                  pl.BlockSpec((CONV_HALO, dc), prev(0)), pl.BlockSpec((CONV_HALO, dc), prev(1)),
                  pl.BlockSpec((tc, dc), lambda t: (t, 0)), pl.BlockSpec((tc, dc), lambda t: (t, 0)),
                  pl.BlockSpec((CONV_HALO, dc), nxt(0)), pl.BlockSpec((CONV_HALO, dc), nxt(0)),
                  pl.BlockSpec((32, dc), lambda t: (0, 0)), vec, vec],
        out_specs=[pl.BlockSpec((tc, 2 * dc), lambda t: (t, 0)), pl.BlockSpec((32, dc), lambda t: (0, 0)), vec, vec, vec],
        out_shape=[_sds((r, 2 * dc), F32), _sds((32, dc), F32), _sds((1, dc), F32), _sds((1, dc), F32), _sds((1, dc), F32)],
        scratch_shapes=[pltpu.VMEM((CONV_HALO + tc, dc), F32), pltpu.VMEM((tc + CONV_HALO, dc), F32)],
        compiler_params=_params(("arbitrary",)),
    )(u, u, u, u, c, dmi, c, dmi, w32, cg, cbe)


def _inproj_bwd(dp1, dus, xcat, g_in, w_in, dep, tp, tm):
    r, d = dp1.shape
    widths = [x.shape[1] for x in dus]
    offs = [sum(widths[:k]) for k in range(len(widths))]
    n = w_in.shape[1]
    nd = len(dus)
    tps = tp // tm

    def body(*refs):
        dp_ref = refs[0]
        du_refs = refs[1:1 + nd]
        x_ref, g_ref, w_ref, _, dx_ref, dmeta_ref, dg_ref, db_ref = refs[1 + nd:]
        i = pl.program_id(0)

        @pl.when(i == 0)
        def _():
            dmeta_ref[...] = jnp.zeros_like(dmeta_ref)
            dg_ref[...] = jnp.zeros_like(dg_ref)
            db_ref[...] = jnp.zeros_like(db_ref)

        ds0 = ALPHA * dp_ref[...]
        for k in range(nd):
            ds0 = ds0 + _mm_nt(du_refs[k][...].astype(BF16), w_ref[:, offs[k]:offs[k] + widths[k]])
        real = _row_in_seq(i, tm, tp) >= PAD_FRONT
        ds0 = jnp.where(real, ds0, 0.0)
        xhat, rstd = _ln(x_ref[...])
        dg_ref[...] += _rowsum(ds0 * xhat)
        db_ref[...] += _rowsum(ds0)
        dx = jnp.where(real, _ln_bwd(ds0 * g_ref[...], xhat, rstd), 0.0)
        dx_ref[...] = dx

        @pl.when(lax.rem(i, tps) == 0)
        def _():
            dmeta_ref[...] += dx[PAD_FRONT:X_OFF, :]

    row = lambda w: pl.BlockSpec((tm, w), lambda i: (i, 0))
    vec = pl.BlockSpec((1, d), lambda i: (0, 0))
    return pl.pallas_call(
        body, name="inproj_bwd", grid=(r // tm,),
        in_specs=[row(d)] + [row(w) for w in widths] + [row(d), vec, pl.BlockSpec((d, n), lambda i: (0, 0)),
                                                        pl.BlockSpec(memory_space=pl.ANY)],
        out_specs=[row(d), pl.BlockSpec((N_META, d), lambda i: (0, 0)), vec, vec],
        out_shape=[_sds((r, d), F32), _sds((N_META, d), F32), _sds((1, d), F32), _sds((1, d), F32)],
        compiler_params=_params(("arbitrary",)),
    )(dp1, *dus, xcat, g_in, w_in, dep)


def _inproj_bwd_w(s0, dus, tm):
    r, d = s0.shape
    widths = [x.shape[1] for x in dus]
    nd = len(dus)

    def body(*refs):
        s_ref = refs[0]
        du_refs = refs[1:1 + nd]
        dw_refs = refs[1 + nd:]
        i = pl.program_id(0)

        @pl.when(i == 0)
        def _():
            for k in range(nd):
                dw_refs[k][...] = jnp.zeros_like(dw_refs[k])

        sb = s_ref[...].astype(BF16)
        for k in range(nd):
            dw_refs[k][...] += _mm_tn(sb, du_refs[k][...].astype(BF16))

    row = lambda w: pl.BlockSpec((tm, w), lambda i: (i, 0))
    return pl.pallas_call(
        body, name="inproj_bwd_w", grid=(r // tm,),
        in_specs=[row(d)] + [row(w) for w in widths],
        out_specs=[pl.BlockSpec((d, w), lambda i: (0, 0)) for w in widths],
        out_shape=[_sds((d, w), F32) for w in widths],
        compiler_params=_params(("arbitrary",)),
    )(s0, *dus)


def _local_step(x, tgt, meta, ln_in_g, ln_in_b, w_in, conv_w, conv_b, conv_ln_g, conv_ln_b, gate_up, gate_bias,
                gla_norm_g, late_weights, ln1_g, ln1_b, ln2_g, ln2_b, push):
    bsz, seq, d = x.shape
    tp = X_OFF + seq
    assert tp % CHUNK == 0
    nc = tp // CHUNK
    r = bsz * tp
    dc = conv_b.shape[1]
    tm = _pick_tile(tp, (352, 128, 64))
    tc = _pick_tile(tp, (704, 128, 64))

    xcat = jnp.concatenate([jnp.zeros((bsz, PAD_FRONT, d), F32), jnp.broadcast_to(meta[None], (bsz, N_META, d)), x],
                           axis=1).reshape(r, d)
    tgt_p = jnp.pad(tgt, ((0, 0), (X_OFF, 0), (0, 0))).reshape(r, d)
    w32 = jnp.pad(conv_w, ((0, 32 - CONV_WIDTH), (0, 0)))
    gup = jnp.pad(gate_up, ((0, LANES - GLA_RANK), (0, 0))).astype(BF16)

    s0, u = _inproj_fwd(xcat, ln_in_g, ln_in_b, w_in, tp, tm)
    c, co = _conv_fwd(u, w32, conv_b, conv_ln_g, conv_ln_b, tp, tc, dc)
    go, sta = _gla_fwd(u, gup, gate_bias, gla_norm_g, bsz, nc)
    w_out, w1g, w2 = late_weights(go)
    nh = w1g.shape[0]
    tmm = _pick_tile(tp, (704, 128, 64))
    ns = 2
    p1, s1, s1b = _outproj_fwd(s0, co, go, w_out, ln1_g, ln1_b, tm)
    hm, dp2, dpb, loss, dg2, db2 = _mlp_fwd(s1, s1b, w1g, w2, ln2_g, ln2_b, tgt_p, tp, tmm, ns)

    dh, dp1, dg1, db1 = _mlp_bwd_act(dp2, dpb, hm, w1g, w2, p1, ln1_g, tmm, ns)
    dw1, dw2 = _mlp_bwd_w(s1b, hm, dh, dpb, nh, tmm, ns)
    tok = push("ff", (dw1, dw2))
    dmi, dwo = _outproj_bwd(dp1, co, go, w_out, tok, tm)
    tok = push("out", (dwo,))
    dqk, dv, dr, dgd, dgn, dgb, dgup = _gla_bwd(u, dmi, sta, gup, gate_bias, gla_norm_g, tok, bsz, nc)
    dcv, dcw, dcb, dcg, dcbe = _conv_bwd(u, c, dmi, w32, conv_ln_g, conv_ln_b, tp, tc, dc)
    dus = [dcv, dqk, dv, dr, dgd]
    dwi = _inproj_bwd_w(s0, dus, tm)
    tok = push("in", (jnp.concatenate(dwi, axis=1),))
    dxcat, dmeta, dgi, dbi = _inproj_bwd(dp1, dus, xcat, ln_in_g, w_in, tok, tp, tm)

    grad_x = dxcat.reshape(bsz, tp, d)[:, X_OFF:, :]
    return dict(loss=loss[0, 0], grad_x=grad_x, meta_tokens=dmeta, ln_in_g=dgi, ln_in_b=dbi,
                conv_w=dcw[:CONV_WIDTH], conv_b=dcb, conv_ln_g=dcg, conv_ln_b=dcbe,
                gate_up=dgup[:GLA_RANK], gate_bias=dgb, gla_norm_g=dgn, ln1_g=dg1, ln1_b=db1, ln2_g=dg2, ln2_b=db2)


def _exchange(arrays, scatter, name):
    na = len(arrays)
    npeer = N_DEV - 1

    def body(*refs):
        srcs = refs[:na]
        outs = refs[na:2 * na]
        send_sems, recv_sems, local_sems = refs[2 * na:]
        xi, yi, ci = (lax.axis_index(a) for a in MESH_AXES)
        me = 4 * xi + 2 * yi + ci
        copies = []
        for a in range(na):
            own = srcs[a].at[me] if scatter[a] else srcs[a]
            cp = pltpu.make_async_copy(own, outs[a].at[me], local_sems.at[a])
            cp.start()
            copies.append(cp)
        remote = []
        for k in range(1, N_DEV):
            px, py, pc = xi ^ (k >> 2), yi ^ ((k >> 1) & 1), ci ^ (k & 1)
            peer = 4 * px + 2 * py + pc
            for a in range(na):
                src = srcs[a].at[peer] if scatter[a] else srcs[a]
                cp = pltpu.make_async_remote_copy(
                    src_ref=src, dst_ref=outs[a].at[me],
                    send_sem=send_sems.at[a * npeer + k - 1], recv_sem=recv_sems.at[a * npeer + k - 1],
                    device_id=(px, py, pc), device_id_type=pl.DeviceIdType.MESH)
                cp.start()
                remote.append(cp)
        for cp in remote:
            cp.wait()
        for cp in copies:
            cp.wait()

    out_shape = [_sds(a.shape if scatter[i] else (N_DEV,) + a.shape, a.dtype) for i, a in enumerate(arrays)]
    anyspec = pl.BlockSpec(memory_space=pl.ANY)
    return pl.pallas_call(
        body, name=name,
        in_specs=[anyspec] * na, out_specs=[anyspec] * na, out_shape=out_shape,
        scratch_shapes=[pltpu.SemaphoreType.DMA((na * npeer,)), pltpu.SemaphoreType.DMA((na * npeer,)),
                        pltpu.SemaphoreType.DMA((na,))],
    )(*arrays)


def _peers(xi, yi, ci):
    for k in range(1, N_DEV):
        px, py, pc = xi ^ (k >> 2), yi ^ ((k >> 1) & 1), ci ^ (k & 1)
        yield (px, py, pc), 4 * px + 2 * py + pc


def _sc_exchange(arrays, scatter, name, collective_id, after=None):
    na = len(arrays)
    npeer = N_DEV - 1
    ndep = 0 if after is None else 1

    def body(*refs):
        srcs = refs[:na]
        outs = refs[na + ndep:2 * na + ndep]
        send_sems, recv_sems, own_sems = refs[2 * na + ndep:]
        xi, yi, ci = (lax.axis_index(a) for a in MESH_AXES)
        me = 4 * xi + 2 * yi + ci
        barrier = pltpu.get_barrier_semaphore()
        for pos, _ in _peers(xi, yi, ci):
            pl.semaphore_signal(barrier, inc=1, device_id=pos, device_id_type=pl.DeviceIdType.MESH)
        pl.semaphore_wait(barrier, npeer)
        own = [pltpu.make_async_copy(srcs[a].at[me] if scatter[a] else srcs[a], outs[a].at[me], own_sems.at[a])
               for a in range(na)]
        for cp in own:
            cp.start()
        remote = []
        for a in range(na):
            for k, (pos, peer) in enumerate(_peers(xi, yi, ci)):
                cp = pltpu.make_async_remote_copy(
                    src_ref=srcs[a].at[peer] if scatter[a] else srcs[a], dst_ref=outs[a].at[me],
                    send_sem=send_sems.at[a * npeer + k], recv_sem=recv_sems.at[a * npeer + k],
                    device_id=pos, device_id_type=pl.DeviceIdType.MESH)
                cp.start()
                remote.append(cp)
        for cp in own:
            cp.wait()
        for cp in remote:
            cp.wait()

    out_type = [_sds(a.shape if scatter[i] else (N_DEV,) + a.shape, a.dtype) for i, a in enumerate(arrays)]
    sent = sum(a.size * a.dtype.itemsize // (N_DEV if scatter[i] else 1) for i, a in enumerate(arrays))
    return pl.kernel(
        body, out_type=out_type, mesh=plsc.ScalarSubcoreMesh(axis_name="seq", num_cores=1), name=name,
        scratch_types=[pltpu.SemaphoreType.DMA((na * npeer,)), pltpu.SemaphoreType.DMA((na * npeer,)),
                       pltpu.SemaphoreType.DMA((na,))],
        compiler_params=pltpu.CompilerParams(collective_id=collective_id),
        cost_estimate=pl.CostEstimate(flops=0, transcendentals=0, bytes_accessed=2 * N_DEV * sent,
                                      remote_bytes_transferred=npeer * sent),
    )(*arrays, *([] if after is None else [after]))


def _push_start(arrays, scatter, name, dep=None):
    na = len(arrays)
    shapes = [a.shape if scatter[i] else (N_DEV,) + a.shape for i, a in enumerate(arrays)]
    hbm = pl.BlockSpec(memory_space=pltpu.HBM)
    sem = pl.BlockSpec(memory_space=pltpu.SEMAPHORE)
    ndep = 0 if dep is None else 1

    def body(*refs):
        srcs = refs[:na]
        lands = refs[na:2 * na]
        send_sems, recv_sems = refs[2 * na + ndep:2 * na + ndep + 2]
        own_sems = refs[4 * na + ndep + 2]
        xi, yi, ci = (lax.axis_index(a) for a in MESH_AXES)
        me = 4 * xi + 2 * yi + ci
        own = [pltpu.make_async_copy(srcs[a].at[me] if scatter[a] else srcs[a], lands[a].at[me], own_sems.at[a])
               for a in range(na)]
        for cp in own:
            cp.start()
        for cp in own:
            cp.wait()
        for a in range(na):
            for pos, peer in _peers(xi, yi, ci):
                pltpu.make_async_remote_copy(
                    src_ref=srcs[a].at[peer] if scatter[a] else srcs[a], dst_ref=lands[a].at[me],
                    send_sem=send_sems.at[a], recv_sem=recv_sems.at[a],
                    device_id=pos, device_id_type=pl.DeviceIdType.MESH).start()

    ins = [pltpu.with_memory_space_constraint(a, pltpu.HBM) for a in arrays]
    ins += [pltpu.with_memory_space_constraint(lax.empty(s, a.dtype), pltpu.HBM) for s, a in zip(shapes, arrays)]
    res = pl.pallas_call(
        body, name=name,
        in_specs=[hbm] * (2 * na) + [pl.BlockSpec(memory_space=pl.ANY)] * ndep,
        out_specs=[sem, sem] + [hbm] * (2 * na),
        out_shape=[pltpu.SemaphoreType.DMA((na,)), pltpu.SemaphoreType.DMA((na,))]
                  + [pltpu.HBM(a.shape, a.dtype) for a in arrays] + [pltpu.HBM(s, a.dtype) for s, a in zip(shapes, arrays)],
        input_output_aliases={i: 2 + i for i in range(2 * na)},
        scratch_shapes=[pltpu.SemaphoreType.DMA((na,))],
        compiler_params=pltpu.CompilerParams(has_side_effects=pltpu.SideEffectType.DATAFLOW_SIDE_EFFECTING),
    )(*ins, *([] if dep is None else [dep]))
    return (res[0], res[1], list(res[2:2 + na]), list(res[2 + na:2 + 2 * na])), res[2]


def _push_wait(handle, after, name):
    send_sems, recv_sems, srcs, lands = handle
    na = len(srcs)
    hbm = pl.BlockSpec(memory_space=pltpu.HBM)
    sem = pl.BlockSpec(memory_space=pltpu.SEMAPHORE)

    def body(*refs):
        land_refs = refs[na:2 * na]
        send_ref, recv_ref = refs[2 * na:2 * na + 2]
        me = tuple(lax.axis_index(a) for a in MESH_AXES)
        for a in range(na):
            seven = land_refs[a].at[pl.ds(0, N_DEV - 1)]
            cp = pltpu.make_async_remote_copy(src_ref=seven, dst_ref=seven, send_sem=send_ref.at[a], recv_sem=recv_ref.at[a],
                                              device_id=me, device_id_type=pl.DeviceIdType.MESH)
            cp.wait_send()
            cp.wait_recv()

    res = pl.pallas_call(
        body, name=name,
        in_specs=[hbm] * (2 * na) + [sem, sem, pl.BlockSpec(memory_space=pl.ANY)],
        out_specs=[hbm] * (2 * na),
        out_shape=[pltpu.HBM(a.shape, a.dtype) for a in srcs] + [pltpu.HBM(a.shape, a.dtype) for a in lands],
        input_output_aliases={i: i for i in range(2 * na)},
        compiler_params=pltpu.CompilerParams(has_side_effects=pltpu.SideEffectType.DATAFLOW_SIDE_EFFECTING),
    )(*srcs, *lands, send_sems, recv_sems, after)
    return list(res[na:])


def _adamw(w, g, m, v):
    m = ADAM_B1 * m + (1.0 - ADAM_B1) * g
    v = ADAM_B2 * v + (1.0 - ADAM_B2) * jnp.square(g)
    m_hat = m / (1.0 - ADAM_B1 ** ADAM_STEP)
    v_hat = v / (1.0 - ADAM_B2 ** ADAM_STEP)
    delta = -ADAM_LR * (m_hat / (jnp.sqrt(v_hat) + ADAM_EPS) + ADAM_WD * w)
    return delta, m, v


def _sum_devices(ref):
    g = ref[0].astype(F32)
    for k in range(1, N_DEV):
        g = g + ref[k].astype(F32)
    return g


def _update_big(parts, w, m, v, name):
    rows, cols = w.shape
    tr = _pick_tile(rows, (128, 64, 16))

    def body(p_ref, w_ref, m_ref, v_ref, g_ref, d_ref, nm_ref, nv_ref):
        g = _sum_devices(p_ref)
        g_ref[...] = g
        d_ref[...], nm_ref[...], nv_ref[...] = _adamw(w_ref[...], g, m_ref[...], v_ref[...])

    blk = pl.BlockSpec((tr, cols), lambda i: (i, 0))
    return pl.pallas_call(
        body, name=name, grid=(rows // tr,),
        in_specs=[pl.BlockSpec((N_DEV, tr, cols), lambda i: (0, i, 0)), blk, blk, blk],
        out_specs=[blk] * 4, out_shape=[_sds((rows, cols), F32)] * 4,
        compiler_params=_params(("parallel",)),
    )(parts, w, m, v)


_VEC_ORDER = ("ln_in_g", "ln_in_b", "conv_b", "conv_ln_g", "conv_ln_b", "gate_bias", "gla_norm_g",
              "ln1_g", "ln1_b", "ln2_g", "ln2_b")
_SHARDED_SMALL = (("meta_tokens", 0, N_META, LANES), ("conv_w", N_META, CONV_WIDTH, None), ("gate_up", N_META + 32, GLA_RANK, None))


def _update_small(parts_sh, parts_vec, wmv):
    names = [s[0] for s in _SHARDED_SMALL] + list(_VEC_ORDER)
    flat = [a for nme in names for a in wmv[nme]]

    def body(*refs):
        sh_ref, vec_ref = refs[0], refs[1]
        ins = refs[2:2 + len(flat)]
        outs = refs[2 + len(flat):2 + len(flat) + 4 * len(names)]
        gsh_ref, gvec_ref = refs[-2:]
        gsh_ref[...] = _sum_devices(sh_ref)
        gvec_ref[...] = _sum_devices(vec_ref)
        for idx, nme in enumerate(names):
            w_ref, m_ref, v_ref = ins[3 * idx:3 * idx + 3]
            rows, cols = w_ref.shape
            if idx < len(_SHARDED_SMALL):
                r0 = _SHARDED_SMALL[idx][1]
                g = gsh_ref[r0:r0 + rows, 0:cols]
            else:
                j = idx - len(_SHARDED_SMALL)
                g = gvec_ref[j:j + 1, 0:cols]
            o = outs[4 * idx:4 * idx + 4]
            o[0][...] = g
            o[1][...], o[2][...], o[3][...] = _adamw(w_ref[...], g, m_ref[...], v_ref[...])

    out_shape = [_sds(wmv[nme][0].shape, F32) for nme in names for _ in range(4)]
    vmem = pl.BlockSpec(memory_space=pltpu.VMEM)
    res = pl.pallas_call(
        body, name="update_small", out_shape=out_shape,
        in_specs=[vmem] * (2 + len(flat)), out_specs=[vmem] * len(out_shape),
        scratch_shapes=[pltpu.VMEM(parts_sh.shape[1:], F32), pltpu.VMEM(parts_vec.shape[1:], F32)],
    )(parts_sh, parts_vec, *flat)
    return {nme: res[4 * i:4 * i + 4] for i, nme in enumerate(names)}


_WEIGHTS = ("meta_tokens", "ln_in_g", "ln_in_b", "w_in", "conv_w", "conv_b", "conv_ln_g", "conv_ln_b", "gate_up",
            "gate_bias", "gla_norm_g", "w_out", "ln1_g", "ln1_b", "w_ff1", "w_ff2", "ln2_g", "ln2_b")


def kernel(x, meta_tokens, ln_in_g, ln_in_b, w_in, conv_w, conv_b, conv_ln_g, conv_ln_b, gate_up, gate_bias, gla_norm_g, w_out, ln1_g, ln1_b, w_ff1, w_ff2, ln2_g, ln2_b, loss_target, m_meta_tokens, m_ln_in_g, m_ln_in_b, m_w_in, m_conv_w, m_conv_b, m_conv_ln_g, m_conv_ln_b, m_gate_up, m_gate_bias, m_gla_norm_g, m_w_out, m_ln1_g, m_ln1_b, m_w_ff1, m_w_ff2, m_ln2_g, m_ln2_b, v_meta_tokens, v_ln_in_g, v_ln_in_b, v_w_in, v_conv_w, v_conv_b, v_conv_ln_g, v_conv_ln_b, v_gate_up, v_gate_bias, v_gla_norm_g, v_w_out, v_ln1_g, v_ln1_b, v_w_ff1, v_w_ff2, v_ln2_g, v_ln2_b):
    w = dict(meta_tokens=meta_tokens, ln_in_g=ln_in_g, ln_in_b=ln_in_b, w_in=w_in, conv_w=conv_w, conv_b=conv_b,
             conv_ln_g=conv_ln_g, conv_ln_b=conv_ln_b, gate_up=gate_up, gate_bias=gate_bias, gla_norm_g=gla_norm_g,
             w_out=w_out, ln1_g=ln1_g, ln1_b=ln1_b, w_ff1=w_ff1, w_ff2=w_ff2, ln2_g=ln2_g, ln2_b=ln2_b)
    mom = dict(meta_tokens=m_meta_tokens, ln_in_g=m_ln_in_g, ln_in_b=m_ln_in_b, w_in=m_w_in, conv_w=m_conv_w,
               conv_b=m_conv_b, conv_ln_g=m_conv_ln_g, conv_ln_b=m_conv_ln_b, gate_up=m_gate_up, gate_bias=m_gate_bias,
               gla_norm_g=m_gla_norm_g, w_out=m_w_out, ln1_g=m_ln1_g, ln1_b=m_ln1_b, w_ff1=m_w_ff1, w_ff2=m_w_ff2,
               ln2_g=m_ln2_g, ln2_b=m_ln2_b)
    var = dict(meta_tokens=v_meta_tokens, ln_in_g=v_ln_in_g, ln_in_b=v_ln_in_b, w_in=v_w_in, conv_w=v_conv_w,
               conv_b=v_conv_b, conv_ln_g=v_conv_ln_g, conv_ln_b=v_conv_ln_b, gate_up=v_gate_up, gate_bias=v_gate_bias,
               gla_norm_g=v_gla_norm_g, w_out=v_w_out, ln1_g=v_ln1_g, ln1_b=v_ln1_b, w_ff1=v_w_ff1, w_ff2=v_w_ff2,
               ln2_g=v_ln2_g, ln2_b=v_ln2_b)
    shapes = {k: a.shape for k, a in w.items()}

    def two_d(a):
        return a.reshape(1, -1) if a.ndim == 1 else a.reshape(a.shape[-2:])

    w2d = {k: two_d(a) for k, a in w.items()}
    m2d = {k: two_d(a) for k, a in mom.items()}
    v2d = {k: two_d(a) for k, a in var.items()}
    d = x.shape[-1]
    d_in = w2d["w_in"].shape[1] * N_DEV
    d_in_p = -(-d_in // LANES) * LANES

    g_in, g_meta, g_conv, g_gup = _sc_exchange(
        [w2d["w_in"].astype(BF16), w2d["meta_tokens"], w2d["conv_w"], w2d["gate_up"]], [False] * 4, "gather_first", 0)
    g_out, g_ff1, g_ff2 = _sc_exchange(
        [w2d["w_out"].astype(BF16), w2d["w_ff1"].astype(BF16), w2d["w_ff2"].astype(BF16)], [False] * 3, "gather_late", 1,
        after=g_gup)
    w_in_full = jnp.pad(g_in.transpose(1, 0, 2).reshape(d, d_in), ((0, 0), (0, d_in_p - d_in)))
    meta_full = g_meta.transpose(1, 0, 2).reshape(N_META, d)
    conv_w_full = g_conv.transpose(1, 0, 2).reshape(CONV_WIDTH, -1)
    gate_up_full = g_gup.transpose(1, 0, 2).reshape(GLA_RANK, -1)

    def late_weights(after):
        return g_out.reshape(-1, d), g_ff1, g_ff2.reshape(-1, d)

    pushed = {}

    def push(tag, grads):
        if tag == "ff":
            pushed["ff1"], pushed["ff2"] = _sc_exchange(list(grads), [True, True], "scatter_ff", 2)
        elif tag == "out":
            pushed["p_out"] = grads[0].reshape(N_DEV, -1, d)
        else:
            p_in = grads[0][:, :d_in].reshape(d, N_DEV, d_in // N_DEV).transpose(1, 0, 2).astype(BF16)
            pushed["in"], pushed["out"] = _sc_exchange([p_in, pushed["p_out"]], [True, True], "scatter_rest", 3,
                                                       after=pushed["ff1"])
        return grads[0]

    res = _local_step(x, loss_target, meta_full, w2d["ln_in_g"], w2d["ln_in_b"], w_in_full, conv_w_full, w2d["conv_b"],
                      w2d["conv_ln_g"], w2d["conv_ln_b"], gate_up_full, w2d["gate_bias"], w2d["gla_norm_g"], late_weights,
                      w2d["ln1_g"], w2d["ln1_b"], w2d["ln2_g"], w2d["ln2_b"], push)

    dc = res["conv_w"].shape[1]
    hk = res["gate_up"].shape[1]
    sh_meta = res["meta_tokens"].reshape(N_META, N_DEV, LANES).transpose(1, 0, 2)
    sh_conv = jnp.pad(res["conv_w"].reshape(CONV_WIDTH, N_DEV, dc // N_DEV).transpose(1, 0, 2),
                      ((0, 0), (0, 32 - CONV_WIDTH), (0, LANES - dc // N_DEV)))
    sh_gup = jnp.pad(res["gate_up"].reshape(GLA_RANK, N_DEV, hk // N_DEV).transpose(1, 0, 2),
                     ((0, 0), (0, 0), (0, LANES - hk // N_DEV)))
    p_sh = jnp.concatenate([sh_meta, sh_conv, sh_gup], axis=1)
    p_vec = jnp.concatenate([jnp.pad(res[k], ((0, 0), (0, d - res[k].shape[1]))) for k in _VEC_ORDER]
                            + [jnp.zeros((16 - len(_VEC_ORDER), d), F32)], axis=0)

    r_sh, r_vec = _exchange([p_sh, p_vec], [True, False], "scatter_small")
    r_ff1, r_ff2, r_out, r_in = pushed["ff1"], pushed["ff2"], pushed["out"], pushed["in"]

    upd = {}
    upd["w_in"] = _update_big(r_in, w2d["w_in"], m2d["w_in"], v2d["w_in"], "update_w_in")
    upd["w_out"] = _update_big(r_out, w2d["w_out"], m2d["w_out"], v2d["w_out"], "update_w_out")
    upd["w_ff1"] = _update_big(r_ff1, w2d["w_ff1"], m2d["w_ff1"], v2d["w_ff1"], "update_w_ff1")
    upd["w_ff2"] = _update_big(r_ff2, w2d["w_ff2"], m2d["w_ff2"], v2d["w_ff2"], "update_w_ff2")
    small = [s[0] for s in _SHARDED_SMALL] + list(_VEC_ORDER)
    upd.update(_update_small(r_sh, r_vec, {k: (w2d[k], m2d[k], v2d[k]) for k in small}))

    loss = lax.psum(res["loss"], MESH_AXES)
    outs = [loss, res["grad_x"]]
    for j in range(4):
        outs += [upd[k][j].reshape(shapes[k]) for k in _WEIGHTS]
    return tuple(outs)
```

```python
import functools

import jax
import jax.numpy as jnp
from jax import lax
from jax.experimental import pallas as pl
from jax.experimental.pallas import tpu as pltpu
from jax.experimental.pallas import tpu_sc as plsc

F32 = jnp.float32
BF16 = jnp.bfloat16

N_META = 16
CHUNK = 64
PAD_FRONT = (-N_META) % CHUNK
X_OFF = PAD_FRONT + N_META
CONV_WIDTH = 31
CONV_HALO = 32
CONV_SUB = 64
CONV_WIN = CONV_SUB + CONV_HALO
GLA_HEADS = 4
GLA_DK = 64
GLA_DV = 128
GLA_RANK = 16
GLA_TAU = 16.0
QK_SCALE = GLA_DK ** -0.5
LN_EPS = 1e-5
ALPHA = 2.0 ** 0.25
LANES = 128
N_DEV = 8
ADAM_LR = 0.001
ADAM_B1 = 0.9
ADAM_B2 = 0.999
ADAM_EPS = 1e-08
ADAM_WD = 0.01
ADAM_STEP = 10
VMEM_LIMIT = 56 * 1024 * 1024
MESH_AXES = ("x", "y", "c")


def _sds(shape, dtype):
    return jax.ShapeDtypeStruct(shape, dtype)


def _mm(a, b):
    return jnp.dot(a, b, preferred_element_type=F32)


def _mm_nt(a, b):
    return lax.dot_general(a, b, (((1,), (1,)), ((), ())), preferred_element_type=F32)


def _mm_tn(a, b):
    return lax.dot_general(a, b, (((0,), (0,)), ((), ())), preferred_element_type=F32)


def _sigmoid(x):
    return 1.0 / (1.0 + jnp.exp(-x))


def _log_sigmoid(z):
    return jnp.minimum(z, 0.0) - jnp.log(1.0 + jnp.exp(-jnp.abs(z)))


def _ln(x):
    mu = jnp.mean(x, axis=-1, keepdims=True)
    xc = x - mu
    var = jnp.mean(xc * xc, axis=-1, keepdims=True)
    rstd = lax.rsqrt(var + LN_EPS)
    return xc * rstd, rstd


def _ln_bwd(dyg, xhat, rstd):
    m1 = jnp.mean(dyg, axis=-1, keepdims=True)
    m2 = jnp.mean(dyg * xhat, axis=-1, keepdims=True)
    return rstd * (dyg - m1 - xhat * m2)


def _rowsum(x):
    return jnp.sum(x, axis=0, keepdims=True)


def _row_in_seq(i, tm, tp):
    base = lax.rem(i * tm, tp)
    return base + lax.broadcasted_iota(jnp.int32, (tm, 1), 0)


def _split3(x):
    hi = x.astype(BF16)
    r1 = x - hi.astype(F32)
    mid = r1.astype(BF16)
    lo = (r1 - mid.astype(F32)).astype(BF16)
    return hi, mid, lo


def _tri_mm(tri, x):
    hi, mid, lo = _split3(x)
    return _mm(tri, hi) + _mm(tri, mid) + _mm(tri, lo)


def _params(sem):
    return pltpu.CompilerParams(dimension_semantics=sem, vmem_limit_bytes=VMEM_LIMIT)


def _pick_tile(n, prefs):
    for t in prefs:
        if n % t == 0:
            return t
    raise ValueError(f"no tile for {n}")


def _inproj_fwd(xcat, g, b, w_in, tp, tm):
    r, d = xcat.shape
    n = w_in.shape[1]

    def body(x_ref, g_ref, b_ref, w_ref, s0_ref, u_ref):
        i = pl.program_id(0)
        xhat, _ = _ln(x_ref[...])
        real = _row_in_seq(i, tm, tp) >= PAD_FRONT
        s = jnp.where(real, xhat * g_ref[...] + b_ref[...], 0.0)
        s0_ref[...] = s
        u_ref[...] = _mm(s.astype(BF16), w_ref[...])

    return pl.pallas_call(
        body, name="inproj_fwd", grid=(r // tm,),
        in_specs=[pl.BlockSpec((tm, d), lambda i: (i, 0)), pl.BlockSpec((1, d), lambda i: (0, 0)),
                  pl.BlockSpec((1, d), lambda i: (0, 0)), pl.BlockSpec((d, n), lambda i: (0, 0))],
        out_specs=[pl.BlockSpec((tm, d), lambda i: (i, 0)), pl.BlockSpec((tm, n), lambda i: (i, 0))],
        out_shape=[_sds((r, d), F32), _sds((r, n), F32)],
        compiler_params=_params(("parallel",)),
    )(xcat, g, b, w_in)


def _conv_taps(win, coef, lo):
    acc = None
    for rho in range(8):
        offs = [o for o in range(lo, lo + CONV_WIDTH) if o % 8 == rho]
        if not offs:
            continue
        rolled = win if rho == 0 else pltpu.roll(win, CONV_WIN - rho, 0)
        for o in offs:
            m8 = o - rho
            term = rolled[m8:m8 + CONV_SUB, :] * coef(o)
            acc = term if acc is None else acc + term
    return acc


def _conv_fwd(u, w32, cb, cg, cbe, tp, tc, dc):
    r = u.shape[0]
    hb = tc // CONV_HALO

    def body(a_ref, g_ref, ah_ref, gh_ref, w_ref, cb_ref, cg_ref, cbe_ref, c_ref, co_ref, hs_ref):
        t = pl.program_id(0)
        first = lax.rem(t * tc, tp) == 0
        hh = ah_ref[...] * _sigmoid(gh_ref[...])
        hs_ref[0:CONV_HALO, :] = jnp.where(first, 0.0, hh)
        hs_ref[CONV_HALO:CONV_HALO + tc, :] = a_ref[...] * _sigmoid(g_ref[...])

        def sub(k, carry):
            r0 = pl.multiple_of(k * CONV_SUB, CONV_SUB)
            win = hs_ref[pl.ds(r0, CONV_WIN), :]
            c = _conv_taps(win, lambda o: w_ref[o - 2:o - 1, :], 2) + cb_ref[...]
            c_ref[pl.ds(r0, CONV_SUB), :] = c
            xhat, _ = _ln(c)
            cn = xhat * cg_ref[...] + cbe_ref[...]
            co_ref[pl.ds(r0, CONV_SUB), :] = (cn * _sigmoid(cn)).astype(BF16)
            return carry

        lax.fori_loop(0, tc // CONV_SUB, sub, 0)

    vec = pl.BlockSpec((1, dc), lambda t: (0, 0))
    return pl.pallas_call(
        body, name="conv_fwd", grid=(r // tc,),
        in_specs=[pl.BlockSpec((tc, dc), lambda t: (t, 0)), pl.BlockSpec((tc, dc), lambda t: (t, 1)),
                  pl.BlockSpec((CONV_HALO, dc), lambda t: (jnp.maximum(t * hb - 1, 0), 0)),
                  pl.BlockSpec((CONV_HALO, dc), lambda t: (jnp.maximum(t * hb - 1, 0), 1)),
                  pl.BlockSpec((32, dc), lambda t: (0, 0)), vec, vec, vec],
        out_specs=[pl.BlockSpec((tc, dc), lambda t: (t, 0)), pl.BlockSpec((tc, dc), lambda t: (t, 0))],
        out_shape=[_sds((r, dc), F32), _sds((r, dc), BF16)],
        scratch_shapes=[pltpu.VMEM((CONV_HALO + tc, dc), F32)],
        compiler_params=_params(("parallel",)),
    )(u, u, u, u, w32, cb, cg, cbe)


def _gla_prep(qk, gd, gup, gb, n):
    z = _mm(gd.astype(BF16), gup) + gb
    lg = _log_sigmoid(z) * (1.0 / GLA_TAU)
    row = n * CHUNK + lax.broadcasted_iota(jnp.int32, (CHUNK, 1), 0)
    real = row >= PAD_FRONT
    lg = jnp.where(real, lg, 0.0)
    ri = lax.broadcasted_iota(jnp.int32, (CHUNK, CHUNK), 0)
    ci = lax.broadcasted_iota(jnp.int32, (CHUNK, CHUNK), 1)
    low = (ri >= ci).astype(BF16)
    b = _tri_mm(low, lg)
    bl = _rowsum(lg)
    hk = GLA_HEADS * GLA_DK
    q = qk[:, :hk] * QK_SCALE
    k = qk[:, hk:]
    eb = jnp.exp(b)
    enb = jnp.exp(-b)
    ebl = jnp.exp(bl - b)
    gam = jnp.exp(bl)
    return dict(z=z, real=real, ri=ri, ci=ci, eb=eb, enb=enb, ebl=ebl, gam=gam, k=k,
                qe=q * eb, ke=k * enb, kd=k * ebl)


def _head_mask(h2):
    lane = lax.broadcasted_iota(jnp.int32, (1, LANES), 1)
    return (lane < GLA_DK) if h2 == 0 else (lane >= GLA_DK)


def _gla_fwd(u, gup, gb, gn, bsz, nc, kc):
    r = u.shape[0]
    hv = GLA_HEADS * GLA_DV
    ns = nc // kc

    def body(qk_ref, v_ref, r_ref, gd_ref, gup_ref, gb_ref, gn_ref, go_ref, sta_ref, st_ref):
        t = pl.program_id(1)

        @pl.when(t == 0)
        def _():
            st_ref[...] = jnp.zeros_like(st_ref)

        sts = [st_ref[h] for h in range(GLA_HEADS)]
        for j in range(kc):
            rows = slice(j * CHUNK, (j + 1) * CHUNK)
            p = _gla_prep(qk_ref[rows, :], gd_ref[rows, :], gup_ref[...], gb_ref[...], t * kc + j)
            tril = p["ri"] >= p["ci"]
            for h in range(GLA_HEADS):
                hp, h2 = divmod(h, 2)
                ls = slice(hp * LANES, (hp + 1) * LANES)
                vs = slice(h * GLA_DV, (h + 1) * GLA_DV)
                m = _head_mask(h2)
                qeh = jnp.where(m, p["qe"][:, ls], 0.0).astype(BF16)
                kdh = jnp.where(m, p["kd"][:, ls], 0.0).astype(BF16)
                keh = p["ke"][:, ls].astype(BF16)
                vh = v_ref[rows, vs].astype(BF16)
                st = sts[h]
                sta_ref[j, h] = st
                a = jnp.where(tril, _mm_nt(qeh, keh), 0.0)
                o = _mm(a.astype(BF16), vh) + _mm_nt(qeh, st.astype(BF16))
                sts[h] = st * p["gam"][:, ls] + _mm_tn(vh, kdh)
                rs = lax.rsqrt(jnp.mean(o * o, axis=-1, keepdims=True) + LN_EPS)
                rr = r_ref[rows, vs]
                go = o * rs * gn_ref[...] * (rr * _sigmoid(rr))
                go_ref[rows, vs] = go.astype(BF16)
        for h in range(GLA_HEADS):
            st_ref[h] = sts[h]

    rowblk = lambda col: (lambda b, t: (b * ns + t, col))
    const = lambda b, t: (0, 0)
    return pl.pallas_call(
        body, name="gla_fwd", grid=(bsz, ns),
        in_specs=[pl.BlockSpec((kc * CHUNK, 512), rowblk(2)), pl.BlockSpec((kc * CHUNK, hv), rowblk(3)),
                  pl.BlockSpec((kc * CHUNK, hv), rowblk(4)), pl.BlockSpec((kc * CHUNK, LANES), rowblk(20)),
                  pl.BlockSpec((LANES, 256), const), pl.BlockSpec((1, 256), const), pl.BlockSpec((1, GLA_DV), const)],
        out_specs=[pl.BlockSpec((kc * CHUNK, hv), rowblk(0)),
                   pl.BlockSpec((kc, GLA_HEADS, LANES, LANES), lambda b, t: (b * ns + t, 0, 0, 0))],
        out_shape=[_sds((r, hv), BF16), _sds((bsz * nc, GLA_HEADS, LANES, LANES), F32)],
        scratch_shapes=[pltpu.VMEM((GLA_HEADS, LANES, LANES), F32)],
        compiler_params=_params(("parallel", "arbitrary")),
    )(u, u, u, u, gup, gb, gn)


def _outproj_fwd(s0, co, go, w_out, g1, b1, tm):
    r, d = s0.shape
    dc = co.shape[1]

    def body(s0_ref, co_ref, go_ref, w_ref, g_ref, b_ref, p1_ref, s1_ref, s1b_ref):
        mix = _mm(co_ref[...], w_ref[0:dc, :]) + _mm(go_ref[...], w_ref[dc:2 * dc, :])
        p1 = ALPHA * s0_ref[...] + mix
        p1_ref[...] = p1
        xhat, _ = _ln(p1)
        s1 = xhat * g_ref[...] + b_ref[...]
        s1_ref[...] = s1
        s1b_ref[...] = s1.astype(BF16)

    row = lambda w: pl.BlockSpec((tm, w), lambda i: (i, 0))
    vec = pl.BlockSpec((1, d), lambda i: (0, 0))
    return pl.pallas_call(
        body, name="outproj_fwd", grid=(r // tm,),
        in_specs=[row(d), row(dc), row(dc), pl.BlockSpec((2 * dc, d), lambda i: (0, 0)), vec, vec],
        out_specs=[row(d), row(d), row(d)],
        out_shape=[_sds((r, d), F32), _sds((r, d), F32), _sds((r, d), BF16)],
        compiler_params=_params(("parallel",)),
    )(s0, co, go, w_out, g1, b1)


def _mlp_fwd(s1, s1b, w1g, w2, g2, b2, tgt, tp, tm, ns):
    r, d = s1.shape
    nh, _, th = w1g.shape
    nj = nh // ns

    def body(s1_ref, sb_ref, w1_ref, w2_ref, g_ref, b_ref, t_ref, hm_ref, dp2_ref, dpb_ref, loss_ref, dg_ref, db_ref, acc_ref):
        i = pl.program_id(0)
        j = pl.program_id(1)

        @pl.when(jnp.logical_and(i == 0, j == 0))
        def _():
            loss_ref[...] = jnp.zeros_like(loss_ref)
            dg_ref[...] = jnp.zeros_like(dg_ref)
            db_ref[...] = jnp.zeros_like(db_ref)

        @pl.when(j == 0)
        def _():
            acc_ref[...] = jnp.zeros_like(acc_ref)

        for s in range(ns):
            h = _mm(sb_ref[...], w1_ref[s])
            hm_ref[:, s * th:(s + 1) * th] = h.astype(BF16)
            act = jnp.square(jnp.maximum(h, 0.0))
            acc_ref[...] += _mm(act.astype(BF16), w2_ref[s * th:(s + 1) * th, :])

        @pl.when(j == nj - 1)
        def _():
            p2 = ALPHA * s1_ref[...] + acc_ref[...]
            xhat, rstd = _ln(p2)
            s2 = xhat * g_ref[...] + b_ref[...]
            isx = _row_in_seq(i, tm, tp) >= X_OFF
            err = jnp.where(isx, s2 - t_ref[...], 0.0)
            loss_ref[...] += 0.5 * jnp.sum(jnp.mean(err * err, axis=-1, keepdims=True))
            dy = err * (1.0 / d)
            dg_ref[...] += _rowsum(dy * xhat)
            db_ref[...] += _rowsum(dy)
            dp2 = _ln_bwd(dy * g_ref[...], xhat, rstd)
            dp2_ref[...] = dp2
            dpb_ref[...] = dp2.astype(BF16)

    row = pl.BlockSpec((tm, d), lambda i, j: (i, 0))
    vec = pl.BlockSpec((1, d), lambda i, j: (0, 0))
    return pl.pallas_call(
        body, name="mlp_fwd", grid=(r // tm, nj),
        in_specs=[row, row, pl.BlockSpec((ns, d, th), lambda i, j: (j, 0, 0)), pl.BlockSpec((ns * th, d), lambda i, j: (j, 0)),
                  vec, vec, row],
        out_specs=[pl.BlockSpec((tm, ns * th), lambda i, j: (i, j)), row, row,
                   pl.BlockSpec((8, LANES), lambda i, j: (0, 0)), vec, vec],
        out_shape=[_sds((r, nh * th), BF16), _sds((r, d), F32), _sds((r, d), BF16), _sds((8, LANES), F32),
                   _sds((1, d), F32), _sds((1, d), F32)],
        scratch_shapes=[pltpu.VMEM((tm, d), F32)],
        compiler_params=_params(("arbitrary", "arbitrary")),
    )(s1, s1b, w1g, w2, g2, b2, tgt)


def _mlp_bwd_act(dp2, dpb, hm, w1g, w2, p1, g1, tm, ns):
    r, d = dp2.shape
    nh, _, th = w1g.shape
    nj = nh // ns

    def body(dp2_ref, dpb_ref, hm_ref, w1_ref, w2_ref, p1_ref, g_ref, dh_ref, dp1_ref, dg_ref, db_ref, acc_ref):
        i = pl.program_id(0)
        j = pl.program_id(1)

        @pl.when(jnp.logical_and(i == 0, j == 0))
        def _():
            dg_ref[...] = jnp.zeros_like(dg_ref)
            db_ref[...] = jnp.zeros_like(db_ref)

        @pl.when(j == 0)
        def _():
            acc_ref[...] = jnp.zeros_like(acc_ref)

        for s in range(ns):
            cols = slice(s * th, (s + 1) * th)
            dact = _mm_nt(dpb_ref[...], w2_ref[cols, :])
            dh = (dact * (2.0 * jnp.maximum(hm_ref[:, cols].astype(F32), 0.0))).astype(BF16)
            dh_ref[:, cols] = dh
            acc_ref[...] += _mm_nt(dh, w1_ref[s])

        @pl.when(j == nj - 1)
        def _():
            ds1 = ALPHA * dp2_ref[...] + acc_ref[...]
            xhat, rstd = _ln(p1_ref[...])
            dg_ref[...] += _rowsum(ds1 * xhat)
            db_ref[...] += _rowsum(ds1)
            dp1_ref[...] = _ln_bwd(ds1 * g_ref[...], xhat, rstd)

    row = pl.BlockSpec((tm, d), lambda i, j: (i, 0))
    vec = pl.BlockSpec((1, d), lambda i, j: (0, 0))
    blk = pl.BlockSpec((tm, ns * th), lambda i, j: (i, j))
    return pl.pallas_call(
        body, name="mlp_bwd_act", grid=(r // tm, nj),
        in_specs=[row, row, blk, pl.BlockSpec((ns, d, th), lambda i, j: (j, 0, 0)),
                  pl.BlockSpec((ns * th, d), lambda i, j: (j, 0)), row, vec],
        out_specs=[blk, row, vec, vec],
        out_shape=[_sds((r, nh * th), BF16), _sds((r, d), F32), _sds((1, d), F32), _sds((1, d), F32)],
        scratch_shapes=[pltpu.VMEM((tm, d), F32)],
        compiler_params=_params(("arbitrary", "arbitrary")),
    )(dp2, dpb, hm, w1g, w2, p1, g1)


def _mlp_bwd_w(s1b, hm, dh, dpb, nh, tm, ns):
    r, d = s1b.shape
    th = hm.shape[1] // nh

    def body(s1_ref, hm_ref, dh_ref, dp2_ref, dw1_ref, dw2_ref, a1_ref, a2_ref):
        i = pl.program_id(1)

        @pl.when(i == 0)
        def _():
            a1_ref[...] = jnp.zeros_like(a1_ref)
            a2_ref[...] = jnp.zeros_like(a2_ref)

        for s in range(ns):
            cols = slice(s * th, (s + 1) * th)
            act = jnp.square(jnp.maximum(hm_ref[:, cols].astype(F32), 0.0)).astype(BF16)
            a1_ref[s] += _mm_tn(s1_ref[...], dh_ref[:, cols])
            a2_ref[s] += _mm_tn(act, dp2_ref[...])

        @pl.when(i == pl.num_programs(1) - 1)
        def _():
            dw1_ref[...] = a1_ref[...].astype(BF16)
            dw2_ref[...] = a2_ref[...].astype(BF16)

    row = pl.BlockSpec((tm, d), lambda j, i: (i, 0))
    blk = pl.BlockSpec((tm, ns * th), lambda j, i: (i, j))
    return pl.pallas_call(
        body, name="mlp_bwd_w", grid=(nh // ns, r // tm),
        in_specs=[row, blk, blk, row],
        out_specs=[pl.BlockSpec((ns, d, th), lambda j, i: (j, 0, 0)), pl.BlockSpec((ns, th, d), lambda j, i: (j, 0, 0))],
        out_shape=[_sds((nh, d, th), BF16), _sds((nh, th, d), BF16)],
        scratch_shapes=[pltpu.VMEM((ns, d, th), F32), pltpu.VMEM((ns, th, d), F32)],
        compiler_params=_params(("parallel", "arbitrary")),
    )(s1b, hm, dh, dpb)


def _outproj_bwd(dp1, co, go, w_out, dep, tm):
    r, d = dp1.shape
    dc = co.shape[1]

    def body(dp_ref, co_ref, go_ref, w_ref, dep_ref, dmi_ref, dw_ref, acc_ref):
        i = pl.program_id(0)

        @pl.when(i == 0)
        def _():
            acc_ref[...] = jnp.zeros_like(acc_ref)

        dpb = dp_ref[...].astype(BF16)
        dmi_ref[...] = _mm_nt(dpb, w_ref[...])
        acc_ref[0:dc, :] += _mm_tn(co_ref[...], dpb)
        acc_ref[dc:2 * dc, :] += _mm_tn(go_ref[...], dpb)

        @pl.when(i == pl.num_programs(0) - 1)
        def _():
            dw_ref[...] = acc_ref[...].astype(BF16)

    row = lambda w: pl.BlockSpec((tm, w), lambda i: (i, 0))
    full = pl.BlockSpec((2 * dc, d), lambda i: (0, 0))
    return pl.pallas_call(
        body, name="outproj_bwd", grid=(r // tm,),
        in_specs=[row(d), row(dc), row(dc), full, pl.BlockSpec(memory_space=pl.ANY)],
        out_specs=[row(2 * dc), full],
        out_shape=[_sds((r, 2 * dc), F32), _sds((2 * dc, d), BF16)],
        scratch_shapes=[pltpu.VMEM((2 * dc, d), F32)],
        compiler_params=_params(("arbitrary",)),
    )(dp1, co, go, w_out, dep)


def _gla_bwd(u, dmi, sta, gup, gb, gn, dep, bsz, nc, kc):
    r = u.shape[0]
    hv = GLA_HEADS * GLA_DV
    hk = GLA_HEADS * GLA_DK
    ns = nc // kc

    def body(qk_ref, v_ref, r_ref, gd_ref, dgo_ref, sta_ref, gup_ref, gb_ref, gn_ref, dep_ref,
             dqk_ref, dv_ref, dr_ref, dgd_ref, dgn_ref, dgb_ref, dgup_ref, dst_ref):
        bi = pl.program_id(0)
        t = pl.program_id(1)

        @pl.when(jnp.logical_and(bi == 0, t == 0))
        def _():
            dgn_ref[...] = jnp.zeros_like(dgn_ref)
            dgb_ref[...] = jnp.zeros_like(dgb_ref)
            dgup_ref[...] = jnp.zeros_like(dgup_ref)

        @pl.when(t == 0)
        def _():
            dst_ref[...] = jnp.zeros_like(dst_ref)

        dsts = [dst_ref[h] for h in range(GLA_HEADS)]
        dgn = jnp.zeros((1, GLA_DV), F32)
        dgb = jnp.zeros((1, hk), F32)
        dgup = jnp.zeros((LANES, hk), F32)
        for j in reversed(range(kc)):
            rows = slice(j * CHUNK, (j + 1) * CHUNK)
            gd = gd_ref[rows, :]
            p = _gla_prep(qk_ref[rows, :], gd, gup_ref[...], gb_ref[...], (ns - 1 - t) * kc + j)
            tril = p["ri"] >= p["ci"]
            dqe_t, dke_t, dkd_t, dgam_t = [], [], [], []
            for hp in range(GLA_HEADS // 2):
                ls = slice(hp * LANES, (hp + 1) * LANES)
                dqe = jnp.zeros((CHUNK, LANES), F32)
                dke = jnp.zeros((CHUNK, LANES), F32)
                dkd = jnp.zeros((CHUNK, LANES), F32)
                dgam = jnp.zeros((1, LANES), F32)
                for h2 in range(2):
                    h = 2 * hp + h2
                    vs = slice(h * GLA_DV, (h + 1) * GLA_DV)
                    m = _head_mask(h2)
                    qeh = jnp.where(m, p["qe"][:, ls], 0.0).astype(BF16)
                    kdh = jnp.where(m, p["kd"][:, ls], 0.0).astype(BF16)
                    keh = p["ke"][:, ls].astype(BF16)
                    vh = v_ref[rows, vs].astype(BF16)
                    st = sta_ref[j, h]
                    stb = st.astype(BF16)
                    a = jnp.where(tril, _mm_nt(qeh, keh), 0.0)
                    ab = a.astype(BF16)
                    o = _mm(ab, vh) + _mm_nt(qeh, stb)
                    rr = r_ref[rows, vs]
                    sr = _sigmoid(rr)
                    rs = lax.rsqrt(jnp.mean(o * o, axis=-1, keepdims=True) + LN_EPS)
                    y = o * rs
                    dgo = dgo_ref[rows, vs]
                    don = dgo * (rr * sr)
                    dr_ref[rows, vs] = dgo * (y * gn_ref[...]) * (sr * (1.0 + rr * (1.0 - sr)))
                    dgn = dgn + _rowsum(don * y)
                    dxn = don * gn_ref[...]
                    do = rs * (dxn - y * jnp.mean(dxn * y, axis=-1, keepdims=True))
                    dob = do.astype(BF16)
                    dst = dsts[h]
                    dstb = dst.astype(BF16)
                    da = jnp.where(tril, _mm_nt(dob, vh), 0.0).astype(BF16)
                    dv_ref[rows, vs] = _mm_tn(ab, dob) + _mm_nt(kdh, dstb)
                    dqe = dqe + jnp.where(m, _mm(da, keh) + _mm(dob, stb), 0.0)
                    dke = dke + _mm_tn(da, qeh)
                    dkd = dkd + jnp.where(m, _mm(vh, dstb), 0.0)
                    dgam = dgam + _rowsum(dst * st)
                    dsts[h] = dst * p["gam"][:, ls] + _mm_tn(dob, qeh)
                dqe_t.append(dqe)
                dke_t.append(dke)
                dkd_t.append(dkd)
                dgam_t.append(dgam)
            dqe = jnp.concatenate(dqe_t, axis=1)
            dke = jnp.concatenate(dke_t, axis=1)
            dkd = jnp.concatenate(dkd_t, axis=1)
            dgam = jnp.concatenate(dgam_t, axis=1)
            dqk_ref[rows, :hk] = dqe * p["eb"] * QK_SCALE
            dqk_ref[rows, hk:] = dke * p["enb"] + dkd * p["ebl"]
            dkdkd = dkd * p["kd"]
            db = dqe * p["qe"] - dke * p["ke"] - dkdkd
            dbl = _rowsum(dkdkd) + dgam * p["gam"]
            upper = (p["ri"] <= p["ci"]).astype(BF16)
            dlg = _tri_mm(upper, db) + dbl
            dz = jnp.where(p["real"], dlg * (1.0 / GLA_TAU) * _sigmoid(-p["z"]), 0.0)
            dzb = dz.astype(BF16)
            dgb = dgb + _rowsum(dz)
            dgup = dgup + _mm_tn(gd.astype(BF16), dzb)
            dgd_ref[rows, :] = _mm_nt(dzb, gup_ref[...])
        for h in range(GLA_HEADS):
            dst_ref[h] = dsts[h]
        dgb_ref[...] += dgb
        dgup_ref[...] += dgup
        dgn_ref[...] += dgn

    rowblk = lambda col: (lambda b, t: (b * ns + ns - 1 - t, col))
    const = lambda b, t: (0, 0)
    return pl.pallas_call(
        body, name="gla_bwd", grid=(bsz, ns),
        in_specs=[pl.BlockSpec((kc * CHUNK, 2 * hk), rowblk(2)), pl.BlockSpec((kc * CHUNK, hv), rowblk(3)),
                  pl.BlockSpec((kc * CHUNK, hv), rowblk(4)), pl.BlockSpec((kc * CHUNK, LANES), rowblk(20)),
                  pl.BlockSpec((kc * CHUNK, hv), rowblk(1)),
                  pl.BlockSpec((kc, GLA_HEADS, LANES, LANES), lambda b, t: (b * ns + ns - 1 - t, 0, 0, 0)),
                  pl.BlockSpec((LANES, 256), const), pl.BlockSpec((1, 256), const), pl.BlockSpec((1, GLA_DV), const),
                  pl.BlockSpec(memory_space=pl.ANY)],
        out_specs=[pl.BlockSpec((kc * CHUNK, 2 * hk), rowblk(0)), pl.BlockSpec((kc * CHUNK, hv), rowblk(0)),
                   pl.BlockSpec((kc * CHUNK, hv), rowblk(0)), pl.BlockSpec((kc * CHUNK, LANES), rowblk(0)),
                   pl.BlockSpec((1, GLA_DV), const), pl.BlockSpec((1, 256), const), pl.BlockSpec((LANES, 256), const)],
        out_shape=[_sds((r, 2 * hk), F32), _sds((r, hv), F32), _sds((r, hv), F32), _sds((r, LANES), F32),
                   _sds((1, GLA_DV), F32), _sds((1, 256), F32), _sds((LANES, 256), F32)],
        scratch_shapes=[pltpu.VMEM((GLA_HEADS, LANES, LANES), F32)],
        compiler_params=_params(("arbitrary", "arbitrary")),
    )(u, u, u, u, dmi, sta, gup, gb, gn, dep)


def _conv_bwd(u, c, dmi, w32, cg, cbe, tp, tc, dc):
    r = u.shape[0]
    hb = tc // CONV_HALO
    nhalo = r // CONV_HALO

    def dconv(cv, dco, cg_ref, cbe_ref):
        xhat, rstd = _ln(cv)
        cn = xhat * cg_ref[...] + cbe_ref[...]
        sg = _sigmoid(cn)
        dcn = dco * (sg * (1.0 + cn * (1.0 - sg)))
        return _ln_bwd(dcn * cg_ref[...], xhat, rstd), dcn, xhat

    def body(a_ref, g_ref, ah_ref, gh_ref, c_ref, dco_ref, ch_ref, dcoh_ref, w_ref, cg_ref, cbe_ref,
             du_ref, dw_ref, dcb_ref, dcg_ref, dcbe_ref, hs_ref, dcs_ref):
        t = pl.program_id(0)

        @pl.when(t == 0)
        def _():
            dw_ref[...] = jnp.zeros_like(dw_ref)
            dcb_ref[...] = jnp.zeros_like(dcb_ref)
            dcg_ref[...] = jnp.zeros_like(dcg_ref)
            dcbe_ref[...] = jnp.zeros_like(dcbe_ref)

        first = lax.rem(t * tc, tp) == 0
        last = lax.rem((t + 1) * tc, tp) == 0
        hh = ah_ref[...] * _sigmoid(gh_ref[...])
        hs_ref[0:CONV_HALO, :] = jnp.where(first, 0.0, hh)
        hs_ref[CONV_HALO:CONV_HALO + tc, :] = a_ref[...] * _sigmoid(g_ref[...])
        dch, _, _ = dconv(ch_ref[...], dcoh_ref[...], cg_ref, cbe_ref)
        dcs_ref[tc:tc + CONV_HALO, :] = jnp.where(last, 0.0, dch)

        def sub1(k, carry):
            r0 = pl.multiple_of(k * CONV_SUB, CONV_SUB)
            dcv, dcn, xhat = dconv(c_ref[pl.ds(r0, CONV_SUB), :], dco_ref[pl.ds(r0, CONV_SUB), :], cg_ref, cbe_ref)
            dcs_ref[pl.ds(r0, CONV_SUB), :] = dcv
            dcb_ref[...] += _rowsum(dcv)
            dcg_ref[...] += _rowsum(dcn * xhat)
            dcbe_ref[...] += _rowsum(dcn)
            return carry

        lax.fori_loop(0, tc // CONV_SUB, sub1, 0)

        def sub2(k, carry):
            r0 = pl.multiple_of(k * CONV_SUB, CONV_SUB)
            dwin = dcs_ref[pl.ds(r0, CONV_WIN), :]
            dh = _conv_taps(dwin, lambda o: w_ref[CONV_WIDTH - 1 - o:CONV_WIDTH - o, :], 0)
            av = a_ref[pl.ds(r0, CONV_SUB), :]
            sg = _sigmoid(g_ref[pl.ds(r0, CONV_SUB), :])
            du_ref[pl.ds(r0, CONV_SUB), 0:dc] = dh * sg
            du_ref[pl.ds(r0, CONV_SUB), dc:2 * dc] = dh * av * sg * (1.0 - sg)
            hwin = hs_ref[pl.ds(r0, CONV_WIN), :]
            dcv = dwin[0:CONV_SUB, :]
            for rho in range(8):
                offs = [o for o in range(2, 2 + CONV_WIDTH) if o % 8 == rho]
                rolled = hwin if rho == 0 else pltpu.roll(hwin, CONV_WIN - rho, 0)
                for o in offs:
                    m8 = o - rho
                    dw_ref[o - 2:o - 1, :] += _rowsum(dcv * rolled[m8:m8 + CONV_SUB, :])
            return carry

        lax.fori_loop(0, tc // CONV_SUB, sub2, 0)

    vec = pl.BlockSpec((1, dc), lambda t: (0, 0))
    prev = lambda col: (lambda t: (jnp.maximum(t * hb - 1, 0), col))
    nxt = lambda col: (lambda t: (jnp.minimum((t + 1) * hb, nhalo - 1), col))
    return pl.pallas_call(
        body, name="conv_bwd", grid=(r // tc,),
        in_specs=[pl.BlockSpec((tc, dc), lambda t: (t, 0)), pl.BlockSpec((tc, dc), lambda t: (t, 1)),
                  pl.BlockSpec((CONV_HALO, dc), prev(0)), pl.BlockSpec((CONV_HALO, dc), prev(1)),
                  pl.BlockSpec((tc, dc), lambda t: (t, 0)), pl.BlockSpec((tc, dc), lambda t: (t, 0)),
                  pl.BlockSpec((CONV_HALO, dc), nxt(0)), pl.BlockSpec((CONV_HALO, dc), nxt(0)),
                  pl.BlockSpec((32, dc), lambda t: (0, 0)), vec, vec],
        out_specs=[pl.BlockSpec((tc, 2 * dc), lambda t: (t, 0)), pl.BlockSpec((32, dc), lambda t: (0, 0)), vec, vec, vec],
        out_shape=[_sds((r, 2 * dc), F32), _sds((32, dc), F32), _sds((1, dc), F32), _sds((1, dc), F32), _sds((1, dc), F32)],
        scratch_shapes=[pltpu.VMEM((CONV_HALO + tc, dc), F32), pltpu.VMEM((tc + CONV_HALO, dc), F32)],
        compiler_params=_params(("arbitrary",)),
    )(u, u, u, u, c, dmi, c, dmi, w32, cg, cbe)


def _inproj_bwd(dp1, dus, xcat, g_in, w_in, dep, tp, tm):
    r, d = dp1.shape
    widths = [x.shape[1] for x in dus]
    offs = [sum(widths[:k]) for k in range(len(widths))]
    n = w_in.shape[1]
    nd = len(dus)
    tps = tp // tm

    def body(*refs):
        dp_ref = refs[0]
        du_refs = refs[1:1 + nd]
        x_ref, g_ref, w_ref, _, dx_ref, dmeta_ref, dg_ref, db_ref = refs[1 + nd:]
        i = pl.program_id(0)

        @pl.when(i == 0)
        def _():
            dmeta_ref[...] = jnp.zeros_like(dmeta_ref)
            dg_ref[...] = jnp.zeros_like(dg_ref)
            db_ref[...] = jnp.zeros_like(db_ref)

        ds0 = ALPHA * dp_ref[...]
        for k in range(nd):
            ds0 = ds0 + _mm_nt(du_refs[k][...].astype(BF16), w_ref[:, offs[k]:offs[k] + widths[k]])
        real = _row_in_seq(i, tm, tp) >= PAD_FRONT
        ds0 = jnp.where(real, ds0, 0.0)
        xhat, rstd = _ln(x_ref[...])
        dg_ref[...] += _rowsum(ds0 * xhat)
        db_ref[...] += _rowsum(ds0)
        dx = jnp.where(real, _ln_bwd(ds0 * g_ref[...], xhat, rstd), 0.0)
        dx_ref[...] = dx

        @pl.when(lax.rem(i, tps) == 0)
        def _():
            dmeta_ref[...] += dx[PAD_FRONT:X_OFF, :]

    row = lambda w: pl.BlockSpec((tm, w), lambda i: (i, 0))
    vec = pl.BlockSpec((1, d), lambda i: (0, 0))
    return pl.pallas_call(
        body, name="inproj_bwd", grid=(r // tm,),
        in_specs=[row(d)] + [row(w) for w in widths] + [row(d), vec, pl.BlockSpec((d, n), lambda i: (0, 0)),
                                                        pl.BlockSpec(memory_space=pl.ANY)],
        out_specs=[row(d), pl.BlockSpec((N_META, d), lambda i: (0, 0)), vec, vec],
        out_shape=[_sds((r, d), F32), _sds((N_META, d), F32), _sds((1, d), F32), _sds((1, d), F32)],
        compiler_params=_params(("arbitrary",)),
    )(dp1, *dus, xcat, g_in, w_in, dep)


def _inproj_bwd_w(s0, dus, tm):
    r, d = s0.shape
    widths = [x.shape[1] for x in dus]
    nd = len(dus)

    def body(*refs):
        s_ref = refs[0]
        du_refs = refs[1:1 + nd]
        dw_refs = refs[1 + nd:]
        i = pl.program_id(0)

        @pl.when(i == 0)
        def _():
            for k in range(nd):
                dw_refs[k][...] = jnp.zeros_like(dw_refs[k])

        sb = s_ref[...].astype(BF16)
        for k in range(nd):
            dw_refs[k][...] += _mm_tn(sb, du_refs[k][...].astype(BF16))

    row = lambda w: pl.BlockSpec((tm, w), lambda i: (i, 0))
    return pl.pallas_call(
        body, name="inproj_bwd_w", grid=(r // tm,),
        in_specs=[row(d)] + [row(w) for w in widths],
        out_specs=[pl.BlockSpec((d, w), lambda i: (0, 0)) for w in widths],
        out_shape=[_sds((d, w), F32) for w in widths],
        compiler_params=_params(("arbitrary",)),
    )(s0, *dus)


def _local_step(x, tgt, meta, ln_in_g, ln_in_b, w_in, conv_w, conv_b, conv_ln_g, conv_ln_b, gate_up, gate_bias,
                gla_norm_g, late_weights, ln1_g, ln1_b, ln2_g, ln2_b, push):
    bsz, seq, d = x.shape
    tp = X_OFF + seq
    assert tp % CHUNK == 0
    nc = tp // CHUNK
    r = bsz * tp
    dc = conv_b.shape[1]
    tm = _pick_tile(tp, (352, 128, 64))
    tc = _pick_tile(tp, (704, 128, 64))

    xcat = jnp.concatenate([jnp.zeros((bsz, PAD_FRONT, d), F32), jnp.broadcast_to(meta[None], (bsz, N_META, d)), x],
                           axis=1).reshape(r, d)
    tgt_p = jnp.pad(tgt, ((0, 0), (X_OFF, 0), (0, 0))).reshape(r, d)
    w32 = jnp.pad(conv_w, ((0, 32 - CONV_WIDTH), (0, 0)))
    gup = jnp.pad(gate_up, ((0, LANES - GLA_RANK), (0, 0))).astype(BF16)

    s0, u = _inproj_fwd(xcat, ln_in_g, ln_in_b, w_in, tp, tm)
    c, co = _conv_fwd(u, w32, conv_b, conv_ln_g, conv_ln_b, tp, tc, dc)
    kc = _pick_tile(nc, (3, 2, 1))
    go, sta = _gla_fwd(u, gup, gate_bias, gla_norm_g, bsz, nc, kc)
    w_out, w1g, w2 = late_weights(go)
    nh = w1g.shape[0]
    tmm = _pick_tile(tp, (704, 128, 64))
    ns = 2
    p1, s1, s1b = _outproj_fwd(s0, co, go, w_out, ln1_g, ln1_b, tm)
    hm, dp2, dpb, loss, dg2, db2 = _mlp_fwd(s1, s1b, w1g, w2, ln2_g, ln2_b, tgt_p, tp, tmm, ns)

    dh, dp1, dg1, db1 = _mlp_bwd_act(dp2, dpb, hm, w1g, w2, p1, ln1_g, tmm, ns)
    dw1, dw2 = _mlp_bwd_w(s1b, hm, dh, dpb, nh, tmm, ns)
    tok = push("ff", (dw1, dw2))
    dmi, dwo = _outproj_bwd(dp1, co, go, w_out, tok, tm)
    tok = push("out", (dwo,))
    dqk, dv, dr, dgd, dgn, dgb, dgup = _gla_bwd(u, dmi, sta, gup, gate_bias, gla_norm_g, tok, bsz, nc, kc)
    dcv, dcw, dcb, dcg, dcbe = _conv_bwd(u, c, dmi, w32, conv_ln_g, conv_ln_b, tp, tc, dc)
    dus = [dcv, dqk, dv, dr, dgd]
    dwi = _inproj_bwd_w(s0, dus, tm)
    tok = push("in", (jnp.concatenate(dwi, axis=1),))
    dxcat, dmeta, dgi, dbi = _inproj_bwd(dp1, dus, xcat, ln_in_g, w_in, tok, tp, tm)

    grad_x = dxcat.reshape(bsz, tp, d)[:, X_OFF:, :]
    return dict(loss=loss[0, 0], grad_x=grad_x, meta_tokens=dmeta, ln_in_g=dgi, ln_in_b=dbi,
                conv_w=dcw[:CONV_WIDTH], conv_b=dcb, conv_ln_g=dcg, conv_ln_b=dcbe,
                gate_up=dgup[:GLA_RANK], gate_bias=dgb, gla_norm_g=dgn, ln1_g=dg1, ln1_b=db1, ln2_g=dg2, ln2_b=db2)


def _exchange(arrays, scatter, name):
    na = len(arrays)
    npeer = N_DEV - 1

    def body(*refs):
        srcs = refs[:na]
        outs = refs[na:2 * na]
        send_sems, recv_sems, local_sems = refs[2 * na:]
        xi, yi, ci = (lax.axis_index(a) for a in MESH_AXES)
        me = 4 * xi + 2 * yi + ci
        copies = []
        for a in range(na):
            own = srcs[a].at[me] if scatter[a] else srcs[a]
            cp = pltpu.make_async_copy(own, outs[a].at[me], local_sems.at[a])
            cp.start()
            copies.append(cp)
        remote = []
        for k in range(1, N_DEV):
            px, py, pc = xi ^ (k >> 2), yi ^ ((k >> 1) & 1), ci ^ (k & 1)
            peer = 4 * px + 2 * py + pc
            for a in range(na):
                src = srcs[a].at[peer] if scatter[a] else srcs[a]
                cp = pltpu.make_async_remote_copy(
                    src_ref=src, dst_ref=outs[a].at[me],
                    send_sem=send_sems.at[a * npeer + k - 1], recv_sem=recv_sems.at[a * npeer + k - 1],
                    device_id=(px, py, pc), device_id_type=pl.DeviceIdType.MESH)
                cp.start()
                remote.append(cp)
        for cp in remote:
            cp.wait()
        for cp in copies:
            cp.wait()

    out_shape = [_sds(a.shape if scatter[i] else (N_DEV,) + a.shape, a.dtype) for i, a in enumerate(arrays)]
    anyspec = pl.BlockSpec(memory_space=pl.ANY)
    return pl.pallas_call(
        body, name=name,
        in_specs=[anyspec] * na, out_specs=[anyspec] * na, out_shape=out_shape,
        scratch_shapes=[pltpu.SemaphoreType.DMA((na * npeer,)), pltpu.SemaphoreType.DMA((na * npeer,)),
                        pltpu.SemaphoreType.DMA((na,))],
    )(*arrays)


def _peers(xi, yi, ci):
    for k in range(1, N_DEV):
        px, py, pc = xi ^ (k >> 2), yi ^ ((k >> 1) & 1), ci ^ (k & 1)
        yield (px, py, pc), 4 * px + 2 * py + pc


def _sc_exchange(arrays, scatter, name, collective_id, after=None):
    na = len(arrays)
    npeer = N_DEV - 1
    ndep = 0 if after is None else 1

    def body(*refs):
        srcs = refs[:na]
        outs = refs[na + ndep:2 * na + ndep]
        send_sems, recv_sems, own_sems = refs[2 * na + ndep:]
        xi, yi, ci = (lax.axis_index(a) for a in MESH_AXES)
        me = 4 * xi + 2 * yi + ci
        barrier = pltpu.get_barrier_semaphore()
        for pos, _ in _peers(xi, yi, ci):
            pl.semaphore_signal(barrier, inc=1, device_id=pos, device_id_type=pl.DeviceIdType.MESH)
        pl.semaphore_wait(barrier, npeer)
        own = [pltpu.make_async_copy(srcs[a].at[me] if scatter[a] else srcs[a], outs[a].at[me], own_sems.at[a])
               for a in range(na)]
        for cp in own:
            cp.start()
        remote = []
        for a in range(na):
            for k, (pos, peer) in enumerate(_peers(xi, yi, ci)):
                cp = pltpu.make_async_remote_copy(
                    src_ref=srcs[a].at[peer] if scatter[a] else srcs[a], dst_ref=outs[a].at[me],
                    send_sem=send_sems.at[a * npeer + k], recv_sem=recv_sems.at[a * npeer + k],
                    device_id=pos, device_id_type=pl.DeviceIdType.MESH)
                cp.start()
                remote.append(cp)
        for cp in own:
            cp.wait()
        for cp in remote:
            cp.wait()

    out_type = [_sds(a.shape if scatter[i] else (N_DEV,) + a.shape, a.dtype) for i, a in enumerate(arrays)]
    sent = sum(a.size * a.dtype.itemsize // (N_DEV if scatter[i] else 1) for i, a in enumerate(arrays))
    return pl.kernel(
        body, out_type=out_type, mesh=plsc.ScalarSubcoreMesh(axis_name="seq", num_cores=1), name=name,
        scratch_types=[pltpu.SemaphoreType.DMA((na * npeer,)), pltpu.SemaphoreType.DMA((na * npeer,)),
                       pltpu.SemaphoreType.DMA((na,))],
        compiler_params=pltpu.CompilerParams(collective_id=collective_id),
        cost_estimate=pl.CostEstimate(flops=0, transcendentals=0, bytes_accessed=2 * N_DEV * sent,
                                      remote_bytes_transferred=npeer * sent),
    )(*arrays, *([] if after is None else [after]))


def _push_start(arrays, scatter, name, dep=None):
    na = len(arrays)
    shapes = [a.shape if scatter[i] else (N_DEV,) + a.shape for i, a in enumerate(arrays)]
    hbm = pl.BlockSpec(memory_space=pltpu.HBM)
    sem = pl.BlockSpec(memory_space=pltpu.SEMAPHORE)
    ndep = 0 if dep is None else 1

    def body(*refs):
        srcs = refs[:na]
        lands = refs[na:2 * na]
        send_sems, recv_sems = refs[2 * na + ndep:2 * na + ndep + 2]
        own_sems = refs[4 * na + ndep + 2]
        xi, yi, ci = (lax.axis_index(a) for a in MESH_AXES)
        me = 4 * xi + 2 * yi + ci
        own = [pltpu.make_async_copy(srcs[a].at[me] if scatter[a] else srcs[a], lands[a].at[me], own_sems.at[a])
               for a in range(na)]
        for cp in own:
            cp.start()
        for cp in own:
            cp.wait()
        for a in range(na):
            for pos, peer in _peers(xi, yi, ci):
                pltpu.make_async_remote_copy(
                    src_ref=srcs[a].at[peer] if scatter[a] else srcs[a], dst_ref=lands[a].at[me],
                    send_sem=send_sems.at[a], recv_sem=recv_sems.at[a],
                    device_id=pos, device_id_type=pl.DeviceIdType.MESH).start()

    ins = [pltpu.with_memory_space_constraint(a, pltpu.HBM) for a in arrays]
    ins += [pltpu.with_memory_space_constraint(lax.empty(s, a.dtype), pltpu.HBM) for s, a in zip(shapes, arrays)]
    res = pl.pallas_call(
        body, name=name,
        in_specs=[hbm] * (2 * na) + [pl.BlockSpec(memory_space=pl.ANY)] * ndep,
        out_specs=[sem, sem] + [hbm] * (2 * na),
        out_shape=[pltpu.SemaphoreType.DMA((na,)), pltpu.SemaphoreType.DMA((na,))]
                  + [pltpu.HBM(a.shape, a.dtype) for a in arrays] + [pltpu.HBM(s, a.dtype) for s, a in zip(shapes, arrays)],
        input_output_aliases={i: 2 + i for i in range(2 * na)},
        scratch_shapes=[pltpu.SemaphoreType.DMA((na,))],
        compiler_params=pltpu.CompilerParams(has_side_effects=pltpu.SideEffectType.DATAFLOW_SIDE_EFFECTING),
    )(*ins, *([] if dep is None else [dep]))
    return (res[0], res[1], list(res[2:2 + na]), list(res[2 + na:2 + 2 * na])), res[2]


def _push_wait(handle, after, name):
    send_sems, recv_sems, srcs, lands = handle
    na = len(srcs)
    hbm = pl.BlockSpec(memory_space=pltpu.HBM)
    sem = pl.BlockSpec(memory_space=pltpu.SEMAPHORE)

    def body(*refs):
        land_refs = refs[na:2 * na]
        send_ref, recv_ref = refs[2 * na:2 * na + 2]
        me = tuple(lax.axis_index(a) for a in MESH_AXES)
        for a in range(na):
            seven = land_refs[a].at[pl.ds(0, N_DEV - 1)]
            cp = pltpu.make_async_remote_copy(src_ref=seven, dst_ref=seven, send_sem=send_ref.at[a], recv_sem=recv_ref.at[a],
                                              device_id=me, device_id_type=pl.DeviceIdType.MESH)
            cp.wait_send()
            cp.wait_recv()

    res = pl.pallas_call(
        body, name=name,
        in_specs=[hbm] * (2 * na) + [sem, sem, pl.BlockSpec(memory_space=pl.ANY)],
        out_specs=[hbm] * (2 * na),
        out_shape=[pltpu.HBM(a.shape, a.dtype) for a in srcs] + [pltpu.HBM(a.shape, a.dtype) for a in lands],
        input_output_aliases={i: i for i in range(2 * na)},
        compiler_params=pltpu.CompilerParams(has_side_effects=pltpu.SideEffectType.DATAFLOW_SIDE_EFFECTING),
    )(*srcs, *lands, send_sems, recv_sems, after)
    return list(res[na:])


def _adamw(w, g, m, v):
    m = ADAM_B1 * m + (1.0 - ADAM_B1) * g
    v = ADAM_B2 * v + (1.0 - ADAM_B2) * jnp.square(g)
    m_hat = m / (1.0 - ADAM_B1 ** ADAM_STEP)
    v_hat = v / (1.0 - ADAM_B2 ** ADAM_STEP)
    delta = -ADAM_LR * (m_hat / (jnp.sqrt(v_hat) + ADAM_EPS) + ADAM_WD * w)
    return delta, m, v


def _sum_devices(ref):
    g = ref[0].astype(F32)
    for k in range(1, N_DEV):
        g = g + ref[k].astype(F32)
    return g


def _update_big(parts, w, m, v, name):
    rows, cols = w.shape
    tr = _pick_tile(rows, (128, 64, 16))

    def body(p_ref, w_ref, m_ref, v_ref, g_ref, d_ref, nm_ref, nv_ref):
        g = _sum_devices(p_ref)
        g_ref[...] = g
        d_ref[...], nm_ref[...], nv_ref[...] = _adamw(w_ref[...], g, m_ref[...], v_ref[...])

    blk = pl.BlockSpec((tr, cols), lambda i: (i, 0))
    return pl.pallas_call(
        body, name=name, grid=(rows // tr,),
        in_specs=[pl.BlockSpec((N_DEV, tr, cols), lambda i: (0, i, 0)), blk, blk, blk],
        out_specs=[blk] * 4, out_shape=[_sds((rows, cols), F32)] * 4,
        compiler_params=_params(("parallel",)),
    )(parts, w, m, v)


_VEC_ORDER = ("ln_in_g", "ln_in_b", "conv_b", "conv_ln_g", "conv_ln_b", "gate_bias", "gla_norm_g",
              "ln1_g", "ln1_b", "ln2_g", "ln2_b")
_SHARDED_SMALL = (("meta_tokens", 0, N_META, LANES), ("conv_w", N_META, CONV_WIDTH, None), ("gate_up", N_META + 32, GLA_RANK, None))


def _update_small(parts_sh, parts_vec, wmv):
    names = [s[0] for s in _SHARDED_SMALL] + list(_VEC_ORDER)
    flat = [a for nme in names for a in wmv[nme]]
    nv = len(_VEC_ORDER)

    def body(*refs):
        sh_ref, vec_ref = refs[0], refs[1]
        ins = refs[2:2 + len(flat)]
        outs = refs[2 + len(flat):2 + len(flat) + 4 * len(names)]
        loss_ref = refs[2 + len(flat) + 4 * len(names)]
        gsh_ref, gvec_ref = refs[-2:]
        gsh_ref[...] = _sum_devices(sh_ref)
        gvec_ref[...] = _sum_devices(vec_ref)
        loss_ref[...] = gvec_ref[nv:nv + 1, :]
        for idx, nme in enumerate(names):
            w_ref, m_ref, v_ref = ins[3 * idx:3 * idx + 3]
            rows, cols = w_ref.shape
            if idx < len(_SHARDED_SMALL):
                r0 = _SHARDED_SMALL[idx][1]
                g = gsh_ref[r0:r0 + rows, 0:cols]
            else:
                j = idx - len(_SHARDED_SMALL)
                g = gvec_ref[j:j + 1, 0:cols]
            o = outs[4 * idx:4 * idx + 4]
            o[0][...] = g
            o[1][...], o[2][...], o[3][...] = _adamw(w_ref[...], g, m_ref[...], v_ref[...])

    out_shape = [_sds(wmv[nme][0].shape, F32) for nme in names for _ in range(4)] + [_sds((1, parts_vec.shape[2]), F32)]
    vmem = pl.BlockSpec(memory_space=pltpu.VMEM)
    res = pl.pallas_call(
        body, name="update_small", out_shape=out_shape,
        in_specs=[vmem] * (2 + len(flat)), out_specs=[vmem] * len(out_shape),
        scratch_shapes=[pltpu.VMEM(parts_sh.shape[1:], F32), pltpu.VMEM(parts_vec.shape[1:], F32)],
    )(parts_sh, parts_vec, *flat)
    return {nme: res[4 * i:4 * i + 4] for i, nme in enumerate(names)}, res[-1][0, 0]


_WEIGHTS = ("meta_tokens", "ln_in_g", "ln_in_b", "w_in", "conv_w", "conv_b", "conv_ln_g", "conv_ln_b", "gate_up",
            "gate_bias", "gla_norm_g", "w_out", "ln1_g", "ln1_b", "w_ff1", "w_ff2", "ln2_g", "ln2_b")


def kernel(x, meta_tokens, ln_in_g, ln_in_b, w_in, conv_w, conv_b, conv_ln_g, conv_ln_b, gate_up, gate_bias, gla_norm_g, w_out, ln1_g, ln1_b, w_ff1, w_ff2, ln2_g, ln2_b, loss_target, m_meta_tokens, m_ln_in_g, m_ln_in_b, m_w_in, m_conv_w, m_conv_b, m_conv_ln_g, m_conv_ln_b, m_gate_up, m_gate_bias, m_gla_norm_g, m_w_out, m_ln1_g, m_ln1_b, m_w_ff1, m_w_ff2, m_ln2_g, m_ln2_b, v_meta_tokens, v_ln_in_g, v_ln_in_b, v_w_in, v_conv_w, v_conv_b, v_conv_ln_g, v_conv_ln_b, v_gate_up, v_gate_bias, v_gla_norm_g, v_w_out, v_ln1_g, v_ln1_b, v_w_ff1, v_w_ff2, v_ln2_g, v_ln2_b):
    w = dict(meta_tokens=meta_tokens, ln_in_g=ln_in_g, ln_in_b=ln_in_b, w_in=w_in, conv_w=conv_w, conv_b=conv_b,
             conv_ln_g=conv_ln_g, conv_ln_b=conv_ln_b, gate_up=gate_up, gate_bias=gate_bias, gla_norm_g=gla_norm_g,
             w_out=w_out, ln1_g=ln1_g, ln1_b=ln1_b, w_ff1=w_ff1, w_ff2=w_ff2, ln2_g=ln2_g, ln2_b=ln2_b)
    mom = dict(meta_tokens=m_meta_tokens, ln_in_g=m_ln_in_g, ln_in_b=m_ln_in_b, w_in=m_w_in, conv_w=m_conv_w,
               conv_b=m_conv_b, conv_ln_g=m_conv_ln_g, conv_ln_b=m_conv_ln_b, gate_up=m_gate_up, gate_bias=m_gate_bias,
               gla_norm_g=m_gla_norm_g, w_out=m_w_out, ln1_g=m_ln1_g, ln1_b=m_ln1_b, w_ff1=m_w_ff1, w_ff2=m_w_ff2,
               ln2_g=m_ln2_g, ln2_b=m_ln2_b)
    var = dict(meta_tokens=v_meta_tokens, ln_in_g=v_ln_in_g, ln_in_b=v_ln_in_b, w_in=v_w_in, conv_w=v_conv_w,
               conv_b=v_conv_b, conv_ln_g=v_conv_ln_g, conv_ln_b=v_conv_ln_b, gate_up=v_gate_up, gate_bias=v_gate_bias,
               gla_norm_g=v_gla_norm_g, w_out=v_w_out, ln1_g=v_ln1_g, ln1_b=v_ln1_b, w_ff1=v_w_ff1, w_ff2=v_w_ff2,
               ln2_g=v_ln2_g, ln2_b=v_ln2_b)
    shapes = {k: a.shape for k, a in w.items()}

    def two_d(a):
        return a.reshape(1, -1) if a.ndim == 1 else a.reshape(a.shape[-2:])

    w2d = {k: two_d(a) for k, a in w.items()}
    m2d = {k: two_d(a) for k, a in mom.items()}
    v2d = {k: two_d(a) for k, a in var.items()}
    d = x.shape[-1]
    d_in = w2d["w_in"].shape[1] * N_DEV
    d_in_p = -(-d_in // LANES) * LANES

    g_in, g_meta, g_conv, g_gup = _sc_exchange(
        [w2d["w_in"].astype(BF16), w2d["meta_tokens"], w2d["conv_w"], w2d["gate_up"]], [False] * 4, "gather_first", 0)
    g_out, g_ff1, g_ff2 = _sc_exchange(
        [w2d["w_out"].astype(BF16), w2d["w_ff1"].astype(BF16), w2d["w_ff2"].astype(BF16)], [False] * 3, "gather_late", 1,
        after=g_gup)
    w_in_full = jnp.pad(g_in.transpose(1, 0, 2).reshape(d, d_in), ((0, 0), (0, d_in_p - d_in)))
    meta_full = g_meta.transpose(1, 0, 2).reshape(N_META, d)
    conv_w_full = g_conv.transpose(1, 0, 2).reshape(CONV_WIDTH, -1)
    gate_up_full = g_gup.transpose(1, 0, 2).reshape(GLA_RANK, -1)

    def late_weights(after):
        return g_out.reshape(-1, d), g_ff1, g_ff2.reshape(-1, d)

    pushed = {}

    def push(tag, grads):
        if tag == "ff":
            pushed["ff1"], pushed["ff2"] = _sc_exchange(list(grads), [True, True], "scatter_ff", 2)
        elif tag == "out":
            pushed["p_out"] = grads[0].reshape(N_DEV, -1, d)
        else:
            p_in = grads[0][:, :d_in].reshape(d, N_DEV, d_in // N_DEV).transpose(1, 0, 2).astype(BF16)
            pushed["in"], pushed["out"] = _sc_exchange([p_in, pushed["p_out"]], [True, True], "scatter_rest", 3,
                                                       after=pushed["ff1"])
        return grads[0]

    res = _local_step(x, loss_target, meta_full, w2d["ln_in_g"], w2d["ln_in_b"], w_in_full, conv_w_full, w2d["conv_b"],
                      w2d["conv_ln_g"], w2d["conv_ln_b"], gate_up_full, w2d["gate_bias"], w2d["gla_norm_g"], late_weights,
                      w2d["ln1_g"], w2d["ln1_b"], w2d["ln2_g"], w2d["ln2_b"], push)

    dc = res["conv_w"].shape[1]
    hk = res["gate_up"].shape[1]
    sh_meta = res["meta_tokens"].reshape(N_META, N_DEV, LANES).transpose(1, 0, 2)
    sh_conv = jnp.pad(res["conv_w"].reshape(CONV_WIDTH, N_DEV, dc // N_DEV).transpose(1, 0, 2),
                      ((0, 0), (0, 32 - CONV_WIDTH), (0, LANES - dc // N_DEV)))
    sh_gup = jnp.pad(res["gate_up"].reshape(GLA_RANK, N_DEV, hk // N_DEV).transpose(1, 0, 2),
                     ((0, 0), (0, 0), (0, LANES - hk // N_DEV)))
    p_sh = jnp.concatenate([sh_meta, sh_conv, sh_gup], axis=1)
    p_vec = jnp.concatenate([jnp.pad(res[k], ((0, 0), (0, d - res[k].shape[1]))) for k in _VEC_ORDER]
                            + [jnp.full((1, d), res["loss"], F32), jnp.zeros((15 - len(_VEC_ORDER), d), F32)], axis=0)

    r_sh, r_vec = _exchange([p_sh, p_vec], [True, False], "scatter_small")
    r_ff1, r_ff2, r_out, r_in = pushed["ff1"], pushed["ff2"], pushed["out"], pushed["in"]

    upd = {}
    upd["w_in"] = _update_big(r_in, w2d["w_in"], m2d["w_in"], v2d["w_in"], "update_w_in")
    upd["w_out"] = _update_big(r_out, w2d["w_out"], m2d["w_out"], v2d["w_out"], "update_w_out")
    upd["w_ff1"] = _update_big(r_ff1, w2d["w_ff1"], m2d["w_ff1"], v2d["w_ff1"], "update_w_ff1")
    upd["w_ff2"] = _update_big(r_ff2, w2d["w_ff2"], m2d["w_ff2"], v2d["w_ff2"], "update_w_ff2")
    small = [s[0] for s in _SHARDED_SMALL] + list(_VEC_ORDER)
    upd_small, loss = _update_small(r_sh, r_vec, {k: (w2d[k], m2d[k], v2d[k]) for k in small})
    upd.update(upd_small)

    outs = [loss, res["grad_x"]]
    for j in range(4):
        outs += [upd[k][j].reshape(shapes[k]) for k in _WEIGHTS]
    return tuple(outs)
```

```python
import functools

import jax
import jax.numpy as jnp
from jax import lax
from jax.experimental import pallas as pl
from jax.experimental.pallas import tpu as pltpu
from jax.experimental.pallas import tpu_sc as plsc

F32 = jnp.float32
BF16 = jnp.bfloat16

N_META = 16
CHUNK = 64
PAD_FRONT = (-N_META) % CHUNK
X_OFF = PAD_FRONT + N_META
CONV_WIDTH = 31
CONV_HALO = 32
CONV_SUB = 64
CONV_WIN = CONV_SUB + CONV_HALO
GLA_HEADS = 4
GLA_DK = 64
GLA_DV = 128
GLA_RANK = 16
GLA_TAU = 16.0
QK_SCALE = GLA_DK ** -0.5
LN_EPS = 1e-5
ALPHA = 2.0 ** 0.25
LANES = 128
N_DEV = 8
ADAM_LR = 0.001
ADAM_B1 = 0.9
ADAM_B2 = 0.999
ADAM_EPS = 1e-08
ADAM_WD = 0.01
ADAM_STEP = 10
VMEM_LIMIT = 56 * 1024 * 1024
MESH_AXES = ("x", "y", "c")


def _sds(shape, dtype):
    return jax.ShapeDtypeStruct(shape, dtype)


def _mm(a, b):
    return jnp.dot(a, b, preferred_element_type=F32)


def _mm_nt(a, b):
    return lax.dot_general(a, b, (((1,), (1,)), ((), ())), preferred_element_type=F32)


def _mm_tn(a, b):
    return lax.dot_general(a, b, (((0,), (0,)), ((), ())), preferred_element_type=F32)


def _sigmoid(x):
    return 1.0 / (1.0 + jnp.exp(-x))


def _log_sigmoid(z):
    return jnp.minimum(z, 0.0) - jnp.log(1.0 + jnp.exp(-jnp.abs(z)))


def _ln(x):
    mu = jnp.mean(x, axis=-1, keepdims=True)
    xc = x - mu
    var = jnp.mean(xc * xc, axis=-1, keepdims=True)
    rstd = lax.rsqrt(var + LN_EPS)
    return xc * rstd, rstd


def _ln_bwd(dyg, xhat, rstd):
    m1 = jnp.mean(dyg, axis=-1, keepdims=True)
    m2 = jnp.mean(dyg * xhat, axis=-1, keepdims=True)
    return rstd * (dyg - m1 - xhat * m2)


def _rowsum(x):
    return jnp.sum(x, axis=0, keepdims=True)


def _row_in_seq(i, tm, tp):
    base = lax.rem(i * tm, tp)
    return base + lax.broadcasted_iota(jnp.int32, (tm, 1), 0)


def _split3(x):
    hi = x.astype(BF16)
    r1 = x - hi.astype(F32)
    mid = r1.astype(BF16)
    lo = (r1 - mid.astype(F32)).astype(BF16)
    return hi, mid, lo


def _tri_mm(tri, x):
    hi, mid, lo = _split3(x)
    return _mm(tri, hi) + _mm(tri, mid) + _mm(tri, lo)


def _params(sem):
    return pltpu.CompilerParams(dimension_semantics=sem, vmem_limit_bytes=VMEM_LIMIT)


def _pick_tile(n, prefs):
    for t in prefs:
        if n % t == 0:
            return t
    raise ValueError(f"no tile for {n}")


def _inproj_fwd(xcat, g, b, w_in, tp, tm):
    r, d = xcat.shape
    n = w_in.shape[1]

    def body(x_ref, g_ref, b_ref, w_ref, s0_ref, u_ref):
        i = pl.program_id(0)
        xhat, _ = _ln(x_ref[...])
        real = _row_in_seq(i, tm, tp) >= PAD_FRONT
        s = jnp.where(real, xhat * g_ref[...] + b_ref[...], 0.0)
        s0_ref[...] = s
        u_ref[...] = _mm(s.astype(BF16), w_ref[...])

    return pl.pallas_call(
        body, name="inproj_fwd", grid=(r // tm,),
        in_specs=[pl.BlockSpec((tm, d), lambda i: (i, 0)), pl.BlockSpec((1, d), lambda i: (0, 0)),
                  pl.BlockSpec((1, d), lambda i: (0, 0)), pl.BlockSpec((d, n), lambda i: (0, 0))],
        out_specs=[pl.BlockSpec((tm, d), lambda i: (i, 0)), pl.BlockSpec((tm, n), lambda i: (i, 0))],
        out_shape=[_sds((r, d), F32), _sds((r, n), F32)],
        compiler_params=_params(("parallel",)),
    )(xcat, g, b, w_in)


def _conv_taps(win, coef, lo):
    acc = None
    for rho in range(8):
        offs = [o for o in range(lo, lo + CONV_WIDTH) if o % 8 == rho]
        if not offs:
            continue
        rolled = win if rho == 0 else pltpu.roll(win, CONV_WIN - rho, 0)
        for o in offs:
            m8 = o - rho
            term = rolled[m8:m8 + CONV_SUB, :] * coef(o)
            acc = term if acc is None else acc + term
    return acc


def _conv_fwd(u, w32, cb, cg, cbe, tp, tc, dc):
    r = u.shape[0]
    hb = tc // CONV_HALO

    def body(a_ref, g_ref, ah_ref, gh_ref, w_ref, cb_ref, cg_ref, cbe_ref, c_ref, co_ref, hs_ref):
        t = pl.program_id(0)
        first = lax.rem(t * tc, tp) == 0
        hh = ah_ref[...] * _sigmoid(gh_ref[...])
        hs_ref[0:CONV_HALO, :] = jnp.where(first, 0.0, hh)
        hs_ref[CONV_HALO:CONV_HALO + tc, :] = a_ref[...] * _sigmoid(g_ref[...])

        def sub(k, carry):
            r0 = pl.multiple_of(k * CONV_SUB, CONV_SUB)
            win = hs_ref[pl.ds(r0, CONV_WIN), :]
            c = _conv_taps(win, lambda o: w_ref[o - 2:o - 1, :], 2) + cb_ref[...]
            c_ref[pl.ds(r0, CONV_SUB), :] = c
            xhat, _ = _ln(c)
            cn = xhat * cg_ref[...] + cbe_ref[...]
            co_ref[pl.ds(r0, CONV_SUB), :] = (cn * _sigmoid(cn)).astype(BF16)
            return carry

        lax.fori_loop(0, tc // CONV_SUB, sub, 0)

    vec = pl.BlockSpec((1, dc), lambda t: (0, 0))
    return pl.pallas_call(
        body, name="conv_fwd", grid=(r // tc,),
        in_specs=[pl.BlockSpec((tc, dc), lambda t: (t, 0)), pl.BlockSpec((tc, dc), lambda t: (t, 1)),
                  pl.BlockSpec((CONV_HALO, dc), lambda t: (jnp.maximum(t * hb - 1, 0), 0)),
                  pl.BlockSpec((CONV_HALO, dc), lambda t: (jnp.maximum(t * hb - 1, 0), 1)),
                  pl.BlockSpec((32, dc), lambda t: (0, 0)), vec, vec, vec],
        out_specs=[pl.BlockSpec((tc, dc), lambda t: (t, 0)), pl.BlockSpec((tc, dc), lambda t: (t, 0))],
        out_shape=[_sds((r, dc), F32), _sds((r, dc), BF16)],
        scratch_shapes=[pltpu.VMEM((CONV_HALO + tc, dc), F32)],
        compiler_params=_params(("parallel",)),
    )(u, u, u, u, w32, cb, cg, cbe)


def _tri_mm_all(tri, xs):
    parts = [_split3(x) for x in xs]
    acc = [None] * len(xs)
    for t in range(3):
        for j in range(len(xs)):
            term = _mm(tri, parts[j][t])
            acc[j] = term if t == 0 else acc[j] + term
    return acc


def _gla_prep(qk_ref, gd_ref, gup, gb, n0, kc):
    rows = [slice(j * CHUNK, (j + 1) * CHUNK) for j in range(kc)]
    ri = lax.broadcasted_iota(jnp.int32, (CHUNK, CHUNK), 0)
    ci = lax.broadcasted_iota(jnp.int32, (CHUNK, CHUNK), 1)
    low = (ri >= ci).astype(BF16)
    hk = GLA_HEADS * GLA_DK
    gds = [gd_ref[rw, :] for rw in rows]
    zs = [_mm(g.astype(BF16), gup) + gb for g in gds]
    reals = [(n0 + j) * CHUNK + lax.broadcasted_iota(jnp.int32, (CHUNK, 1), 0) >= PAD_FRONT for j in range(kc)]
    lgs = [jnp.where(reals[j], _log_sigmoid(zs[j]) * (1.0 / GLA_TAU), 0.0) for j in range(kc)]
    bs = _tri_mm_all(low, lgs)
    out = []
    for j in range(kc):
        b, bl = bs[j], _rowsum(lgs[j])
        q = qk_ref[rows[j], :hk] * QK_SCALE
        k = qk_ref[rows[j], hk:]
        eb, enb, ebl = jnp.exp(b), jnp.exp(-b), jnp.exp(bl - b)
        out.append(dict(rows=rows[j], gd=gds[j], z=zs[j], real=reals[j], eb=eb, enb=enb, ebl=ebl, gam=jnp.exp(bl),
                        qe=q * eb, ke=k * enb, kd=k * ebl))
    return out, ri, ci


def _gla_heads(p, v_ref):
    ops = []
    for h in range(GLA_HEADS):
        hp, h2 = divmod(h, 2)
        ls = slice(hp * LANES, (hp + 1) * LANES)
        m = _head_mask(h2)
        ops.append(dict(ls=ls, m=m, vs=slice(h * GLA_DV, (h + 1) * GLA_DV),
                        qe=jnp.where(m, p["qe"][:, ls], 0.0).astype(BF16),
                        kd=jnp.where(m, p["kd"][:, ls], 0.0).astype(BF16),
                        ke=p["ke"][:, ls].astype(BF16),
                        v=v_ref[p["rows"], h * GLA_DV:(h + 1) * GLA_DV].astype(BF16)))
    return ops


def _head_mask(h2):
    lane = lax.broadcasted_iota(jnp.int32, (1, LANES), 1)
    return (lane < GLA_DK) if h2 == 0 else (lane >= GLA_DK)


def _gla_fwd(u, gup, gb, gn, bsz, nc, kc):
    r = u.shape[0]
    hv = GLA_HEADS * GLA_DV
    ns = nc // kc

    def body(qk_ref, v_ref, r_ref, gd_ref, gup_ref, gb_ref, gn_ref, go_ref, sta_ref, st_ref):
        t = pl.program_id(1)

        @pl.when(t == 0)
        def _():
            st_ref[...] = jnp.zeros_like(st_ref)

        ps, ri, ci = _gla_prep(qk_ref, gd_ref, gup_ref[...], gb_ref[...], t * kc, kc)
        tril = ri >= ci
        items = [(j, h) for j in range(kc) for h in range(GLA_HEADS)]
        ops = [_gla_heads(p, v_ref) for p in ps]
        a = {jh: jnp.where(tril, _mm_nt(ops[jh[0]][jh[1]]["qe"], ops[jh[0]][jh[1]]["ke"]), 0.0).astype(BF16) for jh in items}
        oi = {jh: _mm(a[jh], ops[jh[0]][jh[1]]["v"]) for jh in items}
        inc = {jh: _mm_tn(ops[jh[0]][jh[1]]["v"], ops[jh[0]][jh[1]]["kd"]) for jh in items}
        sts = [st_ref[h] for h in range(GLA_HEADS)]
        for j, h in items:
            op, p = ops[j][h], ps[j]
            st = sts[h]
            sta_ref[j, h] = st
            o = oi[j, h] + _mm_nt(op["qe"], st.astype(BF16))
            sts[h] = st * p["gam"][:, op["ls"]] + inc[j, h]
            rs = lax.rsqrt(jnp.mean(o * o, axis=-1, keepdims=True) + LN_EPS)
            rr = r_ref[p["rows"], op["vs"]]
            go_ref[p["rows"], op["vs"]] = (o * rs * gn_ref[...] * (rr * _sigmoid(rr))).astype(BF16)
        for h in range(GLA_HEADS):
            st_ref[h] = sts[h]

    rowblk = lambda col: (lambda b, t: (b * ns + t, col))
    const = lambda b, t: (0, 0)
    return pl.pallas_call(
        body, name="gla_fwd", grid=(bsz, ns),
        in_specs=[pl.BlockSpec((kc * CHUNK, 512), rowblk(2)), pl.BlockSpec((kc * CHUNK, hv), rowblk(3)),
                  pl.BlockSpec((kc * CHUNK, hv), rowblk(4)), pl.BlockSpec((kc * CHUNK, LANES), rowblk(20)),
                  pl.BlockSpec((LANES, 256), const), pl.BlockSpec((1, 256), const), pl.BlockSpec((1, GLA_DV), const)],
        out_specs=[pl.BlockSpec((kc * CHUNK, hv), rowblk(0)),
                   pl.BlockSpec((kc, GLA_HEADS, LANES, LANES), lambda b, t: (b * ns + t, 0, 0, 0))],
        out_shape=[_sds((r, hv), BF16), _sds((bsz * nc, GLA_HEADS, LANES, LANES), F32)],
        scratch_shapes=[pltpu.VMEM((GLA_HEADS, LANES, LANES), F32)],
        compiler_params=_params(("parallel", "arbitrary")),
    )(u, u, u, u, gup, gb, gn)


def _outproj_fwd(s0, co, go, w_out, g1, b1, tm):
    r, d = s0.shape
    dc = co.shape[1]

    def body(s0_ref, co_ref, go_ref, w_ref, g_ref, b_ref, p1_ref, s1_ref, s1b_ref):
        mix = _mm(co_ref[...], w_ref[0:dc, :]) + _mm(go_ref[...], w_ref[dc:2 * dc, :])
        p1 = ALPHA * s0_ref[...] + mix
        p1_ref[...] = p1
        xhat, _ = _ln(p1)
        s1 = xhat * g_ref[...] + b_ref[...]
        s1_ref[...] = s1
        s1b_ref[...] = s1.astype(BF16)

    row = lambda w: pl.BlockSpec((tm, w), lambda i: (i, 0))
    vec = pl.BlockSpec((1, d), lambda i: (0, 0))
    return pl.pallas_call(
        body, name="outproj_fwd", grid=(r // tm,),
        in_specs=[row(d), row(dc), row(dc), pl.BlockSpec((2 * dc, d), lambda i: (0, 0)), vec, vec],
        out_specs=[row(d), row(d), row(d)],
        out_shape=[_sds((r, d), F32), _sds((r, d), F32), _sds((r, d), BF16)],
        compiler_params=_params(("parallel",)),
    )(s0, co, go, w_out, g1, b1)


def _mlp_fwd(s1, s1b, w1g, w2, g2, b2, tgt, tp, tm, ns):
    r, d = s1.shape
    nh, _, th = w1g.shape
    nj = nh // ns

    def body(s1_ref, sb_ref, w1_ref, w2_ref, g_ref, b_ref, t_ref, hm_ref, dp2_ref, dpb_ref, loss_ref, dg_ref, db_ref, acc_ref):
        i = pl.program_id(0)
        j = pl.program_id(1)

        @pl.when(jnp.logical_and(i == 0, j == 0))
        def _():
            loss_ref[...] = jnp.zeros_like(loss_ref)
            dg_ref[...] = jnp.zeros_like(dg_ref)
            db_ref[...] = jnp.zeros_like(db_ref)

        @pl.when(j == 0)
        def _():
            acc_ref[...] = jnp.zeros_like(acc_ref)

        hs = [_mm(sb_ref[...], w1_ref[s]) for s in range(ns)]
        acc = acc_ref[...]
        for s in range(ns):
            hm_ref[:, s * th:(s + 1) * th] = hs[s].astype(BF16)
            act = jnp.square(jnp.maximum(hs[s], 0.0))
            acc = acc + _mm(act.astype(BF16), w2_ref[s * th:(s + 1) * th, :])
        acc_ref[...] = acc

        @pl.when(j == nj - 1)
        def _():
            p2 = ALPHA * s1_ref[...] + acc_ref[...]
            xhat, rstd = _ln(p2)
            s2 = xhat * g_ref[...] + b_ref[...]
            isx = _row_in_seq(i, tm, tp) >= X_OFF
            err = jnp.where(isx, s2 - t_ref[...], 0.0)
            loss_ref[...] += 0.5 * jnp.sum(jnp.mean(err * err, axis=-1, keepdims=True))
            dy = err * (1.0 / d)
            dg_ref[...] += _rowsum(dy * xhat)
            db_ref[...] += _rowsum(dy)
            dp2 = _ln_bwd(dy * g_ref[...], xhat, rstd)
            dp2_ref[...] = dp2
            dpb_ref[...] = dp2.astype(BF16)

    row = pl.BlockSpec((tm, d), lambda i, j: (i, 0))
    vec = pl.BlockSpec((1, d), lambda i, j: (0, 0))
    return pl.pallas_call(
        body, name="mlp_fwd", grid=(r // tm, nj),
        in_specs=[row, row, pl.BlockSpec((ns, d, th), lambda i, j: (j, 0, 0)), pl.BlockSpec((ns * th, d), lambda i, j: (j, 0)),
                  vec, vec, row],
        out_specs=[pl.BlockSpec((tm, ns * th), lambda i, j: (i, j)), row, row,
                   pl.BlockSpec((8, LANES), lambda i, j: (0, 0)), vec, vec],
        out_shape=[_sds((r, nh * th), BF16), _sds((r, d), F32), _sds((r, d), BF16), _sds((8, LANES), F32),
                   _sds((1, d), F32), _sds((1, d), F32)],
        scratch_shapes=[pltpu.VMEM((tm, d), F32)],
        compiler_params=_params(("arbitrary", "arbitrary")),
    )(s1, s1b, w1g, w2, g2, b2, tgt)


def _mlp_bwd_act(dp2, dpb, hm, w1g, w2, p1, g1, tm, ns):
    r, d = dp2.shape
    nh, _, th = w1g.shape
    nj = nh // ns

    def body(dp2_ref, dpb_ref, hm_ref, w1_ref, w2_ref, p1_ref, g_ref, dh_ref, dp1_ref, dg_ref, db_ref, acc_ref):
        i = pl.program_id(0)
        j = pl.program_id(1)

        @pl.when(jnp.logical_and(i == 0, j == 0))
        def _():
            dg_ref[...] = jnp.zeros_like(dg_ref)
            db_ref[...] = jnp.zeros_like(db_ref)

        @pl.when(j == 0)
        def _():
            acc_ref[...] = jnp.zeros_like(acc_ref)

        dacts = [_mm_nt(dpb_ref[...], w2_ref[s * th:(s + 1) * th, :]) for s in range(ns)]
        acc = acc_ref[...]
        for s in range(ns):
            cols = slice(s * th, (s + 1) * th)
            dh = (dacts[s] * (2.0 * jnp.maximum(hm_ref[:, cols].astype(F32), 0.0))).astype(BF16)
            dh_ref[:, cols] = dh
            acc = acc + _mm_nt(dh, w1_ref[s])
        acc_ref[...] = acc

        @pl.when(j == nj - 1)
        def _():
            ds1 = ALPHA * dp2_ref[...] + acc_ref[...]
            xhat, rstd = _ln(p1_ref[...])
            dg_ref[...] += _rowsum(ds1 * xhat)
            db_ref[...] += _rowsum(ds1)
            dp1_ref[...] = _ln_bwd(ds1 * g_ref[...], xhat, rstd)

    row = pl.BlockSpec((tm, d), lambda i, j: (i, 0))
    vec = pl.BlockSpec((1, d), lambda i, j: (0, 0))
    blk = pl.BlockSpec((tm, ns * th), lambda i, j: (i, j))
    return pl.pallas_call(
        body, name="mlp_bwd_act", grid=(r // tm, nj),
        in_specs=[row, row, blk, pl.BlockSpec((ns, d, th), lambda i, j: (j, 0, 0)),
                  pl.BlockSpec((ns * th, d), lambda i, j: (j, 0)), row, vec],
        out_specs=[blk, row, vec, vec],
        out_shape=[_sds((r, nh * th), BF16), _sds((r, d), F32), _sds((1, d), F32), _sds((1, d), F32)],
        scratch_shapes=[pltpu.VMEM((tm, d), F32)],
        compiler_params=_params(("arbitrary", "arbitrary")),
    )(dp2, dpb, hm, w1g, w2, p1, g1)


def _mlp_bwd_w(s1b, hm, dh, dpb, nh, tm, ns):
    r, d = s1b.shape
    th = hm.shape[1] // nh

    def body(s1_ref, hm_ref, dh_ref, dp2_ref, dw1_ref, dw2_ref, a1_ref, a2_ref):
        i = pl.program_id(1)

        @pl.when(i == 0)
        def _():
            a1_ref[...] = jnp.zeros_like(a1_ref)
            a2_ref[...] = jnp.zeros_like(a2_ref)

        for s in range(ns):
            a1_ref[s] += _mm_tn(s1_ref[...], dh_ref[:, s * th:(s + 1) * th])
        for s in range(ns):
            act = jnp.square(jnp.maximum(hm_ref[:, s * th:(s + 1) * th].astype(F32), 0.0)).astype(BF16)
            a2_ref[s] += _mm_tn(act, dp2_ref[...])

        @pl.when(i == pl.num_programs(1) - 1)
        def _():
            dw1_ref[...] = a1_ref[...].astype(BF16)
            dw2_ref[...] = a2_ref[...].astype(BF16)

    row = pl.BlockSpec((tm, d), lambda j, i: (i, 0))
    blk = pl.BlockSpec((tm, ns * th), lambda j, i: (i, j))
    return pl.pallas_call(
        body, name="mlp_bwd_w", grid=(nh // ns, r // tm),
        in_specs=[row, blk, blk, row],
        out_specs=[pl.BlockSpec((ns, d, th), lambda j, i: (j, 0, 0)), pl.BlockSpec((ns, th, d), lambda j, i: (j, 0, 0))],
        out_shape=[_sds((nh, d, th), BF16), _sds((nh, th, d), BF16)],
        scratch_shapes=[pltpu.VMEM((ns, d, th), F32), pltpu.VMEM((ns, th, d), F32)],
        compiler_params=_params(("parallel", "arbitrary")),
    )(s1b, hm, dh, dpb)


def _outproj_bwd(dp1, co, go, w_out, dep, tm):
    r, d = dp1.shape
    dc = co.shape[1]

    def body(dp_ref, co_ref, go_ref, w_ref, dep_ref, dmi_ref, dw_ref, acc_ref):
        i = pl.program_id(0)

        @pl.when(i == 0)
        def _():
            acc_ref[...] = jnp.zeros_like(acc_ref)

        dpb = dp_ref[...].astype(BF16)
        dmi_ref[...] = _mm_nt(dpb, w_ref[...])
        acc_ref[0:dc, :] += _mm_tn(co_ref[...], dpb)
        acc_ref[dc:2 * dc, :] += _mm_tn(go_ref[...], dpb)

        @pl.when(i == pl.num_programs(0) - 1)
        def _():
            dw_ref[...] = acc_ref[...].astype(BF16)

    row = lambda w: pl.BlockSpec((tm, w), lambda i: (i, 0))
    full = pl.BlockSpec((2 * dc, d), lambda i: (0, 0))
    return pl.pallas_call(
        body, name="outproj_bwd", grid=(r // tm,),
        in_specs=[row(d), row(dc), row(dc), full, pl.BlockSpec(memory_space=pl.ANY)],
        out_specs=[row(2 * dc), full],
        out_shape=[_sds((r, 2 * dc), F32), _sds((2 * dc, d), BF16)],
        scratch_shapes=[pltpu.VMEM((2 * dc, d), F32)],
        compiler_params=_params(("arbitrary",)),
    )(dp1, co, go, w_out, dep)


def _gla_bwd(u, dmi, sta, gup, gb, gn, dep, bsz, nc, kc):
    r = u.shape[0]
    hv = GLA_HEADS * GLA_DV
    hk = GLA_HEADS * GLA_DK
    ns = nc // kc

    def body(qk_ref, v_ref, r_ref, gd_ref, dgo_ref, sta_ref, gup_ref, gb_ref, gn_ref, dep_ref,
             dqk_ref, dv_ref, dr_ref, dgd_ref, dgn_ref, dgb_ref, dgup_ref, dst_ref):
        bi = pl.program_id(0)
        t = pl.program_id(1)

        @pl.when(jnp.logical_and(bi == 0, t == 0))
        def _():
            dgn_ref[...] = jnp.zeros_like(dgn_ref)
            dgb_ref[...] = jnp.zeros_like(dgb_ref)
            dgup_ref[...] = jnp.zeros_like(dgup_ref)

        @pl.when(t == 0)
        def _():
            dst_ref[...] = jnp.zeros_like(dst_ref)

        ps, ri, ci = _gla_prep(qk_ref, gd_ref, gup_ref[...], gb_ref[...], (ns - 1 - t) * kc, kc)
        tril = ri >= ci
        items = [(j, h) for j in reversed(range(kc)) for h in range(GLA_HEADS)]
        ops = [_gla_heads(p, v_ref) for p in ps]
        op = lambda jh: ops[jh[0]][jh[1]]
        st = {jh: sta_ref[jh[0], jh[1]] for jh in items}
        stb = {jh: st[jh].astype(BF16) for jh in items}
        a = {jh: jnp.where(tril, _mm_nt(op(jh)["qe"], op(jh)["ke"]), 0.0).astype(BF16) for jh in items}
        o1 = {jh: _mm(a[jh], op(jh)["v"]) for jh in items}
        o2 = {jh: _mm_nt(op(jh)["qe"], stb[jh]) for jh in items}
        dob = {}
        dgn = jnp.zeros((1, GLA_DV), F32)
        for jh in items:
            rows, vs = ps[jh[0]]["rows"], op(jh)["vs"]
            o = o1[jh] + o2[jh]
            rr = r_ref[rows, vs]
            sr = _sigmoid(rr)
            rs = lax.rsqrt(jnp.mean(o * o, axis=-1, keepdims=True) + LN_EPS)
            y = o * rs
            dgo = dgo_ref[rows, vs]
            don = dgo * (rr * sr)
            dr_ref[rows, vs] = dgo * (y * gn_ref[...]) * (sr * (1.0 + rr * (1.0 - sr)))
            dgn = dgn + _rowsum(don * y)
            dxn = don * gn_ref[...]
            dob[jh] = (rs * (dxn - y * jnp.mean(dxn * y, axis=-1, keepdims=True))).astype(BF16)
        da = {jh: jnp.where(tril, _mm_nt(dob[jh], op(jh)["v"]), 0.0).astype(BF16) for jh in items}
        dv1 = {jh: _mm_tn(a[jh], dob[jh]) for jh in items}
        dqe1 = {jh: _mm(da[jh], op(jh)["ke"]) for jh in items}
        dqe2 = {jh: _mm(dob[jh], stb[jh]) for jh in items}
        dke1 = {jh: _mm_tn(da[jh], op(jh)["qe"]) for jh in items}
        inc = {jh: _mm_tn(dob[jh], op(jh)["qe"]) for jh in items}
        dsts = [dst_ref[h] for h in range(GLA_HEADS)]
        dkd1, dgam1 = {}, {}
        for jh in items:
            j, h = jh
            dst = dsts[h]
            dstb = dst.astype(BF16)
            dv_ref[ps[j]["rows"], op(jh)["vs"]] = dv1[jh] + _mm_nt(op(jh)["kd"], dstb)
            dkd1[jh] = _mm(op(jh)["v"], dstb)
            dgam1[jh] = _rowsum(dst * st[jh])
            dsts[h] = dst * ps[j]["gam"][:, op(jh)["ls"]] + inc[jh]
        for h in range(GLA_HEADS):
            dst_ref[h] = dsts[h]
        upper = (ri <= ci).astype(BF16)
        dbs, dbls = [], []
        for j in range(kc):
            p = ps[j]
            tiles = [[op((j, 2 * hp + h2)) for h2 in range(2)] for hp in range(GLA_HEADS // 2)]
            head = lambda d, hp, h2: d[j, 2 * hp + h2]
            lanes = lambda f: jnp.concatenate([f(hp) for hp in range(GLA_HEADS // 2)], axis=1)
            dqe = lanes(lambda hp: sum(jnp.where(tiles[hp][h2]["m"], head(dqe1, hp, h2) + head(dqe2, hp, h2), 0.0)
                                       for h2 in range(2)))
            dke = lanes(lambda hp: head(dke1, hp, 0) + head(dke1, hp, 1))
            dkd = lanes(lambda hp: sum(jnp.where(tiles[hp][h2]["m"], head(dkd1, hp, h2), 0.0) for h2 in range(2)))
            dgam = lanes(lambda hp: head(dgam1, hp, 0) + head(dgam1, hp, 1))
            dqk_ref[p["rows"], :hk] = dqe * p["eb"] * QK_SCALE
            dqk_ref[p["rows"], hk:] = dke * p["enb"] + dkd * p["ebl"]
            dkdkd = dkd * p["kd"]
            dbs.append(dqe * p["qe"] - dke * p["ke"] - dkdkd)
            dbls.append(_rowsum(dkdkd) + dgam * p["gam"])
        dlgs = _tri_mm_all(upper, dbs)
        dzb = []
        dgb = jnp.zeros((1, hk), F32)
        for j in range(kc):
            p = ps[j]
            dz = jnp.where(p["real"], (dlgs[j] + dbls[j]) * (1.0 / GLA_TAU) * _sigmoid(-p["z"]), 0.0)
            dgb = dgb + _rowsum(dz)
            dzb.append(dz.astype(BF16))
        dgup = sum(_mm_tn(ps[j]["gd"].astype(BF16), dzb[j]) for j in range(kc))
        for j in range(kc):
            dgd_ref[ps[j]["rows"], :] = _mm_nt(dzb[j], gup_ref[...])
        dgb_ref[...] += dgb
        dgup_ref[...] += dgup
        dgn_ref[...] += dgn

    rowblk = lambda col: (lambda b, t: (b * ns + ns - 1 - t, col))
    const = lambda b, t: (0, 0)
    return pl.pallas_call(
        body, name="gla_bwd", grid=(bsz, ns),
        in_specs=[pl.BlockSpec((kc * CHUNK, 2 * hk), rowblk(2)), pl.BlockSpec((kc * CHUNK, hv), rowblk(3)),
                  pl.BlockSpec((kc * CHUNK, hv), rowblk(4)), pl.BlockSpec((kc * CHUNK, LANES), rowblk(20)),
                  pl.BlockSpec((kc * CHUNK, hv), rowblk(1)),
                  pl.BlockSpec((kc, GLA_HEADS, LANES, LANES), lambda b, t: (b * ns + ns - 1 - t, 0, 0, 0)),
                  pl.BlockSpec((LANES, 256), const), pl.BlockSpec((1, 256), const), pl.BlockSpec((1, GLA_DV), const),
                  pl.BlockSpec(memory_space=pl.ANY)],
        out_specs=[pl.BlockSpec((kc * CHUNK, 2 * hk), rowblk(0)), pl.BlockSpec((kc * CHUNK, hv), rowblk(0)),
                   pl.BlockSpec((kc * CHUNK, hv), rowblk(0)), pl.BlockSpec((kc * CHUNK, LANES), rowblk(0)),
                   pl.BlockSpec((1, GLA_DV), const), pl.BlockSpec((1, 256), const), pl.BlockSpec((LANES, 256), const)],
        out_shape=[_sds((r, 2 * hk), F32), _sds((r, hv), F32), _sds((r, hv), F32), _sds((r, LANES), F32),
                   _sds((1, GLA_DV), F32), _sds((1, 256), F32), _sds((LANES, 256), F32)],
        scratch_shapes=[pltpu.VMEM((GLA_HEADS, LANES, LANES), F32)],
        compiler_params=_params(("arbitrary", "arbitrary")),
    )(u, u, u, u, dmi, sta, gup, gb, gn, dep)


def _conv_bwd(u, c, dmi, w32, cg, cbe, tp, tc, dc):
    r = u.shape[0]
    hb = tc // CONV_HALO
    nhalo = r // CONV_HALO

    def dconv(cv, dco, cg_ref, cbe_ref):
        xhat, rstd = _ln(cv)
        cn = xhat * cg_ref[...] + cbe_ref[...]
        sg = _sigmoid(cn)
        dcn = dco * (sg * (1.0 + cn * (1.0 - sg)))
        return _ln_bwd(dcn * cg_ref[...], xhat, rstd), dcn, xhat

    def body(a_ref, g_ref, ah_ref, gh_ref, c_ref, dco_ref, ch_ref, dcoh_ref, w_ref, cg_ref, cbe_ref,
             du_ref, dw_ref, dcb_ref, dcg_ref, dcbe_ref, hs_ref, dcs_ref):
        t = pl.program_id(0)

        @pl.when(t == 0)
        def _():
            dw_ref[...] = jnp.zeros_like(dw_ref)
            dcb_ref[...] = jnp.zeros_like(dcb_ref)
            dcg_ref[...] = jnp.zeros_like(dcg_ref)
            dcbe_ref[...] = jnp.zeros_like(dcbe_ref)

        first = lax.rem(t * tc, tp) == 0
        last = lax.rem((t + 1) * tc, tp) == 0
        hh = ah_ref[...] * _sigmoid(gh_ref[...])
        hs_ref[0:CONV_HALO, :] = jnp.where(first, 0.0, hh)
        hs_ref[CONV_HALO:CONV_HALO + tc, :] = a_ref[...] * _sigmoid(g_ref[...])
        dch, _, _ = dconv(ch_ref[...], dcoh_ref[...], cg_ref, cbe_ref)
        dcs_ref[tc:tc + CONV_HALO, :] = jnp.where(last, 0.0, dch)

        def sub1(k, carry):
            r0 = pl.multiple_of(k * CONV_SUB, CONV_SUB)
            dcv, dcn, xhat = dconv(c_ref[pl.ds(r0, CONV_SUB), :], dco_ref[pl.ds(r0, CONV_SUB), :], cg_ref, cbe_ref)
            dcs_ref[pl.ds(r0, CONV_SUB), :] = dcv
            dcb_ref[...] += _rowsum(dcv)
            dcg_ref[...] += _rowsum(dcn * xhat)
            dcbe_ref[...] += _rowsum(dcn)
            return carry

        lax.fori_loop(0, tc // CONV_SUB, sub1, 0)

        def sub2(k, carry):
            r0 = pl.multiple_of(k * CONV_SUB, CONV_SUB)
            dwin = dcs_ref[pl.ds(r0, CONV_WIN), :]
            dh = _conv_taps(dwin, lambda o: w_ref[CONV_WIDTH - 1 - o:CONV_WIDTH - o, :], 0)
            av = a_ref[pl.ds(r0, CONV_SUB), :]
            sg = _sigmoid(g_ref[pl.ds(r0, CONV_SUB), :])
            du_ref[pl.ds(r0, CONV_SUB), 0:dc] = dh * sg
            du_ref[pl.ds(r0, CONV_SUB), dc:2 * dc] = dh * av * sg * (1.0 - sg)
            hwin = hs_ref[pl.ds(r0, CONV_WIN), :]
            dcv = dwin[0:CONV_SUB, :]
            for rho in range(8):
                offs = [o for o in range(2, 2 + CONV_WIDTH) if o % 8 == rho]
                rolled = hwin if rho == 0 else pltpu.roll(hwin, CONV_WIN - rho, 0)
                for o in offs:
                    m8 = o - rho
                    dw_ref[o - 2:o - 1, :] += _rowsum(dcv * rolled[m8:m8 + CONV_SUB, :])
            return carry

        lax.fori_loop(0, tc // CONV_SUB, sub2, 0)

    vec = pl.BlockSpec((1, dc), lambda t: (0, 0))
    prev = lambda col: (lambda t: (jnp.maximum(t * hb - 1, 0), col))
    nxt = lambda col: (lambda t: (jnp.minimum((t + 1) * hb, nhalo - 1), col))
    return pl.pallas_call(
        body, name="conv_bwd", grid=(r // tc,),
        in_specs=[pl.BlockSpec((tc, dc), lambda t: (t, 0)), pl.BlockSpec((tc, dc), lambda t: (t, 1)),
                  pl.BlockSpec((CONV_HALO, dc), prev(0)), pl.BlockSpec((CONV_HALO, dc), prev(1)),
                  pl.BlockSpec((tc, dc), lambda t: (t, 0)), pl.BlockSpec((tc, dc), lambda t: (t, 0)),
                  pl.BlockSpec((CONV_HALO, dc), nxt(0)), pl.BlockSpec((CONV_HALO, dc), nxt(0)),
                  pl.BlockSpec((32, dc), lambda t: (0, 0)), vec, vec],
        out_specs=[pl.BlockSpec((tc, 2 * dc), lambda t: (t, 0)), pl.BlockSpec((32, dc), lambda t: (0, 0)), vec, vec, vec],
        out_shape=[_sds((r, 2 * dc), F32), _sds((32, dc), F32), _sds((1, dc), F32), _sds((1, dc), F32), _sds((1, dc), F32)],
        scratch_shapes=[pltpu.VMEM((CONV_HALO + tc, dc), F32), pltpu.VMEM((tc + CONV_HALO, dc), F32)],
        compiler_params=_params(("arbitrary",)),
    )(u, u, u, u, c, dmi, c, dmi, w32, cg, cbe)


def _inproj_bwd(dp1, dus, xcat, g_in, w_in, dep, tp, tm):
    r, d = dp1.shape
    widths = [x.shape[1] for x in dus]
    offs = [sum(widths[:k]) for k in range(len(widths))]
    n = w_in.shape[1]
    nd = len(dus)
    tps = tp // tm

    def body(*refs):
        dp_ref = refs[0]
        du_refs = refs[1:1 + nd]
        x_ref, g_ref, w_ref, _, dx_ref, dmeta_ref, dg_ref, db_ref = refs[1 + nd:]
        i = pl.program_id(0)

        @pl.when(i == 0)
        def _():
            dmeta_ref[...] = jnp.zeros_like(dmeta_ref)
            dg_ref[...] = jnp.zeros_like(dg_ref)
            db_ref[...] = jnp.zeros_like(db_ref)

        ds0 = ALPHA * dp_ref[...]
        for k in range(nd):
            ds0 = ds0 + _mm_nt(du_refs[k][...].astype(BF16), w_ref[:, offs[k]:offs[k] + widths[k]])
        real = _row_in_seq(i, tm, tp) >= PAD_FRONT
        ds0 = jnp.where(real, ds0, 0.0)
        xhat, rstd = _ln(x_ref[...])
        dg_ref[...] += _rowsum(ds0 * xhat)
        db_ref[...] += _rowsum(ds0)
        dx = jnp.where(real, _ln_bwd(ds0 * g_ref[...], xhat, rstd), 0.0)
        dx_ref[...] = dx

        @pl.when(lax.rem(i, tps) == 0)
        def _():
            dmeta_ref[...] += dx[PAD_FRONT:X_OFF, :]

    row = lambda w: pl.BlockSpec((tm, w), lambda i: (i, 0))
    vec = pl.BlockSpec((1, d), lambda i: (0, 0))
    return pl.pallas_call(
        body, name="inproj_bwd", grid=(r // tm,),
        in_specs=[row(d)] + [row(w) for w in widths] + [row(d), vec, pl.BlockSpec((d, n), lambda i: (0, 0)),
                                                        pl.BlockSpec(memory_space=pl.ANY)],
        out_specs=[row(d), pl.BlockSpec((N_META, d), lambda i: (0, 0)), vec, vec],
        out_shape=[_sds((r, d), F32), _sds((N_META, d), F32), _sds((1, d), F32), _sds((1, d), F32)],
        compiler_params=_params(("arbitrary",)),
    )(dp1, *dus, xcat, g_in, w_in, dep)


def _inproj_bwd_w(s0, dus, tm):
    r, d = s0.shape
    widths = [x.shape[1] for x in dus]
    nd = len(dus)

    def body(*refs):
        s_ref = refs[0]
        du_refs = refs[1:1 + nd]
        dw_refs = refs[1 + nd:]
        i = pl.program_id(0)

        @pl.when(i == 0)
        def _():
            for k in range(nd):
                dw_refs[k][...] = jnp.zeros_like(dw_refs[k])

        sb = s_ref[...].astype(BF16)
        for k in range(nd):
            dw_refs[k][...] += _mm_tn(sb, du_refs[k][...].astype(BF16))

    row = lambda w: pl.BlockSpec((tm, w), lambda i: (i, 0))
    return pl.pallas_call(
        body, name="inproj_bwd_w", grid=(r // tm,),
        in_specs=[row(d)] + [row(w) for w in widths],
        out_specs=[pl.BlockSpec((d, w), lambda i: (0, 0)) for w in widths],
        out_shape=[_sds((d, w), F32) for w in widths],
        compiler_params=_params(("arbitrary",)),
    )(s0, *dus)


def _local_step(x, tgt, meta, ln_in_g, ln_in_b, w_in, conv_w, conv_b, conv_ln_g, conv_ln_b, gate_up, gate_bias,
                gla_norm_g, late_weights, ln1_g, ln1_b, ln2_g, ln2_b, push):
    bsz, seq, d = x.shape
    tp = X_OFF + seq
    assert tp % CHUNK == 0
    nc = tp // CHUNK
    r = bsz * tp
    dc = conv_b.shape[1]
    tm = _pick_tile(tp, (352, 128, 64))
    tc = _pick_tile(tp, (704, 128, 64))

    xcat = jnp.concatenate([jnp.zeros((bsz, PAD_FRONT, d), F32), jnp.broadcast_to(meta[None], (bsz, N_META, d)), x],
                           axis=1).reshape(r, d)
    tgt_p = jnp.pad(tgt, ((0, 0), (X_OFF, 0), (0, 0))).reshape(r, d)
    w32 = jnp.pad(conv_w, ((0, 32 - CONV_WIDTH), (0, 0)))
    gup = jnp.pad(gate_up, ((0, LANES - GLA_RANK), (0, 0))).astype(BF16)

    s0, u = _inproj_fwd(xcat, ln_in_g, ln_in_b, w_in, tp, tm)
    c, co = _conv_fwd(u, w32, conv_b, conv_ln_g, conv_ln_b, tp, tc, dc)
    kc = _pick_tile(nc, (3, 2, 1))
    go, sta = _gla_fwd(u, gup, gate_bias, gla_norm_g, bsz, nc, kc)
    w_out, w1g, w2 = late_weights(go)
    nh = w1g.shape[0]
    tmm = _pick_tile(tp, (704, 128, 64))
    ns = 2
    p1, s1, s1b = _outproj_fwd(s0, co, go, w_out, ln1_g, ln1_b, tm)
    hm, dp2, dpb, loss, dg2, db2 = _mlp_fwd(s1, s1b, w1g, w2, ln2_g, ln2_b, tgt_p, tp, tmm, ns)

    dh, dp1, dg1, db1 = _mlp_bwd_act(dp2, dpb, hm, w1g, w2, p1, ln1_g, tmm, ns)
    dw1, dw2 = _mlp_bwd_w(s1b, hm, dh, dpb, nh, tmm, ns)
    tok = push("ff", (dw1, dw2))
    dmi, dwo = _outproj_bwd(dp1, co, go, w_out, tok, tm)
    tok = push("out", (dwo,))
    dqk, dv, dr, dgd, dgn, dgb, dgup = _gla_bwd(u, dmi, sta, gup, gate_bias, gla_norm_g, tok, bsz, nc, kc)
    dcv, dcw, dcb, dcg, dcbe = _conv_bwd(u, c, dmi, w32, conv_ln_g, conv_ln_b, tp, tc, dc)
    dus = [dcv, dqk, dv, dr, dgd]
    dwi = _inproj_bwd_w(s0, dus, tm)
    tok = push("in", (jnp.concatenate(dwi, axis=1),))
    dxcat, dmeta, dgi, dbi = _inproj_bwd(dp1, dus, xcat, ln_in_g, w_in, tok, tp, tm)

    grad_x = dxcat.reshape(bsz, tp, d)[:, X_OFF:, :]
    return dict(loss=loss[0, 0], grad_x=grad_x, meta_tokens=dmeta, ln_in_g=dgi, ln_in_b=dbi,
                conv_w=dcw[:CONV_WIDTH], conv_b=dcb, conv_ln_g=dcg, conv_ln_b=dcbe,
                gate_up=dgup[:GLA_RANK], gate_bias=dgb, gla_norm_g=dgn, ln1_g=dg1, ln1_b=db1, ln2_g=dg2, ln2_b=db2)


def _exchange(arrays, scatter, name):
    na = len(arrays)
    npeer = N_DEV - 1

    def body(*refs):
        srcs = refs[:na]
        outs = refs[na:2 * na]
        send_sems, recv_sems, local_sems = refs[2 * na:]
        xi, yi, ci = (lax.axis_index(a) for a in MESH_AXES)
        me = 4 * xi + 2 * yi + ci
        copies = []
        for a in range(na):
            own = srcs[a].at[me] if scatter[a] else srcs[a]
            cp = pltpu.make_async_copy(own, outs[a].at[me], local_sems.at[a])
            cp.start()
            copies.append(cp)
        remote = []
        for k in range(1, N_DEV):
            px, py, pc = xi ^ (k >> 2), yi ^ ((k >> 1) & 1), ci ^ (k & 1)
            peer = 4 * px + 2 * py + pc
            for a in range(na):
                src = srcs[a].at[peer] if scatter[a] else srcs[a]
                cp = pltpu.make_async_remote_copy(
                    src_ref=src, dst_ref=outs[a].at[me],
                    send_sem=send_sems.at[a * npeer + k - 1], recv_sem=recv_sems.at[a * npeer + k - 1],
                    device_id=(px, py, pc), device_id_type=pl.DeviceIdType.MESH)
                cp.start()
                remote.append(cp)
        for cp in remote:
            cp.wait()
        for cp in copies:
            cp.wait()

    out_shape = [_sds(a.shape if scatter[i] else (N_DEV,) + a.shape, a.dtype) for i, a in enumerate(arrays)]
    anyspec = pl.BlockSpec(memory_space=pl.ANY)
    return pl.pallas_call(
        body, name=name,
        in_specs=[anyspec] * na, out_specs=[anyspec] * na, out_shape=out_shape,
        scratch_shapes=[pltpu.SemaphoreType.DMA((na * npeer,)), pltpu.SemaphoreType.DMA((na * npeer,)),
                        pltpu.SemaphoreType.DMA((na,))],
    )(*arrays)


def _peers(xi, yi, ci):
    for k in range(1, N_DEV):
        px, py, pc = xi ^ (k >> 2), yi ^ ((k >> 1) & 1), ci ^ (k & 1)
        yield (px, py, pc), 4 * px + 2 * py + pc


def _sc_exchange(arrays, scatter, name, collective_id, after=None):
    na = len(arrays)
    npeer = N_DEV - 1
    ndep = 0 if after is None else 1

    def body(*refs):
        srcs = refs[:na]
        outs = refs[na + ndep:2 * na + ndep]
        send_sems, recv_sems, own_sems = refs[2 * na + ndep:]
        xi, yi, ci = (lax.axis_index(a) for a in MESH_AXES)
        me = 4 * xi + 2 * yi + ci
        barrier = pltpu.get_barrier_semaphore()
        for pos, _ in _peers(xi, yi, ci):
            pl.semaphore_signal(barrier, inc=1, device_id=pos, device_id_type=pl.DeviceIdType.MESH)
        pl.semaphore_wait(barrier, npeer)
        own = [pltpu.make_async_copy(srcs[a].at[me] if scatter[a] else srcs[a], outs[a].at[me], own_sems.at[a])
               for a in range(na)]
        for cp in own:
            cp.start()
        remote = []
        for a in range(na):
            for k, (pos, peer) in enumerate(_peers(xi, yi, ci)):
                cp = pltpu.make_async_remote_copy(
                    src_ref=srcs[a].at[peer] if scatter[a] else srcs[a], dst_ref=outs[a].at[me],
                    send_sem=send_sems.at[a * npeer + k], recv_sem=recv_sems.at[a * npeer + k],
                    device_id=pos, device_id_type=pl.DeviceIdType.MESH)
                cp.start()
                remote.append(cp)
        for cp in own:
            cp.wait()
        for cp in remote:
            cp.wait()

    out_type = [_sds(a.shape if scatter[i] else (N_DEV,) + a.shape, a.dtype) for i, a in enumerate(arrays)]
    sent = sum(a.size * a.dtype.itemsize // (N_DEV if scatter[i] else 1) for i, a in enumerate(arrays))
    return pl.kernel(
        body, out_type=out_type, mesh=plsc.ScalarSubcoreMesh(axis_name="seq", num_cores=1), name=name,
        scratch_types=[pltpu.SemaphoreType.DMA((na * npeer,)), pltpu.SemaphoreType.DMA((na * npeer,)),
                       pltpu.SemaphoreType.DMA((na,))],
        compiler_params=pltpu.CompilerParams(collective_id=collective_id),
        cost_estimate=pl.CostEstimate(flops=0, transcendentals=0, bytes_accessed=2 * N_DEV * sent,
                                      remote_bytes_transferred=npeer * sent),
    )(*arrays, *([] if after is None else [after]))


def _push_start(arrays, scatter, name, dep=None):
    na = len(arrays)
    shapes = [a.shape if scatter[i] else (N_DEV,) + a.shape for i, a in enumerate(arrays)]
    hbm = pl.BlockSpec(memory_space=pltpu.HBM)
    sem = pl.BlockSpec(memory_space=pltpu.SEMAPHORE)
    ndep = 0 if dep is None else 1

    def body(*refs):
        srcs = refs[:na]
        lands = refs[na:2 * na]
        send_sems, recv_sems = refs[2 * na + ndep:2 * na + ndep + 2]
        own_sems = refs[4 * na + ndep + 2]
        xi, yi, ci = (lax.axis_index(a) for a in MESH_AXES)
        me = 4 * xi + 2 * yi + ci
        own = [pltpu.make_async_copy(srcs[a].at[me] if scatter[a] else srcs[a], lands[a].at[me], own_sems.at[a])
               for a in range(na)]
        for cp in own:
            cp.start()
        for cp in own:
            cp.wait()
        for a in range(na):
            for pos, peer in _peers(xi, yi, ci):
                pltpu.make_async_remote_copy(
                    src_ref=srcs[a].at[peer] if scatter[a] else srcs[a], dst_ref=lands[a].at[me],
                    send_sem=send_sems.at[a], recv_sem=recv_sems.at[a],
                    device_id=pos, device_id_type=pl.DeviceIdType.MESH).start()

    ins = [pltpu.with_memory_space_constraint(a, pltpu.HBM) for a in arrays]
    ins += [pltpu.with_memory_space_constraint(lax.empty(s, a.dtype), pltpu.HBM) for s, a in zip(shapes, arrays)]
    res = pl.pallas_call(
        body, name=name,
        in_specs=[hbm] * (2 * na) + [pl.BlockSpec(memory_space=pl.ANY)] * ndep,
        out_specs=[sem, sem] + [hbm] * (2 * na),
        out_shape=[pltpu.SemaphoreType.DMA((na,)), pltpu.SemaphoreType.DMA((na,))]
                  + [pltpu.HBM(a.shape, a.dtype) for a in arrays] + [pltpu.HBM(s, a.dtype) for s, a in zip(shapes, arrays)],
        input_output_aliases={i: 2 + i for i in range(2 * na)},
        scratch_shapes=[pltpu.SemaphoreType.DMA((na,))],
        compiler_params=pltpu.CompilerParams(has_side_effects=pltpu.SideEffectType.DATAFLOW_SIDE_EFFECTING),
    )(*ins, *([] if dep is None else [dep]))
    return (res[0], res[1], list(res[2:2 + na]), list(res[2 + na:2 + 2 * na])), res[2]


def _push_wait(handle, after, name):
    send_sems, recv_sems, srcs, lands = handle
    na = len(srcs)
    hbm = pl.BlockSpec(memory_space=pltpu.HBM)
    sem = pl.BlockSpec(memory_space=pltpu.SEMAPHORE)

    def body(*refs):
        land_refs = refs[na:2 * na]
        send_ref, recv_ref = refs[2 * na:2 * na + 2]
        me = tuple(lax.axis_index(a) for a in MESH_AXES)
        for a in range(na):
            seven = land_refs[a].at[pl.ds(0, N_DEV - 1)]
            cp = pltpu.make_async_remote_copy(src_ref=seven, dst_ref=seven, send_sem=send_ref.at[a], recv_sem=recv_ref.at[a],
                                              device_id=me, device_id_type=pl.DeviceIdType.MESH)
            cp.wait_send()
            cp.wait_recv()

    res = pl.pallas_call(
        body, name=name,
        in_specs=[hbm] * (2 * na) + [sem, sem, pl.BlockSpec(memory_space=pl.ANY)],
        out_specs=[hbm] * (2 * na),
        out_shape=[pltpu.HBM(a.shape, a.dtype) for a in srcs] + [pltpu.HBM(a.shape, a.dtype) for a in lands],
        input_output_aliases={i: i for i in range(2 * na)},
        compiler_params=pltpu.CompilerParams(has_side_effects=pltpu.SideEffectType.DATAFLOW_SIDE_EFFECTING),
    )(*srcs, *lands, send_sems, recv_sems, after)
    return list(res[na:])


def _adamw(w, g, m, v):
    m = ADAM_B1 * m + (1.0 - ADAM_B1) * g
    v = ADAM_B2 * v + (1.0 - ADAM_B2) * jnp.square(g)
    m_hat = m / (1.0 - ADAM_B1 ** ADAM_STEP)
    v_hat = v / (1.0 - ADAM_B2 ** ADAM_STEP)
    delta = -ADAM_LR * (m_hat / (jnp.sqrt(v_hat) + ADAM_EPS) + ADAM_WD * w)
    return delta, m, v


def _sum_devices(ref):
    g = ref[0].astype(F32)
    for k in range(1, N_DEV):
        g = g + ref[k].astype(F32)
    return g


def _update_big(parts, w, m, v, name):
    rows, cols = w.shape
    tr = _pick_tile(rows, (128, 64, 16))

    def body(p_ref, w_ref, m_ref, v_ref, g_ref, d_ref, nm_ref, nv_ref):
        g = _sum_devices(p_ref)
        g_ref[...] = g
        d_ref[...], nm_ref[...], nv_ref[...] = _adamw(w_ref[...], g, m_ref[...], v_ref[...])

    blk = pl.BlockSpec((tr, cols), lambda i: (i, 0))
    return pl.pallas_call(
        body, name=name, grid=(rows // tr,),
        in_specs=[pl.BlockSpec((N_DEV, tr, cols), lambda i: (0, i, 0)), blk, blk, blk],
        out_specs=[blk] * 4, out_shape=[_sds((rows, cols), F32)] * 4,
        compiler_params=_params(("parallel",)),
    )(parts, w, m, v)


_VEC_ORDER = ("ln_in_g", "ln_in_b", "conv_b", "conv_ln_g", "conv_ln_b", "gate_bias", "gla_norm_g",
              "ln1_g", "ln1_b", "ln2_g", "ln2_b")
_SHARDED_SMALL = (("meta_tokens", 0, N_META, LANES), ("conv_w", N_META, CONV_WIDTH, None), ("gate_up", N_META + 32, GLA_RANK, None))


def _update_small(parts_sh, parts_vec, wmv):
    names = [s[0] for s in _SHARDED_SMALL] + list(_VEC_ORDER)
    flat = [a for nme in names for a in wmv[nme]]
    nv = len(_VEC_ORDER)

    def body(*refs):
        sh_ref, vec_ref = refs[0], refs[1]
        ins = refs[2:2 + len(flat)]
        outs = refs[2 + len(flat):2 + len(flat) + 4 * len(names)]
        loss_ref = refs[2 + len(flat) + 4 * len(names)]
        gsh_ref, gvec_ref = refs[-2:]
        gsh_ref[...] = _sum_devices(sh_ref)
        gvec_ref[...] = _sum_devices(vec_ref)
        loss_ref[...] = gvec_ref[nv:nv + 1, :]
        for idx, nme in enumerate(names):
            w_ref, m_ref, v_ref = ins[3 * idx:3 * idx + 3]
            rows, cols = w_ref.shape
            if idx < len(_SHARDED_SMALL):
                r0 = _SHARDED_SMALL[idx][1]
                g = gsh_ref[r0:r0 + rows, 0:cols]
            else:
                j = idx - len(_SHARDED_SMALL)
                g = gvec_ref[j:j + 1, 0:cols]
            o = outs[4 * idx:4 * idx + 4]
            o[0][...] = g
            o[1][...], o[2][...], o[3][...] = _adamw(w_ref[...], g, m_ref[...], v_ref[...])

    out_shape = [_sds(wmv[nme][0].shape, F32) for nme in names for _ in range(4)] + [_sds((1, parts_vec.shape[2]), F32)]
    vmem = pl.BlockSpec(memory_space=pltpu.VMEM)
    res = pl.pallas_call(
        body, name="update_small", out_shape=out_shape,
        in_specs=[vmem] * (2 + len(flat)), out_specs=[vmem] * len(out_shape),
        scratch_shapes=[pltpu.VMEM(parts_sh.shape[1:], F32), pltpu.VMEM(parts_vec.shape[1:], F32)],
    )(parts_sh, parts_vec, *flat)
    return {nme: res[4 * i:4 * i + 4] for i, nme in enumerate(names)}, res[-1][0, 0]


_WEIGHTS = ("meta_tokens", "ln_in_g", "ln_in_b", "w_in", "conv_w", "conv_b", "conv_ln_g", "conv_ln_b", "gate_up",
            "gate_bias", "gla_norm_g", "w_out", "ln1_g", "ln1_b", "w_ff1", "w_ff2", "ln2_g", "ln2_b")


def kernel(x, meta_tokens, ln_in_g, ln_in_b, w_in, conv_w, conv_b, conv_ln_g, conv_ln_b, gate_up, gate_bias, gla_norm_g, w_out, ln1_g, ln1_b, w_ff1, w_ff2, ln2_g, ln2_b, loss_target, m_meta_tokens, m_ln_in_g, m_ln_in_b, m_w_in, m_conv_w, m_conv_b, m_conv_ln_g, m_conv_ln_b, m_gate_up, m_gate_bias, m_gla_norm_g, m_w_out, m_ln1_g, m_ln1_b, m_w_ff1, m_w_ff2, m_ln2_g, m_ln2_b, v_meta_tokens, v_ln_in_g, v_ln_in_b, v_w_in, v_conv_w, v_conv_b, v_conv_ln_g, v_conv_ln_b, v_gate_up, v_gate_bias, v_gla_norm_g, v_w_out, v_ln1_g, v_ln1_b, v_w_ff1, v_w_ff2, v_ln2_g, v_ln2_b):
    w = dict(meta_tokens=meta_tokens, ln_in_g=ln_in_g, ln_in_b=ln_in_b, w_in=w_in, conv_w=conv_w, conv_b=conv_b,
             conv_ln_g=conv_ln_g, conv_ln_b=conv_ln_b, gate_up=gate_up, gate_bias=gate_bias, gla_norm_g=gla_norm_g,
             w_out=w_out, ln1_g=ln1_g, ln1_b=ln1_b, w_ff1=w_ff1, w_ff2=w_ff2, ln2_g=ln2_g, ln2_b=ln2_b)
    mom = dict(meta_tokens=m_meta_tokens, ln_in_g=m_ln_in_g, ln_in_b=m_ln_in_b, w_in=m_w_in, conv_w=m_conv_w,
               conv_b=m_conv_b, conv_ln_g=m_conv_ln_g, conv_ln_b=m_conv_ln_b, gate_up=m_gate_up, gate_bias=m_gate_bias,
               gla_norm_g=m_gla_norm_g, w_out=m_w_out, ln1_g=m_ln1_g, ln1_b=m_ln1_b, w_ff1=m_w_ff1, w_ff2=m_w_ff2,
               ln2_g=m_ln2_g, ln2_b=m_ln2_b)
    var = dict(meta_tokens=v_meta_tokens, ln_in_g=v_ln_in_g, ln_in_b=v_ln_in_b, w_in=v_w_in, conv_w=v_conv_w,
               conv_b=v_conv_b, conv_ln_g=v_conv_ln_g, conv_ln_b=v_conv_ln_b, gate_up=v_gate_up, gate_bias=v_gate_bias,
               gla_norm_g=v_gla_norm_g, w_out=v_w_out, ln1_g=v_ln1_g, ln1_b=v_ln1_b, w_ff1=v_w_ff1, w_ff2=v_w_ff2,
               ln2_g=v_ln2_g, ln2_b=v_ln2_b)
    shapes = {k: a.shape for k, a in w.items()}

    def two_d(a):
        return a.reshape(1, -1) if a.ndim == 1 else a.reshape(a.shape[-2:])

    w2d = {k: two_d(a) for k, a in w.items()}
    m2d = {k: two_d(a) for k, a in mom.items()}
    v2d = {k: two_d(a) for k, a in var.items()}
    d = x.shape[-1]
    d_in = w2d["w_in"].shape[1] * N_DEV
    d_in_p = -(-d_in // LANES) * LANES

    g_in, g_meta, g_conv, g_gup = _sc_exchange(
        [w2d["w_in"].astype(BF16), w2d["meta_tokens"], w2d["conv_w"], w2d["gate_up"]], [False] * 4, "gather_first", 0)
    g_out, g_ff1, g_ff2 = _sc_exchange(
        [w2d["w_out"].astype(BF16), w2d["w_ff1"].astype(BF16), w2d["w_ff2"].astype(BF16)], [False] * 3, "gather_late", 1,
        after=g_gup)
    w_in_full = jnp.pad(g_in.transpose(1, 0, 2).reshape(d, d_in), ((0, 0), (0, d_in_p - d_in)))
    meta_full = g_meta.transpose(1, 0, 2).reshape(N_META, d)
    conv_w_full = g_conv.transpose(1, 0, 2).reshape(CONV_WIDTH, -1)
    gate_up_full = g_gup.transpose(1, 0, 2).reshape(GLA_RANK, -1)

    def late_weights(after):
        return g_out.reshape(-1, d), g_ff1, g_ff2.reshape(-1, d)

    pushed = {}

    def push(tag, grads):
        if tag == "ff":
            pushed["ff1"], pushed["ff2"] = _sc_exchange(list(grads), [True, True], "scatter_ff", 2)
        elif tag == "out":
            pushed["p_out"] = grads[0].reshape(N_DEV, -1, d)
        else:
            p_in = grads[0][:, :d_in].reshape(d, N_DEV, d_in // N_DEV).transpose(1, 0, 2).astype(BF16)
            pushed["in"], pushed["out"] = _sc_exchange([p_in, pushed["p_out"]], [True, True], "scatter_rest", 3,
                                                       after=pushed["ff1"])
        return grads[0]

    res = _local_step(x, loss_target, meta_full, w2d["ln_in_g"], w2d["ln_in_b"], w_in_full, conv_w_full, w2d["conv_b"],
                      w2d["conv_ln_g"], w2d["conv_ln_b"], gate_up_full, w2d["gate_bias"], w2d["gla_norm_g"], late_weights,
                      w2d["ln1_g"], w2d["ln1_b"], w2d["ln2_g"], w2d["ln2_b"], push)

    dc = res["conv_w"].shape[1]
    hk = res["gate_up"].shape[1]
    sh_meta = res["meta_tokens"].reshape(N_META, N_DEV, LANES).transpose(1, 0, 2)
    sh_conv = jnp.pad(res["conv_w"].reshape(CONV_WIDTH, N_DEV, dc // N_DEV).transpose(1, 0, 2),
                      ((0, 0), (0, 32 - CONV_WIDTH), (0, LANES - dc // N_DEV)))
    sh_gup = jnp.pad(res["gate_up"].reshape(GLA_RANK, N_DEV, hk // N_DEV).transpose(1, 0, 2),
                     ((0, 0), (0, 0), (0, LANES - hk // N_DEV)))
    p_sh = jnp.concatenate([sh_meta, sh_conv, sh_gup], axis=1)
    p_vec = jnp.concatenate([jnp.pad(res[k], ((0, 0), (0, d - res[k].shape[1]))) for k in _VEC_ORDER]
                            + [jnp.full((1, d), res["loss"], F32), jnp.zeros((15 - len(_VEC_ORDER), d), F32)], axis=0)

    r_sh, r_vec = _exchange([p_sh, p_vec], [True, False], "scatter_small")
    r_ff1, r_ff2, r_out, r_in = pushed["ff1"], pushed["ff2"], pushed["out"], pushed["in"]

    upd = {}
    upd["w_in"] = _update_big(r_in, w2d["w_in"], m2d["w_in"], v2d["w_in"], "update_w_in")
    upd["w_out"] = _update_big(r_out, w2d["w_out"], m2d["w_out"], v2d["w_out"], "update_w_out")
    upd["w_ff1"] = _update_big(r_ff1, w2d["w_ff1"], m2d["w_ff1"], v2d["w_ff1"], "update_w_ff1")
    upd["w_ff2"] = _update_big(r_ff2, w2d["w_ff2"], m2d["w_ff2"], v2d["w_ff2"], "update_w_ff2")
    small = [s[0] for s in _SHARDED_SMALL] + list(_VEC_ORDER)
    upd_small, loss = _update_small(r_sh, r_vec, {k: (w2d[k], m2d[k], v2d[k]) for k in small})
    upd.update(upd_small)

    outs = [loss, res["grad_x"]]
    for j in range(4):
        outs += [upd[k][j].reshape(shapes[k]) for k in _WEIGHTS]
    return tuple(outs)
```

```python
import functools

import jax
import jax.numpy as jnp
from jax import lax
from jax.experimental import pallas as pl
from jax.experimental.pallas import tpu as pltpu
from jax.experimental.pallas import tpu_sc as plsc

F32 = jnp.float32
BF16 = jnp.bfloat16

N_META = 16
CHUNK = 64
PAD_FRONT = (-N_META) % CHUNK
X_OFF = PAD_FRONT + N_META
CONV_WIDTH = 31
CONV_HALO = 32
CONV_SUB = 64
CONV_WIN = CONV_SUB + CONV_HALO
GLA_HEADS = 4
GLA_DK = 64
GLA_DV = 128
GLA_RANK = 16
GLA_TAU = 16.0
QK_SCALE = GLA_DK ** -0.5
LN_EPS = 1e-5
ALPHA = 2.0 ** 0.25
LANES = 128
N_DEV = 8
ADAM_LR = 0.001
ADAM_B1 = 0.9
ADAM_B2 = 0.999
ADAM_EPS = 1e-08
ADAM_WD = 0.01
ADAM_STEP = 10
VMEM_LIMIT = 56 * 1024 * 1024
MESH_AXES = ("x", "y", "c")


def _sds(shape, dtype):
    return jax.ShapeDtypeStruct(shape, dtype)


def _mm(a, b):
    return jnp.dot(a, b, preferred_element_type=F32)


def _mm_nt(a, b):
    return lax.dot_general(a, b, (((1,), (1,)), ((), ())), preferred_element_type=F32)


def _mm_tn(a, b):
    return lax.dot_general(a, b, (((0,), (0,)), ((), ())), preferred_element_type=F32)


def _sigmoid(x):
    return 1.0 / (1.0 + jnp.exp(-x))


def _log_sigmoid(z):
    return jnp.minimum(z, 0.0) - jnp.log(1.0 + jnp.exp(-jnp.abs(z)))


def _ln(x):
    mu = jnp.mean(x, axis=-1, keepdims=True)
    xc = x - mu
    var = jnp.mean(xc * xc, axis=-1, keepdims=True)
    rstd = lax.rsqrt(var + LN_EPS)
    return xc * rstd, rstd


def _ln_bwd(dyg, xhat, rstd):
    m1 = jnp.mean(dyg, axis=-1, keepdims=True)
    m2 = jnp.mean(dyg * xhat, axis=-1, keepdims=True)
    return rstd * (dyg - m1 - xhat * m2)


def _rowsum(x):
    return jnp.sum(x, axis=0, keepdims=True)


def _row_in_seq(i, tm, tp):
    base = lax.rem(i * tm, tp)
    return base + lax.broadcasted_iota(jnp.int32, (tm, 1), 0)


def _split3(x):
    hi = x.astype(BF16)
    r1 = x - hi.astype(F32)
    mid = r1.astype(BF16)
    lo = (r1 - mid.astype(F32)).astype(BF16)
    return hi, mid, lo


def _tri_mm(tri, x):
    hi, mid, lo = _split3(x)
    return _mm(tri, hi) + _mm(tri, mid) + _mm(tri, lo)


def _params(sem):
    return pltpu.CompilerParams(dimension_semantics=sem, vmem_limit_bytes=VMEM_LIMIT)


def _pick_tile(n, prefs):
    for t in prefs:
        if n % t == 0:
            return t
    raise ValueError(f"no tile for {n}")


def _inproj_fwd(xcat, g, b, w_in, tp, tm):
    r, d = xcat.shape
    n = w_in.shape[1]

    def body(x_ref, g_ref, b_ref, w_ref, s0_ref, u_ref):
        i = pl.program_id(0)
        xhat, _ = _ln(x_ref[...])
        real = _row_in_seq(i, tm, tp) >= PAD_FRONT
        s = jnp.where(real, xhat * g_ref[...] + b_ref[...], 0.0)
        s0_ref[...] = s
        u_ref[...] = _mm(s.astype(BF16), w_ref[...])

    return pl.pallas_call(
        body, name="inproj_fwd", grid=(r // tm,),
        in_specs=[pl.BlockSpec((tm, d), lambda i: (i, 0)), pl.BlockSpec((1, d), lambda i: (0, 0)),
                  pl.BlockSpec((1, d), lambda i: (0, 0)), pl.BlockSpec((d, n), lambda i: (0, 0))],
        out_specs=[pl.BlockSpec((tm, d), lambda i: (i, 0)), pl.BlockSpec((tm, n), lambda i: (i, 0))],
        out_shape=[_sds((r, d), F32), _sds((r, n), F32)],
        compiler_params=_params(("parallel",)),
    )(xcat, g, b, w_in)


def _conv_taps(win, coef, lo):
    acc = None
    for rho in range(8):
        offs = [o for o in range(lo, lo + CONV_WIDTH) if o % 8 == rho]
        if not offs:
            continue
        rolled = win if rho == 0 else pltpu.roll(win, CONV_WIN - rho, 0)
        for o in offs:
            m8 = o - rho
            term = rolled[m8:m8 + CONV_SUB, :] * coef(o)
            acc = term if acc is None else acc + term
    return acc


def _conv_fwd(u, w32, cb, cg, cbe, tp, tc, dc):
    r = u.shape[0]
    hb = tc // CONV_HALO

    def body(a_ref, g_ref, ah_ref, gh_ref, w_ref, cb_ref, cg_ref, cbe_ref, c_ref, co_ref, hs_ref):
        t = pl.program_id(0)
        first = lax.rem(t * tc, tp) == 0
        hh = ah_ref[...] * _sigmoid(gh_ref[...])
        hs_ref[0:CONV_HALO, :] = jnp.where(first, 0.0, hh)
        hs_ref[CONV_HALO:CONV_HALO + tc, :] = a_ref[...] * _sigmoid(g_ref[...])

        def sub(k, carry):
            r0 = pl.multiple_of(k * CONV_SUB, CONV_SUB)
            win = hs_ref[pl.ds(r0, CONV_WIN), :]
            c = _conv_taps(win, lambda o: w_ref[o - 2:o - 1, :], 2) + cb_ref[...]
            c_ref[pl.ds(r0, CONV_SUB), :] = c
            xhat, _ = _ln(c)
            cn = xhat * cg_ref[...] + cbe_ref[...]
            co_ref[pl.ds(r0, CONV_SUB), :] = (cn * _sigmoid(cn)).astype(BF16)
            return carry

        lax.fori_loop(0, tc // CONV_SUB, sub, 0)

    vec = pl.BlockSpec((1, dc), lambda t: (0, 0))
    return pl.pallas_call(
        body, name="conv_fwd", grid=(r // tc,),
        in_specs=[pl.BlockSpec((tc, dc), lambda t: (t, 0)), pl.BlockSpec((tc, dc), lambda t: (t, 1)),
                  pl.BlockSpec((CONV_HALO, dc), lambda t: (jnp.maximum(t * hb - 1, 0), 0)),
                  pl.BlockSpec((CONV_HALO, dc), lambda t: (jnp.maximum(t * hb - 1, 0), 1)),
                  pl.BlockSpec((32, dc), lambda t: (0, 0)), vec, vec, vec],
        out_specs=[pl.BlockSpec((tc, dc), lambda t: (t, 0)), pl.BlockSpec((tc, dc), lambda t: (t, 0))],
        out_shape=[_sds((r, dc), F32), _sds((r, dc), BF16)],
        scratch_shapes=[pltpu.VMEM((CONV_HALO + tc, dc), F32)],
        compiler_params=_params(("parallel",)),
    )(u, u, u, u, w32, cb, cg, cbe)


def _tri_mm_all(tri, xs):
    parts = [_split3(x) for x in xs]
    acc = [None] * len(xs)
    for t in range(3):
        for j in range(len(xs)):
            term = _mm(tri, parts[j][t])
            acc[j] = term if t == 0 else acc[j] + term
    return acc


def _gla_prep(qk_ref, gd_ref, gup, gb, n0, kc):
    rows = [slice(j * CHUNK, (j + 1) * CHUNK) for j in range(kc)]
    ri = lax.broadcasted_iota(jnp.int32, (CHUNK, CHUNK), 0)
    ci = lax.broadcasted_iota(jnp.int32, (CHUNK, CHUNK), 1)
    low = (ri >= ci).astype(BF16)
    hk = GLA_HEADS * GLA_DK
    gds = [gd_ref[rw, :] for rw in rows]
    zs = [_mm(g.astype(BF16), gup) + gb for g in gds]
    reals = [(n0 + j) * CHUNK + lax.broadcasted_iota(jnp.int32, (CHUNK, 1), 0) >= PAD_FRONT for j in range(kc)]
    lgs = [jnp.where(reals[j], _log_sigmoid(zs[j]) * (1.0 / GLA_TAU), 0.0) for j in range(kc)]
    bs = _tri_mm_all(low, lgs)
    out = []
    for j in range(kc):
        b, bl = bs[j], _rowsum(lgs[j])
        q = qk_ref[rows[j], :hk] * QK_SCALE
        k = qk_ref[rows[j], hk:]
        eb, enb, ebl = jnp.exp(b), jnp.exp(-b), jnp.exp(bl - b)
        out.append(dict(rows=rows[j], gd=gds[j], z=zs[j], real=reals[j], eb=eb, enb=enb, ebl=ebl, gam=jnp.exp(bl),
                        qe=q * eb, ke=k * enb, kd=k * ebl))
    return out, ri, ci


def _gla_heads(p, v_ref):
    ops = []
    for h in range(GLA_HEADS):
        hp, h2 = divmod(h, 2)
        ls = slice(hp * LANES, (hp + 1) * LANES)
        m = _head_mask(h2)
        ops.append(dict(ls=ls, m=m, vs=slice(h * GLA_DV, (h + 1) * GLA_DV),
                        qe=jnp.where(m, p["qe"][:, ls], 0.0).astype(BF16),
                        kd=jnp.where(m, p["kd"][:, ls], 0.0).astype(BF16),
                        ke=p["ke"][:, ls].astype(BF16),
                        v=v_ref[p["rows"], h * GLA_DV:(h + 1) * GLA_DV].astype(BF16)))
    return ops


def _head_mask(h2):
    lane = lax.broadcasted_iota(jnp.int32, (1, LANES), 1)
    return (lane < GLA_DK) if h2 == 0 else (lane >= GLA_DK)


def _gla_fwd(u, gup, gb, gn, bsz, nc, kc):
    r = u.shape[0]
    hv = GLA_HEADS * GLA_DV
    ns = nc // kc

    def body(qk_ref, v_ref, r_ref, gd_ref, gup_ref, gb_ref, gn_ref, go_ref, sta_ref, st_ref):
        t = pl.program_id(1)

        @pl.when(t == 0)
        def _():
            st_ref[...] = jnp.zeros_like(st_ref)

        ps, ri, ci = _gla_prep(qk_ref, gd_ref, gup_ref[...], gb_ref[...], t * kc, kc)
        tril = ri >= ci
        items = [(j, h) for j in range(kc) for h in range(GLA_HEADS)]
        ops = [_gla_heads(p, v_ref) for p in ps]
        a = {jh: jnp.where(tril, _mm_nt(ops[jh[0]][jh[1]]["qe"], ops[jh[0]][jh[1]]["ke"]), 0.0).astype(BF16) for jh in items}
        oi = {jh: _mm(a[jh], ops[jh[0]][jh[1]]["v"]) for jh in items}
        inc = {jh: _mm_tn(ops[jh[0]][jh[1]]["v"], ops[jh[0]][jh[1]]["kd"]) for jh in items}
        sts = [st_ref[h] for h in range(GLA_HEADS)]
        for j, h in items:
            op, p = ops[j][h], ps[j]
            st = sts[h]
            sta_ref[j, h] = st
            o = oi[j, h] + _mm_nt(op["qe"], st.astype(BF16))
            sts[h] = st * p["gam"][:, op["ls"]] + inc[j, h]
            rs = lax.rsqrt(jnp.mean(o * o, axis=-1, keepdims=True) + LN_EPS)
            rr = r_ref[p["rows"], op["vs"]]
            go_ref[p["rows"], op["vs"]] = (o * rs * gn_ref[...] * (rr * _sigmoid(rr))).astype(BF16)
        for h in range(GLA_HEADS):
            st_ref[h] = sts[h]

    rowblk = lambda col: (lambda b, t: (b * ns + t, col))
    const = lambda b, t: (0, 0)
    return pl.pallas_call(
        body, name="gla_fwd", grid=(bsz, ns),
        in_specs=[pl.BlockSpec((kc * CHUNK, 512), rowblk(2)), pl.BlockSpec((kc * CHUNK, hv), rowblk(3)),
                  pl.BlockSpec((kc * CHUNK, hv), rowblk(4)), pl.BlockSpec((kc * CHUNK, LANES), rowblk(20)),
                  pl.BlockSpec((LANES, 256), const), pl.BlockSpec((1, 256), const), pl.BlockSpec((1, GLA_DV), const)],
        out_specs=[pl.BlockSpec((kc * CHUNK, hv), rowblk(0)),
                   pl.BlockSpec((kc, GLA_HEADS, LANES, LANES), lambda b, t: (b * ns + t, 0, 0, 0))],
        out_shape=[_sds((r, hv), BF16), _sds((bsz * nc, GLA_HEADS, LANES, LANES), F32)],
        scratch_shapes=[pltpu.VMEM((GLA_HEADS, LANES, LANES), F32)],
        compiler_params=_params(("parallel", "arbitrary")),
    )(u, u, u, u, gup, gb, gn)


def _outproj_fwd(s0, co, go, w_out, g1, b1, tm):
    r, d = s0.shape
    dc = co.shape[1]

    def body(s0_ref, co_ref, go_ref, w_ref, g_ref, b_ref, p1_ref, s1_ref, s1b_ref):
        mix = _mm(co_ref[...], w_ref[0:dc, :]) + _mm(go_ref[...], w_ref[dc:2 * dc, :])
        p1 = ALPHA * s0_ref[...] + mix
        p1_ref[...] = p1
        xhat, _ = _ln(p1)
        s1 = xhat * g_ref[...] + b_ref[...]
        s1_ref[...] = s1
        s1b_ref[...] = s1.astype(BF16)

    row = lambda w: pl.BlockSpec((tm, w), lambda i: (i, 0))
    vec = pl.BlockSpec((1, d), lambda i: (0, 0))
    return pl.pallas_call(
        body, name="outproj_fwd", grid=(r // tm,),
        in_specs=[row(d), row(dc), row(dc), pl.BlockSpec((2 * dc, d), lambda i: (0, 0)), vec, vec],
        out_specs=[row(d), row(d), row(d)],
        out_shape=[_sds((r, d), F32), _sds((r, d), F32), _sds((r, d), BF16)],
        compiler_params=_params(("parallel",)),
    )(s0, co, go, w_out, g1, b1)


def _mlp_fwd(s1, s1b, w1g, w2, g2, b2, tgt, tp, tm, ns):
    r, d = s1.shape
    nh, _, th = w1g.shape
    nj = nh // ns

    def body(s1_ref, sb_ref, w1_ref, w2_ref, g_ref, b_ref, t_ref, hm_ref, dp2_ref, dpb_ref, loss_ref, dg_ref, db_ref, acc_ref):
        i = pl.program_id(0)
        j = pl.program_id(1)

        @pl.when(jnp.logical_and(i == 0, j == 0))
        def _():
            loss_ref[...] = jnp.zeros_like(loss_ref)
            dg_ref[...] = jnp.zeros_like(dg_ref)
            db_ref[...] = jnp.zeros_like(db_ref)

        @pl.when(j == 0)
        def _():
            acc_ref[...] = jnp.zeros_like(acc_ref)

        hs = [_mm(sb_ref[...], w1_ref[s]) for s in range(ns)]
        acc = acc_ref[...]
        for s in range(ns):
            hm_ref[:, s * th:(s + 1) * th] = hs[s].astype(BF16)
            act = jnp.square(jnp.maximum(hs[s], 0.0))
            acc = acc + _mm(act.astype(BF16), w2_ref[s * th:(s + 1) * th, :])
        acc_ref[...] = acc

        @pl.when(j == nj - 1)
        def _():
            p2 = ALPHA * s1_ref[...] + acc_ref[...]
            xhat, rstd = _ln(p2)
            s2 = xhat * g_ref[...] + b_ref[...]
            isx = _row_in_seq(i, tm, tp) >= X_OFF
            err = jnp.where(isx, s2 - t_ref[...], 0.0)
            loss_ref[...] += 0.5 * jnp.sum(jnp.mean(err * err, axis=-1, keepdims=True))
            dy = err * (1.0 / d)
            dg_ref[...] += _rowsum(dy * xhat)
            db_ref[...] += _rowsum(dy)
            dp2 = _ln_bwd(dy * g_ref[...], xhat, rstd)
            dp2_ref[...] = dp2
            dpb_ref[...] = dp2.astype(BF16)

    row = pl.BlockSpec((tm, d), lambda i, j: (i, 0))
    vec = pl.BlockSpec((1, d), lambda i, j: (0, 0))
    return pl.pallas_call(
        body, name="mlp_fwd", grid=(r // tm, nj),
        in_specs=[row, row, pl.BlockSpec((ns, d, th), lambda i, j: (j, 0, 0)), pl.BlockSpec((ns * th, d), lambda i, j: (j, 0)),
                  vec, vec, row],
        out_specs=[pl.BlockSpec((tm, ns * th), lambda i, j: (i, j)), row, row,
                   pl.BlockSpec((8, LANES), lambda i, j: (0, 0)), vec, vec],
        out_shape=[_sds((r, nh * th), BF16), _sds((r, d), F32), _sds((r, d), BF16), _sds((8, LANES), F32),
                   _sds((1, d), F32), _sds((1, d), F32)],
        scratch_shapes=[pltpu.VMEM((tm, d), F32)],
        compiler_params=_params(("arbitrary", "arbitrary")),
    )(s1, s1b, w1g, w2, g2, b2, tgt)


def _mlp_bwd_act(dp2, dpb, hm, w1g, w2, p1, g1, tm, ns):
    r, d = dp2.shape
    nh, _, th = w1g.shape
    nj = nh // ns

    def body(dp2_ref, dpb_ref, hm_ref, w1_ref, w2_ref, p1_ref, g_ref, dh_ref, dp1_ref, dg_ref, db_ref, acc_ref):
        i = pl.program_id(0)
        j = pl.program_id(1)

        @pl.when(jnp.logical_and(i == 0, j == 0))
        def _():
            dg_ref[...] = jnp.zeros_like(dg_ref)
            db_ref[...] = jnp.zeros_like(db_ref)

        @pl.when(j == 0)
        def _():
            acc_ref[...] = jnp.zeros_like(acc_ref)

        dacts = [_mm_nt(dpb_ref[...], w2_ref[s * th:(s + 1) * th, :]) for s in range(ns)]
        acc = acc_ref[...]
        for s in range(ns):
            cols = slice(s * th, (s + 1) * th)
            dh = (dacts[s] * (2.0 * jnp.maximum(hm_ref[:, cols].astype(F32), 0.0))).astype(BF16)
            dh_ref[:, cols] = dh
            acc = acc + _mm_nt(dh, w1_ref[s])
        acc_ref[...] = acc

        @pl.when(j == nj - 1)
        def _():
            ds1 = ALPHA * dp2_ref[...] + acc_ref[...]
            xhat, rstd = _ln(p1_ref[...])
            dg_ref[...] += _rowsum(ds1 * xhat)
            db_ref[...] += _rowsum(ds1)
            dp1_ref[...] = _ln_bwd(ds1 * g_ref[...], xhat, rstd)

    row = pl.BlockSpec((tm, d), lambda i, j: (i, 0))
    vec = pl.BlockSpec((1, d), lambda i, j: (0, 0))
    blk = pl.BlockSpec((tm, ns * th), lambda i, j: (i, j))
    return pl.pallas_call(
        body, name="mlp_bwd_act", grid=(r // tm, nj),
        in_specs=[row, row, blk, pl.BlockSpec((ns, d, th), lambda i, j: (j, 0, 0)),
                  pl.BlockSpec((ns * th, d), lambda i, j: (j, 0)), row, vec],
        out_specs=[blk, row, vec, vec],
        out_shape=[_sds((r, nh * th), BF16), _sds((r, d), F32), _sds((1, d), F32), _sds((1, d), F32)],
        scratch_shapes=[pltpu.VMEM((tm, d), F32)],
        compiler_params=_params(("arbitrary", "arbitrary")),
    )(dp2, dpb, hm, w1g, w2, p1, g1)


def _mlp_bwd_w(s1b, hm, dh, dpb, nh, tm, ns):
    r, d = s1b.shape
    th = hm.shape[1] // nh

    def body(s1_ref, hm_ref, dh_ref, dp2_ref, dw1_ref, dw2_ref, a1_ref, a2_ref):
        i = pl.program_id(1)

        @pl.when(i == 0)
        def _():
            a1_ref[...] = jnp.zeros_like(a1_ref)
            a2_ref[...] = jnp.zeros_like(a2_ref)

        for s in range(ns):
            a1_ref[s] += _mm_tn(s1_ref[...], dh_ref[:, s * th:(s + 1) * th])
        for s in range(ns):
            act = jnp.square(jnp.maximum(hm_ref[:, s * th:(s + 1) * th].astype(F32), 0.0)).astype(BF16)
            a2_ref[s] += _mm_tn(act, dp2_ref[...])

        @pl.when(i == pl.num_programs(1) - 1)
        def _():
            dw1_ref[...] = a1_ref[...].astype(BF16)
            dw2_ref[...] = a2_ref[...].astype(BF16)

    row = pl.BlockSpec((tm, d), lambda j, i: (i, 0))
    blk = pl.BlockSpec((tm, ns * th), lambda j, i: (i, j))
    return pl.pallas_call(
        body, name="mlp_bwd_w", grid=(nh // ns, r // tm),
        in_specs=[row, blk, blk, row],
        out_specs=[pl.BlockSpec((ns, d, th), lambda j, i: (j, 0, 0)), pl.BlockSpec((ns, th, d), lambda j, i: (j, 0, 0))],
        out_shape=[_sds((nh, d, th), BF16), _sds((nh, th, d), BF16)],
        scratch_shapes=[pltpu.VMEM((ns, d, th), F32), pltpu.VMEM((ns, th, d), F32)],
        compiler_params=_params(("parallel", "arbitrary")),
    )(s1b, hm, dh, dpb)


def _outproj_bwd(dp1, co, go, w_out, dep, tm):
    r, d = dp1.shape
    dc = co.shape[1]

    def body(dp_ref, co_ref, go_ref, w_ref, dep_ref, dmi_ref, dw_ref, acc_ref):
        i = pl.program_id(0)

        @pl.when(i == 0)
        def _():
            acc_ref[...] = jnp.zeros_like(acc_ref)

        dpb = dp_ref[...].astype(BF16)
        dmi_ref[...] = _mm_nt(dpb, w_ref[...])
        acc_ref[0:dc, :] += _mm_tn(co_ref[...], dpb)
        acc_ref[dc:2 * dc, :] += _mm_tn(go_ref[...], dpb)

        @pl.when(i == pl.num_programs(0) - 1)
        def _():
            dw_ref[...] = acc_ref[...].astype(BF16)

    row = lambda w: pl.BlockSpec((tm, w), lambda i: (i, 0))
    full = pl.BlockSpec((2 * dc, d), lambda i: (0, 0))
    return pl.pallas_call(
        body, name="outproj_bwd", grid=(r // tm,),
        in_specs=[row(d), row(dc), row(dc), full, pl.BlockSpec(memory_space=pl.ANY)],
        out_specs=[row(2 * dc), full],
        out_shape=[_sds((r, 2 * dc), F32), _sds((2 * dc, d), BF16)],
        scratch_shapes=[pltpu.VMEM((2 * dc, d), F32)],
        compiler_params=_params(("arbitrary",)),
    )(dp1, co, go, w_out, dep)


def _gla_bwd(u, dmi, sta, gup, gb, gn, dep, bsz, nc, kc):
    r = u.shape[0]
    hv = GLA_HEADS * GLA_DV
    hk = GLA_HEADS * GLA_DK
    ns = nc // kc

    def body(qk_ref, v_ref, r_ref, gd_ref, dgo_ref, sta_ref, gup_ref, gb_ref, gn_ref, dep_ref,
             dqk_ref, dv_ref, dr_ref, dgd_ref, dgn_ref, dgb_ref, dgup_ref, dst_ref):
        bi = pl.program_id(0)
        t = pl.program_id(1)

        @pl.when(jnp.logical_and(bi == 0, t == 0))
        def _():
            dgn_ref[...] = jnp.zeros_like(dgn_ref)
            dgb_ref[...] = jnp.zeros_like(dgb_ref)
            dgup_ref[...] = jnp.zeros_like(dgup_ref)

        @pl.when(t == 0)
        def _():
            dst_ref[...] = jnp.zeros_like(dst_ref)

        ps, ri, ci = _gla_prep(qk_ref, gd_ref, gup_ref[...], gb_ref[...], (ns - 1 - t) * kc, kc)
        tril = ri >= ci
        items = [(j, h) for j in reversed(range(kc)) for h in range(GLA_HEADS)]
        ops = [_gla_heads(p, v_ref) for p in ps]
        op = lambda jh: ops[jh[0]][jh[1]]
        st = {jh: sta_ref[jh[0], jh[1]] for jh in items}
        stb = {jh: st[jh].astype(BF16) for jh in items}
        a = {jh: jnp.where(tril, _mm_nt(op(jh)["qe"], op(jh)["ke"]), 0.0).astype(BF16) for jh in items}
        o1 = {jh: _mm(a[jh], op(jh)["v"]) for jh in items}
        o2 = {jh: _mm_nt(op(jh)["qe"], stb[jh]) for jh in items}
        dob = {}
        dgn = jnp.zeros((1, GLA_DV), F32)
        for jh in items:
            rows, vs = ps[jh[0]]["rows"], op(jh)["vs"]
            o = o1[jh] + o2[jh]
            rr = r_ref[rows, vs]
            sr = _sigmoid(rr)
            rs = lax.rsqrt(jnp.mean(o * o, axis=-1, keepdims=True) + LN_EPS)
            y = o * rs
            dgo = dgo_ref[rows, vs]
            don = dgo * (rr * sr)
            dr_ref[rows, vs] = dgo * (y * gn_ref[...]) * (sr * (1.0 + rr * (1.0 - sr)))
            dgn = dgn + _rowsum(don * y)
            dxn = don * gn_ref[...]
            dob[jh] = (rs * (dxn - y * jnp.mean(dxn * y, axis=-1, keepdims=True))).astype(BF16)
        da = {jh: jnp.where(tril, _mm_nt(dob[jh], op(jh)["v"]), 0.0).astype(BF16) for jh in items}
        dv1 = {jh: _mm_tn(a[jh], dob[jh]) for jh in items}
        dqe1 = {jh: _mm(da[jh], op(jh)["ke"]) for jh in items}
        dqe2 = {jh: _mm(dob[jh], stb[jh]) for jh in items}
        dke1 = {jh: _mm_tn(da[jh], op(jh)["qe"]) for jh in items}
        inc = {jh: _mm_tn(dob[jh], op(jh)["qe"]) for jh in items}
        dsts = [dst_ref[h] for h in range(GLA_HEADS)]
        dkd1, dgam1 = {}, {}
        for jh in items:
            j, h = jh
            dst = dsts[h]
            dstb = dst.astype(BF16)
            dv_ref[ps[j]["rows"], op(jh)["vs"]] = dv1[jh] + _mm_nt(op(jh)["kd"], dstb)
            dkd1[jh] = _mm(op(jh)["v"], dstb)
            dgam1[jh] = _rowsum(dst * st[jh])
            dsts[h] = dst * ps[j]["gam"][:, op(jh)["ls"]] + inc[jh]
        for h in range(GLA_HEADS):
            dst_ref[h] = dsts[h]
        upper = (ri <= ci).astype(BF16)
        dbs, dbls = [], []
        for j in range(kc):
            p = ps[j]
            tiles = [[op((j, 2 * hp + h2)) for h2 in range(2)] for hp in range(GLA_HEADS // 2)]
            head = lambda d, hp, h2: d[j, 2 * hp + h2]
            lanes = lambda f: jnp.concatenate([f(hp) for hp in range(GLA_HEADS // 2)], axis=1)
            dqe = lanes(lambda hp: sum(jnp.where(tiles[hp][h2]["m"], head(dqe1, hp, h2) + head(dqe2, hp, h2), 0.0)
                                       for h2 in range(2)))
            dke = lanes(lambda hp: head(dke1, hp, 0) + head(dke1, hp, 1))
            dkd = lanes(lambda hp: sum(jnp.where(tiles[hp][h2]["m"], head(dkd1, hp, h2), 0.0) for h2 in range(2)))
            dgam = lanes(lambda hp: head(dgam1, hp, 0) + head(dgam1, hp, 1))
            dqk_ref[p["rows"], :hk] = dqe * p["eb"] * QK_SCALE
            dqk_ref[p["rows"], hk:] = dke * p["enb"] + dkd * p["ebl"]
            dkdkd = dkd * p["kd"]
            dbs.append(dqe * p["qe"] - dke * p["ke"] - dkdkd)
            dbls.append(_rowsum(dkdkd) + dgam * p["gam"])
        dlgs = _tri_mm_all(upper, dbs)
        dzb = []
        dgb = jnp.zeros((1, hk), F32)
        for j in range(kc):
            p = ps[j]
            dz = jnp.where(p["real"], (dlgs[j] + dbls[j]) * (1.0 / GLA_TAU) * _sigmoid(-p["z"]), 0.0)
            dgb = dgb + _rowsum(dz)
            dzb.append(dz.astype(BF16))
        dgup = sum(_mm_tn(ps[j]["gd"].astype(BF16), dzb[j]) for j in range(kc))
        for j in range(kc):
            dgd_ref[ps[j]["rows"], :] = _mm_nt(dzb[j], gup_ref[...])
        dgb_ref[...] += dgb
        dgup_ref[...] += dgup
        dgn_ref[...] += dgn

    rowblk = lambda col: (lambda b, t: (b * ns + ns - 1 - t, col))
    const = lambda b, t: (0, 0)
    return pl.pallas_call(
        body, name="gla_bwd", grid=(bsz, ns),
        in_specs=[pl.BlockSpec((kc * CHUNK, 2 * hk), rowblk(2)), pl.BlockSpec((kc * CHUNK, hv), rowblk(3)),
                  pl.BlockSpec((kc * CHUNK, hv), rowblk(4)), pl.BlockSpec((kc * CHUNK, LANES), rowblk(20)),
                  pl.BlockSpec((kc * CHUNK, hv), rowblk(1)),
                  pl.BlockSpec((kc, GLA_HEADS, LANES, LANES), lambda b, t: (b * ns + ns - 1 - t, 0, 0, 0)),
                  pl.BlockSpec((LANES, 256), const), pl.BlockSpec((1, 256), const), pl.BlockSpec((1, GLA_DV), const),
                  pl.BlockSpec(memory_space=pl.ANY)],
        out_specs=[pl.BlockSpec((kc * CHUNK, 2 * hk), rowblk(0)), pl.BlockSpec((kc * CHUNK, hv), rowblk(0)),
                   pl.BlockSpec((kc * CHUNK, hv), rowblk(0)), pl.BlockSpec((kc * CHUNK, LANES), rowblk(0)),
                   pl.BlockSpec((1, GLA_DV), const), pl.BlockSpec((1, 256), const), pl.BlockSpec((LANES, 256), const)],
        out_shape=[_sds((r, 2 * hk), F32), _sds((r, hv), F32), _sds((r, hv), F32), _sds((r, LANES), F32),
                   _sds((1, GLA_DV), F32), _sds((1, 256), F32), _sds((LANES, 256), F32)],
        scratch_shapes=[pltpu.VMEM((GLA_HEADS, LANES, LANES), F32)],
        compiler_params=_params(("arbitrary", "arbitrary")),
    )(u, u, u, u, dmi, sta, gup, gb, gn, dep)


def _conv_bwd(u, c, dmi, w32, cg, cbe, tp, tc, dc):
    r = u.shape[0]
    hb = tc // CONV_HALO
    nhalo = r // CONV_HALO

    def dconv(cv, dco, cg_ref, cbe_ref):
        xhat, rstd = _ln(cv)
        cn = xhat * cg_ref[...] + cbe_ref[...]
        sg = _sigmoid(cn)
        dcn = dco * (sg * (1.0 + cn * (1.0 - sg)))
        return _ln_bwd(dcn * cg_ref[...], xhat, rstd), dcn, xhat

    def body(a_ref, g_ref, ah_ref, gh_ref, c_ref, dco_ref, ch_ref, dcoh_ref, w_ref, cg_ref, cbe_ref,
             du_ref, dw_ref, dcb_ref, dcg_ref, dcbe_ref, hs_ref, dcs_ref):
        t = pl.program_id(0)

        @pl.when(t == 0)
        def _():
            dw_ref[...] = jnp.zeros_like(dw_ref)
            dcb_ref[...] = jnp.zeros_like(dcb_ref)
            dcg_ref[...] = jnp.zeros_like(dcg_ref)
            dcbe_ref[...] = jnp.zeros_like(dcbe_ref)

        first = lax.rem(t * tc, tp) == 0
        last = lax.rem((t + 1) * tc, tp) == 0
        hh = ah_ref[...] * _sigmoid(gh_ref[...])
        hs_ref[0:CONV_HALO, :] = jnp.where(first, 0.0, hh)
        hs_ref[CONV_HALO:CONV_HALO + tc, :] = a_ref[...] * _sigmoid(g_ref[...])
        dch, _, _ = dconv(ch_ref[...], dcoh_ref[...], cg_ref, cbe_ref)
        dcs_ref[tc:tc + CONV_HALO, :] = jnp.where(last, 0.0, dch)

        def sub1(k, carry):
            r0 = pl.multiple_of(k * CONV_SUB, CONV_SUB)
            dcv, dcn, xhat = dconv(c_ref[pl.ds(r0, CONV_SUB), :], dco_ref[pl.ds(r0, CONV_SUB), :], cg_ref, cbe_ref)
            dcs_ref[pl.ds(r0, CONV_SUB), :] = dcv
            dcb_ref[...] += _rowsum(dcv)
            dcg_ref[...] += _rowsum(dcn * xhat)
            dcbe_ref[...] += _rowsum(dcn)
            return carry

        lax.fori_loop(0, tc // CONV_SUB, sub1, 0)

        def sub2(k, carry):
            r0 = pl.multiple_of(k * CONV_SUB, CONV_SUB)
            dwin = dcs_ref[pl.ds(r0, CONV_WIN), :]
            dh = _conv_taps(dwin, lambda o: w_ref[CONV_WIDTH - 1 - o:CONV_WIDTH - o, :], 0)
            av = a_ref[pl.ds(r0, CONV_SUB), :]
            sg = _sigmoid(g_ref[pl.ds(r0, CONV_SUB), :])
            du_ref[pl.ds(r0, CONV_SUB), 0:dc] = dh * sg
            du_ref[pl.ds(r0, CONV_SUB), dc:2 * dc] = dh * av * sg * (1.0 - sg)
            hwin = hs_ref[pl.ds(r0, CONV_WIN), :]
            dcv = dwin[0:CONV_SUB, :]
            for rho in range(8):
                offs = [o for o in range(2, 2 + CONV_WIDTH) if o % 8 == rho]
                rolled = hwin if rho == 0 else pltpu.roll(hwin, CONV_WIN - rho, 0)
                for o in offs:
                    m8 = o - rho
                    dw_ref[o - 2:o - 1, :] += _rowsum(dcv * rolled[m8:m8 + CONV_SUB, :])
            return carry

        lax.fori_loop(0, tc // CONV_SUB, sub2, 0)

    vec = pl.BlockSpec((1, dc), lambda t: (0, 0))
    prev = lambda col: (lambda t: (jnp.maximum(t * hb - 1, 0), col))
    nxt = lambda col: (lambda t: (jnp.minimum((t + 1) * hb, nhalo - 1), col))
    return pl.pallas_call(
        body, name="conv_bwd", grid=(r // tc,),
        in_specs=[pl.BlockSpec((tc, dc), lambda t: (t, 0)), pl.BlockSpec((tc, dc), lambda t: (t, 1)),
                  pl.BlockSpec((CONV_HALO, dc), prev(0)), pl.BlockSpec((CONV_HALO, dc), prev(1)),
                  pl.BlockSpec((tc, dc), lambda t: (t, 0)), pl.BlockSpec((tc, dc), lambda t: (t, 0)),
                  pl.BlockSpec((CONV_HALO, dc), nxt(0)), pl.BlockSpec((CONV_HALO, dc), nxt(0)),
                  pl.BlockSpec((32, dc), lambda t: (0, 0)), vec, vec],
        out_specs=[pl.BlockSpec((tc, 2 * dc), lambda t: (t, 0)), pl.BlockSpec((32, dc), lambda t: (0, 0)), vec, vec, vec],
        out_shape=[_sds((r, 2 * dc), F32), _sds((32, dc), F32), _sds((1, dc), F32), _sds((1, dc), F32), _sds((1, dc), F32)],
        scratch_shapes=[pltpu.VMEM((CONV_HALO + tc, dc), F32), pltpu.VMEM((tc + CONV_HALO, dc), F32)],
        compiler_params=_params(("arbitrary",)),
    )(u, u, u, u, c, dmi, c, dmi, w32, cg, cbe)


def _inproj_bwd(dp1, dus, xcat, g_in, w_in, dep, tp, tm):
    r, d = dp1.shape
    widths = [x.shape[1] for x in dus]
    offs = [sum(widths[:k]) for k in range(len(widths))]
    n = w_in.shape[1]
    nd = len(dus)
    tps = tp // tm

    def body(*refs):
        dp_ref = refs[0]
        du_refs = refs[1:1 + nd]
        x_ref, g_ref, w_ref, _, dx_ref, dmeta_ref, dg_ref, db_ref = refs[1 + nd:]
        i = pl.program_id(0)

        @pl.when(i == 0)
        def _():
            dmeta_ref[...] = jnp.zeros_like(dmeta_ref)
            dg_ref[...] = jnp.zeros_like(dg_ref)
            db_ref[...] = jnp.zeros_like(db_ref)

        ds0 = ALPHA * dp_ref[...]
        for k in range(nd):
            ds0 = ds0 + _mm_nt(du_refs[k][...].astype(BF16), w_ref[:, offs[k]:offs[k] + widths[k]])
        real = _row_in_seq(i, tm, tp) >= PAD_FRONT
        ds0 = jnp.where(real, ds0, 0.0)
        xhat, rstd = _ln(x_ref[...])
        dg_ref[...] += _rowsum(ds0 * xhat)
        db_ref[...] += _rowsum(ds0)
        dx = jnp.where(real, _ln_bwd(ds0 * g_ref[...], xhat, rstd), 0.0)
        dx_ref[...] = dx

        @pl.when(lax.rem(i, tps) == 0)
        def _():
            dmeta_ref[...] += dx[PAD_FRONT:X_OFF, :]

    row = lambda w: pl.BlockSpec((tm, w), lambda i: (i, 0))
    vec = pl.BlockSpec((1, d), lambda i: (0, 0))
    return pl.pallas_call(
        body, name="inproj_bwd", grid=(r // tm,),
        in_specs=[row(d)] + [row(w) for w in widths] + [row(d), vec, pl.BlockSpec((d, n), lambda i: (0, 0)),
                                                        pl.BlockSpec(memory_space=pl.ANY)],
        out_specs=[row(d), pl.BlockSpec((N_META, d), lambda i: (0, 0)), vec, vec],
        out_shape=[_sds((r, d), F32), _sds((N_META, d), F32), _sds((1, d), F32), _sds((1, d), F32)],
        compiler_params=_params(("arbitrary",)),
    )(dp1, *dus, xcat, g_in, w_in, dep)


def _inproj_bwd_w(s0, dus, tm):
    r, d = s0.shape
    widths = [x.shape[1] for x in dus]
    nd = len(dus)

    def body(*refs):
        s_ref = refs[0]
        du_refs = refs[1:1 + nd]
        dw_refs = refs[1 + nd:]
        i = pl.program_id(0)

        @pl.when(i == 0)
        def _():
            for k in range(nd):
                dw_refs[k][...] = jnp.zeros_like(dw_refs[k])

        sb = s_ref[...].astype(BF16)
        for k in range(nd):
            dw_refs[k][...] += _mm_tn(sb, du_refs[k][...].astype(BF16))

    row = lambda w: pl.BlockSpec((tm, w), lambda i: (i, 0))
    return pl.pallas_call(
        body, name="inproj_bwd_w", grid=(r // tm,),
        in_specs=[row(d)] + [row(w) for w in widths],
        out_specs=[pl.BlockSpec((d, w), lambda i: (0, 0)) for w in widths],
        out_shape=[_sds((d, w), F32) for w in widths],
        compiler_params=_params(("arbitrary",)),
    )(s0, *dus)


def _local_step(x, tgt, meta, ln_in_g, ln_in_b, w_in, conv_w, conv_b, conv_ln_g, conv_ln_b, gate_up, gate_bias,
                gla_norm_g, late_weights, ln1_g, ln1_b, ln2_g, ln2_b, push):
    bsz, seq, d = x.shape
    tp = X_OFF + seq
    assert tp % CHUNK == 0
    nc = tp // CHUNK
    r = bsz * tp
    dc = conv_b.shape[1]
    tm = _pick_tile(tp, (352, 128, 64))
    tc = _pick_tile(tp, (704, 128, 64))

    xcat = jnp.concatenate([jnp.zeros((bsz, PAD_FRONT, d), F32), jnp.broadcast_to(meta[None], (bsz, N_META, d)), x],
                           axis=1).reshape(r, d)
    tgt_p = jnp.pad(tgt, ((0, 0), (X_OFF, 0), (0, 0))).reshape(r, d)
    w32 = jnp.pad(conv_w, ((0, 32 - CONV_WIDTH), (0, 0)))
    gup = jnp.pad(gate_up, ((0, LANES - GLA_RANK), (0, 0))).astype(BF16)

    s0, u = _inproj_fwd(xcat, ln_in_g, ln_in_b, w_in, tp, tm)
    c, co = _conv_fwd(u, w32, conv_b, conv_ln_g, conv_ln_b, tp, tc, dc)
    kc = _pick_tile(nc, (3, 2, 1))
    go, sta = _gla_fwd(u, gup, gate_bias, gla_norm_g, bsz, nc, kc)
    w_out, w1g, w2 = late_weights(go)
    nh = w1g.shape[0]
    tmm = _pick_tile(tp, (704, 128, 64))
    ns = 2
    p1, s1, s1b = _outproj_fwd(s0, co, go, w_out, ln1_g, ln1_b, tm)
    hm, dp2, dpb, loss, dg2, db2 = _mlp_fwd(s1, s1b, w1g, w2, ln2_g, ln2_b, tgt_p, tp, tmm, ns)

    dh, dp1, dg1, db1 = _mlp_bwd_act(dp2, dpb, hm, w1g, w2, p1, ln1_g, tmm, ns)
    dw1, dw2 = _mlp_bwd_w(s1b, hm, dh, dpb, nh, tmm, ns)
    tok = push("ff", (dw1, dw2))
    dmi, dwo = _outproj_bwd(dp1, co, go, w_out, tok, tm)
    tok = push("out", (dwo,))
    dqk, dv, dr, dgd, dgn, dgb, dgup = _gla_bwd(u, dmi, sta, gup, gate_bias, gla_norm_g, tok, bsz, nc, kc)
    dcv, dcw, dcb, dcg, dcbe = _conv_bwd(u, c, dmi, w32, conv_ln_g, conv_ln_b, tp, tc, dc)
    dus = [dcv, dqk, dv, dr, dgd]
    dwi = _inproj_bwd_w(s0, dus, tm)
    tok = push("in", (jnp.concatenate(dwi, axis=1),))
    dxcat, dmeta, dgi, dbi = _inproj_bwd(dp1, dus, xcat, ln_in_g, w_in, tok, tp, tm)

    grad_x = dxcat.reshape(bsz, tp, d)[:, X_OFF:, :]
    return dict(loss=loss[0, 0], grad_x=grad_x, meta_tokens=dmeta, ln_in_g=dgi, ln_in_b=dbi,
                conv_w=dcw[:CONV_WIDTH], conv_b=dcb, conv_ln_g=dcg, conv_ln_b=dcbe,
                gate_up=dgup[:GLA_RANK], gate_bias=dgb, gla_norm_g=dgn, ln1_g=dg1, ln1_b=db1, ln2_g=dg2, ln2_b=db2)


def _exchange(arrays, scatter, name):
    na = len(arrays)
    npeer = N_DEV - 1

    def body(*refs):
        srcs = refs[:na]
        outs = refs[na:2 * na]
        send_sems, recv_sems, local_sems = refs[2 * na:]
        xi, yi, ci = (lax.axis_index(a) for a in MESH_AXES)
        me = 4 * xi + 2 * yi + ci
        copies = []
        for a in range(na):
            own = srcs[a].at[me] if scatter[a] else srcs[a]
            cp = pltpu.make_async_copy(own, outs[a].at[me], local_sems.at[a])
            cp.start()
            copies.append(cp)
        remote = []
        for k in range(1, N_DEV):
            px, py, pc = xi ^ (k >> 2), yi ^ ((k >> 1) & 1), ci ^ (k & 1)
            peer = 4 * px + 2 * py + pc
            for a in range(na):
                src = srcs[a].at[peer] if scatter[a] else srcs[a]
                cp = pltpu.make_async_remote_copy(
                    src_ref=src, dst_ref=outs[a].at[me],
                    send_sem=send_sems.at[a * npeer + k - 1], recv_sem=recv_sems.at[a * npeer + k - 1],
                    device_id=(px, py, pc), device_id_type=pl.DeviceIdType.MESH)
                cp.start()
                remote.append(cp)
        for cp in remote:
            cp.wait()
        for cp in copies:
            cp.wait()

    out_shape = [_sds(a.shape if scatter[i] else (N_DEV,) + a.shape, a.dtype) for i, a in enumerate(arrays)]
    anyspec = pl.BlockSpec(memory_space=pl.ANY)
    return pl.pallas_call(
        body, name=name,
        in_specs=[anyspec] * na, out_specs=[anyspec] * na, out_shape=out_shape,
        scratch_shapes=[pltpu.SemaphoreType.DMA((na * npeer,)), pltpu.SemaphoreType.DMA((na * npeer,)),
                        pltpu.SemaphoreType.DMA((na,))],
    )(*arrays)


def _peers(xi, yi, ci):
    for k in range(1, N_DEV):
        px, py, pc = xi ^ (k >> 2), yi ^ ((k >> 1) & 1), ci ^ (k & 1)
        yield (px, py, pc), 4 * px + 2 * py + pc


def _sc_exchange(arrays, scatter, name, collective_id, after=None):
    na = len(arrays)
    npeer = N_DEV - 1
    ndep = 0 if after is None else 1

    def body(*refs):
        srcs = refs[:na]
        outs = refs[na + ndep:2 * na + ndep]
        send_sems, recv_sems, own_sems = refs[2 * na + ndep:]
        xi, yi, ci = (lax.axis_index(a) for a in MESH_AXES)
        me = 4 * xi + 2 * yi + ci
        barrier = pltpu.get_barrier_semaphore()
        for pos, _ in _peers(xi, yi, ci):
            pl.semaphore_signal(barrier, inc=1, device_id=pos, device_id_type=pl.DeviceIdType.MESH)
        pl.semaphore_wait(barrier, npeer)
        own = [pltpu.make_async_copy(srcs[a].at[me] if scatter[a] else srcs[a], outs[a].at[me], own_sems.at[a])
               for a in range(na)]
        for cp in own:
            cp.start()
        remote = []
        for a in range(na):
            for k, (pos, peer) in enumerate(_peers(xi, yi, ci)):
                cp = pltpu.make_async_remote_copy(
                    src_ref=srcs[a].at[peer] if scatter[a] else srcs[a], dst_ref=outs[a].at[me],
                    send_sem=send_sems.at[a * npeer + k], recv_sem=recv_sems.at[a * npeer + k],
                    device_id=pos, device_id_type=pl.DeviceIdType.MESH)
                cp.start()
                remote.append(cp)
        for cp in own:
            cp.wait()
        for cp in remote:
            cp.wait()

    out_type = [_sds(a.shape if scatter[i] else (N_DEV,) + a.shape, a.dtype) for i, a in enumerate(arrays)]
    sent = sum(a.size * a.dtype.itemsize // (N_DEV if scatter[i] else 1) for i, a in enumerate(arrays))
    return pl.kernel(
        body, out_type=out_type, mesh=plsc.ScalarSubcoreMesh(axis_name="seq", num_cores=1), name=name,
        scratch_types=[pltpu.SemaphoreType.DMA((na * npeer,)), pltpu.SemaphoreType.DMA((na * npeer,)),
                       pltpu.SemaphoreType.DMA((na,))],
        compiler_params=pltpu.CompilerParams(collective_id=collective_id),
        cost_estimate=pl.CostEstimate(flops=0, transcendentals=0, bytes_accessed=2 * N_DEV * sent,
                                      remote_bytes_transferred=npeer * sent),
    )(*arrays, *([] if after is None else [after]))


def _sc_gather(arrays, name, collective_id, after=None):
    na = len(arrays)
    ndep = 0 if after is None else 1
    npair = N_DEV - 1

    def body(*refs):
        srcs = refs[:na]
        outs = refs[na + ndep:2 * na + ndep]
        send_sems, recv_sems, own_sems = refs[2 * na + ndep:]
        xi, yi, ci = (lax.axis_index(a) for a in MESH_AXES)
        me = 4 * xi + 2 * yi + ci
        sibling = (xi, yi, 1 - ci)
        chips = [(1 - xi, yi), (xi, 1 - yi), (1 - xi, 1 - yi)]
        barrier = pltpu.get_barrier_semaphore()
        for pos, _ in _peers(xi, yi, ci):
            pl.semaphore_signal(barrier, inc=1, device_id=pos, device_id_type=pl.DeviceIdType.MESH)
        pl.semaphore_wait(barrier, npair)

        def copy(a, k, src, slot, to):
            return pltpu.make_async_remote_copy(
                src_ref=src, dst_ref=outs[a].at[slot], send_sem=send_sems.at[a * npair + k],
                recv_sem=recv_sems.at[a * npair + k], device_id=to, device_id_type=pl.DeviceIdType.MESH)

        own = [pltpu.make_async_copy(srcs[a], outs[a].at[me], own_sems.at[a]) for a in range(na)]
        for cp in own:
            cp.start()
        sent = []
        for a in range(na):
            sent.append(copy(a, 0, srcs[a], me, sibling))
            sent += [copy(a, 1 + j, srcs[a], me, (*chip, ci)) for j, chip in enumerate(chips)]
        for cp in sent:
            cp.start()
        for j, (cx, cy) in enumerate(chips):
            slot = 4 * cx + 2 * cy + ci
            for a in range(na):
                copy(a, 1 + j, srcs[a], slot, sibling).wait_recv()
                cp = copy(a, 4 + j, outs[a].at[slot], slot, sibling)
                cp.start()
                sent.append(cp)
        for a in range(na):
            copy(a, 0, srcs[a], me, sibling).wait_recv()
            for j in range(len(chips)):
                copy(a, 4 + j, srcs[a], me, sibling).wait_recv()
        for cp in sent:
            cp.wait_send()
        for cp in own:
            cp.wait()

    out_type = [_sds((N_DEV,) + a.shape, a.dtype) for a in arrays]
    sent_bytes = sum(a.size * a.dtype.itemsize for a in arrays)
    return pl.kernel(
        body, out_type=out_type, mesh=plsc.ScalarSubcoreMesh(axis_name="seq", num_cores=1), name=name,
        scratch_types=[pltpu.SemaphoreType.DMA((na * npair,)), pltpu.SemaphoreType.DMA((na * npair,)),
                       pltpu.SemaphoreType.DMA((na,))],
        compiler_params=pltpu.CompilerParams(collective_id=collective_id),
        cost_estimate=pl.CostEstimate(flops=0, transcendentals=0, bytes_accessed=2 * N_DEV * sent_bytes,
                                      remote_bytes_transferred=npair * sent_bytes),
    )(*arrays, *([] if after is None else [after]))


def _push_start(arrays, scatter, name, dep=None):
    na = len(arrays)
    shapes = [a.shape if scatter[i] else (N_DEV,) + a.shape for i, a in enumerate(arrays)]
    hbm = pl.BlockSpec(memory_space=pltpu.HBM)
    sem = pl.BlockSpec(memory_space=pltpu.SEMAPHORE)
    ndep = 0 if dep is None else 1

    def body(*refs):
        srcs = refs[:na]
        lands = refs[na:2 * na]
        send_sems, recv_sems = refs[2 * na + ndep:2 * na + ndep + 2]
        own_sems = refs[4 * na + ndep + 2]
        xi, yi, ci = (lax.axis_index(a) for a in MESH_AXES)
        me = 4 * xi + 2 * yi + ci
        own = [pltpu.make_async_copy(srcs[a].at[me] if scatter[a] else srcs[a], lands[a].at[me], own_sems.at[a])
               for a in range(na)]
        for cp in own:
            cp.start()
        for cp in own:
            cp.wait()
        for a in range(na):
            for pos, peer in _peers(xi, yi, ci):
                pltpu.make_async_remote_copy(
                    src_ref=srcs[a].at[peer] if scatter[a] else srcs[a], dst_ref=lands[a].at[me],
                    send_sem=send_sems.at[a], recv_sem=recv_sems.at[a],
                    device_id=pos, device_id_type=pl.DeviceIdType.MESH).start()

    ins = [pltpu.with_memory_space_constraint(a, pltpu.HBM) for a in arrays]
    ins += [pltpu.with_memory_space_constraint(lax.empty(s, a.dtype), pltpu.HBM) for s, a in zip(shapes, arrays)]
    res = pl.pallas_call(
        body, name=name,
        in_specs=[hbm] * (2 * na) + [pl.BlockSpec(memory_space=pl.ANY)] * ndep,
        out_specs=[sem, sem] + [hbm] * (2 * na),
        out_shape=[pltpu.SemaphoreType.DMA((na,)), pltpu.SemaphoreType.DMA((na,))]
                  + [pltpu.HBM(a.shape, a.dtype) for a in arrays] + [pltpu.HBM(s, a.dtype) for s, a in zip(shapes, arrays)],
        input_output_aliases={i: 2 + i for i in range(2 * na)},
        scratch_shapes=[pltpu.SemaphoreType.DMA((na,))],
        compiler_params=pltpu.CompilerParams(has_side_effects=pltpu.SideEffectType.DATAFLOW_SIDE_EFFECTING),
    )(*ins, *([] if dep is None else [dep]))
    return (res[0], res[1], list(res[2:2 + na]), list(res[2 + na:2 + 2 * na])), res[2]


def _push_wait(handle, after, name):
    send_sems, recv_sems, srcs, lands = handle
    na = len(srcs)
    hbm = pl.BlockSpec(memory_space=pltpu.HBM)
    sem = pl.BlockSpec(memory_space=pltpu.SEMAPHORE)

    def body(*refs):
        land_refs = refs[na:2 * na]
        send_ref, recv_ref = refs[2 * na:2 * na + 2]
        me = tuple(lax.axis_index(a) for a in MESH_AXES)
        for a in range(na):
            seven = land_refs[a].at[pl.ds(0, N_DEV - 1)]
            cp = pltpu.make_async_remote_copy(src_ref=seven, dst_ref=seven, send_sem=send_ref.at[a], recv_sem=recv_ref.at[a],
                                              device_id=me, device_id_type=pl.DeviceIdType.MESH)
            cp.wait_send()
            cp.wait_recv()

    res = pl.pallas_call(
        body, name=name,
        in_specs=[hbm] * (2 * na) + [sem, sem, pl.BlockSpec(memory_space=pl.ANY)],
        out_specs=[hbm] * (2 * na),
        out_shape=[pltpu.HBM(a.shape, a.dtype) for a in srcs] + [pltpu.HBM(a.shape, a.dtype) for a in lands],
        input_output_aliases={i: i for i in range(2 * na)},
        compiler_params=pltpu.CompilerParams(has_side_effects=pltpu.SideEffectType.DATAFLOW_SIDE_EFFECTING),
    )(*srcs, *lands, send_sems, recv_sems, after)
    return list(res[na:])


def _adamw(w, g, m, v):
    m = ADAM_B1 * m + (1.0 - ADAM_B1) * g
    v = ADAM_B2 * v + (1.0 - ADAM_B2) * jnp.square(g)
    m_hat = m / (1.0 - ADAM_B1 ** ADAM_STEP)
    v_hat = v / (1.0 - ADAM_B2 ** ADAM_STEP)
    delta = -ADAM_LR * (m_hat / (jnp.sqrt(v_hat) + ADAM_EPS) + ADAM_WD * w)
    return delta, m, v


def _sum_devices(ref):
    g = ref[0].astype(F32)
    for k in range(1, N_DEV):
        g = g + ref[k].astype(F32)
    return g


def _update_big(parts, w, m, v, name):
    rows, cols = w.shape
    tr = _pick_tile(rows, (128, 64, 16))

    def body(p_ref, w_ref, m_ref, v_ref, g_ref, d_ref, nm_ref, nv_ref):
        g = _sum_devices(p_ref)
        g_ref[...] = g
        d_ref[...], nm_ref[...], nv_ref[...] = _adamw(w_ref[...], g, m_ref[...], v_ref[...])

    blk = pl.BlockSpec((tr, cols), lambda i: (i, 0))
    return pl.pallas_call(
        body, name=name, grid=(rows // tr,),
        in_specs=[pl.BlockSpec((N_DEV, tr, cols), lambda i: (0, i, 0)), blk, blk, blk],
        out_specs=[blk] * 4, out_shape=[_sds((rows, cols), F32)] * 4,
        compiler_params=_params(("parallel",)),
    )(parts, w, m, v)


_VEC_ORDER = ("ln_in_g", "ln_in_b", "conv_b", "conv_ln_g", "conv_ln_b", "gate_bias", "gla_norm_g",
              "ln1_g", "ln1_b", "ln2_g", "ln2_b")
_SHARDED_SMALL = (("meta_tokens", 0, N_META, LANES), ("conv_w", N_META, CONV_WIDTH, None), ("gate_up", N_META + 32, GLA_RANK, None))


def _update_small(parts_sh, parts_vec, wmv):
    names = [s[0] for s in _SHARDED_SMALL] + list(_VEC_ORDER)
    flat = [a for nme in names for a in wmv[nme]]
    nv = len(_VEC_ORDER)

    def body(*refs):
        sh_ref, vec_ref = refs[0], refs[1]
        ins = refs[2:2 + len(flat)]
        outs = refs[2 + len(flat):2 + len(flat) + 4 * len(names)]
        loss_ref = refs[2 + len(flat) + 4 * len(names)]
        gsh_ref, gvec_ref = refs[-2:]
        gsh_ref[...] = _sum_devices(sh_ref)
        gvec_ref[...] = _sum_devices(vec_ref)
        loss_ref[...] = gvec_ref[nv:nv + 1, :]
        for idx, nme in enumerate(names):
            w_ref, m_ref, v_ref = ins[3 * idx:3 * idx + 3]
            rows, cols = w_ref.shape
            if idx < len(_SHARDED_SMALL):
                r0 = _SHARDED_SMALL[idx][1]
                g = gsh_ref[r0:r0 + rows, 0:cols]
            else:
                j = idx - len(_SHARDED_SMALL)
                g = gvec_ref[j:j + 1, 0:cols]
            o = outs[4 * idx:4 * idx + 4]
            o[0][...] = g
            o[1][...], o[2][...], o[3][...] = _adamw(w_ref[...], g, m_ref[...], v_ref[...])

    out_shape = [_sds(wmv[nme][0].shape, F32) for nme in names for _ in range(4)] + [_sds((1, parts_vec.shape[2]), F32)]
    vmem = pl.BlockSpec(memory_space=pltpu.VMEM)
    res = pl.pallas_call(
        body, name="update_small", out_shape=out_shape,
        in_specs=[vmem] * (2 + len(flat)), out_specs=[vmem] * len(out_shape),
        scratch_shapes=[pltpu.VMEM(parts_sh.shape[1:], F32), pltpu.VMEM(parts_vec.shape[1:], F32)],
    )(parts_sh, parts_vec, *flat)
    return {nme: res[4 * i:4 * i + 4] for i, nme in enumerate(names)}, res[-1][0, 0]


_WEIGHTS = ("meta_tokens", "ln_in_g", "ln_in_b", "w_in", "conv_w", "conv_b", "conv_ln_g", "conv_ln_b", "gate_up",
            "gate_bias", "gla_norm_g", "w_out", "ln1_g", "ln1_b", "w_ff1", "w_ff2", "ln2_g", "ln2_b")


def kernel(x, meta_tokens, ln_in_g, ln_in_b, w_in, conv_w, conv_b, conv_ln_g, conv_ln_b, gate_up, gate_bias, gla_norm_g, w_out, ln1_g, ln1_b, w_ff1, w_ff2, ln2_g, ln2_b, loss_target, m_meta_tokens, m_ln_in_g, m_ln_in_b, m_w_in, m_conv_w, m_conv_b, m_conv_ln_g, m_conv_ln_b, m_gate_up, m_gate_bias, m_gla_norm_g, m_w_out, m_ln1_g, m_ln1_b, m_w_ff1, m_w_ff2, m_ln2_g, m_ln2_b, v_meta_tokens, v_ln_in_g, v_ln_in_b, v_w_in, v_conv_w, v_conv_b, v_conv_ln_g, v_conv_ln_b, v_gate_up, v_gate_bias, v_gla_norm_g, v_w_out, v_ln1_g, v_ln1_b, v_w_ff1, v_w_ff2, v_ln2_g, v_ln2_b):
    w = dict(meta_tokens=meta_tokens, ln_in_g=ln_in_g, ln_in_b=ln_in_b, w_in=w_in, conv_w=conv_w, conv_b=conv_b,
             conv_ln_g=conv_ln_g, conv_ln_b=conv_ln_b, gate_up=gate_up, gate_bias=gate_bias, gla_norm_g=gla_norm_g,
             w_out=w_out, ln1_g=ln1_g, ln1_b=ln1_b, w_ff1=w_ff1, w_ff2=w_ff2, ln2_g=ln2_g, ln2_b=ln2_b)
    mom = dict(meta_tokens=m_meta_tokens, ln_in_g=m_ln_in_g, ln_in_b=m_ln_in_b, w_in=m_w_in, conv_w=m_conv_w,
               conv_b=m_conv_b, conv_ln_g=m_conv_ln_g, conv_ln_b=m_conv_ln_b, gate_up=m_gate_up, gate_bias=m_gate_bias,
               gla_norm_g=m_gla_norm_g, w_out=m_w_out, ln1_g=m_ln1_g, ln1_b=m_ln1_b, w_ff1=m_w_ff1, w_ff2=m_w_ff2,
               ln2_g=m_ln2_g, ln2_b=m_ln2_b)
    var = dict(meta_tokens=v_meta_tokens, ln_in_g=v_ln_in_g, ln_in_b=v_ln_in_b, w_in=v_w_in, conv_w=v_conv_w,
               conv_b=v_conv_b, conv_ln_g=v_conv_ln_g, conv_ln_b=v_conv_ln_b, gate_up=v_gate_up, gate_bias=v_gate_bias,
               gla_norm_g=v_gla_norm_g, w_out=v_w_out, ln1_g=v_ln1_g, ln1_b=v_ln1_b, w_ff1=v_w_ff1, w_ff2=v_w_ff2,
               ln2_g=v_ln2_g, ln2_b=v_ln2_b)
    shapes = {k: a.shape for k, a in w.items()}

    def two_d(a):
        return a.reshape(1, -1) if a.ndim == 1 else a.reshape(a.shape[-2:])

    w2d = {k: two_d(a) for k, a in w.items()}
    m2d = {k: two_d(a) for k, a in mom.items()}
    v2d = {k: two_d(a) for k, a in var.items()}
    d = x.shape[-1]
    d_in = w2d["w_in"].shape[1] * N_DEV
    d_in_p = -(-d_in // LANES) * LANES

    g_in, g_meta, g_conv, g_gup = _sc_gather(
        [w2d["w_in"].astype(BF16), w2d["meta_tokens"], w2d["conv_w"], w2d["gate_up"]], "gather_first", 0)
    g_out, g_ff1, g_ff2 = _sc_gather(
        [w2d["w_out"].astype(BF16), w2d["w_ff1"].astype(BF16), w2d["w_ff2"].astype(BF16)], "gather_late", 1, after=g_gup)
    w_in_full = jnp.pad(g_in.transpose(1, 0, 2).reshape(d, d_in), ((0, 0), (0, d_in_p - d_in)))
    meta_full = g_meta.transpose(1, 0, 2).reshape(N_META, d)
    conv_w_full = g_conv.transpose(1, 0, 2).reshape(CONV_WIDTH, -1)
    gate_up_full = g_gup.transpose(1, 0, 2).reshape(GLA_RANK, -1)

    def late_weights(after):
        return g_out.reshape(-1, d), g_ff1, g_ff2.reshape(-1, d)

    pushed = {}

    def push(tag, grads):
        if tag == "ff":
            pushed["ff1"], pushed["ff2"] = _sc_exchange(list(grads), [True, True], "scatter_ff", 2)
        elif tag == "out":
            pushed["p_out"] = grads[0].reshape(N_DEV, -1, d)
        else:
            p_in = grads[0][:, :d_in].reshape(d, N_DEV, d_in // N_DEV).transpose(1, 0, 2).astype(BF16)
            pushed["in"], pushed["out"] = _sc_exchange([p_in, pushed["p_out"]], [True, True], "scatter_rest", 3,
                                                       after=pushed["ff1"])
        return grads[0]

    res = _local_step(x, loss_target, meta_full, w2d["ln_in_g"], w2d["ln_in_b"], w_in_full, conv_w_full, w2d["conv_b"],
                      w2d["conv_ln_g"], w2d["conv_ln_b"], gate_up_full, w2d["gate_bias"], w2d["gla_norm_g"], late_weights,
                      w2d["ln1_g"], w2d["ln1_b"], w2d["ln2_g"], w2d["ln2_b"], push)

    dc = res["conv_w"].shape[1]
    hk = res["gate_up"].shape[1]
    sh_meta = res["meta_tokens"].reshape(N_META, N_DEV, LANES).transpose(1, 0, 2)
    sh_conv = jnp.pad(res["conv_w"].reshape(CONV_WIDTH, N_DEV, dc // N_DEV).transpose(1, 0, 2),
                      ((0, 0), (0, 32 - CONV_WIDTH), (0, LANES - dc // N_DEV)))
    sh_gup = jnp.pad(res["gate_up"].reshape(GLA_RANK, N_DEV, hk // N_DEV).transpose(1, 0, 2),
                     ((0, 0), (0, 0), (0, LANES - hk // N_DEV)))
    p_sh = jnp.concatenate([sh_meta, sh_conv, sh_gup], axis=1)
    p_vec = jnp.concatenate([jnp.pad(res[k], ((0, 0), (0, d - res[k].shape[1]))) for k in _VEC_ORDER]
                            + [jnp.full((1, d), res["loss"], F32), jnp.zeros((15 - len(_VEC_ORDER), d), F32)], axis=0)

    r_sh, r_vec = _exchange([p_sh, p_vec], [True, False], "scatter_small")
    r_ff1, r_ff2, r_out, r_in = pushed["ff1"], pushed["ff2"], pushed["out"], pushed["in"]

    upd = {}
    upd["w_in"] = _update_big(r_in, w2d["w_in"], m2d["w_in"], v2d["w_in"], "update_w_in")
    upd["w_out"] = _update_big(r_out, w2d["w_out"], m2d["w_out"], v2d["w_out"], "update_w_out")
    upd["w_ff1"] = _update_big(r_ff1, w2d["w_ff1"], m2d["w_ff1"], v2d["w_ff1"], "update_w_ff1")
    upd["w_ff2"] = _update_big(r_ff2, w2d["w_ff2"], m2d["w_ff2"], v2d["w_ff2"], "update_w_ff2")
    small = [s[0] for s in _SHARDED_SMALL] + list(_VEC_ORDER)
    upd_small, loss = _update_small(r_sh, r_vec, {k: (w2d[k], m2d[k], v2d[k]) for k in small})
    upd.update(upd_small)

    outs = [loss, res["grad_x"]]
    for j in range(4):
        outs += [upd[k][j].reshape(shapes[k]) for k in _WEIGHTS]
    return tuple(outs)
```

```python
import functools

import jax
import jax.numpy as jnp
from jax import lax
from jax.experimental import pallas as pl
from jax.experimental.pallas import tpu as pltpu
from jax.experimental.pallas import tpu_sc as plsc

F32 = jnp.float32
BF16 = jnp.bfloat16

N_META = 16
CHUNK = 64
PAD_FRONT = (-N_META) % CHUNK
X_OFF = PAD_FRONT + N_META
CONV_WIDTH = 31
CONV_HALO = 32
CONV_SUB = 64
CONV_WIN = CONV_SUB + CONV_HALO
GLA_HEADS = 4
GLA_DK = 64
GLA_DV = 128
GLA_RANK = 16
GLA_TAU = 16.0
QK_SCALE = GLA_DK ** -0.5
LN_EPS = 1e-5
ALPHA = 2.0 ** 0.25
LANES = 128
N_DEV = 8
ADAM_LR = 0.001
ADAM_B1 = 0.9
ADAM_B2 = 0.999
ADAM_EPS = 1e-08
ADAM_WD = 0.01
ADAM_STEP = 10
VMEM_LIMIT = 56 * 1024 * 1024
MESH_AXES = ("x", "y", "c")


def _sds(shape, dtype):
    return jax.ShapeDtypeStruct(shape, dtype)


def _mm(a, b):
    return jnp.dot(a, b, preferred_element_type=F32)


def _mm_nt(a, b):
    return lax.dot_general(a, b, (((1,), (1,)), ((), ())), preferred_element_type=F32)


def _mm_tn(a, b):
    return lax.dot_general(a, b, (((0,), (0,)), ((), ())), preferred_element_type=F32)


def _sigmoid(x):
    return 1.0 / (1.0 + jnp.exp(-x))


def _log_sigmoid(z):
    return jnp.minimum(z, 0.0) - jnp.log(1.0 + jnp.exp(-jnp.abs(z)))


def _ln(x):
    mu = jnp.mean(x, axis=-1, keepdims=True)
    xc = x - mu
    var = jnp.mean(xc * xc, axis=-1, keepdims=True)
    rstd = lax.rsqrt(var + LN_EPS)
    return xc * rstd, rstd


def _ln_bwd(dyg, xhat, rstd):
    m1 = jnp.mean(dyg, axis=-1, keepdims=True)
    m2 = jnp.mean(dyg * xhat, axis=-1, keepdims=True)
    return rstd * (dyg - m1 - xhat * m2)


def _rowsum(x):
    return jnp.sum(x, axis=0, keepdims=True)


def _row_in_seq(i, tm, tp):
    base = lax.rem(i * tm, tp)
    return base + lax.broadcasted_iota(jnp.int32, (tm, 1), 0)


def _split3(x):
    hi = x.astype(BF16)
    r1 = x - hi.astype(F32)
    mid = r1.astype(BF16)
    lo = (r1 - mid.astype(F32)).astype(BF16)
    return hi, mid, lo


def _tri_mm(tri, x):
    hi, mid, lo = _split3(x)
    return _mm(tri, hi) + _mm(tri, mid) + _mm(tri, lo)


def _params(sem):
    return pltpu.CompilerParams(dimension_semantics=sem, vmem_limit_bytes=VMEM_LIMIT)


def _pick_tile(n, prefs):
    for t in prefs:
        if n % t == 0:
            return t
    raise ValueError(f"no tile for {n}")


def _x_tile_row(tp, seq, tx):
    tps = seq // tx
    return lambda i: pl.multiple_of((i // tps) * tp + X_OFF + (i % tps) * tx, CHUNK)


def _inproj_fwd_x(x2, g, b, w_in, tp, seq, tx):
    rx, d = x2.shape
    n = w_in.shape[1]
    r = rx // seq * tp
    row = _x_tile_row(tp, seq, tx)

    def body(x_ref, g_ref, b_ref, w_ref, s0_ref, u_ref):
        xhat, _ = _ln(x_ref[...])
        s = xhat * g_ref[...] + b_ref[...]
        s0_ref[...] = s
        u_ref[...] = _mm(s.astype(BF16), w_ref[...])

    return pl.pallas_call(
        body, name="inproj_fwd_x", grid=(rx // tx,),
        in_specs=[pl.BlockSpec((tx, d), lambda i: (i, 0)), pl.BlockSpec((1, d), lambda i: (0, 0)),
                  pl.BlockSpec((1, d), lambda i: (0, 0)), pl.BlockSpec((d, n), lambda i: (0, 0))],
        out_specs=[pl.BlockSpec((pl.Element(tx), pl.Element(d)), lambda i: (row(i), 0)),
                   pl.BlockSpec((pl.Element(tx), pl.Element(n)), lambda i: (row(i), 0))],
        out_shape=[_sds((r, d), F32), _sds((r, n), F32)],
        compiler_params=_params(("parallel",)),
    )(x2, g, b, w_in)


def _inproj_fwd_head(head, g, b, w_in, s0, u, tp):
    r, d = s0.shape
    n = w_in.shape[1]
    nb = tp // X_OFF

    def body(h_ref, g_ref, b_ref, w_ref, s0_in, u_in, s0_ref, u_ref):
        xhat, _ = _ln(h_ref[...])
        real = lax.broadcasted_iota(jnp.int32, (X_OFF, 1), 0) >= PAD_FRONT
        s = jnp.where(real, xhat * g_ref[...] + b_ref[...], 0.0)
        s0_ref[...] = s
        u_ref[...] = _mm(s.astype(BF16), w_ref[...])

    anyspec = pl.BlockSpec(memory_space=pl.ANY)
    return pl.pallas_call(
        body, name="inproj_fwd_head", grid=(r // tp,),
        in_specs=[pl.BlockSpec((X_OFF, d), lambda i: (0, 0)), pl.BlockSpec((1, d), lambda i: (0, 0)),
                  pl.BlockSpec((1, d), lambda i: (0, 0)), pl.BlockSpec((d, n), lambda i: (0, 0)), anyspec, anyspec],
        out_specs=[pl.BlockSpec((X_OFF, d), lambda i: (i * nb, 0)), pl.BlockSpec((X_OFF, n), lambda i: (i * nb, 0))],
        out_shape=[_sds((r, d), F32), _sds((r, n), F32)],
        input_output_aliases={4: 0, 5: 1},
        compiler_params=_params(("parallel",)),
    )(head, g, b, w_in, s0, u)


def _conv_taps(win, coef, lo):
    acc = None
    for rho in range(8):
        offs = [o for o in range(lo, lo + CONV_WIDTH) if o % 8 == rho]
        if not offs:
            continue
        rolled = win if rho == 0 else pltpu.roll(win, CONV_WIN - rho, 0)
        for o in offs:
            m8 = o - rho
            term = rolled[m8:m8 + CONV_SUB, :] * coef(o)
            acc = term if acc is None else acc + term
    return acc


def _conv_fwd(u, w32, cb, cg, cbe, tp, tc, dc):
    r = u.shape[0]
    hb = tc // CONV_HALO

    def body(a_ref, g_ref, ah_ref, gh_ref, w_ref, cb_ref, cg_ref, cbe_ref, c_ref, co_ref, hs_ref):
        t = pl.program_id(0)
        first = lax.rem(t * tc, tp) == 0
        hh = ah_ref[...] * _sigmoid(gh_ref[...])
        hs_ref[0:CONV_HALO, :] = jnp.where(first, 0.0, hh)
        hs_ref[CONV_HALO:CONV_HALO + tc, :] = a_ref[...] * _sigmoid(g_ref[...])

        def sub(k, carry):
            r0 = pl.multiple_of(k * CONV_SUB, CONV_SUB)
            win = hs_ref[pl.ds(r0, CONV_WIN), :]
            c = _conv_taps(win, lambda o: w_ref[o - 2:o - 1, :], 2) + cb_ref[...]
            c_ref[pl.ds(r0, CONV_SUB), :] = c
            xhat, _ = _ln(c)
            cn = xhat * cg_ref[...] + cbe_ref[...]
            co_ref[pl.ds(r0, CONV_SUB), :] = (cn * _sigmoid(cn)).astype(BF16)
            return carry

        lax.fori_loop(0, tc // CONV_SUB, sub, 0)

    vec = pl.BlockSpec((1, dc), lambda t: (0, 0))
    return pl.pallas_call(
        body, name="conv_fwd", grid=(r // tc,),
        in_specs=[pl.BlockSpec((tc, dc), lambda t: (t, 0)), pl.BlockSpec((tc, dc), lambda t: (t, 1)),
                  pl.BlockSpec((CONV_HALO, dc), lambda t: (jnp.maximum(t * hb - 1, 0), 0)),
                  pl.BlockSpec((CONV_HALO, dc), lambda t: (jnp.maximum(t * hb - 1, 0), 1)),
                  pl.BlockSpec((32, dc), lambda t: (0, 0)), vec, vec, vec],
        out_specs=[pl.BlockSpec((tc, dc), lambda t: (t, 0)), pl.BlockSpec((tc, dc), lambda t: (t, 0))],
        out_shape=[_sds((r, dc), F32), _sds((r, dc), BF16)],
        scratch_shapes=[pltpu.VMEM((CONV_HALO + tc, dc), F32)],
        compiler_params=_params(("parallel",)),
    )(u, u, u, u, w32, cb, cg, cbe)


def _tri_mm_all(tri, xs):
    parts = [_split3(x) for x in xs]
    acc = [None] * len(xs)
    for t in range(3):
        for j in range(len(xs)):
            term = _mm(tri, parts[j][t])
            acc[j] = term if t == 0 else acc[j] + term
    return acc


def _gla_prep(qk_ref, gd_ref, gup, gb, n0, kc):
    rows = [slice(j * CHUNK, (j + 1) * CHUNK) for j in range(kc)]
    ri = lax.broadcasted_iota(jnp.int32, (CHUNK, CHUNK), 0)
    ci = lax.broadcasted_iota(jnp.int32, (CHUNK, CHUNK), 1)
    low = (ri >= ci).astype(BF16)
    hk = GLA_HEADS * GLA_DK
    gds = [gd_ref[rw, :] for rw in rows]
    zs = [_mm(g.astype(BF16), gup) + gb for g in gds]
    reals = [(n0 + j) * CHUNK + lax.broadcasted_iota(jnp.int32, (CHUNK, 1), 0) >= PAD_FRONT for j in range(kc)]
    lgs = [jnp.where(reals[j], _log_sigmoid(zs[j]) * (1.0 / GLA_TAU), 0.0) for j in range(kc)]
    bs = _tri_mm_all(low, lgs)
    out = []
    for j in range(kc):
        b, bl = bs[j], _rowsum(lgs[j])
        q = qk_ref[rows[j], :hk] * QK_SCALE
        k = qk_ref[rows[j], hk:]
        eb, enb, ebl = jnp.exp(b), jnp.exp(-b), jnp.exp(bl - b)
        out.append(dict(rows=rows[j], gd=gds[j], z=zs[j], real=reals[j], eb=eb, enb=enb, ebl=ebl, gam=jnp.exp(bl),
                        qe=q * eb, ke=k * enb, kd=k * ebl))
    return out, ri, ci


def _gla_heads(p, v_ref):
    ops = []
    for h in range(GLA_HEADS):
        hp, h2 = divmod(h, 2)
        ls = slice(hp * LANES, (hp + 1) * LANES)
        m = _head_mask(h2)
        ops.append(dict(ls=ls, m=m, vs=slice(h * GLA_DV, (h + 1) * GLA_DV),
                        qe=jnp.where(m, p["qe"][:, ls], 0.0).astype(BF16),
                        kd=jnp.where(m, p["kd"][:, ls], 0.0).astype(BF16),
                        ke=p["ke"][:, ls].astype(BF16),
                        v=v_ref[p["rows"], h * GLA_DV:(h + 1) * GLA_DV].astype(BF16)))
    return ops


def _head_mask(h2):
    lane = lax.broadcasted_iota(jnp.int32, (1, LANES), 1)
    return (lane < GLA_DK) if h2 == 0 else (lane >= GLA_DK)


def _gla_fwd(u, gup, gb, gn, bsz, nc, kc):
    r = u.shape[0]
    hv = GLA_HEADS * GLA_DV
    ns = nc // kc

    def body(qk_ref, v_ref, r_ref, gd_ref, gup_ref, gb_ref, gn_ref, go_ref, sta_ref, st_ref):
        t = pl.program_id(1)

        @pl.when(t == 0)
        def _():
            st_ref[...] = jnp.zeros_like(st_ref)

        ps, ri, ci = _gla_prep(qk_ref, gd_ref, gup_ref[...], gb_ref[...], t * kc, kc)
        tril = ri >= ci
        items = [(j, h) for j in range(kc) for h in range(GLA_HEADS)]
        ops = [_gla_heads(p, v_ref) for p in ps]
        a = {jh: jnp.where(tril, _mm_nt(ops[jh[0]][jh[1]]["qe"], ops[jh[0]][jh[1]]["ke"]), 0.0).astype(BF16) for jh in items}
        oi = {jh: _mm(a[jh], ops[jh[0]][jh[1]]["v"]) for jh in items}
        inc = {jh: _mm_tn(ops[jh[0]][jh[1]]["v"], ops[jh[0]][jh[1]]["kd"]) for jh in items}
        sts = [st_ref[h] for h in range(GLA_HEADS)]
        for j, h in items:
            op, p = ops[j][h], ps[j]
            st = sts[h]
            sta_ref[j, h] = st
            o = oi[j, h] + _mm_nt(op["qe"], st.astype(BF16))
            sts[h] = st * p["gam"][:, op["ls"]] + inc[j, h]
            rs = lax.rsqrt(jnp.mean(o * o, axis=-1, keepdims=True) + LN_EPS)
            rr = r_ref[p["rows"], op["vs"]]
            go_ref[p["rows"], op["vs"]] = (o * rs * gn_ref[...] * (rr * _sigmoid(rr))).astype(BF16)
        for h in range(GLA_HEADS):
            st_ref[h] = sts[h]

    rowblk = lambda col: (lambda b, t: (b * ns + t, col))
    const = lambda b, t: (0, 0)
    return pl.pallas_call(
        body, name="gla_fwd", grid=(bsz, ns),
        in_specs=[pl.BlockSpec((kc * CHUNK, 512), rowblk(2)), pl.BlockSpec((kc * CHUNK, hv), rowblk(3)),
                  pl.BlockSpec((kc * CHUNK, hv), rowblk(4)), pl.BlockSpec((kc * CHUNK, LANES), rowblk(20)),
                  pl.BlockSpec((LANES, 256), const), pl.BlockSpec((1, 256), const), pl.BlockSpec((1, GLA_DV), const)],
        out_specs=[pl.BlockSpec((kc * CHUNK, hv), rowblk(0)),
                   pl.BlockSpec((kc, GLA_HEADS, LANES, LANES), lambda b, t: (b * ns + t, 0, 0, 0))],
        out_shape=[_sds((r, hv), BF16), _sds((bsz * nc, GLA_HEADS, LANES, LANES), F32)],
        scratch_shapes=[pltpu.VMEM((GLA_HEADS, LANES, LANES), F32)],
        compiler_params=_params(("parallel", "arbitrary")),
    )(u, u, u, u, gup, gb, gn)


def _outproj_fwd(s0, co, go, w_out, g1, b1, tm):
    r, d = s0.shape
    dc = co.shape[1]

    def body(s0_ref, co_ref, go_ref, w_ref, g_ref, b_ref, p1_ref, s1_ref, s1b_ref):
        mix = _mm(co_ref[...], w_ref[0:dc, :]) + _mm(go_ref[...], w_ref[dc:2 * dc, :])
        p1 = ALPHA * s0_ref[...] + mix
        p1_ref[...] = p1
        xhat, _ = _ln(p1)
        s1 = xhat * g_ref[...] + b_ref[...]
        s1_ref[...] = s1
        s1b_ref[...] = s1.astype(BF16)

    row = lambda w: pl.BlockSpec((tm, w), lambda i: (i, 0))
    vec = pl.BlockSpec((1, d), lambda i: (0, 0))
    return pl.pallas_call(
        body, name="outproj_fwd", grid=(r // tm,),
        in_specs=[row(d), row(dc), row(dc), pl.BlockSpec((2 * dc, d), lambda i: (0, 0)), vec, vec],
        out_specs=[row(d), row(d), row(d)],
        out_shape=[_sds((r, d), F32), _sds((r, d), F32), _sds((r, d), BF16)],
        compiler_params=_params(("parallel",)),
    )(s0, co, go, w_out, g1, b1)


def _mlp_fwd(s1, s1b, w1g, w2, g2, b2, tgt, tp, tm, ns):
    r, d = s1.shape
    nh, _, th = w1g.shape
    nj = nh // ns

    def body(s1_ref, sb_ref, w1_ref, w2_ref, g_ref, b_ref, t_ref, hm_ref, dp2_ref, dpb_ref, loss_ref, dg_ref, db_ref, acc_ref):
        i = pl.program_id(0)
        j = pl.program_id(1)

        @pl.when(jnp.logical_and(i == 0, j == 0))
        def _():
            loss_ref[...] = jnp.zeros_like(loss_ref)
            dg_ref[...] = jnp.zeros_like(dg_ref)
            db_ref[...] = jnp.zeros_like(db_ref)

        @pl.when(j == 0)
        def _():
            acc_ref[...] = jnp.zeros_like(acc_ref)

        hs = [_mm(sb_ref[...], w1_ref[s]) for s in range(ns)]
        acc = acc_ref[...]
        for s in range(ns):
            hm_ref[:, s * th:(s + 1) * th] = hs[s].astype(BF16)
            act = jnp.square(jnp.maximum(hs[s], 0.0))
            acc = acc + _mm(act.astype(BF16), w2_ref[s * th:(s + 1) * th, :])
        acc_ref[...] = acc

        @pl.when(j == nj - 1)
        def _():
            p2 = ALPHA * s1_ref[...] + acc_ref[...]
            xhat, rstd = _ln(p2)
            s2 = xhat * g_ref[...] + b_ref[...]
            isx = _row_in_seq(i, tm, tp) >= X_OFF
            tg = t_ref[...]
            tg = jnp.where(i == 0, pltpu.roll(tg, X_OFF, 0), tg)
            err = jnp.where(isx, s2 - tg, 0.0)
            loss_ref[...] += 0.5 * jnp.sum(jnp.mean(err * err, axis=-1, keepdims=True))
            dy = err * (1.0 / d)
            dg_ref[...] += _rowsum(dy * xhat)
            db_ref[...] += _rowsum(dy)
            dp2 = _ln_bwd(dy * g_ref[...], xhat, rstd)
            dp2_ref[...] = dp2
            dpb_ref[...] = dp2.astype(BF16)

    row = pl.BlockSpec((tm, d), lambda i, j: (i, 0))
    vec = pl.BlockSpec((1, d), lambda i, j: (0, 0))
    tgt_row = pl.BlockSpec((pl.Element(tm), pl.Element(d)),
                           lambda i, j: (pl.multiple_of(jnp.maximum(i * tm - X_OFF * ((i * tm) // tp + 1), 0), CHUNK), 0))
    return pl.pallas_call(
        body, name="mlp_fwd", grid=(r // tm, nj),
        in_specs=[row, row, pl.BlockSpec((ns, d, th), lambda i, j: (j, 0, 0)), pl.BlockSpec((ns * th, d), lambda i, j: (j, 0)),
                  vec, vec, tgt_row],
        out_specs=[pl.BlockSpec((tm, ns * th), lambda i, j: (i, j)), row, row,
                   pl.BlockSpec((8, LANES), lambda i, j: (0, 0)), vec, vec],
        out_shape=[_sds((r, nh * th), BF16), _sds((r, d), F32), _sds((r, d), BF16), _sds((8, LANES), F32),
                   _sds((1, d), F32), _sds((1, d), F32)],
        scratch_shapes=[pltpu.VMEM((tm, d), F32)],
        compiler_params=_params(("arbitrary", "arbitrary")),
    )(s1, s1b, w1g, w2, g2, b2, tgt)


def _mlp_bwd_act(dp2, dpb, hm, w1g, w2, p1, g1, tm, ns):
    r, d = dp2.shape
    nh, _, th = w1g.shape
    nj = nh // ns

    def body(dp2_ref, dpb_ref, hm_ref, w1_ref, w2_ref, p1_ref, g_ref, dh_ref, dp1_ref, dg_ref, db_ref, acc_ref):
        i = pl.program_id(0)
        j = pl.program_id(1)

        @pl.when(jnp.logical_and(i == 0, j == 0))
        def _():
            dg_ref[...] = jnp.zeros_like(dg_ref)
            db_ref[...] = jnp.zeros_like(db_ref)

        @pl.when(j == 0)
        def _():
            acc_ref[...] = jnp.zeros_like(acc_ref)

        dacts = [_mm_nt(dpb_ref[...], w2_ref[s * th:(s + 1) * th, :]) for s in range(ns)]
        acc = acc_ref[...]
        for s in range(ns):
            cols = slice(s * th, (s + 1) * th)
            dh = (dacts[s] * (2.0 * jnp.maximum(hm_ref[:, cols].astype(F32), 0.0))).astype(BF16)
            dh_ref[:, cols] = dh
            acc = acc + _mm_nt(dh, w1_ref[s])
        acc_ref[...] = acc

        @pl.when(j == nj - 1)
        def _():
            ds1 = ALPHA * dp2_ref[...] + acc_ref[...]
            xhat, rstd = _ln(p1_ref[...])
            dg_ref[...] += _rowsum(ds1 * xhat)
            db_ref[...] += _rowsum(ds1)
            dp1_ref[...] = _ln_bwd(ds1 * g_ref[...], xhat, rstd)

    row = pl.BlockSpec((tm, d), lambda i, j: (i, 0))
    vec = pl.BlockSpec((1, d), lambda i, j: (0, 0))
    blk = pl.BlockSpec((tm, ns * th), lambda i, j: (i, j))
    return pl.pallas_call(
        body, name="mlp_bwd_act", grid=(r // tm, nj),
        in_specs=[row, row, blk, pl.BlockSpec((ns, d, th), lambda i, j: (j, 0, 0)),
                  pl.BlockSpec((ns * th, d), lambda i, j: (j, 0)), row, vec],
        out_specs=[blk, row, vec, vec],
        out_shape=[_sds((r, nh * th), BF16), _sds((r, d), F32), _sds((1, d), F32), _sds((1, d), F32)],
        scratch_shapes=[pltpu.VMEM((tm, d), F32)],
        compiler_params=_params(("arbitrary", "arbitrary")),
    )(dp2, dpb, hm, w1g, w2, p1, g1)


def _mlp_bwd_w(s1b, hm, dh, dpb, nh, tm, ns):
    r, d = s1b.shape
    th = hm.shape[1] // nh

    def body(s1_ref, hm_ref, dh_ref, dp2_ref, dw1_ref, dw2_ref, a1_ref, a2_ref):
        i = pl.program_id(1)

        @pl.when(i == 0)
        def _():
            a1_ref[...] = jnp.zeros_like(a1_ref)
            a2_ref[...] = jnp.zeros_like(a2_ref)

        for s in range(ns):
            a1_ref[s] += _mm_tn(s1_ref[...], dh_ref[:, s * th:(s + 1) * th])
        for s in range(ns):
            act = jnp.square(jnp.maximum(hm_ref[:, s * th:(s + 1) * th].astype(F32), 0.0)).astype(BF16)
            a2_ref[s] += _mm_tn(act, dp2_ref[...])

        @pl.when(i == pl.num_programs(1) - 1)
        def _():
            dw1_ref[...] = a1_ref[...].astype(BF16)
            dw2_ref[...] = a2_ref[...].astype(BF16)

    row = pl.BlockSpec((tm, d), lambda j, i: (i, 0))
    blk = pl.BlockSpec((tm, ns * th), lambda j, i: (i, j))
    return pl.pallas_call(
        body, name="mlp_bwd_w", grid=(nh // ns, r // tm),
        in_specs=[row, blk, blk, row],
        out_specs=[pl.BlockSpec((ns, d, th), lambda j, i: (j, 0, 0)), pl.BlockSpec((ns, th, d), lambda j, i: (j, 0, 0))],
        out_shape=[_sds((nh, d, th), BF16), _sds((nh, th, d), BF16)],
        scratch_shapes=[pltpu.VMEM((ns, d, th), F32), pltpu.VMEM((ns, th, d), F32)],
        compiler_params=_params(("parallel", "arbitrary")),
    )(s1b, hm, dh, dpb)


def _outproj_bwd(dp1, co, go, w_out, dep, tm):
    r, d = dp1.shape
    dc = co.shape[1]

    def body(dp_ref, co_ref, go_ref, w_ref, dep_ref, dmi_ref, dw_ref, acc_ref):
        i = pl.program_id(0)

        @pl.when(i == 0)
        def _():
            acc_ref[...] = jnp.zeros_like(acc_ref)

        dpb = dp_ref[...].astype(BF16)
        dmi_ref[...] = _mm_nt(dpb, w_ref[...])
        acc_ref[0:dc, :] += _mm_tn(co_ref[...], dpb)
        acc_ref[dc:2 * dc, :] += _mm_tn(go_ref[...], dpb)

        @pl.when(i == pl.num_programs(0) - 1)
        def _():
            dw_ref[...] = acc_ref[...].astype(BF16)

    row = lambda w: pl.BlockSpec((tm, w), lambda i: (i, 0))
    full = pl.BlockSpec((2 * dc, d), lambda i: (0, 0))
    return pl.pallas_call(
        body, name="outproj_bwd", grid=(r // tm,),
        in_specs=[row(d), row(dc), row(dc), full, pl.BlockSpec(memory_space=pl.ANY)],
        out_specs=[row(2 * dc), full],
        out_shape=[_sds((r, 2 * dc), F32), _sds((2 * dc, d), BF16)],
        scratch_shapes=[pltpu.VMEM((2 * dc, d), F32)],
        compiler_params=_params(("arbitrary",)),
    )(dp1, co, go, w_out, dep)


def _gla_bwd(u, dmi, sta, gup, gb, gn, dep, bsz, nc, kc):
    r = u.shape[0]
    hv = GLA_HEADS * GLA_DV
    hk = GLA_HEADS * GLA_DK
    ns = nc // kc

    def body(qk_ref, v_ref, r_ref, gd_ref, dgo_ref, sta_ref, gup_ref, gb_ref, gn_ref, dep_ref,
             dqk_ref, dv_ref, dr_ref, dgd_ref, dgn_ref, dgb_ref, dgup_ref, dst_ref):
        bi = pl.program_id(0)
        t = pl.program_id(1)

        @pl.when(jnp.logical_and(bi == 0, t == 0))
        def _():
            dgn_ref[...] = jnp.zeros_like(dgn_ref)
            dgb_ref[...] = jnp.zeros_like(dgb_ref)
            dgup_ref[...] = jnp.zeros_like(dgup_ref)

        @pl.when(t == 0)
        def _():
            dst_ref[...] = jnp.zeros_like(dst_ref)

        ps, ri, ci = _gla_prep(qk_ref, gd_ref, gup_ref[...], gb_ref[...], (ns - 1 - t) * kc, kc)
        tril = ri >= ci
        items = [(j, h) for j in reversed(range(kc)) for h in range(GLA_HEADS)]
        ops = [_gla_heads(p, v_ref) for p in ps]
        op = lambda jh: ops[jh[0]][jh[1]]
        st = {jh: sta_ref[jh[0], jh[1]] for jh in items}
        stb = {jh: st[jh].astype(BF16) for jh in items}
        a = {jh: jnp.where(tril, _mm_nt(op(jh)["qe"], op(jh)["ke"]), 0.0).astype(BF16) for jh in items}
        o1 = {jh: _mm(a[jh], op(jh)["v"]) for jh in items}
        o2 = {jh: _mm_nt(op(jh)["qe"], stb[jh]) for jh in items}
        dob = {}
        dgn = jnp.zeros((1, GLA_DV), F32)
        for jh in items:
            rows, vs = ps[jh[0]]["rows"], op(jh)["vs"]
            o = o1[jh] + o2[jh]
            rr = r_ref[rows, vs]
            sr = _sigmoid(rr)
            rs = lax.rsqrt(jnp.mean(o * o, axis=-1, keepdims=True) + LN_EPS)
            y = o * rs
            dgo = dgo_ref[rows, vs]
            don = dgo * (rr * sr)
            dr_ref[rows, vs] = dgo * (y * gn_ref[...]) * (sr * (1.0 + rr * (1.0 - sr)))
            dgn = dgn + _rowsum(don * y)
            dxn = don * gn_ref[...]
            dob[jh] = (rs * (dxn - y * jnp.mean(dxn * y, axis=-1, keepdims=True))).astype(BF16)
        da = {jh: jnp.where(tril, _mm_nt(dob[jh], op(jh)["v"]), 0.0).astype(BF16) for jh in items}
        dv1 = {jh: _mm_tn(a[jh], dob[jh]) for jh in items}
        dqe1 = {jh: _mm(da[jh], op(jh)["ke"]) for jh in items}
        dqe2 = {jh: _mm(dob[jh], stb[jh]) for jh in items}
        dke1 = {jh: _mm_tn(da[jh], op(jh)["qe"]) for jh in items}
        inc = {jh: _mm_tn(dob[jh], op(jh)["qe"]) for jh in items}
        dsts = [dst_ref[h] for h in range(GLA_HEADS)]
        dkd1, dgam1 = {}, {}
        for jh in items:
            j, h = jh
            dst = dsts[h]
            dstb = dst.astype(BF16)
            dv_ref[ps[j]["rows"], op(jh)["vs"]] = dv1[jh] + _mm_nt(op(jh)["kd"], dstb)
            dkd1[jh] = _mm(op(jh)["v"], dstb)
            dgam1[jh] = _rowsum(dst * st[jh])
            dsts[h] = dst * ps[j]["gam"][:, op(jh)["ls"]] + inc[jh]
        for h in range(GLA_HEADS):
            dst_ref[h] = dsts[h]
        upper = (ri <= ci).astype(BF16)
        dbs, dbls = [], []
        for j in range(kc):
            p = ps[j]
            tiles = [[op((j, 2 * hp + h2)) for h2 in range(2)] for hp in range(GLA_HEADS // 2)]
            head = lambda d, hp, h2: d[j, 2 * hp + h2]
            lanes = lambda f: jnp.concatenate([f(hp) for hp in range(GLA_HEADS // 2)], axis=1)
            dqe = lanes(lambda hp: sum(jnp.where(tiles[hp][h2]["m"], head(dqe1, hp, h2) + head(dqe2, hp, h2), 0.0)
                                       for h2 in range(2)))
            dke = lanes(lambda hp: head(dke1, hp, 0) + head(dke1, hp, 1))
            dkd = lanes(lambda hp: sum(jnp.where(tiles[hp][h2]["m"], head(dkd1, hp, h2), 0.0) for h2 in range(2)))
            dgam = lanes(lambda hp: head(dgam1, hp, 0) + head(dgam1, hp, 1))
            dqk_ref[p["rows"], :hk] = dqe * p["eb"] * QK_SCALE
            dqk_ref[p["rows"], hk:] = dke * p["enb"] + dkd * p["ebl"]
            dkdkd = dkd * p["kd"]
            dbs.append(dqe * p["qe"] - dke * p["ke"] - dkdkd)
            dbls.append(_rowsum(dkdkd) + dgam * p["gam"])
        dlgs = _tri_mm_all(upper, dbs)
        dzb = []
        dgb = jnp.zeros((1, hk), F32)
        for j in range(kc):
            p = ps[j]
            dz = jnp.where(p["real"], (dlgs[j] + dbls[j]) * (1.0 / GLA_TAU) * _sigmoid(-p["z"]), 0.0)
            dgb = dgb + _rowsum(dz)
            dzb.append(dz.astype(BF16))
        dgup = sum(_mm_tn(ps[j]["gd"].astype(BF16), dzb[j]) for j in range(kc))
        for j in range(kc):
            dgd_ref[ps[j]["rows"], :] = _mm_nt(dzb[j], gup_ref[...])
        dgb_ref[...] += dgb
        dgup_ref[...] += dgup
        dgn_ref[...] += dgn

    rowblk = lambda col: (lambda b, t: (b * ns + ns - 1 - t, col))
    const = lambda b, t: (0, 0)
    return pl.pallas_call(
        body, name="gla_bwd", grid=(bsz, ns),
        in_specs=[pl.BlockSpec((kc * CHUNK, 2 * hk), rowblk(2)), pl.BlockSpec((kc * CHUNK, hv), rowblk(3)),
                  pl.BlockSpec((kc * CHUNK, hv), rowblk(4)), pl.BlockSpec((kc * CHUNK, LANES), rowblk(20)),
                  pl.BlockSpec((kc * CHUNK, hv), rowblk(1)),
                  pl.BlockSpec((kc, GLA_HEADS, LANES, LANES), lambda b, t: (b * ns + ns - 1 - t, 0, 0, 0)),
                  pl.BlockSpec((LANES, 256), const), pl.BlockSpec((1, 256), const), pl.BlockSpec((1, GLA_DV), const),
                  pl.BlockSpec(memory_space=pl.ANY)],
        out_specs=[pl.BlockSpec((kc * CHUNK, 2 * hk), rowblk(0)), pl.BlockSpec((kc * CHUNK, hv), rowblk(0)),
                   pl.BlockSpec((kc * CHUNK, hv), rowblk(0)), pl.BlockSpec((kc * CHUNK, LANES), rowblk(0)),
                   pl.BlockSpec((1, GLA_DV), const), pl.BlockSpec((1, 256), const), pl.BlockSpec((LANES, 256), const)],
        out_shape=[_sds((r, 2 * hk), F32), _sds((r, hv), F32), _sds((r, hv), F32), _sds((r, LANES), F32),
                   _sds((1, GLA_DV), F32), _sds((1, 256), F32), _sds((LANES, 256), F32)],
        scratch_shapes=[pltpu.VMEM((GLA_HEADS, LANES, LANES), F32)],
        compiler_params=_params(("arbitrary", "arbitrary")),
    )(u, u, u, u, dmi, sta, gup, gb, gn, dep)


def _conv_bwd(u, c, dmi, w32, cg, cbe, tp, tc, dc):
    r = u.shape[0]
    hb = tc // CONV_HALO
    nhalo = r // CONV_HALO

    def dconv(cv, dco, cg_ref, cbe_ref):
        xhat, rstd = _ln(cv)
        cn = xhat * cg_ref[...] + cbe_ref[...]
        sg = _sigmoid(cn)
        dcn = dco * (sg * (1.0 + cn * (1.0 - sg)))
        return _ln_bwd(dcn * cg_ref[...], xhat, rstd), dcn, xhat

    def body(a_ref, g_ref, ah_ref, gh_ref, c_ref, dco_ref, ch_ref, dcoh_ref, w_ref, cg_ref, cbe_ref,
             du_ref, dw_ref, dcb_ref, dcg_ref, dcbe_ref, hs_ref, dcs_ref):
        t = pl.program_id(0)

        @pl.when(t == 0)
        def _():
            dw_ref[...] = jnp.zeros_like(dw_ref)
            dcb_ref[...] = jnp.zeros_like(dcb_ref)
            dcg_ref[...] = jnp.zeros_like(dcg_ref)
            dcbe_ref[...] = jnp.zeros_like(dcbe_ref)

        first = lax.rem(t * tc, tp) == 0
        last = lax.rem((t + 1) * tc, tp) == 0
        hh = ah_ref[...] * _sigmoid(gh_ref[...])
        hs_ref[0:CONV_HALO, :] = jnp.where(first, 0.0, hh)
        hs_ref[CONV_HALO:CONV_HALO + tc, :] = a_ref[...] * _sigmoid(g_ref[...])
        dch, _, _ = dconv(ch_ref[...], dcoh_ref[...], cg_ref, cbe_ref)
        dcs_ref[tc:tc + CONV_HALO, :] = jnp.where(last, 0.0, dch)

        def sub1(k, carry):
            r0 = pl.multiple_of(k * CONV_SUB, CONV_SUB)
            dcv, dcn, xhat = dconv(c_ref[pl.ds(r0, CONV_SUB), :], dco_ref[pl.ds(r0, CONV_SUB), :], cg_ref, cbe_ref)
            dcs_ref[pl.ds(r0, CONV_SUB), :] = dcv
            dcb_ref[...] += _rowsum(dcv)
            dcg_ref[...] += _rowsum(dcn * xhat)
            dcbe_ref[...] += _rowsum(dcn)
            return carry

        lax.fori_loop(0, tc // CONV_SUB, sub1, 0)

        def sub2(k, carry):
            r0 = pl.multiple_of(k * CONV_SUB, CONV_SUB)
            dwin = dcs_ref[pl.ds(r0, CONV_WIN), :]
            dh = _conv_taps(dwin, lambda o: w_ref[CONV_WIDTH - 1 - o:CONV_WIDTH - o, :], 0)
            av = a_ref[pl.ds(r0, CONV_SUB), :]
            sg = _sigmoid(g_ref[pl.ds(r0, CONV_SUB), :])
            du_ref[pl.ds(r0, CONV_SUB), 0:dc] = dh * sg
            du_ref[pl.ds(r0, CONV_SUB), dc:2 * dc] = dh * av * sg * (1.0 - sg)
            hwin = hs_ref[pl.ds(r0, CONV_WIN), :]
            dcv = dwin[0:CONV_SUB, :]
            for rho in range(8):
                offs = [o for o in range(2, 2 + CONV_WIDTH) if o % 8 == rho]
                rolled = hwin if rho == 0 else pltpu.roll(hwin, CONV_WIN - rho, 0)
                for o in offs:
                    m8 = o - rho
                    dw_ref[o - 2:o - 1, :] += _rowsum(dcv * rolled[m8:m8 + CONV_SUB, :])
            return carry

        lax.fori_loop(0, tc // CONV_SUB, sub2, 0)

    vec = pl.BlockSpec((1, dc), lambda t: (0, 0))
    prev = lambda col: (lambda t: (jnp.maximum(t * hb - 1, 0), col))
    nxt = lambda col: (lambda t: (jnp.minimum((t + 1) * hb, nhalo - 1), col))
    return pl.pallas_call(
        body, name="conv_bwd", grid=(r // tc,),
        in_specs=[pl.BlockSpec((tc, dc), lambda t: (t, 0)), pl.BlockSpec((tc, dc), lambda t: (t, 1)),
                  pl.BlockSpec((CONV_HALO, dc), prev(0)), pl.BlockSpec((CONV_HALO, dc), prev(1)),
                  pl.BlockSpec((tc, dc), lambda t: (t, 0)), pl.BlockSpec((tc, dc), lambda t: (t, 0)),
                  pl.BlockSpec((CONV_HALO, dc), nxt(0)), pl.BlockSpec((CONV_HALO, dc), nxt(0)),
                  pl.BlockSpec((32, dc), lambda t: (0, 0)), vec, vec],
        out_specs=[pl.BlockSpec((tc, 2 * dc), lambda t: (t, 0)), pl.BlockSpec((32, dc), lambda t: (0, 0)), vec, vec, vec],
        out_shape=[_sds((r, 2 * dc), F32), _sds((32, dc), F32), _sds((1, dc), F32), _sds((1, dc), F32), _sds((1, dc), F32)],
        scratch_shapes=[pltpu.VMEM((CONV_HALO + tc, dc), F32), pltpu.VMEM((tc + CONV_HALO, dc), F32)],
        compiler_params=_params(("arbitrary",)),
    )(u, u, u, u, c, dmi, c, dmi, w32, cg, cbe)


def _inproj_bwd(dp1, dus, xsrc, g_in, w_in, dep, tp, seq, tx):
    r, d = dp1.shape
    widths = [x.shape[1] for x in dus]
    offs = [sum(widths[:k]) for k in range(len(widths))]
    n = w_in.shape[1]
    nd = len(dus)
    head = tx == 0
    rows = X_OFF if head else tx

    def body(*refs):
        dp_ref = refs[0]
        du_refs = refs[1:1 + nd]
        x_ref, g_ref, w_ref, _, out_ref, dg_ref, db_ref = refs[1 + nd:]
        i = pl.program_id(0)

        @pl.when(i == 0)
        def _():
            dg_ref[...] = jnp.zeros_like(dg_ref)
            db_ref[...] = jnp.zeros_like(db_ref)
            if head:
                out_ref[...] = jnp.zeros_like(out_ref)

        ds0 = ALPHA * dp_ref[...]
        for k in range(nd):
            ds0 = ds0 + _mm_nt(du_refs[k][...].astype(BF16), w_ref[:, offs[k]:offs[k] + widths[k]])
        if head:
            ds0 = jnp.where(lax.broadcasted_iota(jnp.int32, (X_OFF, 1), 0) >= PAD_FRONT, ds0, 0.0)
        xhat, rstd = _ln(x_ref[...])
        dg_ref[...] += _rowsum(ds0 * xhat)
        db_ref[...] += _rowsum(ds0)
        dx = _ln_bwd(ds0 * g_ref[...], xhat, rstd)
        if head:
            out_ref[...] += dx[PAD_FRONT:X_OFF, :]
        else:
            out_ref[...] = dx

    if head:
        nb = tp // X_OFF
        row = lambda w: pl.BlockSpec((X_OFF, w), lambda i: (i * nb, 0))
        xspec = pl.BlockSpec((X_OFF, d), lambda i: (0, 0))
        ospec, oshape, steps = pl.BlockSpec((N_META, d), lambda i: (0, 0)), _sds((N_META, d), F32), r // tp
    else:
        start = _x_tile_row(tp, seq, tx)
        row = lambda w: pl.BlockSpec((pl.Element(tx), pl.Element(w)), lambda i: (start(i), 0))
        xspec = pl.BlockSpec((tx, d), lambda i: (i, 0))
        ospec, oshape, steps = xspec, _sds(xsrc.shape, F32), xsrc.shape[0] // tx
    vec = pl.BlockSpec((1, d), lambda i: (0, 0))
    return pl.pallas_call(
        body, name="inproj_bwd_head" if head else "inproj_bwd_x", grid=(steps,),
        in_specs=[row(d)] + [row(w) for w in widths] + [xspec, vec, pl.BlockSpec((d, n), lambda i: (0, 0)),
                                                        pl.BlockSpec(memory_space=pl.ANY)],
        out_specs=[ospec, vec, vec],
        out_shape=[oshape, _sds((1, d), F32), _sds((1, d), F32)],
        compiler_params=_params(("arbitrary",)),
    )(dp1, *dus, xsrc, g_in, w_in, dep)


def _inproj_bwd_w(s0, dus, tm):
    r, d = s0.shape
    widths = [x.shape[1] for x in dus]
    nd = len(dus)

    def body(*refs):
        s_ref = refs[0]
        du_refs = refs[1:1 + nd]
        dw_refs = refs[1 + nd:]
        i = pl.program_id(0)

        @pl.when(i == 0)
        def _():
            for k in range(nd):
                dw_refs[k][...] = jnp.zeros_like(dw_refs[k])

        sb = s_ref[...].astype(BF16)
        for k in range(nd):
            dw_refs[k][...] += _mm_tn(sb, du_refs[k][...].astype(BF16))

    row = lambda w: pl.BlockSpec((tm, w), lambda i: (i, 0))
    return pl.pallas_call(
        body, name="inproj_bwd_w", grid=(r // tm,),
        in_specs=[row(d)] + [row(w) for w in widths],
        out_specs=[pl.BlockSpec((d, w), lambda i: (0, 0)) for w in widths],
        out_shape=[_sds((d, w), F32) for w in widths],
        compiler_params=_params(("arbitrary",)),
    )(s0, *dus)


def _local_step(x, tgt, meta, ln_in_g, ln_in_b, w_in, conv_w, conv_b, conv_ln_g, conv_ln_b, gate_up, gate_bias,
                gla_norm_g, late_weights, ln1_g, ln1_b, ln2_g, ln2_b, push):
    bsz, seq, d = x.shape
    tp = X_OFF + seq
    assert tp % CHUNK == 0
    nc = tp // CHUNK
    r = bsz * tp
    dc = conv_b.shape[1]
    tm = _pick_tile(tp, (352, 128, 64))
    tc = _pick_tile(tp, (704, 128, 64))

    x2 = x.reshape(bsz * seq, d)
    head = jnp.pad(meta, ((PAD_FRONT, 0), (0, 0)))
    tx = _pick_tile(seq, (512, 64))
    tgt_p = tgt.reshape(bsz * seq, d)
    w32 = jnp.pad(conv_w, ((0, 32 - CONV_WIDTH), (0, 0)))
    gup = jnp.pad(gate_up, ((0, LANES - GLA_RANK), (0, 0))).astype(BF16)

    s0, u = _inproj_fwd_x(x2, ln_in_g, ln_in_b, w_in, tp, seq, tx)
    s0, u = _inproj_fwd_head(head, ln_in_g, ln_in_b, w_in, s0, u, tp)
    c, co = _conv_fwd(u, w32, conv_b, conv_ln_g, conv_ln_b, tp, tc, dc)
    kc = _pick_tile(nc, (3, 2, 1))
    go, sta = _gla_fwd(u, gup, gate_bias, gla_norm_g, bsz, nc, kc)
    w_out, w1g, w2 = late_weights(go)
    nh = w1g.shape[0]
    tmm = _pick_tile(tp, (704, 128, 64))
    ns = 2
    p1, s1, s1b = _outproj_fwd(s0, co, go, w_out, ln1_g, ln1_b, tmm)
    hm, dp2, dpb, loss, dg2, db2 = _mlp_fwd(s1, s1b, w1g, w2, ln2_g, ln2_b, tgt_p, tp, tmm, ns)

    dh, dp1, dg1, db1 = _mlp_bwd_act(dp2, dpb, hm, w1g, w2, p1, ln1_g, tmm, ns)
    dw1, dw2 = _mlp_bwd_w(s1b, hm, dh, dpb, nh, tmm, ns)
    tok = push("ff", (dw1, dw2))
    dmi, dwo = _outproj_bwd(dp1, co, go, w_out, tok, tmm)
    tok = push("out", (dwo,))
    dqk, dv, dr, dgd, dgn, dgb, dgup = _gla_bwd(u, dmi, sta, gup, gate_bias, gla_norm_g, tok, bsz, nc, kc)
    dcv, dcw, dcb, dcg, dcbe = _conv_bwd(u, c, dmi, w32, conv_ln_g, conv_ln_b, tp, tc, dc)
    dus = [dcv, dqk, dv, dr, dgd]
    dwi = _inproj_bwd_w(s0, dus, tm)
    tok = push("in", (jnp.concatenate(dwi, axis=1),))
    gx, dgx, dbx = _inproj_bwd(dp1, dus, x2, ln_in_g, w_in, tok, tp, seq, tx)
    dmeta, dgh, dbh = _inproj_bwd(dp1, dus, head, ln_in_g, w_in, tok, tp, seq, 0)

    return dict(loss=loss[0, 0], grad_x=gx.reshape(bsz, seq, d), meta_tokens=dmeta, ln_in_g=dgx + dgh, ln_in_b=dbx + dbh,
                conv_w=dcw[:CONV_WIDTH], conv_b=dcb, conv_ln_g=dcg, conv_ln_b=dcbe,
                gate_up=dgup[:GLA_RANK], gate_bias=dgb, gla_norm_g=dgn, ln1_g=dg1, ln1_b=db1, ln2_g=dg2, ln2_b=db2)


def _exchange(arrays, scatter, name):
    na = len(arrays)
    npeer = N_DEV - 1

    def body(*refs):
        srcs = refs[:na]
        outs = refs[na:2 * na]
        send_sems, recv_sems, local_sems = refs[2 * na:]
        xi, yi, ci = (lax.axis_index(a) for a in MESH_AXES)
        me = 4 * xi + 2 * yi + ci
        copies = []
        for a in range(na):
            own = srcs[a].at[me] if scatter[a] else srcs[a]
            cp = pltpu.make_async_copy(own, outs[a].at[me], local_sems.at[a])
            cp.start()
            copies.append(cp)
        remote = []
        for k in range(1, N_DEV):
            px, py, pc = xi ^ (k >> 2), yi ^ ((k >> 1) & 1), ci ^ (k & 1)
            peer = 4 * px + 2 * py + pc
            for a in range(na):
                src = srcs[a].at[peer] if scatter[a] else srcs[a]
                cp = pltpu.make_async_remote_copy(
                    src_ref=src, dst_ref=outs[a].at[me],
                    send_sem=send_sems.at[a * npeer + k - 1], recv_sem=recv_sems.at[a * npeer + k - 1],
                    device_id=(px, py, pc), device_id_type=pl.DeviceIdType.MESH)
                cp.start()
                remote.append(cp)
        for cp in remote:
            cp.wait()
        for cp in copies:
            cp.wait()

    out_shape = [_sds(a.shape if scatter[i] else (N_DEV,) + a.shape, a.dtype) for i, a in enumerate(arrays)]
    anyspec = pl.BlockSpec(memory_space=pl.ANY)
    return pl.pallas_call(
        body, name=name,
        in_specs=[anyspec] * na, out_specs=[anyspec] * na, out_shape=out_shape,
        scratch_shapes=[pltpu.SemaphoreType.DMA((na * npeer,)), pltpu.SemaphoreType.DMA((na * npeer,)),
                        pltpu.SemaphoreType.DMA((na,))],
    )(*arrays)


def _peers(xi, yi, ci):
    for k in range(1, N_DEV):
        px, py, pc = xi ^ (k >> 2), yi ^ ((k >> 1) & 1), ci ^ (k & 1)
        yield (px, py, pc), 4 * px + 2 * py + pc


def _sc_exchange(arrays, scatter, name, collective_id, after=None):
    na = len(arrays)
    npeer = N_DEV - 1
    ndep = 0 if after is None else 1

    def body(*refs):
        srcs = refs[:na]
        outs = refs[na + ndep:2 * na + ndep]
        send_sems, recv_sems, own_sems = refs[2 * na + ndep:]
        xi, yi, ci = (lax.axis_index(a) for a in MESH_AXES)
        me = 4 * xi + 2 * yi + ci
        barrier = pltpu.get_barrier_semaphore()
        for pos, _ in _peers(xi, yi, ci):
            pl.semaphore_signal(barrier, inc=1, device_id=pos, device_id_type=pl.DeviceIdType.MESH)
        pl.semaphore_wait(barrier, npeer)
        own = [pltpu.make_async_copy(srcs[a].at[me] if scatter[a] else srcs[a], outs[a].at[me], own_sems.at[a])
               for a in range(na)]
        for cp in own:
            cp.start()
        remote = []
        for a in range(na):
            for k, (pos, peer) in enumerate(_peers(xi, yi, ci)):
                cp = pltpu.make_async_remote_copy(
                    src_ref=srcs[a].at[peer] if scatter[a] else srcs[a], dst_ref=outs[a].at[me],
                    send_sem=send_sems.at[a * npeer + k], recv_sem=recv_sems.at[a * npeer + k],
                    device_id=pos, device_id_type=pl.DeviceIdType.MESH)
                cp.start()
                remote.append(cp)
        for cp in own:
            cp.wait()
        for cp in remote:
            cp.wait()

    out_type = [_sds(a.shape if scatter[i] else (N_DEV,) + a.shape, a.dtype) for i, a in enumerate(arrays)]
    sent = sum(a.size * a.dtype.itemsize // (N_DEV if scatter[i] else 1) for i, a in enumerate(arrays))
    return pl.kernel(
        body, out_type=out_type, mesh=plsc.ScalarSubcoreMesh(axis_name="seq", num_cores=1), name=name,
        scratch_types=[pltpu.SemaphoreType.DMA((na * npeer,)), pltpu.SemaphoreType.DMA((na * npeer,)),
                       pltpu.SemaphoreType.DMA((na,))],
        compiler_params=pltpu.CompilerParams(collective_id=collective_id),
        cost_estimate=pl.CostEstimate(flops=0, transcendentals=0, bytes_accessed=2 * N_DEV * sent,
                                      remote_bytes_transferred=npeer * sent),
    )(*arrays, *([] if after is None else [after]))


def _sc_gather(arrays, name, collective_id, after=None):
    na = len(arrays)
    ndep = 0 if after is None else 1
    npair = N_DEV - 1

    def body(*refs):
        srcs = refs[:na]
        outs = refs[na + ndep:2 * na + ndep]
        send_sems, recv_sems, own_sems = refs[2 * na + ndep:]
        xi, yi, ci = (lax.axis_index(a) for a in MESH_AXES)
        me = 4 * xi + 2 * yi + ci
        sibling = (xi, yi, 1 - ci)
        chips = [(1 - xi, yi), (xi, 1 - yi), (1 - xi, 1 - yi)]
        barrier = pltpu.get_barrier_semaphore()
        for pos, _ in _peers(xi, yi, ci):
            pl.semaphore_signal(barrier, inc=1, device_id=pos, device_id_type=pl.DeviceIdType.MESH)
        pl.semaphore_wait(barrier, npair)

        def copy(a, k, src, slot, to):
            return pltpu.make_async_remote_copy(
                src_ref=src, dst_ref=outs[a].at[slot], send_sem=send_sems.at[a * npair + k],
                recv_sem=recv_sems.at[a * npair + k], device_id=to, device_id_type=pl.DeviceIdType.MESH)

        own = [pltpu.make_async_copy(srcs[a], outs[a].at[me], own_sems.at[a]) for a in range(na)]
        for cp in own:
            cp.start()
        sent = []
        for a in range(na):
            sent.append(copy(a, 0, srcs[a], me, sibling))
            sent += [copy(a, 1 + j, srcs[a], me, (*chip, ci)) for j, chip in enumerate(chips)]
        for cp in sent:
            cp.start()
        for j, (cx, cy) in enumerate(chips):
            slot = 4 * cx + 2 * cy + ci
            for a in range(na):
                copy(a, 1 + j, srcs[a], slot, sibling).wait_recv()
                cp = copy(a, 4 + j, outs[a].at[slot], slot, sibling)
                cp.start()
                sent.append(cp)
        for a in range(na):
            copy(a, 0, srcs[a], me, sibling).wait_recv()
            for j in range(len(chips)):
                copy(a, 4 + j, srcs[a], me, sibling).wait_recv()
        for cp in sent:
            cp.wait_send()
        for cp in own:
            cp.wait()

    out_type = [_sds((N_DEV,) + a.shape, a.dtype) for a in arrays]
    sent_bytes = sum(a.size * a.dtype.itemsize for a in arrays)
    return pl.kernel(
        body, out_type=out_type, mesh=plsc.ScalarSubcoreMesh(axis_name="seq", num_cores=1), name=name,
        scratch_types=[pltpu.SemaphoreType.DMA((na * npair,)), pltpu.SemaphoreType.DMA((na * npair,)),
                       pltpu.SemaphoreType.DMA((na,))],
        compiler_params=pltpu.CompilerParams(collective_id=collective_id),
        cost_estimate=pl.CostEstimate(flops=0, transcendentals=0, bytes_accessed=2 * N_DEV * sent_bytes,
                                      remote_bytes_transferred=npair * sent_bytes),
    )(*arrays, *([] if after is None else [after]))


def _push_start(arrays, scatter, name, dep=None):
    na = len(arrays)
    shapes = [a.shape if scatter[i] else (N_DEV,) + a.shape for i, a in enumerate(arrays)]
    hbm = pl.BlockSpec(memory_space=pltpu.HBM)
    sem = pl.BlockSpec(memory_space=pltpu.SEMAPHORE)
    ndep = 0 if dep is None else 1

    def body(*refs):
        srcs = refs[:na]
        lands = refs[na:2 * na]
        send_sems, recv_sems = refs[2 * na + ndep:2 * na + ndep + 2]
        own_sems = refs[4 * na + ndep + 2]
        xi, yi, ci = (lax.axis_index(a) for a in MESH_AXES)
        me = 4 * xi + 2 * yi + ci
        own = [pltpu.make_async_copy(srcs[a].at[me] if scatter[a] else srcs[a], lands[a].at[me], own_sems.at[a])
               for a in range(na)]
        for cp in own:
            cp.start()
        for cp in own:
            cp.wait()
        for a in range(na):
            for pos, peer in _peers(xi, yi, ci):
                pltpu.make_async_remote_copy(
                    src_ref=srcs[a].at[peer] if scatter[a] else srcs[a], dst_ref=lands[a].at[me],
                    send_sem=send_sems.at[a], recv_sem=recv_sems.at[a],
                    device_id=pos, device_id_type=pl.DeviceIdType.MESH).start()

    ins = [pltpu.with_memory_space_constraint(a, pltpu.HBM) for a in arrays]
    ins += [pltpu.with_memory_space_constraint(lax.empty(s, a.dtype), pltpu.HBM) for s, a in zip(shapes, arrays)]
    res = pl.pallas_call(
        body, name=name,
        in_specs=[hbm] * (2 * na) + [pl.BlockSpec(memory_space=pl.ANY)] * ndep,
        out_specs=[sem, sem] + [hbm] * (2 * na),
        out_shape=[pltpu.SemaphoreType.DMA((na,)), pltpu.SemaphoreType.DMA((na,))]
                  + [pltpu.HBM(a.shape, a.dtype) for a in arrays] + [pltpu.HBM(s, a.dtype) for s, a in zip(shapes, arrays)],
        input_output_aliases={i: 2 + i for i in range(2 * na)},
        scratch_shapes=[pltpu.SemaphoreType.DMA((na,))],
        compiler_params=pltpu.CompilerParams(has_side_effects=pltpu.SideEffectType.DATAFLOW_SIDE_EFFECTING),
    )(*ins, *([] if dep is None else [dep]))
    return (res[0], res[1], list(res[2:2 + na]), list(res[2 + na:2 + 2 * na])), res[2]


def _push_wait(handle, after, name):
    send_sems, recv_sems, srcs, lands = handle
    na = len(srcs)
    hbm = pl.BlockSpec(memory_space=pltpu.HBM)
    sem = pl.BlockSpec(memory_space=pltpu.SEMAPHORE)

    def body(*refs):
        land_refs = refs[na:2 * na]
        send_ref, recv_ref = refs[2 * na:2 * na + 2]
        me = tuple(lax.axis_index(a) for a in MESH_AXES)
        for a in range(na):
            seven = land_refs[a].at[pl.ds(0, N_DEV - 1)]
            cp = pltpu.make_async_remote_copy(src_ref=seven, dst_ref=seven, send_sem=send_ref.at[a], recv_sem=recv_ref.at[a],
                                              device_id=me, device_id_type=pl.DeviceIdType.MESH)
            cp.wait_send()
            cp.wait_recv()

    res = pl.pallas_call(
        body, name=name,
        in_specs=[hbm] * (2 * na) + [sem, sem, pl.BlockSpec(memory_space=pl.ANY)],
        out_specs=[hbm] * (2 * na),
        out_shape=[pltpu.HBM(a.shape, a.dtype) for a in srcs] + [pltpu.HBM(a.shape, a.dtype) for a in lands],
        input_output_aliases={i: i for i in range(2 * na)},
        compiler_params=pltpu.CompilerParams(has_side_effects=pltpu.SideEffectType.DATAFLOW_SIDE_EFFECTING),
    )(*srcs, *lands, send_sems, recv_sems, after)
    return list(res[na:])


def _adamw(w, g, m, v):
    m = ADAM_B1 * m + (1.0 - ADAM_B1) * g
    v = ADAM_B2 * v + (1.0 - ADAM_B2) * jnp.square(g)
    m_hat = m / (1.0 - ADAM_B1 ** ADAM_STEP)
    v_hat = v / (1.0 - ADAM_B2 ** ADAM_STEP)
    delta = -ADAM_LR * (m_hat / (jnp.sqrt(v_hat) + ADAM_EPS) + ADAM_WD * w)
    return delta, m, v


def _sum_devices(ref):
    g = ref[0].astype(F32)
    for k in range(1, N_DEV):
        g = g + ref[k].astype(F32)
    return g


def _update_big(parts, w, m, v, name):
    rows, cols = w.shape
    tr = _pick_tile(rows, (128, 64, 16))

    def body(p_ref, w_ref, m_ref, v_ref, g_ref, d_ref, nm_ref, nv_ref):
        g = _sum_devices(p_ref)
        g_ref[...] = g
        d_ref[...], nm_ref[...], nv_ref[...] = _adamw(w_ref[...], g, m_ref[...], v_ref[...])

    blk = pl.BlockSpec((tr, cols), lambda i: (i, 0))
    return pl.pallas_call(
        body, name=name, grid=(rows // tr,),
        in_specs=[pl.BlockSpec((N_DEV, tr, cols), lambda i: (0, i, 0)), blk, blk, blk],
        out_specs=[blk] * 4, out_shape=[_sds((rows, cols), F32)] * 4,
        compiler_params=_params(("parallel",)),
    )(parts, w, m, v)


_VEC_ORDER = ("ln_in_g", "ln_in_b", "conv_b", "conv_ln_g", "conv_ln_b", "gate_bias", "gla_norm_g",
              "ln1_g", "ln1_b", "ln2_g", "ln2_b")
_SHARDED_SMALL = (("meta_tokens", 0, N_META, LANES), ("conv_w", N_META, CONV_WIDTH, None), ("gate_up", N_META + 32, GLA_RANK, None))


def _update_small(parts_sh, parts_vec, wmv):
    names = [s[0] for s in _SHARDED_SMALL] + list(_VEC_ORDER)
    flat = [a for nme in names for a in wmv[nme]]
    nv = len(_VEC_ORDER)

    def body(*refs):
        sh_ref, vec_ref = refs[0], refs[1]
        ins = refs[2:2 + len(flat)]
        outs = refs[2 + len(flat):2 + len(flat) + 4 * len(names)]
        loss_ref = refs[2 + len(flat) + 4 * len(names)]
        gsh_ref, gvec_ref = refs[-2:]
        gsh_ref[...] = _sum_devices(sh_ref)
        gvec_ref[...] = _sum_devices(vec_ref)
        loss_ref[...] = gvec_ref[nv:nv + 1, :]
        for idx, nme in enumerate(names):
            w_ref, m_ref, v_ref = ins[3 * idx:3 * idx + 3]
            rows, cols = w_ref.shape
            if idx < len(_SHARDED_SMALL):
                r0 = _SHARDED_SMALL[idx][1]
                g = gsh_ref[r0:r0 + rows, 0:cols]
            else:
                j = idx - len(_SHARDED_SMALL)
                g = gvec_ref[j:j + 1, 0:cols]
            o = outs[4 * idx:4 * idx + 4]
            o[0][...] = g
            o[1][...], o[2][...], o[3][...] = _adamw(w_ref[...], g, m_ref[...], v_ref[...])

    out_shape = [_sds(wmv[nme][0].shape, F32) for nme in names for _ in range(4)] + [_sds((1, parts_vec.shape[2]), F32)]
    vmem = pl.BlockSpec(memory_space=pltpu.VMEM)
    res = pl.pallas_call(
        body, name="update_small", out_shape=out_shape,
        in_specs=[vmem] * (2 + len(flat)), out_specs=[vmem] * len(out_shape),
        scratch_shapes=[pltpu.VMEM(parts_sh.shape[1:], F32), pltpu.VMEM(parts_vec.shape[1:], F32)],
    )(parts_sh, parts_vec, *flat)
    return {nme: res[4 * i:4 * i + 4] for i, nme in enumerate(names)}, res[-1][0, 0]


_WEIGHTS = ("meta_tokens", "ln_in_g", "ln_in_b", "w_in", "conv_w", "conv_b", "conv_ln_g", "conv_ln_b", "gate_up",
            "gate_bias", "gla_norm_g", "w_out", "ln1_g", "ln1_b", "w_ff1", "w_ff2", "ln2_g", "ln2_b")


def kernel(x, meta_tokens, ln_in_g, ln_in_b, w_in, conv_w, conv_b, conv_ln_g, conv_ln_b, gate_up, gate_bias, gla_norm_g, w_out, ln1_g, ln1_b, w_ff1, w_ff2, ln2_g, ln2_b, loss_target, m_meta_tokens, m_ln_in_g, m_ln_in_b, m_w_in, m_conv_w, m_conv_b, m_conv_ln_g, m_conv_ln_b, m_gate_up, m_gate_bias, m_gla_norm_g, m_w_out, m_ln1_g, m_ln1_b, m_w_ff1, m_w_ff2, m_ln2_g, m_ln2_b, v_meta_tokens, v_ln_in_g, v_ln_in_b, v_w_in, v_conv_w, v_conv_b, v_conv_ln_g, v_conv_ln_b, v_gate_up, v_gate_bias, v_gla_norm_g, v_w_out, v_ln1_g, v_ln1_b, v_w_ff1, v_w_ff2, v_ln2_g, v_ln2_b):
    w = dict(meta_tokens=meta_tokens, ln_in_g=ln_in_g, ln_in_b=ln_in_b, w_in=w_in, conv_w=conv_w, conv_b=conv_b,
             conv_ln_g=conv_ln_g, conv_ln_b=conv_ln_b, gate_up=gate_up, gate_bias=gate_bias, gla_norm_g=gla_norm_g,
             w_out=w_out, ln1_g=ln1_g, ln1_b=ln1_b, w_ff1=w_ff1, w_ff2=w_ff2, ln2_g=ln2_g, ln2_b=ln2_b)
    mom = dict(meta_tokens=m_meta_tokens, ln_in_g=m_ln_in_g, ln_in_b=m_ln_in_b, w_in=m_w_in, conv_w=m_conv_w,
               conv_b=m_conv_b, conv_ln_g=m_conv_ln_g, conv_ln_b=m_conv_ln_b, gate_up=m_gate_up, gate_bias=m_gate_bias,
               gla_norm_g=m_gla_norm_g, w_out=m_w_out, ln1_g=m_ln1_g, ln1_b=m_ln1_b, w_ff1=m_w_ff1, w_ff2=m_w_ff2,
               ln2_g=m_ln2_g, ln2_b=m_ln2_b)
    var = dict(meta_tokens=v_meta_tokens, ln_in_g=v_ln_in_g, ln_in_b=v_ln_in_b, w_in=v_w_in, conv_w=v_conv_w,
               conv_b=v_conv_b, conv_ln_g=v_conv_ln_g, conv_ln_b=v_conv_ln_b, gate_up=v_gate_up, gate_bias=v_gate_bias,
               gla_norm_g=v_gla_norm_g, w_out=v_w_out, ln1_g=v_ln1_g, ln1_b=v_ln1_b, w_ff1=v_w_ff1, w_ff2=v_w_ff2,
               ln2_g=v_ln2_g, ln2_b=v_ln2_b)
    shapes = {k: a.shape for k, a in w.items()}

    def two_d(a):
        return a.reshape(1, -1) if a.ndim == 1 else a.reshape(a.shape[-2:])

    w2d = {k: two_d(a) for k, a in w.items()}
    m2d = {k: two_d(a) for k, a in mom.items()}
    v2d = {k: two_d(a) for k, a in var.items()}
    d = x.shape[-1]
    d_in = w2d["w_in"].shape[1] * N_DEV
    d_in_p = -(-d_in // LANES) * LANES

    g_in, g_meta, g_conv, g_gup = _sc_gather(
        [w2d["w_in"].astype(BF16), w2d["meta_tokens"], w2d["conv_w"], w2d["gate_up"]], "gather_first", 0)
    g_out, g_ff1, g_ff2 = _sc_gather(
        [w2d["w_out"].astype(BF16), w2d["w_ff1"].astype(BF16), w2d["w_ff2"].astype(BF16)], "gather_late", 1, after=g_gup)
    w_in_full = jnp.pad(g_in.transpose(1, 0, 2).reshape(d, d_in), ((0, 0), (0, d_in_p - d_in)))
    meta_full = g_meta.transpose(1, 0, 2).reshape(N_META, d)
    conv_w_full = g_conv.transpose(1, 0, 2).reshape(CONV_WIDTH, -1)
    gate_up_full = g_gup.transpose(1, 0, 2).reshape(GLA_RANK, -1)

    def late_weights(after):
        return g_out.reshape(-1, d), g_ff1, g_ff2.reshape(-1, d)

    pushed = {}

    def push(tag, grads):
        if tag == "ff":
            pushed["ff1"], pushed["ff2"] = _sc_exchange(list(grads), [True, True], "scatter_ff", 2)
        elif tag == "out":
            pushed["p_out"] = grads[0].reshape(N_DEV, -1, d)
        else:
            p_in = grads[0][:, :d_in].reshape(d, N_DEV, d_in // N_DEV).transpose(1, 0, 2).astype(BF16)
            pushed["in"], pushed["out"] = _sc_exchange([p_in, pushed["p_out"]], [True, True], "scatter_rest", 3,
                                                       after=pushed["ff1"])
        return grads[0]

    res = _local_step(x, loss_target, meta_full, w2d["ln_in_g"], w2d["ln_in_b"], w_in_full, conv_w_full, w2d["conv_b"],
                      w2d["conv_ln_g"], w2d["conv_ln_b"], gate_up_full, w2d["gate_bias"], w2d["gla_norm_g"], late_weights,
                      w2d["ln1_g"], w2d["ln1_b"], w2d["ln2_g"], w2d["ln2_b"], push)

    dc = res["conv_w"].shape[1]
    hk = res["gate_up"].shape[1]
    sh_meta = res["meta_tokens"].reshape(N_META, N_DEV, LANES).transpose(1, 0, 2)
    sh_conv = jnp.pad(res["conv_w"].reshape(CONV_WIDTH, N_DEV, dc // N_DEV).transpose(1, 0, 2),
                      ((0, 0), (0, 32 - CONV_WIDTH), (0, LANES - dc // N_DEV)))
    sh_gup = jnp.pad(res["gate_up"].reshape(GLA_RANK, N_DEV, hk // N_DEV).transpose(1, 0, 2),
                     ((0, 0), (0, 0), (0, LANES - hk // N_DEV)))
    p_sh = jnp.concatenate([sh_meta, sh_conv, sh_gup], axis=1)
    p_vec = jnp.concatenate([jnp.pad(res[k], ((0, 0), (0, d - res[k].shape[1]))) for k in _VEC_ORDER]
                            + [jnp.full((1, d), res["loss"], F32), jnp.zeros((15 - len(_VEC_ORDER), d), F32)], axis=0)

    r_sh, r_vec = _exchange([p_sh, p_vec], [True, False], "scatter_small")
    r_ff1, r_ff2, r_out, r_in = pushed["ff1"], pushed["ff2"], pushed["out"], pushed["in"]

    upd = {}
    upd["w_in"] = _update_big(r_in, w2d["w_in"], m2d["w_in"], v2d["w_in"], "update_w_in")
    upd["w_out"] = _update_big(r_out, w2d["w_out"], m2d["w_out"], v2d["w_out"], "update_w_out")
    upd["w_ff1"] = _update_big(r_ff1, w2d["w_ff1"], m2d["w_ff1"], v2d["w_ff1"], "update_w_ff1")
    upd["w_ff2"] = _update_big(r_ff2, w2d["w_ff2"], m2d["w_ff2"], v2d["w_ff2"], "update_w_ff2")
    small = [s[0] for s in _SHARDED_SMALL] + list(_VEC_ORDER)
    upd_small, loss = _update_small(r_sh, r_vec, {k: (w2d[k], m2d[k], v2d[k]) for k in small})
    upd.update(upd_small)

    outs = [loss, res["grad_x"]]
    for j in range(4):
        outs += [upd[k][j].reshape(shapes[k]) for k in _WEIGHTS]
    return tuple(outs)
```

```python
import functools

import jax
import jax.numpy as jnp
from jax import lax
from jax.experimental import pallas as pl
from jax.experimental.pallas import tpu as pltpu
from jax.experimental.pallas import tpu_sc as plsc

F32 = jnp.float32
BF16 = jnp.bfloat16

N_META = 16
CHUNK = 64
PAD_FRONT = (-N_META) % CHUNK
X_OFF = PAD_FRONT + N_META
CONV_WIDTH = 31
CONV_HALO = 32
CONV_SUB = 64
CONV_WIN = CONV_SUB + CONV_HALO
GLA_HEADS = 4
GLA_DK = 64
GLA_DV = 128
GLA_RANK = 16
GLA_TAU = 16.0
QK_SCALE = GLA_DK ** -0.5
LN_EPS = 1e-5
ALPHA = 2.0 ** 0.25
LANES = 128
N_DEV = 8
ADAM_LR = 0.001
ADAM_B1 = 0.9
ADAM_B2 = 0.999
ADAM_EPS = 1e-08
ADAM_WD = 0.01
ADAM_STEP = 10
VMEM_LIMIT = 56 * 1024 * 1024
MESH_AXES = ("x", "y", "c")


def _sds(shape, dtype):
    return jax.ShapeDtypeStruct(shape, dtype)


def _mm(a, b):
    return jnp.dot(a, b, preferred_element_type=F32)


def _mm_nt(a, b):
    return lax.dot_general(a, b, (((1,), (1,)), ((), ())), preferred_element_type=F32)


def _mm_tn(a, b):
    return lax.dot_general(a, b, (((0,), (0,)), ((), ())), preferred_element_type=F32)


def _sigmoid(x):
    return 1.0 / (1.0 + jnp.exp(-x))


def _log_sigmoid(z):
    return jnp.minimum(z, 0.0) - jnp.log(1.0 + jnp.exp(-jnp.abs(z)))


def _ln(x):
    mu = jnp.mean(x, axis=-1, keepdims=True)
    xc = x - mu
    var = jnp.mean(xc * xc, axis=-1, keepdims=True)
    rstd = lax.rsqrt(var + LN_EPS)
    return xc * rstd, rstd


def _ln_bwd(dyg, xhat, rstd):
    m1 = jnp.mean(dyg, axis=-1, keepdims=True)
    m2 = jnp.mean(dyg * xhat, axis=-1, keepdims=True)
    return rstd * (dyg - m1 - xhat * m2)


def _rowsum(x):
    return jnp.sum(x, axis=0, keepdims=True)


def _row_in_seq(i, tm, tp):
    base = lax.rem(i * tm, tp)
    return base + lax.broadcasted_iota(jnp.int32, (tm, 1), 0)


def _split3(x):
    hi = x.astype(BF16)
    r1 = x - hi.astype(F32)
    mid = r1.astype(BF16)
    lo = (r1 - mid.astype(F32)).astype(BF16)
    return hi, mid, lo


def _tri_mm(tri, x):
    hi, mid, lo = _split3(x)
    return _mm(tri, hi) + _mm(tri, mid) + _mm(tri, lo)


def _params(sem):
    return pltpu.CompilerParams(dimension_semantics=sem, vmem_limit_bytes=VMEM_LIMIT)


def _pick_tile(n, prefs):
    for t in prefs:
        if n % t == 0:
            return t
    raise ValueError(f"no tile for {n}")


def _x_tile_row(tp, seq, tx):
    tps = seq // tx
    return lambda i: pl.multiple_of((i // tps) * tp + X_OFF + (i % tps) * tx, CHUNK)


def _inproj_fwd_x(x2, g, b, w_in, tp, seq, tx):
    rx, d = x2.shape
    n = w_in.shape[0]
    r = rx // seq * tp
    row = _x_tile_row(tp, seq, tx)

    def body(x_ref, g_ref, b_ref, w_ref, s0_ref, u_ref):
        xhat, _ = _ln(x_ref[...])
        s = xhat * g_ref[...] + b_ref[...]
        s0_ref[...] = s
        u_ref[...] = _mm_nt(s.astype(BF16), w_ref[...])

    return pl.pallas_call(
        body, name="inproj_fwd_x", grid=(rx // tx,),
        in_specs=[pl.BlockSpec((tx, d), lambda i: (i, 0)), pl.BlockSpec((1, d), lambda i: (0, 0)),
                  pl.BlockSpec((1, d), lambda i: (0, 0)), pl.BlockSpec((n, d), lambda i: (0, 0))],
        out_specs=[pl.BlockSpec((pl.Element(tx), pl.Element(d)), lambda i: (row(i), 0)),
                   pl.BlockSpec((pl.Element(tx), pl.Element(n)), lambda i: (row(i), 0))],
        out_shape=[_sds((r, d), F32), _sds((r, n), F32)],
        compiler_params=_params(("parallel",)),
    )(x2, g, b, w_in)


def _inproj_fwd_head(head, g, b, w_in, s0, u, tp):
    r, d = s0.shape
    n = w_in.shape[0]
    nb = tp // X_OFF

    def body(h_ref, g_ref, b_ref, w_ref, s0_in, u_in, s0_ref, u_ref):
        xhat, _ = _ln(h_ref[...])
        real = lax.broadcasted_iota(jnp.int32, (X_OFF, 1), 0) >= PAD_FRONT
        s = jnp.where(real, xhat * g_ref[...] + b_ref[...], 0.0)
        s0_ref[...] = s
        u_ref[...] = _mm_nt(s.astype(BF16), w_ref[...])

    anyspec = pl.BlockSpec(memory_space=pl.ANY)
    return pl.pallas_call(
        body, name="inproj_fwd_head", grid=(r // tp,),
        in_specs=[pl.BlockSpec((X_OFF, d), lambda i: (0, 0)), pl.BlockSpec((1, d), lambda i: (0, 0)),
                  pl.BlockSpec((1, d), lambda i: (0, 0)), pl.BlockSpec((n, d), lambda i: (0, 0)), anyspec, anyspec],
        out_specs=[pl.BlockSpec((X_OFF, d), lambda i: (i * nb, 0)), pl.BlockSpec((X_OFF, n), lambda i: (i * nb, 0))],
        out_shape=[_sds((r, d), F32), _sds((r, n), F32)],
        input_output_aliases={4: 0, 5: 1},
        compiler_params=_params(("parallel",)),
    )(head, g, b, w_in, s0, u)


def _conv_taps(win, coef, lo):
    acc = None
    for rho in range(8):
        offs = [o for o in range(lo, lo + CONV_WIDTH) if o % 8 == rho]
        if not offs:
            continue
        rolled = win if rho == 0 else pltpu.roll(win, CONV_WIN - rho, 0)
        for o in offs:
            m8 = o - rho
            term = rolled[m8:m8 + CONV_SUB, :] * coef(o)
            acc = term if acc is None else acc + term
    return acc


def _conv_fwd(u, w32, cb, cg, cbe, tp, tc, dc):
    r = u.shape[0]
    hb = tc // CONV_HALO

    def body(a_ref, g_ref, ah_ref, gh_ref, w_ref, cb_ref, cg_ref, cbe_ref, c_ref, co_ref, hs_ref):
        t = pl.program_id(0)
        first = lax.rem(t * tc, tp) == 0
        hh = ah_ref[...] * _sigmoid(gh_ref[...])
        hs_ref[0:CONV_HALO, :] = jnp.where(first, 0.0, hh)
        hs_ref[CONV_HALO:CONV_HALO + tc, :] = a_ref[...] * _sigmoid(g_ref[...])

        def sub(k, carry):
            r0 = pl.multiple_of(k * CONV_SUB, CONV_SUB)
            win = hs_ref[pl.ds(r0, CONV_WIN), :]
            c = _conv_taps(win, lambda o: w_ref[o - 2:o - 1, :], 2) + cb_ref[...]
            c_ref[pl.ds(r0, CONV_SUB), :] = c
            xhat, _ = _ln(c)
            cn = xhat * cg_ref[...] + cbe_ref[...]
            co_ref[pl.ds(r0, CONV_SUB), :] = (cn * _sigmoid(cn)).astype(BF16)
            return carry

        lax.fori_loop(0, tc // CONV_SUB, sub, 0)

    vec = pl.BlockSpec((1, dc), lambda t: (0, 0))
    return pl.pallas_call(
        body, name="conv_fwd", grid=(r // tc,),
        in_specs=[pl.BlockSpec((tc, dc), lambda t: (t, 0)), pl.BlockSpec((tc, dc), lambda t: (t, 1)),
                  pl.BlockSpec((CONV_HALO, dc), lambda t: (jnp.maximum(t * hb - 1, 0), 0)),
                  pl.BlockSpec((CONV_HALO, dc), lambda t: (jnp.maximum(t * hb - 1, 0), 1)),
                  pl.BlockSpec((32, dc), lambda t: (0, 0)), vec, vec, vec],
        out_specs=[pl.BlockSpec((tc, dc), lambda t: (t, 0)), pl.BlockSpec((tc, dc), lambda t: (t, 0))],
        out_shape=[_sds((r, dc), F32), _sds((r, dc), BF16)],
        scratch_shapes=[pltpu.VMEM((CONV_HALO + tc, dc), F32)],
        compiler_params=_params(("parallel",)),
    )(u, u, u, u, w32, cb, cg, cbe)


def _tri_mm_all(tri, xs):
    parts = [_split3(x) for x in xs]
    acc = [None] * len(xs)
    for t in range(3):
        for j in range(len(xs)):
            term = _mm(tri, parts[j][t])
            acc[j] = term if t == 0 else acc[j] + term
    return acc


def _gla_prep(qk_ref, gd_ref, gup, gb, n0, kc):
    rows = [slice(j * CHUNK, (j + 1) * CHUNK) for j in range(kc)]
    ri = lax.broadcasted_iota(jnp.int32, (CHUNK, CHUNK), 0)
    ci = lax.broadcasted_iota(jnp.int32, (CHUNK, CHUNK), 1)
    low = (ri >= ci).astype(BF16)
    hk = GLA_HEADS * GLA_DK
    gds = [gd_ref[rw, :] for rw in rows]
    zs = [_mm(g.astype(BF16), gup) + gb for g in gds]
    reals = [(n0 + j) * CHUNK + lax.broadcasted_iota(jnp.int32, (CHUNK, 1), 0) >= PAD_FRONT for j in range(kc)]
    lgs = [jnp.where(reals[j], _log_sigmoid(zs[j]) * (1.0 / GLA_TAU), 0.0) for j in range(kc)]
    bs = _tri_mm_all(low, lgs)
    out = []
    for j in range(kc):
        b, bl = bs[j], _rowsum(lgs[j])
        q = qk_ref[rows[j], :hk] * QK_SCALE
        k = qk_ref[rows[j], hk:]
        eb, enb, ebl = jnp.exp(b), jnp.exp(-b), jnp.exp(bl - b)
        out.append(dict(rows=rows[j], gd=gds[j], z=zs[j], real=reals[j], eb=eb, enb=enb, ebl=ebl, gam=jnp.exp(bl),
                        qe=q * eb, ke=k * enb, kd=k * ebl))
    return out, ri, ci


def _gla_heads(p, v_ref):
    ops = []
    for h in range(GLA_HEADS):
        hp, h2 = divmod(h, 2)
        ls = slice(hp * LANES, (hp + 1) * LANES)
        m = _head_mask(h2)
        ops.append(dict(ls=ls, m=m, vs=slice(h * GLA_DV, (h + 1) * GLA_DV),
                        qe=jnp.where(m, p["qe"][:, ls], 0.0).astype(BF16),
                        kd=jnp.where(m, p["kd"][:, ls], 0.0).astype(BF16),
                        ke=p["ke"][:, ls].astype(BF16),
                        v=v_ref[p["rows"], h * GLA_DV:(h + 1) * GLA_DV].astype(BF16)))
    return ops


def _head_mask(h2):
    lane = lax.broadcasted_iota(jnp.int32, (1, LANES), 1)
    return (lane < GLA_DK) if h2 == 0 else (lane >= GLA_DK)


def _gla_fwd(u, gup, gb, gn, bsz, nc, kc):
    r = u.shape[0]
    hv = GLA_HEADS * GLA_DV
    ns = nc // kc

    def body(qk_ref, v_ref, r_ref, gd_ref, gup_ref, gb_ref, gn_ref, go_ref, sta_ref, st_ref):
        t = pl.program_id(1)

        @pl.when(t == 0)
        def _():
            st_ref[...] = jnp.zeros_like(st_ref)

        ps, ri, ci = _gla_prep(qk_ref, gd_ref, gup_ref[...], gb_ref[...], t * kc, kc)
        tril = ri >= ci
        items = [(j, h) for j in range(kc) for h in range(GLA_HEADS)]
        ops = [_gla_heads(p, v_ref) for p in ps]
        a = {jh: jnp.where(tril, _mm_nt(ops[jh[0]][jh[1]]["qe"], ops[jh[0]][jh[1]]["ke"]), 0.0).astype(BF16) for jh in items}
        oi = {jh: _mm(a[jh], ops[jh[0]][jh[1]]["v"]) for jh in items}
        inc = {jh: _mm_tn(ops[jh[0]][jh[1]]["v"], ops[jh[0]][jh[1]]["kd"]) for jh in items}
        sts = [st_ref[h] for h in range(GLA_HEADS)]
        for j, h in items:
            op, p = ops[j][h], ps[j]
            st = sts[h]
            sta_ref[j, h] = st
            o = oi[j, h] + _mm_nt(op["qe"], st.astype(BF16))
            sts[h] = st * p["gam"][:, op["ls"]] + inc[j, h]
            rs = lax.rsqrt(jnp.mean(o * o, axis=-1, keepdims=True) + LN_EPS)
            rr = r_ref[p["rows"], op["vs"]]
            go_ref[p["rows"], op["vs"]] = (o * rs * gn_ref[...] * (rr * _sigmoid(rr))).astype(BF16)
        for h in range(GLA_HEADS):
            st_ref[h] = sts[h]

    rowblk = lambda col: (lambda b, t: (b * ns + t, col))
    const = lambda b, t: (0, 0)
    return pl.pallas_call(
        body, name="gla_fwd", grid=(bsz, ns),
        in_specs=[pl.BlockSpec((kc * CHUNK, 512), rowblk(2)), pl.BlockSpec((kc * CHUNK, hv), rowblk(3)),
                  pl.BlockSpec((kc * CHUNK, hv), rowblk(4)), pl.BlockSpec((kc * CHUNK, LANES), rowblk(20)),
                  pl.BlockSpec((LANES, 256), const), pl.BlockSpec((1, 256), const), pl.BlockSpec((1, GLA_DV), const)],
        out_specs=[pl.BlockSpec((kc * CHUNK, hv), rowblk(0)),
                   pl.BlockSpec((kc, GLA_HEADS, LANES, LANES), lambda b, t: (b * ns + t, 0, 0, 0))],
        out_shape=[_sds((r, hv), BF16), _sds((bsz * nc, GLA_HEADS, LANES, LANES), F32)],
        scratch_shapes=[pltpu.VMEM((GLA_HEADS, LANES, LANES), F32)],
        compiler_params=_params(("parallel", "arbitrary")),
    )(u, u, u, u, gup, gb, gn)


def _outproj_fwd(s0, co, go, w_out, g1, b1, tm):
    r, d = s0.shape
    dc = co.shape[1]

    def body(s0_ref, co_ref, go_ref, w_ref, g_ref, b_ref, p1_ref, s1_ref, s1b_ref):
        mix = _mm(co_ref[...], w_ref[0:dc, :]) + _mm(go_ref[...], w_ref[dc:2 * dc, :])
        p1 = ALPHA * s0_ref[...] + mix
        p1_ref[...] = p1
        xhat, _ = _ln(p1)
        s1 = xhat * g_ref[...] + b_ref[...]
        s1_ref[...] = s1
        s1b_ref[...] = s1.astype(BF16)

    row = lambda w: pl.BlockSpec((tm, w), lambda i: (i, 0))
    vec = pl.BlockSpec((1, d), lambda i: (0, 0))
    return pl.pallas_call(
        body, name="outproj_fwd", grid=(r // tm,),
        in_specs=[row(d), row(dc), row(dc), pl.BlockSpec((2 * dc, d), lambda i: (0, 0)), vec, vec],
        out_specs=[row(d), row(d), row(d)],
        out_shape=[_sds((r, d), F32), _sds((r, d), F32), _sds((r, d), BF16)],
        compiler_params=_params(("parallel",)),
    )(s0, co, go, w_out, g1, b1)


def _mlp_fwd(s1, s1b, w1g, w2, g2, b2, tgt, tp, tm, ns):
    r, d = s1.shape
    nh, _, th = w1g.shape
    nj = nh // ns

    def body(s1_ref, sb_ref, w1_ref, w2_ref, g_ref, b_ref, t_ref, hm_ref, dp2_ref, dpb_ref, loss_ref, dg_ref, db_ref, acc_ref):
        i = pl.program_id(0)
        j = pl.program_id(1)

        @pl.when(jnp.logical_and(i == 0, j == 0))
        def _():
            loss_ref[...] = jnp.zeros_like(loss_ref)
            dg_ref[...] = jnp.zeros_like(dg_ref)
            db_ref[...] = jnp.zeros_like(db_ref)

        @pl.when(j == 0)
        def _():
            acc_ref[...] = jnp.zeros_like(acc_ref)

        hs = [_mm(sb_ref[...], w1_ref[s]) for s in range(ns)]
        acc = acc_ref[...]
        for s in range(ns):
            hm_ref[:, s * th:(s + 1) * th] = hs[s].astype(BF16)
            act = jnp.square(jnp.maximum(hs[s], 0.0))
            acc = acc + _mm(act.astype(BF16), w2_ref[s * th:(s + 1) * th, :])
        acc_ref[...] = acc

        @pl.when(j == nj - 1)
        def _():
            p2 = ALPHA * s1_ref[...] + acc_ref[...]
            xhat, rstd = _ln(p2)
            s2 = xhat * g_ref[...] + b_ref[...]
            isx = _row_in_seq(i, tm, tp) >= X_OFF
            tg = t_ref[...]
            tg = jnp.where(i == 0, pltpu.roll(tg, X_OFF, 0), tg)
            err = jnp.where(isx, s2 - tg, 0.0)
            loss_ref[...] += 0.5 * jnp.sum(jnp.mean(err * err, axis=-1, keepdims=True))
            dy = err * (1.0 / d)
            dg_ref[...] += _rowsum(dy * xhat)
            db_ref[...] += _rowsum(dy)
            dp2 = _ln_bwd(dy * g_ref[...], xhat, rstd)
            dp2_ref[...] = dp2
            dpb_ref[...] = dp2.astype(BF16)

    row = pl.BlockSpec((tm, d), lambda i, j: (i, 0))
    vec = pl.BlockSpec((1, d), lambda i, j: (0, 0))
    tgt_row = pl.BlockSpec((pl.Element(tm), pl.Element(d)),
                           lambda i, j: (pl.multiple_of(jnp.maximum(i * tm - X_OFF * ((i * tm) // tp + 1), 0), CHUNK), 0))
    return pl.pallas_call(
        body, name="mlp_fwd", grid=(r // tm, nj),
        in_specs=[row, row, pl.BlockSpec((ns, d, th), lambda i, j: (j, 0, 0)), pl.BlockSpec((ns * th, d), lambda i, j: (j, 0)),
                  vec, vec, tgt_row],
        out_specs=[pl.BlockSpec((tm, ns * th), lambda i, j: (i, j)), row, row,
                   pl.BlockSpec((8, LANES), lambda i, j: (0, 0)), vec, vec],
        out_shape=[_sds((r, nh * th), BF16), _sds((r, d), F32), _sds((r, d), BF16), _sds((8, LANES), F32),
                   _sds((1, d), F32), _sds((1, d), F32)],
        scratch_shapes=[pltpu.VMEM((tm, d), F32)],
        compiler_params=_params(("arbitrary", "arbitrary")),
    )(s1, s1b, w1g, w2, g2, b2, tgt)


def _mlp_bwd_act(dp2, dpb, hm, w1g, w2, p1, g1, tm, ns):
    r, d = dp2.shape
    nh, _, th = w1g.shape
    nj = nh // ns

    def body(dp2_ref, dpb_ref, hm_ref, w1_ref, w2_ref, p1_ref, g_ref, dh_ref, dp1_ref, dg_ref, db_ref, acc_ref):
        i = pl.program_id(0)
        j = pl.program_id(1)

        @pl.when(jnp.logical_and(i == 0, j == 0))
        def _():
            dg_ref[...] = jnp.zeros_like(dg_ref)
            db_ref[...] = jnp.zeros_like(db_ref)

        @pl.when(j == 0)
        def _():
            acc_ref[...] = jnp.zeros_like(acc_ref)

        dacts = [_mm_nt(dpb_ref[...], w2_ref[s * th:(s + 1) * th, :]) for s in range(ns)]
        acc = acc_ref[...]
        for s in range(ns):
            cols = slice(s * th, (s + 1) * th)
            dh = (dacts[s] * (2.0 * jnp.maximum(hm_ref[:, cols].astype(F32), 0.0))).astype(BF16)
            dh_ref[:, cols] = dh
            acc = acc + _mm_nt(dh, w1_ref[s])
        acc_ref[...] = acc

        @pl.when(j == nj - 1)
        def _():
            ds1 = ALPHA * dp2_ref[...] + acc_ref[...]
            xhat, rstd = _ln(p1_ref[...])
            dg_ref[...] += _rowsum(ds1 * xhat)
            db_ref[...] += _rowsum(ds1)
            dp1_ref[...] = _ln_bwd(ds1 * g_ref[...], xhat, rstd)

    row = pl.BlockSpec((tm, d), lambda i, j: (i, 0))
    vec = pl.BlockSpec((1, d), lambda i, j: (0, 0))
    blk = pl.BlockSpec((tm, ns * th), lambda i, j: (i, j))
    return pl.pallas_call(
        body, name="mlp_bwd_act", grid=(r // tm, nj),
        in_specs=[row, row, blk, pl.BlockSpec((ns, d, th), lambda i, j: (j, 0, 0)),
                  pl.BlockSpec((ns * th, d), lambda i, j: (j, 0)), row, vec],
        out_specs=[blk, row, vec, vec],
        out_shape=[_sds((r, nh * th), BF16), _sds((r, d), F32), _sds((1, d), F32), _sds((1, d), F32)],
        scratch_shapes=[pltpu.VMEM((tm, d), F32)],
        compiler_params=_params(("arbitrary", "arbitrary")),
    )(dp2, dpb, hm, w1g, w2, p1, g1)


def _mlp_bwd_w(s1b, hm, dh, dpb, nh, tm, ns):
    r, d = s1b.shape
    th = hm.shape[1] // nh

    def body(s1_ref, hm_ref, dh_ref, dp2_ref, dw1_ref, dw2_ref, a1_ref, a2_ref):
        i = pl.program_id(1)

        @pl.when(i == 0)
        def _():
            a1_ref[...] = jnp.zeros_like(a1_ref)
            a2_ref[...] = jnp.zeros_like(a2_ref)

        for s in range(ns):
            a1_ref[s] += _mm_tn(s1_ref[...], dh_ref[:, s * th:(s + 1) * th])
        for s in range(ns):
            act = jnp.square(jnp.maximum(hm_ref[:, s * th:(s + 1) * th].astype(F32), 0.0)).astype(BF16)
            a2_ref[s] += _mm_tn(act, dp2_ref[...])

        @pl.when(i == pl.num_programs(1) - 1)
        def _():
            dw1_ref[...] = a1_ref[...].astype(BF16)
            dw2_ref[...] = a2_ref[...].astype(BF16)

    row = pl.BlockSpec((tm, d), lambda j, i: (i, 0))
    blk = pl.BlockSpec((tm, ns * th), lambda j, i: (i, j))
    return pl.pallas_call(
        body, name="mlp_bwd_w", grid=(nh // ns, r // tm),
        in_specs=[row, blk, blk, row],
        out_specs=[pl.BlockSpec((ns, d, th), lambda j, i: (j, 0, 0)), pl.BlockSpec((ns, th, d), lambda j, i: (j, 0, 0))],
        out_shape=[_sds((nh, d, th), BF16), _sds((nh, th, d), BF16)],
        scratch_shapes=[pltpu.VMEM((ns, d, th), F32), pltpu.VMEM((ns, th, d), F32)],
        compiler_params=_params(("parallel", "arbitrary")),
    )(s1b, hm, dh, dpb)


def _outproj_bwd(dp1, co, go, w_out, dep, tm):
    r, d = dp1.shape
    dc = co.shape[1]

    def body(dp_ref, co_ref, go_ref, w_ref, dep_ref, dmi_ref, dw_ref, acc_ref):
        i = pl.program_id(0)

        @pl.when(i == 0)
        def _():
            acc_ref[...] = jnp.zeros_like(acc_ref)

        dpb = dp_ref[...].astype(BF16)
        dmi_ref[...] = _mm_nt(dpb, w_ref[...])
        acc_ref[0:dc, :] += _mm_tn(co_ref[...], dpb)
        acc_ref[dc:2 * dc, :] += _mm_tn(go_ref[...], dpb)

        @pl.when(i == pl.num_programs(0) - 1)
        def _():
            dw_ref[...] = acc_ref[...].astype(BF16)

    row = lambda w: pl.BlockSpec((tm, w), lambda i: (i, 0))
    full = pl.BlockSpec((2 * dc, d), lambda i: (0, 0))
    return pl.pallas_call(
        body, name="outproj_bwd", grid=(r // tm,),
        in_specs=[row(d), row(dc), row(dc), full, pl.BlockSpec(memory_space=pl.ANY)],
        out_specs=[row(2 * dc), full],
        out_shape=[_sds((r, 2 * dc), F32), _sds((2 * dc, d), BF16)],
        scratch_shapes=[pltpu.VMEM((2 * dc, d), F32)],
        compiler_params=_params(("arbitrary",)),
    )(dp1, co, go, w_out, dep)


def _gla_bwd(u, dmi, sta, gup, gb, gn, dep, bsz, nc, kc):
    r = u.shape[0]
    hv = GLA_HEADS * GLA_DV
    hk = GLA_HEADS * GLA_DK
    ns = nc // kc

    def body(qk_ref, v_ref, r_ref, gd_ref, dgo_ref, sta_ref, gup_ref, gb_ref, gn_ref, dep_ref,
             dqk_ref, dv_ref, dr_ref, dgd_ref, dgn_ref, dgb_ref, dgup_ref, dst_ref):
        bi = pl.program_id(0)
        t = pl.program_id(1)

        @pl.when(jnp.logical_and(bi == 0, t == 0))
        def _():
            dgn_ref[...] = jnp.zeros_like(dgn_ref)
            dgb_ref[...] = jnp.zeros_like(dgb_ref)
            dgup_ref[...] = jnp.zeros_like(dgup_ref)

        @pl.when(t == 0)
        def _():
            dst_ref[...] = jnp.zeros_like(dst_ref)

        ps, ri, ci = _gla_prep(qk_ref, gd_ref, gup_ref[...], gb_ref[...], (ns - 1 - t) * kc, kc)
        tril = ri >= ci
        items = [(j, h) for j in reversed(range(kc)) for h in range(GLA_HEADS)]
        ops = [_gla_heads(p, v_ref) for p in ps]
        op = lambda jh: ops[jh[0]][jh[1]]
        st = {jh: sta_ref[jh[0], jh[1]] for jh in items}
        stb = {jh: st[jh].astype(BF16) for jh in items}
        a = {jh: jnp.where(tril, _mm_nt(op(jh)["qe"], op(jh)["ke"]), 0.0).astype(BF16) for jh in items}
        o1 = {jh: _mm(a[jh], op(jh)["v"]) for jh in items}
        o2 = {jh: _mm_nt(op(jh)["qe"], stb[jh]) for jh in items}
        dob = {}
        dgn = jnp.zeros((1, GLA_DV), F32)
        for jh in items:
            rows, vs = ps[jh[0]]["rows"], op(jh)["vs"]
            o = o1[jh] + o2[jh]
            rr = r_ref[rows, vs]
            sr = _sigmoid(rr)
            rs = lax.rsqrt(jnp.mean(o * o, axis=-1, keepdims=True) + LN_EPS)
            y = o * rs
            dgo = dgo_ref[rows, vs]
            don = dgo * (rr * sr)
            dr_ref[rows, vs] = dgo * (y * gn_ref[...]) * (sr * (1.0 + rr * (1.0 - sr)))
            dgn = dgn + _rowsum(don * y)
            dxn = don * gn_ref[...]
            dob[jh] = (rs * (dxn - y * jnp.mean(dxn * y, axis=-1, keepdims=True))).astype(BF16)
        da = {jh: jnp.where(tril, _mm_nt(dob[jh], op(jh)["v"]), 0.0).astype(BF16) for jh in items}
        dv1 = {jh: _mm_tn(a[jh], dob[jh]) for jh in items}
        dqe1 = {jh: _mm(da[jh], op(jh)["ke"]) for jh in items}
        dqe2 = {jh: _mm(dob[jh], stb[jh]) for jh in items}
        dke1 = {jh: _mm_tn(da[jh], op(jh)["qe"]) for jh in items}
        inc = {jh: _mm_tn(dob[jh], op(jh)["qe"]) for jh in items}
        dsts = [dst_ref[h] for h in range(GLA_HEADS)]
        dkd1, dgam1 = {}, {}
        for jh in items:
            j, h = jh
            dst = dsts[h]
            dstb = dst.astype(BF16)
            dv_ref[ps[j]["rows"], op(jh)["vs"]] = dv1[jh] + _mm_nt(op(jh)["kd"], dstb)
            dkd1[jh] = _mm(op(jh)["v"], dstb)
            dgam1[jh] = _rowsum(dst * st[jh])
            dsts[h] = dst * ps[j]["gam"][:, op(jh)["ls"]] + inc[jh]
        for h in range(GLA_HEADS):
            dst_ref[h] = dsts[h]
        upper = (ri <= ci).astype(BF16)
        dbs, dbls = [], []
        for j in range(kc):
            p = ps[j]
            tiles = [[op((j, 2 * hp + h2)) for h2 in range(2)] for hp in range(GLA_HEADS // 2)]
            head = lambda d, hp, h2: d[j, 2 * hp + h2]
            lanes = lambda f: jnp.concatenate([f(hp) for hp in range(GLA_HEADS // 2)], axis=1)
            dqe = lanes(lambda hp: sum(jnp.where(tiles[hp][h2]["m"], head(dqe1, hp, h2) + head(dqe2, hp, h2), 0.0)
                                       for h2 in range(2)))
            dke = lanes(lambda hp: head(dke1, hp, 0) + head(dke1, hp, 1))
            dkd = lanes(lambda hp: sum(jnp.where(tiles[hp][h2]["m"], head(dkd1, hp, h2), 0.0) for h2 in range(2)))
            dgam = lanes(lambda hp: head(dgam1, hp, 0) + head(dgam1, hp, 1))
            dqk_ref[p["rows"], :hk] = dqe * p["eb"] * QK_SCALE
            dqk_ref[p["rows"], hk:] = dke * p["enb"] + dkd * p["ebl"]
            dkdkd = dkd * p["kd"]
            dbs.append(dqe * p["qe"] - dke * p["ke"] - dkdkd)
            dbls.append(_rowsum(dkdkd) + dgam * p["gam"])
        dlgs = _tri_mm_all(upper, dbs)
        dzb = []
        dgb = jnp.zeros((1, hk), F32)
        for j in range(kc):
            p = ps[j]
            dz = jnp.where(p["real"], (dlgs[j] + dbls[j]) * (1.0 / GLA_TAU) * _sigmoid(-p["z"]), 0.0)
            dgb = dgb + _rowsum(dz)
            dzb.append(dz.astype(BF16))
        dgup = sum(_mm_tn(ps[j]["gd"].astype(BF16), dzb[j]) for j in range(kc))
        for j in range(kc):
            dgd_ref[ps[j]["rows"], :] = _mm_nt(dzb[j], gup_ref[...])
        dgb_ref[...] += dgb
        dgup_ref[...] += dgup
        dgn_ref[...] += dgn

    rowblk = lambda col: (lambda b, t: (b * ns + ns - 1 - t, col))
    const = lambda b, t: (0, 0)
    return pl.pallas_call(
        body, name="gla_bwd", grid=(bsz, ns),
        in_specs=[pl.BlockSpec((kc * CHUNK, 2 * hk), rowblk(2)), pl.BlockSpec((kc * CHUNK, hv), rowblk(3)),
                  pl.BlockSpec((kc * CHUNK, hv), rowblk(4)), pl.BlockSpec((kc * CHUNK, LANES), rowblk(20)),
                  pl.BlockSpec((kc * CHUNK, hv), rowblk(1)),
                  pl.BlockSpec((kc, GLA_HEADS, LANES, LANES), lambda b, t: (b * ns + ns - 1 - t, 0, 0, 0)),
                  pl.BlockSpec((LANES, 256), const), pl.BlockSpec((1, 256), const), pl.BlockSpec((1, GLA_DV), const),
                  pl.BlockSpec(memory_space=pl.ANY)],
        out_specs=[pl.BlockSpec((kc * CHUNK, 2 * hk), rowblk(0)), pl.BlockSpec((kc * CHUNK, hv), rowblk(0)),
                   pl.BlockSpec((kc * CHUNK, hv), rowblk(0)), pl.BlockSpec((kc * CHUNK, LANES), rowblk(0)),
                   pl.BlockSpec((1, GLA_DV), const), pl.BlockSpec((1, 256), const), pl.BlockSpec((LANES, 256), const)],
        out_shape=[_sds((r, 2 * hk), F32), _sds((r, hv), F32), _sds((r, hv), F32), _sds((r, LANES), F32),
                   _sds((1, GLA_DV), F32), _sds((1, 256), F32), _sds((LANES, 256), F32)],
        scratch_shapes=[pltpu.VMEM((GLA_HEADS, LANES, LANES), F32)],
        compiler_params=_params(("arbitrary", "arbitrary")),
    )(u, u, u, u, dmi, sta, gup, gb, gn, dep)


def _conv_bwd(u, c, dmi, w32, cg, cbe, tp, tc, dc):
    r = u.shape[0]
    hb = tc // CONV_HALO
    nhalo = r // CONV_HALO

    def dconv(cv, dco, cg_ref, cbe_ref):
        xhat, rstd = _ln(cv)
        cn = xhat * cg_ref[...] + cbe_ref[...]
        sg = _sigmoid(cn)
        dcn = dco * (sg * (1.0 + cn * (1.0 - sg)))
        return _ln_bwd(dcn * cg_ref[...], xhat, rstd), dcn, xhat

    def body(a_ref, g_ref, ah_ref, gh_ref, c_ref, dco_ref, ch_ref, dcoh_ref, w_ref, cg_ref, cbe_ref,
             du_ref, dw_ref, dcb_ref, dcg_ref, dcbe_ref, hs_ref, dcs_ref):
        t = pl.program_id(0)

        @pl.when(t == 0)
        def _():
            dw_ref[...] = jnp.zeros_like(dw_ref)
            dcb_ref[...] = jnp.zeros_like(dcb_ref)
            dcg_ref[...] = jnp.zeros_like(dcg_ref)
            dcbe_ref[...] = jnp.zeros_like(dcbe_ref)

        first = lax.rem(t * tc, tp) == 0
        last = lax.rem((t + 1) * tc, tp) == 0
        hh = ah_ref[...] * _sigmoid(gh_ref[...])
        hs_ref[0:CONV_HALO, :] = jnp.where(first, 0.0, hh)
        hs_ref[CONV_HALO:CONV_HALO + tc, :] = a_ref[...] * _sigmoid(g_ref[...])
        dch, _, _ = dconv(ch_ref[...], dcoh_ref[...], cg_ref, cbe_ref)
        dcs_ref[tc:tc + CONV_HALO, :] = jnp.where(last, 0.0, dch)

        def sub1(k, carry):
            r0 = pl.multiple_of(k * CONV_SUB, CONV_SUB)
            dcv, dcn, xhat = dconv(c_ref[pl.ds(r0, CONV_SUB), :], dco_ref[pl.ds(r0, CONV_SUB), :], cg_ref, cbe_ref)
            dcs_ref[pl.ds(r0, CONV_SUB), :] = dcv
            dcb_ref[...] += _rowsum(dcv)
            dcg_ref[...] += _rowsum(dcn * xhat)
            dcbe_ref[...] += _rowsum(dcn)
            return carry

        lax.fori_loop(0, tc // CONV_SUB, sub1, 0)

        def sub2(k, carry):
            r0 = pl.multiple_of(k * CONV_SUB, CONV_SUB)
            dwin = dcs_ref[pl.ds(r0, CONV_WIN), :]
            dh = _conv_taps(dwin, lambda o: w_ref[CONV_WIDTH - 1 - o:CONV_WIDTH - o, :], 0)
            av = a_ref[pl.ds(r0, CONV_SUB), :]
            sg = _sigmoid(g_ref[pl.ds(r0, CONV_SUB), :])
            du_ref[pl.ds(r0, CONV_SUB), 0:dc] = dh * sg
            du_ref[pl.ds(r0, CONV_SUB), dc:2 * dc] = dh * av * sg * (1.0 - sg)
            hwin = hs_ref[pl.ds(r0, CONV_WIN), :]
            dcv = dwin[0:CONV_SUB, :]
            for rho in range(8):
                offs = [o for o in range(2, 2 + CONV_WIDTH) if o % 8 == rho]
                rolled = hwin if rho == 0 else pltpu.roll(hwin, CONV_WIN - rho, 0)
                for o in offs:
                    m8 = o - rho
                    dw_ref[o - 2:o - 1, :] += _rowsum(dcv * rolled[m8:m8 + CONV_SUB, :])
            return carry

        lax.fori_loop(0, tc // CONV_SUB, sub2, 0)

    vec = pl.BlockSpec((1, dc), lambda t: (0, 0))
    prev = lambda col: (lambda t: (jnp.maximum(t * hb - 1, 0), col))
    nxt = lambda col: (lambda t: (jnp.minimum((t + 1) * hb, nhalo - 1), col))
    return pl.pallas_call(
        body, name="conv_bwd", grid=(r // tc,),
        in_specs=[pl.BlockSpec((tc, dc), lambda t: (t, 0)), pl.BlockSpec((tc, dc), lambda t: (t, 1)),
                  pl.BlockSpec((CONV_HALO, dc), prev(0)), pl.BlockSpec((CONV_HALO, dc), prev(1)),
                  pl.BlockSpec((tc, dc), lambda t: (t, 0)), pl.BlockSpec((tc, dc), lambda t: (t, 0)),
                  pl.BlockSpec((CONV_HALO, dc), nxt(0)), pl.BlockSpec((CONV_HALO, dc), nxt(0)),
                  pl.BlockSpec((32, dc), lambda t: (0, 0)), vec, vec],
        out_specs=[pl.BlockSpec((tc, 2 * dc), lambda t: (t, 0)), pl.BlockSpec((32, dc), lambda t: (0, 0)), vec, vec, vec],
        out_shape=[_sds((r, 2 * dc), F32), _sds((32, dc), F32), _sds((1, dc), F32), _sds((1, dc), F32), _sds((1, dc), F32)],
        scratch_shapes=[pltpu.VMEM((CONV_HALO + tc, dc), F32), pltpu.VMEM((tc + CONV_HALO, dc), F32)],
        compiler_params=_params(("arbitrary",)),
    )(u, u, u, u, c, dmi, c, dmi, w32, cg, cbe)


def _inproj_bwd(dp1, dus, xsrc, g_in, w_in, dep, tp, seq, tx):
    r, d = dp1.shape
    widths = [x.shape[1] for x in dus]
    offs = [sum(widths[:k]) for k in range(len(widths))]
    n = w_in.shape[0]
    nd = len(dus)
    head = tx == 0
    rows = X_OFF if head else tx

    def body(*refs):
        dp_ref = refs[0]
        du_refs = refs[1:1 + nd]
        x_ref, g_ref, w_ref, _, out_ref, dg_ref, db_ref = refs[1 + nd:]
        i = pl.program_id(0)

        @pl.when(i == 0)
        def _():
            dg_ref[...] = jnp.zeros_like(dg_ref)
            db_ref[...] = jnp.zeros_like(db_ref)
            if head:
                out_ref[...] = jnp.zeros_like(out_ref)

        ds0 = ALPHA * dp_ref[...]
        for k in range(nd):
            ds0 = ds0 + _mm(du_refs[k][...].astype(BF16), w_ref[offs[k]:offs[k] + widths[k], :])
        if head:
            ds0 = jnp.where(lax.broadcasted_iota(jnp.int32, (X_OFF, 1), 0) >= PAD_FRONT, ds0, 0.0)
        xhat, rstd = _ln(x_ref[...])
        dg_ref[...] += _rowsum(ds0 * xhat)
        db_ref[...] += _rowsum(ds0)
        dx = _ln_bwd(ds0 * g_ref[...], xhat, rstd)
        if head:
            out_ref[...] += dx[PAD_FRONT:X_OFF, :]
        else:
            out_ref[...] = dx

    if head:
        nb = tp // X_OFF
        row = lambda w: pl.BlockSpec((X_OFF, w), lambda i: (i * nb, 0))
        xspec = pl.BlockSpec((X_OFF, d), lambda i: (0, 0))
        ospec, oshape, steps = pl.BlockSpec((N_META, d), lambda i: (0, 0)), _sds((N_META, d), F32), r // tp
    else:
        start = _x_tile_row(tp, seq, tx)
        row = lambda w: pl.BlockSpec((pl.Element(tx), pl.Element(w)), lambda i: (start(i), 0))
        xspec = pl.BlockSpec((tx, d), lambda i: (i, 0))
        ospec, oshape, steps = xspec, _sds(xsrc.shape, F32), xsrc.shape[0] // tx
    vec = pl.BlockSpec((1, d), lambda i: (0, 0))
    return pl.pallas_call(
        body, name="inproj_bwd_head" if head else "inproj_bwd_x", grid=(steps,),
        in_specs=[row(d)] + [row(w) for w in widths] + [xspec, vec, pl.BlockSpec((n, d), lambda i: (0, 0)),
                                                        pl.BlockSpec(memory_space=pl.ANY)],
        out_specs=[ospec, vec, vec],
        out_shape=[oshape, _sds((1, d), F32), _sds((1, d), F32)],
        compiler_params=_params(("arbitrary",)),
    )(dp1, *dus, xsrc, g_in, w_in, dep)


def _inproj_bwd_w(s0, dus, tm):
    r, d = s0.shape
    widths = [x.shape[1] for x in dus]
    offs = [sum(widths[:k]) for k in range(len(widths))]
    nd = len(dus)

    def body(*refs):
        s_ref = refs[0]
        du_refs = refs[1:1 + nd]
        dw_ref = refs[1 + nd]
        i = pl.program_id(0)

        @pl.when(i == 0)
        def _():
            dw_ref[...] = jnp.zeros_like(dw_ref)

        sb = s_ref[...].astype(BF16)
        for k in range(nd):
            dw_ref[offs[k]:offs[k] + widths[k], :] += _mm_tn(du_refs[k][...].astype(BF16), sb)

    row = lambda w: pl.BlockSpec((tm, w), lambda i: (i, 0))
    return pl.pallas_call(
        body, name="inproj_bwd_w", grid=(r // tm,),
        in_specs=[row(d)] + [row(w) for w in widths],
        out_specs=pl.BlockSpec((sum(widths), d), lambda i: (0, 0)),
        out_shape=_sds((sum(widths), d), F32),
        compiler_params=_params(("arbitrary",)),
    )(s0, *dus)


def _local_step(x, tgt, meta, ln_in_g, ln_in_b, w_in, conv_w, conv_b, conv_ln_g, conv_ln_b, gate_up, gate_bias,
                gla_norm_g, late_weights, ln1_g, ln1_b, ln2_g, ln2_b, push):
    bsz, seq, d = x.shape
    tp = X_OFF + seq
    assert tp % CHUNK == 0
    nc = tp // CHUNK
    r = bsz * tp
    dc = conv_b.shape[1]
    tm = _pick_tile(tp, (352, 128, 64))
    tc = _pick_tile(tp, (704, 128, 64))

    x2 = x.reshape(bsz * seq, d)
    head = jnp.pad(meta, ((PAD_FRONT, 0), (0, 0)))
    tx = _pick_tile(seq, (512, 64))
    tgt_p = tgt.reshape(bsz * seq, d)
    w32 = jnp.pad(conv_w, ((0, 32 - CONV_WIDTH), (0, 0)))
    gup = jnp.pad(gate_up, ((0, LANES - GLA_RANK), (0, 0))).astype(BF16)

    s0, u = _inproj_fwd_x(x2, ln_in_g, ln_in_b, w_in, tp, seq, tx)
    s0, u = _inproj_fwd_head(head, ln_in_g, ln_in_b, w_in, s0, u, tp)
    c, co = _conv_fwd(u, w32, conv_b, conv_ln_g, conv_ln_b, tp, tc, dc)
    kc = _pick_tile(nc, (3, 2, 1))
    go, sta = _gla_fwd(u, gup, gate_bias, gla_norm_g, bsz, nc, kc)
    w_out, w1g, w2 = late_weights(go)
    nh = w1g.shape[0]
    tmm = _pick_tile(tp, (704, 128, 64))
    ns = 2
    p1, s1, s1b = _outproj_fwd(s0, co, go, w_out, ln1_g, ln1_b, tmm)
    hm, dp2, dpb, loss, dg2, db2 = _mlp_fwd(s1, s1b, w1g, w2, ln2_g, ln2_b, tgt_p, tp, tmm, ns)

    dh, dp1, dg1, db1 = _mlp_bwd_act(dp2, dpb, hm, w1g, w2, p1, ln1_g, tmm, ns)
    dw1, dw2 = _mlp_bwd_w(s1b, hm, dh, dpb, nh, tmm, ns)
    tok = push("ff", (dw1, dw2))
    dmi, dwo = _outproj_bwd(dp1, co, go, w_out, tok, tmm)
    tok = push("out", (dwo,))
    dqk, dv, dr, dgd, dgn, dgb, dgup = _gla_bwd(u, dmi, sta, gup, gate_bias, gla_norm_g, tok, bsz, nc, kc)
    dcv, dcw, dcb, dcg, dcbe = _conv_bwd(u, c, dmi, w32, conv_ln_g, conv_ln_b, tp, tc, dc)
    dus = [dcv, dqk, dv, dr, dgd]
    dwi = _inproj_bwd_w(s0, dus, tm)
    tok = push("in", (dwi,))
    gx, dgx, dbx = _inproj_bwd(dp1, dus, x2, ln_in_g, w_in, tok, tp, seq, tx)
    dmeta, dgh, dbh = _inproj_bwd(dp1, dus, head, ln_in_g, w_in, tok, tp, seq, 0)

    return dict(loss=loss[0, 0], grad_x=gx.reshape(bsz, seq, d), meta_tokens=dmeta, ln_in_g=dgx + dgh, ln_in_b=dbx + dbh,
                conv_w=dcw[:CONV_WIDTH], conv_b=dcb, conv_ln_g=dcg, conv_ln_b=dcbe,
                gate_up=dgup[:GLA_RANK], gate_bias=dgb, gla_norm_g=dgn, ln1_g=dg1, ln1_b=db1, ln2_g=dg2, ln2_b=db2)


def _exchange(arrays, scatter, name):
    na = len(arrays)
    npeer = N_DEV - 1

    def body(*refs):
        srcs = refs[:na]
        outs = refs[na:2 * na]
        send_sems, recv_sems, local_sems = refs[2 * na:]
        xi, yi, ci = (lax.axis_index(a) for a in MESH_AXES)
        me = 4 * xi + 2 * yi + ci
        copies = []
        for a in range(na):
            own = srcs[a].at[me] if scatter[a] else srcs[a]
            cp = pltpu.make_async_copy(own, outs[a].at[me], local_sems.at[a])
            cp.start()
            copies.append(cp)
        remote = []
        for k in range(1, N_DEV):
            px, py, pc = xi ^ (k >> 2), yi ^ ((k >> 1) & 1), ci ^ (k & 1)
            peer = 4 * px + 2 * py + pc
            for a in range(na):
                src = srcs[a].at[peer] if scatter[a] else srcs[a]
                cp = pltpu.make_async_remote_copy(
                    src_ref=src, dst_ref=outs[a].at[me],
                    send_sem=send_sems.at[a * npeer + k - 1], recv_sem=recv_sems.at[a * npeer + k - 1],
                    device_id=(px, py, pc), device_id_type=pl.DeviceIdType.MESH)
                cp.start()
                remote.append(cp)
        for cp in remote:
            cp.wait()
        for cp in copies:
            cp.wait()

    out_shape = [_sds(a.shape if scatter[i] else (N_DEV,) + a.shape, a.dtype) for i, a in enumerate(arrays)]
    anyspec = pl.BlockSpec(memory_space=pl.ANY)
    return pl.pallas_call(
        body, name=name,
        in_specs=[anyspec] * na, out_specs=[anyspec] * na, out_shape=out_shape,
        scratch_shapes=[pltpu.SemaphoreType.DMA((na * npeer,)), pltpu.SemaphoreType.DMA((na * npeer,)),
                        pltpu.SemaphoreType.DMA((na,))],
    )(*arrays)


def _peers(xi, yi, ci):
    for k in range(1, N_DEV):
        px, py, pc = xi ^ (k >> 2), yi ^ ((k >> 1) & 1), ci ^ (k & 1)
        yield (px, py, pc), 4 * px + 2 * py + pc


def _sc_exchange(arrays, scatter, name, collective_id, after=None):
    na = len(arrays)
    npeer = N_DEV - 1
    ndep = 0 if after is None else 1

    def body(*refs):
        srcs = refs[:na]
        outs = refs[na + ndep:2 * na + ndep]
        send_sems, recv_sems, own_sems = refs[2 * na + ndep:]
        xi, yi, ci = (lax.axis_index(a) for a in MESH_AXES)
        me = 4 * xi + 2 * yi + ci
        barrier = pltpu.get_barrier_semaphore()
        for pos, _ in _peers(xi, yi, ci):
            pl.semaphore_signal(barrier, inc=1, device_id=pos, device_id_type=pl.DeviceIdType.MESH)
        pl.semaphore_wait(barrier, npeer)
        own = [pltpu.make_async_copy(srcs[a].at[me] if scatter[a] else srcs[a], outs[a].at[me], own_sems.at[a])
               for a in range(na)]
        for cp in own:
            cp.start()
        remote = []
        for a in range(na):
            for k, (pos, peer) in enumerate(_peers(xi, yi, ci)):
                cp = pltpu.make_async_remote_copy(
                    src_ref=srcs[a].at[peer] if scatter[a] else srcs[a], dst_ref=outs[a].at[me],
                    send_sem=send_sems.at[a * npeer + k], recv_sem=recv_sems.at[a * npeer + k],
                    device_id=pos, device_id_type=pl.DeviceIdType.MESH)
                cp.start()
                remote.append(cp)
        for cp in own:
            cp.wait()
        for cp in remote:
            cp.wait()

    out_type = [_sds(a.shape if scatter[i] else (N_DEV,) + a.shape, a.dtype) for i, a in enumerate(arrays)]
    sent = sum(a.size * a.dtype.itemsize // (N_DEV if scatter[i] else 1) for i, a in enumerate(arrays))
    return pl.kernel(
        body, out_type=out_type, mesh=plsc.ScalarSubcoreMesh(axis_name="seq", num_cores=1), name=name,
        scratch_types=[pltpu.SemaphoreType.DMA((na * npeer,)), pltpu.SemaphoreType.DMA((na * npeer,)),
                       pltpu.SemaphoreType.DMA((na,))],
        compiler_params=pltpu.CompilerParams(collective_id=collective_id),
        cost_estimate=pl.CostEstimate(flops=0, transcendentals=0, bytes_accessed=2 * N_DEV * sent,
                                      remote_bytes_transferred=npeer * sent),
    )(*arrays, *([] if after is None else [after]))


def _sc_gather(arrays, name, collective_id, after=None):
    na = len(arrays)
    ndep = 0 if after is None else 1
    npair = N_DEV - 1

    def body(*refs):
        srcs = refs[:na]
        outs = refs[na + ndep:2 * na + ndep]
        send_sems, recv_sems, own_sems = refs[2 * na + ndep:]
        xi, yi, ci = (lax.axis_index(a) for a in MESH_AXES)
        me = 4 * xi + 2 * yi + ci
        sibling = (xi, yi, 1 - ci)
        chips = [(1 - xi, yi), (xi, 1 - yi), (1 - xi, 1 - yi)]
        barrier = pltpu.get_barrier_semaphore()
        for pos, _ in _peers(xi, yi, ci):
            pl.semaphore_signal(barrier, inc=1, device_id=pos, device_id_type=pl.DeviceIdType.MESH)
        pl.semaphore_wait(barrier, npair)

        def copy(a, k, src, slot, to):
            return pltpu.make_async_remote_copy(
                src_ref=src, dst_ref=outs[a].at[slot], send_sem=send_sems.at[a * npair + k],
                recv_sem=recv_sems.at[a * npair + k], device_id=to, device_id_type=pl.DeviceIdType.MESH)

        own = [pltpu.make_async_copy(srcs[a], outs[a].at[me], own_sems.at[a]) for a in range(na)]
        for cp in own:
            cp.start()
        sent = []
        for a in range(na):
            sent.append(copy(a, 0, srcs[a], me, sibling))
            sent += [copy(a, 1 + j, srcs[a], me, (*chip, ci)) for j, chip in enumerate(chips)]
        for cp in sent:
            cp.start()
        for j, (cx, cy) in enumerate(chips):
            slot = 4 * cx + 2 * cy + ci
            for a in range(na):
                copy(a, 1 + j, srcs[a], slot, sibling).wait_recv()
                cp = copy(a, 4 + j, outs[a].at[slot], slot, sibling)
                cp.start()
                sent.append(cp)
        for a in range(na):
            copy(a, 0, srcs[a], me, sibling).wait_recv()
            for j in range(len(chips)):
                copy(a, 4 + j, srcs[a], me, sibling).wait_recv()
        for cp in sent:
            cp.wait_send()
        for cp in own:
            cp.wait()

    out_type = [_sds((N_DEV,) + a.shape, a.dtype) for a in arrays]
    sent_bytes = sum(a.size * a.dtype.itemsize for a in arrays)
    return pl.kernel(
        body, out_type=out_type, mesh=plsc.ScalarSubcoreMesh(axis_name="seq", num_cores=1), name=name,
        scratch_types=[pltpu.SemaphoreType.DMA((na * npair,)), pltpu.SemaphoreType.DMA((na * npair,)),
                       pltpu.SemaphoreType.DMA((na,))],
        compiler_params=pltpu.CompilerParams(collective_id=collective_id),
        cost_estimate=pl.CostEstimate(flops=0, transcendentals=0, bytes_accessed=2 * N_DEV * sent_bytes,
                                      remote_bytes_transferred=npair * sent_bytes),
    )(*arrays, *([] if after is None else [after]))


def _push_start(arrays, scatter, name, dep=None):
    na = len(arrays)
    shapes = [a.shape if scatter[i] else (N_DEV,) + a.shape for i, a in enumerate(arrays)]
    hbm = pl.BlockSpec(memory_space=pltpu.HBM)
    sem = pl.BlockSpec(memory_space=pltpu.SEMAPHORE)
    ndep = 0 if dep is None else 1

    def body(*refs):
        srcs = refs[:na]
        lands = refs[na:2 * na]
        send_sems, recv_sems = refs[2 * na + ndep:2 * na + ndep + 2]
        own_sems = refs[4 * na + ndep + 2]
        xi, yi, ci = (lax.axis_index(a) for a in MESH_AXES)
        me = 4 * xi + 2 * yi + ci
        own = [pltpu.make_async_copy(srcs[a].at[me] if scatter[a] else srcs[a], lands[a].at[me], own_sems.at[a])
               for a in range(na)]
        for cp in own:
            cp.start()
        for cp in own:
            cp.wait()
        for a in range(na):
            for pos, peer in _peers(xi, yi, ci):
                pltpu.make_async_remote_copy(
                    src_ref=srcs[a].at[peer] if scatter[a] else srcs[a], dst_ref=lands[a].at[me],
                    send_sem=send_sems.at[a], recv_sem=recv_sems.at[a],
                    device_id=pos, device_id_type=pl.DeviceIdType.MESH).start()

    ins = [pltpu.with_memory_space_constraint(a, pltpu.HBM) for a in arrays]
    ins += [pltpu.with_memory_space_constraint(lax.empty(s, a.dtype), pltpu.HBM) for s, a in zip(shapes, arrays)]
    res = pl.pallas_call(
        body, name=name,
        in_specs=[hbm] * (2 * na) + [pl.BlockSpec(memory_space=pl.ANY)] * ndep,
        out_specs=[sem, sem] + [hbm] * (2 * na),
        out_shape=[pltpu.SemaphoreType.DMA((na,)), pltpu.SemaphoreType.DMA((na,))]
                  + [pltpu.HBM(a.shape, a.dtype) for a in arrays] + [pltpu.HBM(s, a.dtype) for s, a in zip(shapes, arrays)],
        input_output_aliases={i: 2 + i for i in range(2 * na)},
        scratch_shapes=[pltpu.SemaphoreType.DMA((na,))],
        compiler_params=pltpu.CompilerParams(has_side_effects=pltpu.SideEffectType.DATAFLOW_SIDE_EFFECTING),
    )(*ins, *([] if dep is None else [dep]))
    return (res[0], res[1], list(res[2:2 + na]), list(res[2 + na:2 + 2 * na])), res[2]


def _push_wait(handle, after, name):
    send_sems, recv_sems, srcs, lands = handle
    na = len(srcs)
    hbm = pl.BlockSpec(memory_space=pltpu.HBM)
    sem = pl.BlockSpec(memory_space=pltpu.SEMAPHORE)

    def body(*refs):
        land_refs = refs[na:2 * na]
        send_ref, recv_ref = refs[2 * na:2 * na + 2]
        me = tuple(lax.axis_index(a) for a in MESH_AXES)
        for a in range(na):
            seven = land_refs[a].at[pl.ds(0, N_DEV - 1)]
            cp = pltpu.make_async_remote_copy(src_ref=seven, dst_ref=seven, send_sem=send_ref.at[a], recv_sem=recv_ref.at[a],
                                              device_id=me, device_id_type=pl.DeviceIdType.MESH)
            cp.wait_send()
            cp.wait_recv()

    res = pl.pallas_call(
        body, name=name,
        in_specs=[hbm] * (2 * na) + [sem, sem, pl.BlockSpec(memory_space=pl.ANY)],
        out_specs=[hbm] * (2 * na),
        out_shape=[pltpu.HBM(a.shape, a.dtype) for a in srcs] + [pltpu.HBM(a.shape, a.dtype) for a in lands],
        input_output_aliases={i: i for i in range(2 * na)},
        compiler_params=pltpu.CompilerParams(has_side_effects=pltpu.SideEffectType.DATAFLOW_SIDE_EFFECTING),
    )(*srcs, *lands, send_sems, recv_sems, after)
    return list(res[na:])


def _adamw(w, g, m, v):
    m = ADAM_B1 * m + (1.0 - ADAM_B1) * g
    v = ADAM_B2 * v + (1.0 - ADAM_B2) * jnp.square(g)
    m_hat = m / (1.0 - ADAM_B1 ** ADAM_STEP)
    v_hat = v / (1.0 - ADAM_B2 ** ADAM_STEP)
    delta = -ADAM_LR * (m_hat / (jnp.sqrt(v_hat) + ADAM_EPS) + ADAM_WD * w)
    return delta, m, v


def _sum_devices(ref):
    g = ref[0].astype(F32)
    for k in range(1, N_DEV):
        g = g + ref[k].astype(F32)
    return g


def _update_big(parts, w, m, v, name):
    rows, cols = w.shape

    def body(p_ref, w_ref, m_ref, v_ref, g_ref, d_ref, nm_ref, nv_ref):
        g = _sum_devices(p_ref)
        g_ref[...] = g
        d_ref[...], nm_ref[...], nv_ref[...] = _adamw(w_ref[...], g, m_ref[...], v_ref[...])

    if rows % 16 == 0:
        tr = _pick_tile(rows, (128, 64, 16))
        steps, blk = rows // tr, pl.BlockSpec((tr, cols), lambda i: (i, 0))
        pblk = pl.BlockSpec((N_DEV, tr, cols), lambda i: (0, i, 0))
    else:
        tcol = 2 * LANES
        steps, blk = cols // tcol, pl.BlockSpec((rows, tcol), lambda i: (0, i))
        pblk = pl.BlockSpec((N_DEV, rows, tcol), lambda i: (0, 0, i))
    return pl.pallas_call(
        body, name=name, grid=(steps,),
        in_specs=[pblk, blk, blk, blk],
        out_specs=[blk] * 4, out_shape=[_sds((rows, cols), F32)] * 4,
        compiler_params=_params(("parallel",)),
    )(parts, w, m, v)


_VEC_ORDER = ("ln_in_g", "ln_in_b", "conv_b", "conv_ln_g", "conv_ln_b", "gate_bias", "gla_norm_g",
              "ln1_g", "ln1_b", "ln2_g", "ln2_b")
_SHARDED_SMALL = (("meta_tokens", 0, N_META, LANES), ("conv_w", N_META, CONV_WIDTH, None), ("gate_up", N_META + 32, GLA_RANK, None))


def _update_small(parts_sh, parts_vec, wmv):
    names = [s[0] for s in _SHARDED_SMALL] + list(_VEC_ORDER)
    flat = [a for nme in names for a in wmv[nme]]
    nv = len(_VEC_ORDER)

    def body(*refs):
        sh_ref, vec_ref = refs[0], refs[1]
        ins = refs[2:2 + len(flat)]
        outs = refs[2 + len(flat):2 + len(flat) + 4 * len(names)]
        loss_ref = refs[2 + len(flat) + 4 * len(names)]
        gsh_ref, gvec_ref = refs[-2:]
        gsh_ref[...] = _sum_devices(sh_ref)
        gvec_ref[...] = _sum_devices(vec_ref)
        loss_ref[...] = gvec_ref[nv:nv + 1, :]
        for idx, nme in enumerate(names):
            w_ref, m_ref, v_ref = ins[3 * idx:3 * idx + 3]
            rows, cols = w_ref.shape
            if idx < len(_SHARDED_SMALL):
                r0 = _SHARDED_SMALL[idx][1]
                g = gsh_ref[r0:r0 + rows, 0:cols]
            else:
                j = idx - len(_SHARDED_SMALL)
                g = gvec_ref[j:j + 1, 0:cols]
            o = outs[4 * idx:4 * idx + 4]
            o[0][...] = g
            o[1][...], o[2][...], o[3][...] = _adamw(w_ref[...], g, m_ref[...], v_ref[...])

    out_shape = [_sds(wmv[nme][0].shape, F32) for nme in names for _ in range(4)] + [_sds((1, parts_vec.shape[2]), F32)]
    vmem = pl.BlockSpec(memory_space=pltpu.VMEM)
    res = pl.pallas_call(
        body, name="update_small", out_shape=out_shape,
        in_specs=[vmem] * (2 + len(flat)), out_specs=[vmem] * len(out_shape),
        scratch_shapes=[pltpu.VMEM(parts_sh.shape[1:], F32), pltpu.VMEM(parts_vec.shape[1:], F32)],
    )(parts_sh, parts_vec, *flat)
    return {nme: res[4 * i:4 * i + 4] for i, nme in enumerate(names)}, res[-1][0, 0]


_WEIGHTS = ("meta_tokens", "ln_in_g", "ln_in_b", "w_in", "conv_w", "conv_b", "conv_ln_g", "conv_ln_b", "gate_up",
            "gate_bias", "gla_norm_g", "w_out", "ln1_g", "ln1_b", "w_ff1", "w_ff2", "ln2_g", "ln2_b")


def kernel(x, meta_tokens, ln_in_g, ln_in_b, w_in, conv_w, conv_b, conv_ln_g, conv_ln_b, gate_up, gate_bias, gla_norm_g, w_out, ln1_g, ln1_b, w_ff1, w_ff2, ln2_g, ln2_b, loss_target, m_meta_tokens, m_ln_in_g, m_ln_in_b, m_w_in, m_conv_w, m_conv_b, m_conv_ln_g, m_conv_ln_b, m_gate_up, m_gate_bias, m_gla_norm_g, m_w_out, m_ln1_g, m_ln1_b, m_w_ff1, m_w_ff2, m_ln2_g, m_ln2_b, v_meta_tokens, v_ln_in_g, v_ln_in_b, v_w_in, v_conv_w, v_conv_b, v_conv_ln_g, v_conv_ln_b, v_gate_up, v_gate_bias, v_gla_norm_g, v_w_out, v_ln1_g, v_ln1_b, v_w_ff1, v_w_ff2, v_ln2_g, v_ln2_b):
    w = dict(meta_tokens=meta_tokens, ln_in_g=ln_in_g, ln_in_b=ln_in_b, w_in=w_in, conv_w=conv_w, conv_b=conv_b,
             conv_ln_g=conv_ln_g, conv_ln_b=conv_ln_b, gate_up=gate_up, gate_bias=gate_bias, gla_norm_g=gla_norm_g,
             w_out=w_out, ln1_g=ln1_g, ln1_b=ln1_b, w_ff1=w_ff1, w_ff2=w_ff2, ln2_g=ln2_g, ln2_b=ln2_b)
    mom = dict(meta_tokens=m_meta_tokens, ln_in_g=m_ln_in_g, ln_in_b=m_ln_in_b, w_in=m_w_in, conv_w=m_conv_w,
               conv_b=m_conv_b, conv_ln_g=m_conv_ln_g, conv_ln_b=m_conv_ln_b, gate_up=m_gate_up, gate_bias=m_gate_bias,
               gla_norm_g=m_gla_norm_g, w_out=m_w_out, ln1_g=m_ln1_g, ln1_b=m_ln1_b, w_ff1=m_w_ff1, w_ff2=m_w_ff2,
               ln2_g=m_ln2_g, ln2_b=m_ln2_b)
    var = dict(meta_tokens=v_meta_tokens, ln_in_g=v_ln_in_g, ln_in_b=v_ln_in_b, w_in=v_w_in, conv_w=v_conv_w,
               conv_b=v_conv_b, conv_ln_g=v_conv_ln_g, conv_ln_b=v_conv_ln_b, gate_up=v_gate_up, gate_bias=v_gate_bias,
               gla_norm_g=v_gla_norm_g, w_out=v_w_out, ln1_g=v_ln1_g, ln1_b=v_ln1_b, w_ff1=v_w_ff1, w_ff2=v_w_ff2,
               ln2_g=v_ln2_g, ln2_b=v_ln2_b)
    shapes = {k: a.shape for k, a in w.items()}

    def two_d(a):
        return a.reshape(1, -1) if a.ndim == 1 else a.reshape(a.shape[-2:])

    w2d = {k: two_d(a) for k, a in w.items()}
    m2d = {k: two_d(a) for k, a in mom.items()}
    v2d = {k: two_d(a) for k, a in var.items()}
    d = x.shape[-1]
    d_in = w2d["w_in"].shape[1] * N_DEV
    d_in_p = -(-d_in // LANES) * LANES

    for dct in (w2d, m2d, v2d):
        dct["w_in"] = dct["w_in"].T
    g_in, g_meta, g_conv, g_gup = _sc_gather(
        [w2d["w_in"].astype(BF16), w2d["meta_tokens"], w2d["conv_w"], w2d["gate_up"]], "gather_first", 0)
    g_out, g_ff1, g_ff2 = _sc_gather(
        [w2d["w_out"].astype(BF16), w2d["w_ff1"].astype(BF16), w2d["w_ff2"].astype(BF16)], "gather_late", 1, after=g_gup)
    w_in_full = jnp.pad(g_in.reshape(d_in, d), ((0, d_in_p - d_in), (0, 0)))
    meta_full = g_meta.transpose(1, 0, 2).reshape(N_META, d)
    conv_w_full = g_conv.transpose(1, 0, 2).reshape(CONV_WIDTH, -1)
    gate_up_full = g_gup.transpose(1, 0, 2).reshape(GLA_RANK, -1)

    def late_weights(after):
        return g_out.reshape(-1, d), g_ff1, g_ff2.reshape(-1, d)

    pushed = {}

    def push(tag, grads):
        if tag == "ff":
            pushed["ff1"], pushed["ff2"] = _sc_exchange(list(grads), [True, True], "scatter_ff", 2)
        elif tag == "out":
            pushed["p_out"] = grads[0].reshape(N_DEV, -1, d)
        else:
            p_in = grads[0][:d_in].reshape(N_DEV, d_in // N_DEV, d).astype(BF16)
            pushed["in"], pushed["out"] = _sc_exchange([p_in, pushed["p_out"]], [True, True], "scatter_rest", 3,
                                                       after=pushed["ff1"])
        return grads[0]

    res = _local_step(x, loss_target, meta_full, w2d["ln_in_g"], w2d["ln_in_b"], w_in_full, conv_w_full, w2d["conv_b"],
                      w2d["conv_ln_g"], w2d["conv_ln_b"], gate_up_full, w2d["gate_bias"], w2d["gla_norm_g"], late_weights,
                      w2d["ln1_g"], w2d["ln1_b"], w2d["ln2_g"], w2d["ln2_b"], push)

    dc = res["conv_w"].shape[1]
    hk = res["gate_up"].shape[1]
    sh_meta = res["meta_tokens"].reshape(N_META, N_DEV, LANES).transpose(1, 0, 2)
    sh_conv = jnp.pad(res["conv_w"].reshape(CONV_WIDTH, N_DEV, dc // N_DEV).transpose(1, 0, 2),
                      ((0, 0), (0, 32 - CONV_WIDTH), (0, LANES - dc // N_DEV)))
    sh_gup = jnp.pad(res["gate_up"].reshape(GLA_RANK, N_DEV, hk // N_DEV).transpose(1, 0, 2),
                     ((0, 0), (0, 0), (0, LANES - hk // N_DEV)))
    p_sh = jnp.concatenate([sh_meta, sh_conv, sh_gup], axis=1)
    p_vec = jnp.concatenate([jnp.pad(res[k], ((0, 0), (0, d - res[k].shape[1]))) for k in _VEC_ORDER]
                            + [jnp.full((1, d), res["loss"], F32), jnp.zeros((15 - len(_VEC_ORDER), d), F32)], axis=0)

    r_sh, r_vec = _exchange([p_sh, p_vec], [True, False], "scatter_small")
    r_ff1, r_ff2, r_out, r_in = pushed["ff1"], pushed["ff2"], pushed["out"], pushed["in"]

    upd = {}
    upd["w_in"] = [a.T for a in _update_big(r_in, w2d["w_in"], m2d["w_in"], v2d["w_in"], "update_w_in")]
    upd["w_out"] = _update_big(r_out, w2d["w_out"], m2d["w_out"], v2d["w_out"], "update_w_out")
    upd["w_ff1"] = _update_big(r_ff1, w2d["w_ff1"], m2d["w_ff1"], v2d["w_ff1"], "update_w_ff1")
    upd["w_ff2"] = _update_big(r_ff2, w2d["w_ff2"], m2d["w_ff2"], v2d["w_ff2"], "update_w_ff2")
    small = [s[0] for s in _SHARDED_SMALL] + list(_VEC_ORDER)
    upd_small, loss = _update_small(r_sh, r_vec, {k: (w2d[k], m2d[k], v2d[k]) for k in small})
    upd.update(upd_small)

    outs = [loss, res["grad_x"]]
    for j in range(4):
        outs += [upd[k][j].reshape(shapes[k]) for k in _WEIGHTS]
    return tuple(outs)
```

```python
import functools

import jax
import jax.numpy as jnp
from jax import lax
from jax.experimental import pallas as pl
from jax.experimental.pallas import tpu as pltpu
from jax.experimental.pallas import tpu_sc as plsc

F32 = jnp.float32
BF16 = jnp.bfloat16

N_META = 16
CHUNK = 64
PAD_FRONT = (-N_META) % CHUNK
X_OFF = PAD_FRONT + N_META
CONV_WIDTH = 31
CONV_HALO = 32
CONV_SUB = 64
CONV_WIN = CONV_SUB + CONV_HALO
GLA_HEADS = 4
GLA_DK = 64
GLA_DV = 128
GLA_RANK = 16
GLA_TAU = 16.0
QK_SCALE = GLA_DK ** -0.5
LN_EPS = 1e-5
ALPHA = 2.0 ** 0.25
LANES = 128
N_DEV = 8
ADAM_LR = 0.001
ADAM_B1 = 0.9
ADAM_B2 = 0.999
ADAM_EPS = 1e-08
ADAM_WD = 0.01
ADAM_STEP = 10
VMEM_LIMIT = 56 * 1024 * 1024
MESH_AXES = ("x", "y", "c")


def _sds(shape, dtype):
    return jax.ShapeDtypeStruct(shape, dtype)


def _mm(a, b):
    return jnp.dot(a, b, preferred_element_type=F32)


def _mm_nt(a, b):
    return lax.dot_general(a, b, (((1,), (1,)), ((), ())), preferred_element_type=F32)


def _mm_tn(a, b):
    return lax.dot_general(a, b, (((0,), (0,)), ((), ())), preferred_element_type=F32)


def _sigmoid(x):
    return 1.0 / (1.0 + jnp.exp(-x))


def _log_sigmoid(z):
    return jnp.minimum(z, 0.0) - jnp.log(1.0 + jnp.exp(-jnp.abs(z)))


def _ln(x):
    mu = jnp.mean(x, axis=-1, keepdims=True)
    xc = x - mu
    var = jnp.mean(xc * xc, axis=-1, keepdims=True)
    rstd = lax.rsqrt(var + LN_EPS)
    return xc * rstd, rstd


def _ln_bwd(dyg, xhat, rstd):
    m1 = jnp.mean(dyg, axis=-1, keepdims=True)
    m2 = jnp.mean(dyg * xhat, axis=-1, keepdims=True)
    return rstd * (dyg - m1 - xhat * m2)


def _rowsum(x):
    return jnp.sum(x, axis=0, keepdims=True)


def _row_in_seq(i, tm, tp):
    base = lax.rem(i * tm, tp)
    return base + lax.broadcasted_iota(jnp.int32, (tm, 1), 0)


def _split3(x):
    hi = x.astype(BF16)
    r1 = x - hi.astype(F32)
    mid = r1.astype(BF16)
    lo = (r1 - mid.astype(F32)).astype(BF16)
    return hi, mid, lo


def _tri_mm(tri, x):
    hi, mid, lo = _split3(x)
    return _mm(tri, hi) + _mm(tri, mid) + _mm(tri, lo)


def _params(sem):
    return pltpu.CompilerParams(dimension_semantics=sem, vmem_limit_bytes=VMEM_LIMIT)


def _pick_tile(n, prefs):
    for t in prefs:
        if n % t == 0:
            return t
    raise ValueError(f"no tile for {n}")


def _x_tile_row(tp, seq, tx):
    tps = seq // tx
    return lambda i: pl.multiple_of((i // tps) * tp + X_OFF + (i % tps) * tx, CHUNK)


def _ln_in_x(x2, g, b, tp, seq, tx):
    rx, d = x2.shape
    r = rx // seq * tp
    row = _x_tile_row(tp, seq, tx)

    def body(x_ref, g_ref, b_ref, s0_ref, sb_ref):
        xhat, _ = _ln(x_ref[...])
        s = xhat * g_ref[...] + b_ref[...]
        s0_ref[...] = s
        sb_ref[...] = s.astype(BF16)

    out = pl.BlockSpec((pl.Element(tx), pl.Element(d)), lambda i: (row(i), 0))
    return pl.pallas_call(
        body, name="ln_in_x", grid=(rx // tx,),
        in_specs=[pl.BlockSpec((tx, d), lambda i: (i, 0)), pl.BlockSpec((1, d), lambda i: (0, 0)),
                  pl.BlockSpec((1, d), lambda i: (0, 0))],
        out_specs=[out, out],
        out_shape=[_sds((r, d), F32), _sds((r, d), BF16)],
        compiler_params=_params(("parallel",)),
    )(x2, g, b)


def _ln_in_head(head, g, b, s0, s0b, tp):
    r, d = s0.shape
    nb = tp // X_OFF

    def body(h_ref, g_ref, b_ref, s0_in, sb_in, s0_ref, sb_ref):
        xhat, _ = _ln(h_ref[...])
        real = lax.broadcasted_iota(jnp.int32, (X_OFF, 1), 0) >= PAD_FRONT
        s = jnp.where(real, xhat * g_ref[...] + b_ref[...], 0.0)
        s0_ref[...] = s
        sb_ref[...] = s.astype(BF16)

    anyspec = pl.BlockSpec(memory_space=pl.ANY)
    out = pl.BlockSpec((X_OFF, d), lambda i: (i * nb, 0))
    return pl.pallas_call(
        body, name="ln_in_head", grid=(r // tp,),
        in_specs=[pl.BlockSpec((X_OFF, d), lambda i: (0, 0)), pl.BlockSpec((1, d), lambda i: (0, 0)),
                  pl.BlockSpec((1, d), lambda i: (0, 0)), anyspec, anyspec],
        out_specs=[out, out],
        out_shape=[_sds((r, d), F32), _sds((r, d), BF16)],
        input_output_aliases={3: 0, 4: 1},
        compiler_params=_params(("parallel",)),
    )(head, g, b, s0, s0b)


def _inproj_fwd(s0b, w_in, tm):
    r, d = s0b.shape
    n = w_in.shape[0]

    def body(s_ref, w_ref, u_ref):
        u_ref[...] = _mm_nt(s_ref[...], w_ref[...])

    return pl.pallas_call(
        body, name="inproj_fwd", grid=(r // tm,),
        in_specs=[pl.BlockSpec((tm, d), lambda i: (i, 0)), pl.BlockSpec((n, d), lambda i: (0, 0))],
        out_specs=pl.BlockSpec((tm, n), lambda i: (i, 0)),
        out_shape=_sds((r, n), F32),
        compiler_params=_params(("parallel",)),
    )(s0b, w_in)


def _conv_taps(win, coef, lo):
    acc = None
    for rho in range(8):
        offs = [o for o in range(lo, lo + CONV_WIDTH) if o % 8 == rho]
        if not offs:
            continue
        rolled = win if rho == 0 else pltpu.roll(win, CONV_WIN - rho, 0)
        for o in offs:
            m8 = o - rho
            term = rolled[m8:m8 + CONV_SUB, :] * coef(o)
            acc = term if acc is None else acc + term
    return acc


def _conv_fwd(u, w32, cb, cg, cbe, tp, tc, dc):
    r = u.shape[0]
    hb = tc // CONV_HALO

    def body(a_ref, g_ref, ah_ref, gh_ref, w_ref, cb_ref, cg_ref, cbe_ref, c_ref, co_ref, hs_ref):
        t = pl.program_id(0)
        first = lax.rem(t * tc, tp) == 0
        hh = ah_ref[...] * _sigmoid(gh_ref[...])
        hs_ref[0:CONV_HALO, :] = jnp.where(first, 0.0, hh)
        hs_ref[CONV_HALO:CONV_HALO + tc, :] = a_ref[...] * _sigmoid(g_ref[...])

        def sub(k, carry):
            r0 = pl.multiple_of(k * CONV_SUB, CONV_SUB)
            win = hs_ref[pl.ds(r0, CONV_WIN), :]
            c = _conv_taps(win, lambda o: w_ref[o - 2:o - 1, :], 2) + cb_ref[...]
            c_ref[pl.ds(r0, CONV_SUB), :] = c
            xhat, _ = _ln(c)
            cn = xhat * cg_ref[...] + cbe_ref[...]
            co_ref[pl.ds(r0, CONV_SUB), :] = (cn * _sigmoid(cn)).astype(BF16)
            return carry

        lax.fori_loop(0, tc // CONV_SUB, sub, 0)

    vec = pl.BlockSpec((1, dc), lambda t: (0, 0))
    return pl.pallas_call(
        body, name="conv_fwd", grid=(r // tc,),
        in_specs=[pl.BlockSpec((tc, dc), lambda t: (t, 0)), pl.BlockSpec((tc, dc), lambda t: (t, 1)),
                  pl.BlockSpec((CONV_HALO, dc), lambda t: (jnp.maximum(t * hb - 1, 0), 0)),
                  pl.BlockSpec((CONV_HALO, dc), lambda t: (jnp.maximum(t * hb - 1, 0), 1)),
                  pl.BlockSpec((32, dc), lambda t: (0, 0)), vec, vec, vec],
        out_specs=[pl.BlockSpec((tc, dc), lambda t: (t, 0)), pl.BlockSpec((tc, dc), lambda t: (t, 0))],
        out_shape=[_sds((r, dc), F32), _sds((r, dc), BF16)],
        scratch_shapes=[pltpu.VMEM((CONV_HALO + tc, dc), F32)],
        compiler_params=_params(("parallel",)),
    )(u, u, u, u, w32, cb, cg, cbe)


def _tri_mm_all(tri, xs):
    parts = [_split3(x) for x in xs]
    acc = [None] * len(xs)
    for t in range(3):
        for j in range(len(xs)):
            term = _mm(tri, parts[j][t])
            acc[j] = term if t == 0 else acc[j] + term
    return acc


def _gla_prep(qk_ref, gd_ref, gup, gb, n0, kc):
    rows = [slice(j * CHUNK, (j + 1) * CHUNK) for j in range(kc)]
    ri = lax.broadcasted_iota(jnp.int32, (CHUNK, CHUNK), 0)
    ci = lax.broadcasted_iota(jnp.int32, (CHUNK, CHUNK), 1)
    low = (ri >= ci).astype(BF16)
    hk = GLA_HEADS * GLA_DK
    gds = [gd_ref[rw, :] for rw in rows]
    zs = [_mm(g.astype(BF16), gup) + gb for g in gds]
    reals = [(n0 + j) * CHUNK + lax.broadcasted_iota(jnp.int32, (CHUNK, 1), 0) >= PAD_FRONT for j in range(kc)]
    lgs = [jnp.where(reals[j], _log_sigmoid(zs[j]) * (1.0 / GLA_TAU), 0.0) for j in range(kc)]
    bs = _tri_mm_all(low, lgs)
    out = []
    for j in range(kc):
        b, bl = bs[j], _rowsum(lgs[j])
        q = qk_ref[rows[j], :hk] * QK_SCALE
        k = qk_ref[rows[j], hk:]
        eb, enb, ebl = jnp.exp(b), jnp.exp(-b), jnp.exp(bl - b)
        out.append(dict(rows=rows[j], gd=gds[j], z=zs[j], real=reals[j], eb=eb, enb=enb, ebl=ebl, gam=jnp.exp(bl),
                        qe=q * eb, ke=k * enb, kd=k * ebl))
    return out, ri, ci


def _gla_heads(p, v_ref):
    ops = []
    for h in range(GLA_HEADS):
        hp, h2 = divmod(h, 2)
        ls = slice(hp * LANES, (hp + 1) * LANES)
        m = _head_mask(h2)
        ops.append(dict(ls=ls, m=m, vs=slice(h * GLA_DV, (h + 1) * GLA_DV),
                        qe=jnp.where(m, p["qe"][:, ls], 0.0).astype(BF16),
                        kd=jnp.where(m, p["kd"][:, ls], 0.0).astype(BF16),
                        ke=p["ke"][:, ls].astype(BF16),
                        v=v_ref[p["rows"], h * GLA_DV:(h + 1) * GLA_DV].astype(BF16)))
    return ops


def _head_mask(h2):
    lane = lax.broadcasted_iota(jnp.int32, (1, LANES), 1)
    return (lane < GLA_DK) if h2 == 0 else (lane >= GLA_DK)


def _gla_fwd(u, gup, gb, gn, bsz, nc, kc):
    r = u.shape[0]
    hv = GLA_HEADS * GLA_DV
    ns = nc // kc

    def body(qk_ref, v_ref, r_ref, gd_ref, gup_ref, gb_ref, gn_ref, go_ref, sta_ref, st_ref):
        t = pl.program_id(1)

        @pl.when(t == 0)
        def _():
            st_ref[...] = jnp.zeros_like(st_ref)

        ps, ri, ci = _gla_prep(qk_ref, gd_ref, gup_ref[...], gb_ref[...], t * kc, kc)
        tril = ri >= ci
        items = [(j, h) for j in range(kc) for h in range(GLA_HEADS)]
        ops = [_gla_heads(p, v_ref) for p in ps]
        a = {jh: jnp.where(tril, _mm_nt(ops[jh[0]][jh[1]]["qe"], ops[jh[0]][jh[1]]["ke"]), 0.0).astype(BF16) for jh in items}
        oi = {jh: _mm(a[jh], ops[jh[0]][jh[1]]["v"]) for jh in items}
        inc = {jh: _mm_tn(ops[jh[0]][jh[1]]["v"], ops[jh[0]][jh[1]]["kd"]) for jh in items}
        sts = [st_ref[h] for h in range(GLA_HEADS)]
        for j, h in items:
            op, p = ops[j][h], ps[j]
            st = sts[h]
            sta_ref[j, h] = st
            o = oi[j, h] + _mm_nt(op["qe"], st.astype(BF16))
            sts[h] = st * p["gam"][:, op["ls"]] + inc[j, h]
            rs = lax.rsqrt(jnp.mean(o * o, axis=-1, keepdims=True) + LN_EPS)
            rr = r_ref[p["rows"], op["vs"]]
            go_ref[p["rows"], op["vs"]] = (o * rs * gn_ref[...] * (rr * _sigmoid(rr))).astype(BF16)
        for h in range(GLA_HEADS):
            st_ref[h] = sts[h]

    rowblk = lambda col: (lambda b, t: (b * ns + t, col))
    const = lambda b, t: (0, 0)
    return pl.pallas_call(
        body, name="gla_fwd", grid=(bsz, ns),
        in_specs=[pl.BlockSpec((kc * CHUNK, 512), rowblk(2)), pl.BlockSpec((kc * CHUNK, hv), rowblk(3)),
                  pl.BlockSpec((kc * CHUNK, hv), rowblk(4)), pl.BlockSpec((kc * CHUNK, LANES), rowblk(20)),
                  pl.BlockSpec((LANES, 256), const), pl.BlockSpec((1, 256), const), pl.BlockSpec((1, GLA_DV), const)],
        out_specs=[pl.BlockSpec((kc * CHUNK, hv), rowblk(0)),
                   pl.BlockSpec((kc, GLA_HEADS, LANES, LANES), lambda b, t: (b * ns + t, 0, 0, 0))],
        out_shape=[_sds((r, hv), BF16), _sds((bsz * nc, GLA_HEADS, LANES, LANES), F32)],
        scratch_shapes=[pltpu.VMEM((GLA_HEADS, LANES, LANES), F32)],
        compiler_params=_params(("parallel", "arbitrary")),
    )(u, u, u, u, gup, gb, gn)


def _outproj_fwd(s0, co, go, w_out, g1, b1, tm):
    r, d = s0.shape
    dc = co.shape[1]

    def body(s0_ref, co_ref, go_ref, w_ref, g_ref, b_ref, p1_ref, s1_ref, s1b_ref):
        nb = 4 if tm % 64 == 0 else 1
        blocks = [slice(k * (tm // nb), (k + 1) * (tm // nb)) for k in range(nb)]
        mixes = [_mm(co_ref[rows, :], w_ref[0:dc, :]) + _mm(go_ref[rows, :], w_ref[dc:2 * dc, :]) for rows in blocks]
        for rows, mix in zip(blocks, mixes):
            p1 = ALPHA * s0_ref[rows, :] + mix
            p1_ref[rows, :] = p1
            xhat, _ = _ln(p1)
            s1 = xhat * g_ref[...] + b_ref[...]
            s1_ref[rows, :] = s1
            s1b_ref[rows, :] = s1.astype(BF16)

    row = lambda w: pl.BlockSpec((tm, w), lambda i: (i, 0))
    vec = pl.BlockSpec((1, d), lambda i: (0, 0))
    return pl.pallas_call(
        body, name="outproj_fwd", grid=(r // tm,),
        in_specs=[row(d), row(dc), row(dc), pl.BlockSpec((2 * dc, d), lambda i: (0, 0)), vec, vec],
        out_specs=[row(d), row(d), row(d)],
        out_shape=[_sds((r, d), F32), _sds((r, d), F32), _sds((r, d), BF16)],
        compiler_params=_params(("parallel",)),
    )(s0, co, go, w_out, g1, b1)


def _mlp_fwd(s1, s1b, w1g, w2, g2, b2, tgt, tp, tm, ns):
    r, d = s1.shape
    nh, _, th = w1g.shape
    nj = nh // ns

    def body(s1_ref, sb_ref, w1_ref, w2_ref, g_ref, b_ref, t_ref, hm_ref, dp2_ref, dpb_ref, loss_ref, dg_ref, db_ref, acc_ref):
        i = pl.program_id(0)
        j = pl.program_id(1)

        @pl.when(jnp.logical_and(i == 0, j == 0))
        def _():
            loss_ref[...] = jnp.zeros_like(loss_ref)
            dg_ref[...] = jnp.zeros_like(dg_ref)
            db_ref[...] = jnp.zeros_like(db_ref)

        @pl.when(j == 0)
        def _():
            acc_ref[...] = jnp.zeros_like(acc_ref)

        def mlp_rows(rows):
            hs = [_mm(sb_ref[rows, :], w1_ref[s]) for s in range(ns)]
            acc = acc_ref[rows, :]
            for s in range(ns):
                hm_ref[rows, s * th:(s + 1) * th] = hs[s].astype(BF16)
                act = jnp.square(jnp.maximum(hs[s], 0.0))
                acc = acc + _mm(act.astype(BF16), w2_ref[s * th:(s + 1) * th, :])
            return acc

        @pl.when(j < nj - 1)
        def _():
            acc_ref[...] = mlp_rows(slice(None))

        @pl.when(j == nj - 1)
        def _():
            halves = [slice(0, tm // 2), slice(tm // 2, tm)]
            accs = [mlp_rows(rows) for rows in halves]
            isx = _row_in_seq(i, tm, tp) >= X_OFF
            tg = t_ref[...]
            tg = jnp.where(i == 0, pltpu.roll(tg, X_OFF, 0), tg)
            for rows, acc in zip(halves, accs):
                p2 = ALPHA * s1_ref[rows, :] + acc
                xhat, rstd = _ln(p2)
                s2 = xhat * g_ref[...] + b_ref[...]
                err = jnp.where(isx[rows], s2 - tg[rows], 0.0)
                loss_ref[...] += 0.5 * jnp.sum(jnp.mean(err * err, axis=-1, keepdims=True))
                dy = err * (1.0 / d)
                dg_ref[...] += _rowsum(dy * xhat)
                db_ref[...] += _rowsum(dy)
                dp2 = _ln_bwd(dy * g_ref[...], xhat, rstd)
                dp2_ref[rows, :] = dp2
                dpb_ref[rows, :] = dp2.astype(BF16)

    row = pl.BlockSpec((tm, d), lambda i, j: (i, 0))
    vec = pl.BlockSpec((1, d), lambda i, j: (0, 0))
    tgt_row = pl.BlockSpec((pl.Element(tm), pl.Element(d)),
                           lambda i, j: (pl.multiple_of(jnp.maximum(i * tm - X_OFF * ((i * tm) // tp + 1), 0), CHUNK), 0))
    return pl.pallas_call(
        body, name="mlp_fwd", grid=(r // tm, nj),
        in_specs=[row, row, pl.BlockSpec((ns, d, th), lambda i, j: (j, 0, 0)), pl.BlockSpec((ns * th, d), lambda i, j: (j, 0)),
                  vec, vec, tgt_row],
        out_specs=[pl.BlockSpec((tm, ns * th), lambda i, j: (i, j)), row, row,
                   pl.BlockSpec((8, LANES), lambda i, j: (0, 0)), vec, vec],
        out_shape=[_sds((r, nh * th), BF16), _sds((r, d), F32), _sds((r, d), BF16), _sds((8, LANES), F32),
                   _sds((1, d), F32), _sds((1, d), F32)],
        scratch_shapes=[pltpu.VMEM((tm, d), F32)],
        compiler_params=_params(("arbitrary", "arbitrary")),
    )(s1, s1b, w1g, w2, g2, b2, tgt)


def _mlp_bwd_act(dp2, dpb, hm, w1g, w2, p1, g1, tm, ns):
    r, d = dp2.shape
    nh, _, th = w1g.shape
    nj = nh // ns

    def body(dp2_ref, dpb_ref, hm_ref, w1_ref, w2_ref, p1_ref, g_ref, dh_ref, dp1_ref, dg_ref, db_ref, acc_ref):
        i = pl.program_id(0)
        j = pl.program_id(1)

        @pl.when(jnp.logical_and(i == 0, j == 0))
        def _():
            dg_ref[...] = jnp.zeros_like(dg_ref)
            db_ref[...] = jnp.zeros_like(db_ref)

        @pl.when(j == 0)
        def _():
            acc_ref[...] = jnp.zeros_like(acc_ref)

        def mlp_rows(rows):
            dacts = [_mm_nt(dpb_ref[rows, :], w2_ref[s * th:(s + 1) * th, :]) for s in range(ns)]
            acc = acc_ref[rows, :]
            for s in range(ns):
                cols = slice(s * th, (s + 1) * th)
                dh = (dacts[s] * (2.0 * jnp.maximum(hm_ref[rows, cols].astype(F32), 0.0))).astype(BF16)
                dh_ref[rows, cols] = dh
                acc = acc + _mm_nt(dh, w1_ref[s])
            return acc

        @pl.when(j < nj - 1)
        def _():
            acc_ref[...] = mlp_rows(slice(None))

        @pl.when(j == nj - 1)
        def _():
            halves = [slice(0, tm // 2), slice(tm // 2, tm)]
            accs = [mlp_rows(rows) for rows in halves]
            for rows, acc in zip(halves, accs):
                ds1 = ALPHA * dp2_ref[rows, :] + acc
                xhat, rstd = _ln(p1_ref[rows, :])
                dg_ref[...] += _rowsum(ds1 * xhat)
                db_ref[...] += _rowsum(ds1)
                dp1_ref[rows, :] = _ln_bwd(ds1 * g_ref[...], xhat, rstd)

    row = pl.BlockSpec((tm, d), lambda i, j: (i, 0))
    vec = pl.BlockSpec((1, d), lambda i, j: (0, 0))
    blk = pl.BlockSpec((tm, ns * th), lambda i, j: (i, j))
    return pl.pallas_call(
        body, name="mlp_bwd_act", grid=(r // tm, nj),
        in_specs=[row, row, blk, pl.BlockSpec((ns, d, th), lambda i, j: (j, 0, 0)),
                  pl.BlockSpec((ns * th, d), lambda i, j: (j, 0)), row, vec],
        out_specs=[blk, row, vec, vec],
        out_shape=[_sds((r, nh * th), BF16), _sds((r, d), F32), _sds((1, d), F32), _sds((1, d), F32)],
        scratch_shapes=[pltpu.VMEM((tm, d), F32)],
        compiler_params=_params(("arbitrary", "arbitrary")),
    )(dp2, dpb, hm, w1g, w2, p1, g1)


def _mlp_bwd_w(s1b, hm, dh, dpb, nh, tm, ns):
    r, d = s1b.shape
    th = hm.shape[1] // nh

    def body(s1_ref, hm_ref, dh_ref, dp2_ref, dw1_ref, dw2_ref, a1_ref, a2_ref):
        i = pl.program_id(1)

        @pl.when(i == 0)
        def _():
            a1_ref[...] = jnp.zeros_like(a1_ref)
            a2_ref[...] = jnp.zeros_like(a2_ref)

        for s in range(ns):
            a1_ref[s] += _mm_tn(s1_ref[...], dh_ref[:, s * th:(s + 1) * th])
        for s in range(ns):
            act = jnp.square(jnp.maximum(hm_ref[:, s * th:(s + 1) * th].astype(F32), 0.0)).astype(BF16)
            a2_ref[s] += _mm_tn(act, dp2_ref[...])

        @pl.when(i == pl.num_programs(1) - 1)
        def _():
            dw1_ref[...] = a1_ref[...].astype(BF16)
            dw2_ref[...] = a2_ref[...].astype(BF16)

    row = pl.BlockSpec((tm, d), lambda j, i: (i, 0))
    blk = pl.BlockSpec((tm, ns * th), lambda j, i: (i, j))
    return pl.pallas_call(
        body, name="mlp_bwd_w", grid=(nh // ns, r // tm),
        in_specs=[row, blk, blk, row],
        out_specs=[pl.BlockSpec((ns, d, th), lambda j, i: (j, 0, 0)), pl.BlockSpec((ns, th, d), lambda j, i: (j, 0, 0))],
        out_shape=[_sds((nh, d, th), BF16), _sds((nh, th, d), BF16)],
        scratch_shapes=[pltpu.VMEM((ns, d, th), F32), pltpu.VMEM((ns, th, d), F32)],
        compiler_params=_params(("parallel", "arbitrary")),
    )(s1b, hm, dh, dpb)


def _outproj_bwd(dp1, co, go, w_out, dep, tm):
    r, d = dp1.shape
    dc = co.shape[1]

    def body(dp_ref, co_ref, go_ref, w_ref, dep_ref, dmi_ref, dw_ref, acc_ref):
        i = pl.program_id(0)

        @pl.when(i == 0)
        def _():
            acc_ref[...] = jnp.zeros_like(acc_ref)

        dpb = dp_ref[...].astype(BF16)
        dmi_ref[...] = _mm_nt(dpb, w_ref[...])
        acc_ref[0:dc, :] += _mm_tn(co_ref[...], dpb)
        acc_ref[dc:2 * dc, :] += _mm_tn(go_ref[...], dpb)

        @pl.when(i == pl.num_programs(0) - 1)
        def _():
            dw_ref[...] = acc_ref[...].astype(BF16)

    row = lambda w: pl.BlockSpec((tm, w), lambda i: (i, 0))
    full = pl.BlockSpec((2 * dc, d), lambda i: (0, 0))
    return pl.pallas_call(
        body, name="outproj_bwd", grid=(r // tm,),
        in_specs=[row(d), row(dc), row(dc), full, pl.BlockSpec(memory_space=pl.ANY)],
        out_specs=[row(2 * dc), full],
        out_shape=[_sds((r, 2 * dc), F32), _sds((2 * dc, d), BF16)],
        scratch_shapes=[pltpu.VMEM((2 * dc, d), F32)],
        compiler_params=_params(("arbitrary",)),
    )(dp1, co, go, w_out, dep)


def _gla_bwd(u, dmi, sta, gup, gb, gn, dep, bsz, nc, kc):
    r = u.shape[0]
    hv = GLA_HEADS * GLA_DV
    hk = GLA_HEADS * GLA_DK
    ns = nc // kc

    def body(qk_ref, v_ref, r_ref, gd_ref, dgo_ref, sta_ref, gup_ref, gb_ref, gn_ref, dep_ref,
             dqk_ref, dv_ref, dr_ref, dgd_ref, dgn_ref, dgb_ref, dgup_ref, dst_ref):
        bi = pl.program_id(0)
        t = pl.program_id(1)

        @pl.when(jnp.logical_and(bi == 0, t == 0))
        def _():
            dgn_ref[...] = jnp.zeros_like(dgn_ref)
            dgb_ref[...] = jnp.zeros_like(dgb_ref)
            dgup_ref[...] = jnp.zeros_like(dgup_ref)

        @pl.when(t == 0)
        def _():
            dst_ref[...] = jnp.zeros_like(dst_ref)

        ps, ri, ci = _gla_prep(qk_ref, gd_ref, gup_ref[...], gb_ref[...], (ns - 1 - t) * kc, kc)
        tril = ri >= ci
        items = [(j, h) for j in reversed(range(kc)) for h in range(GLA_HEADS)]
        ops = [_gla_heads(p, v_ref) for p in ps]
        op = lambda jh: ops[jh[0]][jh[1]]
        st = {jh: sta_ref[jh[0], jh[1]] for jh in items}
        stb = {jh: st[jh].astype(BF16) for jh in items}
        a = {jh: jnp.where(tril, _mm_nt(op(jh)["qe"], op(jh)["ke"]), 0.0).astype(BF16) for jh in items}
        o1 = {jh: _mm(a[jh], op(jh)["v"]) for jh in items}
        o2 = {jh: _mm_nt(op(jh)["qe"], stb[jh]) for jh in items}
        dob = {}
        dgn = jnp.zeros((1, GLA_DV), F32)
        for jh in items:
            rows, vs = ps[jh[0]]["rows"], op(jh)["vs"]
            o = o1[jh] + o2[jh]
            rr = r_ref[rows, vs]
            sr = _sigmoid(rr)
            rs = lax.rsqrt(jnp.mean(o * o, axis=-1, keepdims=True) + LN_EPS)
            y = o * rs
            dgo = dgo_ref[rows, vs]
            don = dgo * (rr * sr)
            dr_ref[rows, vs] = dgo * (y * gn_ref[...]) * (sr * (1.0 + rr * (1.0 - sr)))
            dgn = dgn + _rowsum(don * y)
            dxn = don * gn_ref[...]
            dob[jh] = (rs * (dxn - y * jnp.mean(dxn * y, axis=-1, keepdims=True))).astype(BF16)
        da = {jh: jnp.where(tril, _mm_nt(dob[jh], op(jh)["v"]), 0.0).astype(BF16) for jh in items}
        dv1 = {jh: _mm_tn(a[jh], dob[jh]) for jh in items}
        dqe1 = {jh: _mm(da[jh], op(jh)["ke"]) for jh in items}
        dqe2 = {jh: _mm(dob[jh], stb[jh]) for jh in items}
        dke1 = {jh: _mm_tn(da[jh], op(jh)["qe"]) for jh in items}
        inc = {jh: _mm_tn(dob[jh], op(jh)["qe"]) for jh in items}
        dsts = [dst_ref[h] for h in range(GLA_HEADS)]
        dkd1, dgam1 = {}, {}
        for jh in items:
            j, h = jh
            dst = dsts[h]
            dstb = dst.astype(BF16)
            dv_ref[ps[j]["rows"], op(jh)["vs"]] = dv1[jh] + _mm_nt(op(jh)["kd"], dstb)
            dkd1[jh] = _mm(op(jh)["v"], dstb)
            dgam1[jh] = _rowsum(dst * st[jh])
            dsts[h] = dst * ps[j]["gam"][:, op(jh)["ls"]] + inc[jh]
        for h in range(GLA_HEADS):
            dst_ref[h] = dsts[h]
        upper = (ri <= ci).astype(BF16)
        dbs, dbls = [], []
        for j in range(kc):
            p = ps[j]
            tiles = [[op((j, 2 * hp + h2)) for h2 in range(2)] for hp in range(GLA_HEADS // 2)]
            head = lambda d, hp, h2: d[j, 2 * hp + h2]
            lanes = lambda f: jnp.concatenate([f(hp) for hp in range(GLA_HEADS // 2)], axis=1)
            dqe = lanes(lambda hp: sum(jnp.where(tiles[hp][h2]["m"], head(dqe1, hp, h2) + head(dqe2, hp, h2), 0.0)
                                       for h2 in range(2)))
            dke = lanes(lambda hp: head(dke1, hp, 0) + head(dke1, hp, 1))
            dkd = lanes(lambda hp: sum(jnp.where(tiles[hp][h2]["m"], head(dkd1, hp, h2), 0.0) for h2 in range(2)))
            dgam = lanes(lambda hp: head(dgam1, hp, 0) + head(dgam1, hp, 1))
            dqk_ref[p["rows"], :hk] = dqe * p["eb"] * QK_SCALE
            dqk_ref[p["rows"], hk:] = dke * p["enb"] + dkd * p["ebl"]
            dkdkd = dkd * p["kd"]
            dbs.append(dqe * p["qe"] - dke * p["ke"] - dkdkd)
            dbls.append(_rowsum(dkdkd) + dgam * p["gam"])
        dlgs = _tri_mm_all(upper, dbs)
        dzb = []
        dgb = jnp.zeros((1, hk), F32)
        for j in range(kc):
            p = ps[j]
            dz = jnp.where(p["real"], (dlgs[j] + dbls[j]) * (1.0 / GLA_TAU) * _sigmoid(-p["z"]), 0.0)
            dgb = dgb + _rowsum(dz)
            dzb.append(dz.astype(BF16))
        dgup = sum(_mm_tn(ps[j]["gd"].astype(BF16), dzb[j]) for j in range(kc))
        for j in range(kc):
            dgd_ref[ps[j]["rows"], :] = _mm_nt(dzb[j], gup_ref[...])
        dgb_ref[...] += dgb
        dgup_ref[...] += dgup
        dgn_ref[...] += dgn

    rowblk = lambda col: (lambda b, t: (b * ns + ns - 1 - t, col))
    const = lambda b, t: (0, 0)
    return pl.pallas_call(
        body, name="gla_bwd", grid=(bsz, ns),
        in_specs=[pl.BlockSpec((kc * CHUNK, 2 * hk), rowblk(2)), pl.BlockSpec((kc * CHUNK, hv), rowblk(3)),
                  pl.BlockSpec((kc * CHUNK, hv), rowblk(4)), pl.BlockSpec((kc * CHUNK, LANES), rowblk(20)),
                  pl.BlockSpec((kc * CHUNK, hv), rowblk(1)),
                  pl.BlockSpec((kc, GLA_HEADS, LANES, LANES), lambda b, t: (b * ns + ns - 1 - t, 0, 0, 0)),
                  pl.BlockSpec((LANES, 256), const), pl.BlockSpec((1, 256), const), pl.BlockSpec((1, GLA_DV), const),
                  pl.BlockSpec(memory_space=pl.ANY)],
        out_specs=[pl.BlockSpec((kc * CHUNK, 2 * hk), rowblk(0)), pl.BlockSpec((kc * CHUNK, hv), rowblk(0)),
                   pl.BlockSpec((kc * CHUNK, hv), rowblk(0)), pl.BlockSpec((kc * CHUNK, LANES), rowblk(0)),
                   pl.BlockSpec((1, GLA_DV), const), pl.BlockSpec((1, 256), const), pl.BlockSpec((LANES, 256), const)],
        out_shape=[_sds((r, 2 * hk), F32), _sds((r, hv), F32), _sds((r, hv), F32), _sds((r, LANES), F32),
                   _sds((1, GLA_DV), F32), _sds((1, 256), F32), _sds((LANES, 256), F32)],
        scratch_shapes=[pltpu.VMEM((GLA_HEADS, LANES, LANES), F32)],
        compiler_params=_params(("arbitrary", "arbitrary")),
    )(u, u, u, u, dmi, sta, gup, gb, gn, dep)


def _conv_bwd(u, c, dmi, w32, cg, cbe, tp, tc, dc):
    r = u.shape[0]
    hb = tc // CONV_HALO
    nhalo = r // CONV_HALO

    def dconv(cv, dco, cg_ref, cbe_ref):
        xhat, rstd = _ln(cv)
        cn = xhat * cg_ref[...] + cbe_ref[...]
        sg = _sigmoid(cn)
        dcn = dco * (sg * (1.0 + cn * (1.0 - sg)))
        return _ln_bwd(dcn * cg_ref[...], xhat, rstd), dcn, xhat

    def body(a_ref, g_ref, ah_ref, gh_ref, c_ref, dco_ref, ch_ref, dcoh_ref, w_ref, cg_ref, cbe_ref,
             du_ref, dw_ref, dcb_ref, dcg_ref, dcbe_ref, hs_ref, dcs_ref):
        t = pl.program_id(0)

        @pl.when(t == 0)
        def _():
            dw_ref[...] = jnp.zeros_like(dw_ref)
            dcb_ref[...] = jnp.zeros_like(dcb_ref)
            dcg_ref[...] = jnp.zeros_like(dcg_ref)
            dcbe_ref[...] = jnp.zeros_like(dcbe_ref)

        first = lax.rem(t * tc, tp) == 0
        last = lax.rem((t + 1) * tc, tp) == 0
        hh = ah_ref[...] * _sigmoid(gh_ref[...])
        hs_ref[0:CONV_HALO, :] = jnp.where(first, 0.0, hh)
        hs_ref[CONV_HALO:CONV_HALO + tc, :] = a_ref[...] * _sigmoid(g_ref[...])
        dch, _, _ = dconv(ch_ref[...], dcoh_ref[...], cg_ref, cbe_ref)
        dcs_ref[tc:tc + CONV_HALO, :] = jnp.where(last, 0.0, dch)

        def sub1(k, carry):
            r0 = pl.multiple_of(k * CONV_SUB, CONV_SUB)
            dcv, dcn, xhat = dconv(c_ref[pl.ds(r0, CONV_SUB), :], dco_ref[pl.ds(r0, CONV_SUB), :], cg_ref, cbe_ref)
            dcs_ref[pl.ds(r0, CONV_SUB), :] = dcv
            dcb_ref[...] += _rowsum(dcv)
            dcg_ref[...] += _rowsum(dcn * xhat)
            dcbe_ref[...] += _rowsum(dcn)
            return carry

        lax.fori_loop(0, tc // CONV_SUB, sub1, 0)

        def sub2(k, carry):
            r0 = pl.multiple_of(k * CONV_SUB, CONV_SUB)
            dwin = dcs_ref[pl.ds(r0, CONV_WIN), :]
            dh = _conv_taps(dwin, lambda o: w_ref[CONV_WIDTH - 1 - o:CONV_WIDTH - o, :], 0)
            av = a_ref[pl.ds(r0, CONV_SUB), :]
            sg = _sigmoid(g_ref[pl.ds(r0, CONV_SUB), :])
            du_ref[pl.ds(r0, CONV_SUB), 0:dc] = dh * sg
            du_ref[pl.ds(r0, CONV_SUB), dc:2 * dc] = dh * av * sg * (1.0 - sg)
            hwin = hs_ref[pl.ds(r0, CONV_WIN), :]
            dcv = dwin[0:CONV_SUB, :]
            for rho in range(8):
                offs = [o for o in range(2, 2 + CONV_WIDTH) if o % 8 == rho]
                rolled = hwin if rho == 0 else pltpu.roll(hwin, CONV_WIN - rho, 0)
                for o in offs:
                    m8 = o - rho
                    dw_ref[o - 2:o - 1, :] += _rowsum(dcv * rolled[m8:m8 + CONV_SUB, :])
            return carry

        lax.fori_loop(0, tc // CONV_SUB, sub2, 0)

    vec = pl.BlockSpec((1, dc), lambda t: (0, 0))
    prev = lambda col: (lambda t: (jnp.maximum(t * hb - 1, 0), col))
    nxt = lambda col: (lambda t: (jnp.minimum((t + 1) * hb, nhalo - 1), col))
    return pl.pallas_call(
        body, name="conv_bwd", grid=(r // tc,),
        in_specs=[pl.BlockSpec((tc, dc), lambda t: (t, 0)), pl.BlockSpec((tc, dc), lambda t: (t, 1)),
                  pl.BlockSpec((CONV_HALO, dc), prev(0)), pl.BlockSpec((CONV_HALO, dc), prev(1)),
                  pl.BlockSpec((tc, dc), lambda t: (t, 0)), pl.BlockSpec((tc, dc), lambda t: (t, 0)),
                  pl.BlockSpec((CONV_HALO, dc), nxt(0)), pl.BlockSpec((CONV_HALO, dc), nxt(0)),
                  pl.BlockSpec((32, dc), lambda t: (0, 0)), vec, vec],
        out_specs=[pl.BlockSpec((tc, 2 * dc), lambda t: (t, 0)), pl.BlockSpec((32, dc), lambda t: (0, 0)), vec, vec, vec],
        out_shape=[_sds((r, 2 * dc), F32), _sds((32, dc), F32), _sds((1, dc), F32), _sds((1, dc), F32), _sds((1, dc), F32)],
        scratch_shapes=[pltpu.VMEM((CONV_HALO + tc, dc), F32), pltpu.VMEM((tc + CONV_HALO, dc), F32)],
        compiler_params=_params(("arbitrary",)),
    )(u, u, u, u, c, dmi, c, dmi, w32, cg, cbe)


def _inproj_bwd(dp1, dus, xsrc, g_in, w_in, dep, tp, seq, tx):
    r, d = dp1.shape
    widths = [x.shape[1] for x in dus]
    offs = [sum(widths[:k]) for k in range(len(widths))]
    n = w_in.shape[0]
    nd = len(dus)
    head = tx == 0
    rows = X_OFF if head else tx

    def body(*refs):
        dp_ref = refs[0]
        du_refs = refs[1:1 + nd]
        x_ref, g_ref, w_ref, _, out_ref, dg_ref, db_ref = refs[1 + nd:]
        i = pl.program_id(0)

        @pl.when(i == 0)
        def _():
            dg_ref[...] = jnp.zeros_like(dg_ref)
            db_ref[...] = jnp.zeros_like(db_ref)
            if head:
                out_ref[...] = jnp.zeros_like(out_ref)

        ds0 = ALPHA * dp_ref[...]
        for k in range(nd):
            ds0 = ds0 + _mm(du_refs[k][...].astype(BF16), w_ref[offs[k]:offs[k] + widths[k], :])
        if head:
            ds0 = jnp.where(lax.broadcasted_iota(jnp.int32, (X_OFF, 1), 0) >= PAD_FRONT, ds0, 0.0)
        xhat, rstd = _ln(x_ref[...])
        dg_ref[...] += _rowsum(ds0 * xhat)
        db_ref[...] += _rowsum(ds0)
        dx = _ln_bwd(ds0 * g_ref[...], xhat, rstd)
        if head:
            out_ref[...] += dx[PAD_FRONT:X_OFF, :]
        else:
            out_ref[...] = dx

    if head:
        nb = tp // X_OFF
        row = lambda w: pl.BlockSpec((X_OFF, w), lambda i: (i * nb, 0))
        xspec = pl.BlockSpec((X_OFF, d), lambda i: (0, 0))
        ospec, oshape, steps = pl.BlockSpec((N_META, d), lambda i: (0, 0)), _sds((N_META, d), F32), r // tp
    else:
        start = _x_tile_row(tp, seq, tx)
        row = lambda w: pl.BlockSpec((pl.Element(tx), pl.Element(w)), lambda i: (start(i), 0))
        xspec = pl.BlockSpec((tx, d), lambda i: (i, 0))
        ospec, oshape, steps = xspec, _sds(xsrc.shape, F32), xsrc.shape[0] // tx
    vec = pl.BlockSpec((1, d), lambda i: (0, 0))
    return pl.pallas_call(
        body, name="inproj_bwd_head" if head else "inproj_bwd_x", grid=(steps,),
        in_specs=[row(d)] + [row(w) for w in widths] + [xspec, vec, pl.BlockSpec((n, d), lambda i: (0, 0)),
                                                        pl.BlockSpec(memory_space=pl.ANY)],
        out_specs=[ospec, vec, vec],
        out_shape=[oshape, _sds((1, d), F32), _sds((1, d), F32)],
        compiler_params=_params(("arbitrary",)),
    )(dp1, *dus, xsrc, g_in, w_in, dep)


def _inproj_bwd_w(s0, dus, tm):
    r, d = s0.shape
    widths = [x.shape[1] for x in dus]
    offs = [sum(widths[:k]) for k in range(len(widths))]
    nd = len(dus)

    def body(*refs):
        s_ref = refs[0]
        du_refs = refs[1:1 + nd]
        dw_ref = refs[1 + nd]
        i = pl.program_id(0)

        @pl.when(i == 0)
        def _():
            dw_ref[...] = jnp.zeros_like(dw_ref)

        for k in range(nd):
            dw_ref[offs[k]:offs[k] + widths[k], :] += _mm_tn(du_refs[k][...].astype(BF16), s_ref[...])

    row = lambda w: pl.BlockSpec((tm, w), lambda i: (i, 0))
    return pl.pallas_call(
        body, name="inproj_bwd_w", grid=(r // tm,),
        in_specs=[row(d)] + [row(w) for w in widths],
        out_specs=pl.BlockSpec((sum(widths), d), lambda i: (0, 0)),
        out_shape=_sds((sum(widths), d), F32),
        compiler_params=_params(("arbitrary",)),
    )(s0, *dus)


def _local_step(x, tgt, meta, ln_in_g, ln_in_b, w_in, conv_w, conv_b, conv_ln_g, conv_ln_b, gate_up, gate_bias,
                gla_norm_g, late_weights, ln1_g, ln1_b, ln2_g, ln2_b, push):
    bsz, seq, d = x.shape
    tp = X_OFF + seq
    assert tp % CHUNK == 0
    nc = tp // CHUNK
    r = bsz * tp
    dc = conv_b.shape[1]
    tm = _pick_tile(tp, (352, 128, 64))
    tc = _pick_tile(tp, (704, 128, 64))

    x2 = x.reshape(bsz * seq, d)
    head = jnp.pad(meta, ((PAD_FRONT, 0), (0, 0)))
    tx = _pick_tile(seq, (512, 64))
    tgt_p = tgt.reshape(bsz * seq, d)
    w32 = jnp.pad(conv_w, ((0, 32 - CONV_WIDTH), (0, 0)))
    gup = jnp.pad(gate_up, ((0, LANES - GLA_RANK), (0, 0))).astype(BF16)

    s0, s0b = _ln_in_x(x2, ln_in_g, ln_in_b, tp, seq, tx)
    s0, s0b = _ln_in_head(head, ln_in_g, ln_in_b, s0, s0b, tp)
    tmm = _pick_tile(tp, (704, 128, 64))
    u = _inproj_fwd(s0b, w_in, tmm)
    c, co = _conv_fwd(u, w32, conv_b, conv_ln_g, conv_ln_b, tp, tc, dc)
    kc = _pick_tile(nc, (3, 2, 1))
    go, sta = _gla_fwd(u, gup, gate_bias, gla_norm_g, bsz, nc, kc)
    w_out, w1g, w2 = late_weights(go)
    nh = w1g.shape[0]
    tmm = _pick_tile(tp, (704, 128, 64))
    ns = 2
    p1, s1, s1b = _outproj_fwd(s0, co, go, w_out, ln1_g, ln1_b, tmm)
    hm, dp2, dpb, loss, dg2, db2 = _mlp_fwd(s1, s1b, w1g, w2, ln2_g, ln2_b, tgt_p, tp, tmm, ns)

    dh, dp1, dg1, db1 = _mlp_bwd_act(dp2, dpb, hm, w1g, w2, p1, ln1_g, tmm, ns)
    dw1, dw2 = _mlp_bwd_w(s1b, hm, dh, dpb, nh, tmm, ns)
    tok = push("ff", (dw1, dw2))
    dmi, dwo = _outproj_bwd(dp1, co, go, w_out, tok, tmm)
    tok = push("out", (dwo,))
    dqk, dv, dr, dgd, dgn, dgb, dgup = _gla_bwd(u, dmi, sta, gup, gate_bias, gla_norm_g, tok, bsz, nc, kc)
    dcv, dcw, dcb, dcg, dcbe = _conv_bwd(u, c, dmi, w32, conv_ln_g, conv_ln_b, tp, tc, dc)
    dus = [dcv, dqk, dv, dr, dgd]
    dwi = _inproj_bwd_w(s0b, dus, tm)
    tok = push("in", (dwi,))
    gx, dgx, dbx = _inproj_bwd(dp1, dus, x2, ln_in_g, w_in, tok, tp, seq, tx)
    dmeta, dgh, dbh = _inproj_bwd(dp1, dus, head, ln_in_g, w_in, tok, tp, seq, 0)

    return dict(loss=loss[0, 0], grad_x=gx.reshape(bsz, seq, d), meta_tokens=dmeta, ln_in_g=dgx + dgh, ln_in_b=dbx + dbh,
                conv_w=dcw[:CONV_WIDTH], conv_b=dcb, conv_ln_g=dcg, conv_ln_b=dcbe,
                gate_up=dgup[:GLA_RANK], gate_bias=dgb, gla_norm_g=dgn, ln1_g=dg1, ln1_b=db1, ln2_g=dg2, ln2_b=db2)


def _exchange(arrays, scatter, name):
    na = len(arrays)
    npeer = N_DEV - 1

    def body(*refs):
        srcs = refs[:na]
        outs = refs[na:2 * na]
        send_sems, recv_sems, local_sems = refs[2 * na:]
        xi, yi, ci = (lax.axis_index(a) for a in MESH_AXES)
        me = 4 * xi + 2 * yi + ci
        copies = []
        for a in range(na):
            own = srcs[a].at[me] if scatter[a] else srcs[a]
            cp = pltpu.make_async_copy(own, outs[a].at[me], local_sems.at[a])
            cp.start()
            copies.append(cp)
        remote = []
        for k in range(1, N_DEV):
            px, py, pc = xi ^ (k >> 2), yi ^ ((k >> 1) & 1), ci ^ (k & 1)
            peer = 4 * px + 2 * py + pc
            for a in range(na):
                src = srcs[a].at[peer] if scatter[a] else srcs[a]
                cp = pltpu.make_async_remote_copy(
                    src_ref=src, dst_ref=outs[a].at[me],
                    send_sem=send_sems.at[a * npeer + k - 1], recv_sem=recv_sems.at[a * npeer + k - 1],
                    device_id=(px, py, pc), device_id_type=pl.DeviceIdType.MESH)
                cp.start()
                remote.append(cp)
        for cp in remote:
            cp.wait()
        for cp in copies:
            cp.wait()

    out_shape = [_sds(a.shape if scatter[i] else (N_DEV,) + a.shape, a.dtype) for i, a in enumerate(arrays)]
    anyspec = pl.BlockSpec(memory_space=pl.ANY)
    return pl.pallas_call(
        body, name=name,
        in_specs=[anyspec] * na, out_specs=[anyspec] * na, out_shape=out_shape,
        scratch_shapes=[pltpu.SemaphoreType.DMA((na * npeer,)), pltpu.SemaphoreType.DMA((na * npeer,)),
                        pltpu.SemaphoreType.DMA((na,))],
    )(*arrays)


def _peers(xi, yi, ci):
    for k in range(1, N_DEV):
        px, py, pc = xi ^ (k >> 2), yi ^ ((k >> 1) & 1), ci ^ (k & 1)
        yield (px, py, pc), 4 * px + 2 * py + pc


def _sc_exchange(arrays, scatter, name, collective_id, after=None):
    na = len(arrays)
    npeer = N_DEV - 1
    ndep = 0 if after is None else 1

    def body(*refs):
        srcs = refs[:na]
        outs = refs[na + ndep:2 * na + ndep]
        send_sems, recv_sems, own_sems = refs[2 * na + ndep:]
        xi, yi, ci = (lax.axis_index(a) for a in MESH_AXES)
        me = 4 * xi + 2 * yi + ci
        barrier = pltpu.get_barrier_semaphore()
        for pos, _ in _peers(xi, yi, ci):
            pl.semaphore_signal(barrier, inc=1, device_id=pos, device_id_type=pl.DeviceIdType.MESH)
        pl.semaphore_wait(barrier, npeer)
        own = [pltpu.make_async_copy(srcs[a].at[me] if scatter[a] else srcs[a], outs[a].at[me], own_sems.at[a])
               for a in range(na)]
        for cp in own:
            cp.start()
        remote = []
        for a in range(na):
            for k, (pos, peer) in enumerate(_peers(xi, yi, ci)):
                cp = pltpu.make_async_remote_copy(
                    src_ref=srcs[a].at[peer] if scatter[a] else srcs[a], dst_ref=outs[a].at[me],
                    send_sem=send_sems.at[a * npeer + k], recv_sem=recv_sems.at[a * npeer + k],
                    device_id=pos, device_id_type=pl.DeviceIdType.MESH)
                cp.start()
                remote.append(cp)
        for cp in own:
            cp.wait()
        for cp in remote:
            cp.wait()

    out_type = [_sds(a.shape if scatter[i] else (N_DEV,) + a.shape, a.dtype) for i, a in enumerate(arrays)]
    sent = sum(a.size * a.dtype.itemsize // (N_DEV if scatter[i] else 1) for i, a in enumerate(arrays))
    return pl.kernel(
        body, out_type=out_type, mesh=plsc.ScalarSubcoreMesh(axis_name="seq", num_cores=1), name=name,
        scratch_types=[pltpu.SemaphoreType.DMA((na * npeer,)), pltpu.SemaphoreType.DMA((na * npeer,)),
                       pltpu.SemaphoreType.DMA((na,))],
        compiler_params=pltpu.CompilerParams(collective_id=collective_id),
        cost_estimate=pl.CostEstimate(flops=0, transcendentals=0, bytes_accessed=2 * N_DEV * sent,
                                      remote_bytes_transferred=npeer * sent),
    )(*arrays, *([] if after is None else [after]))


def _sc_gather(arrays, name, collective_id, after=None):
    na = len(arrays)
    ndep = 0 if after is None else 1
    npair = N_DEV - 1

    def body(*refs):
        srcs = refs[:na]
        outs = refs[na + ndep:2 * na + ndep]
        send_sems, recv_sems, own_sems = refs[2 * na + ndep:]
        xi, yi, ci = (lax.axis_index(a) for a in MESH_AXES)
        me = 4 * xi + 2 * yi + ci
        sibling = (xi, yi, 1 - ci)
        chips = [(1 - xi, yi), (xi, 1 - yi), (1 - xi, 1 - yi)]
        barrier = pltpu.get_barrier_semaphore()
        for pos, _ in _peers(xi, yi, ci):
            pl.semaphore_signal(barrier, inc=1, device_id=pos, device_id_type=pl.DeviceIdType.MESH)
        pl.semaphore_wait(barrier, npair)

        def copy(a, k, src, slot, to):
            return pltpu.make_async_remote_copy(
                src_ref=src, dst_ref=outs[a].at[slot], send_sem=send_sems.at[a * npair + k],
                recv_sem=recv_sems.at[a * npair + k], device_id=to, device_id_type=pl.DeviceIdType.MESH)

        own = [pltpu.make_async_copy(srcs[a], outs[a].at[me], own_sems.at[a]) for a in range(na)]
        for cp in own:
            cp.start()
        sent = []
        for a in range(na):
            sent.append(copy(a, 0, srcs[a], me, sibling))
            sent += [copy(a, 1 + j, srcs[a], me, (*chip, ci)) for j, chip in enumerate(chips)]
        for cp in sent:
            cp.start()
        for j, (cx, cy) in enumerate(chips):
            slot = 4 * cx + 2 * cy + ci
            for a in range(na):
                copy(a, 1 + j, srcs[a], slot, sibling).wait_recv()
                cp = copy(a, 4 + j, outs[a].at[slot], slot, sibling)
                cp.start()
                sent.append(cp)
        for a in range(na):
            copy(a, 0, srcs[a], me, sibling).wait_recv()
            for j in range(len(chips)):
                copy(a, 4 + j, srcs[a], me, sibling).wait_recv()
        for cp in sent:
            cp.wait_send()
        for cp in own:
            cp.wait()

    out_type = [_sds((N_DEV,) + a.shape, a.dtype) for a in arrays]
    sent_bytes = sum(a.size * a.dtype.itemsize for a in arrays)
    return pl.kernel(
        body, out_type=out_type, mesh=plsc.ScalarSubcoreMesh(axis_name="seq", num_cores=1), name=name,
        scratch_types=[pltpu.SemaphoreType.DMA((na * npair,)), pltpu.SemaphoreType.DMA((na * npair,)),
                       pltpu.SemaphoreType.DMA((na,))],
        compiler_params=pltpu.CompilerParams(collective_id=collective_id),
        cost_estimate=pl.CostEstimate(flops=0, transcendentals=0, bytes_accessed=2 * N_DEV * sent_bytes,
                                      remote_bytes_transferred=npair * sent_bytes),
    )(*arrays, *([] if after is None else [after]))


def _push_start(arrays, scatter, name, dep=None):
    na = len(arrays)
    shapes = [a.shape if scatter[i] else (N_DEV,) + a.shape for i, a in enumerate(arrays)]
    hbm = pl.BlockSpec(memory_space=pltpu.HBM)
    sem = pl.BlockSpec(memory_space=pltpu.SEMAPHORE)
    ndep = 0 if dep is None else 1

    def body(*refs):
        srcs = refs[:na]
        lands = refs[na:2 * na]
        send_sems, recv_sems = refs[2 * na + ndep:2 * na + ndep + 2]
        own_sems = refs[4 * na + ndep + 2]
        xi, yi, ci = (lax.axis_index(a) for a in MESH_AXES)
        me = 4 * xi + 2 * yi + ci
        own = [pltpu.make_async_copy(srcs[a].at[me] if scatter[a] else srcs[a], lands[a].at[me], own_sems.at[a])
               for a in range(na)]
        for cp in own:
            cp.start()
        for cp in own:
            cp.wait()
        for a in range(na):
            for pos, peer in _peers(xi, yi, ci):
                pltpu.make_async_remote_copy(
                    src_ref=srcs[a].at[peer] if scatter[a] else srcs[a], dst_ref=lands[a].at[me],
                    send_sem=send_sems.at[a], recv_sem=recv_sems.at[a],
                    device_id=pos, device_id_type=pl.DeviceIdType.MESH).start()

    ins = [pltpu.with_memory_space_constraint(a, pltpu.HBM) for a in arrays]
    ins += [pltpu.with_memory_space_constraint(lax.empty(s, a.dtype), pltpu.HBM) for s, a in zip(shapes, arrays)]
    res = pl.pallas_call(
        body, name=name,
        in_specs=[hbm] * (2 * na) + [pl.BlockSpec(memory_space=pl.ANY)] * ndep,
        out_specs=[sem, sem] + [hbm] * (2 * na),
        out_shape=[pltpu.SemaphoreType.DMA((na,)), pltpu.SemaphoreType.DMA((na,))]
                  + [pltpu.HBM(a.shape, a.dtype) for a in arrays] + [pltpu.HBM(s, a.dtype) for s, a in zip(shapes, arrays)],
        input_output_aliases={i: 2 + i for i in range(2 * na)},
        scratch_shapes=[pltpu.SemaphoreType.DMA((na,))],
        compiler_params=pltpu.CompilerParams(has_side_effects=pltpu.SideEffectType.DATAFLOW_SIDE_EFFECTING),
    )(*ins, *([] if dep is None else [dep]))
    return (res[0], res[1], list(res[2:2 + na]), list(res[2 + na:2 + 2 * na])), res[2]


def _push_wait(handle, after, name):
    send_sems, recv_sems, srcs, lands = handle
    na = len(srcs)
    hbm = pl.BlockSpec(memory_space=pltpu.HBM)
    sem = pl.BlockSpec(memory_space=pltpu.SEMAPHORE)

    def body(*refs):
        land_refs = refs[na:2 * na]
        send_ref, recv_ref = refs[2 * na:2 * na + 2]
        me = tuple(lax.axis_index(a) for a in MESH_AXES)
        for a in range(na):
            seven = land_refs[a].at[pl.ds(0, N_DEV - 1)]
            cp = pltpu.make_async_remote_copy(src_ref=seven, dst_ref=seven, send_sem=send_ref.at[a], recv_sem=recv_ref.at[a],
                                              device_id=me, device_id_type=pl.DeviceIdType.MESH)
            cp.wait_send()
            cp.wait_recv()

    res = pl.pallas_call(
        body, name=name,
        in_specs=[hbm] * (2 * na) + [sem, sem, pl.BlockSpec(memory_space=pl.ANY)],
        out_specs=[hbm] * (2 * na),
        out_shape=[pltpu.HBM(a.shape, a.dtype) for a in srcs] + [pltpu.HBM(a.shape, a.dtype) for a in lands],
        input_output_aliases={i: i for i in range(2 * na)},
        compiler_params=pltpu.CompilerParams(has_side_effects=pltpu.SideEffectType.DATAFLOW_SIDE_EFFECTING),
    )(*srcs, *lands, send_sems, recv_sems, after)
    return list(res[na:])


def _adamw(w, g, m, v):
    m = ADAM_B1 * m + (1.0 - ADAM_B1) * g
    v = ADAM_B2 * v + (1.0 - ADAM_B2) * jnp.square(g)
    m_hat = m / (1.0 - ADAM_B1 ** ADAM_STEP)
    v_hat = v / (1.0 - ADAM_B2 ** ADAM_STEP)
    delta = -ADAM_LR * (m_hat / (jnp.sqrt(v_hat) + ADAM_EPS) + ADAM_WD * w)
    return delta, m, v


def _sum_devices(ref):
    g = ref[0].astype(F32)
    for k in range(1, N_DEV):
        g = g + ref[k].astype(F32)
    return g


def _update_big(parts, w, m, v, name):
    rows, cols = w.shape

    def body(p_ref, w_ref, m_ref, v_ref, g_ref, d_ref, nm_ref, nv_ref):
        g = _sum_devices(p_ref)
        g_ref[...] = g
        d_ref[...], nm_ref[...], nv_ref[...] = _adamw(w_ref[...], g, m_ref[...], v_ref[...])

    if rows % 16 == 0:
        tr = _pick_tile(rows, (128, 64, 16))
        steps, blk = rows // tr, pl.BlockSpec((tr, cols), lambda i: (i, 0))
        pblk = pl.BlockSpec((N_DEV, tr, cols), lambda i: (0, i, 0))
    else:
        tcol = 2 * LANES
        steps, blk = cols // tcol, pl.BlockSpec((rows, tcol), lambda i: (0, i))
        pblk = pl.BlockSpec((N_DEV, rows, tcol), lambda i: (0, 0, i))
    return pl.pallas_call(
        body, name=name, grid=(steps,),
        in_specs=[pblk, blk, blk, blk],
        out_specs=[blk] * 4, out_shape=[_sds((rows, cols), F32)] * 4,
        compiler_params=_params(("parallel",)),
    )(parts, w, m, v)


_VEC_ORDER = ("ln_in_g", "ln_in_b", "conv_b", "conv_ln_g", "conv_ln_b", "gate_bias", "gla_norm_g",
              "ln1_g", "ln1_b", "ln2_g", "ln2_b")
_SHARDED_SMALL = (("meta_tokens", 0, N_META, LANES), ("conv_w", N_META, CONV_WIDTH, None), ("gate_up", N_META + 32, GLA_RANK, None))


def _update_small(parts_sh, parts_vec, wmv):
    names = [s[0] for s in _SHARDED_SMALL] + list(_VEC_ORDER)
    flat = [a for nme in names for a in wmv[nme]]
    nv = len(_VEC_ORDER)

    def body(*refs):
        sh_ref, vec_ref = refs[0], refs[1]
        ins = refs[2:2 + len(flat)]
        outs = refs[2 + len(flat):2 + len(flat) + 4 * len(names)]
        loss_ref = refs[2 + len(flat) + 4 * len(names)]
        gsh_ref, gvec_ref = refs[-2:]
        gsh_ref[...] = _sum_devices(sh_ref)
        gvec_ref[...] = _sum_devices(vec_ref)
        loss_ref[...] = gvec_ref[nv:nv + 1, :]
        for idx, nme in enumerate(names):
            w_ref, m_ref, v_ref = ins[3 * idx:3 * idx + 3]
            rows, cols = w_ref.shape
            if idx < len(_SHARDED_SMALL):
                r0 = _SHARDED_SMALL[idx][1]
                g = gsh_ref[r0:r0 + rows, 0:cols]
            else:
                j = idx - len(_SHARDED_SMALL)
                g = gvec_ref[j:j + 1, 0:cols]
            o = outs[4 * idx:4 * idx + 4]
            o[0][...] = g
            o[1][...], o[2][...], o[3][...] = _adamw(w_ref[...], g, m_ref[...], v_ref[...])

    out_shape = [_sds(wmv[nme][0].shape, F32) for nme in names for _ in range(4)] + [_sds((1, parts_vec.shape[2]), F32)]
    vmem = pl.BlockSpec(memory_space=pltpu.VMEM)
    res = pl.pallas_call(
        body, name="update_small", out_shape=out_shape,
        in_specs=[vmem] * (2 + len(flat)), out_specs=[vmem] * len(out_shape),
        scratch_shapes=[pltpu.VMEM(parts_sh.shape[1:], F32), pltpu.VMEM(parts_vec.shape[1:], F32)],
    )(parts_sh, parts_vec, *flat)
    return {nme: res[4 * i:4 * i + 4] for i, nme in enumerate(names)}, res[-1][0, 0]


_WEIGHTS = ("meta_tokens", "ln_in_g", "ln_in_b", "w_in", "conv_w", "conv_b", "conv_ln_g", "conv_ln_b", "gate_up",
            "gate_bias", "gla_norm_g", "w_out", "ln1_g", "ln1_b", "w_ff1", "w_ff2", "ln2_g", "ln2_b")


def kernel(x, meta_tokens, ln_in_g, ln_in_b, w_in, conv_w, conv_b, conv_ln_g, conv_ln_b, gate_up, gate_bias, gla_norm_g, w_out, ln1_g, ln1_b, w_ff1, w_ff2, ln2_g, ln2_b, loss_target, m_meta_tokens, m_ln_in_g, m_ln_in_b, m_w_in, m_conv_w, m_conv_b, m_conv_ln_g, m_conv_ln_b, m_gate_up, m_gate_bias, m_gla_norm_g, m_w_out, m_ln1_g, m_ln1_b, m_w_ff1, m_w_ff2, m_ln2_g, m_ln2_b, v_meta_tokens, v_ln_in_g, v_ln_in_b, v_w_in, v_conv_w, v_conv_b, v_conv_ln_g, v_conv_ln_b, v_gate_up, v_gate_bias, v_gla_norm_g, v_w_out, v_ln1_g, v_ln1_b, v_w_ff1, v_w_ff2, v_ln2_g, v_ln2_b):
    w = dict(meta_tokens=meta_tokens, ln_in_g=ln_in_g, ln_in_b=ln_in_b, w_in=w_in, conv_w=conv_w, conv_b=conv_b,
             conv_ln_g=conv_ln_g, conv_ln_b=conv_ln_b, gate_up=gate_up, gate_bias=gate_bias, gla_norm_g=gla_norm_g,
             w_out=w_out, ln1_g=ln1_g, ln1_b=ln1_b, w_ff1=w_ff1, w_ff2=w_ff2, ln2_g=ln2_g, ln2_b=ln2_b)
    mom = dict(meta_tokens=m_meta_tokens, ln_in_g=m_ln_in_g, ln_in_b=m_ln_in_b, w_in=m_w_in, conv_w=m_conv_w,
               conv_b=m_conv_b, conv_ln_g=m_conv_ln_g, conv_ln_b=m_conv_ln_b, gate_up=m_gate_up, gate_bias=m_gate_bias,
               gla_norm_g=m_gla_norm_g, w_out=m_w_out, ln1_g=m_ln1_g, ln1_b=m_ln1_b, w_ff1=m_w_ff1, w_ff2=m_w_ff2,
               ln2_g=m_ln2_g, ln2_b=m_ln2_b)
    var = dict(meta_tokens=v_meta_tokens, ln_in_g=v_ln_in_g, ln_in_b=v_ln_in_b, w_in=v_w_in, conv_w=v_conv_w,
               conv_b=v_conv_b, conv_ln_g=v_conv_ln_g, conv_ln_b=v_conv_ln_b, gate_up=v_gate_up, gate_bias=v_gate_bias,
               gla_norm_g=v_gla_norm_g, w_out=v_w_out, ln1_g=v_ln1_g, ln1_b=v_ln1_b, w_ff1=v_w_ff1, w_ff2=v_w_ff2,
               ln2_g=v_ln2_g, ln2_b=v_ln2_b)
    shapes = {k: a.shape for k, a in w.items()}

    def two_d(a):
        return a.reshape(1, -1) if a.ndim == 1 else a.reshape(a.shape[-2:])

    w2d = {k: two_d(a) for k, a in w.items()}
    m2d = {k: two_d(a) for k, a in mom.items()}
    v2d = {k: two_d(a) for k, a in var.items()}
    d = x.shape[-1]
    d_in = w2d["w_in"].shape[1] * N_DEV
    d_in_p = -(-d_in // LANES) * LANES

    for dct in (w2d, m2d, v2d):
        dct["w_in"] = dct["w_in"].T
    g_in, g_meta, g_conv, g_gup = _sc_gather(
        [w2d["w_in"].astype(BF16), w2d["meta_tokens"], w2d["conv_w"], w2d["gate_up"]], "gather_first", 0)
    g_out, g_ff1, g_ff2 = _sc_gather(
        [w2d["w_out"].astype(BF16), w2d["w_ff1"].astype(BF16), w2d["w_ff2"].astype(BF16)], "gather_late", 1)
    w_in_full = jnp.pad(g_in.reshape(d_in, d), ((0, d_in_p - d_in), (0, 0)))
    meta_full = g_meta.transpose(1, 0, 2).reshape(N_META, d)
    conv_w_full = g_conv.transpose(1, 0, 2).reshape(CONV_WIDTH, -1)
    gate_up_full = g_gup.transpose(1, 0, 2).reshape(GLA_RANK, -1)

    def late_weights(after):
        return g_out.reshape(-1, d), g_ff1, g_ff2.reshape(-1, d)

    pushed = {}

    def push(tag, grads):
        if tag == "ff":
            pushed["ff1"], pushed["ff2"] = _sc_exchange(list(grads), [True, True], "scatter_ff", 2)
        elif tag == "out":
            pushed["p_out"] = grads[0].reshape(N_DEV, -1, d)
        else:
            p_in = grads[0][:d_in].reshape(N_DEV, d_in // N_DEV, d).astype(BF16)
            pushed["in"], pushed["out"] = _sc_exchange([p_in, pushed["p_out"]], [True, True], "scatter_rest", 3,
                                                       after=pushed["ff1"])
        return grads[0]

    res = _local_step(x, loss_target, meta_full, w2d["ln_in_g"], w2d["ln_in_b"], w_in_full, conv_w_full, w2d["conv_b"],
                      w2d["conv_ln_g"], w2d["conv_ln_b"], gate_up_full, w2d["gate_bias"], w2d["gla_norm_g"], late_weights,
                      w2d["ln1_g"], w2d["ln1_b"], w2d["ln2_g"], w2d["ln2_b"], push)

    dc = res["conv_w"].shape[1]
    hk = res["gate_up"].shape[1]
    sh_meta = res["meta_tokens"].reshape(N_META, N_DEV, LANES).transpose(1, 0, 2)
    sh_conv = jnp.pad(res["conv_w"].reshape(CONV_WIDTH, N_DEV, dc // N_DEV).transpose(1, 0, 2),
                      ((0, 0), (0, 32 - CONV_WIDTH), (0, LANES - dc // N_DEV)))
    sh_gup = jnp.pad(res["gate_up"].reshape(GLA_RANK, N_DEV, hk // N_DEV).transpose(1, 0, 2),
                     ((0, 0), (0, 0), (0, LANES - hk // N_DEV)))
    p_sh = jnp.concatenate([sh_meta, sh_conv, sh_gup], axis=1)
    p_vec = jnp.concatenate([jnp.pad(res[k], ((0, 0), (0, d - res[k].shape[1]))) for k in _VEC_ORDER]
                            + [jnp.full((1, d), res["loss"], F32), jnp.zeros((15 - len(_VEC_ORDER), d), F32)], axis=0)

    r_sh, r_vec = _exchange([p_sh, p_vec], [True, False], "scatter_small")
    r_ff1, r_ff2, r_out, r_in = pushed["ff1"], pushed["ff2"], pushed["out"], pushed["in"]

    upd = {}
    upd["w_in"] = [a.T for a in _update_big(r_in, w2d["w_in"], m2d["w_in"], v2d["w_in"], "update_w_in")]
    upd["w_out"] = _update_big(r_out, w2d["w_out"], m2d["w_out"], v2d["w_out"], "update_w_out")
    upd["w_ff1"] = _update_big(r_ff1, w2d["w_ff1"], m2d["w_ff1"], v2d["w_ff1"], "update_w_ff1")
    upd["w_ff2"] = _update_big(r_ff2, w2d["w_ff2"], m2d["w_ff2"], v2d["w_ff2"], "update_w_ff2")
    small = [s[0] for s in _SHARDED_SMALL] + list(_VEC_ORDER)
    upd_small, loss = _update_small(r_sh, r_vec, {k: (w2d[k], m2d[k], v2d[k]) for k in small})
    upd.update(upd_small)

    outs = [loss, res["grad_x"]]
    for j in range(4):
        outs += [upd[k][j].reshape(shapes[k]) for k in _WEIGHTS]
    return tuple(outs)
```

```python
import functools

import jax
import jax.numpy as jnp
from jax import lax
from jax.experimental import pallas as pl
from jax.experimental.pallas import tpu as pltpu
from jax.experimental.pallas import tpu_sc as plsc

F32 = jnp.float32
BF16 = jnp.bfloat16

N_META = 16
CHUNK = 64
PAD_FRONT = (-N_META) % CHUNK
X_OFF = PAD_FRONT + N_META
CONV_WIDTH = 31
CONV_HALO = 32
CONV_SUB = 64
CONV_WIN = CONV_SUB + CONV_HALO
GLA_HEADS = 4
GLA_DK = 64
GLA_DV = 128
GLA_RANK = 16
GLA_TAU = 16.0
QK_SCALE = GLA_DK ** -0.5
LN_EPS = 1e-5
ALPHA = 2.0 ** 0.25
LANES = 128
N_DEV = 8
ADAM_LR = 0.001
ADAM_B1 = 0.9
ADAM_B2 = 0.999
ADAM_EPS = 1e-08
ADAM_WD = 0.01
ADAM_STEP = 10
VMEM_LIMIT = 56 * 1024 * 1024
MESH_AXES = ("x", "y", "c")


def _sds(shape, dtype):
    return jax.ShapeDtypeStruct(shape, dtype)


def _mm(a, b):
    return jnp.dot(a, b, preferred_element_type=F32)


def _mm_nt(a, b):
    return lax.dot_general(a, b, (((1,), (1,)), ((), ())), preferred_element_type=F32)


def _mm_tn(a, b):
    return lax.dot_general(a, b, (((0,), (0,)), ((), ())), preferred_element_type=F32)


def _sigmoid(x):
    return 1.0 / (1.0 + jnp.exp(-x))


def _log_sigmoid(z):
    return jnp.minimum(z, 0.0) - jnp.log(1.0 + jnp.exp(-jnp.abs(z)))


def _ln(x):
    mu = jnp.mean(x, axis=-1, keepdims=True)
    xc = x - mu
    var = jnp.mean(xc * xc, axis=-1, keepdims=True)
    rstd = lax.rsqrt(var + LN_EPS)
    return xc * rstd, rstd


def _ln_bwd(dyg, xhat, rstd):
    m1 = jnp.mean(dyg, axis=-1, keepdims=True)
    m2 = jnp.mean(dyg * xhat, axis=-1, keepdims=True)
    return rstd * (dyg - m1 - xhat * m2)


def _rowsum(x):
    return jnp.sum(x, axis=0, keepdims=True)


def _row_in_seq(i, tm, tp):
    base = lax.rem(i * tm, tp)
    return base + lax.broadcasted_iota(jnp.int32, (tm, 1), 0)


def _split3(x):
    hi = x.astype(BF16)
    r1 = x - hi.astype(F32)
    mid = r1.astype(BF16)
    lo = (r1 - mid.astype(F32)).astype(BF16)
    return hi, mid, lo


def _tri_mm(tri, x):
    hi, mid, lo = _split3(x)
    return _mm(tri, hi) + _mm(tri, mid) + _mm(tri, lo)


def _params(sem):
    return pltpu.CompilerParams(dimension_semantics=sem, vmem_limit_bytes=VMEM_LIMIT)


def _pick_tile(n, prefs):
    for t in prefs:
        if n % t == 0:
            return t
    raise ValueError(f"no tile for {n}")


def _x_tile_row(tp, seq, tx):
    tps = seq // tx
    return lambda i: pl.multiple_of((i // tps) * tp + X_OFF + (i % tps) * tx, CHUNK)


def _ln_in_x(x2, g, b, tp, seq, tx):
    rx, d = x2.shape
    r = rx // seq * tp
    row = _x_tile_row(tp, seq, tx)

    def body(x_ref, g_ref, b_ref, s0_ref, sb_ref):
        xhat, _ = _ln(x_ref[...])
        s = xhat * g_ref[...] + b_ref[...]
        s0_ref[...] = s
        sb_ref[...] = s.astype(BF16)

    out = pl.BlockSpec((pl.Element(tx), pl.Element(d)), lambda i: (row(i), 0))
    return pl.pallas_call(
        body, name="ln_in_x", grid=(rx // tx,),
        in_specs=[pl.BlockSpec((tx, d), lambda i: (i, 0)), pl.BlockSpec((1, d), lambda i: (0, 0)),
                  pl.BlockSpec((1, d), lambda i: (0, 0))],
        out_specs=[out, out],
        out_shape=[_sds((r, d), F32), _sds((r, d), BF16)],
        compiler_params=_params(("parallel",)),
    )(x2, g, b)


def _ln_in_head(head, g, b, s0, s0b, tp):
    r, d = s0.shape
    nb = tp // X_OFF

    def body(h_ref, g_ref, b_ref, s0_in, sb_in, s0_ref, sb_ref):
        xhat, _ = _ln(h_ref[...])
        real = lax.broadcasted_iota(jnp.int32, (X_OFF, 1), 0) >= PAD_FRONT
        s = jnp.where(real, xhat * g_ref[...] + b_ref[...], 0.0)
        s0_ref[...] = s
        sb_ref[...] = s.astype(BF16)

    anyspec = pl.BlockSpec(memory_space=pl.ANY)
    out = pl.BlockSpec((X_OFF, d), lambda i: (i * nb, 0))
    return pl.pallas_call(
        body, name="ln_in_head", grid=(r // tp,),
        in_specs=[pl.BlockSpec((X_OFF, d), lambda i: (0, 0)), pl.BlockSpec((1, d), lambda i: (0, 0)),
                  pl.BlockSpec((1, d), lambda i: (0, 0)), anyspec, anyspec],
        out_specs=[out, out],
        out_shape=[_sds((r, d), F32), _sds((r, d), BF16)],
        input_output_aliases={3: 0, 4: 1},
        compiler_params=_params(("parallel",)),
    )(head, g, b, s0, s0b)


def _inproj_fwd(s0b, w_in, tm):
    r, d = s0b.shape
    n = w_in.shape[0]

    def body(s_ref, w_ref, u_ref):
        u_ref[...] = _mm_nt(s_ref[...], w_ref[...])

    return pl.pallas_call(
        body, name="inproj_fwd", grid=(r // tm,),
        in_specs=[pl.BlockSpec((tm, d), lambda i: (i, 0)), pl.BlockSpec((n, d), lambda i: (0, 0))],
        out_specs=pl.BlockSpec((tm, n), lambda i: (i, 0)),
        out_shape=_sds((r, n), F32),
        compiler_params=_params(("parallel",)),
    )(s0b, w_in)


def _conv_taps(win, coef, lo):
    acc = None
    for rho in range(8):
        offs = [o for o in range(lo, lo + CONV_WIDTH) if o % 8 == rho]
        if not offs:
            continue
        rolled = win if rho == 0 else pltpu.roll(win, CONV_WIN - rho, 0)
        for o in offs:
            m8 = o - rho
            term = rolled[m8:m8 + CONV_SUB, :] * coef(o)
            acc = term if acc is None else acc + term
    return acc


def _conv_fwd(u, w32, cb, cg, cbe, tp, tc, dc):
    r = u.shape[0]
    hb = tc // CONV_HALO

    def body(a_ref, g_ref, ah_ref, gh_ref, w_ref, cb_ref, cg_ref, cbe_ref, c_ref, co_ref, hs_ref):
        t = pl.program_id(0)
        first = lax.rem(t * tc, tp) == 0
        hh = ah_ref[...] * _sigmoid(gh_ref[...])
        hs_ref[0:CONV_HALO, :] = jnp.where(first, 0.0, hh)
        hs_ref[CONV_HALO:CONV_HALO + tc, :] = a_ref[...] * _sigmoid(g_ref[...])

        def sub(k, carry):
            r0 = pl.multiple_of(k * CONV_SUB, CONV_SUB)
            win = hs_ref[pl.ds(r0, CONV_WIN), :]
            c = _conv_taps(win, lambda o: w_ref[o - 2:o - 1, :], 2) + cb_ref[...]
            c_ref[pl.ds(r0, CONV_SUB), :] = c
            xhat, _ = _ln(c)
            cn = xhat * cg_ref[...] + cbe_ref[...]
            co_ref[pl.ds(r0, CONV_SUB), :] = (cn * _sigmoid(cn)).astype(BF16)
            return carry

        lax.fori_loop(0, tc // CONV_SUB, sub, 0)

    vec = pl.BlockSpec((1, dc), lambda t: (0, 0))
    return pl.pallas_call(
        body, name="conv_fwd", grid=(r // tc,),
        in_specs=[pl.BlockSpec((tc, dc), lambda t: (t, 0)), pl.BlockSpec((tc, dc), lambda t: (t, 1)),
                  pl.BlockSpec((CONV_HALO, dc), lambda t: (jnp.maximum(t * hb - 1, 0), 0)),
                  pl.BlockSpec((CONV_HALO, dc), lambda t: (jnp.maximum(t * hb - 1, 0), 1)),
                  pl.BlockSpec((32, dc), lambda t: (0, 0)), vec, vec, vec],
        out_specs=[pl.BlockSpec((tc, dc), lambda t: (t, 0)), pl.BlockSpec((tc, dc), lambda t: (t, 0))],
        out_shape=[_sds((r, dc), F32), _sds((r, dc), BF16)],
        scratch_shapes=[pltpu.VMEM((CONV_HALO + tc, dc), F32)],
        compiler_params=_params(("parallel",)),
    )(u, u, u, u, w32, cb, cg, cbe)


def _tri_mm_all(tri, xs):
    parts = [_split3(x) for x in xs]
    acc = [None] * len(xs)
    for t in range(3):
        for j in range(len(xs)):
            term = _mm(tri, parts[j][t])
            acc[j] = term if t == 0 else acc[j] + term
    return acc


def _gla_prep(qk_ref, gd_ref, gup, gb, n0, kc):
    rows = [slice(j * CHUNK, (j + 1) * CHUNK) for j in range(kc)]
    ri = lax.broadcasted_iota(jnp.int32, (CHUNK, CHUNK), 0)
    ci = lax.broadcasted_iota(jnp.int32, (CHUNK, CHUNK), 1)
    low = (ri >= ci).astype(BF16)
    hk = GLA_HEADS * GLA_DK
    gds = [gd_ref[rw, :] for rw in rows]
    zs = [_mm(g.astype(BF16), gup) + gb for g in gds]
    reals = [(n0 + j) * CHUNK + lax.broadcasted_iota(jnp.int32, (CHUNK, 1), 0) >= PAD_FRONT for j in range(kc)]
    lgs = [jnp.where(reals[j], _log_sigmoid(zs[j]) * (1.0 / GLA_TAU), 0.0) for j in range(kc)]
    bs = _tri_mm_all(low, lgs)
    out = []
    for j in range(kc):
        b, bl = bs[j], _rowsum(lgs[j])
        q = qk_ref[rows[j], :hk] * QK_SCALE
        k = qk_ref[rows[j], hk:]
        eb, enb, ebl = jnp.exp(b), jnp.exp(-b), jnp.exp(bl - b)
        out.append(dict(rows=rows[j], gd=gds[j], z=zs[j], real=reals[j], eb=eb, enb=enb, ebl=ebl, gam=jnp.exp(bl),
                        qe=q * eb, ke=k * enb, kd=k * ebl))
    return out, ri, ci


def _gla_heads(p, v_ref):
    ops = []
    for h in range(GLA_HEADS):
        hp, h2 = divmod(h, 2)
        ls = slice(hp * LANES, (hp + 1) * LANES)
        m = _head_mask(h2)
        ops.append(dict(ls=ls, m=m, vs=slice(h * GLA_DV, (h + 1) * GLA_DV),
                        qe=jnp.where(m, p["qe"][:, ls], 0.0).astype(BF16),
                        kd=jnp.where(m, p["kd"][:, ls], 0.0).astype(BF16),
                        ke=p["ke"][:, ls].astype(BF16),
                        v=v_ref[p["rows"], h * GLA_DV:(h + 1) * GLA_DV].astype(BF16)))
    return ops


def _head_mask(h2):
    lane = lax.broadcasted_iota(jnp.int32, (1, LANES), 1)
    return (lane < GLA_DK) if h2 == 0 else (lane >= GLA_DK)


def _gla_fwd(u, gup, gb, gn, bsz, nc, kc):
    r = u.shape[0]
    hv = GLA_HEADS * GLA_DV
    ns = nc // kc

    def body(qk_ref, v_ref, r_ref, gd_ref, gup_ref, gb_ref, gn_ref, go_ref, sta_ref, st_ref):
        t = pl.program_id(1)

        @pl.when(t == 0)
        def _():
            st_ref[...] = jnp.zeros_like(st_ref)

        ps, ri, ci = _gla_prep(qk_ref, gd_ref, gup_ref[...], gb_ref[...], t * kc, kc)
        tril = ri >= ci
        items = [(j, h) for j in range(kc) for h in range(GLA_HEADS)]
        ops = [_gla_heads(p, v_ref) for p in ps]
        a = {jh: jnp.where(tril, _mm_nt(ops[jh[0]][jh[1]]["qe"], ops[jh[0]][jh[1]]["ke"]), 0.0).astype(BF16) for jh in items}
        oi = {jh: _mm(a[jh], ops[jh[0]][jh[1]]["v"]) for jh in items}
        inc = {jh: _mm_tn(ops[jh[0]][jh[1]]["v"], ops[jh[0]][jh[1]]["kd"]) for jh in items}
        sts = [st_ref[h] for h in range(GLA_HEADS)]
        for j, h in items:
            op, p = ops[j][h], ps[j]
            st = sts[h]
            sta_ref[j, h] = st
            o = oi[j, h] + _mm_nt(op["qe"], st.astype(BF16))
            sts[h] = st * p["gam"][:, op["ls"]] + inc[j, h]
            rs = lax.rsqrt(jnp.mean(o * o, axis=-1, keepdims=True) + LN_EPS)
            rr = r_ref[p["rows"], op["vs"]]
            go_ref[p["rows"], op["vs"]] = (o * rs * gn_ref[...] * (rr * _sigmoid(rr))).astype(BF16)
        for h in range(GLA_HEADS):
            st_ref[h] = sts[h]

    rowblk = lambda col: (lambda b, t: (b * ns + t, col))
    const = lambda b, t: (0, 0)
    return pl.pallas_call(
        body, name="gla_fwd", grid=(bsz, ns),
        in_specs=[pl.BlockSpec((kc * CHUNK, 512), rowblk(2)), pl.BlockSpec((kc * CHUNK, hv), rowblk(3)),
                  pl.BlockSpec((kc * CHUNK, hv), rowblk(4)), pl.BlockSpec((kc * CHUNK, LANES), rowblk(20)),
                  pl.BlockSpec((LANES, 256), const), pl.BlockSpec((1, 256), const), pl.BlockSpec((1, GLA_DV), const)],
        out_specs=[pl.BlockSpec((kc * CHUNK, hv), rowblk(0)),
                   pl.BlockSpec((kc, GLA_HEADS, LANES, LANES), lambda b, t: (b * ns + t, 0, 0, 0))],
        out_shape=[_sds((r, hv), BF16), _sds((bsz * nc, GLA_HEADS, LANES, LANES), F32)],
        scratch_shapes=[pltpu.VMEM((GLA_HEADS, LANES, LANES), F32)],
        compiler_params=_params(("parallel", "arbitrary")),
    )(u, u, u, u, gup, gb, gn)


def _outproj_fwd(s0, co, go, w_out, g1, b1, tm):
    r, d = s0.shape
    dc = co.shape[1]

    def body(s0_ref, co_ref, go_ref, w_ref, g_ref, b_ref, p1_ref, s1_ref, s1b_ref):
        nb = 4 if tm % 64 == 0 else 1
        blocks = [slice(k * (tm // nb), (k + 1) * (tm // nb)) for k in range(nb)]
        mixes = [_mm(co_ref[rows, :], w_ref[0:dc, :]) + _mm(go_ref[rows, :], w_ref[dc:2 * dc, :]) for rows in blocks]
        for rows, mix in zip(blocks, mixes):
            p1 = ALPHA * s0_ref[rows, :] + mix
            p1_ref[rows, :] = p1
            xhat, _ = _ln(p1)
            s1 = xhat * g_ref[...] + b_ref[...]
            s1_ref[rows, :] = s1
            s1b_ref[rows, :] = s1.astype(BF16)

    row = lambda w: pl.BlockSpec((tm, w), lambda i: (i, 0))
    vec = pl.BlockSpec((1, d), lambda i: (0, 0))
    return pl.pallas_call(
        body, name="outproj_fwd", grid=(r // tm,),
        in_specs=[row(d), row(dc), row(dc), pl.BlockSpec((2 * dc, d), lambda i: (0, 0)), vec, vec],
        out_specs=[row(d), row(d), row(d)],
        out_shape=[_sds((r, d), F32), _sds((r, d), F32), _sds((r, d), BF16)],
        compiler_params=_params(("parallel",)),
    )(s0, co, go, w_out, g1, b1)


def _mlp_fwd(s1, s1b, w1g, w2, g2, b2, tgt, tp, tm, ns):
    r, d = s1.shape
    nh, _, th = w1g.shape
    nj = nh // ns

    def body(s1_ref, sb_ref, w1_ref, w2_ref, g_ref, b_ref, t_ref, hm_ref, dp2_ref, dpb_ref, loss_ref, dg_ref, db_ref, acc_ref):
        i = pl.program_id(0)
        j = pl.program_id(1)

        @pl.when(jnp.logical_and(i == 0, j == 0))
        def _():
            loss_ref[...] = jnp.zeros_like(loss_ref)
            dg_ref[...] = jnp.zeros_like(dg_ref)
            db_ref[...] = jnp.zeros_like(db_ref)

        @pl.when(j == 0)
        def _():
            acc_ref[...] = jnp.zeros_like(acc_ref)

        def mlp_rows(rows):
            hs = [_mm(sb_ref[rows, :], w1_ref[s]) for s in range(ns)]
            acc = acc_ref[rows, :]
            for s in range(ns):
                hm_ref[rows, s * th:(s + 1) * th] = hs[s].astype(BF16)
                act = jnp.square(jnp.maximum(hs[s], 0.0))
                acc = acc + _mm(act.astype(BF16), w2_ref[s * th:(s + 1) * th, :])
            return acc

        @pl.when(j < nj - 1)
        def _():
            acc_ref[...] = mlp_rows(slice(None))

        @pl.when(j == nj - 1)
        def _():
            halves = [slice(0, tm // 2), slice(tm // 2, tm)]
            accs = [mlp_rows(rows) for rows in halves]
            isx = _row_in_seq(i, tm, tp) >= X_OFF
            tg = t_ref[...]
            tg = jnp.where(i == 0, pltpu.roll(tg, X_OFF, 0), tg)
            for rows, acc in zip(halves, accs):
                p2 = ALPHA * s1_ref[rows, :] + acc
                xhat, rstd = _ln(p2)
                s2 = xhat * g_ref[...] + b_ref[...]
                err = jnp.where(isx[rows], s2 - tg[rows], 0.0)
                loss_ref[...] += 0.5 * jnp.sum(jnp.mean(err * err, axis=-1, keepdims=True))
                dy = err * (1.0 / d)
                dg_ref[...] += _rowsum(dy * xhat)
                db_ref[...] += _rowsum(dy)
                dp2 = _ln_bwd(dy * g_ref[...], xhat, rstd)
                dp2_ref[rows, :] = dp2
                dpb_ref[rows, :] = dp2.astype(BF16)

    row = pl.BlockSpec((tm, d), lambda i, j: (i, 0))
    vec = pl.BlockSpec((1, d), lambda i, j: (0, 0))
    tgt_row = pl.BlockSpec((pl.Element(tm), pl.Element(d)),
                           lambda i, j: (pl.multiple_of(jnp.maximum(i * tm - X_OFF * ((i * tm) // tp + 1), 0), CHUNK), 0))
    return pl.pallas_call(
        body, name="mlp_fwd", grid=(r // tm, nj),
        in_specs=[row, row, pl.BlockSpec((ns, d, th), lambda i, j: (j, 0, 0)), pl.BlockSpec((ns * th, d), lambda i, j: (j, 0)),
                  vec, vec, tgt_row],
        out_specs=[pl.BlockSpec((tm, ns * th), lambda i, j: (i, j)), row, row,
                   pl.BlockSpec((8, LANES), lambda i, j: (0, 0)), vec, vec],
        out_shape=[_sds((r, nh * th), BF16), _sds((r, d), F32), _sds((r, d), BF16), _sds((8, LANES), F32),
                   _sds((1, d), F32), _sds((1, d), F32)],
        scratch_shapes=[pltpu.VMEM((tm, d), F32)],
        compiler_params=_params(("arbitrary", "arbitrary")),
    )(s1, s1b, w1g, w2, g2, b2, tgt)


def _mlp_bwd_act(dp2, dpb, hm, w1g, w2, p1, g1, tm, ns):
    r, d = dp2.shape
    nh, _, th = w1g.shape
    nj = nh // ns

    def body(dp2_ref, dpb_ref, hm_ref, w1_ref, w2_ref, p1_ref, g_ref, dh_ref, dp1_ref, dg_ref, db_ref, acc_ref):
        i = pl.program_id(0)
        j = pl.program_id(1)

        @pl.when(jnp.logical_and(i == 0, j == 0))
        def _():
            dg_ref[...] = jnp.zeros_like(dg_ref)
            db_ref[...] = jnp.zeros_like(db_ref)

        @pl.when(j == 0)
        def _():
            acc_ref[...] = jnp.zeros_like(acc_ref)

        def mlp_rows(rows):
            dacts = [_mm_nt(dpb_ref[rows, :], w2_ref[s * th:(s + 1) * th, :]) for s in range(ns)]
            acc = acc_ref[rows, :]
            for s in range(ns):
                cols = slice(s * th, (s + 1) * th)
                dh = (dacts[s] * (2.0 * jnp.maximum(hm_ref[rows, cols].astype(F32), 0.0))).astype(BF16)
                dh_ref[rows, cols] = dh
                acc = acc + _mm_nt(dh, w1_ref[s])
            return acc

        @pl.when(j < nj - 1)
        def _():
            acc_ref[...] = mlp_rows(slice(None))

        @pl.when(j == nj - 1)
        def _():
            halves = [slice(0, tm // 2), slice(tm // 2, tm)]
            accs = [mlp_rows(rows) for rows in halves]
            for rows, acc in zip(halves, accs):
                ds1 = ALPHA * dp2_ref[rows, :] + acc
                xhat, rstd = _ln(p1_ref[rows, :])
                dg_ref[...] += _rowsum(ds1 * xhat)
                db_ref[...] += _rowsum(ds1)
                dp1_ref[rows, :] = _ln_bwd(ds1 * g_ref[...], xhat, rstd)

    row = pl.BlockSpec((tm, d), lambda i, j: (i, 0))
    vec = pl.BlockSpec((1, d), lambda i, j: (0, 0))
    blk = pl.BlockSpec((tm, ns * th), lambda i, j: (i, j))
    return pl.pallas_call(
        body, name="mlp_bwd_act", grid=(r // tm, nj),
        in_specs=[row, row, blk, pl.BlockSpec((ns, d, th), lambda i, j: (j, 0, 0)),
                  pl.BlockSpec((ns * th, d), lambda i, j: (j, 0)), row, vec],
        out_specs=[blk, row, vec, vec],
        out_shape=[_sds((r, nh * th), BF16), _sds((r, d), F32), _sds((1, d), F32), _sds((1, d), F32)],
        scratch_shapes=[pltpu.VMEM((tm, d), F32)],
        compiler_params=_params(("arbitrary", "arbitrary")),
    )(dp2, dpb, hm, w1g, w2, p1, g1)


def _mlp_bwd_w(s1b, hm, dh, dpb, nh, tm, ns):
    r, d = s1b.shape
    th = hm.shape[1] // nh

    def body(s1_ref, hm_ref, dh_ref, dp2_ref, dw1_ref, dw2_ref, a1_ref, a2_ref):
        i = pl.program_id(1)

        @pl.when(i == 0)
        def _():
            a1_ref[...] = jnp.zeros_like(a1_ref)
            a2_ref[...] = jnp.zeros_like(a2_ref)

        for s in range(ns):
            a1_ref[s] += _mm_tn(s1_ref[...], dh_ref[:, s * th:(s + 1) * th])
        for s in range(ns):
            act = jnp.square(jnp.maximum(hm_ref[:, s * th:(s + 1) * th].astype(F32), 0.0)).astype(BF16)
            a2_ref[s] += _mm_tn(act, dp2_ref[...])

        @pl.when(i == pl.num_programs(1) - 1)
        def _():
            dw1_ref[...] = a1_ref[...].astype(BF16)
            dw2_ref[...] = a2_ref[...].astype(BF16)

    row = pl.BlockSpec((tm, d), lambda j, i: (i, 0))
    blk = pl.BlockSpec((tm, ns * th), lambda j, i: (i, j))
    return pl.pallas_call(
        body, name="mlp_bwd_w", grid=(nh // ns, r // tm),
        in_specs=[row, blk, blk, row],
        out_specs=[pl.BlockSpec((ns, d, th), lambda j, i: (j, 0, 0)), pl.BlockSpec((ns, th, d), lambda j, i: (j, 0, 0))],
        out_shape=[_sds((nh, d, th), BF16), _sds((nh, th, d), BF16)],
        scratch_shapes=[pltpu.VMEM((ns, d, th), F32), pltpu.VMEM((ns, th, d), F32)],
        compiler_params=_params(("parallel", "arbitrary")),
    )(s1b, hm, dh, dpb)


def _outproj_bwd(dp1, co, go, w_out, dep, tm):
    r, d = dp1.shape
    dc = co.shape[1]

    def body(dp_ref, co_ref, go_ref, w_ref, dep_ref, dmi_ref, dw_ref, acc_ref):
        i = pl.program_id(0)

        @pl.when(i == 0)
        def _():
            acc_ref[...] = jnp.zeros_like(acc_ref)

        dpb = dp_ref[...].astype(BF16)
        dmi_ref[...] = _mm_nt(dpb, w_ref[...])
        acc_ref[0:dc, :] += _mm_tn(co_ref[...], dpb)
        acc_ref[dc:2 * dc, :] += _mm_tn(go_ref[...], dpb)

        @pl.when(i == pl.num_programs(0) - 1)
        def _():
            dw_ref[...] = acc_ref[...].astype(BF16)

    row = lambda w: pl.BlockSpec((tm, w), lambda i: (i, 0))
    full = pl.BlockSpec((2 * dc, d), lambda i: (0, 0))
    return pl.pallas_call(
        body, name="outproj_bwd", grid=(r // tm,),
        in_specs=[row(d), row(dc), row(dc), full, pl.BlockSpec(memory_space=pl.ANY)],
        out_specs=[row(2 * dc), full],
        out_shape=[_sds((r, 2 * dc), F32), _sds((2 * dc, d), BF16)],
        scratch_shapes=[pltpu.VMEM((2 * dc, d), F32)],
        compiler_params=_params(("arbitrary",)),
    )(dp1, co, go, w_out, dep)


def _gla_bwd(u, dmi, sta, gup, gb, gn, dep, bsz, nc, kc):
    r = u.shape[0]
    hv = GLA_HEADS * GLA_DV
    hk = GLA_HEADS * GLA_DK
    ns = nc // kc

    def body(qk_ref, v_ref, r_ref, gd_ref, dgo_ref, sta_ref, gup_ref, gb_ref, gn_ref, dep_ref,
             dqk_ref, dv_ref, dr_ref, dgd_ref, dgn_ref, dgb_ref, dgup_ref, dst_ref):
        bi = pl.program_id(0)
        t = pl.program_id(1)

        @pl.when(jnp.logical_and(bi == 0, t == 0))
        def _():
            dgn_ref[...] = jnp.zeros_like(dgn_ref)
            dgb_ref[...] = jnp.zeros_like(dgb_ref)
            dgup_ref[...] = jnp.zeros_like(dgup_ref)

        @pl.when(t == 0)
        def _():
            dst_ref[...] = jnp.zeros_like(dst_ref)

        ps, ri, ci = _gla_prep(qk_ref, gd_ref, gup_ref[...], gb_ref[...], (ns - 1 - t) * kc, kc)
        tril = ri >= ci
        items = [(j, h) for j in reversed(range(kc)) for h in range(GLA_HEADS)]
        ops = [_gla_heads(p, v_ref) for p in ps]
        op = lambda jh: ops[jh[0]][jh[1]]
        st = {jh: sta_ref[jh[0], jh[1]] for jh in items}
        stb = {jh: st[jh].astype(BF16) for jh in items}
        a = {jh: jnp.where(tril, _mm_nt(op(jh)["qe"], op(jh)["ke"]), 0.0).astype(BF16) for jh in items}
        o1 = {jh: _mm(a[jh], op(jh)["v"]) for jh in items}
        o2 = {jh: _mm_nt(op(jh)["qe"], stb[jh]) for jh in items}
        dob = {}
        dgn = jnp.zeros((1, GLA_DV), F32)
        for jh in items:
            rows, vs = ps[jh[0]]["rows"], op(jh)["vs"]
            o = o1[jh] + o2[jh]
            rr = r_ref[rows, vs]
            sr = _sigmoid(rr)
            rs = lax.rsqrt(jnp.mean(o * o, axis=-1, keepdims=True) + LN_EPS)
            y = o * rs
            dgo = dgo_ref[rows, vs]
            don = dgo * (rr * sr)
            dr_ref[rows, vs] = (dgo * (y * gn_ref[...]) * (sr * (1.0 + rr * (1.0 - sr)))).astype(BF16)
            dgn = dgn + _rowsum(don * y)
            dxn = don * gn_ref[...]
            dob[jh] = (rs * (dxn - y * jnp.mean(dxn * y, axis=-1, keepdims=True))).astype(BF16)
        da = {jh: jnp.where(tril, _mm_nt(dob[jh], op(jh)["v"]), 0.0).astype(BF16) for jh in items}
        dv1 = {jh: _mm_tn(a[jh], dob[jh]) for jh in items}
        dqe1 = {jh: _mm(da[jh], op(jh)["ke"]) for jh in items}
        dqe2 = {jh: _mm(dob[jh], stb[jh]) for jh in items}
        dke1 = {jh: _mm_tn(da[jh], op(jh)["qe"]) for jh in items}
        inc = {jh: _mm_tn(dob[jh], op(jh)["qe"]) for jh in items}
        dsts = [dst_ref[h] for h in range(GLA_HEADS)]
        dkd1, dgam1 = {}, {}
        for jh in items:
            j, h = jh
            dst = dsts[h]
            dstb = dst.astype(BF16)
            dv_ref[ps[j]["rows"], op(jh)["vs"]] = (dv1[jh] + _mm_nt(op(jh)["kd"], dstb)).astype(BF16)
            dkd1[jh] = _mm(op(jh)["v"], dstb)
            dgam1[jh] = _rowsum(dst * st[jh])
            dsts[h] = dst * ps[j]["gam"][:, op(jh)["ls"]] + inc[jh]
        for h in range(GLA_HEADS):
            dst_ref[h] = dsts[h]
        upper = (ri <= ci).astype(BF16)
        dbs, dbls = [], []
        for j in range(kc):
            p = ps[j]
            tiles = [[op((j, 2 * hp + h2)) for h2 in range(2)] for hp in range(GLA_HEADS // 2)]
            head = lambda d, hp, h2: d[j, 2 * hp + h2]
            lanes = lambda f: jnp.concatenate([f(hp) for hp in range(GLA_HEADS // 2)], axis=1)
            dqe = lanes(lambda hp: sum(jnp.where(tiles[hp][h2]["m"], head(dqe1, hp, h2) + head(dqe2, hp, h2), 0.0)
                                       for h2 in range(2)))
            dke = lanes(lambda hp: head(dke1, hp, 0) + head(dke1, hp, 1))
            dkd = lanes(lambda hp: sum(jnp.where(tiles[hp][h2]["m"], head(dkd1, hp, h2), 0.0) for h2 in range(2)))
            dgam = lanes(lambda hp: head(dgam1, hp, 0) + head(dgam1, hp, 1))
            dqk_ref[p["rows"], :hk] = (dqe * p["eb"] * QK_SCALE).astype(BF16)
            dqk_ref[p["rows"], hk:] = (dke * p["enb"] + dkd * p["ebl"]).astype(BF16)
            dkdkd = dkd * p["kd"]
            dbs.append(dqe * p["qe"] - dke * p["ke"] - dkdkd)
            dbls.append(_rowsum(dkdkd) + dgam * p["gam"])
        dlgs = _tri_mm_all(upper, dbs)
        dzb = []
        dgb = jnp.zeros((1, hk), F32)
        for j in range(kc):
            p = ps[j]
            dz = jnp.where(p["real"], (dlgs[j] + dbls[j]) * (1.0 / GLA_TAU) * _sigmoid(-p["z"]), 0.0)
            dgb = dgb + _rowsum(dz)
            dzb.append(dz.astype(BF16))
        dgup = sum(_mm_tn(ps[j]["gd"].astype(BF16), dzb[j]) for j in range(kc))
        for j in range(kc):
            dgd_ref[ps[j]["rows"], :] = _mm_nt(dzb[j], gup_ref[...]).astype(BF16)
        dgb_ref[...] += dgb
        dgup_ref[...] += dgup
        dgn_ref[...] += dgn

    rowblk = lambda col: (lambda b, t: (b * ns + ns - 1 - t, col))
    const = lambda b, t: (0, 0)
    return pl.pallas_call(
        body, name="gla_bwd", grid=(bsz, ns),
        in_specs=[pl.BlockSpec((kc * CHUNK, 2 * hk), rowblk(2)), pl.BlockSpec((kc * CHUNK, hv), rowblk(3)),
                  pl.BlockSpec((kc * CHUNK, hv), rowblk(4)), pl.BlockSpec((kc * CHUNK, LANES), rowblk(20)),
                  pl.BlockSpec((kc * CHUNK, hv), rowblk(1)),
                  pl.BlockSpec((kc, GLA_HEADS, LANES, LANES), lambda b, t: (b * ns + ns - 1 - t, 0, 0, 0)),
                  pl.BlockSpec((LANES, 256), const), pl.BlockSpec((1, 256), const), pl.BlockSpec((1, GLA_DV), const),
                  pl.BlockSpec(memory_space=pl.ANY)],
        out_specs=[pl.BlockSpec((kc * CHUNK, 2 * hk), rowblk(0)), pl.BlockSpec((kc * CHUNK, hv), rowblk(0)),
                   pl.BlockSpec((kc * CHUNK, hv), rowblk(0)), pl.BlockSpec((kc * CHUNK, LANES), rowblk(0)),
                   pl.BlockSpec((1, GLA_DV), const), pl.BlockSpec((1, 256), const), pl.BlockSpec((LANES, 256), const)],
        out_shape=[_sds((r, 2 * hk), BF16), _sds((r, hv), BF16), _sds((r, hv), BF16), _sds((r, LANES), BF16),
                   _sds((1, GLA_DV), F32), _sds((1, 256), F32), _sds((LANES, 256), F32)],
        scratch_shapes=[pltpu.VMEM((GLA_HEADS, LANES, LANES), F32)],
        compiler_params=_params(("arbitrary", "arbitrary")),
    )(u, u, u, u, dmi, sta, gup, gb, gn, dep)


def _conv_bwd(u, c, dmi, w32, cg, cbe, tp, tc, dc):
    r = u.shape[0]
    hb = tc // CONV_HALO
    nhalo = r // CONV_HALO

    def dconv(cv, dco, cg_ref, cbe_ref):
        xhat, rstd = _ln(cv)
        cn = xhat * cg_ref[...] + cbe_ref[...]
        sg = _sigmoid(cn)
        dcn = dco * (sg * (1.0 + cn * (1.0 - sg)))
        return _ln_bwd(dcn * cg_ref[...], xhat, rstd), dcn, xhat

    def body(a_ref, g_ref, ah_ref, gh_ref, c_ref, dco_ref, ch_ref, dcoh_ref, w_ref, cg_ref, cbe_ref,
             du_ref, dw_ref, dcb_ref, dcg_ref, dcbe_ref, hs_ref, dcs_ref, dw8_ref):
        t = pl.program_id(0)

        @pl.when(t == 0)
        def _():
            dw8_ref[...] = jnp.zeros_like(dw8_ref)
            dcb_ref[...] = jnp.zeros_like(dcb_ref)
            dcg_ref[...] = jnp.zeros_like(dcg_ref)
            dcbe_ref[...] = jnp.zeros_like(dcbe_ref)

        first = lax.rem(t * tc, tp) == 0
        last = lax.rem((t + 1) * tc, tp) == 0
        hh = ah_ref[...] * _sigmoid(gh_ref[...])
        hs_ref[0:CONV_HALO, :] = jnp.where(first, 0.0, hh)
        hs_ref[CONV_HALO:CONV_HALO + tc, :] = a_ref[...] * _sigmoid(g_ref[...])
        dch, _, _ = dconv(ch_ref[...], dcoh_ref[...], cg_ref, cbe_ref)
        dcs_ref[tc:tc + CONV_HALO, :] = jnp.where(last, 0.0, dch)

        def sub1(k, carry):
            r0 = pl.multiple_of(k * CONV_SUB, CONV_SUB)
            dcv, dcn, xhat = dconv(c_ref[pl.ds(r0, CONV_SUB), :], dco_ref[pl.ds(r0, CONV_SUB), :], cg_ref, cbe_ref)
            dcs_ref[pl.ds(r0, CONV_SUB), :] = dcv
            dcb_ref[...] += _rowsum(dcv)
            dcg_ref[...] += _rowsum(dcn * xhat)
            dcbe_ref[...] += _rowsum(dcn)
            return carry

        lax.fori_loop(0, tc // CONV_SUB, sub1, 0)

        def sub2(k, carry):
            r0 = pl.multiple_of(k * CONV_SUB, CONV_SUB)
            dwin = dcs_ref[pl.ds(r0, CONV_WIN), :]
            dh = _conv_taps(dwin, lambda o: w_ref[CONV_WIDTH - 1 - o:CONV_WIDTH - o, :], 0)
            av = a_ref[pl.ds(r0, CONV_SUB), :]
            sg = _sigmoid(g_ref[pl.ds(r0, CONV_SUB), :])
            du_ref[pl.ds(r0, CONV_SUB), 0:dc] = (dh * sg).astype(BF16)
            du_ref[pl.ds(r0, CONV_SUB), dc:2 * dc] = (dh * av * sg * (1.0 - sg)).astype(BF16)
            hwin = hs_ref[pl.ds(r0, CONV_WIN), :]
            dcv = dwin[0:CONV_SUB, :]
            for rho in range(8):
                offs = [o for o in range(2, 2 + CONV_WIDTH) if o % 8 == rho]
                rolled = hwin if rho == 0 else pltpu.roll(hwin, CONV_WIN - rho, 0)
                for o in offs:
                    m8 = o - rho
                    prod = dcv * rolled[m8:m8 + CONV_SUB, :]
                    dw8_ref[8 * (o - 2):8 * (o - 1), :] += jnp.sum(prod.reshape(CONV_SUB // 8, 8, dc), axis=0)
            return carry

        lax.fori_loop(0, tc // CONV_SUB, sub2, 0)

        @pl.when(t == pl.num_programs(0) - 1)
        def _():
            dw_ref[...] = jnp.zeros_like(dw_ref)
            for j in range(CONV_WIDTH):
                dw_ref[j:j + 1, :] = _rowsum(dw8_ref[8 * j:8 * (j + 1), :])

    vec = pl.BlockSpec((1, dc), lambda t: (0, 0))
    prev = lambda col: (lambda t: (jnp.maximum(t * hb - 1, 0), col))
    nxt = lambda col: (lambda t: (jnp.minimum((t + 1) * hb, nhalo - 1), col))
    return pl.pallas_call(
        body, name="conv_bwd", grid=(r // tc,),
        in_specs=[pl.BlockSpec((tc, dc), lambda t: (t, 0)), pl.BlockSpec((tc, dc), lambda t: (t, 1)),
                  pl.BlockSpec((CONV_HALO, dc), prev(0)), pl.BlockSpec((CONV_HALO, dc), prev(1)),
                  pl.BlockSpec((tc, dc), lambda t: (t, 0)), pl.BlockSpec((tc, dc), lambda t: (t, 0)),
                  pl.BlockSpec((CONV_HALO, dc), nxt(0)), pl.BlockSpec((CONV_HALO, dc), nxt(0)),
                  pl.BlockSpec((32, dc), lambda t: (0, 0)), vec, vec],
        out_specs=[pl.BlockSpec((tc, 2 * dc), lambda t: (t, 0)), pl.BlockSpec((32, dc), lambda t: (0, 0)), vec, vec, vec],
        out_shape=[_sds((r, 2 * dc), BF16), _sds((32, dc), F32), _sds((1, dc), F32), _sds((1, dc), F32), _sds((1, dc), F32)],
        scratch_shapes=[pltpu.VMEM((CONV_HALO + tc, dc), F32), pltpu.VMEM((tc + CONV_HALO, dc), F32),
                        pltpu.VMEM((8 * 32, dc), F32)],
        compiler_params=_params(("arbitrary",)),
    )(u, u, u, u, c, dmi, c, dmi, w32, cg, cbe)


def _inproj_bwd(dp1, dus, xsrc, g_in, w_in, dep, tp, seq, tx):
    r, d = dp1.shape
    widths = [x.shape[1] for x in dus]
    offs = [sum(widths[:k]) for k in range(len(widths))]
    n = w_in.shape[0]
    nd = len(dus)
    head = tx == 0
    rows = X_OFF if head else tx

    def body(*refs):
        dp_ref = refs[0]
        du_refs = refs[1:1 + nd]
        x_ref, g_ref, w_ref, _, out_ref, dg_ref, db_ref = refs[1 + nd:]
        i = pl.program_id(0)

        @pl.when(i == 0)
        def _():
            dg_ref[...] = jnp.zeros_like(dg_ref)
            db_ref[...] = jnp.zeros_like(db_ref)
            if head:
                out_ref[...] = jnp.zeros_like(out_ref)

        ds0 = ALPHA * dp_ref[...]
        for k in range(nd):
            ds0 = ds0 + _mm(du_refs[k][...], w_ref[offs[k]:offs[k] + widths[k], :])
        if head:
            ds0 = jnp.where(lax.broadcasted_iota(jnp.int32, (X_OFF, 1), 0) >= PAD_FRONT, ds0, 0.0)
        xhat, rstd = _ln(x_ref[...])
        dg_ref[...] += _rowsum(ds0 * xhat)
        db_ref[...] += _rowsum(ds0)
        dx = _ln_bwd(ds0 * g_ref[...], xhat, rstd)
        if head:
            out_ref[...] += dx[PAD_FRONT:X_OFF, :]
        else:
            out_ref[...] = dx

    if head:
        nb = tp // X_OFF
        row = lambda w: pl.BlockSpec((X_OFF, w), lambda i: (i * nb, 0))
        xspec = pl.BlockSpec((X_OFF, d), lambda i: (0, 0))
        ospec, oshape, steps = pl.BlockSpec((N_META, d), lambda i: (0, 0)), _sds((N_META, d), F32), r // tp
    else:
        start = _x_tile_row(tp, seq, tx)
        row = lambda w: pl.BlockSpec((pl.Element(tx), pl.Element(w)), lambda i: (start(i), 0))
        xspec = pl.BlockSpec((tx, d), lambda i: (i, 0))
        ospec, oshape, steps = xspec, _sds(xsrc.shape, F32), xsrc.shape[0] // tx
    vec = pl.BlockSpec((1, d), lambda i: (0, 0))
    return pl.pallas_call(
        body, name="inproj_bwd_head" if head else "inproj_bwd_x", grid=(steps,),
        in_specs=[row(d)] + [row(w) for w in widths] + [xspec, vec, pl.BlockSpec((n, d), lambda i: (0, 0)),
                                                        pl.BlockSpec(memory_space=pl.ANY)],
        out_specs=[ospec, vec, vec],
        out_shape=[oshape, _sds((1, d), F32), _sds((1, d), F32)],
        compiler_params=_params(("arbitrary",)),
    )(dp1, *dus, xsrc, g_in, w_in, dep)


def _inproj_bwd_w(s0, dus, tm):
    r, d = s0.shape
    widths = [x.shape[1] for x in dus]
    offs = [sum(widths[:k]) for k in range(len(widths))]
    nd = len(dus)

    def body(*refs):
        s_ref = refs[0]
        du_refs = refs[1:1 + nd]
        dw_ref = refs[1 + nd]
        i = pl.program_id(0)

        @pl.when(i == 0)
        def _():
            dw_ref[...] = jnp.zeros_like(dw_ref)

        for k in range(nd):
            dw_ref[offs[k]:offs[k] + widths[k], :] += _mm_tn(du_refs[k][...], s_ref[...])

    row = lambda w: pl.BlockSpec((tm, w), lambda i: (i, 0))
    return pl.pallas_call(
        body, name="inproj_bwd_w", grid=(r // tm,),
        in_specs=[row(d)] + [row(w) for w in widths],
        out_specs=pl.BlockSpec((sum(widths), d), lambda i: (0, 0)),
        out_shape=_sds((sum(widths), d), F32),
        compiler_params=_params(("arbitrary",)),
    )(s0, *dus)


def _local_step(x, tgt, meta, ln_in_g, ln_in_b, w_in, conv_w, conv_b, conv_ln_g, conv_ln_b, gate_up, gate_bias,
                gla_norm_g, late_weights, ln1_g, ln1_b, ln2_g, ln2_b, push):
    bsz, seq, d = x.shape
    tp = X_OFF + seq
    assert tp % CHUNK == 0
    nc = tp // CHUNK
    r = bsz * tp
    dc = conv_b.shape[1]
    tm = _pick_tile(tp, (352, 128, 64))
    tc = _pick_tile(tp, (704, 128, 64))

    x2 = x.reshape(bsz * seq, d)
    head = jnp.pad(meta, ((PAD_FRONT, 0), (0, 0)))
    tx = _pick_tile(seq, (512, 64))
    tgt_p = tgt.reshape(bsz * seq, d)
    w32 = jnp.pad(conv_w, ((0, 32 - CONV_WIDTH), (0, 0)))
    gup = jnp.pad(gate_up, ((0, LANES - GLA_RANK), (0, 0))).astype(BF16)

    s0, s0b = _ln_in_x(x2, ln_in_g, ln_in_b, tp, seq, tx)
    s0, s0b = _ln_in_head(head, ln_in_g, ln_in_b, s0, s0b, tp)
    tmm = _pick_tile(tp, (704, 128, 64))
    u = _inproj_fwd(s0b, w_in, tmm)
    c, co = _conv_fwd(u, w32, conv_b, conv_ln_g, conv_ln_b, tp, tc, dc)
    kc = _pick_tile(nc, (3, 2, 1))
    go, sta = _gla_fwd(u, gup, gate_bias, gla_norm_g, bsz, nc, kc)
    w_out, w1g, w2 = late_weights(go)
    nh = w1g.shape[0]
    tmm = _pick_tile(tp, (704, 128, 64))
    ns = 2
    p1, s1, s1b = _outproj_fwd(s0, co, go, w_out, ln1_g, ln1_b, tmm)
    hm, dp2, dpb, loss, dg2, db2 = _mlp_fwd(s1, s1b, w1g, w2, ln2_g, ln2_b, tgt_p, tp, tmm, ns)

    dh, dp1, dg1, db1 = _mlp_bwd_act(dp2, dpb, hm, w1g, w2, p1, ln1_g, tmm, ns)
    dw1, dw2 = _mlp_bwd_w(s1b, hm, dh, dpb, nh, tmm, ns)
    tok = push("ff", (dw1, dw2))
    dmi, dwo = _outproj_bwd(dp1, co, go, w_out, tok, tmm)
    tok = push("out", (dwo,))
    dqk, dv, dr, dgd, dgn, dgb, dgup = _gla_bwd(u, dmi, sta, gup, gate_bias, gla_norm_g, tok, bsz, nc, kc)
    dcv, dcw, dcb, dcg, dcbe = _conv_bwd(u, c, dmi, w32, conv_ln_g, conv_ln_b, tp, tc, dc)
    dus = [dcv, dqk, dv, dr, dgd]
    dwi = _inproj_bwd_w(s0b, dus, tm)
    tok = push("in", (dwi,))
    gx, dgx, dbx = _inproj_bwd(dp1, dus, x2, ln_in_g, w_in, tok, tp, seq, tx)
    dmeta, dgh, dbh = _inproj_bwd(dp1, dus, head, ln_in_g, w_in, tok, tp, seq, 0)

    return dict(loss=loss[0, 0], grad_x=gx.reshape(bsz, seq, d), meta_tokens=dmeta, ln_in_g=dgx + dgh, ln_in_b=dbx + dbh,
                conv_w=dcw[:CONV_WIDTH], conv_b=dcb, conv_ln_g=dcg, conv_ln_b=dcbe,
                gate_up=dgup[:GLA_RANK], gate_bias=dgb, gla_norm_g=dgn, ln1_g=dg1, ln1_b=db1, ln2_g=dg2, ln2_b=db2)


def _exchange(arrays, scatter, name):
    na = len(arrays)
    npeer = N_DEV - 1

    def body(*refs):
        srcs = refs[:na]
        outs = refs[na:2 * na]
        send_sems, recv_sems, local_sems = refs[2 * na:]
        xi, yi, ci = (lax.axis_index(a) for a in MESH_AXES)
        me = 4 * xi + 2 * yi + ci
        copies = []
        for a in range(na):
            own = srcs[a].at[me] if scatter[a] else srcs[a]
            cp = pltpu.make_async_copy(own, outs[a].at[me], local_sems.at[a])
            cp.start()
            copies.append(cp)
        remote = []
        for k in range(1, N_DEV):
            px, py, pc = xi ^ (k >> 2), yi ^ ((k >> 1) & 1), ci ^ (k & 1)
            peer = 4 * px + 2 * py + pc
            for a in range(na):
                src = srcs[a].at[peer] if scatter[a] else srcs[a]
                cp = pltpu.make_async_remote_copy(
                    src_ref=src, dst_ref=outs[a].at[me],
                    send_sem=send_sems.at[a * npeer + k - 1], recv_sem=recv_sems.at[a * npeer + k - 1],
                    device_id=(px, py, pc), device_id_type=pl.DeviceIdType.MESH)
                cp.start()
                remote.append(cp)
        for cp in remote:
            cp.wait()
        for cp in copies:
            cp.wait()

    out_shape = [_sds(a.shape if scatter[i] else (N_DEV,) + a.shape, a.dtype) for i, a in enumerate(arrays)]
    anyspec = pl.BlockSpec(memory_space=pl.ANY)
    return pl.pallas_call(
        body, name=name,
        in_specs=[anyspec] * na, out_specs=[anyspec] * na, out_shape=out_shape,
        scratch_shapes=[pltpu.SemaphoreType.DMA((na * npeer,)), pltpu.SemaphoreType.DMA((na * npeer,)),
                        pltpu.SemaphoreType.DMA((na,))],
    )(*arrays)


def _peers(xi, yi, ci):
    for k in range(1, N_DEV):
        px, py, pc = xi ^ (k >> 2), yi ^ ((k >> 1) & 1), ci ^ (k & 1)
        yield (px, py, pc), 4 * px + 2 * py + pc


def _sc_exchange(arrays, scatter, name, collective_id, after=None):
    na = len(arrays)
    npeer = N_DEV - 1
    ndep = 0 if after is None else 1

    def body(*refs):
        srcs = refs[:na]
        outs = refs[na + ndep:2 * na + ndep]
        send_sems, recv_sems, own_sems = refs[2 * na + ndep:]
        xi, yi, ci = (lax.axis_index(a) for a in MESH_AXES)
        me = 4 * xi + 2 * yi + ci
        barrier = pltpu.get_barrier_semaphore()
        for pos, _ in _peers(xi, yi, ci):
            pl.semaphore_signal(barrier, inc=1, device_id=pos, device_id_type=pl.DeviceIdType.MESH)
        pl.semaphore_wait(barrier, npeer)
        own = [pltpu.make_async_copy(srcs[a].at[me] if scatter[a] else srcs[a], outs[a].at[me], own_sems.at[a])
               for a in range(na)]
        for cp in own:
            cp.start()
        remote = []
        for a in range(na):
            for k, (pos, peer) in enumerate(_peers(xi, yi, ci)):
                cp = pltpu.make_async_remote_copy(
                    src_ref=srcs[a].at[peer] if scatter[a] else srcs[a], dst_ref=outs[a].at[me],
                    send_sem=send_sems.at[a * npeer + k], recv_sem=recv_sems.at[a * npeer + k],
                    device_id=pos, device_id_type=pl.DeviceIdType.MESH)
                cp.start()
                remote.append(cp)
        for cp in own:
            cp.wait()
        for cp in remote:
            cp.wait()

    out_type = [_sds(a.shape if scatter[i] else (N_DEV,) + a.shape, a.dtype) for i, a in enumerate(arrays)]
    sent = sum(a.size * a.dtype.itemsize // (N_DEV if scatter[i] else 1) for i, a in enumerate(arrays))
    return pl.kernel(
        body, out_type=out_type, mesh=plsc.ScalarSubcoreMesh(axis_name="seq", num_cores=1), name=name,
        scratch_types=[pltpu.SemaphoreType.DMA((na * npeer,)), pltpu.SemaphoreType.DMA((na * npeer,)),
                       pltpu.SemaphoreType.DMA((na,))],
        compiler_params=pltpu.CompilerParams(collective_id=collective_id),
        cost_estimate=pl.CostEstimate(flops=0, transcendentals=0, bytes_accessed=2 * N_DEV * sent,
                                      remote_bytes_transferred=npeer * sent),
    )(*arrays, *([] if after is None else [after]))


def _sc_gather(arrays, name, collective_id, after=None):
    na = len(arrays)
    ndep = 0 if after is None else 1
    npair = N_DEV - 1

    def body(*refs):
        srcs = refs[:na]
        outs = refs[na + ndep:2 * na + ndep]
        send_sems, recv_sems, own_sems = refs[2 * na + ndep:]
        xi, yi, ci = (lax.axis_index(a) for a in MESH_AXES)
        me = 4 * xi + 2 * yi + ci
        sibling = (xi, yi, 1 - ci)
        chips = [(1 - xi, yi), (xi, 1 - yi), (1 - xi, 1 - yi)]
        barrier = pltpu.get_barrier_semaphore()
        for pos, _ in _peers(xi, yi, ci):
            pl.semaphore_signal(barrier, inc=1, device_id=pos, device_id_type=pl.DeviceIdType.MESH)
        pl.semaphore_wait(barrier, npair)

        def copy(a, k, src, slot, to):
            return pltpu.make_async_remote_copy(
                src_ref=src, dst_ref=outs[a].at[slot], send_sem=send_sems.at[a * npair + k],
                recv_sem=recv_sems.at[a * npair + k], device_id=to, device_id_type=pl.DeviceIdType.MESH)

        own = [pltpu.make_async_copy(srcs[a], outs[a].at[me], own_sems.at[a]) for a in range(na)]
        for cp in own:
            cp.start()
        sent = []
        for a in range(na):
            sent.append(copy(a, 0, srcs[a], me, sibling))
            sent += [copy(a, 1 + j, srcs[a], me, (*chip, ci)) for j, chip in enumerate(chips)]
        for cp in sent:
            cp.start()
        for j, (cx, cy) in enumerate(chips):
            slot = 4 * cx + 2 * cy + ci
            for a in range(na):
                copy(a, 1 + j, srcs[a], slot, sibling).wait_recv()
                cp = copy(a, 4 + j, outs[a].at[slot], slot, sibling)
                cp.start()
                sent.append(cp)
        for a in range(na):
            copy(a, 0, srcs[a], me, sibling).wait_recv()
            for j in range(len(chips)):
                copy(a, 4 + j, srcs[a], me, sibling).wait_recv()
        for cp in sent:
            cp.wait_send()
        for cp in own:
            cp.wait()

    out_type = [_sds((N_DEV,) + a.shape, a.dtype) for a in arrays]
    sent_bytes = sum(a.size * a.dtype.itemsize for a in arrays)
    return pl.kernel(
        body, out_type=out_type, mesh=plsc.ScalarSubcoreMesh(axis_name="seq", num_cores=1), name=name,
        scratch_types=[pltpu.SemaphoreType.DMA((na * npair,)), pltpu.SemaphoreType.DMA((na * npair,)),
                       pltpu.SemaphoreType.DMA((na,))],
        compiler_params=pltpu.CompilerParams(collective_id=collective_id),
        cost_estimate=pl.CostEstimate(flops=0, transcendentals=0, bytes_accessed=2 * N_DEV * sent_bytes,
                                      remote_bytes_transferred=npair * sent_bytes),
    )(*arrays, *([] if after is None else [after]))


def _push_start(arrays, scatter, name, dep=None):
    na = len(arrays)
    shapes = [a.shape if scatter[i] else (N_DEV,) + a.shape for i, a in enumerate(arrays)]
    hbm = pl.BlockSpec(memory_space=pltpu.HBM)
    sem = pl.BlockSpec(memory_space=pltpu.SEMAPHORE)
    ndep = 0 if dep is None else 1

    def body(*refs):
        srcs = refs[:na]
        lands = refs[na:2 * na]
        send_sems, recv_sems = refs[2 * na + ndep:2 * na + ndep + 2]
        own_sems = refs[4 * na + ndep + 2]
        xi, yi, ci = (lax.axis_index(a) for a in MESH_AXES)
        me = 4 * xi + 2 * yi + ci
        own = [pltpu.make_async_copy(srcs[a].at[me] if scatter[a] else srcs[a], lands[a].at[me], own_sems.at[a])
               for a in range(na)]
        for cp in own:
            cp.start()
        for cp in own:
            cp.wait()
        for a in range(na):
            for pos, peer in _peers(xi, yi, ci):
                pltpu.make_async_remote_copy(
                    src_ref=srcs[a].at[peer] if scatter[a] else srcs[a], dst_ref=lands[a].at[me],
                    send_sem=send_sems.at[a], recv_sem=recv_sems.at[a],
                    device_id=pos, device_id_type=pl.DeviceIdType.MESH).start()

    ins = [pltpu.with_memory_space_constraint(a, pltpu.HBM) for a in arrays]
    ins += [pltpu.with_memory_space_constraint(lax.empty(s, a.dtype), pltpu.HBM) for s, a in zip(shapes, arrays)]
    res = pl.pallas_call(
        body, name=name,
        in_specs=[hbm] * (2 * na) + [pl.BlockSpec(memory_space=pl.ANY)] * ndep,
        out_specs=[sem, sem] + [hbm] * (2 * na),
        out_shape=[pltpu.SemaphoreType.DMA((na,)), pltpu.SemaphoreType.DMA((na,))]
                  + [pltpu.HBM(a.shape, a.dtype) for a in arrays] + [pltpu.HBM(s, a.dtype) for s, a in zip(shapes, arrays)],
        input_output_aliases={i: 2 + i for i in range(2 * na)},
        scratch_shapes=[pltpu.SemaphoreType.DMA((na,))],
        compiler_params=pltpu.CompilerParams(has_side_effects=pltpu.SideEffectType.DATAFLOW_SIDE_EFFECTING),
    )(*ins, *([] if dep is None else [dep]))
    return (res[0], res[1], list(res[2:2 + na]), list(res[2 + na:2 + 2 * na])), res[2]


def _push_wait(handle, after, name):
    send_sems, recv_sems, srcs, lands = handle
    na = len(srcs)
    hbm = pl.BlockSpec(memory_space=pltpu.HBM)
    sem = pl.BlockSpec(memory_space=pltpu.SEMAPHORE)

    def body(*refs):
        land_refs = refs[na:2 * na]
        send_ref, recv_ref = refs[2 * na:2 * na + 2]
        me = tuple(lax.axis_index(a) for a in MESH_AXES)
        for a in range(na):
            seven = land_refs[a].at[pl.ds(0, N_DEV - 1)]
            cp = pltpu.make_async_remote_copy(src_ref=seven, dst_ref=seven, send_sem=send_ref.at[a], recv_sem=recv_ref.at[a],
                                              device_id=me, device_id_type=pl.DeviceIdType.MESH)
            cp.wait_send()
            cp.wait_recv()

    res = pl.pallas_call(
        body, name=name,
        in_specs=[hbm] * (2 * na) + [sem, sem, pl.BlockSpec(memory_space=pl.ANY)],
        out_specs=[hbm] * (2 * na),
        out_shape=[pltpu.HBM(a.shape, a.dtype) for a in srcs] + [pltpu.HBM(a.shape, a.dtype) for a in lands],
        input_output_aliases={i: i for i in range(2 * na)},
        compiler_params=pltpu.CompilerParams(has_side_effects=pltpu.SideEffectType.DATAFLOW_SIDE_EFFECTING),
    )(*srcs, *lands, send_sems, recv_sems, after)
    return list(res[na:])


def _adamw(w, g, m, v):
    m = ADAM_B1 * m + (1.0 - ADAM_B1) * g
    v = ADAM_B2 * v + (1.0 - ADAM_B2) * jnp.square(g)
    m_hat = m / (1.0 - ADAM_B1 ** ADAM_STEP)
    v_hat = v / (1.0 - ADAM_B2 ** ADAM_STEP)
    delta = -ADAM_LR * (m_hat / (jnp.sqrt(v_hat) + ADAM_EPS) + ADAM_WD * w)
    return delta, m, v


def _sum_devices(ref):
    g = ref[0].astype(F32)
    for k in range(1, N_DEV):
        g = g + ref[k].astype(F32)
    return g


def _update_big(parts, w, m, v, name):
    rows, cols = w.shape

    def body(p_ref, w_ref, m_ref, v_ref, g_ref, d_ref, nm_ref, nv_ref):
        g = _sum_devices(p_ref)
        g_ref[...] = g
        d_ref[...], nm_ref[...], nv_ref[...] = _adamw(w_ref[...], g, m_ref[...], v_ref[...])

    if rows % 16 == 0:
        tr = _pick_tile(rows, (128, 64, 16))
        steps, blk = rows // tr, pl.BlockSpec((tr, cols), lambda i: (i, 0))
        pblk = pl.BlockSpec((N_DEV, tr, cols), lambda i: (0, i, 0))
    else:
        tcol = 2 * LANES
        steps, blk = cols // tcol, pl.BlockSpec((rows, tcol), lambda i: (0, i))
        pblk = pl.BlockSpec((N_DEV, rows, tcol), lambda i: (0, 0, i))
    return pl.pallas_call(
        body, name=name, grid=(steps,),
        in_specs=[pblk, blk, blk, blk],
        out_specs=[blk] * 4, out_shape=[_sds((rows, cols), F32)] * 4,
        compiler_params=_params(("parallel",)),
    )(parts, w, m, v)


_VEC_ORDER = ("ln_in_g", "ln_in_b", "conv_b", "conv_ln_g", "conv_ln_b", "gate_bias", "gla_norm_g",
              "ln1_g", "ln1_b", "ln2_g", "ln2_b")
_SHARDED_SMALL = (("meta_tokens", 0, N_META, LANES), ("conv_w", N_META, CONV_WIDTH, None), ("gate_up", N_META + 32, GLA_RANK, None))


def _update_small(parts_sh, parts_vec, wmv):
    names = [s[0] for s in _SHARDED_SMALL] + list(_VEC_ORDER)
    flat = [a for nme in names for a in wmv[nme]]
    nv = len(_VEC_ORDER)

    def body(*refs):
        sh_ref, vec_ref = refs[0], refs[1]
        ins = refs[2:2 + len(flat)]
        outs = refs[2 + len(flat):2 + len(flat) + 4 * len(names)]
        loss_ref = refs[2 + len(flat) + 4 * len(names)]
        gsh_ref, gvec_ref = refs[-2:]
        gsh_ref[...] = _sum_devices(sh_ref)
        gvec_ref[...] = _sum_devices(vec_ref)
        loss_ref[...] = gvec_ref[nv:nv + 1, :]
        for idx, nme in enumerate(names):
            w_ref, m_ref, v_ref = ins[3 * idx:3 * idx + 3]
            rows, cols = w_ref.shape
            if idx < len(_SHARDED_SMALL):
                r0 = _SHARDED_SMALL[idx][1]
                g = gsh_ref[r0:r0 + rows, 0:cols]
            else:
                j = idx - len(_SHARDED_SMALL)
                g = gvec_ref[j:j + 1, 0:cols]
            o = outs[4 * idx:4 * idx + 4]
            o[0][...] = g
            o[1][...], o[2][...], o[3][...] = _adamw(w_ref[...], g, m_ref[...], v_ref[...])

    out_shape = [_sds(wmv[nme][0].shape, F32) for nme in names for _ in range(4)] + [_sds((1, parts_vec.shape[2]), F32)]
    vmem = pl.BlockSpec(memory_space=pltpu.VMEM)
    res = pl.pallas_call(
        body, name="update_small", out_shape=out_shape,
        in_specs=[vmem] * (2 + len(flat)), out_specs=[vmem] * len(out_shape),
        scratch_shapes=[pltpu.VMEM(parts_sh.shape[1:], F32), pltpu.VMEM(parts_vec.shape[1:], F32)],
    )(parts_sh, parts_vec, *flat)
    return {nme: res[4 * i:4 * i + 4] for i, nme in enumerate(names)}, res[-1][0, 0]


_WEIGHTS = ("meta_tokens", "ln_in_g", "ln_in_b", "w_in", "conv_w", "conv_b", "conv_ln_g", "conv_ln_b", "gate_up",
            "gate_bias", "gla_norm_g", "w_out", "ln1_g", "ln1_b", "w_ff1", "w_ff2", "ln2_g", "ln2_b")


def kernel(x, meta_tokens, ln_in_g, ln_in_b, w_in, conv_w, conv_b, conv_ln_g, conv_ln_b, gate_up, gate_bias, gla_norm_g, w_out, ln1_g, ln1_b, w_ff1, w_ff2, ln2_g, ln2_b, loss_target, m_meta_tokens, m_ln_in_g, m_ln_in_b, m_w_in, m_conv_w, m_conv_b, m_conv_ln_g, m_conv_ln_b, m_gate_up, m_gate_bias, m_gla_norm_g, m_w_out, m_ln1_g, m_ln1_b, m_w_ff1, m_w_ff2, m_ln2_g, m_ln2_b, v_meta_tokens, v_ln_in_g, v_ln_in_b, v_w_in, v_conv_w, v_conv_b, v_conv_ln_g, v_conv_ln_b, v_gate_up, v_gate_bias, v_gla_norm_g, v_w_out, v_ln1_g, v_ln1_b, v_w_ff1, v_w_ff2, v_ln2_g, v_ln2_b):
    w = dict(meta_tokens=meta_tokens, ln_in_g=ln_in_g, ln_in_b=ln_in_b, w_in=w_in, conv_w=conv_w, conv_b=conv_b,
             conv_ln_g=conv_ln_g, conv_ln_b=conv_ln_b, gate_up=gate_up, gate_bias=gate_bias, gla_norm_g=gla_norm_g,
             w_out=w_out, ln1_g=ln1_g, ln1_b=ln1_b, w_ff1=w_ff1, w_ff2=w_ff2, ln2_g=ln2_g, ln2_b=ln2_b)
    mom = dict(meta_tokens=m_meta_tokens, ln_in_g=m_ln_in_g, ln_in_b=m_ln_in_b, w_in=m_w_in, conv_w=m_conv_w,
               conv_b=m_conv_b, conv_ln_g=m_conv_ln_g, conv_ln_b=m_conv_ln_b, gate_up=m_gate_up, gate_bias=m_gate_bias,
               gla_norm_g=m_gla_norm_g, w_out=m_w_out, ln1_g=m_ln1_g, ln1_b=m_ln1_b, w_ff1=m_w_ff1, w_ff2=m_w_ff2,
               ln2_g=m_ln2_g, ln2_b=m_ln2_b)
    var = dict(meta_tokens=v_meta_tokens, ln_in_g=v_ln_in_g, ln_in_b=v_ln_in_b, w_in=v_w_in, conv_w=v_conv_w,
               conv_b=v_conv_b, conv_ln_g=v_conv_ln_g, conv_ln_b=v_conv_ln_b, gate_up=v_gate_up, gate_bias=v_gate_bias,
               gla_norm_g=v_gla_norm_g, w_out=v_w_out, ln1_g=v_ln1_g, ln1_b=v_ln1_b, w_ff1=v_w_ff1, w_ff2=v_w_ff2,
               ln2_g=v_ln2_g, ln2_b=v_ln2_b)
    shapes = {k: a.shape for k, a in w.items()}

    def two_d(a):
        return a.reshape(1, -1) if a.ndim == 1 else a.reshape(a.shape[-2:])

    w2d = {k: two_d(a) for k, a in w.items()}
    m2d = {k: two_d(a) for k, a in mom.items()}
    v2d = {k: two_d(a) for k, a in var.items()}
    d = x.shape[-1]
    d_in = w2d["w_in"].shape[1] * N_DEV
    d_in_p = -(-d_in // LANES) * LANES

    for dct in (w2d, m2d, v2d):
        dct["w_in"] = dct["w_in"].T
    g_in, g_meta, g_conv, g_gup = _sc_gather(
        [w2d["w_in"].astype(BF16), w2d["meta_tokens"], w2d["conv_w"], w2d["gate_up"]], "gather_first", 0)
    g_out, g_ff1, g_ff2 = _sc_gather(
        [w2d["w_out"].astype(BF16), w2d["w_ff1"].astype(BF16), w2d["w_ff2"].astype(BF16)], "gather_late", 1)
    w_in_full = jnp.pad(g_in.reshape(d_in, d), ((0, d_in_p - d_in), (0, 0)))
    meta_full = g_meta.transpose(1, 0, 2).reshape(N_META, d)
    conv_w_full = g_conv.transpose(1, 0, 2).reshape(CONV_WIDTH, -1)
    gate_up_full = g_gup.transpose(1, 0, 2).reshape(GLA_RANK, -1)

    def late_weights(after):
        return g_out.reshape(-1, d), g_ff1, g_ff2.reshape(-1, d)

    pushed = {}

    def push(tag, grads):
        if tag == "ff":
            pushed["ff1"], pushed["ff2"] = _sc_exchange(list(grads), [True, True], "scatter_ff", 2)
        elif tag == "out":
            pushed["p_out"] = grads[0].reshape(N_DEV, -1, d)
        else:
            p_in = grads[0][:d_in].reshape(N_DEV, d_in // N_DEV, d).astype(BF16)
            pushed["in"], pushed["out"] = _sc_exchange([p_in, pushed["p_out"]], [True, True], "scatter_rest", 3,
                                                       after=pushed["ff1"])
        return grads[0]

    res = _local_step(x, loss_target, meta_full, w2d["ln_in_g"], w2d["ln_in_b"], w_in_full, conv_w_full, w2d["conv_b"],
                      w2d["conv_ln_g"], w2d["conv_ln_b"], gate_up_full, w2d["gate_bias"], w2d["gla_norm_g"], late_weights,
                      w2d["ln1_g"], w2d["ln1_b"], w2d["ln2_g"], w2d["ln2_b"], push)

    dc = res["conv_w"].shape[1]
    hk = res["gate_up"].shape[1]
    sh_meta = res["meta_tokens"].reshape(N_META, N_DEV, LANES).transpose(1, 0, 2)
    sh_conv = jnp.pad(res["conv_w"].reshape(CONV_WIDTH, N_DEV, dc // N_DEV).transpose(1, 0, 2),
                      ((0, 0), (0, 32 - CONV_WIDTH), (0, LANES - dc // N_DEV)))
    sh_gup = jnp.pad(res["gate_up"].reshape(GLA_RANK, N_DEV, hk // N_DEV).transpose(1, 0, 2),
                     ((0, 0), (0, 0), (0, LANES - hk // N_DEV)))
    p_sh = jnp.concatenate([sh_meta, sh_conv, sh_gup], axis=1)
    p_vec = jnp.concatenate([jnp.pad(res[k], ((0, 0), (0, d - res[k].shape[1]))) for k in _VEC_ORDER]
                            + [jnp.full((1, d), res["loss"], F32), jnp.zeros((15 - len(_VEC_ORDER), d), F32)], axis=0)

    r_sh, r_vec = _exchange([p_sh, p_vec], [True, False], "scatter_small")
    r_ff1, r_ff2, r_out, r_in = pushed["ff1"], pushed["ff2"], pushed["out"], pushed["in"]

    upd = {}
    upd["w_in"] = [a.T for a in _update_big(r_in, w2d["w_in"], m2d["w_in"], v2d["w_in"], "update_w_in")]
    upd["w_out"] = _update_big(r_out, w2d["w_out"], m2d["w_out"], v2d["w_out"], "update_w_out")
    upd["w_ff1"] = _update_big(r_ff1, w2d["w_ff1"], m2d["w_ff1"], v2d["w_ff1"], "update_w_ff1")
    upd["w_ff2"] = _update_big(r_ff2, w2d["w_ff2"], m2d["w_ff2"], v2d["w_ff2"], "update_w_ff2")
    small = [s[0] for s in _SHARDED_SMALL] + list(_VEC_ORDER)
    upd_small, loss = _update_small(r_sh, r_vec, {k: (w2d[k], m2d[k], v2d[k]) for k in small})
    upd.update(upd_small)

    outs = [loss, res["grad_x"]]
    for j in range(4):
        outs += [upd[k][j].reshape(shapes[k]) for k in _WEIGHTS]
    return tuple(outs)
```

```python
import functools

import jax
import jax.numpy as jnp
from jax import lax
from jax.experimental import pallas as pl
from jax.experimental.pallas import tpu as pltpu
from jax.experimental.pallas import tpu_sc as plsc

F32 = jnp.float32
BF16 = jnp.bfloat16

N_META = 16
CHUNK = 64
PAD_FRONT = (-N_META) % CHUNK
X_OFF = PAD_FRONT + N_META
CONV_WIDTH = 31
CONV_HALO = 32
CONV_SUB = 64
CONV_WIN = CONV_SUB + CONV_HALO
GLA_HEADS = 4
GLA_DK = 64
GLA_DV = 128
GLA_RANK = 16
GLA_TAU = 16.0
QK_SCALE = GLA_DK ** -0.5
LN_EPS = 1e-5
ALPHA = 2.0 ** 0.25
LANES = 128
N_DEV = 8
ADAM_LR = 0.001
ADAM_B1 = 0.9
ADAM_B2 = 0.999
ADAM_EPS = 1e-08
ADAM_WD = 0.01
ADAM_STEP = 10
VMEM_LIMIT = 56 * 1024 * 1024
MESH_AXES = ("x", "y", "c")


def _sds(shape, dtype):
    return jax.ShapeDtypeStruct(shape, dtype)


def _mm(a, b):
    return jnp.dot(a, b, preferred_element_type=F32)


def _mm_nt(a, b):
    return lax.dot_general(a, b, (((1,), (1,)), ((), ())), preferred_element_type=F32)


def _mm_tn(a, b):
    return lax.dot_general(a, b, (((0,), (0,)), ((), ())), preferred_element_type=F32)


def _sigmoid(x):
    return 1.0 / (1.0 + jnp.exp(-x))


def _log_sigmoid(z):
    return jnp.minimum(z, 0.0) - jnp.log(1.0 + jnp.exp(-jnp.abs(z)))


def _ln(x):
    mu = jnp.mean(x, axis=-1, keepdims=True)
    xc = x - mu
    var = jnp.mean(xc * xc, axis=-1, keepdims=True)
    rstd = lax.rsqrt(var + LN_EPS)
    return xc * rstd, rstd


def _ln_bwd(dyg, xhat, rstd):
    m1 = jnp.mean(dyg, axis=-1, keepdims=True)
    m2 = jnp.mean(dyg * xhat, axis=-1, keepdims=True)
    return rstd * (dyg - m1 - xhat * m2)


def _rowsum(x):
    return jnp.sum(x, axis=0, keepdims=True)


def _row_in_seq(i, tm, tp):
    base = lax.rem(i * tm, tp)
    return base + lax.broadcasted_iota(jnp.int32, (tm, 1), 0)


def _split3(x):
    hi = x.astype(BF16)
    r1 = x - hi.astype(F32)
    mid = r1.astype(BF16)
    lo = (r1 - mid.astype(F32)).astype(BF16)
    return hi, mid, lo


def _tri_mm(tri, x):
    hi, mid, lo = _split3(x)
    return _mm(tri, hi) + _mm(tri, mid) + _mm(tri, lo)


def _params(sem):
    return pltpu.CompilerParams(dimension_semantics=sem, vmem_limit_bytes=VMEM_LIMIT)


def _pick_tile(n, prefs):
    for t in prefs:
        if n % t == 0:
            return t
    raise ValueError(f"no tile for {n}")


def _x_tile_row(tp, seq, tx):
    tps = seq // tx
    return lambda i: pl.multiple_of((i // tps) * tp + X_OFF + (i % tps) * tx, CHUNK)


def _ln_in_x(x2, g, b, tp, seq, tx):
    rx, d = x2.shape
    r = rx // seq * tp
    row = _x_tile_row(tp, seq, tx)

    def body(x_ref, g_ref, b_ref, s0_ref, sb_ref):
        xhat, _ = _ln(x_ref[...])
        s = xhat * g_ref[...] + b_ref[...]
        s0_ref[...] = s
        sb_ref[...] = s.astype(BF16)

    out = pl.BlockSpec((pl.Element(tx), pl.Element(d)), lambda i: (row(i), 0))
    return pl.pallas_call(
        body, name="ln_in_x", grid=(rx // tx,),
        in_specs=[pl.BlockSpec((tx, d), lambda i: (i, 0)), pl.BlockSpec((1, d), lambda i: (0, 0)),
                  pl.BlockSpec((1, d), lambda i: (0, 0))],
        out_specs=[out, out],
        out_shape=[_sds((r, d), F32), _sds((r, d), BF16)],
        compiler_params=_params(("parallel",)),
    )(x2, g, b)


def _ln_in_head(head, g, b, s0, s0b, tp):
    r, d = s0.shape
    nb = tp // X_OFF

    def body(h_ref, g_ref, b_ref, s0_in, sb_in, s0_ref, sb_ref):
        xhat, _ = _ln(h_ref[...])
        real = lax.broadcasted_iota(jnp.int32, (X_OFF, 1), 0) >= PAD_FRONT
        s = jnp.where(real, xhat * g_ref[...] + b_ref[...], 0.0)
        s0_ref[...] = s
        sb_ref[...] = s.astype(BF16)

    anyspec = pl.BlockSpec(memory_space=pl.ANY)
    out = pl.BlockSpec((X_OFF, d), lambda i: (i * nb, 0))
    return pl.pallas_call(
        body, name="ln_in_head", grid=(r // tp,),
        in_specs=[pl.BlockSpec((X_OFF, d), lambda i: (0, 0)), pl.BlockSpec((1, d), lambda i: (0, 0)),
                  pl.BlockSpec((1, d), lambda i: (0, 0)), anyspec, anyspec],
        out_specs=[out, out],
        out_shape=[_sds((r, d), F32), _sds((r, d), BF16)],
        input_output_aliases={3: 0, 4: 1},
        compiler_params=_params(("parallel",)),
    )(head, g, b, s0, s0b)


def _inproj_fwd(s0b, w_in, tm):
    r, d = s0b.shape
    n = w_in.shape[0]

    def body(s_ref, w_ref, u_ref):
        u_ref[...] = _mm_nt(s_ref[...], w_ref[...])

    return pl.pallas_call(
        body, name="inproj_fwd", grid=(r // tm,),
        in_specs=[pl.BlockSpec((tm, d), lambda i: (i, 0)), pl.BlockSpec((n, d), lambda i: (0, 0))],
        out_specs=pl.BlockSpec((tm, n), lambda i: (i, 0)),
        out_shape=_sds((r, n), F32),
        compiler_params=_params(("parallel",)),
    )(s0b, w_in)


def _conv_taps(win, coef, lo):
    acc = None
    for rho in range(8):
        offs = [o for o in range(lo, lo + CONV_WIDTH) if o % 8 == rho]
        if not offs:
            continue
        rolled = win if rho == 0 else pltpu.roll(win, CONV_WIN - rho, 0)
        for o in offs:
            m8 = o - rho
            term = rolled[m8:m8 + CONV_SUB, :] * coef(o)
            acc = term if acc is None else acc + term
    return acc


def _conv_fwd(u, w32, cb, cg, cbe, tp, tc, dc):
    r = u.shape[0]
    hb = tc // CONV_HALO

    def body(a_ref, g_ref, ah_ref, gh_ref, w_ref, cb_ref, cg_ref, cbe_ref, c_ref, co_ref, hs_ref):
        t = pl.program_id(0)
        first = lax.rem(t * tc, tp) == 0
        hh = ah_ref[...] * _sigmoid(gh_ref[...])
        hs_ref[0:CONV_HALO, :] = jnp.where(first, 0.0, hh)
        hs_ref[CONV_HALO:CONV_HALO + tc, :] = a_ref[...] * _sigmoid(g_ref[...])

        def sub(k, carry):
            r0 = pl.multiple_of(k * CONV_SUB, CONV_SUB)
            win = hs_ref[pl.ds(r0, CONV_WIN), :]
            c = _conv_taps(win, lambda o: w_ref[o - 2:o - 1, :], 2) + cb_ref[...]
            c_ref[pl.ds(r0, CONV_SUB), :] = c
            xhat, _ = _ln(c)
            cn = xhat * cg_ref[...] + cbe_ref[...]
            co_ref[pl.ds(r0, CONV_SUB), :] = (cn * _sigmoid(cn)).astype(BF16)
            return carry

        lax.fori_loop(0, tc // CONV_SUB, sub, 0)

    vec = pl.BlockSpec((1, dc), lambda t: (0, 0))
    return pl.pallas_call(
        body, name="conv_fwd", grid=(r // tc,),
        in_specs=[pl.BlockSpec((tc, dc), lambda t: (t, 0)), pl.BlockSpec((tc, dc), lambda t: (t, 1)),
                  pl.BlockSpec((CONV_HALO, dc), lambda t: (jnp.maximum(t * hb - 1, 0), 0)),
                  pl.BlockSpec((CONV_HALO, dc), lambda t: (jnp.maximum(t * hb - 1, 0), 1)),
                  pl.BlockSpec((32, dc), lambda t: (0, 0)), vec, vec, vec],
        out_specs=[pl.BlockSpec((tc, dc), lambda t: (t, 0)), pl.BlockSpec((tc, dc), lambda t: (t, 0))],
        out_shape=[_sds((r, dc), F32), _sds((r, dc), BF16)],
        scratch_shapes=[pltpu.VMEM((CONV_HALO + tc, dc), F32)],
        compiler_params=_params(("parallel",)),
    )(u, u, u, u, w32, cb, cg, cbe)


def _tri_mm_all(tri, xs):
    parts = [_split3(x) for x in xs]
    acc = [None] * len(xs)
    for t in range(3):
        for j in range(len(xs)):
            term = _mm(tri, parts[j][t])
            acc[j] = term if t == 0 else acc[j] + term
    return acc


def _gla_prep(qk_ref, gd_ref, gup, gb, n0, kc):
    rows = [slice(j * CHUNK, (j + 1) * CHUNK) for j in range(kc)]
    ri = lax.broadcasted_iota(jnp.int32, (CHUNK, CHUNK), 0)
    ci = lax.broadcasted_iota(jnp.int32, (CHUNK, CHUNK), 1)
    low = (ri >= ci).astype(BF16)
    hk = GLA_HEADS * GLA_DK
    gds = [gd_ref[rw, :] for rw in rows]
    zs = [_mm(g.astype(BF16), gup) + gb for g in gds]
    reals = [(n0 + j) * CHUNK + lax.broadcasted_iota(jnp.int32, (CHUNK, 1), 0) >= PAD_FRONT for j in range(kc)]
    lgs = [jnp.where(reals[j], _log_sigmoid(zs[j]) * (1.0 / GLA_TAU), 0.0) for j in range(kc)]
    bs = _tri_mm_all(low, lgs)
    out = []
    for j in range(kc):
        b, bl = bs[j], _rowsum(lgs[j])
        q = qk_ref[rows[j], :hk] * QK_SCALE
        k = qk_ref[rows[j], hk:]
        eb, enb, ebl = jnp.exp(b), jnp.exp(-b), jnp.exp(bl - b)
        out.append(dict(rows=rows[j], gd=gds[j], z=zs[j], real=reals[j], eb=eb, enb=enb, ebl=ebl, gam=jnp.exp(bl),
                        qe=q * eb, ke=k * enb, kd=k * ebl))
    return out, ri, ci


def _gla_heads(p, v_ref):
    ops = []
    for h in range(GLA_HEADS):
        hp, h2 = divmod(h, 2)
        ls = slice(hp * LANES, (hp + 1) * LANES)
        m = _head_mask(h2)
        ops.append(dict(ls=ls, m=m, vs=slice(h * GLA_DV, (h + 1) * GLA_DV),
                        qe=jnp.where(m, p["qe"][:, ls], 0.0).astype(BF16),
                        kd=jnp.where(m, p["kd"][:, ls], 0.0).astype(BF16),
                        ke=p["ke"][:, ls].astype(BF16),
                        v=v_ref[p["rows"], h * GLA_DV:(h + 1) * GLA_DV].astype(BF16)))
    return ops


def _head_mask(h2):
    lane = lax.broadcasted_iota(jnp.int32, (1, LANES), 1)
    return (lane < GLA_DK) if h2 == 0 else (lane >= GLA_DK)


def _gla_fwd(u, gup, gb, gn, bsz, nc, kc):
    r = u.shape[0]
    hv = GLA_HEADS * GLA_DV
    ns = nc // kc

    def body(qk_ref, v_ref, r_ref, gd_ref, gup_ref, gb_ref, gn_ref, go_ref, sta_ref, st_ref):
        t = pl.program_id(1)

        @pl.when(t == 0)
        def _():
            st_ref[...] = jnp.zeros_like(st_ref)

        ps, ri, ci = _gla_prep(qk_ref, gd_ref, gup_ref[...], gb_ref[...], t * kc, kc)
        tril = ri >= ci
        items = [(j, h) for j in range(kc) for h in range(GLA_HEADS)]
        ops = [_gla_heads(p, v_ref) for p in ps]
        a = {jh: jnp.where(tril, _mm_nt(ops[jh[0]][jh[1]]["qe"], ops[jh[0]][jh[1]]["ke"]), 0.0).astype(BF16) for jh in items}
        oi = {jh: _mm(a[jh], ops[jh[0]][jh[1]]["v"]) for jh in items}
        inc = {jh: _mm_tn(ops[jh[0]][jh[1]]["v"], ops[jh[0]][jh[1]]["kd"]) for jh in items}
        sts = [st_ref[h] for h in range(GLA_HEADS)]
        for j, h in items:
            op, p = ops[j][h], ps[j]
            st = sts[h]
            sta_ref[j, h] = st
            o = oi[j, h] + _mm_nt(op["qe"], st.astype(BF16))
            sts[h] = st * p["gam"][:, op["ls"]] + inc[j, h]
            rs = lax.rsqrt(jnp.mean(o * o, axis=-1, keepdims=True) + LN_EPS)
            rr = r_ref[p["rows"], op["vs"]]
            go_ref[p["rows"], op["vs"]] = (o * rs * gn_ref[...] * (rr * _sigmoid(rr))).astype(BF16)
        for h in range(GLA_HEADS):
            st_ref[h] = sts[h]

    rowblk = lambda col: (lambda b, t: (b * ns + t, col))
    const = lambda b, t: (0, 0)
    return pl.pallas_call(
        body, name="gla_fwd", grid=(bsz, ns),
        in_specs=[pl.BlockSpec((kc * CHUNK, 512), rowblk(2)), pl.BlockSpec((kc * CHUNK, hv), rowblk(3)),
                  pl.BlockSpec((kc * CHUNK, hv), rowblk(4)), pl.BlockSpec((kc * CHUNK, LANES), rowblk(20)),
                  pl.BlockSpec((LANES, 256), const), pl.BlockSpec((1, 256), const), pl.BlockSpec((1, GLA_DV), const)],
        out_specs=[pl.BlockSpec((kc * CHUNK, hv), rowblk(0)),
                   pl.BlockSpec((kc, GLA_HEADS, LANES, LANES), lambda b, t: (b * ns + t, 0, 0, 0))],
        out_shape=[_sds((r, hv), BF16), _sds((bsz * nc, GLA_HEADS, LANES, LANES), F32)],
        scratch_shapes=[pltpu.VMEM((GLA_HEADS, LANES, LANES), F32)],
        compiler_params=_params(("parallel", "arbitrary")),
    )(u, u, u, u, gup, gb, gn)


def _outproj_fwd(s0, co, go, w_out, g1, b1, tm):
    r, d = s0.shape
    dc = co.shape[1]

    def body(s0_ref, co_ref, go_ref, w_ref, g_ref, b_ref, p1_ref, s1_ref, s1b_ref):
        nb = 4 if tm % 64 == 0 else 1
        blocks = [slice(k * (tm // nb), (k + 1) * (tm // nb)) for k in range(nb)]
        mixes = [_mm(co_ref[rows, :], w_ref[0:dc, :]) + _mm(go_ref[rows, :], w_ref[dc:2 * dc, :]) for rows in blocks]
        for rows, mix in zip(blocks, mixes):
            p1 = ALPHA * s0_ref[rows, :] + mix
            p1_ref[rows, :] = p1
            xhat, _ = _ln(p1)
            s1 = xhat * g_ref[...] + b_ref[...]
            s1_ref[rows, :] = s1
            s1b_ref[rows, :] = s1.astype(BF16)

    row = lambda w: pl.BlockSpec((tm, w), lambda i: (i, 0))
    vec = pl.BlockSpec((1, d), lambda i: (0, 0))
    return pl.pallas_call(
        body, name="outproj_fwd", grid=(r // tm,),
        in_specs=[row(d), row(dc), row(dc), pl.BlockSpec((2 * dc, d), lambda i: (0, 0)), vec, vec],
        out_specs=[row(d), row(d), row(d)],
        out_shape=[_sds((r, d), F32), _sds((r, d), F32), _sds((r, d), BF16)],
        compiler_params=_params(("parallel",)),
    )(s0, co, go, w_out, g1, b1)


def _mlp_fwd(s1, s1b, w1g, w2, g2, b2, tgt, tp, tm, ns):
    r, d = s1.shape
    nh, _, th = w1g.shape
    nj = nh // ns

    def body(s1_ref, sb_ref, w1_ref, w2_ref, g_ref, b_ref, t_ref, hm_ref, dp2_ref, dpb_ref, loss_ref, dg_ref, db_ref, acc_ref):
        i = pl.program_id(0)
        j = pl.program_id(1)

        @pl.when(jnp.logical_and(i == 0, j == 0))
        def _():
            loss_ref[...] = jnp.zeros_like(loss_ref)
            dg_ref[...] = jnp.zeros_like(dg_ref)
            db_ref[...] = jnp.zeros_like(db_ref)

        @pl.when(j == 0)
        def _():
            acc_ref[...] = jnp.zeros_like(acc_ref)

        def mlp_rows(rows):
            hs = [_mm(sb_ref[rows, :], w1_ref[s]) for s in range(ns)]
            acc = acc_ref[rows, :]
            for s in range(ns):
                hm_ref[rows, s * th:(s + 1) * th] = hs[s].astype(BF16)
                act = jnp.square(jnp.maximum(hs[s], 0.0))
                acc = acc + _mm(act.astype(BF16), w2_ref[s * th:(s + 1) * th, :])
            return acc

        @pl.when(j < nj - 1)
        def _():
            acc_ref[...] = mlp_rows(slice(None))

        @pl.when(j == nj - 1)
        def _():
            halves = [slice(0, tm // 2), slice(tm // 2, tm)]
            accs = [mlp_rows(rows) for rows in halves]
            isx = _row_in_seq(i, tm, tp) >= X_OFF
            tg = t_ref[...]
            tg = jnp.where(i == 0, pltpu.roll(tg, X_OFF, 0), tg)
            for rows, acc in zip(halves, accs):
                p2 = ALPHA * s1_ref[rows, :] + acc
                xhat, rstd = _ln(p2)
                s2 = xhat * g_ref[...] + b_ref[...]
                err = jnp.where(isx[rows], s2 - tg[rows], 0.0)
                loss_ref[...] += 0.5 * jnp.sum(jnp.mean(err * err, axis=-1, keepdims=True))
                dy = err * (1.0 / d)
                dg_ref[...] += _rowsum(dy * xhat)
                db_ref[...] += _rowsum(dy)
                dp2 = _ln_bwd(dy * g_ref[...], xhat, rstd)
                dp2_ref[rows, :] = dp2
                dpb_ref[rows, :] = dp2.astype(BF16)

    row = pl.BlockSpec((tm, d), lambda i, j: (i, 0))
    vec = pl.BlockSpec((1, d), lambda i, j: (0, 0))
    tgt_row = pl.BlockSpec((pl.Element(tm), pl.Element(d)),
                           lambda i, j: (pl.multiple_of(jnp.maximum(i * tm - X_OFF * ((i * tm) // tp + 1), 0), CHUNK), 0))
    return pl.pallas_call(
        body, name="mlp_fwd", grid=(r // tm, nj),
        in_specs=[row, row, pl.BlockSpec((ns, d, th), lambda i, j: (j, 0, 0)), pl.BlockSpec((ns * th, d), lambda i, j: (j, 0)),
                  vec, vec, tgt_row],
        out_specs=[pl.BlockSpec((tm, ns * th), lambda i, j: (i, j)), row, row,
                   pl.BlockSpec((8, LANES), lambda i, j: (0, 0)), vec, vec],
        out_shape=[_sds((r, nh * th), BF16), _sds((r, d), F32), _sds((r, d), BF16), _sds((8, LANES), F32),
                   _sds((1, d), F32), _sds((1, d), F32)],
        scratch_shapes=[pltpu.VMEM((tm, d), F32)],
        compiler_params=_params(("arbitrary", "arbitrary")),
    )(s1, s1b, w1g, w2, g2, b2, tgt)


def _mlp_bwd_act(dp2, dpb, hm, w1g, w2, p1, g1, tm, ns):
    r, d = dp2.shape
    nh, _, th = w1g.shape
    nj = nh // ns

    def body(dp2_ref, dpb_ref, hm_ref, w1_ref, w2_ref, p1_ref, g_ref, dh_ref, dp1_ref, dg_ref, db_ref, acc_ref):
        i = pl.program_id(0)
        j = pl.program_id(1)

        @pl.when(jnp.logical_and(i == 0, j == 0))
        def _():
            dg_ref[...] = jnp.zeros_like(dg_ref)
            db_ref[...] = jnp.zeros_like(db_ref)

        @pl.when(j == 0)
        def _():
            acc_ref[...] = jnp.zeros_like(acc_ref)

        def mlp_rows(rows):
            dacts = [_mm_nt(dpb_ref[rows, :], w2_ref[s * th:(s + 1) * th, :]) for s in range(ns)]
            acc = acc_ref[rows, :]
            for s in range(ns):
                cols = slice(s * th, (s + 1) * th)
                dh = (dacts[s] * (2.0 * jnp.maximum(hm_ref[rows, cols].astype(F32), 0.0))).astype(BF16)
                dh_ref[rows, cols] = dh
                acc = acc + _mm_nt(dh, w1_ref[s])
            return acc

        @pl.when(j < nj - 1)
        def _():
            acc_ref[...] = mlp_rows(slice(None))

        @pl.when(j == nj - 1)
        def _():
            halves = [slice(0, tm // 2), slice(tm // 2, tm)]
            accs = [mlp_rows(rows) for rows in halves]
            for rows, acc in zip(halves, accs):
                ds1 = ALPHA * dp2_ref[rows, :] + acc
                xhat, rstd = _ln(p1_ref[rows, :])
                dg_ref[...] += _rowsum(ds1 * xhat)
                db_ref[...] += _rowsum(ds1)
                dp1_ref[rows, :] = _ln_bwd(ds1 * g_ref[...], xhat, rstd)

    row = pl.BlockSpec((tm, d), lambda i, j: (i, 0))
    vec = pl.BlockSpec((1, d), lambda i, j: (0, 0))
    blk = pl.BlockSpec((tm, ns * th), lambda i, j: (i, j))
    return pl.pallas_call(
        body, name="mlp_bwd_act", grid=(r // tm, nj),
        in_specs=[row, row, blk, pl.BlockSpec((ns, d, th), lambda i, j: (j, 0, 0)),
                  pl.BlockSpec((ns * th, d), lambda i, j: (j, 0)), row, vec],
        out_specs=[blk, row, vec, vec],
        out_shape=[_sds((r, nh * th), BF16), _sds((r, d), F32), _sds((1, d), F32), _sds((1, d), F32)],
        scratch_shapes=[pltpu.VMEM((tm, d), F32)],
        compiler_params=_params(("arbitrary", "arbitrary")),
    )(dp2, dpb, hm, w1g, w2, p1, g1)


def _mlp_bwd_w(s1b, hm, dh, dpb, nh, tm, ns):
    r, d = s1b.shape
    th = hm.shape[1] // nh

    def body(s1_ref, hm_ref, dh_ref, dp2_ref, dw1_ref, dw2_ref, a1_ref, a2_ref):
        i = pl.program_id(1)

        @pl.when(i == 0)
        def _():
            a1_ref[...] = jnp.zeros_like(a1_ref)
            a2_ref[...] = jnp.zeros_like(a2_ref)

        for s in range(ns):
            a1_ref[s] += _mm_tn(s1_ref[...], dh_ref[:, s * th:(s + 1) * th])
        for s in range(ns):
            act = jnp.square(jnp.maximum(hm_ref[:, s * th:(s + 1) * th].astype(F32), 0.0)).astype(BF16)
            a2_ref[s] += _mm_tn(act, dp2_ref[...])

        @pl.when(i == pl.num_programs(1) - 1)
        def _():
            dw1_ref[...] = a1_ref[...].astype(BF16)
            dw2_ref[...] = a2_ref[...].astype(BF16)

    row = pl.BlockSpec((tm, d), lambda j, i: (i, 0))
    blk = pl.BlockSpec((tm, ns * th), lambda j, i: (i, j))
    return pl.pallas_call(
        body, name="mlp_bwd_w", grid=(nh // ns, r // tm),
        in_specs=[row, blk, blk, row],
        out_specs=[pl.BlockSpec((ns, d, th), lambda j, i: (j, 0, 0)), pl.BlockSpec((ns, th, d), lambda j, i: (j, 0, 0))],
        out_shape=[_sds((nh, d, th), BF16), _sds((nh, th, d), BF16)],
        scratch_shapes=[pltpu.VMEM((ns, d, th), F32), pltpu.VMEM((ns, th, d), F32)],
        compiler_params=_params(("parallel", "arbitrary")),
    )(s1b, hm, dh, dpb)


def _outproj_bwd(dp1, co, go, w_out, dep, tm):
    r, d = dp1.shape
    dc = co.shape[1]

    def body(dp_ref, co_ref, go_ref, w_ref, dep_ref, dmi_ref, dw_ref, acc_ref):
        i = pl.program_id(0)

        @pl.when(i == 0)
        def _():
            acc_ref[...] = jnp.zeros_like(acc_ref)

        dpb = dp_ref[...].astype(BF16)
        dmi_ref[...] = _mm_nt(dpb, w_ref[...])
        acc_ref[0:dc, :] += _mm_tn(co_ref[...], dpb)
        acc_ref[dc:2 * dc, :] += _mm_tn(go_ref[...], dpb)

        @pl.when(i == pl.num_programs(0) - 1)
        def _():
            dw_ref[...] = acc_ref[...].astype(BF16)

    row = lambda w: pl.BlockSpec((tm, w), lambda i: (i, 0))
    full = pl.BlockSpec((2 * dc, d), lambda i: (0, 0))
    return pl.pallas_call(
        body, name="outproj_bwd", grid=(r // tm,),
        in_specs=[row(d), row(dc), row(dc), full, pl.BlockSpec(memory_space=pl.ANY)],
        out_specs=[row(2 * dc), full],
        out_shape=[_sds((r, 2 * dc), F32), _sds((2 * dc, d), BF16)],
        scratch_shapes=[pltpu.VMEM((2 * dc, d), F32)],
        compiler_params=_params(("arbitrary",)),
    )(dp1, co, go, w_out, dep)


def _gla_bwd(u, dmi, sta, gup, gb, gn, dep, bsz, nc, kc):
    r = u.shape[0]
    hv = GLA_HEADS * GLA_DV
    hk = GLA_HEADS * GLA_DK
    ns = nc // kc

    def body(qk_ref, v_ref, r_ref, gd_ref, dgo_ref, sta_ref, gup_ref, gb_ref, gn_ref, dep_ref,
             dqk_ref, dv_ref, dr_ref, dgd_ref, dgn_ref, dgb_ref, dgup_ref, dst_ref):
        bi = pl.program_id(0)
        t = pl.program_id(1)

        @pl.when(jnp.logical_and(bi == 0, t == 0))
        def _():
            dgn_ref[...] = jnp.zeros_like(dgn_ref)
            dgb_ref[...] = jnp.zeros_like(dgb_ref)
            dgup_ref[...] = jnp.zeros_like(dgup_ref)

        @pl.when(t == 0)
        def _():
            dst_ref[...] = jnp.zeros_like(dst_ref)

        ps, ri, ci = _gla_prep(qk_ref, gd_ref, gup_ref[...], gb_ref[...], (ns - 1 - t) * kc, kc)
        tril = ri >= ci
        items = [(j, h) for j in reversed(range(kc)) for h in range(GLA_HEADS)]
        ops = [_gla_heads(p, v_ref) for p in ps]
        op = lambda jh: ops[jh[0]][jh[1]]
        st = {jh: sta_ref[jh[0], jh[1]] for jh in items}
        stb = {jh: st[jh].astype(BF16) for jh in items}
        a = {jh: jnp.where(tril, _mm_nt(op(jh)["qe"], op(jh)["ke"]), 0.0).astype(BF16) for jh in items}
        o1 = {jh: _mm(a[jh], op(jh)["v"]) for jh in items}
        o2 = {jh: _mm_nt(op(jh)["qe"], stb[jh]) for jh in items}
        dob = {}
        dgn = jnp.zeros((1, GLA_DV), F32)
        for jh in items:
            rows, vs = ps[jh[0]]["rows"], op(jh)["vs"]
            o = o1[jh] + o2[jh]
            rr = r_ref[rows, vs]
            sr = _sigmoid(rr)
            rs = lax.rsqrt(jnp.mean(o * o, axis=-1, keepdims=True) + LN_EPS)
            y = o * rs
            dgo = dgo_ref[rows, vs]
            don = dgo * (rr * sr)
            dr_ref[rows, vs] = (dgo * (y * gn_ref[...]) * (sr * (1.0 + rr * (1.0 - sr)))).astype(BF16)
            dgn = dgn + _rowsum(don * y)
            dxn = don * gn_ref[...]
            dob[jh] = (rs * (dxn - y * jnp.mean(dxn * y, axis=-1, keepdims=True))).astype(BF16)
        da = {jh: jnp.where(tril, _mm_nt(dob[jh], op(jh)["v"]), 0.0).astype(BF16) for jh in items}
        dv1 = {jh: _mm_tn(a[jh], dob[jh]) for jh in items}
        dqe1 = {jh: _mm(da[jh], op(jh)["ke"]) for jh in items}
        dqe2 = {jh: _mm(dob[jh], stb[jh]) for jh in items}
        dke1 = {jh: _mm_tn(da[jh], op(jh)["qe"]) for jh in items}
        inc = {jh: _mm_tn(dob[jh], op(jh)["qe"]) for jh in items}
        dsts = [dst_ref[h] for h in range(GLA_HEADS)]
        dkd1, dgam1 = {}, {}
        for jh in items:
            j, h = jh
            dst = dsts[h]
            dstb = dst.astype(BF16)
            dv_ref[ps[j]["rows"], op(jh)["vs"]] = (dv1[jh] + _mm_nt(op(jh)["kd"], dstb)).astype(BF16)
            dkd1[jh] = _mm(op(jh)["v"], dstb)
            dgam1[jh] = _rowsum(dst * st[jh])
            dsts[h] = dst * ps[j]["gam"][:, op(jh)["ls"]] + inc[jh]
        for h in range(GLA_HEADS):
            dst_ref[h] = dsts[h]
        upper = (ri <= ci).astype(BF16)
        dbs, dbls = [], []
        for j in range(kc):
            p = ps[j]
            tiles = [[op((j, 2 * hp + h2)) for h2 in range(2)] for hp in range(GLA_HEADS // 2)]
            head = lambda d, hp, h2: d[j, 2 * hp + h2]
            lanes = lambda f: jnp.concatenate([f(hp) for hp in range(GLA_HEADS // 2)], axis=1)
            dqe = lanes(lambda hp: sum(jnp.where(tiles[hp][h2]["m"], head(dqe1, hp, h2) + head(dqe2, hp, h2), 0.0)
                                       for h2 in range(2)))
            dke = lanes(lambda hp: head(dke1, hp, 0) + head(dke1, hp, 1))
            dkd = lanes(lambda hp: sum(jnp.where(tiles[hp][h2]["m"], head(dkd1, hp, h2), 0.0) for h2 in range(2)))
            dgam = lanes(lambda hp: head(dgam1, hp, 0) + head(dgam1, hp, 1))
            dqk_ref[p["rows"], :hk] = (dqe * p["eb"] * QK_SCALE).astype(BF16)
            dqk_ref[p["rows"], hk:] = (dke * p["enb"] + dkd * p["ebl"]).astype(BF16)
            dkdkd = dkd * p["kd"]
            dbs.append(dqe * p["qe"] - dke * p["ke"] - dkdkd)
            dbls.append(_rowsum(dkdkd) + dgam * p["gam"])
        dlgs = _tri_mm_all(upper, dbs)
        dzb = []
        dgb = jnp.zeros((1, hk), F32)
        for j in range(kc):
            p = ps[j]
            dz = jnp.where(p["real"], (dlgs[j] + dbls[j]) * (1.0 / GLA_TAU) * _sigmoid(-p["z"]), 0.0)
            dgb = dgb + _rowsum(dz)
            dzb.append(dz.astype(BF16))
        dgup = sum(_mm_tn(ps[j]["gd"].astype(BF16), dzb[j]) for j in range(kc))
        for j in range(kc):
            dgd_ref[ps[j]["rows"], :] = _mm_nt(dzb[j], gup_ref[...]).astype(BF16)
        dgb_ref[...] += dgb
        dgup_ref[...] += dgup
        dgn_ref[...] += dgn

    rowblk = lambda col: (lambda b, t: (b * ns + ns - 1 - t, col))
    const = lambda b, t: (0, 0)
    return pl.pallas_call(
        body, name="gla_bwd", grid=(bsz, ns),
        in_specs=[pl.BlockSpec((kc * CHUNK, 2 * hk), rowblk(2)), pl.BlockSpec((kc * CHUNK, hv), rowblk(3)),
                  pl.BlockSpec((kc * CHUNK, hv), rowblk(4)), pl.BlockSpec((kc * CHUNK, LANES), rowblk(20)),
                  pl.BlockSpec((kc * CHUNK, hv), rowblk(1)),
                  pl.BlockSpec((kc, GLA_HEADS, LANES, LANES), lambda b, t: (b * ns + ns - 1 - t, 0, 0, 0)),
                  pl.BlockSpec((LANES, 256), const), pl.BlockSpec((1, 256), const), pl.BlockSpec((1, GLA_DV), const),
                  pl.BlockSpec(memory_space=pl.ANY)],
        out_specs=[pl.BlockSpec((kc * CHUNK, 2 * hk), rowblk(0)), pl.BlockSpec((kc * CHUNK, hv), rowblk(0)),
                   pl.BlockSpec((kc * CHUNK, hv), rowblk(0)), pl.BlockSpec((kc * CHUNK, LANES), rowblk(0)),
                   pl.BlockSpec((1, GLA_DV), const), pl.BlockSpec((1, 256), const), pl.BlockSpec((LANES, 256), const)],
        out_shape=[_sds((r, 2 * hk), BF16), _sds((r, hv), BF16), _sds((r, hv), BF16), _sds((r, LANES), BF16),
                   _sds((1, GLA_DV), F32), _sds((1, 256), F32), _sds((LANES, 256), F32)],
        scratch_shapes=[pltpu.VMEM((GLA_HEADS, LANES, LANES), F32)],
        compiler_params=_params(("arbitrary", "arbitrary")),
    )(u, u, u, u, dmi, sta, gup, gb, gn, dep)


def _conv_bwd(u, c, dmi, w32, cg, cbe, tp, tc, dc):
    r = u.shape[0]
    hb = tc // CONV_HALO
    nhalo = r // CONV_HALO

    def dconv(cv, dco, cg_ref, cbe_ref):
        xhat, rstd = _ln(cv)
        cn = xhat * cg_ref[...] + cbe_ref[...]
        sg = _sigmoid(cn)
        dcn = dco * (sg * (1.0 + cn * (1.0 - sg)))
        return _ln_bwd(dcn * cg_ref[...], xhat, rstd), dcn, xhat

    def body(a_ref, g_ref, ah_ref, gh_ref, c_ref, dco_ref, ch_ref, dcoh_ref, w_ref, cg_ref, cbe_ref,
             du_ref, dw_ref, dcb_ref, dcg_ref, dcbe_ref, hs_ref, dcs_ref, dw8_ref):
        t = pl.program_id(0)

        @pl.when(t == 0)
        def _():
            dw8_ref[...] = jnp.zeros_like(dw8_ref)
            dcb_ref[...] = jnp.zeros_like(dcb_ref)
            dcg_ref[...] = jnp.zeros_like(dcg_ref)
            dcbe_ref[...] = jnp.zeros_like(dcbe_ref)

        first = lax.rem(t * tc, tp) == 0
        last = lax.rem((t + 1) * tc, tp) == 0
        hh = ah_ref[...] * _sigmoid(gh_ref[...])
        hs_ref[0:CONV_HALO, :] = jnp.where(first, 0.0, hh)
        hs_ref[CONV_HALO:CONV_HALO + tc, :] = a_ref[...] * _sigmoid(g_ref[...])
        dch, _, _ = dconv(ch_ref[...], dcoh_ref[...], cg_ref, cbe_ref)
        dcs_ref[tc:tc + CONV_HALO, :] = jnp.where(last, 0.0, dch)

        def sub1(k, carry):
            r0 = pl.multiple_of(k * CONV_SUB, CONV_SUB)
            dcv, dcn, xhat = dconv(c_ref[pl.ds(r0, CONV_SUB), :], dco_ref[pl.ds(r0, CONV_SUB), :], cg_ref, cbe_ref)
            dcs_ref[pl.ds(r0, CONV_SUB), :] = dcv
            dcb_ref[...] += _rowsum(dcv)
            dcg_ref[...] += _rowsum(dcn * xhat)
            dcbe_ref[...] += _rowsum(dcn)
            return carry

        lax.fori_loop(0, tc // CONV_SUB, sub1, 0)

        def sub2(k, carry):
            r0 = pl.multiple_of(k * CONV_SUB, CONV_SUB)
            dwin = dcs_ref[pl.ds(r0, CONV_WIN), :]
            dh = _conv_taps(dwin, lambda o: w_ref[CONV_WIDTH - 1 - o:CONV_WIDTH - o, :], 0)
            av = a_ref[pl.ds(r0, CONV_SUB), :]
            sg = _sigmoid(g_ref[pl.ds(r0, CONV_SUB), :])
            du_ref[pl.ds(r0, CONV_SUB), 0:dc] = (dh * sg).astype(BF16)
            du_ref[pl.ds(r0, CONV_SUB), dc:2 * dc] = (dh * av * sg * (1.0 - sg)).astype(BF16)
            hwin = hs_ref[pl.ds(r0, CONV_WIN), :]
            dcv = dwin[0:CONV_SUB, :]
            for rho in range(8):
                offs = [o for o in range(2, 2 + CONV_WIDTH) if o % 8 == rho]
                rolled = hwin if rho == 0 else pltpu.roll(hwin, CONV_WIN - rho, 0)
                for o in offs:
                    m8 = o - rho
                    prod = dcv * rolled[m8:m8 + CONV_SUB, :]
                    dw8_ref[8 * (o - 2):8 * (o - 1), :] += jnp.sum(prod.reshape(CONV_SUB // 8, 8, dc), axis=0)
            return carry

        lax.fori_loop(0, tc // CONV_SUB, sub2, 0)

        @pl.when(t == pl.num_programs(0) - 1)
        def _():
            dw_ref[...] = jnp.zeros_like(dw_ref)
            for j in range(CONV_WIDTH):
                dw_ref[j:j + 1, :] = _rowsum(dw8_ref[8 * j:8 * (j + 1), :])

    vec = pl.BlockSpec((1, dc), lambda t: (0, 0))
    prev = lambda col: (lambda t: (jnp.maximum(t * hb - 1, 0), col))
    nxt = lambda col: (lambda t: (jnp.minimum((t + 1) * hb, nhalo - 1), col))
    return pl.pallas_call(
        body, name="conv_bwd", grid=(r // tc,),
        in_specs=[pl.BlockSpec((tc, dc), lambda t: (t, 0)), pl.BlockSpec((tc, dc), lambda t: (t, 1)),
                  pl.BlockSpec((CONV_HALO, dc), prev(0)), pl.BlockSpec((CONV_HALO, dc), prev(1)),
                  pl.BlockSpec((tc, dc), lambda t: (t, 0)), pl.BlockSpec((tc, dc), lambda t: (t, 0)),
                  pl.BlockSpec((CONV_HALO, dc), nxt(0)), pl.BlockSpec((CONV_HALO, dc), nxt(0)),
                  pl.BlockSpec((32, dc), lambda t: (0, 0)), vec, vec],
        out_specs=[pl.BlockSpec((tc, 2 * dc), lambda t: (t, 0)), pl.BlockSpec((32, dc), lambda t: (0, 0)), vec, vec, vec],
        out_shape=[_sds((r, 2 * dc), BF16), _sds((32, dc), F32), _sds((1, dc), F32), _sds((1, dc), F32), _sds((1, dc), F32)],
        scratch_shapes=[pltpu.VMEM((CONV_HALO + tc, dc), F32), pltpu.VMEM((tc + CONV_HALO, dc), F32),
                        pltpu.VMEM((8 * 32, dc), F32)],
        compiler_params=_params(("arbitrary",)),
    )(u, u, u, u, c, dmi, c, dmi, w32, cg, cbe)


def _inproj_bwd(dp1, dus, xsrc, g_in, w_in, dep, tp, seq, tx):
    r, d = dp1.shape
    widths = [x.shape[1] for x in dus]
    offs = [sum(widths[:k]) for k in range(len(widths))]
    n = w_in.shape[0]
    nd = len(dus)
    head = tx == 0
    rows = X_OFF if head else tx

    def body(*refs):
        dp_ref = refs[0]
        du_refs = refs[1:1 + nd]
        x_ref, g_ref, w_ref, _, out_ref, dg_ref, db_ref = refs[1 + nd:]
        i = pl.program_id(0)

        @pl.when(i == 0)
        def _():
            dg_ref[...] = jnp.zeros_like(dg_ref)
            db_ref[...] = jnp.zeros_like(db_ref)
            if head:
                out_ref[...] = jnp.zeros_like(out_ref)

        ds0 = ALPHA * dp_ref[...]
        for k in range(nd):
            ds0 = ds0 + _mm(du_refs[k][...], w_ref[offs[k]:offs[k] + widths[k], :])
        if head:
            ds0 = jnp.where(lax.broadcasted_iota(jnp.int32, (X_OFF, 1), 0) >= PAD_FRONT, ds0, 0.0)
        xhat, rstd = _ln(x_ref[...])
        dg_ref[...] += _rowsum(ds0 * xhat)
        db_ref[...] += _rowsum(ds0)
        dx = _ln_bwd(ds0 * g_ref[...], xhat, rstd)
        if head:
            out_ref[...] += dx[PAD_FRONT:X_OFF, :]
        else:
            out_ref[...] = dx

    if head:
        nb = tp // X_OFF
        row = lambda w: pl.BlockSpec((X_OFF, w), lambda i: (i * nb, 0))
        xspec = pl.BlockSpec((X_OFF, d), lambda i: (0, 0))
        ospec, oshape, steps = pl.BlockSpec((N_META, d), lambda i: (0, 0)), _sds((N_META, d), F32), r // tp
    else:
        start = _x_tile_row(tp, seq, tx)
        row = lambda w: pl.BlockSpec((pl.Element(tx), pl.Element(w)), lambda i: (start(i), 0))
        xspec = pl.BlockSpec((tx, d), lambda i: (i, 0))
        ospec, oshape, steps = xspec, _sds(xsrc.shape, F32), xsrc.shape[0] // tx
    vec = pl.BlockSpec((1, d), lambda i: (0, 0))
    return pl.pallas_call(
        body, name="inproj_bwd_head" if head else "inproj_bwd_x", grid=(steps,),
        in_specs=[row(d)] + [row(w) for w in widths] + [xspec, vec, pl.BlockSpec((n, d), lambda i: (0, 0)),
                                                        pl.BlockSpec(memory_space=pl.ANY)],
        out_specs=[ospec, vec, vec],
        out_shape=[oshape, _sds((1, d), F32), _sds((1, d), F32)],
        compiler_params=_params(("arbitrary",)),
    )(dp1, *dus, xsrc, g_in, w_in, dep)


def _inproj_bwd_w(s0, dus, tm):
    r, d = s0.shape
    widths = [x.shape[1] for x in dus]
    offs = [sum(widths[:k]) for k in range(len(widths))]
    nd = len(dus)

    def body(*refs):
        s_ref = refs[0]
        du_refs = refs[1:1 + nd]
        dw_ref, acc_ref = refs[1 + nd:]
        i = pl.program_id(0)

        @pl.when(i == 0)
        def _():
            acc_ref[...] = jnp.zeros_like(acc_ref)

        for k in range(nd):
            acc_ref[offs[k]:offs[k] + widths[k], :] += _mm_tn(du_refs[k][...], s_ref[...])

        @pl.when(i == pl.num_programs(0) - 1)
        def _():
            dw_ref[...] = acc_ref[...].astype(BF16)

    row = lambda w: pl.BlockSpec((tm, w), lambda i: (i, 0))
    return pl.pallas_call(
        body, name="inproj_bwd_w", grid=(r // tm,),
        in_specs=[row(d)] + [row(w) for w in widths],
        out_specs=pl.BlockSpec((sum(widths), d), lambda i: (0, 0)),
        out_shape=_sds((sum(widths), d), BF16),
        scratch_shapes=[pltpu.VMEM((sum(widths), d), F32)],
        compiler_params=_params(("arbitrary",)),
    )(s0, *dus)


def _local_step(x, tgt, meta, ln_in_g, ln_in_b, w_in, conv_w, conv_b, conv_ln_g, conv_ln_b, gate_up, gate_bias,
                gla_norm_g, late_weights, ln1_g, ln1_b, ln2_g, ln2_b, push):
    bsz, seq, d = x.shape
    tp = X_OFF + seq
    assert tp % CHUNK == 0
    nc = tp // CHUNK
    r = bsz * tp
    dc = conv_b.shape[1]
    tm = _pick_tile(tp, (352, 128, 64))
    tc = _pick_tile(tp, (704, 128, 64))

    x2 = x.reshape(bsz * seq, d)
    head = jnp.pad(meta, ((PAD_FRONT, 0), (0, 0)))
    tx = _pick_tile(seq, (512, 64))
    tgt_p = tgt.reshape(bsz * seq, d)
    w32 = jnp.pad(conv_w, ((0, 32 - CONV_WIDTH), (0, 0)))
    gup = jnp.pad(gate_up, ((0, LANES - GLA_RANK), (0, 0))).astype(BF16)

    s0, s0b = _ln_in_x(x2, ln_in_g, ln_in_b, tp, seq, tx)
    s0, s0b = _ln_in_head(head, ln_in_g, ln_in_b, s0, s0b, tp)
    tmm = _pick_tile(tp, (704, 128, 64))
    u = _inproj_fwd(s0b, w_in, tmm)
    c, co = _conv_fwd(u, w32, conv_b, conv_ln_g, conv_ln_b, tp, tc, dc)
    kc = _pick_tile(nc, (11, 3, 2, 1))
    go, sta = _gla_fwd(u, gup, gate_bias, gla_norm_g, bsz, nc, kc)
    w_out, w1g, w2 = late_weights(go)
    nh = w1g.shape[0]
    tmm = _pick_tile(tp, (704, 128, 64))
    ns = 2
    p1, s1, s1b = _outproj_fwd(s0, co, go, w_out, ln1_g, ln1_b, tmm)
    hm, dp2, dpb, loss, dg2, db2 = _mlp_fwd(s1, s1b, w1g, w2, ln2_g, ln2_b, tgt_p, tp, tmm, ns)

    dh, dp1, dg1, db1 = _mlp_bwd_act(dp2, dpb, hm, w1g, w2, p1, ln1_g, tmm, ns)
    dw1, dw2 = _mlp_bwd_w(s1b, hm, dh, dpb, nh, tmm, ns)
    tok = push("ff", (dw1, dw2))
    dmi, dwo = _outproj_bwd(dp1, co, go, w_out, tok, tmm)
    tok = push("out", (dwo,))
    dqk, dv, dr, dgd, dgn, dgb, dgup = _gla_bwd(u, dmi, sta, gup, gate_bias, gla_norm_g, tok, bsz, nc, kc)
    dcv, dcw, dcb, dcg, dcbe = _conv_bwd(u, c, dmi, w32, conv_ln_g, conv_ln_b, tp, tc, dc)
    dus = [dcv, dqk, dv, dr, dgd]
    dwi = _inproj_bwd_w(s0b, dus, tmm)
    tok = push("in", (dwi,))
    gx, dgx, dbx = _inproj_bwd(dp1, dus, x2, ln_in_g, w_in, tok, tp, seq, tx)
    dmeta, dgh, dbh = _inproj_bwd(dp1, dus, head, ln_in_g, w_in, tok, tp, seq, 0)

    return dict(loss=loss[0, 0], grad_x=gx.reshape(bsz, seq, d), meta_tokens=dmeta, ln_in_g=dgx + dgh, ln_in_b=dbx + dbh,
                conv_w=dcw[:CONV_WIDTH], conv_b=dcb, conv_ln_g=dcg, conv_ln_b=dcbe,
                gate_up=dgup[:GLA_RANK], gate_bias=dgb, gla_norm_g=dgn, ln1_g=dg1, ln1_b=db1, ln2_g=dg2, ln2_b=db2)


def _exchange(arrays, scatter, name):
    na = len(arrays)
    npeer = N_DEV - 1

    def body(*refs):
        srcs = refs[:na]
        outs = refs[na:2 * na]
        send_sems, recv_sems, local_sems = refs[2 * na:]
        xi, yi, ci = (lax.axis_index(a) for a in MESH_AXES)
        me = 4 * xi + 2 * yi + ci
        copies = []
        for a in range(na):
            own = srcs[a].at[me] if scatter[a] else srcs[a]
            cp = pltpu.make_async_copy(own, outs[a].at[me], local_sems.at[a])
            cp.start()
            copies.append(cp)
        remote = []
        for k in range(1, N_DEV):
            px, py, pc = xi ^ (k >> 2), yi ^ ((k >> 1) & 1), ci ^ (k & 1)
            peer = 4 * px + 2 * py + pc
            for a in range(na):
                src = srcs[a].at[peer] if scatter[a] else srcs[a]
                cp = pltpu.make_async_remote_copy(
                    src_ref=src, dst_ref=outs[a].at[me],
                    send_sem=send_sems.at[a * npeer + k - 1], recv_sem=recv_sems.at[a * npeer + k - 1],
                    device_id=(px, py, pc), device_id_type=pl.DeviceIdType.MESH)
                cp.start()
                remote.append(cp)
        for cp in remote:
            cp.wait()
        for cp in copies:
            cp.wait()

    out_shape = [_sds(a.shape if scatter[i] else (N_DEV,) + a.shape, a.dtype) for i, a in enumerate(arrays)]
    anyspec = pl.BlockSpec(memory_space=pl.ANY)
    return pl.pallas_call(
        body, name=name,
        in_specs=[anyspec] * na, out_specs=[anyspec] * na, out_shape=out_shape,
        scratch_shapes=[pltpu.SemaphoreType.DMA((na * npeer,)), pltpu.SemaphoreType.DMA((na * npeer,)),
                        pltpu.SemaphoreType.DMA((na,))],
    )(*arrays)


def _peers(xi, yi, ci):
    for k in range(1, N_DEV):
        px, py, pc = xi ^ (k >> 2), yi ^ ((k >> 1) & 1), ci ^ (k & 1)
        yield (px, py, pc), 4 * px + 2 * py + pc


def _sc_exchange(arrays, scatter, name, collective_id, after=None):
    na = len(arrays)
    npeer = N_DEV - 1
    ndep = 0 if after is None else 1

    def body(*refs):
        srcs = refs[:na]
        outs = refs[na + ndep:2 * na + ndep]
        send_sems, recv_sems, own_sems = refs[2 * na + ndep:]
        xi, yi, ci = (lax.axis_index(a) for a in MESH_AXES)
        me = 4 * xi + 2 * yi + ci
        barrier = pltpu.get_barrier_semaphore()
        for pos, _ in _peers(xi, yi, ci):
            pl.semaphore_signal(barrier, inc=1, device_id=pos, device_id_type=pl.DeviceIdType.MESH)
        pl.semaphore_wait(barrier, npeer)
        own = [pltpu.make_async_copy(srcs[a].at[me] if scatter[a] else srcs[a], outs[a].at[me], own_sems.at[a])
               for a in range(na)]
        for cp in own:
            cp.start()
        remote = []
        for a in range(na):
            for k, (pos, peer) in enumerate(_peers(xi, yi, ci)):
                cp = pltpu.make_async_remote_copy(
                    src_ref=srcs[a].at[peer] if scatter[a] else srcs[a], dst_ref=outs[a].at[me],
                    send_sem=send_sems.at[a * npeer + k], recv_sem=recv_sems.at[a * npeer + k],
                    device_id=pos, device_id_type=pl.DeviceIdType.MESH)
                cp.start()
                remote.append(cp)
        for cp in own:
            cp.wait()
        for cp in remote:
            cp.wait()

    out_type = [_sds(a.shape if scatter[i] else (N_DEV,) + a.shape, a.dtype) for i, a in enumerate(arrays)]
    sent = sum(a.size * a.dtype.itemsize // (N_DEV if scatter[i] else 1) for i, a in enumerate(arrays))
    return pl.kernel(
        body, out_type=out_type, mesh=plsc.ScalarSubcoreMesh(axis_name="seq", num_cores=1), name=name,
        scratch_types=[pltpu.SemaphoreType.DMA((na * npeer,)), pltpu.SemaphoreType.DMA((na * npeer,)),
                       pltpu.SemaphoreType.DMA((na,))],
        compiler_params=pltpu.CompilerParams(collective_id=collective_id),
        cost_estimate=pl.CostEstimate(flops=0, transcendentals=0, bytes_accessed=2 * N_DEV * sent,
                                      remote_bytes_transferred=npeer * sent),
    )(*arrays, *([] if after is None else [after]))


def _sc_gather(arrays, name, collective_id, after=None):
    na = len(arrays)
    ndep = 0 if after is None else 1
    npair = N_DEV - 1

    def body(*refs):
        srcs = refs[:na]
        outs = refs[na + ndep:2 * na + ndep]
        send_sems, recv_sems, own_sems = refs[2 * na + ndep:]
        xi, yi, ci = (lax.axis_index(a) for a in MESH_AXES)
        me = 4 * xi + 2 * yi + ci
        sibling = (xi, yi, 1 - ci)
        chips = [(1 - xi, yi), (xi, 1 - yi), (1 - xi, 1 - yi)]
        barrier = pltpu.get_barrier_semaphore()
        for pos, _ in _peers(xi, yi, ci):
            pl.semaphore_signal(barrier, inc=1, device_id=pos, device_id_type=pl.DeviceIdType.MESH)
        pl.semaphore_wait(barrier, npair)

        def copy(a, k, src, slot, to):
            return pltpu.make_async_remote_copy(
                src_ref=src, dst_ref=outs[a].at[slot], send_sem=send_sems.at[a * npair + k],
                recv_sem=recv_sems.at[a * npair + k], device_id=to, device_id_type=pl.DeviceIdType.MESH)

        own = [pltpu.make_async_copy(srcs[a], outs[a].at[me], own_sems.at[a]) for a in range(na)]
        for cp in own:
            cp.start()
        sent = []
        for a in range(na):
            sent.append(copy(a, 0, srcs[a], me, sibling))
            sent += [copy(a, 1 + j, srcs[a], me, (*chip, ci)) for j, chip in enumerate(chips)]
        for cp in sent:
            cp.start()
        for j, (cx, cy) in enumerate(chips):
            slot = 4 * cx + 2 * cy + ci
            for a in range(na):
                copy(a, 1 + j, srcs[a], slot, sibling).wait_recv()
                cp = copy(a, 4 + j, outs[a].at[slot], slot, sibling)
                cp.start()
                sent.append(cp)
        for a in range(na):
            copy(a, 0, srcs[a], me, sibling).wait_recv()
            for j in range(len(chips)):
                copy(a, 4 + j, srcs[a], me, sibling).wait_recv()
        for cp in sent:
            cp.wait_send()
        for cp in own:
            cp.wait()

    out_type = [_sds((N_DEV,) + a.shape, a.dtype) for a in arrays]
    sent_bytes = sum(a.size * a.dtype.itemsize for a in arrays)
    return pl.kernel(
        body, out_type=out_type, mesh=plsc.ScalarSubcoreMesh(axis_name="seq", num_cores=1), name=name,
        scratch_types=[pltpu.SemaphoreType.DMA((na * npair,)), pltpu.SemaphoreType.DMA((na * npair,)),
                       pltpu.SemaphoreType.DMA((na,))],
        compiler_params=pltpu.CompilerParams(collective_id=collective_id),
        cost_estimate=pl.CostEstimate(flops=0, transcendentals=0, bytes_accessed=2 * N_DEV * sent_bytes,
                                      remote_bytes_transferred=npair * sent_bytes),
    )(*arrays, *([] if after is None else [after]))


def _push_start(arrays, scatter, name, dep=None):
    na = len(arrays)
    shapes = [a.shape if scatter[i] else (N_DEV,) + a.shape for i, a in enumerate(arrays)]
    hbm = pl.BlockSpec(memory_space=pltpu.HBM)
    sem = pl.BlockSpec(memory_space=pltpu.SEMAPHORE)
    ndep = 0 if dep is None else 1

    def body(*refs):
        srcs = refs[:na]
        lands = refs[na:2 * na]
        send_sems, recv_sems = refs[2 * na + ndep:2 * na + ndep + 2]
        own_sems = refs[4 * na + ndep + 2]
        xi, yi, ci = (lax.axis_index(a) for a in MESH_AXES)
        me = 4 * xi + 2 * yi + ci
        own = [pltpu.make_async_copy(srcs[a].at[me] if scatter[a] else srcs[a], lands[a].at[me], own_sems.at[a])
               for a in range(na)]
        for cp in own:
            cp.start()
        for cp in own:
            cp.wait()
        for a in range(na):
            for pos, peer in _peers(xi, yi, ci):
                pltpu.make_async_remote_copy(
                    src_ref=srcs[a].at[peer] if scatter[a] else srcs[a], dst_ref=lands[a].at[me],
                    send_sem=send_sems.at[a], recv_sem=recv_sems.at[a],
                    device_id=pos, device_id_type=pl.DeviceIdType.MESH).start()

    ins = [pltpu.with_memory_space_constraint(a, pltpu.HBM) for a in arrays]
    ins += [pltpu.with_memory_space_constraint(lax.empty(s, a.dtype), pltpu.HBM) for s, a in zip(shapes, arrays)]
    res = pl.pallas_call(
        body, name=name,
        in_specs=[hbm] * (2 * na) + [pl.BlockSpec(memory_space=pl.ANY)] * ndep,
        out_specs=[sem, sem] + [hbm] * (2 * na),
        out_shape=[pltpu.SemaphoreType.DMA((na,)), pltpu.SemaphoreType.DMA((na,))]
                  + [pltpu.HBM(a.shape, a.dtype) for a in arrays] + [pltpu.HBM(s, a.dtype) for s, a in zip(shapes, arrays)],
        input_output_aliases={i: 2 + i for i in range(2 * na)},
        scratch_shapes=[pltpu.SemaphoreType.DMA((na,))],
        compiler_params=pltpu.CompilerParams(has_side_effects=pltpu.SideEffectType.DATAFLOW_SIDE_EFFECTING),
    )(*ins, *([] if dep is None else [dep]))
    return (res[0], res[1], list(res[2:2 + na]), list(res[2 + na:2 + 2 * na])), res[2]


def _push_wait(handle, after, name):
    send_sems, recv_sems, srcs, lands = handle
    na = len(srcs)
    hbm = pl.BlockSpec(memory_space=pltpu.HBM)
    sem = pl.BlockSpec(memory_space=pltpu.SEMAPHORE)

    def body(*refs):
        land_refs = refs[na:2 * na]
        send_ref, recv_ref = refs[2 * na:2 * na + 2]
        me = tuple(lax.axis_index(a) for a in MESH_AXES)
        for a in range(na):
            seven = land_refs[a].at[pl.ds(0, N_DEV - 1)]
            cp = pltpu.make_async_remote_copy(src_ref=seven, dst_ref=seven, send_sem=send_ref.at[a], recv_sem=recv_ref.at[a],
                                              device_id=me, device_id_type=pl.DeviceIdType.MESH)
            cp.wait_send()
            cp.wait_recv()

    res = pl.pallas_call(
        body, name=name,
        in_specs=[hbm] * (2 * na) + [sem, sem, pl.BlockSpec(memory_space=pl.ANY)],
        out_specs=[hbm] * (2 * na),
        out_shape=[pltpu.HBM(a.shape, a.dtype) for a in srcs] + [pltpu.HBM(a.shape, a.dtype) for a in lands],
        input_output_aliases={i: i for i in range(2 * na)},
        compiler_params=pltpu.CompilerParams(has_side_effects=pltpu.SideEffectType.DATAFLOW_SIDE_EFFECTING),
    )(*srcs, *lands, send_sems, recv_sems, after)
    return list(res[na:])


def _adamw(w, g, m, v):
    m = ADAM_B1 * m + (1.0 - ADAM_B1) * g
    v = ADAM_B2 * v + (1.0 - ADAM_B2) * jnp.square(g)
    m_hat = m / (1.0 - ADAM_B1 ** ADAM_STEP)
    v_hat = v / (1.0 - ADAM_B2 ** ADAM_STEP)
    delta = -ADAM_LR * (m_hat / (jnp.sqrt(v_hat) + ADAM_EPS) + ADAM_WD * w)
    return delta, m, v


def _sum_devices(ref):
    g = ref[0].astype(F32)
    for k in range(1, N_DEV):
        g = g + ref[k].astype(F32)
    return g


def _update_big(parts, w, m, v, name):
    rows, cols = w.shape

    def body(p_ref, w_ref, m_ref, v_ref, g_ref, d_ref, nm_ref, nv_ref):
        g = _sum_devices(p_ref)
        g_ref[...] = g
        d_ref[...], nm_ref[...], nv_ref[...] = _adamw(w_ref[...], g, m_ref[...], v_ref[...])

    if rows % 16 == 0:
        tr = _pick_tile(rows, (128, 64, 16))
        steps, blk = rows // tr, pl.BlockSpec((tr, cols), lambda i: (i, 0))
        pblk = pl.BlockSpec((N_DEV, tr, cols), lambda i: (0, i, 0))
    else:
        tcol = 2 * LANES
        steps, blk = cols // tcol, pl.BlockSpec((rows, tcol), lambda i: (0, i))
        pblk = pl.BlockSpec((N_DEV, rows, tcol), lambda i: (0, 0, i))
    return pl.pallas_call(
        body, name=name, grid=(steps,),
        in_specs=[pblk, blk, blk, blk],
        out_specs=[blk] * 4, out_shape=[_sds((rows, cols), F32)] * 4,
        compiler_params=_params(("parallel",)),
    )(parts, w, m, v)


_VEC_ORDER = ("ln_in_g", "ln_in_b", "conv_b", "conv_ln_g", "conv_ln_b", "gate_bias", "gla_norm_g",
              "ln1_g", "ln1_b", "ln2_g", "ln2_b")
_SHARDED_SMALL = (("meta_tokens", 0, N_META, LANES), ("conv_w", N_META, CONV_WIDTH, None), ("gate_up", N_META + 32, GLA_RANK, None))


def _update_small(parts_sh, parts_vec, wmv):
    names = [s[0] for s in _SHARDED_SMALL] + list(_VEC_ORDER)
    flat = [a for nme in names for a in wmv[nme]]
    nv = len(_VEC_ORDER)

    def body(*refs):
        sh_ref, vec_ref = refs[0], refs[1]
        ins = refs[2:2 + len(flat)]
        outs = refs[2 + len(flat):2 + len(flat) + 4 * len(names)]
        loss_ref = refs[2 + len(flat) + 4 * len(names)]
        gsh_ref, gvec_ref = refs[-2:]
        gsh_ref[...] = _sum_devices(sh_ref)
        gvec_ref[...] = _sum_devices(vec_ref)
        loss_ref[...] = gvec_ref[nv:nv + 1, :]
        for idx, nme in enumerate(names):
            w_ref, m_ref, v_ref = ins[3 * idx:3 * idx + 3]
            rows, cols = w_ref.shape
            if idx < len(_SHARDED_SMALL):
                r0 = _SHARDED_SMALL[idx][1]
                g = gsh_ref[r0:r0 + rows, 0:cols]
            else:
                j = idx - len(_SHARDED_SMALL)
                g = gvec_ref[j:j + 1, 0:cols]
            o = outs[4 * idx:4 * idx + 4]
            o[0][...] = g
            o[1][...], o[2][...], o[3][...] = _adamw(w_ref[...], g, m_ref[...], v_ref[...])

    out_shape = [_sds(wmv[nme][0].shape, F32) for nme in names for _ in range(4)] + [_sds((1, parts_vec.shape[2]), F32)]
    vmem = pl.BlockSpec(memory_space=pltpu.VMEM)
    res = pl.pallas_call(
        body, name="update_small", out_shape=out_shape,
        in_specs=[vmem] * (2 + len(flat)), out_specs=[vmem] * len(out_shape),
        scratch_shapes=[pltpu.VMEM(parts_sh.shape[1:], F32), pltpu.VMEM(parts_vec.shape[1:], F32)],
    )(parts_sh, parts_vec, *flat)
    return {nme: res[4 * i:4 * i + 4] for i, nme in enumerate(names)}, res[-1][0, 0]


_WEIGHTS = ("meta_tokens", "ln_in_g", "ln_in_b", "w_in", "conv_w", "conv_b", "conv_ln_g", "conv_ln_b", "gate_up",
            "gate_bias", "gla_norm_g", "w_out", "ln1_g", "ln1_b", "w_ff1", "w_ff2", "ln2_g", "ln2_b")


def kernel(x, meta_tokens, ln_in_g, ln_in_b, w_in, conv_w, conv_b, conv_ln_g, conv_ln_b, gate_up, gate_bias, gla_norm_g, w_out, ln1_g, ln1_b, w_ff1, w_ff2, ln2_g, ln2_b, loss_target, m_meta_tokens, m_ln_in_g, m_ln_in_b, m_w_in, m_conv_w, m_conv_b, m_conv_ln_g, m_conv_ln_b, m_gate_up, m_gate_bias, m_gla_norm_g, m_w_out, m_ln1_g, m_ln1_b, m_w_ff1, m_w_ff2, m_ln2_g, m_ln2_b, v_meta_tokens, v_ln_in_g, v_ln_in_b, v_w_in, v_conv_w, v_conv_b, v_conv_ln_g, v_conv_ln_b, v_gate_up, v_gate_bias, v_gla_norm_g, v_w_out, v_ln1_g, v_ln1_b, v_w_ff1, v_w_ff2, v_ln2_g, v_ln2_b):
    w = dict(meta_tokens=meta_tokens, ln_in_g=ln_in_g, ln_in_b=ln_in_b, w_in=w_in, conv_w=conv_w, conv_b=conv_b,
             conv_ln_g=conv_ln_g, conv_ln_b=conv_ln_b, gate_up=gate_up, gate_bias=gate_bias, gla_norm_g=gla_norm_g,
             w_out=w_out, ln1_g=ln1_g, ln1_b=ln1_b, w_ff1=w_ff1, w_ff2=w_ff2, ln2_g=ln2_g, ln2_b=ln2_b)
    mom = dict(meta_tokens=m_meta_tokens, ln_in_g=m_ln_in_g, ln_in_b=m_ln_in_b, w_in=m_w_in, conv_w=m_conv_w,
               conv_b=m_conv_b, conv_ln_g=m_conv_ln_g, conv_ln_b=m_conv_ln_b, gate_up=m_gate_up, gate_bias=m_gate_bias,
               gla_norm_g=m_gla_norm_g, w_out=m_w_out, ln1_g=m_ln1_g, ln1_b=m_ln1_b, w_ff1=m_w_ff1, w_ff2=m_w_ff2,
               ln2_g=m_ln2_g, ln2_b=m_ln2_b)
    var = dict(meta_tokens=v_meta_tokens, ln_in_g=v_ln_in_g, ln_in_b=v_ln_in_b, w_in=v_w_in, conv_w=v_conv_w,
               conv_b=v_conv_b, conv_ln_g=v_conv_ln_g, conv_ln_b=v_conv_ln_b, gate_up=v_gate_up, gate_bias=v_gate_bias,
               gla_norm_g=v_gla_norm_g, w_out=v_w_out, ln1_g=v_ln1_g, ln1_b=v_ln1_b, w_ff1=v_w_ff1, w_ff2=v_w_ff2,
               ln2_g=v_ln2_g, ln2_b=v_ln2_b)
    shapes = {k: a.shape for k, a in w.items()}

    def two_d(a):
        return a.reshape(1, -1) if a.ndim == 1 else a.reshape(a.shape[-2:])

    w2d = {k: two_d(a) for k, a in w.items()}
    m2d = {k: two_d(a) for k, a in mom.items()}
    v2d = {k: two_d(a) for k, a in var.items()}
    d = x.shape[-1]
    d_in = w2d["w_in"].shape[1] * N_DEV
    d_in_p = -(-d_in // LANES) * LANES

    for dct in (w2d, m2d, v2d):
        dct["w_in"] = dct["w_in"].T
    g_in, g_meta, g_conv, g_gup = _sc_gather(
        [w2d["w_in"].astype(BF16), w2d["meta_tokens"], w2d["conv_w"], w2d["gate_up"]], "gather_first", 0)
    g_out, g_ff1, g_ff2 = _sc_gather(
        [w2d["w_out"].astype(BF16), w2d["w_ff1"].astype(BF16), w2d["w_ff2"].astype(BF16)], "gather_late", 1)
    w_in_full = jnp.pad(g_in.reshape(d_in, d), ((0, d_in_p - d_in), (0, 0)))
    meta_full = g_meta.transpose(1, 0, 2).reshape(N_META, d)
    conv_w_full = g_conv.transpose(1, 0, 2).reshape(CONV_WIDTH, -1)
    gate_up_full = g_gup.transpose(1, 0, 2).reshape(GLA_RANK, -1)

    def late_weights(after):
        return g_out.reshape(-1, d), g_ff1, g_ff2.reshape(-1, d)

    pushed = {}

    def push(tag, grads):
        if tag == "ff":
            pushed["ff1"], pushed["ff2"] = _sc_exchange(list(grads), [True, True], "scatter_ff", 2)
        elif tag == "out":
            pushed["p_out"] = grads[0].reshape(N_DEV, -1, d)
        else:
            p_in = grads[0][:d_in].reshape(N_DEV, d_in // N_DEV, d)
            pushed["in"], pushed["out"] = _sc_exchange([p_in, pushed["p_out"]], [True, True], "scatter_rest", 3,
                                                       after=pushed["ff1"])
        return grads[0]

    res = _local_step(x, loss_target, meta_full, w2d["ln_in_g"], w2d["ln_in_b"], w_in_full, conv_w_full, w2d["conv_b"],
                      w2d["conv_ln_g"], w2d["conv_ln_b"], gate_up_full, w2d["gate_bias"], w2d["gla_norm_g"], late_weights,
                      w2d["ln1_g"], w2d["ln1_b"], w2d["ln2_g"], w2d["ln2_b"], push)

    dc = res["conv_w"].shape[1]
    hk = res["gate_up"].shape[1]
    sh_meta = res["meta_tokens"].reshape(N_META, N_DEV, LANES).transpose(1, 0, 2)
    sh_conv = jnp.pad(res["conv_w"].reshape(CONV_WIDTH, N_DEV, dc // N_DEV).transpose(1, 0, 2),
                      ((0, 0), (0, 32 - CONV_WIDTH), (0, LANES - dc // N_DEV)))
    sh_gup = jnp.pad(res["gate_up"].reshape(GLA_RANK, N_DEV, hk // N_DEV).transpose(1, 0, 2),
                     ((0, 0), (0, 0), (0, LANES - hk // N_DEV)))
    p_sh = jnp.concatenate([sh_meta, sh_conv, sh_gup], axis=1)
    p_vec = jnp.concatenate([jnp.pad(res[k], ((0, 0), (0, d - res[k].shape[1]))) for k in _VEC_ORDER]
                            + [jnp.full((1, d), res["loss"], F32), jnp.zeros((15 - len(_VEC_ORDER), d), F32)], axis=0)

    r_sh, r_vec = _exchange([p_sh, p_vec], [True, False], "scatter_small")
    r_ff1, r_ff2, r_out, r_in = pushed["ff1"], pushed["ff2"], pushed["out"], pushed["in"]

    upd = {}
    upd["w_in"] = [a.T for a in _update_big(r_in, w2d["w_in"], m2d["w_in"], v2d["w_in"], "update_w_in")]
    upd["w_out"] = _update_big(r_out, w2d["w_out"], m2d["w_out"], v2d["w_out"], "update_w_out")
    upd["w_ff1"] = _update_big(r_ff1, w2d["w_ff1"], m2d["w_ff1"], v2d["w_ff1"], "update_w_ff1")
    upd["w_ff2"] = _update_big(r_ff2, w2d["w_ff2"], m2d["w_ff2"], v2d["w_ff2"], "update_w_ff2")
    small = [s[0] for s in _SHARDED_SMALL] + list(_VEC_ORDER)
    upd_small, loss = _update_small(r_sh, r_vec, {k: (w2d[k], m2d[k], v2d[k]) for k in small})
    upd.update(upd_small)

    outs = [loss, res["grad_x"]]
    for j in range(4):
        outs += [upd[k][j].reshape(shapes[k]) for k in _WEIGHTS]
    return tuple(outs)
```

```python
import functools

import jax
import jax.numpy as jnp
from jax import lax
from jax.experimental import pallas as pl
from jax.experimental.pallas import tpu as pltpu
from jax.experimental.pallas import tpu_sc as plsc

F32 = jnp.float32
BF16 = jnp.bfloat16

N_META = 16
CHUNK = 64
PAD_FRONT = (-N_META) % CHUNK
X_OFF = PAD_FRONT + N_META
CONV_WIDTH = 31
CONV_HALO = 32
CONV_SUB = 64
CONV_WIN = CONV_SUB + CONV_HALO
GLA_HEADS = 4
GLA_DK = 64
GLA_DV = 128
GLA_RANK = 16
GLA_TAU = 16.0
QK_SCALE = GLA_DK ** -0.5
LN_EPS = 1e-5
ALPHA = 2.0 ** 0.25
LANES = 128
N_DEV = 8
ADAM_LR = 0.001
ADAM_B1 = 0.9
ADAM_B2 = 0.999
ADAM_EPS = 1e-08
ADAM_WD = 0.01
ADAM_STEP = 10
VMEM_LIMIT = 56 * 1024 * 1024
MESH_AXES = ("x", "y", "c")


def _sds(shape, dtype):
    return jax.ShapeDtypeStruct(shape, dtype)


def _mm(a, b):
    return jnp.dot(a, b, preferred_element_type=F32)


def _mm_nt(a, b):
    return lax.dot_general(a, b, (((1,), (1,)), ((), ())), preferred_element_type=F32)


def _mm_tn(a, b):
    return lax.dot_general(a, b, (((0,), (0,)), ((), ())), preferred_element_type=F32)


def _sigmoid(x):
    return 1.0 / (1.0 + jnp.exp(-x))


def _log_sigmoid(z):
    return jnp.minimum(z, 0.0) - jnp.log(1.0 + jnp.exp(-jnp.abs(z)))


def _ln(x):
    mu = jnp.mean(x, axis=-1, keepdims=True)
    xc = x - mu
    var = jnp.mean(xc * xc, axis=-1, keepdims=True)
    rstd = lax.rsqrt(var + LN_EPS)
    return xc * rstd, rstd


def _ln_bwd(dyg, xhat, rstd):
    m1 = jnp.mean(dyg, axis=-1, keepdims=True)
    m2 = jnp.mean(dyg * xhat, axis=-1, keepdims=True)
    return rstd * (dyg - m1 - xhat * m2)


def _rowsum(x):
    return jnp.sum(x, axis=0, keepdims=True)


def _row_in_seq(i, tm, tp):
    base = lax.rem(i * tm, tp)
    return base + lax.broadcasted_iota(jnp.int32, (tm, 1), 0)


def _split3(x):
    hi = x.astype(BF16)
    r1 = x - hi.astype(F32)
    mid = r1.astype(BF16)
    lo = (r1 - mid.astype(F32)).astype(BF16)
    return hi, mid, lo


def _tri_mm(tri, x):
    hi, mid, lo = _split3(x)
    return _mm(tri, hi) + _mm(tri, mid) + _mm(tri, lo)


def _params(sem):
    return pltpu.CompilerParams(dimension_semantics=sem, vmem_limit_bytes=VMEM_LIMIT)


def _pick_tile(n, prefs):
    for t in prefs:
        if n % t == 0:
            return t
    raise ValueError(f"no tile for {n}")


def _x_tile_row(tp, seq, tx):
    tps = seq // tx
    return lambda i: pl.multiple_of((i // tps) * tp + X_OFF + (i % tps) * tx, CHUNK)


def _ln_in_x(x2, g, b, tp, seq, tx):
    rx, d = x2.shape
    r = rx // seq * tp
    row = _x_tile_row(tp, seq, tx)

    def body(x_ref, g_ref, b_ref, s0_ref, sb_ref):
        xhat, _ = _ln(x_ref[...])
        s = xhat * g_ref[...] + b_ref[...]
        s0_ref[...] = s
        sb_ref[...] = s.astype(BF16)

    out = pl.BlockSpec((pl.Element(tx), pl.Element(d)), lambda i: (row(i), 0))
    return pl.pallas_call(
        body, name="ln_in_x", grid=(rx // tx,),
        in_specs=[pl.BlockSpec((tx, d), lambda i: (i, 0)), pl.BlockSpec((1, d), lambda i: (0, 0)),
                  pl.BlockSpec((1, d), lambda i: (0, 0))],
        out_specs=[out, out],
        out_shape=[_sds((r, d), F32), _sds((r, d), BF16)],
        compiler_params=_params(("parallel",)),
    )(x2, g, b)


def _ln_in_head(head, g, b, s0, s0b, tp):
    r, d = s0.shape
    nb = tp // X_OFF

    def body(h_ref, g_ref, b_ref, s0_in, sb_in, s0_ref, sb_ref):
        xhat, _ = _ln(h_ref[...])
        real = lax.broadcasted_iota(jnp.int32, (X_OFF, 1), 0) >= PAD_FRONT
        s = jnp.where(real, xhat * g_ref[...] + b_ref[...], 0.0)
        s0_ref[...] = s
        sb_ref[...] = s.astype(BF16)

    anyspec = pl.BlockSpec(memory_space=pl.ANY)
    out = pl.BlockSpec((X_OFF, d), lambda i: (i * nb, 0))
    return pl.pallas_call(
        body, name="ln_in_head", grid=(r // tp,),
        in_specs=[pl.BlockSpec((X_OFF, d), lambda i: (0, 0)), pl.BlockSpec((1, d), lambda i: (0, 0)),
                  pl.BlockSpec((1, d), lambda i: (0, 0)), anyspec, anyspec],
        out_specs=[out, out],
        out_shape=[_sds((r, d), F32), _sds((r, d), BF16)],
        input_output_aliases={3: 0, 4: 1},
        compiler_params=_params(("parallel",)),
    )(head, g, b, s0, s0b)


def _inproj_fwd(s0b, w_in, tm):
    r, d = s0b.shape
    n = w_in.shape[0]

    def body(s_ref, w_ref, u_ref):
        u_ref[...] = _mm_nt(s_ref[...], w_ref[...])

    return pl.pallas_call(
        body, name="inproj_fwd", grid=(r // tm,),
        in_specs=[pl.BlockSpec((tm, d), lambda i: (i, 0)), pl.BlockSpec((n, d), lambda i: (0, 0))],
        out_specs=pl.BlockSpec((tm, n), lambda i: (i, 0)),
        out_shape=_sds((r, n), F32),
        compiler_params=_params(("parallel",)),
    )(s0b, w_in)


def _conv_taps(win, coef, lo):
    acc = None
    for rho in range(8):
        offs = [o for o in range(lo, lo + CONV_WIDTH) if o % 8 == rho]
        if not offs:
            continue
        rolled = win if rho == 0 else pltpu.roll(win, CONV_WIN - rho, 0)
        for o in offs:
            m8 = o - rho
            term = rolled[m8:m8 + CONV_SUB, :] * coef(o)
            acc = term if acc is None else acc + term
    return acc


def _conv_fwd(u, w32, cb, cg, cbe, tp, tc, dc):
    r = u.shape[0]
    hb = tc // CONV_HALO

    def body(a_ref, g_ref, ah_ref, gh_ref, w_ref, cb_ref, cg_ref, cbe_ref, c_ref, co_ref, hs_ref):
        t = pl.program_id(0)
        first = lax.rem(t * tc, tp) == 0
        hh = ah_ref[...] * _sigmoid(gh_ref[...])
        hs_ref[0:CONV_HALO, :] = jnp.where(first, 0.0, hh)
        hs_ref[CONV_HALO:CONV_HALO + tc, :] = a_ref[...] * _sigmoid(g_ref[...])

        def sub(k, carry):
            r0 = pl.multiple_of(k * CONV_SUB, CONV_SUB)
            win = hs_ref[pl.ds(r0, CONV_WIN), :]
            c = _conv_taps(win, lambda o: w_ref[o - 2:o - 1, :], 2) + cb_ref[...]
            c_ref[pl.ds(r0, CONV_SUB), :] = c
            xhat, _ = _ln(c)
            cn = xhat * cg_ref[...] + cbe_ref[...]
            co_ref[pl.ds(r0, CONV_SUB), :] = (cn * _sigmoid(cn)).astype(BF16)
            return carry

        lax.fori_loop(0, tc // CONV_SUB, sub, 0)

    vec = pl.BlockSpec((1, dc), lambda t: (0, 0))
    return pl.pallas_call(
        body, name="conv_fwd", grid=(r // tc,),
        in_specs=[pl.BlockSpec((tc, dc), lambda t: (t, 0)), pl.BlockSpec((tc, dc), lambda t: (t, 1)),
                  pl.BlockSpec((CONV_HALO, dc), lambda t: (jnp.maximum(t * hb - 1, 0), 0)),
                  pl.BlockSpec((CONV_HALO, dc), lambda t: (jnp.maximum(t * hb - 1, 0), 1)),
                  pl.BlockSpec((32, dc), lambda t: (0, 0)), vec, vec, vec],
        out_specs=[pl.BlockSpec((tc, dc), lambda t: (t, 0)), pl.BlockSpec((tc, dc), lambda t: (t, 0))],
        out_shape=[_sds((r, dc), F32), _sds((r, dc), BF16)],
        scratch_shapes=[pltpu.VMEM((CONV_HALO + tc, dc), F32)],
        compiler_params=_params(("parallel",)),
    )(u, u, u, u, w32, cb, cg, cbe)


def _inproj_conv_fwd(s0b, w_in, w32, cb, cg, cbe, tp, tc, dc):
    r, d = s0b.shape
    n = w_in.shape[0]
    nt = r // tc
    hrows = CONV_HALO + tc

    def body(s_ref, w_ref, cw_ref, cb_ref, cg_ref, cbe_ref, u_ref, c_ref, co_ref, h_ref):
        i = pl.program_id(0)
        cur = lax.rem(i, 2)
        prev = 1 - cur

        @pl.when(i == 0)
        def _():
            h_ref[...] = jnp.zeros_like(h_ref)

        u_ref[...] = _mm_nt(s_ref[...], w_ref[...])

        for k in range(tc // CONV_SUB):
            r0 = k * CONV_SUB
            win = h_ref[prev, r0:r0 + CONV_WIN, :]
            c = _conv_taps(win, lambda o: cw_ref[o - 2:o - 1, :], 2) + cb_ref[...]
            c_ref[r0:r0 + CONV_SUB, :] = c
            xhat, _ = _ln(c)
            cn = xhat * cg_ref[...] + cbe_ref[...]
            co_ref[r0:r0 + CONV_SUB, :] = (cn * _sigmoid(cn)).astype(BF16)

        first = lax.rem(jnp.minimum(i, nt - 1) * tc, tp) == 0
        h_ref[cur, 0:CONV_HALO, :] = jnp.where(first, 0.0, h_ref[prev, tc:hrows, :])
        h_ref[cur, CONV_HALO:hrows, :] = u_ref[:, 0:dc] * _sigmoid(u_ref[:, dc:2 * dc])

    vec = pl.BlockSpec((1, dc), lambda i: (0, 0))
    this = lambda i: (jnp.minimum(i, nt - 1), 0)
    before = lambda i: (jnp.maximum(i - 1, 0), 0)
    return pl.pallas_call(
        body, name="inproj_conv_fwd", grid=(nt + 1,),
        in_specs=[pl.BlockSpec((tc, d), this), pl.BlockSpec((n, d), lambda i: (0, 0)),
                  pl.BlockSpec((32, dc), lambda i: (0, 0)), vec, vec, vec],
        out_specs=[pl.BlockSpec((tc, n), this), pl.BlockSpec((tc, dc), before), pl.BlockSpec((tc, dc), before)],
        out_shape=[_sds((r, n), F32), _sds((r, dc), F32), _sds((r, dc), BF16)],
        scratch_shapes=[pltpu.VMEM((2, hrows, dc), F32)],
        compiler_params=_params(("arbitrary",)),
    )(s0b, w_in, w32, cb, cg, cbe)


def _tri_mm_all(tri, xs):
    parts = [_split3(x) for x in xs]
    acc = [None] * len(xs)
    for t in range(3):
        for j in range(len(xs)):
            term = _mm(tri, parts[j][t])
            acc[j] = term if t == 0 else acc[j] + term
    return acc


def _gla_prep(qk_ref, gd_ref, gup, gb, n0, kc):
    rows = [slice(j * CHUNK, (j + 1) * CHUNK) for j in range(kc)]
    ri = lax.broadcasted_iota(jnp.int32, (CHUNK, CHUNK), 0)
    ci = lax.broadcasted_iota(jnp.int32, (CHUNK, CHUNK), 1)
    low = (ri >= ci).astype(BF16)
    hk = GLA_HEADS * GLA_DK
    gds = [gd_ref[rw, :] for rw in rows]
    zs = [_mm(g.astype(BF16), gup) + gb for g in gds]
    reals = [(n0 + j) * CHUNK + lax.broadcasted_iota(jnp.int32, (CHUNK, 1), 0) >= PAD_FRONT for j in range(kc)]
    lgs = [jnp.where(reals[j], _log_sigmoid(zs[j]) * (1.0 / GLA_TAU), 0.0) for j in range(kc)]
    bs = _tri_mm_all(low, lgs)
    out = []
    for j in range(kc):
        b, bl = bs[j], _rowsum(lgs[j])
        q = qk_ref[rows[j], :hk] * QK_SCALE
        k = qk_ref[rows[j], hk:]
        eb, enb, ebl = jnp.exp(b), jnp.exp(-b), jnp.exp(bl - b)
        out.append(dict(rows=rows[j], gd=gds[j], z=zs[j], real=reals[j], eb=eb, enb=enb, ebl=ebl, gam=jnp.exp(bl),
                        qe=q * eb, ke=k * enb, kd=k * ebl))
    return out, ri, ci


def _gla_heads(p, v_ref):
    ops = []
    for h in range(GLA_HEADS):
        hp, h2 = divmod(h, 2)
        ls = slice(hp * LANES, (hp + 1) * LANES)
        m = _head_mask(h2)
        ops.append(dict(ls=ls, m=m, vs=slice(h * GLA_DV, (h + 1) * GLA_DV),
                        qe=jnp.where(m, p["qe"][:, ls], 0.0).astype(BF16),
                        kd=jnp.where(m, p["kd"][:, ls], 0.0).astype(BF16),
                        ke=p["ke"][:, ls].astype(BF16),
                        v=v_ref[p["rows"], h * GLA_DV:(h + 1) * GLA_DV].astype(BF16)))
    return ops


def _head_mask(h2):
    lane = lax.broadcasted_iota(jnp.int32, (1, LANES), 1)
    return (lane < GLA_DK) if h2 == 0 else (lane >= GLA_DK)


def _gla_fwd(u, gup, gb, gn, bsz, nc, kc):
    r = u.shape[0]
    hv = GLA_HEADS * GLA_DV
    ns = nc // kc

    def body(qk_ref, v_ref, r_ref, gd_ref, gup_ref, gb_ref, gn_ref, go_ref, sta_ref, st_ref):
        t = pl.program_id(1)

        @pl.when(t == 0)
        def _():
            st_ref[...] = jnp.zeros_like(st_ref)

        ps, ri, ci = _gla_prep(qk_ref, gd_ref, gup_ref[...], gb_ref[...], t * kc, kc)
        tril = ri >= ci
        items = [(j, h) for j in range(kc) for h in range(GLA_HEADS)]
        ops = [_gla_heads(p, v_ref) for p in ps]
        a = {jh: jnp.where(tril, _mm_nt(ops[jh[0]][jh[1]]["qe"], ops[jh[0]][jh[1]]["ke"]), 0.0).astype(BF16) for jh in items}
        oi = {jh: _mm(a[jh], ops[jh[0]][jh[1]]["v"]) for jh in items}
        inc = {jh: _mm_tn(ops[jh[0]][jh[1]]["v"], ops[jh[0]][jh[1]]["kd"]) for jh in items}
        sts = [st_ref[h] for h in range(GLA_HEADS)]
        for j, h in items:
            op, p = ops[j][h], ps[j]
            st = sts[h]
            sta_ref[j, h] = st
            o = oi[j, h] + _mm_nt(op["qe"], st.astype(BF16))
            sts[h] = st * p["gam"][:, op["ls"]] + inc[j, h]
            rs = lax.rsqrt(jnp.mean(o * o, axis=-1, keepdims=True) + LN_EPS)
            rr = r_ref[p["rows"], op["vs"]]
            go_ref[p["rows"], op["vs"]] = (o * rs * gn_ref[...] * (rr * _sigmoid(rr))).astype(BF16)
        for h in range(GLA_HEADS):
            st_ref[h] = sts[h]

    rowblk = lambda col: (lambda b, t: (b * ns + t, col))
    const = lambda b, t: (0, 0)
    return pl.pallas_call(
        body, name="gla_fwd", grid=(bsz, ns),
        in_specs=[pl.BlockSpec((kc * CHUNK, 512), rowblk(2)), pl.BlockSpec((kc * CHUNK, hv), rowblk(3)),
                  pl.BlockSpec((kc * CHUNK, hv), rowblk(4)), pl.BlockSpec((kc * CHUNK, LANES), rowblk(20)),
                  pl.BlockSpec((LANES, 256), const), pl.BlockSpec((1, 256), const), pl.BlockSpec((1, GLA_DV), const)],
        out_specs=[pl.BlockSpec((kc * CHUNK, hv), rowblk(0)),
                   pl.BlockSpec((kc, GLA_HEADS, LANES, LANES), lambda b, t: (b * ns + t, 0, 0, 0))],
        out_shape=[_sds((r, hv), BF16), _sds((bsz * nc, GLA_HEADS, LANES, LANES), F32)],
        scratch_shapes=[pltpu.VMEM((GLA_HEADS, LANES, LANES), F32)],
        compiler_params=_params(("parallel", "arbitrary")),
    )(u, u, u, u, gup, gb, gn)


def _outproj_fwd(s0, co, go, w_out, g1, b1, tm):
    r, d = s0.shape
    dc = co.shape[1]

    def body(s0_ref, co_ref, go_ref, w_ref, g_ref, b_ref, p1_ref, s1_ref, s1b_ref):
        nb = 4 if tm % 64 == 0 else 1
        blocks = [slice(k * (tm // nb), (k + 1) * (tm // nb)) for k in range(nb)]
        mixes = [_mm(co_ref[rows, :], w_ref[0:dc, :]) + _mm(go_ref[rows, :], w_ref[dc:2 * dc, :]) for rows in blocks]
        for rows, mix in zip(blocks, mixes):
            p1 = ALPHA * s0_ref[rows, :] + mix
            p1_ref[rows, :] = p1
            xhat, _ = _ln(p1)
            s1 = xhat * g_ref[...] + b_ref[...]
            s1_ref[rows, :] = s1
            s1b_ref[rows, :] = s1.astype(BF16)

    row = lambda w: pl.BlockSpec((tm, w), lambda i: (i, 0))
    vec = pl.BlockSpec((1, d), lambda i: (0, 0))
    return pl.pallas_call(
        body, name="outproj_fwd", grid=(r // tm,),
        in_specs=[row(d), row(dc), row(dc), pl.BlockSpec((2 * dc, d), lambda i: (0, 0)), vec, vec],
        out_specs=[row(d), row(d), row(d)],
        out_shape=[_sds((r, d), F32), _sds((r, d), F32), _sds((r, d), BF16)],
        compiler_params=_params(("parallel",)),
    )(s0, co, go, w_out, g1, b1)


def _mlp_fwd(s1, s1b, w1g, w2, g2, b2, tgt, tp, tm, ns):
    r, d = s1.shape
    nh, _, th = w1g.shape
    nj = nh // ns

    def body(s1_ref, sb_ref, w1_ref, w2_ref, g_ref, b_ref, t_ref, hm_ref, dp2_ref, dpb_ref, loss_ref, dg_ref, db_ref, acc_ref):
        i = pl.program_id(0)
        j = pl.program_id(1)

        @pl.when(jnp.logical_and(i == 0, j == 0))
        def _():
            loss_ref[...] = jnp.zeros_like(loss_ref)
            dg_ref[...] = jnp.zeros_like(dg_ref)
            db_ref[...] = jnp.zeros_like(db_ref)

        @pl.when(j == 0)
        def _():
            acc_ref[...] = jnp.zeros_like(acc_ref)

        def mlp_rows(rows):
            hs = [_mm(sb_ref[rows, :], w1_ref[s]) for s in range(ns)]
            acc = acc_ref[rows, :]
            for s in range(ns):
                hm_ref[rows, s * th:(s + 1) * th] = hs[s].astype(BF16)
                act = jnp.square(jnp.maximum(hs[s], 0.0))
                acc = acc + _mm(act.astype(BF16), w2_ref[s * th:(s + 1) * th, :])
            return acc

        @pl.when(j < nj - 1)
        def _():
            acc_ref[...] = mlp_rows(slice(None))

        @pl.when(j == nj - 1)
        def _():
            halves = [slice(0, tm // 2), slice(tm // 2, tm)]
            accs = [mlp_rows(rows) for rows in halves]
            isx = _row_in_seq(i, tm, tp) >= X_OFF
            tg = t_ref[...]
            tg = jnp.where(i == 0, pltpu.roll(tg, X_OFF, 0), tg)
            for rows, acc in zip(halves, accs):
                p2 = ALPHA * s1_ref[rows, :] + acc
                xhat, rstd = _ln(p2)
                s2 = xhat * g_ref[...] + b_ref[...]
                err = jnp.where(isx[rows], s2 - tg[rows], 0.0)
                loss_ref[...] += 0.5 * jnp.sum(jnp.mean(err * err, axis=-1, keepdims=True))
                dy = err * (1.0 / d)
                dg_ref[...] += _rowsum(dy * xhat)
                db_ref[...] += _rowsum(dy)
                dp2 = _ln_bwd(dy * g_ref[...], xhat, rstd)
                dp2_ref[rows, :] = dp2
                dpb_ref[rows, :] = dp2.astype(BF16)

    row = pl.BlockSpec((tm, d), lambda i, j: (i, 0))
    vec = pl.BlockSpec((1, d), lambda i, j: (0, 0))
    tgt_row = pl.BlockSpec((pl.Element(tm), pl.Element(d)),
                           lambda i, j: (pl.multiple_of(jnp.maximum(i * tm - X_OFF * ((i * tm) // tp + 1), 0), CHUNK), 0))
    return pl.pallas_call(
        body, name="mlp_fwd", grid=(r // tm, nj),
        in_specs=[row, row, pl.BlockSpec((ns, d, th), lambda i, j: (j, 0, 0)), pl.BlockSpec((ns * th, d), lambda i, j: (j, 0)),
                  vec, vec, tgt_row],
        out_specs=[pl.BlockSpec((tm, ns * th), lambda i, j: (i, j)), row, row,
                   pl.BlockSpec((8, LANES), lambda i, j: (0, 0)), vec, vec],
        out_shape=[_sds((r, nh * th), BF16), _sds((r, d), F32), _sds((r, d), BF16), _sds((8, LANES), F32),
                   _sds((1, d), F32), _sds((1, d), F32)],
        scratch_shapes=[pltpu.VMEM((tm, d), F32)],
        compiler_params=_params(("arbitrary", "arbitrary")),
    )(s1, s1b, w1g, w2, g2, b2, tgt)


def _mlp_bwd_act(dp2, dpb, hm, w1g, w2, p1, g1, tm, ns):
    r, d = dp2.shape
    nh, _, th = w1g.shape
    nj = nh // ns

    def body(dp2_ref, dpb_ref, hm_ref, w1_ref, w2_ref, p1_ref, g_ref, dh_ref, dp1_ref, dg_ref, db_ref, acc_ref):
        i = pl.program_id(0)
        j = pl.program_id(1)

        @pl.when(jnp.logical_and(i == 0, j == 0))
        def _():
            dg_ref[...] = jnp.zeros_like(dg_ref)
            db_ref[...] = jnp.zeros_like(db_ref)

        @pl.when(j == 0)
        def _():
            acc_ref[...] = jnp.zeros_like(acc_ref)

        def mlp_rows(rows):
            dacts = [_mm_nt(dpb_ref[rows, :], w2_ref[s * th:(s + 1) * th, :]) for s in range(ns)]
            acc = acc_ref[rows, :]
            for s in range(ns):
                cols = slice(s * th, (s + 1) * th)
                dh = (dacts[s] * (2.0 * jnp.maximum(hm_ref[rows, cols].astype(F32), 0.0))).astype(BF16)
                dh_ref[rows, cols] = dh
                acc = acc + _mm_nt(dh, w1_ref[s])
            return acc

        @pl.when(j < nj - 1)
        def _():
            acc_ref[...] = mlp_rows(slice(None))

        @pl.when(j == nj - 1)
        def _():
            halves = [slice(0, tm // 2), slice(tm // 2, tm)]
            accs = [mlp_rows(rows) for rows in halves]
            for rows, acc in zip(halves, accs):
                ds1 = ALPHA * dp2_ref[rows, :] + acc
                xhat, rstd = _ln(p1_ref[rows, :])
                dg_ref[...] += _rowsum(ds1 * xhat)
                db_ref[...] += _rowsum(ds1)
                dp1_ref[rows, :] = _ln_bwd(ds1 * g_ref[...], xhat, rstd)

    row = pl.BlockSpec((tm, d), lambda i, j: (i, 0))
    vec = pl.BlockSpec((1, d), lambda i, j: (0, 0))
    blk = pl.BlockSpec((tm, ns * th), lambda i, j: (i, j))
    return pl.pallas_call(
        body, name="mlp_bwd_act", grid=(r // tm, nj),
        in_specs=[row, row, blk, pl.BlockSpec((ns, d, th), lambda i, j: (j, 0, 0)),
                  pl.BlockSpec((ns * th, d), lambda i, j: (j, 0)), row, vec],
        out_specs=[blk, row, vec, vec],
        out_shape=[_sds((r, nh * th), BF16), _sds((r, d), F32), _sds((1, d), F32), _sds((1, d), F32)],
        scratch_shapes=[pltpu.VMEM((tm, d), F32)],
        compiler_params=_params(("arbitrary", "arbitrary")),
    )(dp2, dpb, hm, w1g, w2, p1, g1)


def _mlp_bwd_w(s1b, hm, dh, dpb, nh, tm, ns):
    r, d = s1b.shape
    th = hm.shape[1] // nh

    def body(s1_ref, hm_ref, dh_ref, dp2_ref, dw1_ref, dw2_ref, a1_ref, a2_ref):
        i = pl.program_id(1)

        @pl.when(i == 0)
        def _():
            a1_ref[...] = jnp.zeros_like(a1_ref)
            a2_ref[...] = jnp.zeros_like(a2_ref)

        for s in range(ns):
            a1_ref[s] += _mm_tn(s1_ref[...], dh_ref[:, s * th:(s + 1) * th])
        for s in range(ns):
            act = jnp.square(jnp.maximum(hm_ref[:, s * th:(s + 1) * th].astype(F32), 0.0)).astype(BF16)
            a2_ref[s] += _mm_tn(act, dp2_ref[...])

        @pl.when(i == pl.num_programs(1) - 1)
        def _():
            dw1_ref[...] = a1_ref[...].astype(BF16)
            dw2_ref[...] = a2_ref[...].astype(BF16)

    row = pl.BlockSpec((tm, d), lambda j, i: (i, 0))
    blk = pl.BlockSpec((tm, ns * th), lambda j, i: (i, j))
    return pl.pallas_call(
        body, name="mlp_bwd_w", grid=(nh // ns, r // tm),
        in_specs=[row, blk, blk, row],
        out_specs=[pl.BlockSpec((ns, d, th), lambda j, i: (j, 0, 0)), pl.BlockSpec((ns, th, d), lambda j, i: (j, 0, 0))],
        out_shape=[_sds((nh, d, th), BF16), _sds((nh, th, d), BF16)],
        scratch_shapes=[pltpu.VMEM((ns, d, th), F32), pltpu.VMEM((ns, th, d), F32)],
        compiler_params=_params(("parallel", "arbitrary")),
    )(s1b, hm, dh, dpb)


def _outproj_bwd(dp1, co, go, w_out, dep, tm):
    r, d = dp1.shape
    dc = co.shape[1]

    def body(dp_ref, co_ref, go_ref, w_ref, dep_ref, dmi_ref, dw_ref, acc_ref):
        i = pl.program_id(0)

        @pl.when(i == 0)
        def _():
            acc_ref[...] = jnp.zeros_like(acc_ref)

        dpb = dp_ref[...].astype(BF16)
        dmi_ref[...] = _mm_nt(dpb, w_ref[...])
        acc_ref[0:dc, :] += _mm_tn(co_ref[...], dpb)
        acc_ref[dc:2 * dc, :] += _mm_tn(go_ref[...], dpb)

        @pl.when(i == pl.num_programs(0) - 1)
        def _():
            dw_ref[...] = acc_ref[...].astype(BF16)

    row = lambda w: pl.BlockSpec((tm, w), lambda i: (i, 0))
    full = pl.BlockSpec((2 * dc, d), lambda i: (0, 0))
    return pl.pallas_call(
        body, name="outproj_bwd", grid=(r // tm,),
        in_specs=[row(d), row(dc), row(dc), full, pl.BlockSpec(memory_space=pl.ANY)],
        out_specs=[row(2 * dc), full],
        out_shape=[_sds((r, 2 * dc), F32), _sds((2 * dc, d), BF16)],
        scratch_shapes=[pltpu.VMEM((2 * dc, d), F32)],
        compiler_params=_params(("arbitrary",)),
    )(dp1, co, go, w_out, dep)


def _gla_bwd(u, dmi, sta, gup, gb, gn, dep, bsz, nc, kc):
    r = u.shape[0]
    hv = GLA_HEADS * GLA_DV
    hk = GLA_HEADS * GLA_DK
    ns = nc // kc

    def body(qk_ref, v_ref, r_ref, gd_ref, dgo_ref, sta_ref, gup_ref, gb_ref, gn_ref, dep_ref,
             dqk_ref, dv_ref, dr_ref, dgd_ref, dgn_ref, dgb_ref, dgup_ref, dst_ref):
        bi = pl.program_id(0)
        t = pl.program_id(1)

        @pl.when(jnp.logical_and(bi == 0, t == 0))
        def _():
            dgn_ref[...] = jnp.zeros_like(dgn_ref)
            dgb_ref[...] = jnp.zeros_like(dgb_ref)
            dgup_ref[...] = jnp.zeros_like(dgup_ref)

        @pl.when(t == 0)
        def _():
            dst_ref[...] = jnp.zeros_like(dst_ref)

        ps, ri, ci = _gla_prep(qk_ref, gd_ref, gup_ref[...], gb_ref[...], (ns - 1 - t) * kc, kc)
        tril = ri >= ci
        items = [(j, h) for j in reversed(range(kc)) for h in range(GLA_HEADS)]
        ops = [_gla_heads(p, v_ref) for p in ps]
        op = lambda jh: ops[jh[0]][jh[1]]
        st = {jh: sta_ref[jh[0], jh[1]] for jh in items}
        stb = {jh: st[jh].astype(BF16) for jh in items}
        a = {jh: jnp.where(tril, _mm_nt(op(jh)["qe"], op(jh)["ke"]), 0.0).astype(BF16) for jh in items}
        o1 = {jh: _mm(a[jh], op(jh)["v"]) for jh in items}
        o2 = {jh: _mm_nt(op(jh)["qe"], stb[jh]) for jh in items}
        dob = {}
        dgn = jnp.zeros((1, GLA_DV), F32)
        for jh in items:
            rows, vs = ps[jh[0]]["rows"], op(jh)["vs"]
            o = o1[jh] + o2[jh]
            rr = r_ref[rows, vs]
            sr = _sigmoid(rr)
            rs = lax.rsqrt(jnp.mean(o * o, axis=-1, keepdims=True) + LN_EPS)
            y = o * rs
            dgo = dgo_ref[rows, vs]
            don = dgo * (rr * sr)
            dr_ref[rows, vs] = (dgo * (y * gn_ref[...]) * (sr * (1.0 + rr * (1.0 - sr)))).astype(BF16)
            dgn = dgn + _rowsum(don * y)
            dxn = don * gn_ref[...]
            dob[jh] = (rs * (dxn - y * jnp.mean(dxn * y, axis=-1, keepdims=True))).astype(BF16)
        da = {jh: jnp.where(tril, _mm_nt(dob[jh], op(jh)["v"]), 0.0).astype(BF16) for jh in items}
        dv1 = {jh: _mm_tn(a[jh], dob[jh]) for jh in items}
        dqe1 = {jh: _mm(da[jh], op(jh)["ke"]) for jh in items}
        dqe2 = {jh: _mm(dob[jh], stb[jh]) for jh in items}
        dke1 = {jh: _mm_tn(da[jh], op(jh)["qe"]) for jh in items}
        inc = {jh: _mm_tn(dob[jh], op(jh)["qe"]) for jh in items}
        dsts = [dst_ref[h] for h in range(GLA_HEADS)]
        dkd1, dgam1 = {}, {}
        for jh in items:
            j, h = jh
            dst = dsts[h]
            dstb = dst.astype(BF16)
            dv_ref[ps[j]["rows"], op(jh)["vs"]] = (dv1[jh] + _mm_nt(op(jh)["kd"], dstb)).astype(BF16)
            dkd1[jh] = _mm(op(jh)["v"], dstb)
            dgam1[jh] = _rowsum(dst * st[jh])
            dsts[h] = dst * ps[j]["gam"][:, op(jh)["ls"]] + inc[jh]
        for h in range(GLA_HEADS):
            dst_ref[h] = dsts[h]
        upper = (ri <= ci).astype(BF16)
        dbs, dbls = [], []
        for j in range(kc):
            p = ps[j]
            tiles = [[op((j, 2 * hp + h2)) for h2 in range(2)] for hp in range(GLA_HEADS // 2)]
            head = lambda d, hp, h2: d[j, 2 * hp + h2]
            lanes = lambda f: jnp.concatenate([f(hp) for hp in range(GLA_HEADS // 2)], axis=1)
            dqe = lanes(lambda hp: sum(jnp.where(tiles[hp][h2]["m"], head(dqe1, hp, h2) + head(dqe2, hp, h2), 0.0)
                                       for h2 in range(2)))
            dke = lanes(lambda hp: head(dke1, hp, 0) + head(dke1, hp, 1))
            dkd = lanes(lambda hp: sum(jnp.where(tiles[hp][h2]["m"], head(dkd1, hp, h2), 0.0) for h2 in range(2)))
            dgam = lanes(lambda hp: head(dgam1, hp, 0) + head(dgam1, hp, 1))
            dqk_ref[p["rows"], :hk] = (dqe * p["eb"] * QK_SCALE).astype(BF16)
            dqk_ref[p["rows"], hk:] = (dke * p["enb"] + dkd * p["ebl"]).astype(BF16)
            dkdkd = dkd * p["kd"]
            dbs.append(dqe * p["qe"] - dke * p["ke"] - dkdkd)
            dbls.append(_rowsum(dkdkd) + dgam * p["gam"])
        dlgs = _tri_mm_all(upper, dbs)
        dzb = []
        dgb = jnp.zeros((1, hk), F32)
        for j in range(kc):
            p = ps[j]
            dz = jnp.where(p["real"], (dlgs[j] + dbls[j]) * (1.0 / GLA_TAU) * _sigmoid(-p["z"]), 0.0)
            dgb = dgb + _rowsum(dz)
            dzb.append(dz.astype(BF16))
        dgup = sum(_mm_tn(ps[j]["gd"].astype(BF16), dzb[j]) for j in range(kc))
        for j in range(kc):
            dgd_ref[ps[j]["rows"], :] = _mm_nt(dzb[j], gup_ref[...]).astype(BF16)
        dgb_ref[...] += dgb
        dgup_ref[...] += dgup
        dgn_ref[...] += dgn

    rowblk = lambda col: (lambda b, t: (b * ns + ns - 1 - t, col))
    const = lambda b, t: (0, 0)
    return pl.pallas_call(
        body, name="gla_bwd", grid=(bsz, ns),
        in_specs=[pl.BlockSpec((kc * CHUNK, 2 * hk), rowblk(2)), pl.BlockSpec((kc * CHUNK, hv), rowblk(3)),
                  pl.BlockSpec((kc * CHUNK, hv), rowblk(4)), pl.BlockSpec((kc * CHUNK, LANES), rowblk(20)),
                  pl.BlockSpec((kc * CHUNK, hv), rowblk(1)),
                  pl.BlockSpec((kc, GLA_HEADS, LANES, LANES), lambda b, t: (b * ns + ns - 1 - t, 0, 0, 0)),
                  pl.BlockSpec((LANES, 256), const), pl.BlockSpec((1, 256), const), pl.BlockSpec((1, GLA_DV), const),
                  pl.BlockSpec(memory_space=pl.ANY)],
        out_specs=[pl.BlockSpec((kc * CHUNK, 2 * hk), rowblk(0)), pl.BlockSpec((kc * CHUNK, hv), rowblk(0)),
                   pl.BlockSpec((kc * CHUNK, hv), rowblk(0)), pl.BlockSpec((kc * CHUNK, LANES), rowblk(0)),
                   pl.BlockSpec((1, GLA_DV), const), pl.BlockSpec((1, 256), const), pl.BlockSpec((LANES, 256), const)],
        out_shape=[_sds((r, 2 * hk), BF16), _sds((r, hv), BF16), _sds((r, hv), BF16), _sds((r, LANES), BF16),
                   _sds((1, GLA_DV), F32), _sds((1, 256), F32), _sds((LANES, 256), F32)],
        scratch_shapes=[pltpu.VMEM((GLA_HEADS, LANES, LANES), F32)],
        compiler_params=_params(("arbitrary", "arbitrary")),
    )(u, u, u, u, dmi, sta, gup, gb, gn, dep)


def _conv_bwd(u, c, dmi, w32, cg, cbe, tp, tc, dc):
    r = u.shape[0]
    hb = tc // CONV_HALO
    nhalo = r // CONV_HALO

    def dconv(cv, dco, cg_ref, cbe_ref):
        xhat, rstd = _ln(cv)
        cn = xhat * cg_ref[...] + cbe_ref[...]
        sg = _sigmoid(cn)
        dcn = dco * (sg * (1.0 + cn * (1.0 - sg)))
        return _ln_bwd(dcn * cg_ref[...], xhat, rstd), dcn, xhat

    def body(a_ref, g_ref, ah_ref, gh_ref, c_ref, dco_ref, ch_ref, dcoh_ref, w_ref, cg_ref, cbe_ref,
             du_ref, dw_ref, dcb_ref, dcg_ref, dcbe_ref, hs_ref, dcs_ref, dw8_ref):
        t = pl.program_id(0)

        @pl.when(t == 0)
        def _():
            dw8_ref[...] = jnp.zeros_like(dw8_ref)
            dcb_ref[...] = jnp.zeros_like(dcb_ref)
            dcg_ref[...] = jnp.zeros_like(dcg_ref)
            dcbe_ref[...] = jnp.zeros_like(dcbe_ref)

        first = lax.rem(t * tc, tp) == 0
        last = lax.rem((t + 1) * tc, tp) == 0
        hh = ah_ref[...] * _sigmoid(gh_ref[...])
        hs_ref[0:CONV_HALO, :] = jnp.where(first, 0.0, hh)
        hs_ref[CONV_HALO:CONV_HALO + tc, :] = a_ref[...] * _sigmoid(g_ref[...])
        dch, _, _ = dconv(ch_ref[...], dcoh_ref[...], cg_ref, cbe_ref)
        dcs_ref[tc:tc + CONV_HALO, :] = jnp.where(last, 0.0, dch)

        def sub1(k, carry):
            r0 = pl.multiple_of(k * CONV_SUB, CONV_SUB)
            dcv, dcn, xhat = dconv(c_ref[pl.ds(r0, CONV_SUB), :], dco_ref[pl.ds(r0, CONV_SUB), :], cg_ref, cbe_ref)
            dcs_ref[pl.ds(r0, CONV_SUB), :] = dcv
            dcb_ref[...] += _rowsum(dcv)
            dcg_ref[...] += _rowsum(dcn * xhat)
            dcbe_ref[...] += _rowsum(dcn)
            return carry

        lax.fori_loop(0, tc // CONV_SUB, sub1, 0)

        def sub2(k, carry):
            r0 = pl.multiple_of(k * CONV_SUB, CONV_SUB)
            dwin = dcs_ref[pl.ds(r0, CONV_WIN), :]
            dh = _conv_taps(dwin, lambda o: w_ref[CONV_WIDTH - 1 - o:CONV_WIDTH - o, :], 0)
            av = a_ref[pl.ds(r0, CONV_SUB), :]
            sg = _sigmoid(g_ref[pl.ds(r0, CONV_SUB), :])
            du_ref[pl.ds(r0, CONV_SUB), 0:dc] = (dh * sg).astype(BF16)
            du_ref[pl.ds(r0, CONV_SUB), dc:2 * dc] = (dh * av * sg * (1.0 - sg)).astype(BF16)
            hwin = hs_ref[pl.ds(r0, CONV_WIN), :]
            dcv = dwin[0:CONV_SUB, :]
            for rho in range(8):
                offs = [o for o in range(2, 2 + CONV_WIDTH) if o % 8 == rho]
                rolled = hwin if rho == 0 else pltpu.roll(hwin, CONV_WIN - rho, 0)
                for o in offs:
                    m8 = o - rho
                    prod = dcv * rolled[m8:m8 + CONV_SUB, :]
                    dw8_ref[8 * (o - 2):8 * (o - 1), :] += jnp.sum(prod.reshape(CONV_SUB // 8, 8, dc), axis=0)
            return carry

        lax.fori_loop(0, tc // CONV_SUB, sub2, 0)

        @pl.when(t == pl.num_programs(0) - 1)
        def _():
            dw_ref[...] = jnp.zeros_like(dw_ref)
            for j in range(CONV_WIDTH):
                dw_ref[j:j + 1, :] = _rowsum(dw8_ref[8 * j:8 * (j + 1), :])

    vec = pl.BlockSpec((1, dc), lambda t: (0, 0))
    prev = lambda col: (lambda t: (jnp.maximum(t * hb - 1, 0), col))
    nxt = lambda col: (lambda t: (jnp.minimum((t + 1) * hb, nhalo - 1), col))
    return pl.pallas_call(
        body, name="conv_bwd", grid=(r // tc,),
        in_specs=[pl.BlockSpec((tc, dc), lambda t: (t, 0)), pl.BlockSpec((tc, dc), lambda t: (t, 1)),
                  pl.BlockSpec((CONV_HALO, dc), prev(0)), pl.BlockSpec((CONV_HALO, dc), prev(1)),
                  pl.BlockSpec((tc, dc), lambda t: (t, 0)), pl.BlockSpec((tc, dc), lambda t: (t, 0)),
                  pl.BlockSpec((CONV_HALO, dc), nxt(0)), pl.BlockSpec((CONV_HALO, dc), nxt(0)),
                  pl.BlockSpec((32, dc), lambda t: (0, 0)), vec, vec],
        out_specs=[pl.BlockSpec((tc, 2 * dc), lambda t: (t, 0)), pl.BlockSpec((32, dc), lambda t: (0, 0)), vec, vec, vec],
        out_shape=[_sds((r, 2 * dc), BF16), _sds((32, dc), F32), _sds((1, dc), F32), _sds((1, dc), F32), _sds((1, dc), F32)],
        scratch_shapes=[pltpu.VMEM((CONV_HALO + tc, dc), F32), pltpu.VMEM((tc + CONV_HALO, dc), F32),
                        pltpu.VMEM((8 * 32, dc), F32)],
        compiler_params=_params(("arbitrary",)),
    )(u, u, u, u, c, dmi, c, dmi, w32, cg, cbe)


def _inproj_bwd(dp1, dus, xsrc, g_in, w_in, dep, tp, seq, tx):
    r, d = dp1.shape
    widths = [x.shape[1] for x in dus]
    offs = [sum(widths[:k]) for k in range(len(widths))]
    n = w_in.shape[0]
    nd = len(dus)
    head = tx == 0
    rows = X_OFF if head else tx

    def body(*refs):
        dp_ref = refs[0]
        du_refs = refs[1:1 + nd]
        x_ref, g_ref, w_ref, _, out_ref, dg_ref, db_ref = refs[1 + nd:]
        i = pl.program_id(0)

        @pl.when(i == 0)
        def _():
            dg_ref[...] = jnp.zeros_like(dg_ref)
            db_ref[...] = jnp.zeros_like(db_ref)
            if head:
                out_ref[...] = jnp.zeros_like(out_ref)

        ds0 = ALPHA * dp_ref[...]
        for k in range(nd):
            ds0 = ds0 + _mm(du_refs[k][...], w_ref[offs[k]:offs[k] + widths[k], :])
        if head:
            ds0 = jnp.where(lax.broadcasted_iota(jnp.int32, (X_OFF, 1), 0) >= PAD_FRONT, ds0, 0.0)
        xhat, rstd = _ln(x_ref[...])
        dg_ref[...] += _rowsum(ds0 * xhat)
        db_ref[...] += _rowsum(ds0)
        dx = _ln_bwd(ds0 * g_ref[...], xhat, rstd)
        if head:
            out_ref[...] += dx[PAD_FRONT:X_OFF, :]
        else:
            out_ref[...] = dx

    if head:
        nb = tp // X_OFF
        row = lambda w: pl.BlockSpec((X_OFF, w), lambda i: (i * nb, 0))
        xspec = pl.BlockSpec((X_OFF, d), lambda i: (0, 0))
        ospec, oshape, steps = pl.BlockSpec((N_META, d), lambda i: (0, 0)), _sds((N_META, d), F32), r // tp
    else:
        start = _x_tile_row(tp, seq, tx)
        row = lambda w: pl.BlockSpec((pl.Element(tx), pl.Element(w)), lambda i: (start(i), 0))
        xspec = pl.BlockSpec((tx, d), lambda i: (i, 0))
        ospec, oshape, steps = xspec, _sds(xsrc.shape, F32), xsrc.shape[0] // tx
    vec = pl.BlockSpec((1, d), lambda i: (0, 0))
    return pl.pallas_call(
        body, name="inproj_bwd_head" if head else "inproj_bwd_x", grid=(steps,),
        in_specs=[row(d)] + [row(w) for w in widths] + [xspec, vec, pl.BlockSpec((n, d), lambda i: (0, 0)),
                                                        pl.BlockSpec(memory_space=pl.ANY)],
        out_specs=[ospec, vec, vec],
        out_shape=[oshape, _sds((1, d), F32), _sds((1, d), F32)],
        compiler_params=_params(("arbitrary",)),
    )(dp1, *dus, xsrc, g_in, w_in, dep)


def _inproj_bwd_w(s0, dus, tm):
    r, d = s0.shape
    widths = [x.shape[1] for x in dus]
    offs = [sum(widths[:k]) for k in range(len(widths))]
    nd = len(dus)

    def body(*refs):
        s_ref = refs[0]
        du_refs = refs[1:1 + nd]
        dw_ref, acc_ref = refs[1 + nd:]
        i = pl.program_id(0)

        @pl.when(i == 0)
        def _():
            acc_ref[...] = jnp.zeros_like(acc_ref)

        for k in range(nd):
            acc_ref[offs[k]:offs[k] + widths[k], :] += _mm_tn(du_refs[k][...], s_ref[...])

        @pl.when(i == pl.num_programs(0) - 1)
        def _():
            dw_ref[...] = acc_ref[...].astype(BF16)

    row = lambda w: pl.BlockSpec((tm, w), lambda i: (i, 0))
    return pl.pallas_call(
        body, name="inproj_bwd_w", grid=(r // tm,),
        in_specs=[row(d)] + [row(w) for w in widths],
        out_specs=pl.BlockSpec((sum(widths), d), lambda i: (0, 0)),
        out_shape=_sds((sum(widths), d), BF16),
        scratch_shapes=[pltpu.VMEM((sum(widths), d), F32)],
        compiler_params=_params(("arbitrary",)),
    )(s0, *dus)


def _local_step(x, tgt, meta, ln_in_g, ln_in_b, w_in, conv_w, conv_b, conv_ln_g, conv_ln_b, gate_up, gate_bias,
                gla_norm_g, late_weights, ln1_g, ln1_b, ln2_g, ln2_b, push):
    bsz, seq, d = x.shape
    tp = X_OFF + seq
    assert tp % CHUNK == 0
    nc = tp // CHUNK
    r = bsz * tp
    dc = conv_b.shape[1]
    tm = _pick_tile(tp, (352, 128, 64))
    tc = _pick_tile(tp, (704, 128, 64))

    x2 = x.reshape(bsz * seq, d)
    head = jnp.pad(meta, ((PAD_FRONT, 0), (0, 0)))
    tx = _pick_tile(seq, (512, 64))
    tgt_p = tgt.reshape(bsz * seq, d)
    w32 = jnp.pad(conv_w, ((0, 32 - CONV_WIDTH), (0, 0)))
    gup = jnp.pad(gate_up, ((0, LANES - GLA_RANK), (0, 0))).astype(BF16)

    s0, s0b = _ln_in_x(x2, ln_in_g, ln_in_b, tp, seq, tx)
    s0, s0b = _ln_in_head(head, ln_in_g, ln_in_b, s0, s0b, tp)
    tmm = _pick_tile(tp, (704, 128, 64))
    u, c, co = _inproj_conv_fwd(s0b, w_in, w32, conv_b, conv_ln_g, conv_ln_b, tp, tc, dc)
    kc = _pick_tile(nc, (11, 3, 2, 1))
    go, sta = _gla_fwd(u, gup, gate_bias, gla_norm_g, bsz, nc, kc)
    w_out, w1g, w2 = late_weights(go)
    nh = w1g.shape[0]
    tmm = _pick_tile(tp, (704, 128, 64))
    ns = 2
    p1, s1, s1b = _outproj_fwd(s0, co, go, w_out, ln1_g, ln1_b, tmm)
    hm, dp2, dpb, loss, dg2, db2 = _mlp_fwd(s1, s1b, w1g, w2, ln2_g, ln2_b, tgt_p, tp, tmm, ns)

    dh, dp1, dg1, db1 = _mlp_bwd_act(dp2, dpb, hm, w1g, w2, p1, ln1_g, tmm, ns)
    dw1, dw2 = _mlp_bwd_w(s1b, hm, dh, dpb, nh, tmm, ns)
    tok = push("ff", (dw1, dw2))
    dmi, dwo = _outproj_bwd(dp1, co, go, w_out, tok, tmm)
    tok = push("out", (dwo,))
    dqk, dv, dr, dgd, dgn, dgb, dgup = _gla_bwd(u, dmi, sta, gup, gate_bias, gla_norm_g, tok, bsz, nc, kc)
    dcv, dcw, dcb, dcg, dcbe = _conv_bwd(u, c, dmi, w32, conv_ln_g, conv_ln_b, tp, tc, dc)
    dus = [dcv, dqk, dv, dr, dgd]
    dwi = _inproj_bwd_w(s0b, dus, tmm)
    tok = push("in", (dwi,))
    gx, dgx, dbx = _inproj_bwd(dp1, dus, x2, ln_in_g, w_in, tok, tp, seq, tx)
    dmeta, dgh, dbh = _inproj_bwd(dp1, dus, head, ln_in_g, w_in, tok, tp, seq, 0)

    return dict(loss=loss[0, 0], grad_x=gx.reshape(bsz, seq, d), meta_tokens=dmeta, ln_in_g=dgx + dgh, ln_in_b=dbx + dbh,
                conv_w=dcw[:CONV_WIDTH], conv_b=dcb, conv_ln_g=dcg, conv_ln_b=dcbe,
                gate_up=dgup[:GLA_RANK], gate_bias=dgb, gla_norm_g=dgn, ln1_g=dg1, ln1_b=db1, ln2_g=dg2, ln2_b=db2)


def _exchange(arrays, scatter, name):
    na = len(arrays)
    npeer = N_DEV - 1

    def body(*refs):
        srcs = refs[:na]
        outs = refs[na:2 * na]
        send_sems, recv_sems, local_sems = refs[2 * na:]
        xi, yi, ci = (lax.axis_index(a) for a in MESH_AXES)
        me = 4 * xi + 2 * yi + ci
        copies = []
        for a in range(na):
            own = srcs[a].at[me] if scatter[a] else srcs[a]
            cp = pltpu.make_async_copy(own, outs[a].at[me], local_sems.at[a])
            cp.start()
            copies.append(cp)
        remote = []
        for k in range(1, N_DEV):
            px, py, pc = xi ^ (k >> 2), yi ^ ((k >> 1) & 1), ci ^ (k & 1)
            peer = 4 * px + 2 * py + pc
            for a in range(na):
                src = srcs[a].at[peer] if scatter[a] else srcs[a]
                cp = pltpu.make_async_remote_copy(
                    src_ref=src, dst_ref=outs[a].at[me],
                    send_sem=send_sems.at[a * npeer + k - 1], recv_sem=recv_sems.at[a * npeer + k - 1],
                    device_id=(px, py, pc), device_id_type=pl.DeviceIdType.MESH)
                cp.start()
                remote.append(cp)
        for cp in remote:
            cp.wait()
        for cp in copies:
            cp.wait()

    out_shape = [_sds(a.shape if scatter[i] else (N_DEV,) + a.shape, a.dtype) for i, a in enumerate(arrays)]
    anyspec = pl.BlockSpec(memory_space=pl.ANY)
    return pl.pallas_call(
        body, name=name,
        in_specs=[anyspec] * na, out_specs=[anyspec] * na, out_shape=out_shape,
        scratch_shapes=[pltpu.SemaphoreType.DMA((na * npeer,)), pltpu.SemaphoreType.DMA((na * npeer,)),
                        pltpu.SemaphoreType.DMA((na,))],
    )(*arrays)


def _peers(xi, yi, ci):
    for k in range(1, N_DEV):
        px, py, pc = xi ^ (k >> 2), yi ^ ((k >> 1) & 1), ci ^ (k & 1)
        yield (px, py, pc), 4 * px + 2 * py + pc


def _sc_exchange(arrays, scatter, name, collective_id, after=None):
    na = len(arrays)
    npeer = N_DEV - 1
    ndep = 0 if after is None else 1

    def body(*refs):
        srcs = refs[:na]
        outs = refs[na + ndep:2 * na + ndep]
        send_sems, recv_sems, own_sems = refs[2 * na + ndep:]
        xi, yi, ci = (lax.axis_index(a) for a in MESH_AXES)
        me = 4 * xi + 2 * yi + ci
        barrier = pltpu.get_barrier_semaphore()
        for pos, _ in _peers(xi, yi, ci):
            pl.semaphore_signal(barrier, inc=1, device_id=pos, device_id_type=pl.DeviceIdType.MESH)
        pl.semaphore_wait(barrier, npeer)
        own = [pltpu.make_async_copy(srcs[a].at[me] if scatter[a] else srcs[a], outs[a].at[me], own_sems.at[a])
               for a in range(na)]
        for cp in own:
            cp.start()
        remote = []
        for a in range(na):
            for k, (pos, peer) in enumerate(_peers(xi, yi, ci)):
                cp = pltpu.make_async_remote_copy(
                    src_ref=srcs[a].at[peer] if scatter[a] else srcs[a], dst_ref=outs[a].at[me],
                    send_sem=send_sems.at[a * npeer + k], recv_sem=recv_sems.at[a * npeer + k],
                    device_id=pos, device_id_type=pl.DeviceIdType.MESH)
                cp.start()
                remote.append(cp)
        for cp in own:
            cp.wait()
        for cp in remote:
            cp.wait()

    out_type = [_sds(a.shape if scatter[i] else (N_DEV,) + a.shape, a.dtype) for i, a in enumerate(arrays)]
    sent = sum(a.size * a.dtype.itemsize // (N_DEV if scatter[i] else 1) for i, a in enumerate(arrays))
    return pl.kernel(
        body, out_type=out_type, mesh=plsc.ScalarSubcoreMesh(axis_name="seq", num_cores=1), name=name,
        scratch_types=[pltpu.SemaphoreType.DMA((na * npeer,)), pltpu.SemaphoreType.DMA((na * npeer,)),
                       pltpu.SemaphoreType.DMA((na,))],
        compiler_params=pltpu.CompilerParams(collective_id=collective_id),
        cost_estimate=pl.CostEstimate(flops=0, transcendentals=0, bytes_accessed=2 * N_DEV * sent,
                                      remote_bytes_transferred=npeer * sent),
    )(*arrays, *([] if after is None else [after]))


def _sc_gather(arrays, name, collective_id, after=None):
    na = len(arrays)
    ndep = 0 if after is None else 1
    npair = N_DEV - 1

    def body(*refs):
        srcs = refs[:na]
        outs = refs[na + ndep:2 * na + ndep]
        send_sems, recv_sems, own_sems = refs[2 * na + ndep:]
        xi, yi, ci = (lax.axis_index(a) for a in MESH_AXES)
        me = 4 * xi + 2 * yi + ci
        sibling = (xi, yi, 1 - ci)
        chips = [(1 - xi, yi), (xi, 1 - yi), (1 - xi, 1 - yi)]
        barrier = pltpu.get_barrier_semaphore()
        for pos, _ in _peers(xi, yi, ci):
            pl.semaphore_signal(barrier, inc=1, device_id=pos, device_id_type=pl.DeviceIdType.MESH)
        pl.semaphore_wait(barrier, npair)

        def copy(a, k, src, slot, to):
            return pltpu.make_async_remote_copy(
                src_ref=src, dst_ref=outs[a].at[slot], send_sem=send_sems.at[a * npair + k],
                recv_sem=recv_sems.at[a * npair + k], device_id=to, device_id_type=pl.DeviceIdType.MESH)

        own = [pltpu.make_async_copy(srcs[a], outs[a].at[me], own_sems.at[a]) for a in range(na)]
        for cp in own:
            cp.start()
        sent = []
        for a in range(na):
            sent.append(copy(a, 0, srcs[a], me, sibling))
            sent += [copy(a, 1 + j, srcs[a], me, (*chip, ci)) for j, chip in enumerate(chips)]
        for cp in sent:
            cp.start()
        for j, (cx, cy) in enumerate(chips):
            slot = 4 * cx + 2 * cy + ci
            for a in range(na):
                copy(a, 1 + j, srcs[a], slot, sibling).wait_recv()
                cp = copy(a, 4 + j, outs[a].at[slot], slot, sibling)
                cp.start()
                sent.append(cp)
        for a in range(na):
            copy(a, 0, srcs[a], me, sibling).wait_recv()
            for j in range(len(chips)):
                copy(a, 4 + j, srcs[a], me, sibling).wait_recv()
        for cp in sent:
            cp.wait_send()
        for cp in own:
            cp.wait()

    out_type = [_sds((N_DEV,) + a.shape, a.dtype) for a in arrays]
    sent_bytes = sum(a.size * a.dtype.itemsize for a in arrays)
    return pl.kernel(
        body, out_type=out_type, mesh=plsc.ScalarSubcoreMesh(axis_name="seq", num_cores=1), name=name,
        scratch_types=[pltpu.SemaphoreType.DMA((na * npair,)), pltpu.SemaphoreType.DMA((na * npair,)),
                       pltpu.SemaphoreType.DMA((na,))],
        compiler_params=pltpu.CompilerParams(collective_id=collective_id),
        cost_estimate=pl.CostEstimate(flops=0, transcendentals=0, bytes_accessed=2 * N_DEV * sent_bytes,
                                      remote_bytes_transferred=npair * sent_bytes),
    )(*arrays, *([] if after is None else [after]))


def _push_start(arrays, scatter, name, dep=None):
    na = len(arrays)
    shapes = [a.shape if scatter[i] else (N_DEV,) + a.shape for i, a in enumerate(arrays)]
    hbm = pl.BlockSpec(memory_space=pltpu.HBM)
    sem = pl.BlockSpec(memory_space=pltpu.SEMAPHORE)
    ndep = 0 if dep is None else 1

    def body(*refs):
        srcs = refs[:na]
        lands = refs[na:2 * na]
        send_sems, recv_sems = refs[2 * na + ndep:2 * na + ndep + 2]
        own_sems = refs[4 * na + ndep + 2]
        xi, yi, ci = (lax.axis_index(a) for a in MESH_AXES)
        me = 4 * xi + 2 * yi + ci
        own = [pltpu.make_async_copy(srcs[a].at[me] if scatter[a] else srcs[a], lands[a].at[me], own_sems.at[a])
               for a in range(na)]
        for cp in own:
            cp.start()
        for cp in own:
            cp.wait()
        for a in range(na):
            for pos, peer in _peers(xi, yi, ci):
                pltpu.make_async_remote_copy(
                    src_ref=srcs[a].at[peer] if scatter[a] else srcs[a], dst_ref=lands[a].at[me],
                    send_sem=send_sems.at[a], recv_sem=recv_sems.at[a],
                    device_id=pos, device_id_type=pl.DeviceIdType.MESH).start()

    ins = [pltpu.with_memory_space_constraint(a, pltpu.HBM) for a in arrays]
    ins += [pltpu.with_memory_space_constraint(lax.empty(s, a.dtype), pltpu.HBM) for s, a in zip(shapes, arrays)]
    res = pl.pallas_call(
        body, name=name,
        in_specs=[hbm] * (2 * na) + [pl.BlockSpec(memory_space=pl.ANY)] * ndep,
        out_specs=[sem, sem] + [hbm] * (2 * na),
        out_shape=[pltpu.SemaphoreType.DMA((na,)), pltpu.SemaphoreType.DMA((na,))]
                  + [pltpu.HBM(a.shape, a.dtype) for a in arrays] + [pltpu.HBM(s, a.dtype) for s, a in zip(shapes, arrays)],
        input_output_aliases={i: 2 + i for i in range(2 * na)},
        scratch_shapes=[pltpu.SemaphoreType.DMA((na,))],
        compiler_params=pltpu.CompilerParams(has_side_effects=pltpu.SideEffectType.DATAFLOW_SIDE_EFFECTING),
    )(*ins, *([] if dep is None else [dep]))
    return (res[0], res[1], list(res[2:2 + na]), list(res[2 + na:2 + 2 * na])), res[2]


def _push_wait(handle, after, name):
    send_sems, recv_sems, srcs, lands = handle
    na = len(srcs)
    hbm = pl.BlockSpec(memory_space=pltpu.HBM)
    sem = pl.BlockSpec(memory_space=pltpu.SEMAPHORE)

    def body(*refs):
        land_refs = refs[na:2 * na]
        send_ref, recv_ref = refs[2 * na:2 * na + 2]
        me = tuple(lax.axis_index(a) for a in MESH_AXES)
        for a in range(na):
            seven = land_refs[a].at[pl.ds(0, N_DEV - 1)]
            cp = pltpu.make_async_remote_copy(src_ref=seven, dst_ref=seven, send_sem=send_ref.at[a], recv_sem=recv_ref.at[a],
                                              device_id=me, device_id_type=pl.DeviceIdType.MESH)
            cp.wait_send()
            cp.wait_recv()

    res = pl.pallas_call(
        body, name=name,
        in_specs=[hbm] * (2 * na) + [sem, sem, pl.BlockSpec(memory_space=pl.ANY)],
        out_specs=[hbm] * (2 * na),
        out_shape=[pltpu.HBM(a.shape, a.dtype) for a in srcs] + [pltpu.HBM(a.shape, a.dtype) for a in lands],
        input_output_aliases={i: i for i in range(2 * na)},
        compiler_params=pltpu.CompilerParams(has_side_effects=pltpu.SideEffectType.DATAFLOW_SIDE_EFFECTING),
    )(*srcs, *lands, send_sems, recv_sems, after)
    return list(res[na:])


def _adamw(w, g, m, v):
    m = ADAM_B1 * m + (1.0 - ADAM_B1) * g
    v = ADAM_B2 * v + (1.0 - ADAM_B2) * jnp.square(g)
    m_hat = m / (1.0 - ADAM_B1 ** ADAM_STEP)
    v_hat = v / (1.0 - ADAM_B2 ** ADAM_STEP)
    delta = -ADAM_LR * (m_hat / (jnp.sqrt(v_hat) + ADAM_EPS) + ADAM_WD * w)
    return delta, m, v


def _sum_devices(ref):
    g = ref[0].astype(F32)
    for k in range(1, N_DEV):
        g = g + ref[k].astype(F32)
    return g


def _update_big(parts, w, m, v, name):
    rows, cols = w.shape

    def body(p_ref, w_ref, m_ref, v_ref, g_ref, d_ref, nm_ref, nv_ref):
        g = _sum_devices(p_ref)
        g_ref[...] = g
        d_ref[...], nm_ref[...], nv_ref[...] = _adamw(w_ref[...], g, m_ref[...], v_ref[...])

    if rows % 16 == 0:
        tr = _pick_tile(rows, (128, 64, 16))
        steps, blk = rows // tr, pl.BlockSpec((tr, cols), lambda i: (i, 0))
        pblk = pl.BlockSpec((N_DEV, tr, cols), lambda i: (0, i, 0))
    else:
        tcol = 2 * LANES
        steps, blk = cols // tcol, pl.BlockSpec((rows, tcol), lambda i: (0, i))
        pblk = pl.BlockSpec((N_DEV, rows, tcol), lambda i: (0, 0, i))
    return pl.pallas_call(
        body, name=name, grid=(steps,),
        in_specs=[pblk, blk, blk, blk],
        out_specs=[blk] * 4, out_shape=[_sds((rows, cols), F32)] * 4,
        compiler_params=_params(("parallel",)),
    )(parts, w, m, v)


_VEC_ORDER = ("ln_in_g", "ln_in_b", "conv_b", "conv_ln_g", "conv_ln_b", "gate_bias", "gla_norm_g",
              "ln1_g", "ln1_b", "ln2_g", "ln2_b")
_SHARDED_SMALL = (("meta_tokens", 0, N_META, LANES), ("conv_w", N_META, CONV_WIDTH, None), ("gate_up", N_META + 32, GLA_RANK, None))


def _update_small(parts_sh, parts_vec, wmv):
    names = [s[0] for s in _SHARDED_SMALL] + list(_VEC_ORDER)
    flat = [a for nme in names for a in wmv[nme]]
    nv = len(_VEC_ORDER)

    def body(*refs):
        sh_ref, vec_ref = refs[0], refs[1]
        ins = refs[2:2 + len(flat)]
        outs = refs[2 + len(flat):2 + len(flat) + 4 * len(names)]
        loss_ref = refs[2 + len(flat) + 4 * len(names)]
        gsh_ref, gvec_ref = refs[-2:]
        gsh_ref[...] = _sum_devices(sh_ref)
        gvec_ref[...] = _sum_devices(vec_ref)
        loss_ref[...] = gvec_ref[nv:nv + 1, :]
        for idx, nme in enumerate(names):
            w_ref, m_ref, v_ref = ins[3 * idx:3 * idx + 3]
            rows, cols = w_ref.shape
            if idx < len(_SHARDED_SMALL):
                r0 = _SHARDED_SMALL[idx][1]
                g = gsh_ref[r0:r0 + rows, 0:cols]
            else:
                j = idx - len(_SHARDED_SMALL)
                g = gvec_ref[j:j + 1, 0:cols]
            o = outs[4 * idx:4 * idx + 4]
            o[0][...] = g
            o[1][...], o[2][...], o[3][...] = _adamw(w_ref[...], g, m_ref[...], v_ref[...])

    out_shape = [_sds(wmv[nme][0].shape, F32) for nme in names for _ in range(4)] + [_sds((1, parts_vec.shape[2]), F32)]
    vmem = pl.BlockSpec(memory_space=pltpu.VMEM)
    res = pl.pallas_call(
        body, name="update_small", out_shape=out_shape,
        in_specs=[vmem] * (2 + len(flat)), out_specs=[vmem] * len(out_shape),
        scratch_shapes=[pltpu.VMEM(parts_sh.shape[1:], F32), pltpu.VMEM(parts_vec.shape[1:], F32)],
    )(parts_sh, parts_vec, *flat)
    return {nme: res[4 * i:4 * i + 4] for i, nme in enumerate(names)}, res[-1][0, 0]


_WEIGHTS = ("meta_tokens", "ln_in_g", "ln_in_b", "w_in", "conv_w", "conv_b", "conv_ln_g", "conv_ln_b", "gate_up",
            "gate_bias", "gla_norm_g", "w_out", "ln1_g", "ln1_b", "w_ff1", "w_ff2", "ln2_g", "ln2_b")


def kernel(x, meta_tokens, ln_in_g, ln_in_b, w_in, conv_w, conv_b, conv_ln_g, conv_ln_b, gate_up, gate_bias, gla_norm_g, w_out, ln1_g, ln1_b, w_ff1, w_ff2, ln2_g, ln2_b, loss_target, m_meta_tokens, m_ln_in_g, m_ln_in_b, m_w_in, m_conv_w, m_conv_b, m_conv_ln_g, m_conv_ln_b, m_gate_up, m_gate_bias, m_gla_norm_g, m_w_out, m_ln1_g, m_ln1_b, m_w_ff1, m_w_ff2, m_ln2_g, m_ln2_b, v_meta_tokens, v_ln_in_g, v_ln_in_b, v_w_in, v_conv_w, v_conv_b, v_conv_ln_g, v_conv_ln_b, v_gate_up, v_gate_bias, v_gla_norm_g, v_w_out, v_ln1_g, v_ln1_b, v_w_ff1, v_w_ff2, v_ln2_g, v_ln2_b):
    w = dict(meta_tokens=meta_tokens, ln_in_g=ln_in_g, ln_in_b=ln_in_b, w_in=w_in, conv_w=conv_w, conv_b=conv_b,
             conv_ln_g=conv_ln_g, conv_ln_b=conv_ln_b, gate_up=gate_up, gate_bias=gate_bias, gla_norm_g=gla_norm_g,
             w_out=w_out, ln1_g=ln1_g, ln1_b=ln1_b, w_ff1=w_ff1, w_ff2=w_ff2, ln2_g=ln2_g, ln2_b=ln2_b)
    mom = dict(meta_tokens=m_meta_tokens, ln_in_g=m_ln_in_g, ln_in_b=m_ln_in_b, w_in=m_w_in, conv_w=m_conv_w,
               conv_b=m_conv_b, conv_ln_g=m_conv_ln_g, conv_ln_b=m_conv_ln_b, gate_up=m_gate_up, gate_bias=m_gate_bias,
               gla_norm_g=m_gla_norm_g, w_out=m_w_out, ln1_g=m_ln1_g, ln1_b=m_ln1_b, w_ff1=m_w_ff1, w_ff2=m_w_ff2,
               ln2_g=m_ln2_g, ln2_b=m_ln2_b)
    var = dict(meta_tokens=v_meta_tokens, ln_in_g=v_ln_in_g, ln_in_b=v_ln_in_b, w_in=v_w_in, conv_w=v_conv_w,
               conv_b=v_conv_b, conv_ln_g=v_conv_ln_g, conv_ln_b=v_conv_ln_b, gate_up=v_gate_up, gate_bias=v_gate_bias,
               gla_norm_g=v_gla_norm_g, w_out=v_w_out, ln1_g=v_ln1_g, ln1_b=v_ln1_b, w_ff1=v_w_ff1, w_ff2=v_w_ff2,
               ln2_g=v_ln2_g, ln2_b=v_ln2_b)
    shapes = {k: a.shape for k, a in w.items()}

    def two_d(a):
        return a.reshape(1, -1) if a.ndim == 1 else a.reshape(a.shape[-2:])

    w2d = {k: two_d(a) for k, a in w.items()}
    m2d = {k: two_d(a) for k, a in mom.items()}
    v2d = {k: two_d(a) for k, a in var.items()}
    d = x.shape[-1]
    d_in = w2d["w_in"].shape[1] * N_DEV
    d_in_p = -(-d_in // LANES) * LANES

    for dct in (w2d, m2d, v2d):
        dct["w_in"] = dct["w_in"].T
    g_in, g_meta, g_conv, g_gup = _sc_gather(
        [w2d["w_in"].astype(BF16), w2d["meta_tokens"], w2d["conv_w"], w2d["gate_up"]], "gather_first", 0)
    g_out, g_ff1, g_ff2 = _sc_gather(
        [w2d["w_out"].astype(BF16), w2d["w_ff1"].astype(BF16), w2d["w_ff2"].astype(BF16)], "gather_late", 1)
    w_in_full = jnp.pad(g_in.reshape(d_in, d), ((0, d_in_p - d_in), (0, 0)))
    meta_full = g_meta.transpose(1, 0, 2).reshape(N_META, d)
    conv_w_full = g_conv.transpose(1, 0, 2).reshape(CONV_WIDTH, -1)
    gate_up_full = g_gup.transpose(1, 0, 2).reshape(GLA_RANK, -1)

    def late_weights(after):
        return g_out.reshape(-1, d), g_ff1, g_ff2.reshape(-1, d)

    pushed = {}

    def push(tag, grads):
        if tag == "ff":
            pushed["ff1"], pushed["ff2"] = _sc_exchange(list(grads), [True, True], "scatter_ff", 2)
        elif tag == "out":
            pushed["p_out"] = grads[0].reshape(N_DEV, -1, d)
        else:
            p_in = grads[0][:d_in].reshape(N_DEV, d_in // N_DEV, d)
            pushed["in"], pushed["out"] = _sc_exchange([p_in, pushed["p_out"]], [True, True], "scatter_rest", 3,
                                                       after=pushed["ff1"])
        return grads[0]

    res = _local_step(x, loss_target, meta_full, w2d["ln_in_g"], w2d["ln_in_b"], w_in_full, conv_w_full, w2d["conv_b"],
                      w2d["conv_ln_g"], w2d["conv_ln_b"], gate_up_full, w2d["gate_bias"], w2d["gla_norm_g"], late_weights,
                      w2d["ln1_g"], w2d["ln1_b"], w2d["ln2_g"], w2d["ln2_b"], push)

    dc = res["conv_w"].shape[1]
    hk = res["gate_up"].shape[1]
    sh_meta = res["meta_tokens"].reshape(N_META, N_DEV, LANES).transpose(1, 0, 2)
    sh_conv = jnp.pad(res["conv_w"].reshape(CONV_WIDTH, N_DEV, dc // N_DEV).transpose(1, 0, 2),
                      ((0, 0), (0, 32 - CONV_WIDTH), (0, LANES - dc // N_DEV)))
    sh_gup = jnp.pad(res["gate_up"].reshape(GLA_RANK, N_DEV, hk // N_DEV).transpose(1, 0, 2),
                     ((0, 0), (0, 0), (0, LANES - hk // N_DEV)))
    p_sh = jnp.concatenate([sh_meta, sh_conv, sh_gup], axis=1)
    p_vec = jnp.concatenate([jnp.pad(res[k], ((0, 0), (0, d - res[k].shape[1]))) for k in _VEC_ORDER]
                            + [jnp.full((1, d), res["loss"], F32), jnp.zeros((15 - len(_VEC_ORDER), d), F32)], axis=0)

    r_sh, r_vec = _exchange([p_sh, p_vec], [True, False], "scatter_small")
    r_ff1, r_ff2, r_out, r_in = pushed["ff1"], pushed["ff2"], pushed["out"], pushed["in"]

    upd = {}
    upd["w_in"] = [a.T for a in _update_big(r_in, w2d["w_in"], m2d["w_in"], v2d["w_in"], "update_w_in")]
    upd["w_out"] = _update_big(r_out, w2d["w_out"], m2d["w_out"], v2d["w_out"], "update_w_out")
    upd["w_ff1"] = _update_big(r_ff1, w2d["w_ff1"], m2d["w_ff1"], v2d["w_ff1"], "update_w_ff1")
    upd["w_ff2"] = _update_big(r_ff2, w2d["w_ff2"], m2d["w_ff2"], v2d["w_ff2"], "update_w_ff2")
    small = [s[0] for s in _SHARDED_SMALL] + list(_VEC_ORDER)
    upd_small, loss = _update_small(r_sh, r_vec, {k: (w2d[k], m2d[k], v2d[k]) for k in small})
    upd.update(upd_small)

    outs = [loss, res["grad_x"]]
    for j in range(4):
        outs += [upd[k][j].reshape(shapes[k]) for k in _WEIGHTS]
    return tuple(outs)
```

```python
import jax
import jax.numpy as jnp
from jax import lax
from jax.experimental import pallas as pl
from jax.experimental.pallas import tpu as pltpu
from jax.experimental.pallas import tpu_sc as plsc

F32 = jnp.float32
BF16 = jnp.bfloat16

N_META = 16
CHUNK = 64
PAD_FRONT = (-N_META) % CHUNK
X_OFF = PAD_FRONT + N_META
CONV_WIDTH = 31
CONV_HALO = 32
CONV_SUB = 64
CONV_WIN = CONV_SUB + CONV_HALO
GLA_HEADS = 4
GLA_DK = 64
GLA_DV = 128
GLA_RANK = 16
GLA_TAU = 16.0
QK_SCALE = GLA_DK ** -0.5
LN_EPS = 1e-5
ALPHA = 2.0 ** 0.25
LANES = 128
N_DEV = 8
ADAM_LR = 0.001
ADAM_B1 = 0.9
ADAM_B2 = 0.999
ADAM_EPS = 1e-08
ADAM_WD = 0.01
ADAM_STEP = 10
VMEM_LIMIT = 56 * 1024 * 1024
MESH_AXES = ("x", "y", "c")
U_QK, U_V, U_R, U_GD = 2, 3, 4, 20
QK_WIDTH = 2 * GLA_HEADS * GLA_DK
GUP_WIDTH = GLA_HEADS * GLA_DK
DMI_GLA = 1


def _sds(shape, dtype):
    return jax.ShapeDtypeStruct(shape, dtype)


def _mm(a, b):
    return jnp.dot(a, b, preferred_element_type=F32)


def _mm_nt(a, b):
    return lax.dot_general(a, b, (((1,), (1,)), ((), ())), preferred_element_type=F32)


def _mm_tn(a, b):
    return lax.dot_general(a, b, (((0,), (0,)), ((), ())), preferred_element_type=F32)


def _sigmoid(x):
    return 1.0 / (1.0 + jnp.exp(-x))


def _log_sigmoid(z):
    return jnp.minimum(z, 0.0) - jnp.log(1.0 + jnp.exp(-jnp.abs(z)))


def _ln(x):
    mu = jnp.mean(x, axis=-1, keepdims=True)
    xc = x - mu
    var = jnp.mean(xc * xc, axis=-1, keepdims=True)
    rstd = lax.rsqrt(var + LN_EPS)
    return xc * rstd, rstd


def _ln_bwd(dyg, xhat, rstd):
    m1 = jnp.mean(dyg, axis=-1, keepdims=True)
    m2 = jnp.mean(dyg * xhat, axis=-1, keepdims=True)
    return rstd * (dyg - m1 - xhat * m2)


def _rowsum(x):
    return jnp.sum(x, axis=0, keepdims=True)


def _row_in_seq(i, tm, tp):
    base = lax.rem(i * tm, tp)
    return base + lax.broadcasted_iota(jnp.int32, (tm, 1), 0)


def _split3(x):
    hi = x.astype(BF16)
    r1 = x - hi.astype(F32)
    mid = r1.astype(BF16)
    lo = (r1 - mid.astype(F32)).astype(BF16)
    return hi, mid, lo


def _params(sem):
    return pltpu.CompilerParams(dimension_semantics=sem, vmem_limit_bytes=VMEM_LIMIT)


def _pick_tile(n, prefs):
    for t in prefs:
        if n % t == 0:
            return t
    raise ValueError(f"no tile for {n}")


def _x_tile_row(tp, seq, tx):
    tps = seq // tx
    return lambda i: pl.multiple_of((i // tps) * tp + X_OFF + (i % tps) * tx, CHUNK)


def _ln_in_x(x2, g, b, tp, seq, tx):
    rx, d = x2.shape
    r = rx // seq * tp
    row = _x_tile_row(tp, seq, tx)

    def body(x_ref, g_ref, b_ref, s0_ref, sb_ref):
        xhat, _ = _ln(x_ref[...])
        s = xhat * g_ref[...] + b_ref[...]
        s0_ref[...] = s
        sb_ref[...] = s.astype(BF16)

    out = pl.BlockSpec((pl.Element(tx), pl.Element(d)), lambda i: (row(i), 0))
    return pl.pallas_call(
        body, name="ln_in_x", grid=(rx // tx,),
        in_specs=[pl.BlockSpec((tx, d), lambda i: (i, 0)), pl.BlockSpec((1, d), lambda i: (0, 0)),
                  pl.BlockSpec((1, d), lambda i: (0, 0))],
        out_specs=[out, out],
        out_shape=[_sds((r, d), F32), _sds((r, d), BF16)],
        compiler_params=_params(("parallel",)),
    )(x2, g, b)


def _ln_in_head(head, g, b, s0, s0b, tp):
    r, d = s0.shape
    nb = tp // X_OFF

    def body(h_ref, g_ref, b_ref, s0_in, sb_in, s0_ref, sb_ref):
        xhat, _ = _ln(h_ref[...])
        real = lax.broadcasted_iota(jnp.int32, (X_OFF, 1), 0) >= PAD_FRONT
        s = jnp.where(real, xhat * g_ref[...] + b_ref[...], 0.0)
        s0_ref[...] = s
        sb_ref[...] = s.astype(BF16)

    anyspec = pl.BlockSpec(memory_space=pl.ANY)
    out = pl.BlockSpec((X_OFF, d), lambda i: (i * nb, 0))
    return pl.pallas_call(
        body, name="ln_in_head", grid=(r // tp,),
        in_specs=[pl.BlockSpec((X_OFF, d), lambda i: (0, 0)), pl.BlockSpec((1, d), lambda i: (0, 0)),
                  pl.BlockSpec((1, d), lambda i: (0, 0)), anyspec, anyspec],
        out_specs=[out, out],
        out_shape=[_sds((r, d), F32), _sds((r, d), BF16)],
        input_output_aliases={3: 0, 4: 1},
        compiler_params=_params(("parallel",)),
    )(head, g, b, s0, s0b)


def _inproj_fwd(s0b, w_in, tm):
    r, d = s0b.shape
    n = w_in.shape[0]

    def body(s_ref, w_ref, u_ref):
        u_ref[...] = _mm_nt(s_ref[...], w_ref[...])

    return pl.pallas_call(
        body, name="inproj_fwd", grid=(r // tm,),
        in_specs=[pl.BlockSpec((tm, d), lambda i: (i, 0)), pl.BlockSpec((n, d), lambda i: (0, 0))],
        out_specs=pl.BlockSpec((tm, n), lambda i: (i, 0)),
        out_shape=_sds((r, n), F32),
        compiler_params=_params(("parallel",)),
    )(s0b, w_in)


def _conv_taps(win, coef, lo):
    acc = None
    for rho in range(8):
        offs = [o for o in range(lo, lo + CONV_WIDTH) if o % 8 == rho]
        if not offs:
            continue
        rolled = win if rho == 0 else pltpu.roll(win, CONV_WIN - rho, 0)
        for o in offs:
            m8 = o - rho
            term = rolled[m8:m8 + CONV_SUB, :] * coef(o)
            acc = term if acc is None else acc + term
    return acc


def _conv_fwd(u, w32, cb, cg, cbe, tp, tc, dc):
    r = u.shape[0]
    hb = tc // CONV_HALO

    def body(a_ref, g_ref, ah_ref, gh_ref, w_ref, cb_ref, cg_ref, cbe_ref, c_ref, co_ref, hs_ref):
        t = pl.program_id(0)
        first = lax.rem(t * tc, tp) == 0
        hh = ah_ref[...] * _sigmoid(gh_ref[...])
        hs_ref[0:CONV_HALO, :] = jnp.where(first, 0.0, hh)
        hs_ref[CONV_HALO:CONV_HALO + tc, :] = a_ref[...] * _sigmoid(g_ref[...])

        def sub(k, carry):
            r0 = pl.multiple_of(k * CONV_SUB, CONV_SUB)
            win = hs_ref[pl.ds(r0, CONV_WIN), :]
            c = _conv_taps(win, lambda o: w_ref[o - 2:o - 1, :], 2) + cb_ref[...]
            c_ref[pl.ds(r0, CONV_SUB), :] = c
            xhat, _ = _ln(c)
            cn = xhat * cg_ref[...] + cbe_ref[...]
            co_ref[pl.ds(r0, CONV_SUB), :] = (cn * _sigmoid(cn)).astype(BF16)
            return carry

        lax.fori_loop(0, tc // CONV_SUB, sub, 0)

    vec = pl.BlockSpec((1, dc), lambda t: (0, 0))
    return pl.pallas_call(
        body, name="conv_fwd", grid=(r // tc,),
        in_specs=[pl.BlockSpec((tc, dc), lambda t: (t, 0)), pl.BlockSpec((tc, dc), lambda t: (t, 1)),
                  pl.BlockSpec((CONV_HALO, dc), lambda t: (jnp.maximum(t * hb - 1, 0), 0)),
                  pl.BlockSpec((CONV_HALO, dc), lambda t: (jnp.maximum(t * hb - 1, 0), 1)),
                  pl.BlockSpec((32, dc), lambda t: (0, 0)), vec, vec, vec],
        out_specs=[pl.BlockSpec((tc, dc), lambda t: (t, 0)), pl.BlockSpec((tc, dc), lambda t: (t, 0))],
        out_shape=[_sds((r, dc), F32), _sds((r, dc), BF16)],
        scratch_shapes=[pltpu.VMEM((CONV_HALO + tc, dc), F32)],
        compiler_params=_params(("parallel",)),
    )(u, u, u, u, w32, cb, cg, cbe)


def _tri_mm_all(tri, xs):
    parts = [_split3(x) for x in xs]
    acc = [None] * len(xs)
    for t in range(3):
        for j in range(len(xs)):
            term = _mm(tri, parts[j][t])
            acc[j] = term if t == 0 else acc[j] + term
    return acc


def _gla_prep(qk_ref, gd_ref, gup, gb, n0, kc):
    rows = [slice(j * CHUNK, (j + 1) * CHUNK) for j in range(kc)]
    ri = lax.broadcasted_iota(jnp.int32, (CHUNK, CHUNK), 0)
    ci = lax.broadcasted_iota(jnp.int32, (CHUNK, CHUNK), 1)
    low = (ri >= ci).astype(BF16)
    hk = GLA_HEADS * GLA_DK
    gds = [gd_ref[rw, :] for rw in rows]
    zs = [_mm(g.astype(BF16), gup) + gb for g in gds]
    reals = [(n0 + j) * CHUNK + lax.broadcasted_iota(jnp.int32, (CHUNK, 1), 0) >= PAD_FRONT for j in range(kc)]
    lgs = [jnp.where(reals[j], _log_sigmoid(zs[j]) * (1.0 / GLA_TAU), 0.0) for j in range(kc)]
    bs = _tri_mm_all(low, lgs)
    out = []
    for j in range(kc):
        b, bl = bs[j], _rowsum(lgs[j])
        q = qk_ref[rows[j], :hk] * QK_SCALE
        k = qk_ref[rows[j], hk:]
        eb, enb, ebl = jnp.exp(b), jnp.exp(-b), jnp.exp(bl - b)
        out.append(dict(rows=rows[j], gd=gds[j], z=zs[j], real=reals[j], eb=eb, enb=enb, ebl=ebl, gam=jnp.exp(bl),
                        qe=q * eb, ke=k * enb, kd=k * ebl))
    return out, ri, ci


def _gla_heads(p, v_ref):
    ops = []
    for h in range(GLA_HEADS):
        hp, h2 = divmod(h, 2)
        ls = slice(hp * LANES, (hp + 1) * LANES)
        m = _head_mask(h2)
        ops.append(dict(ls=ls, m=m, vs=slice(h * GLA_DV, (h + 1) * GLA_DV),
                        qe=jnp.where(m, p["qe"][:, ls], 0.0).astype(BF16),
                        kd=jnp.where(m, p["kd"][:, ls], 0.0).astype(BF16),
                        ke=p["ke"][:, ls].astype(BF16),
                        v=v_ref[p["rows"], h * GLA_DV:(h + 1) * GLA_DV].astype(BF16)))
    return ops


def _head_mask(h2):
    lane = lax.broadcasted_iota(jnp.int32, (1, LANES), 1)
    return (lane < GLA_DK) if h2 == 0 else (lane >= GLA_DK)


def _gla_fwd(u, gup, gb, gn, bsz, nc, kc):
    r = u.shape[0]
    hv = GLA_HEADS * GLA_DV
    ns = nc // kc

    def body(qk_ref, v_ref, r_ref, gd_ref, gup_ref, gb_ref, gn_ref, go_ref, sta_ref, st_ref):
        t = pl.program_id(1)

        @pl.when(t == 0)
        def _():
            st_ref[...] = jnp.zeros_like(st_ref)

        ps, ri, ci = _gla_prep(qk_ref, gd_ref, gup_ref[...], gb_ref[...], t * kc, kc)
        tril = ri >= ci
        items = [(j, h) for j in range(kc) for h in range(GLA_HEADS)]
        ops = [_gla_heads(p, v_ref) for p in ps]
        a = {jh: jnp.where(tril, _mm_nt(ops[jh[0]][jh[1]]["qe"], ops[jh[0]][jh[1]]["ke"]), 0.0).astype(BF16) for jh in items}
        oi = {jh: _mm(a[jh], ops[jh[0]][jh[1]]["v"]) for jh in items}
        inc = {jh: _mm_tn(ops[jh[0]][jh[1]]["v"], ops[jh[0]][jh[1]]["kd"]) for jh in items}
        sts = [st_ref[h] for h in range(GLA_HEADS)]
        for j, h in items:
            op, p = ops[j][h], ps[j]
            st = sts[h]
            sta_ref[j, h] = st
            o = oi[j, h] + _mm_nt(op["qe"], st.astype(BF16))
            sts[h] = st * p["gam"][:, op["ls"]] + inc[j, h]
            rs = lax.rsqrt(jnp.mean(o * o, axis=-1, keepdims=True) + LN_EPS)
            rr = r_ref[p["rows"], op["vs"]]
            go_ref[p["rows"], op["vs"]] = (o * rs * gn_ref[...] * (rr * _sigmoid(rr))).astype(BF16)
        for h in range(GLA_HEADS):
            st_ref[h] = sts[h]

    rowblk = lambda col: (lambda b, t: (b * ns + t, col))
    const = lambda b, t: (0, 0)
    return pl.pallas_call(
        body, name="gla_fwd", grid=(bsz, ns),
        in_specs=[pl.BlockSpec((kc * CHUNK, QK_WIDTH), rowblk(U_QK)), pl.BlockSpec((kc * CHUNK, hv), rowblk(U_V)),
                  pl.BlockSpec((kc * CHUNK, hv), rowblk(U_R)), pl.BlockSpec((kc * CHUNK, LANES), rowblk(U_GD)),
                  pl.BlockSpec((LANES, GUP_WIDTH), const), pl.BlockSpec((1, GUP_WIDTH), const), pl.BlockSpec((1, GLA_DV), const)],
        out_specs=[pl.BlockSpec((kc * CHUNK, hv), rowblk(0)),
                   pl.BlockSpec((kc, GLA_HEADS, LANES, LANES), lambda b, t: (b * ns + t, 0, 0, 0))],
        out_shape=[_sds((r, hv), BF16), _sds((bsz * nc, GLA_HEADS, LANES, LANES), F32)],
        scratch_shapes=[pltpu.VMEM((GLA_HEADS, LANES, LANES), F32)],
        compiler_params=_params(("parallel", "arbitrary")),
    )(u, u, u, u, gup, gb, gn)


def _outproj_fwd(s0, co, go, w_out, g1, b1, tm):
    r, d = s0.shape
    dc = co.shape[1]

    def body(s0_ref, co_ref, go_ref, w_ref, g_ref, b_ref, p1_ref, s1_ref, s1b_ref):
        nb = 4 if tm % 64 == 0 else 1
        blocks = [slice(k * (tm // nb), (k + 1) * (tm // nb)) for k in range(nb)]
        mixes = [_mm(co_ref[rows, :], w_ref[0:dc, :]) + _mm(go_ref[rows, :], w_ref[dc:2 * dc, :]) for rows in blocks]
        for rows, mix in zip(blocks, mixes):
            p1 = ALPHA * s0_ref[rows, :] + mix
            p1_ref[rows, :] = p1
            xhat, _ = _ln(p1)
            s1 = xhat * g_ref[...] + b_ref[...]
            s1_ref[rows, :] = s1
            s1b_ref[rows, :] = s1.astype(BF16)

    row = lambda w: pl.BlockSpec((tm, w), lambda i: (i, 0))
    vec = pl.BlockSpec((1, d), lambda i: (0, 0))
    return pl.pallas_call(
        body, name="outproj_fwd", grid=(r // tm,),
        in_specs=[row(d), row(dc), row(dc), pl.BlockSpec((2 * dc, d), lambda i: (0, 0)), vec, vec],
        out_specs=[row(d), row(d), row(d)],
        out_shape=[_sds((r, d), F32), _sds((r, d), F32), _sds((r, d), BF16)],
        compiler_params=_params(("parallel",)),
    )(s0, co, go, w_out, g1, b1)


def _mlp_fwd(s1, s1b, w1g, w2, g2, b2, tgt, tp, tm, ns):
    r, d = s1.shape
    nh, _, th = w1g.shape
    nj = nh // ns

    def body(s1_ref, sb_ref, w1_ref, w2_ref, g_ref, b_ref, t_ref, hm_ref, dp2_ref, dpb_ref, loss_ref, dg_ref, db_ref, acc_ref):
        i = pl.program_id(0)
        j = pl.program_id(1)

        @pl.when(jnp.logical_and(i == 0, j == 0))
        def _():
            loss_ref[...] = jnp.zeros_like(loss_ref)
            dg_ref[...] = jnp.zeros_like(dg_ref)
            db_ref[...] = jnp.zeros_like(db_ref)

        @pl.when(j == 0)
        def _():
            acc_ref[...] = jnp.zeros_like(acc_ref)

        def mlp_rows(rows):
            hs = [_mm(sb_ref[rows, :], w1_ref[s]) for s in range(ns)]
            acc = acc_ref[rows, :]
            for s in range(ns):
                hm_ref[rows, s * th:(s + 1) * th] = hs[s].astype(BF16)
                act = jnp.square(jnp.maximum(hs[s], 0.0))
                acc = acc + _mm(act.astype(BF16), w2_ref[s * th:(s + 1) * th, :])
            return acc

        @pl.when(j < nj - 1)
        def _():
            acc_ref[...] = mlp_rows(slice(None))

        @pl.when(j == nj - 1)
        def _():
            halves = [slice(0, tm // 2), slice(tm // 2, tm)]
            accs = [mlp_rows(rows) for rows in halves]
            isx = _row_in_seq(i, tm, tp) >= X_OFF
            tg = t_ref[...]
            tg = jnp.where(i == 0, pltpu.roll(tg, X_OFF, 0), tg)
            for rows, acc in zip(halves, accs):
                p2 = ALPHA * s1_ref[rows, :] + acc
                xhat, rstd = _ln(p2)
                s2 = xhat * g_ref[...] + b_ref[...]
                err = jnp.where(isx[rows], s2 - tg[rows], 0.0)
                loss_ref[...] += 0.5 * jnp.sum(jnp.mean(err * err, axis=-1, keepdims=True))
                dy = err * (1.0 / d)
                dg_ref[...] += _rowsum(dy * xhat)
                db_ref[...] += _rowsum(dy)
                dp2 = _ln_bwd(dy * g_ref[...], xhat, rstd)
                dp2_ref[rows, :] = dp2
                dpb_ref[rows, :] = dp2.astype(BF16)

    row = pl.BlockSpec((tm, d), lambda i, j: (i, 0))
    vec = pl.BlockSpec((1, d), lambda i, j: (0, 0))
    tgt_row = pl.BlockSpec((pl.Element(tm), pl.Element(d)),
                           lambda i, j: (pl.multiple_of(jnp.maximum(i * tm - X_OFF * ((i * tm) // tp + 1), 0), CHUNK), 0))
    return pl.pallas_call(
        body, name="mlp_fwd", grid=(r // tm, nj),
        in_specs=[row, row, pl.BlockSpec((ns, d, th), lambda i, j: (j, 0, 0)), pl.BlockSpec((ns * th, d), lambda i, j: (j, 0)),
                  vec, vec, tgt_row],
        out_specs=[pl.BlockSpec((tm, ns * th), lambda i, j: (i, j)), row, row,
                   pl.BlockSpec((8, LANES), lambda i, j: (0, 0)), vec, vec],
        out_shape=[_sds((r, nh * th), BF16), _sds((r, d), F32), _sds((r, d), BF16), _sds((8, LANES), F32),
                   _sds((1, d), F32), _sds((1, d), F32)],
        scratch_shapes=[pltpu.VMEM((tm, d), F32)],
        compiler_params=_params(("arbitrary", "arbitrary")),
    )(s1, s1b, w1g, w2, g2, b2, tgt)


def _mlp_bwd_act(dp2, dpb, hm, w1g, w2, p1, g1, tm, ns):
    r, d = dp2.shape
    nh, _, th = w1g.shape
    nj = nh // ns

    def body(dp2_ref, dpb_ref, hm_ref, w1_ref, w2_ref, p1_ref, g_ref, dh_ref, dp1_ref, dg_ref, db_ref, acc_ref):
        i = pl.program_id(0)
        j = pl.program_id(1)

        @pl.when(jnp.logical_and(i == 0, j == 0))
        def _():
            dg_ref[...] = jnp.zeros_like(dg_ref)
            db_ref[...] = jnp.zeros_like(db_ref)

        @pl.when(j == 0)
        def _():
            acc_ref[...] = jnp.zeros_like(acc_ref)

        def mlp_rows(rows):
            dacts = [_mm_nt(dpb_ref[rows, :], w2_ref[s * th:(s + 1) * th, :]) for s in range(ns)]
            acc = acc_ref[rows, :]
            for s in range(ns):
                cols = slice(s * th, (s + 1) * th)
                dh = (dacts[s] * (2.0 * jnp.maximum(hm_ref[rows, cols].astype(F32), 0.0))).astype(BF16)
                dh_ref[rows, cols] = dh
                acc = acc + _mm_nt(dh, w1_ref[s])
            return acc

        @pl.when(j < nj - 1)
        def _():
            acc_ref[...] = mlp_rows(slice(None))

        @pl.when(j == nj - 1)
        def _():
            halves = [slice(0, tm // 2), slice(tm // 2, tm)]
            accs = [mlp_rows(rows) for rows in halves]
            for rows, acc in zip(halves, accs):
                ds1 = ALPHA * dp2_ref[rows, :] + acc
                xhat, rstd = _ln(p1_ref[rows, :])
                dg_ref[...] += _rowsum(ds1 * xhat)
                db_ref[...] += _rowsum(ds1)
                dp1_ref[rows, :] = _ln_bwd(ds1 * g_ref[...], xhat, rstd)

    row = pl.BlockSpec((tm, d), lambda i, j: (i, 0))
    vec = pl.BlockSpec((1, d), lambda i, j: (0, 0))
    blk = pl.BlockSpec((tm, ns * th), lambda i, j: (i, j))
    return pl.pallas_call(
        body, name="mlp_bwd_act", grid=(r // tm, nj),
        in_specs=[row, row, blk, pl.BlockSpec((ns, d, th), lambda i, j: (j, 0, 0)),
                  pl.BlockSpec((ns * th, d), lambda i, j: (j, 0)), row, vec],
        out_specs=[blk, row, vec, vec],
        out_shape=[_sds((r, nh * th), BF16), _sds((r, d), F32), _sds((1, d), F32), _sds((1, d), F32)],
        scratch_shapes=[pltpu.VMEM((tm, d), F32)],
        compiler_params=_params(("arbitrary", "arbitrary")),
    )(dp2, dpb, hm, w1g, w2, p1, g1)


def _mlp_bwd_w(s1b, hm, dh, dpb, nh, tm, ns):
    r, d = s1b.shape
    th = hm.shape[1] // nh

    def body(s1_ref, hm_ref, dh_ref, dp2_ref, dw1_ref, dw2_ref, a1_ref, a2_ref):
        i = pl.program_id(1)

        @pl.when(i == 0)
        def _():
            a1_ref[...] = jnp.zeros_like(a1_ref)
            a2_ref[...] = jnp.zeros_like(a2_ref)

        for s in range(ns):
            a1_ref[s] += _mm_tn(s1_ref[...], dh_ref[:, s * th:(s + 1) * th])
        for s in range(ns):
            act = jnp.square(jnp.maximum(hm_ref[:, s * th:(s + 1) * th].astype(F32), 0.0)).astype(BF16)
            a2_ref[s] += _mm_tn(act, dp2_ref[...])

        @pl.when(i == pl.num_programs(1) - 1)
        def _():
            dw1_ref[...] = a1_ref[...].astype(BF16)
            dw2_ref[...] = a2_ref[...].astype(BF16)

    row = pl.BlockSpec((tm, d), lambda j, i: (i, 0))
    blk = pl.BlockSpec((tm, ns * th), lambda j, i: (i, j))
    return pl.pallas_call(
        body, name="mlp_bwd_w", grid=(nh // ns, r // tm),
        in_specs=[row, blk, blk, row],
        out_specs=[pl.BlockSpec((ns, d, th), lambda j, i: (j, 0, 0)), pl.BlockSpec((ns, th, d), lambda j, i: (j, 0, 0))],
        out_shape=[_sds((nh, d, th), BF16), _sds((nh, th, d), BF16)],
        scratch_shapes=[pltpu.VMEM((ns, d, th), F32), pltpu.VMEM((ns, th, d), F32)],
        compiler_params=_params(("parallel", "arbitrary")),
    )(s1b, hm, dh, dpb)


def _outproj_bwd(dp1, co, go, w_out, tm):
    r, d = dp1.shape
    dc = co.shape[1]

    def body(dp_ref, co_ref, go_ref, w_ref, dmi_ref, dw_ref, acc_ref):
        i = pl.program_id(0)

        @pl.when(i == 0)
        def _():
            acc_ref[...] = jnp.zeros_like(acc_ref)

        dpb = dp_ref[...].astype(BF16)
        dmi_ref[...] = _mm_nt(dpb, w_ref[...])
        acc_ref[0:dc, :] += _mm_tn(co_ref[...], dpb)
        acc_ref[dc:2 * dc, :] += _mm_tn(go_ref[...], dpb)

        @pl.when(i == pl.num_programs(0) - 1)
        def _():
            dw_ref[...] = acc_ref[...].astype(BF16)

    row = lambda w: pl.BlockSpec((tm, w), lambda i: (i, 0))
    full = pl.BlockSpec((2 * dc, d), lambda i: (0, 0))
    return pl.pallas_call(
        body, name="outproj_bwd", grid=(r // tm,),
        in_specs=[row(d), row(dc), row(dc), full],
        out_specs=[row(2 * dc), full],
        out_shape=[_sds((r, 2 * dc), F32), _sds((2 * dc, d), BF16)],
        scratch_shapes=[pltpu.VMEM((2 * dc, d), F32)],
        compiler_params=_params(("arbitrary",)),
    )(dp1, co, go, w_out)


def _gla_bwd(u, dmi, sta, gup, gb, gn, bsz, nc, kc):
    r = u.shape[0]
    hv = GLA_HEADS * GLA_DV
    hk = GLA_HEADS * GLA_DK
    ns = nc // kc

    def body(qk_ref, v_ref, r_ref, gd_ref, dgo_ref, sta_ref, gup_ref, gb_ref, gn_ref,
             dqk_ref, dv_ref, dr_ref, dgd_ref, dgn_ref, dgb_ref, dgup_ref, dst_ref):
        bi = pl.program_id(0)
        t = pl.program_id(1)

        @pl.when(jnp.logical_and(bi == 0, t == 0))
        def _():
            dgn_ref[...] = jnp.zeros_like(dgn_ref)
            dgb_ref[...] = jnp.zeros_like(dgb_ref)
            dgup_ref[...] = jnp.zeros_like(dgup_ref)

        @pl.when(t == 0)
        def _():
            dst_ref[...] = jnp.zeros_like(dst_ref)

        ps, ri, ci = _gla_prep(qk_ref, gd_ref, gup_ref[...], gb_ref[...], (ns - 1 - t) * kc, kc)
        tril = ri >= ci
        items = [(j, h) for j in reversed(range(kc)) for h in range(GLA_HEADS)]
        ops = [_gla_heads(p, v_ref) for p in ps]
        op = lambda jh: ops[jh[0]][jh[1]]
        st = {jh: sta_ref[jh[0], jh[1]] for jh in items}
        stb = {jh: st[jh].astype(BF16) for jh in items}
        a = {jh: jnp.where(tril, _mm_nt(op(jh)["qe"], op(jh)["ke"]), 0.0).astype(BF16) for jh in items}
        o1 = {jh: _mm(a[jh], op(jh)["v"]) for jh in items}
        o2 = {jh: _mm_nt(op(jh)["qe"], stb[jh]) for jh in items}
        dob = {}
        dgn = jnp.zeros((1, GLA_DV), F32)
        for jh in items:
            rows, vs = ps[jh[0]]["rows"], op(jh)["vs"]
            o = o1[jh] + o2[jh]
            rr = r_ref[rows, vs]
            sr = _sigmoid(rr)
            rs = lax.rsqrt(jnp.mean(o * o, axis=-1, keepdims=True) + LN_EPS)
            y = o * rs
            dgo = dgo_ref[rows, vs]
            don = dgo * (rr * sr)
            dr_ref[rows, vs] = (dgo * (y * gn_ref[...]) * (sr * (1.0 + rr * (1.0 - sr)))).astype(BF16)
            dgn = dgn + _rowsum(don * y)
            dxn = don * gn_ref[...]
            dob[jh] = (rs * (dxn - y * jnp.mean(dxn * y, axis=-1, keepdims=True))).astype(BF16)
        da = {jh: jnp.where(tril, _mm_nt(dob[jh], op(jh)["v"]), 0.0).astype(BF16) for jh in items}
        dv1 = {jh: _mm_tn(a[jh], dob[jh]) for jh in items}
        dqe1 = {jh: _mm(da[jh], op(jh)["ke"]) for jh in items}
        dqe2 = {jh: _mm(dob[jh], stb[jh]) for jh in items}
        dke1 = {jh: _mm_tn(da[jh], op(jh)["qe"]) for jh in items}
        inc = {jh: _mm_tn(dob[jh], op(jh)["qe"]) for jh in items}
        dsts = [dst_ref[h] for h in range(GLA_HEADS)]
        dkd1, dgam1 = {}, {}
        for jh in items:
            j, h = jh
            dst = dsts[h]
            dstb = dst.astype(BF16)
            dv_ref[ps[j]["rows"], op(jh)["vs"]] = (dv1[jh] + _mm_nt(op(jh)["kd"], dstb)).astype(BF16)
            dkd1[jh] = _mm(op(jh)["v"], dstb)
            dgam1[jh] = _rowsum(dst * st[jh])
            dsts[h] = dst * ps[j]["gam"][:, op(jh)["ls"]] + inc[jh]
        for h in range(GLA_HEADS):
            dst_ref[h] = dsts[h]
        upper = (ri <= ci).astype(BF16)
        dbs, dbls = [], []
        for j in range(kc):
            p = ps[j]
            tiles = [[op((j, 2 * hp + h2)) for h2 in range(2)] for hp in range(GLA_HEADS // 2)]
            head = lambda d, hp, h2: d[j, 2 * hp + h2]
            lanes = lambda f: jnp.concatenate([f(hp) for hp in range(GLA_HEADS // 2)], axis=1)
            dqe = lanes(lambda hp: sum(jnp.where(tiles[hp][h2]["m"], head(dqe1, hp, h2) + head(dqe2, hp, h2), 0.0)
                                       for h2 in range(2)))
            dke = lanes(lambda hp: head(dke1, hp, 0) + head(dke1, hp, 1))
            dkd = lanes(lambda hp: sum(jnp.where(tiles[hp][h2]["m"], head(dkd1, hp, h2), 0.0) for h2 in range(2)))
            dgam = lanes(lambda hp: head(dgam1, hp, 0) + head(dgam1, hp, 1))
            dqk_ref[p["rows"], :hk] = (dqe * p["eb"] * QK_SCALE).astype(BF16)
            dqk_ref[p["rows"], hk:] = (dke * p["enb"] + dkd * p["ebl"]).astype(BF16)
            dkdkd = dkd * p["kd"]
            dbs.append(dqe * p["qe"] - dke * p["ke"] - dkdkd)
            dbls.append(_rowsum(dkdkd) + dgam * p["gam"])
        dlgs = _tri_mm_all(upper, dbs)
        dzb = []
        dgb = jnp.zeros((1, hk), F32)
        for j in range(kc):
            p = ps[j]
            dz = jnp.where(p["real"], (dlgs[j] + dbls[j]) * (1.0 / GLA_TAU) * _sigmoid(-p["z"]), 0.0)
            dgb = dgb + _rowsum(dz)
            dzb.append(dz.astype(BF16))
        dgup = sum(_mm_tn(ps[j]["gd"].astype(BF16), dzb[j]) for j in range(kc))
        for j in range(kc):
            dgd_ref[ps[j]["rows"], :] = _mm_nt(dzb[j], gup_ref[...]).astype(BF16)
        dgb_ref[...] += dgb
        dgup_ref[...] += dgup
        dgn_ref[...] += dgn

    rowblk = lambda col: (lambda b, t: (b * ns + ns - 1 - t, col))
    const = lambda b, t: (0, 0)
    return pl.pallas_call(
        body, name="gla_bwd", grid=(bsz, ns),
        in_specs=[pl.BlockSpec((kc * CHUNK, QK_WIDTH), rowblk(U_QK)), pl.BlockSpec((kc * CHUNK, hv), rowblk(U_V)),
                  pl.BlockSpec((kc * CHUNK, hv), rowblk(U_R)), pl.BlockSpec((kc * CHUNK, LANES), rowblk(U_GD)),
                  pl.BlockSpec((kc * CHUNK, hv), rowblk(DMI_GLA)),
                  pl.BlockSpec((kc, GLA_HEADS, LANES, LANES), lambda b, t: (b * ns + ns - 1 - t, 0, 0, 0)),
                  pl.BlockSpec((LANES, GUP_WIDTH), const), pl.BlockSpec((1, GUP_WIDTH), const), pl.BlockSpec((1, GLA_DV), const)],
        out_specs=[pl.BlockSpec((kc * CHUNK, 2 * hk), rowblk(0)), pl.BlockSpec((kc * CHUNK, hv), rowblk(0)),
                   pl.BlockSpec((kc * CHUNK, hv), rowblk(0)), pl.BlockSpec((kc * CHUNK, LANES), rowblk(0)),
                   pl.BlockSpec((1, GLA_DV), const), pl.BlockSpec((1, GUP_WIDTH), const),
                   pl.BlockSpec((LANES, GUP_WIDTH), const)],
        out_shape=[_sds((r, 2 * hk), BF16), _sds((r, hv), BF16), _sds((r, hv), BF16), _sds((r, LANES), BF16),
                   _sds((1, GLA_DV), F32), _sds((1, GUP_WIDTH), F32), _sds((LANES, GUP_WIDTH), F32)],
        scratch_shapes=[pltpu.VMEM((GLA_HEADS, LANES, LANES), F32)],
        compiler_params=_params(("arbitrary", "arbitrary")),
    )(u, u, u, u, dmi, sta, gup, gb, gn)


def _conv_bwd(u, c, dmi, w32, cg, cbe, tp, tc, dc):
    r = u.shape[0]
    hb = tc // CONV_HALO
    nhalo = r // CONV_HALO

    def dconv(cv, dco, cg_ref, cbe_ref):
        xhat, rstd = _ln(cv)
        cn = xhat * cg_ref[...] + cbe_ref[...]
        sg = _sigmoid(cn)
        dcn = dco * (sg * (1.0 + cn * (1.0 - sg)))
        return _ln_bwd(dcn * cg_ref[...], xhat, rstd), dcn, xhat

    def body(a_ref, g_ref, ah_ref, gh_ref, c_ref, dco_ref, ch_ref, dcoh_ref, w_ref, cg_ref, cbe_ref,
             du_ref, dw_ref, dcb_ref, dcg_ref, dcbe_ref, hs_ref, dcs_ref, dw8_ref):
        t = pl.program_id(0)

        @pl.when(t == 0)
        def _():
            dw8_ref[...] = jnp.zeros_like(dw8_ref)
            dcb_ref[...] = jnp.zeros_like(dcb_ref)
            dcg_ref[...] = jnp.zeros_like(dcg_ref)
            dcbe_ref[...] = jnp.zeros_like(dcbe_ref)

        first = lax.rem(t * tc, tp) == 0
        last = lax.rem((t + 1) * tc, tp) == 0
        hh = ah_ref[...] * _sigmoid(gh_ref[...])
        hs_ref[0:CONV_HALO, :] = jnp.where(first, 0.0, hh)
        hs_ref[CONV_HALO:CONV_HALO + tc, :] = a_ref[...] * _sigmoid(g_ref[...])
        dch, _, _ = dconv(ch_ref[...], dcoh_ref[...], cg_ref, cbe_ref)
        dcs_ref[tc:tc + CONV_HALO, :] = jnp.where(last, 0.0, dch)

        def sub1(k, carry):
            r0 = pl.multiple_of(k * CONV_SUB, CONV_SUB)
            dcv, dcn, xhat = dconv(c_ref[pl.ds(r0, CONV_SUB), :], dco_ref[pl.ds(r0, CONV_SUB), :], cg_ref, cbe_ref)
            dcs_ref[pl.ds(r0, CONV_SUB), :] = dcv
            dcb_ref[...] += _rowsum(dcv)
            dcg_ref[...] += _rowsum(dcn * xhat)
            dcbe_ref[...] += _rowsum(dcn)
            return carry

        lax.fori_loop(0, tc // CONV_SUB, sub1, 0)

        def sub2(k, carry):
            r0 = pl.multiple_of(k * CONV_SUB, CONV_SUB)
            dwin = dcs_ref[pl.ds(r0, CONV_WIN), :]
            dh = _conv_taps(dwin, lambda o: w_ref[CONV_WIDTH - 1 - o:CONV_WIDTH - o, :], 0)
            av = a_ref[pl.ds(r0, CONV_SUB), :]
            sg = _sigmoid(g_ref[pl.ds(r0, CONV_SUB), :])
            du_ref[pl.ds(r0, CONV_SUB), 0:dc] = (dh * sg).astype(BF16)
            du_ref[pl.ds(r0, CONV_SUB), dc:2 * dc] = (dh * av * sg * (1.0 - sg)).astype(BF16)
            hwin = hs_ref[pl.ds(r0, CONV_WIN), :]
            dcv = dwin[0:CONV_SUB, :]
            for rho in range(8):
                offs = [o for o in range(2, 2 + CONV_WIDTH) if o % 8 == rho]
                rolled = hwin if rho == 0 else pltpu.roll(hwin, CONV_WIN - rho, 0)
                for o in offs:
                    m8 = o - rho
                    prod = dcv * rolled[m8:m8 + CONV_SUB, :]
                    dw8_ref[8 * (o - 2):8 * (o - 1), :] += jnp.sum(prod.reshape(CONV_SUB // 8, 8, dc), axis=0)
            return carry

        lax.fori_loop(0, tc // CONV_SUB, sub2, 0)

        @pl.when(t == pl.num_programs(0) - 1)
        def _():
            dw_ref[...] = jnp.zeros_like(dw_ref)
            for j in range(CONV_WIDTH):
                dw_ref[j:j + 1, :] = _rowsum(dw8_ref[8 * j:8 * (j + 1), :])

    vec = pl.BlockSpec((1, dc), lambda t: (0, 0))
    prev = lambda col: (lambda t: (jnp.maximum(t * hb - 1, 0), col))
    nxt = lambda col: (lambda t: (jnp.minimum((t + 1) * hb, nhalo - 1), col))
    return pl.pallas_call(
        body, name="conv_bwd", grid=(r // tc,),
        in_specs=[pl.BlockSpec((tc, dc), lambda t: (t, 0)), pl.BlockSpec((tc, dc), lambda t: (t, 1)),
                  pl.BlockSpec((CONV_HALO, dc), prev(0)), pl.BlockSpec((CONV_HALO, dc), prev(1)),
                  pl.BlockSpec((tc, dc), lambda t: (t, 0)), pl.BlockSpec((tc, dc), lambda t: (t, 0)),
                  pl.BlockSpec((CONV_HALO, dc), nxt(0)), pl.BlockSpec((CONV_HALO, dc), nxt(0)),
                  pl.BlockSpec((32, dc), lambda t: (0, 0)), vec, vec],
        out_specs=[pl.BlockSpec((tc, 2 * dc), lambda t: (t, 0)), pl.BlockSpec((32, dc), lambda t: (0, 0)), vec, vec, vec],
        out_shape=[_sds((r, 2 * dc), BF16), _sds((32, dc), F32), _sds((1, dc), F32), _sds((1, dc), F32), _sds((1, dc), F32)],
        scratch_shapes=[pltpu.VMEM((CONV_HALO + tc, dc), F32), pltpu.VMEM((tc + CONV_HALO, dc), F32),
                        pltpu.VMEM((8 * 32, dc), F32)],
        compiler_params=_params(("arbitrary",)),
    )(u, u, u, u, c, dmi, c, dmi, w32, cg, cbe)


def _inproj_bwd(dp1, dus, xsrc, g_in, w_in, tp, seq, tx):
    r, d = dp1.shape
    widths = [x.shape[1] for x in dus]
    offs = [sum(widths[:k]) for k in range(len(widths))]
    n = w_in.shape[0]
    nd = len(dus)
    head = tx == 0
    rows = X_OFF if head else tx

    def body(*refs):
        dp_ref = refs[0]
        du_refs = refs[1:1 + nd]
        x_ref, g_ref, w_ref, out_ref, dg_ref, db_ref = refs[1 + nd:]
        i = pl.program_id(0)

        @pl.when(i == 0)
        def _():
            dg_ref[...] = jnp.zeros_like(dg_ref)
            db_ref[...] = jnp.zeros_like(db_ref)
            if head:
                out_ref[...] = jnp.zeros_like(out_ref)

        nblk = 1 if head or rows % 32 else 2
        blocks = [slice(b * (rows // nblk), (b + 1) * (rows // nblk)) for b in range(nblk)]
        mms = [sum(_mm(du_refs[k][blk, :], w_ref[offs[k]:offs[k] + widths[k], :]) for k in range(nd)) for blk in blocks]
        for blk, mm in zip(blocks, mms):
            ds0 = ALPHA * dp_ref[blk, :] + mm
            if head:
                ds0 = jnp.where(lax.broadcasted_iota(jnp.int32, (X_OFF, 1), 0) >= PAD_FRONT, ds0, 0.0)
            xhat, rstd = _ln(x_ref[blk, :])
            dg_ref[...] += _rowsum(ds0 * xhat)
            db_ref[...] += _rowsum(ds0)
            dx = _ln_bwd(ds0 * g_ref[...], xhat, rstd)
            if head:
                out_ref[...] += dx[PAD_FRONT:X_OFF, :]
            else:
                out_ref[blk, :] = dx

    if head:
        nb = tp // X_OFF
        row = lambda w: pl.BlockSpec((X_OFF, w), lambda i: (i * nb, 0))
        xspec = pl.BlockSpec((X_OFF, d), lambda i: (0, 0))
        ospec, oshape, steps = pl.BlockSpec((N_META, d), lambda i: (0, 0)), _sds((N_META, d), F32), r // tp
    else:
        start = _x_tile_row(tp, seq, tx)
        row = lambda w: pl.BlockSpec((pl.Element(tx), pl.Element(w)), lambda i: (start(i), 0))
        xspec = pl.BlockSpec((tx, d), lambda i: (i, 0))
        ospec, oshape, steps = xspec, _sds(xsrc.shape, F32), xsrc.shape[0] // tx
    vec = pl.BlockSpec((1, d), lambda i: (0, 0))
    return pl.pallas_call(
        body, name="inproj_bwd_head" if head else "inproj_bwd_x", grid=(steps,),
        in_specs=[row(d)] + [row(w) for w in widths] + [xspec, vec, pl.BlockSpec((n, d), lambda i: (0, 0))],
        out_specs=[ospec, vec, vec],
        out_shape=[oshape, _sds((1, d), F32), _sds((1, d), F32)],
        compiler_params=_params(("arbitrary",)),
    )(dp1, *dus, xsrc, g_in, w_in)


def _inproj_bwd_w(s0, dus, tm):
    r, d = s0.shape
    widths = [x.shape[1] for x in dus]
    offs = [sum(widths[:k]) for k in range(len(widths))]
    nd = len(dus)

    def body(*refs):
        s_ref = refs[0]
        du_refs = refs[1:1 + nd]
        dw_ref, acc_ref = refs[1 + nd:]
        i = pl.program_id(0)

        @pl.when(i == 0)
        def _():
            acc_ref[...] = jnp.zeros_like(acc_ref)

        for k in range(nd):
            acc_ref[offs[k]:offs[k] + widths[k], :] += _mm_tn(du_refs[k][...], s_ref[...])

        @pl.when(i == pl.num_programs(0) - 1)
        def _():
            dw_ref[...] = acc_ref[...].astype(BF16)

    row = lambda w: pl.BlockSpec((tm, w), lambda i: (i, 0))
    return pl.pallas_call(
        body, name="inproj_bwd_w", grid=(r // tm,),
        in_specs=[row(d)] + [row(w) for w in widths],
        out_specs=pl.BlockSpec((sum(widths), d), lambda i: (0, 0)),
        out_shape=_sds((sum(widths), d), BF16),
        scratch_shapes=[pltpu.VMEM((sum(widths), d), F32)],
        compiler_params=_params(("arbitrary",)),
    )(s0, *dus)


def _local_step(x, tgt, meta, ln_in_g, ln_in_b, w_in, conv_w, conv_b, conv_ln_g, conv_ln_b, gate_up, gate_bias,
                gla_norm_g, late_weights, ln1_g, ln1_b, ln2_g, ln2_b, push):
    bsz, seq, d = x.shape
    tp = X_OFF + seq
    assert tp % CHUNK == 0
    nc = tp // CHUNK
    dc = conv_b.shape[1]
    tmm = tc = _pick_tile(tp, (704, 128, 64))
    tx = _pick_tile(seq, (512, 64))
    kc = _pick_tile(nc, (11, 3, 2, 1))
    ns = 2

    x2 = x.reshape(bsz * seq, d)
    head = jnp.pad(meta, ((PAD_FRONT, 0), (0, 0)))
    tgt_p = tgt.reshape(bsz * seq, d)
    w32 = jnp.pad(conv_w, ((0, 32 - CONV_WIDTH), (0, 0)))
    gup = jnp.pad(gate_up, ((0, LANES - GLA_RANK), (0, 0))).astype(BF16)

    s0, s0b = _ln_in_x(x2, ln_in_g, ln_in_b, tp, seq, tx)
    s0, s0b = _ln_in_head(head, ln_in_g, ln_in_b, s0, s0b, tp)
    u = _inproj_fwd(s0b, w_in, tmm)
    c, co = _conv_fwd(u, w32, conv_b, conv_ln_g, conv_ln_b, tp, tc, dc)
    go, sta = _gla_fwd(u, gup, gate_bias, gla_norm_g, bsz, nc, kc)
    w_out, w1g, w2 = late_weights
    nh = w1g.shape[0]
    p1, s1, s1b = _outproj_fwd(s0, co, go, w_out, ln1_g, ln1_b, tmm)
    hm, dp2, dpb, loss, dg2, db2 = _mlp_fwd(s1, s1b, w1g, w2, ln2_g, ln2_b, tgt_p, tp, tmm, ns)

    dh, dp1, dg1, db1 = _mlp_bwd_act(dp2, dpb, hm, w1g, w2, p1, ln1_g, tmm, ns)
    dw1, dw2 = _mlp_bwd_w(s1b, hm, dh, dpb, nh, tmm, ns)
    push("ff", (dw1, dw2))
    dmi, dwo = _outproj_bwd(dp1, co, go, w_out, tmm)
    push("out", (dwo,))
    dqk, dv, dr, dgd, dgn, dgb, dgup = _gla_bwd(u, dmi, sta, gup, gate_bias, gla_norm_g, bsz, nc, kc)
    dcv, dcw, dcb, dcg, dcbe = _conv_bwd(u, c, dmi, w32, conv_ln_g, conv_ln_b, tp, tc, dc)
    dus = [dcv, dqk, dv, dr, dgd]
    dwi = _inproj_bwd_w(s0b, dus, tmm)
    push("in", (dwi,))
    gx, dgx, dbx = _inproj_bwd(dp1, dus, x2, ln_in_g, w_in, tp, seq, tx)
    dmeta, dgh, dbh = _inproj_bwd(dp1, dus, head, ln_in_g, w_in, tp, seq, 0)

    return dict(loss=loss[0, 0], grad_x=gx.reshape(bsz, seq, d), meta_tokens=dmeta, ln_in_g=dgx + dgh, ln_in_b=dbx + dbh,
                conv_w=dcw[:CONV_WIDTH], conv_b=dcb, conv_ln_g=dcg, conv_ln_b=dcbe,
                gate_up=dgup[:GLA_RANK], gate_bias=dgb, gla_norm_g=dgn, ln1_g=dg1, ln1_b=db1, ln2_g=dg2, ln2_b=db2)


def _exchange(arrays, scatter, name):
    na = len(arrays)
    npeer = N_DEV - 1

    def body(*refs):
        srcs = refs[:na]
        outs = refs[na:2 * na]
        send_sems, recv_sems, local_sems = refs[2 * na:]
        xi, yi, ci = (lax.axis_index(a) for a in MESH_AXES)
        me = 4 * xi + 2 * yi + ci
        copies = []
        for a in range(na):
            own = srcs[a].at[me] if scatter[a] else srcs[a]
            cp = pltpu.make_async_copy(own, outs[a].at[me], local_sems.at[a])
            cp.start()
            copies.append(cp)
        remote = []
        for k in range(1, N_DEV):
            px, py, pc = xi ^ (k >> 2), yi ^ ((k >> 1) & 1), ci ^ (k & 1)
            peer = 4 * px + 2 * py + pc
            for a in range(na):
                src = srcs[a].at[peer] if scatter[a] else srcs[a]
                cp = pltpu.make_async_remote_copy(
                    src_ref=src, dst_ref=outs[a].at[me],
                    send_sem=send_sems.at[a * npeer + k - 1], recv_sem=recv_sems.at[a * npeer + k - 1],
                    device_id=(px, py, pc), device_id_type=pl.DeviceIdType.MESH)
                cp.start()
                remote.append(cp)
        for cp in remote:
            cp.wait()
        for cp in copies:
            cp.wait()

    out_shape = [_sds(a.shape if scatter[i] else (N_DEV,) + a.shape, a.dtype) for i, a in enumerate(arrays)]
    anyspec = pl.BlockSpec(memory_space=pl.ANY)
    return pl.pallas_call(
        body, name=name,
        in_specs=[anyspec] * na, out_specs=[anyspec] * na, out_shape=out_shape,
        scratch_shapes=[pltpu.SemaphoreType.DMA((na * npeer,)), pltpu.SemaphoreType.DMA((na * npeer,)),
                        pltpu.SemaphoreType.DMA((na,))],
    )(*arrays)


def _peers(xi, yi, ci):
    for k in range(1, N_DEV):
        px, py, pc = xi ^ (k >> 2), yi ^ ((k >> 1) & 1), ci ^ (k & 1)
        yield (px, py, pc), 4 * px + 2 * py + pc


def _sc_exchange(arrays, scatter, name, collective_id, after=None):
    na = len(arrays)
    npeer = N_DEV - 1
    ndep = 0 if after is None else 1

    def body(*refs):
        srcs = refs[:na]
        outs = refs[na + ndep:2 * na + ndep]
        send_sems, recv_sems, own_sems = refs[2 * na + ndep:]
        xi, yi, ci = (lax.axis_index(a) for a in MESH_AXES)
        me = 4 * xi + 2 * yi + ci
        barrier = pltpu.get_barrier_semaphore()
        for pos, _ in _peers(xi, yi, ci):
            pl.semaphore_signal(barrier, inc=1, device_id=pos, device_id_type=pl.DeviceIdType.MESH)
        pl.semaphore_wait(barrier, npeer)
        own = [pltpu.make_async_copy(srcs[a].at[me] if scatter[a] else srcs[a], outs[a].at[me], own_sems.at[a])
               for a in range(na)]
        for cp in own:
            cp.start()
        remote = []
        for a in range(na):
            for k, (pos, peer) in enumerate(_peers(xi, yi, ci)):
                cp = pltpu.make_async_remote_copy(
                    src_ref=srcs[a].at[peer] if scatter[a] else srcs[a], dst_ref=outs[a].at[me],
                    send_sem=send_sems.at[a * npeer + k], recv_sem=recv_sems.at[a * npeer + k],
                    device_id=pos, device_id_type=pl.DeviceIdType.MESH)
                cp.start()
                remote.append(cp)
        for cp in own:
            cp.wait()
        for cp in remote:
            cp.wait()

    out_type = [_sds(a.shape if scatter[i] else (N_DEV,) + a.shape, a.dtype) for i, a in enumerate(arrays)]
    sent = sum(a.size * a.dtype.itemsize // (N_DEV if scatter[i] else 1) for i, a in enumerate(arrays))
    return pl.kernel(
        body, out_type=out_type, mesh=plsc.ScalarSubcoreMesh(axis_name="seq", num_cores=1), name=name,
        scratch_types=[pltpu.SemaphoreType.DMA((na * npeer,)), pltpu.SemaphoreType.DMA((na * npeer,)),
                       pltpu.SemaphoreType.DMA((na,))],
        compiler_params=pltpu.CompilerParams(collective_id=collective_id),
        cost_estimate=pl.CostEstimate(flops=0, transcendentals=0, bytes_accessed=2 * N_DEV * sent,
                                      remote_bytes_transferred=npeer * sent),
    )(*arrays, *([] if after is None else [after]))


def _sc_gather(arrays, name, collective_id, after=None):
    na = len(arrays)
    ndep = 0 if after is None else 1
    npair = N_DEV - 1

    def body(*refs):
        srcs = refs[:na]
        outs = refs[na + ndep:2 * na + ndep]
        send_sems, recv_sems, own_sems = refs[2 * na + ndep:]
        xi, yi, ci = (lax.axis_index(a) for a in MESH_AXES)
        me = 4 * xi + 2 * yi + ci
        sibling = (xi, yi, 1 - ci)
        chips = [(1 - xi, yi), (xi, 1 - yi), (1 - xi, 1 - yi)]
        barrier = pltpu.get_barrier_semaphore()
        for pos, _ in _peers(xi, yi, ci):
            pl.semaphore_signal(barrier, inc=1, device_id=pos, device_id_type=pl.DeviceIdType.MESH)
        pl.semaphore_wait(barrier, npair)

        def copy(a, k, src, slot, to):
            return pltpu.make_async_remote_copy(
                src_ref=src, dst_ref=outs[a].at[slot], send_sem=send_sems.at[a * npair + k],
                recv_sem=recv_sems.at[a * npair + k], device_id=to, device_id_type=pl.DeviceIdType.MESH)

        own = [pltpu.make_async_copy(srcs[a], outs[a].at[me], own_sems.at[a]) for a in range(na)]
        for cp in own:
            cp.start()
        sent = []
        for a in range(na):
            sent.append(copy(a, 0, srcs[a], me, sibling))
            sent += [copy(a, 1 + j, srcs[a], me, (*chip, ci)) for j, chip in enumerate(chips)]
        for cp in sent:
            cp.start()
        for j, (cx, cy) in enumerate(chips):
            slot = 4 * cx + 2 * cy + ci
            for a in range(na):
                copy(a, 1 + j, srcs[a], slot, sibling).wait_recv()
                cp = copy(a, 4 + j, outs[a].at[slot], slot, sibling)
                cp.start()
                sent.append(cp)
        for a in range(na):
            copy(a, 0, srcs[a], me, sibling).wait_recv()
            for j in range(len(chips)):
                copy(a, 4 + j, srcs[a], me, sibling).wait_recv()
        for cp in sent:
            cp.wait_send()
        for cp in own:
            cp.wait()

    out_type = [_sds((N_DEV,) + a.shape, a.dtype) for a in arrays]
    sent_bytes = sum(a.size * a.dtype.itemsize for a in arrays)
    return pl.kernel(
        body, out_type=out_type, mesh=plsc.ScalarSubcoreMesh(axis_name="seq", num_cores=1), name=name,
        scratch_types=[pltpu.SemaphoreType.DMA((na * npair,)), pltpu.SemaphoreType.DMA((na * npair,)),
                       pltpu.SemaphoreType.DMA((na,))],
        compiler_params=pltpu.CompilerParams(collective_id=collective_id),
        cost_estimate=pl.CostEstimate(flops=0, transcendentals=0, bytes_accessed=2 * N_DEV * sent_bytes,
                                      remote_bytes_transferred=npair * sent_bytes),
    )(*arrays, *([] if after is None else [after]))


def _adamw(w, g, m, v):
    m = ADAM_B1 * m + (1.0 - ADAM_B1) * g
    v = ADAM_B2 * v + (1.0 - ADAM_B2) * jnp.square(g)
    m_hat = m / (1.0 - ADAM_B1 ** ADAM_STEP)
    v_hat = v / (1.0 - ADAM_B2 ** ADAM_STEP)
    delta = -ADAM_LR * (m_hat / (jnp.sqrt(v_hat) + ADAM_EPS) + ADAM_WD * w)
    return delta, m, v


def _sum_devices(ref):
    g = ref[0].astype(F32)
    for k in range(1, N_DEV):
        g = g + ref[k].astype(F32)
    return g


def _update_big(parts, w, m, v, name):
    rows, cols = w.shape

    def body(p_ref, w_ref, m_ref, v_ref, g_ref, d_ref, nm_ref, nv_ref):
        g = _sum_devices(p_ref)
        g_ref[...] = g
        d_ref[...], nm_ref[...], nv_ref[...] = _adamw(w_ref[...], g, m_ref[...], v_ref[...])

    if rows % 16 == 0:
        tr = _pick_tile(rows, (128, 64, 16))
        steps, blk = rows // tr, pl.BlockSpec((tr, cols), lambda i: (i, 0))
        pblk = pl.BlockSpec((N_DEV, tr, cols), lambda i: (0, i, 0))
    else:
        tcol = 2 * LANES
        steps, blk = cols // tcol, pl.BlockSpec((rows, tcol), lambda i: (0, i))
        pblk = pl.BlockSpec((N_DEV, rows, tcol), lambda i: (0, 0, i))
    return pl.pallas_call(
        body, name=name, grid=(steps,),
        in_specs=[pblk, blk, blk, blk],
        out_specs=[blk] * 4, out_shape=[_sds((rows, cols), F32)] * 4,
        compiler_params=_params(("parallel",)),
    )(parts, w, m, v)


_VEC_ORDER = ("ln_in_g", "ln_in_b", "conv_b", "conv_ln_g", "conv_ln_b", "gate_bias", "gla_norm_g",
              "ln1_g", "ln1_b", "ln2_g", "ln2_b")
_SHARDED_SMALL = (("meta_tokens", 0, N_META, LANES), ("conv_w", N_META, CONV_WIDTH, None), ("gate_up", N_META + 32, GLA_RANK, None))


def _update_small(parts_sh, parts_vec, wmv):
    names = [s[0] for s in _SHARDED_SMALL] + list(_VEC_ORDER)
    flat = [a for nme in names for a in wmv[nme]]
    nv = len(_VEC_ORDER)

    def body(*refs):
        sh_ref, vec_ref = refs[0], refs[1]
        ins = refs[2:2 + len(flat)]
        outs = refs[2 + len(flat):2 + len(flat) + 4 * len(names)]
        loss_ref = refs[2 + len(flat) + 4 * len(names)]
        gsh_ref, gvec_ref = refs[-2:]
        gsh_ref[...] = _sum_devices(sh_ref)
        gvec_ref[...] = _sum_devices(vec_ref)
        loss_ref[...] = gvec_ref[nv:nv + 1, :]
        for idx, nme in enumerate(names):
            w_ref, m_ref, v_ref = ins[3 * idx:3 * idx + 3]
            rows, cols = w_ref.shape
            if idx < len(_SHARDED_SMALL):
                r0 = _SHARDED_SMALL[idx][1]
                g = gsh_ref[r0:r0 + rows, 0:cols]
            else:
                j = idx - len(_SHARDED_SMALL)
                g = gvec_ref[j:j + 1, 0:cols]
            o = outs[4 * idx:4 * idx + 4]
            o[0][...] = g
            o[1][...], o[2][...], o[3][...] = _adamw(w_ref[...], g, m_ref[...], v_ref[...])

    out_shape = [_sds(wmv[nme][0].shape, F32) for nme in names for _ in range(4)] + [_sds((1, parts_vec.shape[2]), F32)]
    vmem = pl.BlockSpec(memory_space=pltpu.VMEM)
    res = pl.pallas_call(
        body, name="update_small", out_shape=out_shape,
        in_specs=[vmem] * (2 + len(flat)), out_specs=[vmem] * len(out_shape),
        scratch_shapes=[pltpu.VMEM(parts_sh.shape[1:], F32), pltpu.VMEM(parts_vec.shape[1:], F32)],
    )(parts_sh, parts_vec, *flat)
    return {nme: res[4 * i:4 * i + 4] for i, nme in enumerate(names)}, res[-1][0, 0]


_WEIGHTS = ("meta_tokens", "ln_in_g", "ln_in_b", "w_in", "conv_w", "conv_b", "conv_ln_g", "conv_ln_b", "gate_up",
            "gate_bias", "gla_norm_g", "w_out", "ln1_g", "ln1_b", "w_ff1", "w_ff2", "ln2_g", "ln2_b")


def kernel(x, meta_tokens, ln_in_g, ln_in_b, w_in, conv_w, conv_b, conv_ln_g, conv_ln_b, gate_up, gate_bias, gla_norm_g, w_out, ln1_g, ln1_b, w_ff1, w_ff2, ln2_g, ln2_b, loss_target, m_meta_tokens, m_ln_in_g, m_ln_in_b, m_w_in, m_conv_w, m_conv_b, m_conv_ln_g, m_conv_ln_b, m_gate_up, m_gate_bias, m_gla_norm_g, m_w_out, m_ln1_g, m_ln1_b, m_w_ff1, m_w_ff2, m_ln2_g, m_ln2_b, v_meta_tokens, v_ln_in_g, v_ln_in_b, v_w_in, v_conv_w, v_conv_b, v_conv_ln_g, v_conv_ln_b, v_gate_up, v_gate_bias, v_gla_norm_g, v_w_out, v_ln1_g, v_ln1_b, v_w_ff1, v_w_ff2, v_ln2_g, v_ln2_b):
    w = dict(meta_tokens=meta_tokens, ln_in_g=ln_in_g, ln_in_b=ln_in_b, w_in=w_in, conv_w=conv_w, conv_b=conv_b,
             conv_ln_g=conv_ln_g, conv_ln_b=conv_ln_b, gate_up=gate_up, gate_bias=gate_bias, gla_norm_g=gla_norm_g,
             w_out=w_out, ln1_g=ln1_g, ln1_b=ln1_b, w_ff1=w_ff1, w_ff2=w_ff2, ln2_g=ln2_g, ln2_b=ln2_b)
    mom = dict(meta_tokens=m_meta_tokens, ln_in_g=m_ln_in_g, ln_in_b=m_ln_in_b, w_in=m_w_in, conv_w=m_conv_w,
               conv_b=m_conv_b, conv_ln_g=m_conv_ln_g, conv_ln_b=m_conv_ln_b, gate_up=m_gate_up, gate_bias=m_gate_bias,
               gla_norm_g=m_gla_norm_g, w_out=m_w_out, ln1_g=m_ln1_g, ln1_b=m_ln1_b, w_ff1=m_w_ff1, w_ff2=m_w_ff2,
               ln2_g=m_ln2_g, ln2_b=m_ln2_b)
    var = dict(meta_tokens=v_meta_tokens, ln_in_g=v_ln_in_g, ln_in_b=v_ln_in_b, w_in=v_w_in, conv_w=v_conv_w,
               conv_b=v_conv_b, conv_ln_g=v_conv_ln_g, conv_ln_b=v_conv_ln_b, gate_up=v_gate_up, gate_bias=v_gate_bias,
               gla_norm_g=v_gla_norm_g, w_out=v_w_out, ln1_g=v_ln1_g, ln1_b=v_ln1_b, w_ff1=v_w_ff1, w_ff2=v_w_ff2,
               ln2_g=v_ln2_g, ln2_b=v_ln2_b)
    shapes = {k: a.shape for k, a in w.items()}

    def two_d(a):
        return a.reshape(1, -1) if a.ndim == 1 else a.reshape(a.shape[-2:])

    w2d = {k: two_d(a) for k, a in w.items()}
    m2d = {k: two_d(a) for k, a in mom.items()}
    v2d = {k: two_d(a) for k, a in var.items()}
    d = x.shape[-1]
    d_in = w2d["w_in"].shape[1] * N_DEV
    d_in_p = -(-d_in // LANES) * LANES

    for dct in (w2d, m2d, v2d):
        dct["w_in"] = dct["w_in"].T
    g_in, g_meta, g_conv, g_gup = _sc_gather(
        [w2d["w_in"].astype(BF16), w2d["meta_tokens"], w2d["conv_w"], w2d["gate_up"]], "gather_first", 0)
    g_out, g_ff1, g_ff2 = _sc_gather(
        [w2d["w_out"].astype(BF16), w2d["w_ff1"].astype(BF16), w2d["w_ff2"].astype(BF16)], "gather_late", 1)
    w_in_full = jnp.pad(g_in.reshape(d_in, d), ((0, d_in_p - d_in), (0, 0)))
    meta_full = g_meta.transpose(1, 0, 2).reshape(N_META, d)
    conv_w_full = g_conv.transpose(1, 0, 2).reshape(CONV_WIDTH, -1)
    gate_up_full = g_gup.transpose(1, 0, 2).reshape(GLA_RANK, -1)

    late_weights = (g_out.reshape(-1, d), g_ff1, g_ff2.reshape(-1, d))
    pushed = {}

    def push(tag, grads):
        if tag == "ff":
            pushed["ff1"], pushed["ff2"] = _sc_exchange(list(grads), [True, True], "scatter_ff", 2)
        elif tag == "out":
            pushed["p_out"] = grads[0].reshape(N_DEV, -1, d)
        else:
            p_in = grads[0][:d_in].reshape(N_DEV, d_in // N_DEV, d)
            pushed["in"], pushed["out"] = _sc_exchange([p_in, pushed["p_out"]], [True, True], "scatter_rest", 3,
                                                       after=pushed["ff1"])

    res = _local_step(x, loss_target, meta_full, w2d["ln_in_g"], w2d["ln_in_b"], w_in_full, conv_w_full, w2d["conv_b"],
                      w2d["conv_ln_g"], w2d["conv_ln_b"], gate_up_full, w2d["gate_bias"], w2d["gla_norm_g"], late_weights,
                      w2d["ln1_g"], w2d["ln1_b"], w2d["ln2_g"], w2d["ln2_b"], push)

    dc = res["conv_w"].shape[1]
    hk = res["gate_up"].shape[1]
    sh_meta = res["meta_tokens"].reshape(N_META, N_DEV, LANES).transpose(1, 0, 2)
    sh_conv = jnp.pad(res["conv_w"].reshape(CONV_WIDTH, N_DEV, dc // N_DEV).transpose(1, 0, 2),
                      ((0, 0), (0, 32 - CONV_WIDTH), (0, LANES - dc // N_DEV)))
    sh_gup = jnp.pad(res["gate_up"].reshape(GLA_RANK, N_DEV, hk // N_DEV).transpose(1, 0, 2),
                     ((0, 0), (0, 0), (0, LANES - hk // N_DEV)))
    p_sh = jnp.concatenate([sh_meta, sh_conv, sh_gup], axis=1)
    p_vec = jnp.concatenate([jnp.pad(res[k], ((0, 0), (0, d - res[k].shape[1]))) for k in _VEC_ORDER]
                            + [jnp.full((1, d), res["loss"], F32), jnp.zeros((15 - len(_VEC_ORDER), d), F32)], axis=0)

    r_sh, r_vec = _exchange([p_sh, p_vec], [True, False], "scatter_small")
    r_ff1, r_ff2, r_out, r_in = pushed["ff1"], pushed["ff2"], pushed["out"], pushed["in"]

    upd = {}
    upd["w_in"] = [a.T for a in _update_big(r_in, w2d["w_in"], m2d["w_in"], v2d["w_in"], "update_w_in")]
    upd["w_out"] = _update_big(r_out, w2d["w_out"], m2d["w_out"], v2d["w_out"], "update_w_out")
    upd["w_ff1"] = _update_big(r_ff1, w2d["w_ff1"], m2d["w_ff1"], v2d["w_ff1"], "update_w_ff1")
    upd["w_ff2"] = _update_big(r_ff2, w2d["w_ff2"], m2d["w_ff2"], v2d["w_ff2"], "update_w_ff2")
    small = [s[0] for s in _SHARDED_SMALL] + list(_VEC_ORDER)
    upd_small, loss = _update_small(r_sh, r_vec, {k: (w2d[k], m2d[k], v2d[k]) for k in small})
    upd.update(upd_small)

    outs = [loss, res["grad_x"]]
    for j in range(4):
        outs += [upd[k][j].reshape(shapes[k]) for k in _WEIGHTS]
    return tuple(outs)
```

```python
import jax
import jax.numpy as jnp
from jax import lax
from jax.experimental import pallas as pl
from jax.experimental.pallas import tpu as pltpu
from jax.experimental.pallas import tpu_sc as plsc

F32 = jnp.float32
BF16 = jnp.bfloat16

N_META = 16
CHUNK = 64
PAD_FRONT = (-N_META) % CHUNK
X_OFF = PAD_FRONT + N_META
CONV_WIDTH = 31
CONV_HALO = 32
CONV_SUB = 64
CONV_WIN = CONV_SUB + CONV_HALO
GLA_HEADS = 4
GLA_DK = 64
GLA_DV = 128
GLA_RANK = 16
GLA_TAU = 16.0
QK_SCALE = GLA_DK ** -0.5
LN_EPS = 1e-5
ALPHA = 2.0 ** 0.25
LANES = 128
N_DEV = 8
ADAM_LR = 0.001
ADAM_B1 = 0.9
ADAM_B2 = 0.999
ADAM_EPS = 1e-08
ADAM_WD = 0.01
ADAM_STEP = 10
VMEM_LIMIT = 56 * 1024 * 1024
MESH_AXES = ("x", "y", "c")
U_QK, U_V, U_R, U_GD = 2, 3, 4, 20
QK_WIDTH = 2 * GLA_HEADS * GLA_DK
GUP_WIDTH = GLA_HEADS * GLA_DK
DMI_GLA = 1


def _sds(shape, dtype):
    return jax.ShapeDtypeStruct(shape, dtype)


def _mm(a, b):
    return jnp.dot(a, b, preferred_element_type=F32)


def _mm_nt(a, b):
    return lax.dot_general(a, b, (((1,), (1,)), ((), ())), preferred_element_type=F32)


def _mm_tn(a, b):
    return lax.dot_general(a, b, (((0,), (0,)), ((), ())), preferred_element_type=F32)


def _sigmoid(x):
    return 1.0 / (1.0 + jnp.exp(-x))


def _log_sigmoid(z):
    return jnp.minimum(z, 0.0) - jnp.log(1.0 + jnp.exp(-jnp.abs(z)))


def _ln(x):
    mu = jnp.mean(x, axis=-1, keepdims=True)
    xc = x - mu
    var = jnp.mean(xc * xc, axis=-1, keepdims=True)
    rstd = lax.rsqrt(var + LN_EPS)
    return xc * rstd, rstd


def _ln_bwd(dyg, xhat, rstd):
    m1 = jnp.mean(dyg, axis=-1, keepdims=True)
    m2 = jnp.mean(dyg * xhat, axis=-1, keepdims=True)
    return rstd * (dyg - m1 - xhat * m2)


def _rowsum(x):
    return jnp.sum(x, axis=0, keepdims=True)


def _row_in_seq(i, tm, tp):
    base = lax.rem(i * tm, tp)
    return base + lax.broadcasted_iota(jnp.int32, (tm, 1), 0)


def _split3(x):
    hi = x.astype(BF16)
    r1 = x - hi.astype(F32)
    mid = r1.astype(BF16)
    lo = (r1 - mid.astype(F32)).astype(BF16)
    return hi, mid, lo


def _params(sem):
    return pltpu.CompilerParams(dimension_semantics=sem, vmem_limit_bytes=VMEM_LIMIT)


def _pick_tile(n, prefs):
    for t in prefs:
        if n % t == 0:
            return t
    raise ValueError(f"no tile for {n}")


def _x_tile_row(tp, seq, tx):
    tps = seq // tx
    return lambda i: pl.multiple_of((i // tps) * tp + X_OFF + (i % tps) * tx, CHUNK)


def _ln_in_x(x2, g, b, tp, seq, tx):
    rx, d = x2.shape
    r = rx // seq * tp
    row = _x_tile_row(tp, seq, tx)

    def body(x_ref, g_ref, b_ref, s0_ref, sb_ref):
        xhat, _ = _ln(x_ref[...])
        s = xhat * g_ref[...] + b_ref[...]
        s0_ref[...] = s
        sb_ref[...] = s.astype(BF16)

    out = pl.BlockSpec((pl.Element(tx), pl.Element(d)), lambda i: (row(i), 0))
    return pl.pallas_call(
        body, name="ln_in_x", grid=(rx // tx,),
        in_specs=[pl.BlockSpec((tx, d), lambda i: (i, 0)), pl.BlockSpec((1, d), lambda i: (0, 0)),
                  pl.BlockSpec((1, d), lambda i: (0, 0))],
        out_specs=[out, out],
        out_shape=[_sds((r, d), F32), _sds((r, d), BF16)],
        compiler_params=_params(("parallel",)),
    )(x2, g, b)


def _ln_in_head(head, g, b, s0, s0b, tp):
    r, d = s0.shape
    nb = tp // X_OFF

    def body(h_ref, g_ref, b_ref, s0_in, sb_in, s0_ref, sb_ref):
        xhat, _ = _ln(h_ref[...])
        real = lax.broadcasted_iota(jnp.int32, (X_OFF, 1), 0) >= PAD_FRONT
        s = jnp.where(real, xhat * g_ref[...] + b_ref[...], 0.0)
        s0_ref[...] = s
        sb_ref[...] = s.astype(BF16)

    anyspec = pl.BlockSpec(memory_space=pl.ANY)
    out = pl.BlockSpec((X_OFF, d), lambda i: (i * nb, 0))
    return pl.pallas_call(
        body, name="ln_in_head", grid=(r // tp,),
        in_specs=[pl.BlockSpec((X_OFF, d), lambda i: (0, 0)), pl.BlockSpec((1, d), lambda i: (0, 0)),
                  pl.BlockSpec((1, d), lambda i: (0, 0)), anyspec, anyspec],
        out_specs=[out, out],
        out_shape=[_sds((r, d), F32), _sds((r, d), BF16)],
        input_output_aliases={3: 0, 4: 1},
        compiler_params=_params(("parallel",)),
    )(head, g, b, s0, s0b)


def _inproj_fwd(s0b, w_in, tm):
    r, d = s0b.shape
    n = w_in.shape[0]

    def body(s_ref, w_ref, u_ref):
        u_ref[...] = _mm_nt(s_ref[...], w_ref[...])

    return pl.pallas_call(
        body, name="inproj_fwd", grid=(r // tm,),
        in_specs=[pl.BlockSpec((tm, d), lambda i: (i, 0)), pl.BlockSpec((n, d), lambda i: (0, 0))],
        out_specs=pl.BlockSpec((tm, n), lambda i: (i, 0)),
        out_shape=_sds((r, n), F32),
        compiler_params=_params(("parallel",)),
    )(s0b, w_in)


def _conv_taps(win, coef, lo):
    acc = None
    for rho in range(8):
        offs = [o for o in range(lo, lo + CONV_WIDTH) if o % 8 == rho]
        if not offs:
            continue
        rolled = win if rho == 0 else pltpu.roll(win, CONV_WIN - rho, 0)
        for o in offs:
            m8 = o - rho
            term = rolled[m8:m8 + CONV_SUB, :] * coef(o)
            acc = term if acc is None else acc + term
    return acc


def _conv_fwd(u, w32, cb, cg, cbe, tp, tc, dc):
    r = u.shape[0]
    hb = tc // CONV_HALO

    def body(a_ref, g_ref, ah_ref, gh_ref, w_ref, cb_ref, cg_ref, cbe_ref, c_ref, co_ref, hs_ref):
        t = pl.program_id(0)
        first = lax.rem(t * tc, tp) == 0
        hh = ah_ref[...] * _sigmoid(gh_ref[...])
        hs_ref[0:CONV_HALO, :] = jnp.where(first, 0.0, hh)
        hs_ref[CONV_HALO:CONV_HALO + tc, :] = a_ref[...] * _sigmoid(g_ref[...])

        def sub(k, carry):
            r0 = pl.multiple_of(k * CONV_SUB, CONV_SUB)
            win = hs_ref[pl.ds(r0, CONV_WIN), :]
            c = _conv_taps(win, lambda o: w_ref[o - 2:o - 1, :], 2) + cb_ref[...]
            c_ref[pl.ds(r0, CONV_SUB), :] = c
            xhat, _ = _ln(c)
            cn = xhat * cg_ref[...] + cbe_ref[...]
            co_ref[pl.ds(r0, CONV_SUB), :] = (cn * _sigmoid(cn)).astype(BF16)
            return carry

        lax.fori_loop(0, tc // CONV_SUB, sub, 0)

    vec = pl.BlockSpec((1, dc), lambda t: (0, 0))
    return pl.pallas_call(
        body, name="conv_fwd", grid=(r // tc,),
        in_specs=[pl.BlockSpec((tc, dc), lambda t: (t, 0)), pl.BlockSpec((tc, dc), lambda t: (t, 1)),
                  pl.BlockSpec((CONV_HALO, dc), lambda t: (jnp.maximum(t * hb - 1, 0), 0)),
                  pl.BlockSpec((CONV_HALO, dc), lambda t: (jnp.maximum(t * hb - 1, 0), 1)),
                  pl.BlockSpec((32, dc), lambda t: (0, 0)), vec, vec, vec],
        out_specs=[pl.BlockSpec((tc, dc), lambda t: (t, 0)), pl.BlockSpec((tc, dc), lambda t: (t, 0))],
        out_shape=[_sds((r, dc), F32), _sds((r, dc), BF16)],
        scratch_shapes=[pltpu.VMEM((CONV_HALO + tc, dc), F32)],
        compiler_params=_params(("parallel",)),
    )(u, u, u, u, w32, cb, cg, cbe)


def _tri_mm_all(tri, xs):
    parts = [_split3(x) for x in xs]
    acc = [None] * len(xs)
    for t in range(3):
        for j in range(len(xs)):
            term = _mm(tri, parts[j][t])
            acc[j] = term if t == 0 else acc[j] + term
    return acc


def _gla_prep(qk_ref, gd_ref, gup, gb, n0, kc):
    rows = [slice(j * CHUNK, (j + 1) * CHUNK) for j in range(kc)]
    ri = lax.broadcasted_iota(jnp.int32, (CHUNK, CHUNK), 0)
    ci = lax.broadcasted_iota(jnp.int32, (CHUNK, CHUNK), 1)
    low = (ri >= ci).astype(BF16)
    hk = GLA_HEADS * GLA_DK
    gds = [gd_ref[rw, :] for rw in rows]
    zs = [_mm(g.astype(BF16), gup) + gb for g in gds]
    reals = [(n0 + j) * CHUNK + lax.broadcasted_iota(jnp.int32, (CHUNK, 1), 0) >= PAD_FRONT for j in range(kc)]
    lgs = [jnp.where(reals[j], _log_sigmoid(zs[j]) * (1.0 / GLA_TAU), 0.0) for j in range(kc)]
    bs = _tri_mm_all(low, lgs)
    out = []
    for j in range(kc):
        b, bl = bs[j], _rowsum(lgs[j])
        q = qk_ref[rows[j], :hk] * QK_SCALE
        k = qk_ref[rows[j], hk:]
        eb, enb, ebl = jnp.exp(b), jnp.exp(-b), jnp.exp(bl - b)
        out.append(dict(rows=rows[j], gd=gds[j], z=zs[j], real=reals[j], eb=eb, enb=enb, ebl=ebl, gam=jnp.exp(bl),
                        qe=q * eb, ke=k * enb, kd=k * ebl))
    return out, ri, ci


def _gla_heads(p, v_ref):
    ops = []
    for h in range(GLA_HEADS):
        hp, h2 = divmod(h, 2)
        ls = slice(hp * LANES, (hp + 1) * LANES)
        m = _head_mask(h2)
        ops.append(dict(ls=ls, m=m, vs=slice(h * GLA_DV, (h + 1) * GLA_DV),
                        qe=jnp.where(m, p["qe"][:, ls], 0.0).astype(BF16),
                        kd=jnp.where(m, p["kd"][:, ls], 0.0).astype(BF16),
                        ke=p["ke"][:, ls].astype(BF16),
                        v=v_ref[p["rows"], h * GLA_DV:(h + 1) * GLA_DV].astype(BF16)))
    return ops


def _head_mask(h2):
    lane = lax.broadcasted_iota(jnp.int32, (1, LANES), 1)
    return (lane < GLA_DK) if h2 == 0 else (lane >= GLA_DK)


def _gla_fwd(u, gup, gb, gn, bsz, nc, kc):
    r = u.shape[0]
    hv = GLA_HEADS * GLA_DV
    ns = nc // kc

    def body(qk_ref, v_ref, r_ref, gd_ref, gup_ref, gb_ref, gn_ref, go_ref, sta_ref, st_ref):
        t = pl.program_id(1)

        @pl.when(t == 0)
        def _():
            st_ref[...] = jnp.zeros_like(st_ref)

        ps, ri, ci = _gla_prep(qk_ref, gd_ref, gup_ref[...], gb_ref[...], t * kc, kc)
        tril = ri >= ci
        items = [(j, h) for j in range(kc) for h in range(GLA_HEADS)]
        ops = [_gla_heads(p, v_ref) for p in ps]
        a = {jh: jnp.where(tril, _mm_nt(ops[jh[0]][jh[1]]["qe"], ops[jh[0]][jh[1]]["ke"]), 0.0).astype(BF16) for jh in items}
        oi = {jh: _mm(a[jh], ops[jh[0]][jh[1]]["v"]) for jh in items}
        inc = {jh: _mm_tn(ops[jh[0]][jh[1]]["v"], ops[jh[0]][jh[1]]["kd"]) for jh in items}
        sts = [st_ref[h] for h in range(GLA_HEADS)]
        for j, h in items:
            op, p = ops[j][h], ps[j]
            st = sts[h]
            sta_ref[j, h] = st
            o = oi[j, h] + _mm_nt(op["qe"], st.astype(BF16))
            sts[h] = st * p["gam"][:, op["ls"]] + inc[j, h]
            rs = lax.rsqrt(jnp.mean(o * o, axis=-1, keepdims=True) + LN_EPS)
            rr = r_ref[p["rows"], op["vs"]]
            go_ref[p["rows"], op["vs"]] = (o * rs * gn_ref[...] * (rr * _sigmoid(rr))).astype(BF16)
        for h in range(GLA_HEADS):
            st_ref[h] = sts[h]

    rowblk = lambda col: (lambda b, t: (b * ns + t, col))
    const = lambda b, t: (0, 0)
    return pl.pallas_call(
        body, name="gla_fwd", grid=(bsz, ns),
        in_specs=[pl.BlockSpec((kc * CHUNK, QK_WIDTH), rowblk(U_QK)), pl.BlockSpec((kc * CHUNK, hv), rowblk(U_V)),
                  pl.BlockSpec((kc * CHUNK, hv), rowblk(U_R)), pl.BlockSpec((kc * CHUNK, LANES), rowblk(U_GD)),
                  pl.BlockSpec((LANES, GUP_WIDTH), const), pl.BlockSpec((1, GUP_WIDTH), const), pl.BlockSpec((1, GLA_DV), const)],
        out_specs=[pl.BlockSpec((kc * CHUNK, hv), rowblk(0)),
                   pl.BlockSpec((kc, GLA_HEADS, LANES, LANES), lambda b, t: (b * ns + t, 0, 0, 0))],
        out_shape=[_sds((r, hv), BF16), _sds((bsz * nc, GLA_HEADS, LANES, LANES), F32)],
        scratch_shapes=[pltpu.VMEM((GLA_HEADS, LANES, LANES), F32)],
        compiler_params=_params(("parallel", "arbitrary")),
    )(u, u, u, u, gup, gb, gn)


def _outproj_fwd(s0, co, go, w_out, g1, b1, tm):
    r, d = s0.shape
    dc = co.shape[1]

    def body(s0_ref, co_ref, go_ref, w_ref, g_ref, b_ref, p1_ref, s1_ref, s1b_ref):
        nb = 4 if tm % 64 == 0 else 1
        blocks = [slice(k * (tm // nb), (k + 1) * (tm // nb)) for k in range(nb)]
        mixes = [_mm(co_ref[rows, :], w_ref[0:dc, :]) + _mm(go_ref[rows, :], w_ref[dc:2 * dc, :]) for rows in blocks]
        for rows, mix in zip(blocks, mixes):
            p1 = ALPHA * s0_ref[rows, :] + mix
            p1_ref[rows, :] = p1
            xhat, _ = _ln(p1)
            s1 = xhat * g_ref[...] + b_ref[...]
            s1_ref[rows, :] = s1
            s1b_ref[rows, :] = s1.astype(BF16)

    row = lambda w: pl.BlockSpec((tm, w), lambda i: (i, 0))
    vec = pl.BlockSpec((1, d), lambda i: (0, 0))
    return pl.pallas_call(
        body, name="outproj_fwd", grid=(r // tm,),
        in_specs=[row(d), row(dc), row(dc), pl.BlockSpec((2 * dc, d), lambda i: (0, 0)), vec, vec],
        out_specs=[row(d), row(d), row(d)],
        out_shape=[_sds((r, d), F32), _sds((r, d), F32), _sds((r, d), BF16)],
        compiler_params=_params(("parallel",)),
    )(s0, co, go, w_out, g1, b1)


def _mlp_fwd(s1, s1b, w1g, w2, g2, b2, tgt, tp, tm, ns):
    r, d = s1.shape
    nh, _, th = w1g.shape
    nj = nh // ns

    def body(s1_ref, sb_ref, w1_ref, w2_ref, g_ref, b_ref, t_ref, hm_ref, dp2_ref, dpb_ref, loss_ref, dg_ref, db_ref, acc_ref):
        i = pl.program_id(0)
        j = pl.program_id(1)

        @pl.when(jnp.logical_and(i == 0, j == 0))
        def _():
            loss_ref[...] = jnp.zeros_like(loss_ref)
            dg_ref[...] = jnp.zeros_like(dg_ref)
            db_ref[...] = jnp.zeros_like(db_ref)

        @pl.when(j == 0)
        def _():
            acc_ref[...] = jnp.zeros_like(acc_ref)

        def mlp_rows(rows):
            hs = [_mm(sb_ref[rows, :], w1_ref[s]) for s in range(ns)]
            acc = acc_ref[rows, :]
            for s in range(ns):
                hm_ref[rows, s * th:(s + 1) * th] = hs[s].astype(BF16)
                act = jnp.square(jnp.maximum(hs[s], 0.0))
                acc = acc + _mm(act.astype(BF16), w2_ref[s * th:(s + 1) * th, :])
            return acc

        @pl.when(j < nj - 1)
        def _():
            acc_ref[...] = mlp_rows(slice(None))

        @pl.when(j == nj - 1)
        def _():
            halves = [slice(0, tm // 2), slice(tm // 2, tm)]
            accs = [mlp_rows(rows) for rows in halves]
            isx = _row_in_seq(i, tm, tp) >= X_OFF
            tg = t_ref[...]
            tg = jnp.where(i == 0, pltpu.roll(tg, X_OFF, 0), tg)
            for rows, acc in zip(halves, accs):
                p2 = ALPHA * s1_ref[rows, :] + acc
                xhat, rstd = _ln(p2)
                s2 = xhat * g_ref[...] + b_ref[...]
                err = jnp.where(isx[rows], s2 - tg[rows], 0.0)
                loss_ref[...] += 0.5 * jnp.sum(jnp.mean(err * err, axis=-1, keepdims=True))
                dy = err * (1.0 / d)
                dg_ref[...] += _rowsum(dy * xhat)
                db_ref[...] += _rowsum(dy)
                dp2 = _ln_bwd(dy * g_ref[...], xhat, rstd)
                dp2_ref[rows, :] = dp2
                dpb_ref[rows, :] = dp2.astype(BF16)

    row = pl.BlockSpec((tm, d), lambda i, j: (i, 0))
    vec = pl.BlockSpec((1, d), lambda i, j: (0, 0))
    tgt_row = pl.BlockSpec((pl.Element(tm), pl.Element(d)),
                           lambda i, j: (pl.multiple_of(jnp.maximum(i * tm - X_OFF * ((i * tm) // tp + 1), 0), CHUNK), 0))
    return pl.pallas_call(
        body, name="mlp_fwd", grid=(r // tm, nj),
        in_specs=[row, row, pl.BlockSpec((ns, d, th), lambda i, j: (j, 0, 0)), pl.BlockSpec((ns * th, d), lambda i, j: (j, 0)),
                  vec, vec, tgt_row],
        out_specs=[pl.BlockSpec((tm, ns * th), lambda i, j: (i, j)), row, row,
                   pl.BlockSpec((8, LANES), lambda i, j: (0, 0)), vec, vec],
        out_shape=[_sds((r, nh * th), BF16), _sds((r, d), F32), _sds((r, d), BF16), _sds((8, LANES), F32),
                   _sds((1, d), F32), _sds((1, d), F32)],
        scratch_shapes=[pltpu.VMEM((tm, d), F32)],
        compiler_params=_params(("arbitrary", "arbitrary")),
    )(s1, s1b, w1g, w2, g2, b2, tgt)


def _mlp_bwd_act(dp2, dpb, hm, w1g, w2, p1, g1, tm, ns):
    r, d = dp2.shape
    nh, _, th = w1g.shape
    nj = nh // ns

    def body(dp2_ref, dpb_ref, hm_ref, w1_ref, w2_ref, p1_ref, g_ref, dh_ref, dp1_ref, dg_ref, db_ref, acc_ref):
        i = pl.program_id(0)
        j = pl.program_id(1)

        @pl.when(jnp.logical_and(i == 0, j == 0))
        def _():
            dg_ref[...] = jnp.zeros_like(dg_ref)
            db_ref[...] = jnp.zeros_like(db_ref)

        @pl.when(j == 0)
        def _():
            acc_ref[...] = jnp.zeros_like(acc_ref)

        def mlp_rows(rows):
            dacts = [_mm_nt(dpb_ref[rows, :], w2_ref[s * th:(s + 1) * th, :]) for s in range(ns)]
            acc = acc_ref[rows, :]
            for s in range(ns):
                cols = slice(s * th, (s + 1) * th)
                dh = (dacts[s] * (2.0 * jnp.maximum(hm_ref[rows, cols].astype(F32), 0.0))).astype(BF16)
                dh_ref[rows, cols] = dh
                acc = acc + _mm_nt(dh, w1_ref[s])
            return acc

        @pl.when(j < nj - 1)
        def _():
            acc_ref[...] = mlp_rows(slice(None))

        @pl.when(j == nj - 1)
        def _():
            halves = [slice(0, tm // 2), slice(tm // 2, tm)]
            accs = [mlp_rows(rows) for rows in halves]
            for rows, acc in zip(halves, accs):
                ds1 = ALPHA * dp2_ref[rows, :] + acc
                xhat, rstd = _ln(p1_ref[rows, :])
                dg_ref[...] += _rowsum(ds1 * xhat)
                db_ref[...] += _rowsum(ds1)
                dp1_ref[rows, :] = _ln_bwd(ds1 * g_ref[...], xhat, rstd)

    row = pl.BlockSpec((tm, d), lambda i, j: (i, 0))
    vec = pl.BlockSpec((1, d), lambda i, j: (0, 0))
    blk = pl.BlockSpec((tm, ns * th), lambda i, j: (i, j))
    return pl.pallas_call(
        body, name="mlp_bwd_act", grid=(r // tm, nj),
        in_specs=[row, row, blk, pl.BlockSpec((ns, d, th), lambda i, j: (j, 0, 0)),
                  pl.BlockSpec((ns * th, d), lambda i, j: (j, 0)), row, vec],
        out_specs=[blk, row, vec, vec],
        out_shape=[_sds((r, nh * th), BF16), _sds((r, d), F32), _sds((1, d), F32), _sds((1, d), F32)],
        scratch_shapes=[pltpu.VMEM((tm, d), F32)],
        compiler_params=_params(("arbitrary", "arbitrary")),
    )(dp2, dpb, hm, w1g, w2, p1, g1)


def _mlp_bwd_w(s1b, hm, dh, dpb, nh, tm, ns):
    r, d = s1b.shape
    th = hm.shape[1] // nh

    def body(s1_ref, hm_ref, dh_ref, dp2_ref, dw1_ref, dw2_ref, a1_ref, a2_ref):
        i = pl.program_id(1)

        @pl.when(i == 0)
        def _():
            a1_ref[...] = jnp.zeros_like(a1_ref)
            a2_ref[...] = jnp.zeros_like(a2_ref)

        for s in range(ns):
            a1_ref[s] += _mm_tn(s1_ref[...], dh_ref[:, s * th:(s + 1) * th])
        for s in range(ns):
            act = jnp.square(jnp.maximum(hm_ref[:, s * th:(s + 1) * th].astype(F32), 0.0)).astype(BF16)
            a2_ref[s] += _mm_tn(act, dp2_ref[...])

        @pl.when(i == pl.num_programs(1) - 1)
        def _():
            dw1_ref[...] = a1_ref[...].astype(BF16)
            dw2_ref[...] = a2_ref[...].astype(BF16)

    row = pl.BlockSpec((tm, d), lambda j, i: (i, 0))
    blk = pl.BlockSpec((tm, ns * th), lambda j, i: (i, j))
    return pl.pallas_call(
        body, name="mlp_bwd_w", grid=(nh // ns, r // tm),
        in_specs=[row, blk, blk, row],
        out_specs=[pl.BlockSpec((ns, d, th), lambda j, i: (j, 0, 0)), pl.BlockSpec((ns, th, d), lambda j, i: (j, 0, 0))],
        out_shape=[_sds((nh, d, th), BF16), _sds((nh, th, d), BF16)],
        scratch_shapes=[pltpu.VMEM((ns, d, th), F32), pltpu.VMEM((ns, th, d), F32)],
        compiler_params=_params(("parallel", "arbitrary")),
    )(s1b, hm, dh, dpb)


def _outproj_bwd(dp1, co, go, w_out, tm):
    r, d = dp1.shape
    dc = co.shape[1]

    def body(dp_ref, co_ref, go_ref, w_ref, dmi_ref, dw_ref, acc_ref):
        i = pl.program_id(0)

        @pl.when(i == 0)
        def _():
            acc_ref[...] = jnp.zeros_like(acc_ref)

        dpb = dp_ref[...].astype(BF16)
        dmi_ref[...] = _mm_nt(dpb, w_ref[...])
        acc_ref[0:dc, :] += _mm_tn(co_ref[...], dpb)
        acc_ref[dc:2 * dc, :] += _mm_tn(go_ref[...], dpb)

        @pl.when(i == pl.num_programs(0) - 1)
        def _():
            dw_ref[...] = acc_ref[...].astype(BF16)

    row = lambda w: pl.BlockSpec((tm, w), lambda i: (i, 0))
    full = pl.BlockSpec((2 * dc, d), lambda i: (0, 0))
    return pl.pallas_call(
        body, name="outproj_bwd", grid=(r // tm,),
        in_specs=[row(d), row(dc), row(dc), full],
        out_specs=[row(2 * dc), full],
        out_shape=[_sds((r, 2 * dc), F32), _sds((2 * dc, d), BF16)],
        scratch_shapes=[pltpu.VMEM((2 * dc, d), F32)],
        compiler_params=_params(("arbitrary",)),
    )(dp1, co, go, w_out)


def _gla_bwd(u, dmi, sta, gup, gb, gn, bsz, nc, kc):
    r = u.shape[0]
    hv = GLA_HEADS * GLA_DV
    hk = GLA_HEADS * GLA_DK
    ns = nc // kc

    def body(qk_ref, v_ref, r_ref, gd_ref, dgo_ref, sta_ref, gup_ref, gb_ref, gn_ref,
             dqk_ref, dv_ref, dr_ref, dgd_ref, dgn_ref, dgb_ref, dgup_ref, dst_ref):
        bi = pl.program_id(0)
        t = pl.program_id(1)

        @pl.when(jnp.logical_and(bi == 0, t == 0))
        def _():
            dgn_ref[...] = jnp.zeros_like(dgn_ref)
            dgb_ref[...] = jnp.zeros_like(dgb_ref)
            dgup_ref[...] = jnp.zeros_like(dgup_ref)

        @pl.when(t == 0)
        def _():
            dst_ref[...] = jnp.zeros_like(dst_ref)

        ps, ri, ci = _gla_prep(qk_ref, gd_ref, gup_ref[...], gb_ref[...], (ns - 1 - t) * kc, kc)
        tril = ri >= ci
        items = [(j, h) for j in reversed(range(kc)) for h in range(GLA_HEADS)]
        ops = [_gla_heads(p, v_ref) for p in ps]
        op = lambda jh: ops[jh[0]][jh[1]]
        st = {jh: sta_ref[jh[0], jh[1]] for jh in items}
        stb = {jh: st[jh].astype(BF16) for jh in items}
        a = {jh: jnp.where(tril, _mm_nt(op(jh)["qe"], op(jh)["ke"]), 0.0).astype(BF16) for jh in items}
        o1 = {jh: _mm(a[jh], op(jh)["v"]) for jh in items}
        o2 = {jh: _mm_nt(op(jh)["qe"], stb[jh]) for jh in items}
        dob = {}
        dgn = jnp.zeros((1, GLA_DV), F32)
        for jh in items:
            rows, vs = ps[jh[0]]["rows"], op(jh)["vs"]
            o = o1[jh] + o2[jh]
            rr = r_ref[rows, vs]
            sr = _sigmoid(rr)
            rs = lax.rsqrt(jnp.mean(o * o, axis=-1, keepdims=True) + LN_EPS)
            y = o * rs
            dgo = dgo_ref[rows, vs]
            don = dgo * (rr * sr)
            dr_ref[rows, vs] = (dgo * (y * gn_ref[...]) * (sr * (1.0 + rr * (1.0 - sr)))).astype(BF16)
            dgn = dgn + _rowsum(don * y)
            dxn = don * gn_ref[...]
            dob[jh] = (rs * (dxn - y * jnp.mean(dxn * y, axis=-1, keepdims=True))).astype(BF16)
        da = {jh: jnp.where(tril, _mm_nt(dob[jh], op(jh)["v"]), 0.0).astype(BF16) for jh in items}
        dv1 = {jh: _mm_tn(a[jh], dob[jh]) for jh in items}
        dqe1 = {jh: _mm(da[jh], op(jh)["ke"]) for jh in items}
        dqe2 = {jh: _mm(dob[jh], stb[jh]) for jh in items}
        dke1 = {jh: _mm_tn(da[jh], op(jh)["qe"]) for jh in items}
        inc = {jh: _mm_tn(dob[jh], op(jh)["qe"]) for jh in items}
        dsts = [dst_ref[h] for h in range(GLA_HEADS)]
        dkd1, dgam1 = {}, {}
        for jh in items:
            j, h = jh
            dst = dsts[h]
            dstb = dst.astype(BF16)
            dv_ref[ps[j]["rows"], op(jh)["vs"]] = (dv1[jh] + _mm_nt(op(jh)["kd"], dstb)).astype(BF16)
            dkd1[jh] = _mm(op(jh)["v"], dstb)
            dgam1[jh] = _rowsum(dst * st[jh])
            dsts[h] = dst * ps[j]["gam"][:, op(jh)["ls"]] + inc[jh]
        for h in range(GLA_HEADS):
            dst_ref[h] = dsts[h]
        upper = (ri <= ci).astype(BF16)
        dbs, dbls = [], []
        for j in range(kc):
            p = ps[j]
            tiles = [[op((j, 2 * hp + h2)) for h2 in range(2)] for hp in range(GLA_HEADS // 2)]
            head = lambda d, hp, h2: d[j, 2 * hp + h2]
            lanes = lambda f: jnp.concatenate([f(hp) for hp in range(GLA_HEADS // 2)], axis=1)
            dqe = lanes(lambda hp: sum(jnp.where(tiles[hp][h2]["m"], head(dqe1, hp, h2) + head(dqe2, hp, h2), 0.0)
                                       for h2 in range(2)))
            dke = lanes(lambda hp: head(dke1, hp, 0) + head(dke1, hp, 1))
            dkd = lanes(lambda hp: sum(jnp.where(tiles[hp][h2]["m"], head(dkd1, hp, h2), 0.0) for h2 in range(2)))
            dgam = lanes(lambda hp: head(dgam1, hp, 0) + head(dgam1, hp, 1))
            dqk_ref[p["rows"], :hk] = (dqe * p["eb"] * QK_SCALE).astype(BF16)
            dqk_ref[p["rows"], hk:] = (dke * p["enb"] + dkd * p["ebl"]).astype(BF16)
            dkdkd = dkd * p["kd"]
            dbs.append(dqe * p["qe"] - dke * p["ke"] - dkdkd)
            dbls.append(_rowsum(dkdkd) + dgam * p["gam"])
        dlgs = _tri_mm_all(upper, dbs)
        dzb = []
        dgb = jnp.zeros((1, hk), F32)
        for j in range(kc):
            p = ps[j]
            dz = jnp.where(p["real"], (dlgs[j] + dbls[j]) * (1.0 / GLA_TAU) * _sigmoid(-p["z"]), 0.0)
            dgb = dgb + _rowsum(dz)
            dzb.append(dz.astype(BF16))
        dgup = sum(_mm_tn(ps[j]["gd"].astype(BF16), dzb[j]) for j in range(kc))
        for j in range(kc):
            dgd_ref[ps[j]["rows"], :] = _mm_nt(dzb[j], gup_ref[...]).astype(BF16)
        dgb_ref[...] += dgb
        dgup_ref[...] += dgup
        dgn_ref[...] += dgn

    rowblk = lambda col: (lambda b, t: (b * ns + ns - 1 - t, col))
    const = lambda b, t: (0, 0)
    return pl.pallas_call(
        body, name="gla_bwd", grid=(bsz, ns),
        in_specs=[pl.BlockSpec((kc * CHUNK, QK_WIDTH), rowblk(U_QK)), pl.BlockSpec((kc * CHUNK, hv), rowblk(U_V)),
                  pl.BlockSpec((kc * CHUNK, hv), rowblk(U_R)), pl.BlockSpec((kc * CHUNK, LANES), rowblk(U_GD)),
                  pl.BlockSpec((kc * CHUNK, hv), rowblk(DMI_GLA)),
                  pl.BlockSpec((kc, GLA_HEADS, LANES, LANES), lambda b, t: (b * ns + ns - 1 - t, 0, 0, 0)),
                  pl.BlockSpec((LANES, GUP_WIDTH), const), pl.BlockSpec((1, GUP_WIDTH), const), pl.BlockSpec((1, GLA_DV), const)],
        out_specs=[pl.BlockSpec((kc * CHUNK, 2 * hk), rowblk(0)), pl.BlockSpec((kc * CHUNK, hv), rowblk(0)),
                   pl.BlockSpec((kc * CHUNK, hv), rowblk(0)), pl.BlockSpec((kc * CHUNK, LANES), rowblk(0)),
                   pl.BlockSpec((1, GLA_DV), const), pl.BlockSpec((1, GUP_WIDTH), const),
                   pl.BlockSpec((LANES, GUP_WIDTH), const)],
        out_shape=[_sds((r, 2 * hk), BF16), _sds((r, hv), BF16), _sds((r, hv), BF16), _sds((r, LANES), BF16),
                   _sds((1, GLA_DV), F32), _sds((1, GUP_WIDTH), F32), _sds((LANES, GUP_WIDTH), F32)],
        scratch_shapes=[pltpu.VMEM((GLA_HEADS, LANES, LANES), F32)],
        compiler_params=_params(("arbitrary", "arbitrary")),
    )(u, u, u, u, dmi, sta, gup, gb, gn)


def _conv_bwd(u, c, dmi, w32, cg, cbe, tp, tc, dc):
    r = u.shape[0]
    hb = tc // CONV_HALO
    nhalo = r // CONV_HALO

    def dconv(cv, dco, cg_ref, cbe_ref):
        xhat, rstd = _ln(cv)
        cn = xhat * cg_ref[...] + cbe_ref[...]
        sg = _sigmoid(cn)
        dcn = dco * (sg * (1.0 + cn * (1.0 - sg)))
        return _ln_bwd(dcn * cg_ref[...], xhat, rstd), dcn, xhat

    def body(a_ref, g_ref, ah_ref, gh_ref, c_ref, dco_ref, ch_ref, dcoh_ref, w_ref, cg_ref, cbe_ref,
             du_ref, dw_ref, dcb_ref, dcg_ref, dcbe_ref, hs_ref, dcs_ref, dw8_ref):
        t = pl.program_id(0)

        @pl.when(t == 0)
        def _():
            dw8_ref[...] = jnp.zeros_like(dw8_ref)
            dcb_ref[...] = jnp.zeros_like(dcb_ref)
            dcg_ref[...] = jnp.zeros_like(dcg_ref)
            dcbe_ref[...] = jnp.zeros_like(dcbe_ref)

        first = lax.rem(t * tc, tp) == 0
        last = lax.rem((t + 1) * tc, tp) == 0
        hh = ah_ref[...] * _sigmoid(gh_ref[...])
        hs_ref[0:CONV_HALO, :] = jnp.where(first, 0.0, hh)
        hs_ref[CONV_HALO:CONV_HALO + tc, :] = a_ref[...] * _sigmoid(g_ref[...])
        dch, _, _ = dconv(ch_ref[...], dcoh_ref[...], cg_ref, cbe_ref)
        dcs_ref[tc:tc + CONV_HALO, :] = jnp.where(last, 0.0, dch)

        lrows = tc // 4

        def sub1(k, carry):
            r0 = pl.multiple_of(k * lrows, 8)
            dcv, dcn, xhat = dconv(c_ref[pl.ds(r0, lrows), :], dco_ref[pl.ds(r0, lrows), :], cg_ref, cbe_ref)
            dcs_ref[pl.ds(r0, lrows), :] = dcv
            dcb_ref[...] += _rowsum(dcv)
            dcg_ref[...] += _rowsum(dcn * xhat)
            dcbe_ref[...] += _rowsum(dcn)
            return carry

        lax.fori_loop(0, 4, sub1, 0)

        def sub2(k, carry):
            r0 = pl.multiple_of(k * CONV_SUB, CONV_SUB)
            dwin = dcs_ref[pl.ds(r0, CONV_WIN), :]
            dh = _conv_taps(dwin, lambda o: w_ref[CONV_WIDTH - 1 - o:CONV_WIDTH - o, :], 0)
            av = a_ref[pl.ds(r0, CONV_SUB), :]
            sg = _sigmoid(g_ref[pl.ds(r0, CONV_SUB), :])
            du_ref[pl.ds(r0, CONV_SUB), 0:dc] = (dh * sg).astype(BF16)
            du_ref[pl.ds(r0, CONV_SUB), dc:2 * dc] = (dh * av * sg * (1.0 - sg)).astype(BF16)
            hwin = hs_ref[pl.ds(r0, CONV_WIN), :]
            dcv = dwin[0:CONV_SUB, :]
            for rho in range(8):
                offs = [o for o in range(2, 2 + CONV_WIDTH) if o % 8 == rho]
                rolled = hwin if rho == 0 else pltpu.roll(hwin, CONV_WIN - rho, 0)
                for o in offs:
                    m8 = o - rho
                    prod = dcv * rolled[m8:m8 + CONV_SUB, :]
                    dw8_ref[8 * (o - 2):8 * (o - 1), :] += jnp.sum(prod.reshape(CONV_SUB // 8, 8, dc), axis=0)
            return carry

        lax.fori_loop(0, tc // CONV_SUB, sub2, 0)

        @pl.when(t == pl.num_programs(0) - 1)
        def _():
            dw_ref[...] = jnp.zeros_like(dw_ref)
            for j in range(CONV_WIDTH):
                dw_ref[j:j + 1, :] = _rowsum(dw8_ref[8 * j:8 * (j + 1), :])

    vec = pl.BlockSpec((1, dc), lambda t: (0, 0))
    prev = lambda col: (lambda t: (jnp.maximum(t * hb - 1, 0), col))
    nxt = lambda col: (lambda t: (jnp.minimum((t + 1) * hb, nhalo - 1), col))
    return pl.pallas_call(
        body, name="conv_bwd", grid=(r // tc,),
        in_specs=[pl.BlockSpec((tc, dc), lambda t: (t, 0)), pl.BlockSpec((tc, dc), lambda t: (t, 1)),
                  pl.BlockSpec((CONV_HALO, dc), prev(0)), pl.BlockSpec((CONV_HALO, dc), prev(1)),
                  pl.BlockSpec((tc, dc), lambda t: (t, 0)), pl.BlockSpec((tc, dc), lambda t: (t, 0)),
                  pl.BlockSpec((CONV_HALO, dc), nxt(0)), pl.BlockSpec((CONV_HALO, dc), nxt(0)),
                  pl.BlockSpec((32, dc), lambda t: (0, 0)), vec, vec],
        out_specs=[pl.BlockSpec((tc, 2 * dc), lambda t: (t, 0)), pl.BlockSpec((32, dc), lambda t: (0, 0)), vec, vec, vec],
        out_shape=[_sds((r, 2 * dc), BF16), _sds((32, dc), F32), _sds((1, dc), F32), _sds((1, dc), F32), _sds((1, dc), F32)],
        scratch_shapes=[pltpu.VMEM((CONV_HALO + tc, dc), F32), pltpu.VMEM((tc + CONV_HALO, dc), F32),
                        pltpu.VMEM((8 * 32, dc), F32)],
        compiler_params=_params(("arbitrary",)),
    )(u, u, u, u, c, dmi, c, dmi, w32, cg, cbe)


def _inproj_bwd(dp1, dus, xsrc, g_in, w_in, tp, seq, tx):
    r, d = dp1.shape
    widths = [x.shape[1] for x in dus]
    offs = [sum(widths[:k]) for k in range(len(widths))]
    n = w_in.shape[0]
    nd = len(dus)
    head = tx == 0
    rows = X_OFF if head else tx

    def body(*refs):
        dp_ref = refs[0]
        du_refs = refs[1:1 + nd]
        x_ref, g_ref, w_ref, out_ref, dg_ref, db_ref = refs[1 + nd:]
        i = pl.program_id(0)

        @pl.when(i == 0)
        def _():
            dg_ref[...] = jnp.zeros_like(dg_ref)
            db_ref[...] = jnp.zeros_like(db_ref)
            if head:
                out_ref[...] = jnp.zeros_like(out_ref)

        ds0 = ALPHA * dp_ref[...]
        for k in range(nd):
            ds0 = ds0 + _mm(du_refs[k][...], w_ref[offs[k]:offs[k] + widths[k], :])
        if head:
            ds0 = jnp.where(lax.broadcasted_iota(jnp.int32, (X_OFF, 1), 0) >= PAD_FRONT, ds0, 0.0)
        xhat, rstd = _ln(x_ref[...])
        dg_ref[...] += _rowsum(ds0 * xhat)
        db_ref[...] += _rowsum(ds0)
        dx = _ln_bwd(ds0 * g_ref[...], xhat, rstd)
        if head:
            out_ref[...] += dx[PAD_FRONT:X_OFF, :]
        else:
            out_ref[...] = dx

    if head:
        nb = tp // X_OFF
        row = lambda w: pl.BlockSpec((X_OFF, w), lambda i: (i * nb, 0))
        xspec = pl.BlockSpec((X_OFF, d), lambda i: (0, 0))
        ospec, oshape, steps = pl.BlockSpec((N_META, d), lambda i: (0, 0)), _sds((N_META, d), F32), r // tp
    else:
        start = _x_tile_row(tp, seq, tx)
        row = lambda w: pl.BlockSpec((pl.Element(tx), pl.Element(w)), lambda i: (start(i), 0))
        xspec = pl.BlockSpec((tx, d), lambda i: (i, 0))
        ospec, oshape, steps = xspec, _sds(xsrc.shape, F32), xsrc.shape[0] // tx
    vec = pl.BlockSpec((1, d), lambda i: (0, 0))
    return pl.pallas_call(
        body, name="inproj_bwd_head" if head else "inproj_bwd_x", grid=(steps,),
        in_specs=[row(d)] + [row(w) for w in widths] + [xspec, vec, pl.BlockSpec((n, d), lambda i: (0, 0))],
        out_specs=[ospec, vec, vec],
        out_shape=[oshape, _sds((1, d), F32), _sds((1, d), F32)],
        compiler_params=_params(("arbitrary",)),
    )(dp1, *dus, xsrc, g_in, w_in)


def _inproj_bwd_w(s0, dus, tm):
    r, d = s0.shape
    widths = [x.shape[1] for x in dus]
    offs = [sum(widths[:k]) for k in range(len(widths))]
    nd = len(dus)

    def body(*refs):
        s_ref = refs[0]
        du_refs = refs[1:1 + nd]
        dw_ref, acc_ref = refs[1 + nd:]
        i = pl.program_id(0)

        @pl.when(i == 0)
        def _():
            acc_ref[...] = jnp.zeros_like(acc_ref)

        for k in range(nd):
            acc_ref[offs[k]:offs[k] + widths[k], :] += _mm_tn(du_refs[k][...], s_ref[...])

        @pl.when(i == pl.num_programs(0) - 1)
        def _():
            dw_ref[...] = acc_ref[...].astype(BF16)

    row = lambda w: pl.BlockSpec((tm, w), lambda i: (i, 0))
    return pl.pallas_call(
        body, name="inproj_bwd_w", grid=(r // tm,),
        in_specs=[row(d)] + [row(w) for w in widths],
        out_specs=pl.BlockSpec((sum(widths), d), lambda i: (0, 0)),
        out_shape=_sds((sum(widths), d), BF16),
        scratch_shapes=[pltpu.VMEM((sum(widths), d), F32)],
        compiler_params=_params(("arbitrary",)),
    )(s0, *dus)


def _local_step(x, tgt, meta, ln_in_g, ln_in_b, w_in, conv_w, conv_b, conv_ln_g, conv_ln_b, gate_up, gate_bias,
                gla_norm_g, late_weights, ln1_g, ln1_b, ln2_g, ln2_b, push):
    bsz, seq, d = x.shape
    tp = X_OFF + seq
    assert tp % CHUNK == 0
    nc = tp // CHUNK
    dc = conv_b.shape[1]
    tmm = tc = _pick_tile(tp, (704, 128, 64))
    tw = _pick_tile(bsz * tp, (1408, 128, 64))
    tx = _pick_tile(seq, (512, 64))
    kc = _pick_tile(nc, (11, 3, 2, 1))
    ns = 2

    x2 = x.reshape(bsz * seq, d)
    head = jnp.pad(meta, ((PAD_FRONT, 0), (0, 0)))
    tgt_p = tgt.reshape(bsz * seq, d)
    w32 = jnp.pad(conv_w, ((0, 32 - CONV_WIDTH), (0, 0)))
    gup = jnp.pad(gate_up, ((0, LANES - GLA_RANK), (0, 0))).astype(BF16)

    s0, s0b = _ln_in_x(x2, ln_in_g, ln_in_b, tp, seq, tx)
    s0, s0b = _ln_in_head(head, ln_in_g, ln_in_b, s0, s0b, tp)
    u = _inproj_fwd(s0b, w_in, tmm)
    c, co = _conv_fwd(u, w32, conv_b, conv_ln_g, conv_ln_b, tp, tc, dc)
    go, sta = _gla_fwd(u, gup, gate_bias, gla_norm_g, bsz, nc, kc)
    w_out, w1g, w2 = late_weights
    nh = w1g.shape[0]
    p1, s1, s1b = _outproj_fwd(s0, co, go, w_out, ln1_g, ln1_b, tmm)
    hm, dp2, dpb, loss, dg2, db2 = _mlp_fwd(s1, s1b, w1g, w2, ln2_g, ln2_b, tgt_p, tp, tmm, ns)

    dh, dp1, dg1, db1 = _mlp_bwd_act(dp2, dpb, hm, w1g, w2, p1, ln1_g, tmm, ns)
    dw1, dw2 = _mlp_bwd_w(s1b, hm, dh, dpb, nh, tw, ns)
    push("ff", (dw1, dw2))
    dmi, dwo = _outproj_bwd(dp1, co, go, w_out, tw)
    push("out", (dwo,))
    dqk, dv, dr, dgd, dgn, dgb, dgup = _gla_bwd(u, dmi, sta, gup, gate_bias, gla_norm_g, bsz, nc, kc)
    dcv, dcw, dcb, dcg, dcbe = _conv_bwd(u, c, dmi, w32, conv_ln_g, conv_ln_b, tp, tc, dc)
    dus = [dcv, dqk, dv, dr, dgd]
    dwi = _inproj_bwd_w(s0b, dus, tw)
    push("in", (dwi,))
    gx, dgx, dbx = _inproj_bwd(dp1, dus, x2, ln_in_g, w_in, tp, seq, tx)
    dmeta, dgh, dbh = _inproj_bwd(dp1, dus, head, ln_in_g, w_in, tp, seq, 0)

    return dict(loss=loss[0, 0], grad_x=gx.reshape(bsz, seq, d), meta_tokens=dmeta, ln_in_g=dgx + dgh, ln_in_b=dbx + dbh,
                conv_w=dcw[:CONV_WIDTH], conv_b=dcb, conv_ln_g=dcg, conv_ln_b=dcbe,
                gate_up=dgup[:GLA_RANK], gate_bias=dgb, gla_norm_g=dgn, ln1_g=dg1, ln1_b=db1, ln2_g=dg2, ln2_b=db2)


def _exchange(arrays, scatter, name):
    na = len(arrays)
    npeer = N_DEV - 1

    def body(*refs):
        srcs = refs[:na]
        outs = refs[na:2 * na]
        send_sems, recv_sems, local_sems = refs[2 * na:]
        xi, yi, ci = (lax.axis_index(a) for a in MESH_AXES)
        me = 4 * xi + 2 * yi + ci
        copies = []
        for a in range(na):
            own = srcs[a].at[me] if scatter[a] else srcs[a]
            cp = pltpu.make_async_copy(own, outs[a].at[me], local_sems.at[a])
            cp.start()
            copies.append(cp)
        remote = []
        for k in range(1, N_DEV):
            px, py, pc = xi ^ (k >> 2), yi ^ ((k >> 1) & 1), ci ^ (k & 1)
            peer = 4 * px + 2 * py + pc
            for a in range(na):
                src = srcs[a].at[peer] if scatter[a] else srcs[a]
                cp = pltpu.make_async_remote_copy(
                    src_ref=src, dst_ref=outs[a].at[me],
                    send_sem=send_sems.at[a * npeer + k - 1], recv_sem=recv_sems.at[a * npeer + k - 1],
                    device_id=(px, py, pc), device_id_type=pl.DeviceIdType.MESH)
                cp.start()
                remote.append(cp)
        for cp in remote:
            cp.wait()
        for cp in copies:
            cp.wait()

    out_shape = [_sds(a.shape if scatter[i] else (N_DEV,) + a.shape, a.dtype) for i, a in enumerate(arrays)]
    anyspec = pl.BlockSpec(memory_space=pl.ANY)
    return pl.pallas_call(
        body, name=name,
        in_specs=[anyspec] * na, out_specs=[anyspec] * na, out_shape=out_shape,
        scratch_shapes=[pltpu.SemaphoreType.DMA((na * npeer,)), pltpu.SemaphoreType.DMA((na * npeer,)),
                        pltpu.SemaphoreType.DMA((na,))],
    )(*arrays)


def _peers(xi, yi, ci):
    for k in range(1, N_DEV):
        px, py, pc = xi ^ (k >> 2), yi ^ ((k >> 1) & 1), ci ^ (k & 1)
        yield (px, py, pc), 4 * px + 2 * py + pc


def _sc_exchange(arrays, scatter, name, collective_id, after=None):
    na = len(arrays)
    npeer = N_DEV - 1
    ndep = 0 if after is None else 1

    def body(*refs):
        srcs = refs[:na]
        outs = refs[na + ndep:2 * na + ndep]
        send_sems, recv_sems, own_sems = refs[2 * na + ndep:]
        xi, yi, ci = (lax.axis_index(a) for a in MESH_AXES)
        me = 4 * xi + 2 * yi + ci
        barrier = pltpu.get_barrier_semaphore()
        for pos, _ in _peers(xi, yi, ci):
            pl.semaphore_signal(barrier, inc=1, device_id=pos, device_id_type=pl.DeviceIdType.MESH)
        pl.semaphore_wait(barrier, npeer)
        own = [pltpu.make_async_copy(srcs[a].at[me] if scatter[a] else srcs[a], outs[a].at[me], own_sems.at[a])
               for a in range(na)]
        for cp in own:
            cp.start()
        remote = []
        for a in range(na):
            for k, (pos, peer) in enumerate(_peers(xi, yi, ci)):
                cp = pltpu.make_async_remote_copy(
                    src_ref=srcs[a].at[peer] if scatter[a] else srcs[a], dst_ref=outs[a].at[me],
                    send_sem=send_sems.at[a * npeer + k], recv_sem=recv_sems.at[a * npeer + k],
                    device_id=pos, device_id_type=pl.DeviceIdType.MESH)
                cp.start()
                remote.append(cp)
        for cp in own:
            cp.wait()
        for cp in remote:
            cp.wait()

    out_type = [_sds(a.shape if scatter[i] else (N_DEV,) + a.shape, a.dtype) for i, a in enumerate(arrays)]
    sent = sum(a.size * a.dtype.itemsize // (N_DEV if scatter[i] else 1) for i, a in enumerate(arrays))
    return pl.kernel(
        body, out_type=out_type, mesh=plsc.ScalarSubcoreMesh(axis_name="seq", num_cores=1), name=name,
        scratch_types=[pltpu.SemaphoreType.DMA((na * npeer,)), pltpu.SemaphoreType.DMA((na * npeer,)),
                       pltpu.SemaphoreType.DMA((na,))],
        compiler_params=pltpu.CompilerParams(collective_id=collective_id),
        cost_estimate=pl.CostEstimate(flops=0, transcendentals=0, bytes_accessed=2 * N_DEV * sent,
                                      remote_bytes_transferred=npeer * sent),
    )(*arrays, *([] if after is None else [after]))


def _sc_gather(arrays, name, collective_id, after=None):
    na = len(arrays)
    ndep = 0 if after is None else 1
    npair = N_DEV - 1

    def body(*refs):
        srcs = refs[:na]
        outs = refs[na + ndep:2 * na + ndep]
        send_sems, recv_sems, own_sems = refs[2 * na + ndep:]
        xi, yi, ci = (lax.axis_index(a) for a in MESH_AXES)
        me = 4 * xi + 2 * yi + ci
        sibling = (xi, yi, 1 - ci)
        chips = [(1 - xi, yi), (xi, 1 - yi), (1 - xi, 1 - yi)]
        barrier = pltpu.get_barrier_semaphore()
        for pos, _ in _peers(xi, yi, ci):
            pl.semaphore_signal(barrier, inc=1, device_id=pos, device_id_type=pl.DeviceIdType.MESH)
        pl.semaphore_wait(barrier, npair)

        def copy(a, k, src, slot, to):
            return pltpu.make_async_remote_copy(
                src_ref=src, dst_ref=outs[a].at[slot], send_sem=send_sems.at[a * npair + k],
                recv_sem=recv_sems.at[a * npair + k], device_id=to, device_id_type=pl.DeviceIdType.MESH)

        own = [pltpu.make_async_copy(srcs[a], outs[a].at[me], own_sems.at[a]) for a in range(na)]
        for cp in own:
            cp.start()
        sent = []
        for a in range(na):
            sent.append(copy(a, 0, srcs[a], me, sibling))
            sent += [copy(a, 1 + j, srcs[a], me, (*chip, ci)) for j, chip in enumerate(chips)]
        for cp in sent:
            cp.start()
        for j, (cx, cy) in enumerate(chips):
            slot = 4 * cx + 2 * cy + ci
            for a in range(na):
                copy(a, 1 + j, srcs[a], slot, sibling).wait_recv()
                cp = copy(a, 4 + j, outs[a].at[slot], slot, sibling)
                cp.start()
                sent.append(cp)
        for a in range(na):
            copy(a, 0, srcs[a], me, sibling).wait_recv()
            for j in range(len(chips)):
                copy(a, 4 + j, srcs[a], me, sibling).wait_recv()
        for cp in sent:
            cp.wait_send()
        for cp in own:
            cp.wait()

    out_type = [_sds((N_DEV,) + a.shape, a.dtype) for a in arrays]
    sent_bytes = sum(a.size * a.dtype.itemsize for a in arrays)
    return pl.kernel(
        body, out_type=out_type, mesh=plsc.ScalarSubcoreMesh(axis_name="seq", num_cores=1), name=name,
        scratch_types=[pltpu.SemaphoreType.DMA((na * npair,)), pltpu.SemaphoreType.DMA((na * npair,)),
                       pltpu.SemaphoreType.DMA((na,))],
        compiler_params=pltpu.CompilerParams(collective_id=collective_id),
        cost_estimate=pl.CostEstimate(flops=0, transcendentals=0, bytes_accessed=2 * N_DEV * sent_bytes,
                                      remote_bytes_transferred=npair * sent_bytes),
    )(*arrays, *([] if after is None else [after]))


def _adamw(w, g, m, v):
    m = ADAM_B1 * m + (1.0 - ADAM_B1) * g
    v = ADAM_B2 * v + (1.0 - ADAM_B2) * jnp.square(g)
    m_hat = m / (1.0 - ADAM_B1 ** ADAM_STEP)
    v_hat = v / (1.0 - ADAM_B2 ** ADAM_STEP)
    delta = -ADAM_LR * (m_hat / (jnp.sqrt(v_hat) + ADAM_EPS) + ADAM_WD * w)
    return delta, m, v


def _sum_devices(ref):
    g = ref[0].astype(F32)
    for k in range(1, N_DEV):
        g = g + ref[k].astype(F32)
    return g


def _update_big(parts, w, m, v, name):
    rows, cols = w.shape

    def body(p_ref, w_ref, m_ref, v_ref, g_ref, d_ref, nm_ref, nv_ref):
        g = _sum_devices(p_ref)
        g_ref[...] = g
        d_ref[...], nm_ref[...], nv_ref[...] = _adamw(w_ref[...], g, m_ref[...], v_ref[...])

    if rows % 16 == 0:
        tr = _pick_tile(rows, (128, 64, 16))
        steps, blk = rows // tr, pl.BlockSpec((tr, cols), lambda i: (i, 0))
        pblk = pl.BlockSpec((N_DEV, tr, cols), lambda i: (0, i, 0))
    else:
        tcol = 2 * LANES
        steps, blk = cols // tcol, pl.BlockSpec((rows, tcol), lambda i: (0, i))
        pblk = pl.BlockSpec((N_DEV, rows, tcol), lambda i: (0, 0, i))
    return pl.pallas_call(
        body, name=name, grid=(steps,),
        in_specs=[pblk, blk, blk, blk],
        out_specs=[blk] * 4, out_shape=[_sds((rows, cols), F32)] * 4,
        compiler_params=_params(("parallel",)),
    )(parts, w, m, v)


_VEC_ORDER = ("ln_in_g", "ln_in_b", "conv_b", "conv_ln_g", "conv_ln_b", "gate_bias", "gla_norm_g",
              "ln1_g", "ln1_b", "ln2_g", "ln2_b")
_SHARDED_SMALL = (("meta_tokens", 0, N_META, LANES), ("conv_w", N_META, CONV_WIDTH, None), ("gate_up", N_META + 32, GLA_RANK, None))


def _update_small(parts_sh, parts_vec, wmv):
    names = [s[0] for s in _SHARDED_SMALL] + list(_VEC_ORDER)
    flat = [a for nme in names for a in wmv[nme]]
    nv = len(_VEC_ORDER)

    def body(*refs):
        sh_ref, vec_ref = refs[0], refs[1]
        ins = refs[2:2 + len(flat)]
        outs = refs[2 + len(flat):2 + len(flat) + 4 * len(names)]
        loss_ref = refs[2 + len(flat) + 4 * len(names)]
        gsh_ref, gvec_ref = refs[-2:]
        gsh_ref[...] = _sum_devices(sh_ref)
        gvec_ref[...] = _sum_devices(vec_ref)
        loss_ref[...] = gvec_ref[nv:nv + 1, :]
        for idx, nme in enumerate(names):
            w_ref, m_ref, v_ref = ins[3 * idx:3 * idx + 3]
            rows, cols = w_ref.shape
            if idx < len(_SHARDED_SMALL):
                r0 = _SHARDED_SMALL[idx][1]
                g = gsh_ref[r0:r0 + rows, 0:cols]
            else:
                j = idx - len(_SHARDED_SMALL)
                g = gvec_ref[j:j + 1, 0:cols]
            o = outs[4 * idx:4 * idx + 4]
            o[0][...] = g
            o[1][...], o[2][...], o[3][...] = _adamw(w_ref[...], g, m_ref[...], v_ref[...])

    out_shape = [_sds(wmv[nme][0].shape, F32) for nme in names for _ in range(4)] + [_sds((1, parts_vec.shape[2]), F32)]
    vmem = pl.BlockSpec(memory_space=pltpu.VMEM)
    res = pl.pallas_call(
        body, name="update_small", out_shape=out_shape,
        in_specs=[vmem] * (2 + len(flat)), out_specs=[vmem] * len(out_shape),
        scratch_shapes=[pltpu.VMEM(parts_sh.shape[1:], F32), pltpu.VMEM(parts_vec.shape[1:], F32)],
    )(parts_sh, parts_vec, *flat)
    return {nme: res[4 * i:4 * i + 4] for i, nme in enumerate(names)}, res[-1][0, 0]


_WEIGHTS = ("meta_tokens", "ln_in_g", "ln_in_b", "w_in", "conv_w", "conv_b", "conv_ln_g", "conv_ln_b", "gate_up",
            "gate_bias", "gla_norm_g", "w_out", "ln1_g", "ln1_b", "w_ff1", "w_ff2", "ln2_g", "ln2_b")


def kernel(x, meta_tokens, ln_in_g, ln_in_b, w_in, conv_w, conv_b, conv_ln_g, conv_ln_b, gate_up, gate_bias, gla_norm_g, w_out, ln1_g, ln1_b, w_ff1, w_ff2, ln2_g, ln2_b, loss_target, m_meta_tokens, m_ln_in_g, m_ln_in_b, m_w_in, m_conv_w, m_conv_b, m_conv_ln_g, m_conv_ln_b, m_gate_up, m_gate_bias, m_gla_norm_g, m_w_out, m_ln1_g, m_ln1_b, m_w_ff1, m_w_ff2, m_ln2_g, m_ln2_b, v_meta_tokens, v_ln_in_g, v_ln_in_b, v_w_in, v_conv_w, v_conv_b, v_conv_ln_g, v_conv_ln_b, v_gate_up, v_gate_bias, v_gla_norm_g, v_w_out, v_ln1_g, v_ln1_b, v_w_ff1, v_w_ff2, v_ln2_g, v_ln2_b):
    w = dict(meta_tokens=meta_tokens, ln_in_g=ln_in_g, ln_in_b=ln_in_b, w_in=w_in, conv_w=conv_w, conv_b=conv_b,
             conv_ln_g=conv_ln_g, conv_ln_b=conv_ln_b, gate_up=gate_up, gate_bias=gate_bias, gla_norm_g=gla_norm_g,
             w_out=w_out, ln1_g=ln1_g, ln1_b=ln1_b, w_ff1=w_ff1, w_ff2=w_ff2, ln2_g=ln2_g, ln2_b=ln2_b)
    mom = dict(meta_tokens=m_meta_tokens, ln_in_g=m_ln_in_g, ln_in_b=m_ln_in_b, w_in=m_w_in, conv_w=m_conv_w,
               conv_b=m_conv_b, conv_ln_g=m_conv_ln_g, conv_ln_b=m_conv_ln_b, gate_up=m_gate_up, gate_bias=m_gate_bias,
               gla_norm_g=m_gla_norm_g, w_out=m_w_out, ln1_g=m_ln1_g, ln1_b=m_ln1_b, w_ff1=m_w_ff1, w_ff2=m_w_ff2,
               ln2_g=m_ln2_g, ln2_b=m_ln2_b)
    var = dict(meta_tokens=v_meta_tokens, ln_in_g=v_ln_in_g, ln_in_b=v_ln_in_b, w_in=v_w_in, conv_w=v_conv_w,
               conv_b=v_conv_b, conv_ln_g=v_conv_ln_g, conv_ln_b=v_conv_ln_b, gate_up=v_gate_up, gate_bias=v_gate_bias,
               gla_norm_g=v_gla_norm_g, w_out=v_w_out, ln1_g=v_ln1_g, ln1_b=v_ln1_b, w_ff1=v_w_ff1, w_ff2=v_w_ff2,
               ln2_g=v_ln2_g, ln2_b=v_ln2_b)
    shapes = {k: a.shape for k, a in w.items()}

    def two_d(a):
        return a.reshape(1, -1) if a.ndim == 1 else a.reshape(a.shape[-2:])

    w2d = {k: two_d(a) for k, a in w.items()}
    m2d = {k: two_d(a) for k, a in mom.items()}
    v2d = {k: two_d(a) for k, a in var.items()}
    d = x.shape[-1]
    d_in = w2d["w_in"].shape[1] * N_DEV
    d_in_p = -(-d_in // LANES) * LANES

    for dct in (w2d, m2d, v2d):
        dct["w_in"] = dct["w_in"].T
    g_in, g_meta, g_conv, g_gup = _sc_gather(
        [w2d["w_in"].astype(BF16), w2d["meta_tokens"], w2d["conv_w"], w2d["gate_up"]], "gather_first", 0)
    g_out, g_ff1, g_ff2 = _sc_gather(
        [w2d["w_out"].astype(BF16), w2d["w_ff1"].astype(BF16), w2d["w_ff2"].astype(BF16)], "gather_late", 1)
    w_in_full = jnp.pad(g_in.reshape(d_in, d), ((0, d_in_p - d_in), (0, 0)))
    meta_full = g_meta.transpose(1, 0, 2).reshape(N_META, d)
    conv_w_full = g_conv.transpose(1, 0, 2).reshape(CONV_WIDTH, -1)
    gate_up_full = g_gup.transpose(1, 0, 2).reshape(GLA_RANK, -1)

    late_weights = (g_out.reshape(-1, d), g_ff1, g_ff2.reshape(-1, d))
    pushed = {}

    def push(tag, grads):
        if tag == "ff":
            pushed["ff1"], pushed["ff2"] = _sc_exchange(list(grads), [True, True], "scatter_ff", 2)
        elif tag == "out":
            pushed["p_out"] = grads[0].reshape(N_DEV, -1, d)
        else:
            p_in = grads[0][:d_in].reshape(N_DEV, d_in // N_DEV, d)
            pushed["in"], pushed["out"] = _sc_exchange([p_in, pushed["p_out"]], [True, True], "scatter_rest", 3,
                                                       after=pushed["ff1"])

    res = _local_step(x, loss_target, meta_full, w2d["ln_in_g"], w2d["ln_in_b"], w_in_full, conv_w_full, w2d["conv_b"],
                      w2d["conv_ln_g"], w2d["conv_ln_b"], gate_up_full, w2d["gate_bias"], w2d["gla_norm_g"], late_weights,
                      w2d["ln1_g"], w2d["ln1_b"], w2d["ln2_g"], w2d["ln2_b"], push)

    dc = res["conv_w"].shape[1]
    hk = res["gate_up"].shape[1]
    sh_meta = res["meta_tokens"].reshape(N_META, N_DEV, LANES).transpose(1, 0, 2)
    sh_conv = jnp.pad(res["conv_w"].reshape(CONV_WIDTH, N_DEV, dc // N_DEV).transpose(1, 0, 2),
                      ((0, 0), (0, 32 - CONV_WIDTH), (0, LANES - dc // N_DEV)))
    sh_gup = jnp.pad(res["gate_up"].reshape(GLA_RANK, N_DEV, hk // N_DEV).transpose(1, 0, 2),
                     ((0, 0), (0, 0), (0, LANES - hk // N_DEV)))
    p_sh = jnp.concatenate([sh_meta, sh_conv, sh_gup], axis=1)
    p_vec = jnp.concatenate([jnp.pad(res[k], ((0, 0), (0, d - res[k].shape[1]))) for k in _VEC_ORDER]
                            + [jnp.full((1, d), res["loss"], F32), jnp.zeros((15 - len(_VEC_ORDER), d), F32)], axis=0)

    r_sh, r_vec = _exchange([p_sh, p_vec], [True, False], "scatter_small")
    r_ff1, r_ff2, r_out, r_in = pushed["ff1"], pushed["ff2"], pushed["out"], pushed["in"]

    upd = {}
    upd["w_in"] = [a.T for a in _update_big(r_in, w2d["w_in"], m2d["w_in"], v2d["w_in"], "update_w_in")]
    upd["w_out"] = _update_big(r_out, w2d["w_out"], m2d["w_out"], v2d["w_out"], "update_w_out")
    upd["w_ff1"] = _update_big(r_ff1, w2d["w_ff1"], m2d["w_ff1"], v2d["w_ff1"], "update_w_ff1")
    upd["w_ff2"] = _update_big(r_ff2, w2d["w_ff2"], m2d["w_ff2"], v2d["w_ff2"], "update_w_ff2")
    small = [s[0] for s in _SHARDED_SMALL] + list(_VEC_ORDER)
    upd_small, loss = _update_small(r_sh, r_vec, {k: (w2d[k], m2d[k], v2d[k]) for k in small})
    upd.update(upd_small)

    outs = [loss, res["grad_x"]]
    for j in range(4):
        outs += [upd[k][j].reshape(shapes[k]) for k in _WEIGHTS]
    return tuple(outs)
```

```python
import jax
import jax.numpy as jnp
from jax import lax
from jax.experimental import pallas as pl
from jax.experimental.pallas import tpu as pltpu
from jax.experimental.pallas import tpu_sc as plsc

F32 = jnp.float32
BF16 = jnp.bfloat16

N_META = 16
CHUNK = 64
PAD_FRONT = (-N_META) % CHUNK
X_OFF = PAD_FRONT + N_META
CONV_WIDTH = 31
CONV_HALO = 32
CONV_SUB = 64
CONV_WIN = CONV_SUB + CONV_HALO
GLA_HEADS = 4
GLA_DK = 64
GLA_DV = 128
GLA_RANK = 16
GLA_TAU = 16.0
QK_SCALE = GLA_DK ** -0.5
LN_EPS = 1e-5
ALPHA = 2.0 ** 0.25
LANES = 128
N_DEV = 8
ADAM_LR = 0.001
ADAM_B1 = 0.9
ADAM_B2 = 0.999
ADAM_EPS = 1e-08
ADAM_WD = 0.01
ADAM_STEP = 10
VMEM_LIMIT = 56 * 1024 * 1024
MESH_AXES = ("x", "y", "c")
U_QK, U_V, U_R, U_GD = 2, 3, 4, 20
QK_WIDTH = 2 * GLA_HEADS * GLA_DK
GUP_WIDTH = GLA_HEADS * GLA_DK
DMI_GLA = 1


def _sds(shape, dtype):
    return jax.ShapeDtypeStruct(shape, dtype)


def _mm(a, b):
    return jnp.dot(a, b, preferred_element_type=F32)


def _mm_nt(a, b):
    return lax.dot_general(a, b, (((1,), (1,)), ((), ())), preferred_element_type=F32)


def _mm_tn(a, b):
    return lax.dot_general(a, b, (((0,), (0,)), ((), ())), preferred_element_type=F32)


def _sigmoid(x):
    return 1.0 / (1.0 + jnp.exp(-x))


def _log_sigmoid(z):
    return jnp.minimum(z, 0.0) - jnp.log(1.0 + jnp.exp(-jnp.abs(z)))


def _ln(x):
    mu = jnp.mean(x, axis=-1, keepdims=True)
    xc = x - mu
    var = jnp.mean(xc * xc, axis=-1, keepdims=True)
    rstd = lax.rsqrt(var + LN_EPS)
    return xc * rstd, rstd


def _ln_bwd(dyg, xhat, rstd):
    m1 = jnp.mean(dyg, axis=-1, keepdims=True)
    m2 = jnp.mean(dyg * xhat, axis=-1, keepdims=True)
    return rstd * (dyg - m1 - xhat * m2)


def _rowsum(x):
    return jnp.sum(x, axis=0, keepdims=True)


def _row_in_seq(i, tm, tp):
    base = lax.rem(i * tm, tp)
    return base + lax.broadcasted_iota(jnp.int32, (tm, 1), 0)


def _split3(x):
    hi = x.astype(BF16)
    r1 = x - hi.astype(F32)
    mid = r1.astype(BF16)
    lo = (r1 - mid.astype(F32)).astype(BF16)
    return hi, mid, lo


def _params(sem):
    return pltpu.CompilerParams(dimension_semantics=sem, vmem_limit_bytes=VMEM_LIMIT)


def _pick_tile(n, prefs):
    for t in prefs:
        if n % t == 0:
            return t
    raise ValueError(f"no tile for {n}")


ROW_BLOCKS = 2


def _row_blocks(tm, matmuls, finish):
    blocks = [slice(k * tm // ROW_BLOCKS, (k + 1) * tm // ROW_BLOCKS) for k in range(ROW_BLOCKS)]
    acc = matmuls(blocks[0])
    for prev, rows in zip(blocks, blocks[1:]):
        nxt = matmuls(rows)
        finish(prev, acc)
        acc = nxt
    finish(blocks[-1], acc)


def _x_tile_row(tp, seq, tx):
    tps = seq // tx
    return lambda i: pl.multiple_of((i // tps) * tp + X_OFF + (i % tps) * tx, CHUNK)


def _ln_in_x(x2, g, b, tp, seq, tx):
    rx, d = x2.shape
    r = rx // seq * tp
    row = _x_tile_row(tp, seq, tx)

    def body(x_ref, g_ref, b_ref, s0_ref, sb_ref):
        xhat, _ = _ln(x_ref[...])
        s = xhat * g_ref[...] + b_ref[...]
        s0_ref[...] = s
        sb_ref[...] = s.astype(BF16)

    out = pl.BlockSpec((pl.Element(tx), pl.Element(d)), lambda i: (row(i), 0))
    return pl.pallas_call(
        body, name="ln_in_x", grid=(rx // tx,),
        in_specs=[pl.BlockSpec((tx, d), lambda i: (i, 0)), pl.BlockSpec((1, d), lambda i: (0, 0)),
                  pl.BlockSpec((1, d), lambda i: (0, 0))],
        out_specs=[out, out],
        out_shape=[_sds((r, d), F32), _sds((r, d), BF16)],
        compiler_params=_params(("parallel",)),
    )(x2, g, b)


def _ln_in_head(head, g, b, s0, s0b, tp):
    r, d = s0.shape
    nb = tp // X_OFF

    def body(h_ref, g_ref, b_ref, s0_in, sb_in, s0_ref, sb_ref):
        xhat, _ = _ln(h_ref[...])
        real = lax.broadcasted_iota(jnp.int32, (X_OFF, 1), 0) >= PAD_FRONT
        s = jnp.where(real, xhat * g_ref[...] + b_ref[...], 0.0)
        s0_ref[...] = s
        sb_ref[...] = s.astype(BF16)

    anyspec = pl.BlockSpec(memory_space=pl.ANY)
    out = pl.BlockSpec((X_OFF, d), lambda i: (i * nb, 0))
    return pl.pallas_call(
        body, name="ln_in_head", grid=(r // tp,),
        in_specs=[pl.BlockSpec((X_OFF, d), lambda i: (0, 0)), pl.BlockSpec((1, d), lambda i: (0, 0)),
                  pl.BlockSpec((1, d), lambda i: (0, 0)), anyspec, anyspec],
        out_specs=[out, out],
        out_shape=[_sds((r, d), F32), _sds((r, d), BF16)],
        input_output_aliases={3: 0, 4: 1},
        compiler_params=_params(("parallel",)),
    )(head, g, b, s0, s0b)


def _inproj_fwd(s0b, w_in, tm):
    r, d = s0b.shape
    n = w_in.shape[0]

    def body(s_ref, w_ref, u_ref):
        u_ref[...] = _mm_nt(s_ref[...], w_ref[...])

    return pl.pallas_call(
        body, name="inproj_fwd", grid=(r // tm,),
        in_specs=[pl.BlockSpec((tm, d), lambda i: (i, 0)), pl.BlockSpec((n, d), lambda i: (0, 0))],
        out_specs=pl.BlockSpec((tm, n), lambda i: (i, 0)),
        out_shape=_sds((r, n), F32),
        compiler_params=_params(("parallel",)),
    )(s0b, w_in)


def _conv_taps(win, coef, lo):
    acc = None
    for rho in range(8):
        offs = [o for o in range(lo, lo + CONV_WIDTH) if o % 8 == rho]
        if not offs:
            continue
        rolled = win if rho == 0 else pltpu.roll(win, CONV_WIN - rho, 0)
        for o in offs:
            m8 = o - rho
            term = rolled[m8:m8 + CONV_SUB, :] * coef(o)
            acc = term if acc is None else acc + term
    return acc


def _conv_fwd(u, w32, cb, cg, cbe, tp, tc, dc):
    r = u.shape[0]
    hb = tc // CONV_HALO

    def body(a_ref, g_ref, ah_ref, gh_ref, w_ref, cb_ref, cg_ref, cbe_ref, c_ref, co_ref, hs_ref):
        t = pl.program_id(0)
        first = lax.rem(t * tc, tp) == 0
        hh = ah_ref[...] * _sigmoid(gh_ref[...])
        hs_ref[0:CONV_HALO, :] = jnp.where(first, 0.0, hh)
        hs_ref[CONV_HALO:CONV_HALO + tc, :] = a_ref[...] * _sigmoid(g_ref[...])

        def sub(k, carry):
            r0 = pl.multiple_of(k * CONV_SUB, CONV_SUB)
            win = hs_ref[pl.ds(r0, CONV_WIN), :]
            c = _conv_taps(win, lambda o: w_ref[o - 2:o - 1, :], 2) + cb_ref[...]
            c_ref[pl.ds(r0, CONV_SUB), :] = c
            xhat, _ = _ln(c)
            cn = xhat * cg_ref[...] + cbe_ref[...]
            co_ref[pl.ds(r0, CONV_SUB), :] = (cn * _sigmoid(cn)).astype(BF16)
            return carry

        lax.fori_loop(0, tc // CONV_SUB, sub, 0)

    vec = pl.BlockSpec((1, dc), lambda t: (0, 0))
    return pl.pallas_call(
        body, name="conv_fwd", grid=(r // tc,),
        in_specs=[pl.BlockSpec((tc, dc), lambda t: (t, 0)), pl.BlockSpec((tc, dc), lambda t: (t, 1)),
                  pl.BlockSpec((CONV_HALO, dc), lambda t: (jnp.maximum(t * hb - 1, 0), 0)),
                  pl.BlockSpec((CONV_HALO, dc), lambda t: (jnp.maximum(t * hb - 1, 0), 1)),
                  pl.BlockSpec((32, dc), lambda t: (0, 0)), vec, vec, vec],
        out_specs=[pl.BlockSpec((tc, dc), lambda t: (t, 0)), pl.BlockSpec((tc, dc), lambda t: (t, 0))],
        out_shape=[_sds((r, dc), F32), _sds((r, dc), BF16)],
        scratch_shapes=[pltpu.VMEM((CONV_HALO + tc, dc), F32)],
        compiler_params=_params(("parallel",)),
    )(u, u, u, u, w32, cb, cg, cbe)


def _tri_mm_all(tri, xs):
    parts = [_split3(x) for x in xs]
    acc = [None] * len(xs)
    for t in range(3):
        for j in range(len(xs)):
            term = _mm(tri, parts[j][t])
            acc[j] = term if t == 0 else acc[j] + term
    return acc


def _gla_prep(qk_ref, gd_ref, gup, gb, n0, kc):
    rows = [slice(j * CHUNK, (j + 1) * CHUNK) for j in range(kc)]
    ri = lax.broadcasted_iota(jnp.int32, (CHUNK, CHUNK), 0)
    ci = lax.broadcasted_iota(jnp.int32, (CHUNK, CHUNK), 1)
    low = (ri >= ci).astype(BF16)
    hk = GLA_HEADS * GLA_DK
    gds = [gd_ref[rw, :] for rw in rows]
    zs = [_mm(g.astype(BF16), gup) + gb for g in gds]
    reals = [(n0 + j) * CHUNK + lax.broadcasted_iota(jnp.int32, (CHUNK, 1), 0) >= PAD_FRONT for j in range(kc)]
    lgs = [jnp.where(reals[j], _log_sigmoid(zs[j]) * (1.0 / GLA_TAU), 0.0) for j in range(kc)]
    bs = _tri_mm_all(low, lgs)
    out = []
    for j in range(kc):
        b, bl = bs[j], _rowsum(lgs[j])
        q = qk_ref[rows[j], :hk] * QK_SCALE
        k = qk_ref[rows[j], hk:]
        eb, enb, ebl = jnp.exp(b), jnp.exp(-b), jnp.exp(bl - b)
        out.append(dict(rows=rows[j], gd=gds[j], z=zs[j], real=reals[j], eb=eb, enb=enb, ebl=ebl, gam=jnp.exp(bl),
                        qe=q * eb, ke=k * enb, kd=k * ebl))
    return out, ri, ci


def _gla_heads(p, v_ref):
    ops = []
    for h in range(GLA_HEADS):
        hp, h2 = divmod(h, 2)
        ls = slice(hp * LANES, (hp + 1) * LANES)
        m = _head_mask(h2)
        ops.append(dict(ls=ls, m=m, vs=slice(h * GLA_DV, (h + 1) * GLA_DV),
                        qe=jnp.where(m, p["qe"][:, ls], 0.0).astype(BF16),
                        kd=jnp.where(m, p["kd"][:, ls], 0.0).astype(BF16),
                        ke=p["ke"][:, ls].astype(BF16),
                        v=v_ref[p["rows"], h * GLA_DV:(h + 1) * GLA_DV].astype(BF16)))
    return ops


def _head_mask(h2):
    lane = lax.broadcasted_iota(jnp.int32, (1, LANES), 1)
    return (lane < GLA_DK) if h2 == 0 else (lane >= GLA_DK)


def _gla_fwd(u, gup, gb, gn, bsz, nc, kc):
    r = u.shape[0]
    hv = GLA_HEADS * GLA_DV
    ns = nc // kc

    def body(qk_ref, v_ref, r_ref, gd_ref, gup_ref, gb_ref, gn_ref, go_ref, sta_ref, st_ref):
        t = pl.program_id(1)

        @pl.when(t == 0)
        def _():
            st_ref[...] = jnp.zeros_like(st_ref)

        ps, ri, ci = _gla_prep(qk_ref, gd_ref, gup_ref[...], gb_ref[...], t * kc, kc)
        tril = ri >= ci
        items = [(j, h) for j in range(kc) for h in range(GLA_HEADS)]
        ops = [_gla_heads(p, v_ref) for p in ps]
        a = {jh: jnp.where(tril, _mm_nt(ops[jh[0]][jh[1]]["qe"], ops[jh[0]][jh[1]]["ke"]), 0.0).astype(BF16) for jh in items}
        oi = {jh: _mm(a[jh], ops[jh[0]][jh[1]]["v"]) for jh in items}
        inc = {jh: _mm_tn(ops[jh[0]][jh[1]]["v"], ops[jh[0]][jh[1]]["kd"]) for jh in items}
        sts = [st_ref[h] for h in range(GLA_HEADS)]
        for j, h in items:
            op, p = ops[j][h], ps[j]
            st = sts[h]
            sta_ref[j, h] = st
            o = oi[j, h] + _mm_nt(op["qe"], st.astype(BF16))
            sts[h] = st * p["gam"][:, op["ls"]] + inc[j, h]
            rs = lax.rsqrt(jnp.mean(o * o, axis=-1, keepdims=True) + LN_EPS)
            rr = r_ref[p["rows"], op["vs"]]
            go_ref[p["rows"], op["vs"]] = (o * rs * gn_ref[...] * (rr * _sigmoid(rr))).astype(BF16)
        for h in range(GLA_HEADS):
            st_ref[h] = sts[h]

    rowblk = lambda col: (lambda b, t: (b * ns + t, col))
    const = lambda b, t: (0, 0)
    return pl.pallas_call(
        body, name="gla_fwd", grid=(bsz, ns),
        in_specs=[pl.BlockSpec((kc * CHUNK, QK_WIDTH), rowblk(U_QK)), pl.BlockSpec((kc * CHUNK, hv), rowblk(U_V)),
                  pl.BlockSpec((kc * CHUNK, hv), rowblk(U_R)), pl.BlockSpec((kc * CHUNK, LANES), rowblk(U_GD)),
                  pl.BlockSpec((LANES, GUP_WIDTH), const), pl.BlockSpec((1, GUP_WIDTH), const), pl.BlockSpec((1, GLA_DV), const)],
        out_specs=[pl.BlockSpec((kc * CHUNK, hv), rowblk(0)),
                   pl.BlockSpec((kc, GLA_HEADS, LANES, LANES), lambda b, t: (b * ns + t, 0, 0, 0))],
        out_shape=[_sds((r, hv), BF16), _sds((bsz * nc, GLA_HEADS, LANES, LANES), F32)],
        scratch_shapes=[pltpu.VMEM((GLA_HEADS, LANES, LANES), F32)],
        compiler_params=_params(("parallel", "arbitrary")),
    )(u, u, u, u, gup, gb, gn)


def _outproj_fwd(s0, co, go, w_out, g1, b1, tm):
    r, d = s0.shape
    dc = co.shape[1]

    def body(s0_ref, co_ref, go_ref, w_ref, g_ref, b_ref, p1_ref, s1_ref, s1b_ref):
        nb = 4 if tm % 64 == 0 else 1
        blocks = [slice(k * (tm // nb), (k + 1) * (tm // nb)) for k in range(nb)]
        mixes = [_mm(co_ref[rows, :], w_ref[0:dc, :]) + _mm(go_ref[rows, :], w_ref[dc:2 * dc, :]) for rows in blocks]
        for rows, mix in zip(blocks, mixes):
            p1 = ALPHA * s0_ref[rows, :] + mix
            p1_ref[rows, :] = p1
            xhat, _ = _ln(p1)
            s1 = xhat * g_ref[...] + b_ref[...]
            s1_ref[rows, :] = s1
            s1b_ref[rows, :] = s1.astype(BF16)

    row = lambda w: pl.BlockSpec((tm, w), lambda i: (i, 0))
    vec = pl.BlockSpec((1, d), lambda i: (0, 0))
    return pl.pallas_call(
        body, name="outproj_fwd", grid=(r // tm,),
        in_specs=[row(d), row(dc), row(dc), pl.BlockSpec((2 * dc, d), lambda i: (0, 0)), vec, vec],
        out_specs=[row(d), row(d), row(d)],
        out_shape=[_sds((r, d), F32), _sds((r, d), F32), _sds((r, d), BF16)],
        compiler_params=_params(("parallel",)),
    )(s0, co, go, w_out, g1, b1)


def _mlp_fwd(s1, s1b, w1g, w2, g2, b2, tgt, tp, tm, ns):
    r, d = s1.shape
    nh, _, th = w1g.shape
    nj = nh // ns
    nt = r // tm
    tb = tm // nj
    assert tb % 16 == 0 and (tb > X_OFF or X_OFF % tb == 0)

    def body(s1_ref, sb_ref, w1_ref, w2_ref, g_ref, b_ref, t_ref, hm_ref, dp2_ref, dpb_ref, loss_ref, dg_ref, db_ref, acc_ref):
        i = pl.program_id(0)
        j = pl.program_id(1)

        @pl.when(jnp.logical_and(i == 0, j == 0))
        def _():
            loss_ref[...] = jnp.zeros_like(loss_ref)
            dg_ref[...] = jnp.zeros_like(dg_ref)
            db_ref[...] = jnp.zeros_like(db_ref)

        cur = lax.rem(i, 2)

        @pl.when(jnp.logical_and(i < nt, j == 0))
        def _():
            acc_ref[cur] = jnp.zeros((tm, d), F32)

        def matmuls():
            hs = [_mm(sb_ref[...], w1_ref[s]) for s in range(ns)]
            acc = acc_ref[cur]
            for s in range(ns):
                hm_ref[:, s * th:(s + 1) * th] = hs[s].astype(BF16)
                act = jnp.square(jnp.maximum(hs[s], 0.0))
                acc = acc + _mm(act.astype(BF16), w2_ref[s * th:(s + 1) * th, :])
            acc_ref[cur] = acc

        def finish():
            ip = i - 1
            r0 = pl.multiple_of(j * tb, 16)
            rows = pl.ds(r0, tb)
            isx = lax.rem(ip * tm, tp) + r0 + lax.broadcasted_iota(jnp.int32, (tb, 1), 0) >= X_OFF
            t0 = jnp.where(ip == 0, jnp.maximum(r0 - X_OFF, 0), r0)
            tg = t_ref[pl.ds(pl.multiple_of(t0, 8), tb), :]
            if X_OFF % tb:
                tg = jnp.where(jnp.logical_and(ip == 0, j == 0), pltpu.roll(tg, X_OFF % tb, 0), tg)
            p2 = ALPHA * s1_ref[rows, :] + acc_ref[1 - cur, rows, :]
            xhat, rstd = _ln(p2)
            s2 = xhat * g_ref[...] + b_ref[...]
            err = jnp.where(isx, s2 - tg, 0.0)
            loss_ref[...] += 0.5 * jnp.sum(jnp.mean(err * err, axis=-1, keepdims=True))
            dy = err * (1.0 / d)
            dg_ref[...] += _rowsum(dy * xhat)
            db_ref[...] += _rowsum(dy)
            dp2 = _ln_bwd(dy * g_ref[...], xhat, rstd)
            dp2_ref[rows, :] = dp2
            dpb_ref[rows, :] = dp2.astype(BF16)

        @pl.when(i == 0)
        def _():
            matmuls()

        @pl.when(jnp.logical_and(i > 0, i < nt))
        def _():
            finish()
            matmuls()

        @pl.when(i == nt)
        def _():
            finish()

    def this_tile(i):
        return jnp.minimum(i, nt - 1)

    def prev_tile(i):
        return jnp.maximum(i - 1, 0)

    def slab(i, j):
        return jnp.where(i < nt, j, nj - 1)

    def tgt_start(i):
        r0 = prev_tile(i) * tm
        return pl.multiple_of(jnp.maximum(r0 - X_OFF * (r0 // tp + 1), 0), CHUNK)

    done = pl.BlockSpec((tm, d), lambda i, j: (prev_tile(i), 0))
    vec = pl.BlockSpec((1, d), lambda i, j: (0, 0))
    return pl.pallas_call(
        body, name="mlp_fwd", grid=(nt + 1, nj),
        in_specs=[done, pl.BlockSpec((tm, d), lambda i, j: (this_tile(i), 0)),
                  pl.BlockSpec((ns, d, th), lambda i, j: (slab(i, j), 0, 0)),
                  pl.BlockSpec((ns * th, d), lambda i, j: (slab(i, j), 0)),
                  vec, vec, pl.BlockSpec((pl.Element(tm), pl.Element(d)), lambda i, j: (tgt_start(i), 0))],
        out_specs=[pl.BlockSpec((tm, ns * th), lambda i, j: (this_tile(i), slab(i, j))), done, done,
                   pl.BlockSpec((8, LANES), lambda i, j: (0, 0)), vec, vec],
        out_shape=[_sds((r, nh * th), BF16), _sds((r, d), F32), _sds((r, d), BF16), _sds((8, LANES), F32),
                   _sds((1, d), F32), _sds((1, d), F32)],
        scratch_shapes=[pltpu.VMEM((2, tm, d), F32)],
        compiler_params=_params(("arbitrary", "arbitrary")),
    )(s1, s1b, w1g, w2, g2, b2, tgt)


def _mlp_bwd_act(dp2, dpb, hm, w1g, w2, p1, g1, tm, ns):
    r, d = dp2.shape
    nh, _, th = w1g.shape
    nj = nh // ns

    def body(dp2_ref, dpb_ref, hm_ref, w1_ref, w2_ref, p1_ref, g_ref, dh_ref, dp1_ref, dg_ref, db_ref, acc_ref):
        i = pl.program_id(0)
        j = pl.program_id(1)

        @pl.when(jnp.logical_and(i == 0, j == 0))
        def _():
            dg_ref[...] = jnp.zeros_like(dg_ref)
            db_ref[...] = jnp.zeros_like(db_ref)

        @pl.when(j == 0)
        def _():
            acc_ref[...] = jnp.zeros_like(acc_ref)

        def mlp_rows(rows):
            dacts = [_mm_nt(dpb_ref[rows, :], w2_ref[s * th:(s + 1) * th, :]) for s in range(ns)]
            acc = acc_ref[rows, :]
            for s in range(ns):
                cols = slice(s * th, (s + 1) * th)
                dh = (dacts[s] * (2.0 * jnp.maximum(hm_ref[rows, cols].astype(F32), 0.0))).astype(BF16)
                dh_ref[rows, cols] = dh
                acc = acc + _mm_nt(dh, w1_ref[s])
            return acc

        @pl.when(j < nj - 1)
        def _():
            acc_ref[...] = mlp_rows(slice(None))

        @pl.when(j == nj - 1)
        def _():
            def finish(rows, acc):
                ds1 = ALPHA * dp2_ref[rows, :] + acc
                xhat, rstd = _ln(p1_ref[rows, :])
                dg_ref[...] += _rowsum(ds1 * xhat)
                db_ref[...] += _rowsum(ds1)
                dp1_ref[rows, :] = _ln_bwd(ds1 * g_ref[...], xhat, rstd)

            _row_blocks(tm, mlp_rows, finish)

    row = pl.BlockSpec((tm, d), lambda i, j: (i, 0))
    vec = pl.BlockSpec((1, d), lambda i, j: (0, 0))
    blk = pl.BlockSpec((tm, ns * th), lambda i, j: (i, j))
    return pl.pallas_call(
        body, name="mlp_bwd_act", grid=(r // tm, nj),
        in_specs=[row, row, blk, pl.BlockSpec((ns, d, th), lambda i, j: (j, 0, 0)),
                  pl.BlockSpec((ns * th, d), lambda i, j: (j, 0)), row, vec],
        out_specs=[blk, row, vec, vec],
        out_shape=[_sds((r, nh * th), BF16), _sds((r, d), F32), _sds((1, d), F32), _sds((1, d), F32)],
        scratch_shapes=[pltpu.VMEM((tm, d), F32)],
        compiler_params=_params(("arbitrary", "arbitrary")),
    )(dp2, dpb, hm, w1g, w2, p1, g1)


def _mlp_bwd_w(s1b, hm, dh, dpb, nh, tm, ns):
    r, d = s1b.shape
    th = hm.shape[1] // nh

    def body(s1_ref, hm_ref, dh_ref, dp2_ref, dw1_ref, dw2_ref, a1_ref, a2_ref):
        i = pl.program_id(1)

        @pl.when(i == 0)
        def _():
            a1_ref[...] = jnp.zeros_like(a1_ref)
            a2_ref[...] = jnp.zeros_like(a2_ref)

        for s in range(ns):
            a1_ref[s] += _mm_tn(s1_ref[...], dh_ref[:, s * th:(s + 1) * th])
        for s in range(ns):
            act = jnp.square(jnp.maximum(hm_ref[:, s * th:(s + 1) * th].astype(F32), 0.0)).astype(BF16)
            a2_ref[s] += _mm_tn(act, dp2_ref[...])

        @pl.when(i == pl.num_programs(1) - 1)
        def _():
            dw1_ref[...] = a1_ref[...].astype(BF16)
            dw2_ref[...] = a2_ref[...].astype(BF16)

    row = pl.BlockSpec((tm, d), lambda j, i: (i, 0))
    blk = pl.BlockSpec((tm, ns * th), lambda j, i: (i, j))
    return pl.pallas_call(
        body, name="mlp_bwd_w", grid=(nh // ns, r // tm),
        in_specs=[row, blk, blk, row],
        out_specs=[pl.BlockSpec((ns, d, th), lambda j, i: (j, 0, 0)), pl.BlockSpec((ns, th, d), lambda j, i: (j, 0, 0))],
        out_shape=[_sds((nh, d, th), BF16), _sds((nh, th, d), BF16)],
        scratch_shapes=[pltpu.VMEM((ns, d, th), F32), pltpu.VMEM((ns, th, d), F32)],
        compiler_params=_params(("parallel", "arbitrary")),
    )(s1b, hm, dh, dpb)


def _outproj_bwd(dp1, co, go, w_out, tm):
    r, d = dp1.shape
    dc = co.shape[1]

    def body(dp_ref, co_ref, go_ref, w_ref, dmi_ref, dw_ref, acc_ref):
        i = pl.program_id(0)

        @pl.when(i == 0)
        def _():
            acc_ref[...] = jnp.zeros_like(acc_ref)

        dpb = dp_ref[...].astype(BF16)
        dmi_ref[...] = _mm_nt(dpb, w_ref[...])
        acc_ref[0:dc, :] += _mm_tn(co_ref[...], dpb)
        acc_ref[dc:2 * dc, :] += _mm_tn(go_ref[...], dpb)

        @pl.when(i == pl.num_programs(0) - 1)
        def _():
            dw_ref[...] = acc_ref[...].astype(BF16)

    row = lambda w: pl.BlockSpec((tm, w), lambda i: (i, 0))
    full = pl.BlockSpec((2 * dc, d), lambda i: (0, 0))
    return pl.pallas_call(
        body, name="outproj_bwd", grid=(r // tm,),
        in_specs=[row(d), row(dc), row(dc), full],
        out_specs=[row(2 * dc), full],
        out_shape=[_sds((r, 2 * dc), F32), _sds((2 * dc, d), BF16)],
        scratch_shapes=[pltpu.VMEM((2 * dc, d), F32)],
        compiler_params=_params(("arbitrary",)),
    )(dp1, co, go, w_out)


def _gla_bwd(u, dmi, sta, gup, gb, gn, bsz, nc, kc):
    r = u.shape[0]
    hv = GLA_HEADS * GLA_DV
    hk = GLA_HEADS * GLA_DK
    ns = nc // kc

    def body(qk_ref, v_ref, r_ref, gd_ref, dgo_ref, sta_ref, gup_ref, gb_ref, gn_ref,
             dqk_ref, dv_ref, dr_ref, dgd_ref, dgn_ref, dgb_ref, dgup_ref, dst_ref):
        bi = pl.program_id(0)
        t = pl.program_id(1)

        @pl.when(jnp.logical_and(bi == 0, t == 0))
        def _():
            dgn_ref[...] = jnp.zeros_like(dgn_ref)
            dgb_ref[...] = jnp.zeros_like(dgb_ref)
            dgup_ref[...] = jnp.zeros_like(dgup_ref)

        @pl.when(t == 0)
        def _():
            dst_ref[...] = jnp.zeros_like(dst_ref)

        ps, ri, ci = _gla_prep(qk_ref, gd_ref, gup_ref[...], gb_ref[...], (ns - 1 - t) * kc, kc)
        tril = ri >= ci
        items = [(j, h) for j in reversed(range(kc)) for h in range(GLA_HEADS)]
        ops = [_gla_heads(p, v_ref) for p in ps]
        op = lambda jh: ops[jh[0]][jh[1]]
        st = {jh: sta_ref[jh[0], jh[1]] for jh in items}
        stb = {jh: st[jh].astype(BF16) for jh in items}
        a = {jh: jnp.where(tril, _mm_nt(op(jh)["qe"], op(jh)["ke"]), 0.0).astype(BF16) for jh in items}
        o1 = {jh: _mm(a[jh], op(jh)["v"]) for jh in items}
        o2 = {jh: _mm_nt(op(jh)["qe"], stb[jh]) for jh in items}
        dob = {}
        dgn = jnp.zeros((1, GLA_DV), F32)
        for jh in items:
            rows, vs = ps[jh[0]]["rows"], op(jh)["vs"]
            o = o1[jh] + o2[jh]
            rr = r_ref[rows, vs]
            sr = _sigmoid(rr)
            rs = lax.rsqrt(jnp.mean(o * o, axis=-1, keepdims=True) + LN_EPS)
            y = o * rs
            dgo = dgo_ref[rows, vs]
            don = dgo * (rr * sr)
            dr_ref[rows, vs] = (dgo * (y * gn_ref[...]) * (sr * (1.0 + rr * (1.0 - sr)))).astype(BF16)
            dgn = dgn + _rowsum(don * y)
            dxn = don * gn_ref[...]
            dob[jh] = (rs * (dxn - y * jnp.mean(dxn * y, axis=-1, keepdims=True))).astype(BF16)
        da = {jh: jnp.where(tril, _mm_nt(dob[jh], op(jh)["v"]), 0.0).astype(BF16) for jh in items}
        dv1 = {jh: _mm_tn(a[jh], dob[jh]) for jh in items}
        dqe1 = {jh: _mm(da[jh], op(jh)["ke"]) for jh in items}
        dqe2 = {jh: _mm(dob[jh], stb[jh]) for jh in items}
        dke1 = {jh: _mm_tn(da[jh], op(jh)["qe"]) for jh in items}
        inc = {jh: _mm_tn(dob[jh], op(jh)["qe"]) for jh in items}
        dsts = [dst_ref[h] for h in range(GLA_HEADS)]
        dkd1, dgam1 = {}, {}
        for jh in items:
            j, h = jh
            dst = dsts[h]
            dstb = dst.astype(BF16)
            dv_ref[ps[j]["rows"], op(jh)["vs"]] = (dv1[jh] + _mm_nt(op(jh)["kd"], dstb)).astype(BF16)
            dkd1[jh] = _mm(op(jh)["v"], dstb)
            dgam1[jh] = _rowsum(dst * st[jh])
            dsts[h] = dst * ps[j]["gam"][:, op(jh)["ls"]] + inc[jh]
        for h in range(GLA_HEADS):
            dst_ref[h] = dsts[h]
        upper = (ri <= ci).astype(BF16)
        dbs, dbls = [], []
        for j in range(kc):
            p = ps[j]
            tiles = [[op((j, 2 * hp + h2)) for h2 in range(2)] for hp in range(GLA_HEADS // 2)]
            head = lambda d, hp, h2: d[j, 2 * hp + h2]
            lanes = lambda f: jnp.concatenate([f(hp) for hp in range(GLA_HEADS // 2)], axis=1)
            dqe = lanes(lambda hp: sum(jnp.where(tiles[hp][h2]["m"], head(dqe1, hp, h2) + head(dqe2, hp, h2), 0.0)
                                       for h2 in range(2)))
            dke = lanes(lambda hp: head(dke1, hp, 0) + head(dke1, hp, 1))
            dkd = lanes(lambda hp: sum(jnp.where(tiles[hp][h2]["m"], head(dkd1, hp, h2), 0.0) for h2 in range(2)))
            dgam = lanes(lambda hp: head(dgam1, hp, 0) + head(dgam1, hp, 1))
            dqk_ref[p["rows"], :hk] = (dqe * p["eb"] * QK_SCALE).astype(BF16)
            dqk_ref[p["rows"], hk:] = (dke * p["enb"] + dkd * p["ebl"]).astype(BF16)
            dkdkd = dkd * p["kd"]
            dbs.append(dqe * p["qe"] - dke * p["ke"] - dkdkd)
            dbls.append(_rowsum(dkdkd) + dgam * p["gam"])
        dlgs = _tri_mm_all(upper, dbs)
        dzb = []
        dgb = jnp.zeros((1, hk), F32)
        for j in range(kc):
            p = ps[j]
            dz = jnp.where(p["real"], (dlgs[j] + dbls[j]) * (1.0 / GLA_TAU) * _sigmoid(-p["z"]), 0.0)
            dgb = dgb + _rowsum(dz)
            dzb.append(dz.astype(BF16))
        dgup = sum(_mm_tn(ps[j]["gd"].astype(BF16), dzb[j]) for j in range(kc))
        for j in range(kc):
            dgd_ref[ps[j]["rows"], :] = _mm_nt(dzb[j], gup_ref[...]).astype(BF16)
        dgb_ref[...] += dgb
        dgup_ref[...] += dgup
        dgn_ref[...] += dgn

    rowblk = lambda col: (lambda b, t: (b * ns + ns - 1 - t, col))
    const = lambda b, t: (0, 0)
    return pl.pallas_call(
        body, name="gla_bwd", grid=(bsz, ns),
        in_specs=[pl.BlockSpec((kc * CHUNK, QK_WIDTH), rowblk(U_QK)), pl.BlockSpec((kc * CHUNK, hv), rowblk(U_V)),
                  pl.BlockSpec((kc * CHUNK, hv), rowblk(U_R)), pl.BlockSpec((kc * CHUNK, LANES), rowblk(U_GD)),
                  pl.BlockSpec((kc * CHUNK, hv), rowblk(DMI_GLA)),
                  pl.BlockSpec((kc, GLA_HEADS, LANES, LANES), lambda b, t: (b * ns + ns - 1 - t, 0, 0, 0)),
                  pl.BlockSpec((LANES, GUP_WIDTH), const), pl.BlockSpec((1, GUP_WIDTH), const), pl.BlockSpec((1, GLA_DV), const)],
        out_specs=[pl.BlockSpec((kc * CHUNK, 2 * hk), rowblk(0)), pl.BlockSpec((kc * CHUNK, hv), rowblk(0)),
                   pl.BlockSpec((kc * CHUNK, hv), rowblk(0)), pl.BlockSpec((kc * CHUNK, LANES), rowblk(0)),
                   pl.BlockSpec((1, GLA_DV), const), pl.BlockSpec((1, GUP_WIDTH), const),
                   pl.BlockSpec((LANES, GUP_WIDTH), const)],
        out_shape=[_sds((r, 2 * hk), BF16), _sds((r, hv), BF16), _sds((r, hv), BF16), _sds((r, LANES), BF16),
                   _sds((1, GLA_DV), F32), _sds((1, GUP_WIDTH), F32), _sds((LANES, GUP_WIDTH), F32)],
        scratch_shapes=[pltpu.VMEM((GLA_HEADS, LANES, LANES), F32)],
        compiler_params=_params(("arbitrary", "arbitrary")),
    )(u, u, u, u, dmi, sta, gup, gb, gn)


def _conv_bwd(u, c, dmi, w32, cg, cbe, tp, tc, dc):
    r = u.shape[0]
    hb = tc // CONV_HALO
    nhalo = r // CONV_HALO

    def dconv(cv, dco, cg_ref, cbe_ref):
        xhat, rstd = _ln(cv)
        cn = xhat * cg_ref[...] + cbe_ref[...]
        sg = _sigmoid(cn)
        dcn = dco * (sg * (1.0 + cn * (1.0 - sg)))
        return _ln_bwd(dcn * cg_ref[...], xhat, rstd), dcn, xhat

    def body(a_ref, g_ref, ah_ref, gh_ref, c_ref, dco_ref, ch_ref, dcoh_ref, w_ref, cg_ref, cbe_ref,
             du_ref, dw_ref, dcb_ref, dcg_ref, dcbe_ref, hs_ref, dcs_ref, dw8_ref):
        t = pl.program_id(0)

        @pl.when(t == 0)
        def _():
            dw8_ref[...] = jnp.zeros_like(dw8_ref)
            dcb_ref[...] = jnp.zeros_like(dcb_ref)
            dcg_ref[...] = jnp.zeros_like(dcg_ref)
            dcbe_ref[...] = jnp.zeros_like(dcbe_ref)

        first = lax.rem(t * tc, tp) == 0
        last = lax.rem((t + 1) * tc, tp) == 0
        hh = ah_ref[...] * _sigmoid(gh_ref[...])
        hs_ref[0:CONV_HALO, :] = jnp.where(first, 0.0, hh)
        hs_ref[CONV_HALO:CONV_HALO + tc, :] = a_ref[...] * _sigmoid(g_ref[...])
        dch, _, _ = dconv(ch_ref[...], dcoh_ref[...], cg_ref, cbe_ref)
        dcs_ref[tc:tc + CONV_HALO, :] = jnp.where(last, 0.0, dch)

        lrows = tc // 4

        def sub1(k, carry):
            r0 = pl.multiple_of(k * lrows, 8)
            dcv, dcn, xhat = dconv(c_ref[pl.ds(r0, lrows), :], dco_ref[pl.ds(r0, lrows), :], cg_ref, cbe_ref)
            dcs_ref[pl.ds(r0, lrows), :] = dcv
            dcb_ref[...] += _rowsum(dcv)
            dcg_ref[...] += _rowsum(dcn * xhat)
            dcbe_ref[...] += _rowsum(dcn)
            return carry

        lax.fori_loop(0, 4, sub1, 0)

        def sub2(k, carry):
            r0 = pl.multiple_of(k * CONV_SUB, CONV_SUB)
            dwin = dcs_ref[pl.ds(r0, CONV_WIN), :]
            dh = _conv_taps(dwin, lambda o: w_ref[CONV_WIDTH - 1 - o:CONV_WIDTH - o, :], 0)
            av = a_ref[pl.ds(r0, CONV_SUB), :]
            sg = _sigmoid(g_ref[pl.ds(r0, CONV_SUB), :])
            du_ref[pl.ds(r0, CONV_SUB), 0:dc] = (dh * sg).astype(BF16)
            du_ref[pl.ds(r0, CONV_SUB), dc:2 * dc] = (dh * av * sg * (1.0 - sg)).astype(BF16)
            hwin = hs_ref[pl.ds(r0, CONV_WIN), :]
            dcv = dwin[0:CONV_SUB, :]
            for rho in range(8):
                offs = [o for o in range(2, 2 + CONV_WIDTH) if o % 8 == rho]
                rolled = hwin if rho == 0 else pltpu.roll(hwin, CONV_WIN - rho, 0)
                for o in offs:
                    m8 = o - rho
                    prod = dcv * rolled[m8:m8 + CONV_SUB, :]
                    dw8_ref[8 * (o - 2):8 * (o - 1), :] += jnp.sum(prod.reshape(CONV_SUB // 8, 8, dc), axis=0)
            return carry

        lax.fori_loop(0, tc // CONV_SUB, sub2, 0)

        @pl.when(t == pl.num_programs(0) - 1)
        def _():
            dw_ref[...] = jnp.zeros_like(dw_ref)
            for j in range(CONV_WIDTH):
                dw_ref[j:j + 1, :] = _rowsum(dw8_ref[8 * j:8 * (j + 1), :])

    vec = pl.BlockSpec((1, dc), lambda t: (0, 0))
    prev = lambda col: (lambda t: (jnp.maximum(t * hb - 1, 0), col))
    nxt = lambda col: (lambda t: (jnp.minimum((t + 1) * hb, nhalo - 1), col))
    return pl.pallas_call(
        body, name="conv_bwd", grid=(r // tc,),
        in_specs=[pl.BlockSpec((tc, dc), lambda t: (t, 0)), pl.BlockSpec((tc, dc), lambda t: (t, 1)),
                  pl.BlockSpec((CONV_HALO, dc), prev(0)), pl.BlockSpec((CONV_HALO, dc), prev(1)),
                  pl.BlockSpec((tc, dc), lambda t: (t, 0)), pl.BlockSpec((tc, dc), lambda t: (t, 0)),
                  pl.BlockSpec((CONV_HALO, dc), nxt(0)), pl.BlockSpec((CONV_HALO, dc), nxt(0)),
                  pl.BlockSpec((32, dc), lambda t: (0, 0)), vec, vec],
        out_specs=[pl.BlockSpec((tc, 2 * dc), lambda t: (t, 0)), pl.BlockSpec((32, dc), lambda t: (0, 0)), vec, vec, vec],
        out_shape=[_sds((r, 2 * dc), BF16), _sds((32, dc), F32), _sds((1, dc), F32), _sds((1, dc), F32), _sds((1, dc), F32)],
        scratch_shapes=[pltpu.VMEM((CONV_HALO + tc, dc), F32), pltpu.VMEM((tc + CONV_HALO, dc), F32),
                        pltpu.VMEM((8 * 32, dc), F32)],
        compiler_params=_params(("arbitrary",)),
    )(u, u, u, u, c, dmi, c, dmi, w32, cg, cbe)


def _inproj_bwd(dp1, dus, xsrc, g_in, w_in, tp, seq, tx):
    r, d = dp1.shape
    widths = [x.shape[1] for x in dus]
    offs = [sum(widths[:k]) for k in range(len(widths))]
    n = w_in.shape[0]
    nd = len(dus)
    head = tx == 0
    rows = X_OFF if head else tx

    def body(*refs):
        dp_ref = refs[0]
        du_refs = refs[1:1 + nd]
        x_ref, g_ref, w_ref, out_ref, dg_ref, db_ref = refs[1 + nd:]
        i = pl.program_id(0)

        @pl.when(i == 0)
        def _():
            dg_ref[...] = jnp.zeros_like(dg_ref)
            db_ref[...] = jnp.zeros_like(db_ref)
            if head:
                out_ref[...] = jnp.zeros_like(out_ref)

        ds0 = ALPHA * dp_ref[...]
        for k in range(nd):
            ds0 = ds0 + _mm(du_refs[k][...], w_ref[offs[k]:offs[k] + widths[k], :])
        if head:
            ds0 = jnp.where(lax.broadcasted_iota(jnp.int32, (X_OFF, 1), 0) >= PAD_FRONT, ds0, 0.0)
        xhat, rstd = _ln(x_ref[...])
        dg_ref[...] += _rowsum(ds0 * xhat)
        db_ref[...] += _rowsum(ds0)
        dx = _ln_bwd(ds0 * g_ref[...], xhat, rstd)
        if head:
            out_ref[...] += dx[PAD_FRONT:X_OFF, :]
        else:
            out_ref[...] = dx

    if head:
        nb = tp // X_OFF
        row = lambda w: pl.BlockSpec((X_OFF, w), lambda i: (i * nb, 0))
        xspec = pl.BlockSpec((X_OFF, d), lambda i: (0, 0))
        ospec, oshape, steps = pl.BlockSpec((N_META, d), lambda i: (0, 0)), _sds((N_META, d), F32), r // tp
    else:
        start = _x_tile_row(tp, seq, tx)
        row = lambda w: pl.BlockSpec((pl.Element(tx), pl.Element(w)), lambda i: (start(i), 0))
        xspec = pl.BlockSpec((tx, d), lambda i: (i, 0))
        ospec, oshape, steps = xspec, _sds(xsrc.shape, F32), xsrc.shape[0] // tx
    vec = pl.BlockSpec((1, d), lambda i: (0, 0))
    return pl.pallas_call(
        body, name="inproj_bwd_head" if head else "inproj_bwd_x", grid=(steps,),
        in_specs=[row(d)] + [row(w) for w in widths] + [xspec, vec, pl.BlockSpec((n, d), lambda i: (0, 0))],
        out_specs=[ospec, vec, vec],
        out_shape=[oshape, _sds((1, d), F32), _sds((1, d), F32)],
        compiler_params=_params(("arbitrary",)),
    )(dp1, *dus, xsrc, g_in, w_in)


def _inproj_bwd_w(s0, dus, tm):
    r, d = s0.shape
    widths = [x.shape[1] for x in dus]
    offs = [sum(widths[:k]) for k in range(len(widths))]
    nd = len(dus)

    def body(*refs):
        s_ref = refs[0]
        du_refs = refs[1:1 + nd]
        dw_ref, acc_ref = refs[1 + nd:]
        i = pl.program_id(0)

        @pl.when(i == 0)
        def _():
            acc_ref[...] = jnp.zeros_like(acc_ref)

        for k in range(nd):
            acc_ref[offs[k]:offs[k] + widths[k], :] += _mm_tn(du_refs[k][...], s_ref[...])

        @pl.when(i == pl.num_programs(0) - 1)
        def _():
            dw_ref[...] = acc_ref[...].astype(BF16)

    row = lambda w: pl.BlockSpec((tm, w), lambda i: (i, 0))
    return pl.pallas_call(
        body, name="inproj_bwd_w", grid=(r // tm,),
        in_specs=[row(d)] + [row(w) for w in widths],
        out_specs=pl.BlockSpec((sum(widths), d), lambda i: (0, 0)),
        out_shape=_sds((sum(widths), d), BF16),
        scratch_shapes=[pltpu.VMEM((sum(widths), d), F32)],
        compiler_params=_params(("arbitrary",)),
    )(s0, *dus)


def _local_step(x, tgt, meta, ln_in_g, ln_in_b, w_in, conv_w, conv_b, conv_ln_g, conv_ln_b, gate_up, gate_bias,
                gla_norm_g, late_weights, ln1_g, ln1_b, ln2_g, ln2_b, push):
    bsz, seq, d = x.shape
    tp = X_OFF + seq
    assert tp % CHUNK == 0
    nc = tp // CHUNK
    dc = conv_b.shape[1]
    tmm = tc = _pick_tile(tp, (704, 128, 64))
    tw = _pick_tile(bsz * tp, (1408, 128, 64))
    tx = _pick_tile(seq, (512, 64))
    kc = _pick_tile(nc, (11, 3, 2, 1))
    ns = 2

    x2 = x.reshape(bsz * seq, d)
    head = jnp.pad(meta, ((PAD_FRONT, 0), (0, 0)))
    tgt_p = tgt.reshape(bsz * seq, d)
    w32 = jnp.pad(conv_w, ((0, 32 - CONV_WIDTH), (0, 0)))
    gup = jnp.pad(gate_up, ((0, LANES - GLA_RANK), (0, 0))).astype(BF16)

    s0, s0b = _ln_in_x(x2, ln_in_g, ln_in_b, tp, seq, tx)
    s0, s0b = _ln_in_head(head, ln_in_g, ln_in_b, s0, s0b, tp)
    u = _inproj_fwd(s0b, w_in, tmm)
    c, co = _conv_fwd(u, w32, conv_b, conv_ln_g, conv_ln_b, tp, tc, dc)
    go, sta = _gla_fwd(u, gup, gate_bias, gla_norm_g, bsz, nc, kc)
    w_out, w1g, w2 = late_weights
    nh = w1g.shape[0]
    p1, s1, s1b = _outproj_fwd(s0, co, go, w_out, ln1_g, ln1_b, tmm)
    hm, dp2, dpb, loss, dg2, db2 = _mlp_fwd(s1, s1b, w1g, w2, ln2_g, ln2_b, tgt_p, tp, tmm, ns)

    dh, dp1, dg1, db1 = _mlp_bwd_act(dp2, dpb, hm, w1g, w2, p1, ln1_g, tmm, ns)
    dw1, dw2 = _mlp_bwd_w(s1b, hm, dh, dpb, nh, tw, ns)
    push("ff", (dw1, dw2))
    dmi, dwo = _outproj_bwd(dp1, co, go, w_out, tmm)
    push("out", (dwo,))
    dqk, dv, dr, dgd, dgn, dgb, dgup = _gla_bwd(u, dmi, sta, gup, gate_bias, gla_norm_g, bsz, nc, kc)
    dcv, dcw, dcb, dcg, dcbe = _conv_bwd(u, c, dmi, w32, conv_ln_g, conv_ln_b, tp, tc, dc)
    dus = [dcv, dqk, dv, dr, dgd]
    dwi = _inproj_bwd_w(s0b, dus, tmm)
    push("in", (dwi,))
    gx, dgx, dbx = _inproj_bwd(dp1, dus, x2, ln_in_g, w_in, tp, seq, tx)
    dmeta, dgh, dbh = _inproj_bwd(dp1, dus, head, ln_in_g, w_in, tp, seq, 0)

    return dict(loss=loss[0, 0], grad_x=gx.reshape(bsz, seq, d), meta_tokens=dmeta, ln_in_g=dgx + dgh, ln_in_b=dbx + dbh,
                conv_w=dcw[:CONV_WIDTH], conv_b=dcb, conv_ln_g=dcg, conv_ln_b=dcbe,
                gate_up=dgup[:GLA_RANK], gate_bias=dgb, gla_norm_g=dgn, ln1_g=dg1, ln1_b=db1, ln2_g=dg2, ln2_b=db2)


def _exchange(arrays, scatter, name):
    na = len(arrays)
    npeer = N_DEV - 1

    def body(*refs):
        srcs = refs[:na]
        outs = refs[na:2 * na]
        send_sems, recv_sems, local_sems = refs[2 * na:]
        xi, yi, ci = (lax.axis_index(a) for a in MESH_AXES)
        me = 4 * xi + 2 * yi + ci
        copies = []
        for a in range(na):
            own = srcs[a].at[me] if scatter[a] else srcs[a]
            cp = pltpu.make_async_copy(own, outs[a].at[me], local_sems.at[a])
            cp.start()
            copies.append(cp)
        remote = []
        for k in range(1, N_DEV):
            px, py, pc = xi ^ (k >> 2), yi ^ ((k >> 1) & 1), ci ^ (k & 1)
            peer = 4 * px + 2 * py + pc
            for a in range(na):
                src = srcs[a].at[peer] if scatter[a] else srcs[a]
                cp = pltpu.make_async_remote_copy(
                    src_ref=src, dst_ref=outs[a].at[me],
                    send_sem=send_sems.at[a * npeer + k - 1], recv_sem=recv_sems.at[a * npeer + k - 1],
                    device_id=(px, py, pc), device_id_type=pl.DeviceIdType.MESH)
                cp.start()
                remote.append(cp)
        for cp in remote:
            cp.wait()
        for cp in copies:
            cp.wait()

    out_shape = [_sds(a.shape if scatter[i] else (N_DEV,) + a.shape, a.dtype) for i, a in enumerate(arrays)]
    anyspec = pl.BlockSpec(memory_space=pl.ANY)
    return pl.pallas_call(
        body, name=name,
        in_specs=[anyspec] * na, out_specs=[anyspec] * na, out_shape=out_shape,
        scratch_shapes=[pltpu.SemaphoreType.DMA((na * npeer,)), pltpu.SemaphoreType.DMA((na * npeer,)),
                        pltpu.SemaphoreType.DMA((na,))],
    )(*arrays)


def _peers(xi, yi, ci):
    for k in range(1, N_DEV):
        px, py, pc = xi ^ (k >> 2), yi ^ ((k >> 1) & 1), ci ^ (k & 1)
        yield (px, py, pc), 4 * px + 2 * py + pc


def _sc_exchange(arrays, scatter, name, collective_id, after=None):
    na = len(arrays)
    npeer = N_DEV - 1
    ndep = 0 if after is None else 1

    def body(*refs):
        srcs = refs[:na]
        outs = refs[na + ndep:2 * na + ndep]
        send_sems, recv_sems, own_sems = refs[2 * na + ndep:]
        xi, yi, ci = (lax.axis_index(a) for a in MESH_AXES)
        me = 4 * xi + 2 * yi + ci
        barrier = pltpu.get_barrier_semaphore()
        for pos, _ in _peers(xi, yi, ci):
            pl.semaphore_signal(barrier, inc=1, device_id=pos, device_id_type=pl.DeviceIdType.MESH)
        pl.semaphore_wait(barrier, npeer)
        own = [pltpu.make_async_copy(srcs[a].at[me] if scatter[a] else srcs[a], outs[a].at[me], own_sems.at[a])
               for a in range(na)]
        for cp in own:
            cp.start()
        remote = []
        for a in range(na):
            for k, (pos, peer) in enumerate(_peers(xi, yi, ci)):
                cp = pltpu.make_async_remote_copy(
                    src_ref=srcs[a].at[peer] if scatter[a] else srcs[a], dst_ref=outs[a].at[me],
                    send_sem=send_sems.at[a * npeer + k], recv_sem=recv_sems.at[a * npeer + k],
                    device_id=pos, device_id_type=pl.DeviceIdType.MESH)
                cp.start()
                remote.append(cp)
        for cp in own:
            cp.wait()
        for cp in remote:
            cp.wait()

    out_type = [_sds(a.shape if scatter[i] else (N_DEV,) + a.shape, a.dtype) for i, a in enumerate(arrays)]
    sent = sum(a.size * a.dtype.itemsize // (N_DEV if scatter[i] else 1) for i, a in enumerate(arrays))
    return pl.kernel(
        body, out_type=out_type, mesh=plsc.ScalarSubcoreMesh(axis_name="seq", num_cores=1), name=name,
        scratch_types=[pltpu.SemaphoreType.DMA((na * npeer,)), pltpu.SemaphoreType.DMA((na * npeer,)),
                       pltpu.SemaphoreType.DMA((na,))],
        compiler_params=pltpu.CompilerParams(collective_id=collective_id),
        cost_estimate=pl.CostEstimate(flops=0, transcendentals=0, bytes_accessed=2 * N_DEV * sent,
                                      remote_bytes_transferred=npeer * sent),
    )(*arrays, *([] if after is None else [after]))


def _sc_gather(arrays, name, collective_id, after=None):
    na = len(arrays)
    ndep = 0 if after is None else 1
    npair = N_DEV - 1

    def body(*refs):
        srcs = refs[:na]
        outs = refs[na + ndep:2 * na + ndep]
        send_sems, recv_sems, own_sems = refs[2 * na + ndep:]
        xi, yi, ci = (lax.axis_index(a) for a in MESH_AXES)
        me = 4 * xi + 2 * yi + ci
        sibling = (xi, yi, 1 - ci)
        chips = [(1 - xi, yi), (xi, 1 - yi), (1 - xi, 1 - yi)]
        barrier = pltpu.get_barrier_semaphore()
        for pos, _ in _peers(xi, yi, ci):
            pl.semaphore_signal(barrier, inc=1, device_id=pos, device_id_type=pl.DeviceIdType.MESH)
        pl.semaphore_wait(barrier, npair)

        def copy(a, k, src, slot, to):
            return pltpu.make_async_remote_copy(
                src_ref=src, dst_ref=outs[a].at[slot], send_sem=send_sems.at[a * npair + k],
                recv_sem=recv_sems.at[a * npair + k], device_id=to, device_id_type=pl.DeviceIdType.MESH)

        own = [pltpu.make_async_copy(srcs[a], outs[a].at[me], own_sems.at[a]) for a in range(na)]
        for cp in own:
            cp.start()
        sent = []
        for a in range(na):
            sent.append(copy(a, 0, srcs[a], me, sibling))
            sent += [copy(a, 1 + j, srcs[a], me, (*chip, ci)) for j, chip in enumerate(chips)]
        for cp in sent:
            cp.start()
        for j, (cx, cy) in enumerate(chips):
            slot = 4 * cx + 2 * cy + ci
            for a in range(na):
                copy(a, 1 + j, srcs[a], slot, sibling).wait_recv()
                cp = copy(a, 4 + j, outs[a].at[slot], slot, sibling)
                cp.start()
                sent.append(cp)
        for a in range(na):
            copy(a, 0, srcs[a], me, sibling).wait_recv()
            for j in range(len(chips)):
                copy(a, 4 + j, srcs[a], me, sibling).wait_recv()
        for cp in sent:
            cp.wait_send()
        for cp in own:
            cp.wait()

    out_type = [_sds((N_DEV,) + a.shape, a.dtype) for a in arrays]
    sent_bytes = sum(a.size * a.dtype.itemsize for a in arrays)
    return pl.kernel(
        body, out_type=out_type, mesh=plsc.ScalarSubcoreMesh(axis_name="seq", num_cores=1), name=name,
        scratch_types=[pltpu.SemaphoreType.DMA((na * npair,)), pltpu.SemaphoreType.DMA((na * npair,)),
                       pltpu.SemaphoreType.DMA((na,))],
        compiler_params=pltpu.CompilerParams(collective_id=collective_id),
        cost_estimate=pl.CostEstimate(flops=0, transcendentals=0, bytes_accessed=2 * N_DEV * sent_bytes,
                                      remote_bytes_transferred=npair * sent_bytes),
    )(*arrays, *([] if after is None else [after]))


def _adamw(w, g, m, v):
    m = ADAM_B1 * m + (1.0 - ADAM_B1) * g
    v = ADAM_B2 * v + (1.0 - ADAM_B2) * jnp.square(g)
    m_hat = m / (1.0 - ADAM_B1 ** ADAM_STEP)
    v_hat = v / (1.0 - ADAM_B2 ** ADAM_STEP)
    delta = -ADAM_LR * (m_hat / (jnp.sqrt(v_hat) + ADAM_EPS) + ADAM_WD * w)
    return delta, m, v


def _sum_devices(ref):
    g = ref[0].astype(F32)
    for k in range(1, N_DEV):
        g = g + ref[k].astype(F32)
    return g


def _update_big(parts, w, m, v, name):
    rows, cols = w.shape[0], w.shape[-1]
    mid = (None,) * (w.ndim - 2)
    mid0 = (0,) * (w.ndim - 2)

    def body(p_ref, w_ref, m_ref, v_ref, g_ref, d_ref, nm_ref, nv_ref):
        g = _sum_devices(p_ref)
        g_ref[...] = g
        d_ref[...], nm_ref[...], nv_ref[...] = _adamw(w_ref[...], g, m_ref[...], v_ref[...])

    if rows % 16 == 0:
        tr = _pick_tile(rows, (128, 64, 16))
        steps, blk = rows // tr, pl.BlockSpec((tr,) + mid + (cols,), lambda i: (i,) + mid0 + (0,))
        pblk = pl.BlockSpec((N_DEV, tr, cols), lambda i: (0, i, 0))
    else:
        tcol = 2 * LANES
        steps, blk = cols // tcol, pl.BlockSpec((rows,) + mid + (tcol,), lambda i: (0,) + mid0 + (i,))
        pblk = pl.BlockSpec((N_DEV, rows, tcol), lambda i: (0, 0, i))
    return pl.pallas_call(
        body, name=name, grid=(steps,),
        in_specs=[pblk, blk, blk, blk],
        out_specs=[blk] * 4, out_shape=[_sds(w.shape, F32)] * 4,
        compiler_params=_params(("parallel",)),
    )(parts, w, m, v)


_VEC_ORDER = ("ln_in_g", "ln_in_b", "conv_b", "conv_ln_g", "conv_ln_b", "gate_bias", "gla_norm_g",
              "ln1_g", "ln1_b", "ln2_g", "ln2_b")
_SHARDED_SMALL = (("meta_tokens", 0, N_META, LANES), ("conv_w", N_META, CONV_WIDTH, None), ("gate_up", N_META + 32, GLA_RANK, None))


def _update_small(parts_sh, parts_vec, wmv):
    names = [s[0] for s in _SHARDED_SMALL] + list(_VEC_ORDER)
    flat = [a for nme in names for a in wmv[nme]]
    nv = len(_VEC_ORDER)

    def body(*refs):
        sh_ref, vec_ref = refs[0], refs[1]
        ins = refs[2:2 + len(flat)]
        outs = refs[2 + len(flat):2 + len(flat) + 4 * len(names)]
        loss_ref = refs[2 + len(flat) + 4 * len(names)]
        gsh_ref, gvec_ref = refs[-2:]
        gsh_ref[...] = _sum_devices(sh_ref)
        gvec_ref[...] = _sum_devices(vec_ref)
        loss_ref[...] = gvec_ref[nv:nv + 1, :]
        for idx, nme in enumerate(names):
            w_ref, m_ref, v_ref = ins[3 * idx:3 * idx + 3]
            rows, cols = w_ref.shape
            if idx < len(_SHARDED_SMALL):
                r0 = _SHARDED_SMALL[idx][1]
                g = gsh_ref[r0:r0 + rows, 0:cols]
            else:
                j = idx - len(_SHARDED_SMALL)
                g = gvec_ref[j:j + 1, 0:cols]
            o = outs[4 * idx:4 * idx + 4]
            o[0][...] = g
            o[1][...], o[2][...], o[3][...] = _adamw(w_ref[...], g, m_ref[...], v_ref[...])

    out_shape = [_sds(wmv[nme][0].shape, F32) for nme in names for _ in range(4)] + [_sds((1, parts_vec.shape[2]), F32)]
    vmem = pl.BlockSpec(memory_space=pltpu.VMEM)
    res = pl.pallas_call(
        body, name="update_small", out_shape=out_shape,
        in_specs=[vmem] * (2 + len(flat)), out_specs=[vmem] * len(out_shape),
        scratch_shapes=[pltpu.VMEM(parts_sh.shape[1:], F32), pltpu.VMEM(parts_vec.shape[1:], F32)],
    )(parts_sh, parts_vec, *flat)
    return {nme: res[4 * i:4 * i + 4] for i, nme in enumerate(names)}, res[-1][0, 0]


_WEIGHTS = ("meta_tokens", "ln_in_g", "ln_in_b", "w_in", "conv_w", "conv_b", "conv_ln_g", "conv_ln_b", "gate_up",
            "gate_bias", "gla_norm_g", "w_out", "ln1_g", "ln1_b", "w_ff1", "w_ff2", "ln2_g", "ln2_b")


def kernel(x, meta_tokens, ln_in_g, ln_in_b, w_in, conv_w, conv_b, conv_ln_g, conv_ln_b, gate_up, gate_bias, gla_norm_g, w_out, ln1_g, ln1_b, w_ff1, w_ff2, ln2_g, ln2_b, loss_target, m_meta_tokens, m_ln_in_g, m_ln_in_b, m_w_in, m_conv_w, m_conv_b, m_conv_ln_g, m_conv_ln_b, m_gate_up, m_gate_bias, m_gla_norm_g, m_w_out, m_ln1_g, m_ln1_b, m_w_ff1, m_w_ff2, m_ln2_g, m_ln2_b, v_meta_tokens, v_ln_in_g, v_ln_in_b, v_w_in, v_conv_w, v_conv_b, v_conv_ln_g, v_conv_ln_b, v_gate_up, v_gate_bias, v_gla_norm_g, v_w_out, v_ln1_g, v_ln1_b, v_w_ff1, v_w_ff2, v_ln2_g, v_ln2_b):
    w = dict(meta_tokens=meta_tokens, ln_in_g=ln_in_g, ln_in_b=ln_in_b, w_in=w_in, conv_w=conv_w, conv_b=conv_b,
             conv_ln_g=conv_ln_g, conv_ln_b=conv_ln_b, gate_up=gate_up, gate_bias=gate_bias, gla_norm_g=gla_norm_g,
             w_out=w_out, ln1_g=ln1_g, ln1_b=ln1_b, w_ff1=w_ff1, w_ff2=w_ff2, ln2_g=ln2_g, ln2_b=ln2_b)
    mom = dict(meta_tokens=m_meta_tokens, ln_in_g=m_ln_in_g, ln_in_b=m_ln_in_b, w_in=m_w_in, conv_w=m_conv_w,
               conv_b=m_conv_b, conv_ln_g=m_conv_ln_g, conv_ln_b=m_conv_ln_b, gate_up=m_gate_up, gate_bias=m_gate_bias,
               gla_norm_g=m_gla_norm_g, w_out=m_w_out, ln1_g=m_ln1_g, ln1_b=m_ln1_b, w_ff1=m_w_ff1, w_ff2=m_w_ff2,
               ln2_g=m_ln2_g, ln2_b=m_ln2_b)
    var = dict(meta_tokens=v_meta_tokens, ln_in_g=v_ln_in_g, ln_in_b=v_ln_in_b, w_in=v_w_in, conv_w=v_conv_w,
               conv_b=v_conv_b, conv_ln_g=v_conv_ln_g, conv_ln_b=v_conv_ln_b, gate_up=v_gate_up, gate_bias=v_gate_bias,
               gla_norm_g=v_gla_norm_g, w_out=v_w_out, ln1_g=v_ln1_g, ln1_b=v_ln1_b, w_ff1=v_w_ff1, w_ff2=v_w_ff2,
               ln2_g=v_ln2_g, ln2_b=v_ln2_b)
    shapes = {k: a.shape for k, a in w.items()}

    def two_d(a):
        return a.reshape(1, -1) if a.ndim == 1 else a.reshape(a.shape[-2:])

    w2d = {k: two_d(a) for k, a in w.items()}
    m2d = {k: two_d(a) for k, a in mom.items()}
    v2d = {k: two_d(a) for k, a in var.items()}
    d = x.shape[-1]
    d_in = w2d["w_in"].shape[1] * N_DEV
    d_in_p = -(-d_in // LANES) * LANES

    in_wmv = [jnp.transpose(dct["w_in"], (2, 0, 1)) for dct in (w, mom, var)]
    g_in, g_meta, g_conv, g_gup = _sc_gather(
        [w2d["w_in"].T.astype(BF16), w2d["meta_tokens"], w2d["conv_w"], w2d["gate_up"]], "gather_first", 0)
    g_out, g_ff1, g_ff2 = _sc_gather(
        [w2d["w_out"].astype(BF16), w2d["w_ff1"].astype(BF16), w2d["w_ff2"].astype(BF16)], "gather_late", 1)
    w_in_full = jnp.pad(g_in.reshape(d_in, d), ((0, d_in_p - d_in), (0, 0)))
    meta_full = g_meta.transpose(1, 0, 2).reshape(N_META, d)
    conv_w_full = g_conv.transpose(1, 0, 2).reshape(CONV_WIDTH, -1)
    gate_up_full = g_gup.transpose(1, 0, 2).reshape(GLA_RANK, -1)

    late_weights = (g_out.reshape(-1, d), g_ff1, g_ff2.reshape(-1, d))
    pushed = {}

    def push(tag, grads):
        if tag == "ff":
            pushed["ff1"], pushed["ff2"] = _sc_exchange(list(grads), [True, True], "scatter_ff", 2)
        elif tag == "out":
            pushed["p_out"] = grads[0].reshape(N_DEV, -1, d)
        else:
            p_in = grads[0][:d_in].reshape(N_DEV, d_in // N_DEV, d)
            pushed["in"], pushed["out"] = _sc_exchange([p_in, pushed["p_out"]], [True, True], "scatter_rest", 3,
                                                       after=pushed["ff1"])

    res = _local_step(x, loss_target, meta_full, w2d["ln_in_g"], w2d["ln_in_b"], w_in_full, conv_w_full, w2d["conv_b"],
                      w2d["conv_ln_g"], w2d["conv_ln_b"], gate_up_full, w2d["gate_bias"], w2d["gla_norm_g"], late_weights,
                      w2d["ln1_g"], w2d["ln1_b"], w2d["ln2_g"], w2d["ln2_b"], push)

    dc = res["conv_w"].shape[1]
    hk = res["gate_up"].shape[1]
    sh_meta = res["meta_tokens"].reshape(N_META, N_DEV, LANES).transpose(1, 0, 2)
    sh_conv = jnp.pad(res["conv_w"].reshape(CONV_WIDTH, N_DEV, dc // N_DEV).transpose(1, 0, 2),
                      ((0, 0), (0, 32 - CONV_WIDTH), (0, LANES - dc // N_DEV)))
    sh_gup = jnp.pad(res["gate_up"].reshape(GLA_RANK, N_DEV, hk // N_DEV).transpose(1, 0, 2),
                     ((0, 0), (0, 0), (0, LANES - hk // N_DEV)))
    p_sh = jnp.concatenate([sh_meta, sh_conv, sh_gup], axis=1)
    p_vec = jnp.concatenate([jnp.pad(res[k], ((0, 0), (0, d - res[k].shape[1]))) for k in _VEC_ORDER]
                            + [jnp.full((1, d), res["loss"], F32), jnp.zeros((15 - len(_VEC_ORDER), d), F32)], axis=0)

    r_sh, r_vec = _exchange([p_sh, p_vec], [True, False], "scatter_small")
    r_ff1, r_ff2, r_out, r_in = pushed["ff1"], pushed["ff2"], pushed["out"], pushed["in"]

    upd = {}
    upd["w_in"] = [jnp.transpose(a, (1, 2, 0)) for a in _update_big(r_in, *in_wmv, "update_w_in")]
    upd["w_out"] = _update_big(r_out, w2d["w_out"], m2d["w_out"], v2d["w_out"], "update_w_out")
    upd["w_ff1"] = _update_big(r_ff1, w2d["w_ff1"], m2d["w_ff1"], v2d["w_ff1"], "update_w_ff1")
    upd["w_ff2"] = _update_big(r_ff2, w2d["w_ff2"], m2d["w_ff2"], v2d["w_ff2"], "update_w_ff2")
    small = [s[0] for s in _SHARDED_SMALL] + list(_VEC_ORDER)
    upd_small, loss = _update_small(r_sh, r_vec, {k: (w2d[k], m2d[k], v2d[k]) for k in small})
    upd.update(upd_small)

    outs = [loss, res["grad_x"]]
    for j in range(4):
        outs += [upd[k][j].reshape(shapes[k]) for k in _WEIGHTS]
    return tuple(outs)
```

```python
import jax
import jax.numpy as jnp
from jax import lax
from jax.experimental import pallas as pl
from jax.experimental.pallas import tpu as pltpu
from jax.experimental.pallas import tpu_sc as plsc

F32 = jnp.float32
BF16 = jnp.bfloat16

N_META = 16
CHUNK = 64
PAD_FRONT = (-N_META) % CHUNK
X_OFF = PAD_FRONT + N_META
CONV_WIDTH = 31
CONV_HALO = 32
CONV_SUB = 64
CONV_WIN = CONV_SUB + CONV_HALO
GLA_HEADS = 4
GLA_DK = 64
GLA_DV = 128
GLA_RANK = 16
GLA_TAU = 16.0
QK_SCALE = GLA_DK ** -0.5
LN_EPS = 1e-5
ALPHA = 2.0 ** 0.25
LANES = 128
N_DEV = 8
ADAM_LR = 0.001
ADAM_B1 = 0.9
ADAM_B2 = 0.999
ADAM_EPS = 1e-08
ADAM_WD = 0.01
ADAM_STEP = 10
VMEM_LIMIT = 56 * 1024 * 1024
MESH_AXES = ("x", "y", "c")
U_QK, U_V, U_R, U_GD = 2, 3, 4, 20
QK_WIDTH = 2 * GLA_HEADS * GLA_DK
GUP_WIDTH = GLA_HEADS * GLA_DK
DMI_GLA = 1


def _sds(shape, dtype):
    return jax.ShapeDtypeStruct(shape, dtype)


def _mm(a, b):
    return jnp.dot(a, b, preferred_element_type=F32)


def _mm_nt(a, b):
    return lax.dot_general(a, b, (((1,), (1,)), ((), ())), preferred_element_type=F32)


def _mm_tn(a, b):
    return lax.dot_general(a, b, (((0,), (0,)), ((), ())), preferred_element_type=F32)


def _sigmoid(x):
    return 1.0 / (1.0 + jnp.exp(-x))


def _log_sigmoid(z):
    return jnp.minimum(z, 0.0) - jnp.log(1.0 + jnp.exp(-jnp.abs(z)))


def _ln(x):
    mu = jnp.mean(x, axis=-1, keepdims=True)
    xc = x - mu
    var = jnp.mean(xc * xc, axis=-1, keepdims=True)
    rstd = lax.rsqrt(var + LN_EPS)
    return xc * rstd, rstd


def _ln_bwd(dyg, xhat, rstd):
    m1 = jnp.mean(dyg, axis=-1, keepdims=True)
    m2 = jnp.mean(dyg * xhat, axis=-1, keepdims=True)
    return rstd * (dyg - m1 - xhat * m2)


def _rowsum(x):
    return jnp.sum(x, axis=0, keepdims=True)


def _row_in_seq(i, tm, tp):
    base = lax.rem(i * tm, tp)
    return base + lax.broadcasted_iota(jnp.int32, (tm, 1), 0)


def _split3(x):
    hi = x.astype(BF16)
    r1 = x - hi.astype(F32)
    mid = r1.astype(BF16)
    lo = (r1 - mid.astype(F32)).astype(BF16)
    return hi, mid, lo


def _params(sem):
    return pltpu.CompilerParams(dimension_semantics=sem, vmem_limit_bytes=VMEM_LIMIT)


def _pick_tile(n, prefs):
    for t in prefs:
        if n % t == 0:
            return t
    raise ValueError(f"no tile for {n}")


ROW_BLOCKS = 2


def _row_blocks(tm, matmuls, finish):
    blocks = [slice(k * tm // ROW_BLOCKS, (k + 1) * tm // ROW_BLOCKS) for k in range(ROW_BLOCKS)]
    acc = matmuls(blocks[0])
    for prev, rows in zip(blocks, blocks[1:]):
        nxt = matmuls(rows)
        finish(prev, acc)
        acc = nxt
    finish(blocks[-1], acc)


def _x_tile_row(tp, seq, tx):
    tps = seq // tx
    return lambda i: pl.multiple_of((i // tps) * tp + X_OFF + (i % tps) * tx, CHUNK)


def _ln_in_x(x2, g, b, tp, seq, tx):
    rx, d = x2.shape
    r = rx // seq * tp
    row = _x_tile_row(tp, seq, tx)

    def body(x_ref, g_ref, b_ref, s0_ref, sb_ref):
        xhat, _ = _ln(x_ref[...])
        s = xhat * g_ref[...] + b_ref[...]
        s0_ref[...] = s
        sb_ref[...] = s.astype(BF16)

    out = pl.BlockSpec((pl.Element(tx), pl.Element(d)), lambda i: (row(i), 0))
    return pl.pallas_call(
        body, name="ln_in_x", grid=(rx // tx,),
        in_specs=[pl.BlockSpec((tx, d), lambda i: (i, 0)), pl.BlockSpec((1, d), lambda i: (0, 0)),
                  pl.BlockSpec((1, d), lambda i: (0, 0))],
        out_specs=[out, out],
        out_shape=[_sds((r, d), F32), _sds((r, d), BF16)],
        compiler_params=_params(("parallel",)),
    )(x2, g, b)


def _ln_in_head(head, g, b, s0, s0b, tp):
    r, d = s0.shape
    nb = tp // X_OFF

    def body(h_ref, g_ref, b_ref, s0_in, sb_in, s0_ref, sb_ref):
        xhat, _ = _ln(h_ref[...])
        real = lax.broadcasted_iota(jnp.int32, (X_OFF, 1), 0) >= PAD_FRONT
        s = jnp.where(real, xhat * g_ref[...] + b_ref[...], 0.0)
        s0_ref[...] = s
        sb_ref[...] = s.astype(BF16)

    anyspec = pl.BlockSpec(memory_space=pl.ANY)
    out = pl.BlockSpec((X_OFF, d), lambda i: (i * nb, 0))
    return pl.pallas_call(
        body, name="ln_in_head", grid=(r // tp,),
        in_specs=[pl.BlockSpec((X_OFF, d), lambda i: (0, 0)), pl.BlockSpec((1, d), lambda i: (0, 0)),
                  pl.BlockSpec((1, d), lambda i: (0, 0)), anyspec, anyspec],
        out_specs=[out, out],
        out_shape=[_sds((r, d), F32), _sds((r, d), BF16)],
        input_output_aliases={3: 0, 4: 1},
        compiler_params=_params(("parallel",)),
    )(head, g, b, s0, s0b)


def _inproj_fwd(s0b, w_in, tm):
    r, d = s0b.shape
    n = w_in.shape[0]

    def body(s_ref, w_ref, u_ref):
        u_ref[...] = _mm_nt(s_ref[...], w_ref[...])

    return pl.pallas_call(
        body, name="inproj_fwd", grid=(r // tm,),
        in_specs=[pl.BlockSpec((tm, d), lambda i: (i, 0)), pl.BlockSpec((n, d), lambda i: (0, 0))],
        out_specs=pl.BlockSpec((tm, n), lambda i: (i, 0)),
        out_shape=_sds((r, n), F32),
        compiler_params=_params(("parallel",)),
    )(s0b, w_in)


def _conv_taps(win, coef, lo):
    acc = None
    for rho in range(8):
        offs = [o for o in range(lo, lo + CONV_WIDTH) if o % 8 == rho]
        if not offs:
            continue
        rolled = win if rho == 0 else pltpu.roll(win, CONV_WIN - rho, 0)
        for o in offs:
            m8 = o - rho
            term = rolled[m8:m8 + CONV_SUB, :] * coef(o)
            acc = term if acc is None else acc + term
    return acc


def _conv_fwd(u, w32, cb, cg, cbe, tp, tc, dc):
    r = u.shape[0]
    hb = tc // CONV_HALO

    def body(a_ref, g_ref, ah_ref, gh_ref, w_ref, cb_ref, cg_ref, cbe_ref, c_ref, co_ref, hs_ref):
        t = pl.program_id(0)
        first = lax.rem(t * tc, tp) == 0
        hh = ah_ref[...] * _sigmoid(gh_ref[...])
        hs_ref[0:CONV_HALO, :] = jnp.where(first, 0.0, hh)
        hs_ref[CONV_HALO:CONV_HALO + tc, :] = a_ref[...] * _sigmoid(g_ref[...])

        def sub(k, carry):
            r0 = pl.multiple_of(k * CONV_SUB, CONV_SUB)
            win = hs_ref[pl.ds(r0, CONV_WIN), :]
            c = _conv_taps(win, lambda o: w_ref[o - 2:o - 1, :], 2) + cb_ref[...]
            c_ref[pl.ds(r0, CONV_SUB), :] = c
            xhat, _ = _ln(c)
            cn = xhat * cg_ref[...] + cbe_ref[...]
            co_ref[pl.ds(r0, CONV_SUB), :] = (cn * _sigmoid(cn)).astype(BF16)
            return carry

        lax.fori_loop(0, tc // CONV_SUB, sub, 0)

    vec = pl.BlockSpec((1, dc), lambda t: (0, 0))
    return pl.pallas_call(
        body, name="conv_fwd", grid=(r // tc,),
        in_specs=[pl.BlockSpec((tc, dc), lambda t: (t, 0)), pl.BlockSpec((tc, dc), lambda t: (t, 1)),
                  pl.BlockSpec((CONV_HALO, dc), lambda t: (jnp.maximum(t * hb - 1, 0), 0)),
                  pl.BlockSpec((CONV_HALO, dc), lambda t: (jnp.maximum(t * hb - 1, 0), 1)),
                  pl.BlockSpec((32, dc), lambda t: (0, 0)), vec, vec, vec],
        out_specs=[pl.BlockSpec((tc, dc), lambda t: (t, 0)), pl.BlockSpec((tc, dc), lambda t: (t, 0))],
        out_shape=[_sds((r, dc), F32), _sds((r, dc), BF16)],
        scratch_shapes=[pltpu.VMEM((CONV_HALO + tc, dc), F32)],
        compiler_params=_params(("parallel",)),
    )(u, u, u, u, w32, cb, cg, cbe)


def _tri_mm_all(tri, xs):
    parts = [_split3(x) for x in xs]
    acc = [None] * len(xs)
    for t in range(3):
        for j in range(len(xs)):
            term = _mm(tri, parts[j][t])
            acc[j] = term if t == 0 else acc[j] + term
    return acc


def _gla_prep(qk_ref, gd_ref, gup, gb, n0, kc):
    rows = [slice(j * CHUNK, (j + 1) * CHUNK) for j in range(kc)]
    ri = lax.broadcasted_iota(jnp.int32, (CHUNK, CHUNK), 0)
    ci = lax.broadcasted_iota(jnp.int32, (CHUNK, CHUNK), 1)
    low = (ri >= ci).astype(BF16)
    hk = GLA_HEADS * GLA_DK
    gds = [gd_ref[rw, :] for rw in rows]
    zs = [_mm(g.astype(BF16), gup) + gb for g in gds]
    reals = [(n0 + j) * CHUNK + lax.broadcasted_iota(jnp.int32, (CHUNK, 1), 0) >= PAD_FRONT for j in range(kc)]
    lgs = [jnp.where(reals[j], _log_sigmoid(zs[j]) * (1.0 / GLA_TAU), 0.0) for j in range(kc)]
    bs = _tri_mm_all(low, lgs)
    out = []
    for j in range(kc):
        b, bl = bs[j], _rowsum(lgs[j])
        q = qk_ref[rows[j], :hk] * QK_SCALE
        k = qk_ref[rows[j], hk:]
        eb, enb, ebl = jnp.exp(b), jnp.exp(-b), jnp.exp(bl - b)
        out.append(dict(rows=rows[j], gd=gds[j], z=zs[j], real=reals[j], eb=eb, enb=enb, ebl=ebl, gam=jnp.exp(bl),
                        qe=q * eb, ke=k * enb, kd=k * ebl))
    return out, ri, ci


def _gla_heads(p, v_ref):
    ops = []
    for h in range(GLA_HEADS):
        hp, h2 = divmod(h, 2)
        ls = slice(hp * LANES, (hp + 1) * LANES)
        m = _head_mask(h2)
        ops.append(dict(ls=ls, m=m, vs=slice(h * GLA_DV, (h + 1) * GLA_DV),
                        qe=jnp.where(m, p["qe"][:, ls], 0.0).astype(BF16),
                        kd=jnp.where(m, p["kd"][:, ls], 0.0).astype(BF16),
                        ke=p["ke"][:, ls].astype(BF16),
                        v=v_ref[p["rows"], h * GLA_DV:(h + 1) * GLA_DV].astype(BF16)))
    return ops


def _head_mask(h2):
    lane = lax.broadcasted_iota(jnp.int32, (1, LANES), 1)
    return (lane < GLA_DK) if h2 == 0 else (lane >= GLA_DK)


def _gla_fwd(u, gup, gb, gn, bsz, nc, kc):
    r = u.shape[0]
    hv = GLA_HEADS * GLA_DV
    ns = nc // kc

    def body(qk_ref, v_ref, r_ref, gd_ref, gup_ref, gb_ref, gn_ref, go_ref, sta_ref, st_ref):
        t = pl.program_id(1)

        @pl.when(t == 0)
        def _():
            st_ref[...] = jnp.zeros_like(st_ref)

        ps, ri, ci = _gla_prep(qk_ref, gd_ref, gup_ref[...], gb_ref[...], t * kc, kc)
        tril = ri >= ci
        items = [(j, h) for j in range(kc) for h in range(GLA_HEADS)]
        ops = [_gla_heads(p, v_ref) for p in ps]
        a = {jh: jnp.where(tril, _mm_nt(ops[jh[0]][jh[1]]["qe"], ops[jh[0]][jh[1]]["ke"]), 0.0).astype(BF16) for jh in items}
        oi = {jh: _mm(a[jh], ops[jh[0]][jh[1]]["v"]) for jh in items}
        inc = {jh: _mm_tn(ops[jh[0]][jh[1]]["v"], ops[jh[0]][jh[1]]["kd"]) for jh in items}
        sts = [st_ref[h] for h in range(GLA_HEADS)]
        for j, h in items:
            op, p = ops[j][h], ps[j]
            st = sts[h]
            sta_ref[j, h] = st
            o = oi[j, h] + _mm_nt(op["qe"], st.astype(BF16))
            sts[h] = st * p["gam"][:, op["ls"]] + inc[j, h]
            rs = lax.rsqrt(jnp.mean(o * o, axis=-1, keepdims=True) + LN_EPS)
            rr = r_ref[p["rows"], op["vs"]]
            go_ref[p["rows"], op["vs"]] = (o * rs * gn_ref[...] * (rr * _sigmoid(rr))).astype(BF16)
        for h in range(GLA_HEADS):
            st_ref[h] = sts[h]

    rowblk = lambda col: (lambda b, t: (b * ns + t, col))
    const = lambda b, t: (0, 0)
    return pl.pallas_call(
        body, name="gla_fwd", grid=(bsz, ns),
        in_specs=[pl.BlockSpec((kc * CHUNK, QK_WIDTH), rowblk(U_QK)), pl.BlockSpec((kc * CHUNK, hv), rowblk(U_V)),
                  pl.BlockSpec((kc * CHUNK, hv), rowblk(U_R)), pl.BlockSpec((kc * CHUNK, LANES), rowblk(U_GD)),
                  pl.BlockSpec((LANES, GUP_WIDTH), const), pl.BlockSpec((1, GUP_WIDTH), const), pl.BlockSpec((1, GLA_DV), const)],
        out_specs=[pl.BlockSpec((kc * CHUNK, hv), rowblk(0)),
                   pl.BlockSpec((kc, GLA_HEADS, LANES, LANES), lambda b, t: (b * ns + t, 0, 0, 0))],
        out_shape=[_sds((r, hv), BF16), _sds((bsz * nc, GLA_HEADS, LANES, LANES), F32)],
        scratch_shapes=[pltpu.VMEM((GLA_HEADS, LANES, LANES), F32)],
        compiler_params=_params(("parallel", "arbitrary")),
    )(u, u, u, u, gup, gb, gn)


def _outproj_fwd(s0, co, go, w_out, g1, b1, tm):
    r, d = s0.shape
    dc = co.shape[1]

    def body(s0_ref, co_ref, go_ref, w_ref, g_ref, b_ref, p1_ref, s1_ref, s1b_ref):
        nb = 4 if tm % 64 == 0 else 1
        blocks = [slice(k * (tm // nb), (k + 1) * (tm // nb)) for k in range(nb)]
        mixes = [_mm(co_ref[rows, :], w_ref[0:dc, :]) + _mm(go_ref[rows, :], w_ref[dc:2 * dc, :]) for rows in blocks]
        for rows, mix in zip(blocks, mixes):
            p1 = ALPHA * s0_ref[rows, :] + mix
            p1_ref[rows, :] = p1
            xhat, _ = _ln(p1)
            s1 = xhat * g_ref[...] + b_ref[...]
            s1_ref[rows, :] = s1
            s1b_ref[rows, :] = s1.astype(BF16)

    row = lambda w: pl.BlockSpec((tm, w), lambda i: (i, 0))
    vec = pl.BlockSpec((1, d), lambda i: (0, 0))
    return pl.pallas_call(
        body, name="outproj_fwd", grid=(r // tm,),
        in_specs=[row(d), row(dc), row(dc), pl.BlockSpec((2 * dc, d), lambda i: (0, 0)), vec, vec],
        out_specs=[row(d), row(d), row(d)],
        out_shape=[_sds((r, d), F32), _sds((r, d), F32), _sds((r, d), BF16)],
        compiler_params=_params(("parallel",)),
    )(s0, co, go, w_out, g1, b1)


def _mlp_fwd(s1, s1b, w1g, w2, g2, b2, tgt, tp, tm, ns):
    r, d = s1.shape
    nh, _, th = w1g.shape
    nj = nh // ns

    def body(s1_ref, sb_ref, w1_ref, w2_ref, g_ref, b_ref, t_ref, hm_ref, dp2_ref, dpb_ref, loss_ref, dg_ref, db_ref, acc_ref):
        i = pl.program_id(0)
        j = pl.program_id(1)

        @pl.when(jnp.logical_and(i == 0, j == 0))
        def _():
            loss_ref[...] = jnp.zeros_like(loss_ref)
            dg_ref[...] = jnp.zeros_like(dg_ref)
            db_ref[...] = jnp.zeros_like(db_ref)

        @pl.when(j == 0)
        def _():
            acc_ref[...] = jnp.zeros_like(acc_ref)

        def mlp_rows(rows):
            hs = [_mm(sb_ref[rows, :], w1_ref[s]) for s in range(ns)]
            acc = acc_ref[rows, :]
            for s in range(ns):
                hm_ref[rows, s * th:(s + 1) * th] = hs[s].astype(BF16)
                act = jnp.square(jnp.maximum(hs[s], 0.0))
                acc = acc + _mm(act.astype(BF16), w2_ref[s * th:(s + 1) * th, :])
            return acc

        @pl.when(j < nj - 1)
        def _():
            acc_ref[...] = mlp_rows(slice(None))

        @pl.when(j == nj - 1)
        def _():
            isx = _row_in_seq(i, tm, tp) >= X_OFF
            tg = t_ref[...]
            tg = jnp.where(i == 0, pltpu.roll(tg, X_OFF, 0), tg)

            def finish(rows, acc):
                p2 = ALPHA * s1_ref[rows, :] + acc
                xhat, rstd = _ln(p2)
                s2 = xhat * g_ref[...] + b_ref[...]
                err = jnp.where(isx[rows], s2 - tg[rows], 0.0)
                loss_ref[...] += 0.5 * jnp.sum(jnp.mean(err * err, axis=-1, keepdims=True))
                dy = err * (1.0 / d)
                dg_ref[...] += _rowsum(dy * xhat)
                db_ref[...] += _rowsum(dy)
                dp2 = _ln_bwd(dy * g_ref[...], xhat, rstd)
                dp2_ref[rows, :] = dp2
                dpb_ref[rows, :] = dp2.astype(BF16)

            _row_blocks(tm, mlp_rows, finish)

    row = pl.BlockSpec((tm, d), lambda i, j: (i, 0))
    vec = pl.BlockSpec((1, d), lambda i, j: (0, 0))
    tgt_row = pl.BlockSpec((pl.Element(tm), pl.Element(d)),
                           lambda i, j: (pl.multiple_of(jnp.maximum(i * tm - X_OFF * ((i * tm) // tp + 1), 0), CHUNK), 0))
    return pl.pallas_call(
        body, name="mlp_fwd", grid=(r // tm, nj),
        in_specs=[row, row, pl.BlockSpec((ns, d, th), lambda i, j: (j, 0, 0)), pl.BlockSpec((ns * th, d), lambda i, j: (j, 0)),
                  vec, vec, tgt_row],
        out_specs=[pl.BlockSpec((tm, ns * th), lambda i, j: (i, j)), row, row,
                   pl.BlockSpec((8, LANES), lambda i, j: (0, 0)), vec, vec],
        out_shape=[_sds((r, nh * th), BF16), _sds((r, d), F32), _sds((r, d), BF16), _sds((8, LANES), F32),
                   _sds((1, d), F32), _sds((1, d), F32)],
        scratch_shapes=[pltpu.VMEM((tm, d), F32)],
        compiler_params=_params(("arbitrary", "arbitrary")),
    )(s1, s1b, w1g, w2, g2, b2, tgt)


def _mlp_bwd_act(dp2, dpb, hm, w1g, w2, p1, g1, tm, ns):
    r, d = dp2.shape
    nh, _, th = w1g.shape
    nj = nh // ns

    def body(dp2_ref, dpb_ref, hm_ref, w1_ref, w2_ref, p1_ref, g_ref, dh_ref, dp1_ref, dg_ref, db_ref, acc_ref):
        i = pl.program_id(0)
        j = pl.program_id(1)

        @pl.when(jnp.logical_and(i == 0, j == 0))
        def _():
            dg_ref[...] = jnp.zeros_like(dg_ref)
            db_ref[...] = jnp.zeros_like(db_ref)

        @pl.when(j == 0)
        def _():
            acc_ref[...] = jnp.zeros_like(acc_ref)

        def mlp_rows(rows):
            dacts = [_mm_nt(dpb_ref[rows, :], w2_ref[s * th:(s + 1) * th, :]) for s in range(ns)]
            acc = acc_ref[rows, :]
            for s in range(ns):
                cols = slice(s * th, (s + 1) * th)
                dh = (dacts[s] * (2.0 * jnp.maximum(hm_ref[rows, cols].astype(F32), 0.0))).astype(BF16)
                dh_ref[rows, cols] = dh
                acc = acc + _mm_nt(dh, w1_ref[s])
            return acc

        @pl.when(j < nj - 1)
        def _():
            acc_ref[...] = mlp_rows(slice(None))

        @pl.when(j == nj - 1)
        def _():
            def finish(rows, acc):
                ds1 = ALPHA * dp2_ref[rows, :] + acc
                xhat, rstd = _ln(p1_ref[rows, :])
                dg_ref[...] += _rowsum(ds1 * xhat)
                db_ref[...] += _rowsum(ds1)
                dp1_ref[rows, :] = _ln_bwd(ds1 * g_ref[...], xhat, rstd)

            _row_blocks(tm, mlp_rows, finish)

    row = pl.BlockSpec((tm, d), lambda i, j: (i, 0))
    vec = pl.BlockSpec((1, d), lambda i, j: (0, 0))
    blk = pl.BlockSpec((tm, ns * th), lambda i, j: (i, j))
    return pl.pallas_call(
        body, name="mlp_bwd_act", grid=(r // tm, nj),
        in_specs=[row, row, blk, pl.BlockSpec((ns, d, th), lambda i, j: (j, 0, 0)),
                  pl.BlockSpec((ns * th, d), lambda i, j: (j, 0)), row, vec],
        out_specs=[blk, row, vec, vec],
        out_shape=[_sds((r, nh * th), BF16), _sds((r, d), F32), _sds((1, d), F32), _sds((1, d), F32)],
        scratch_shapes=[pltpu.VMEM((tm, d), F32)],
        compiler_params=_params(("arbitrary", "arbitrary")),
    )(dp2, dpb, hm, w1g, w2, p1, g1)


def _mlp_bwd_w(s1b, hm, dh, dpb, nh, tm, ns):
    r, d = s1b.shape
    th = hm.shape[1] // nh

    def body(s1_ref, hm_ref, dh_ref, dp2_ref, dw1_ref, dw2_ref, a1_ref, a2_ref):
        i = pl.program_id(1)

        @pl.when(i == 0)
        def _():
            a1_ref[...] = jnp.zeros_like(a1_ref)
            a2_ref[...] = jnp.zeros_like(a2_ref)

        for s in range(ns):
            a1_ref[s] += _mm_tn(s1_ref[...], dh_ref[:, s * th:(s + 1) * th])
        for s in range(ns):
            act = jnp.square(jnp.maximum(hm_ref[:, s * th:(s + 1) * th].astype(F32), 0.0)).astype(BF16)
            a2_ref[s] += _mm_tn(act, dp2_ref[...])

        @pl.when(i == pl.num_programs(1) - 1)
        def _():
            dw1_ref[...] = a1_ref[...].astype(BF16)
            dw2_ref[...] = a2_ref[...].astype(BF16)

    row = pl.BlockSpec((tm, d), lambda j, i: (i, 0))
    blk = pl.BlockSpec((tm, ns * th), lambda j, i: (i, j))
    return pl.pallas_call(
        body, name="mlp_bwd_w", grid=(nh // ns, r // tm),
        in_specs=[row, blk, blk, row],
        out_specs=[pl.BlockSpec((ns, d, th), lambda j, i: (j, 0, 0)), pl.BlockSpec((ns, th, d), lambda j, i: (j, 0, 0))],
        out_shape=[_sds((nh, d, th), BF16), _sds((nh, th, d), BF16)],
        scratch_shapes=[pltpu.VMEM((ns, d, th), F32), pltpu.VMEM((ns, th, d), F32)],
        compiler_params=_params(("parallel", "arbitrary")),
    )(s1b, hm, dh, dpb)


def _outproj_bwd(dp1, co, go, w_out, tm):
    r, d = dp1.shape
    dc = co.shape[1]

    def body(dp_ref, co_ref, go_ref, w_ref, dmi_ref, dw_ref, acc_ref):
        i = pl.program_id(0)

        @pl.when(i == 0)
        def _():
            acc_ref[...] = jnp.zeros_like(acc_ref)

        dpb = dp_ref[...].astype(BF16)
        dmi_ref[...] = _mm_nt(dpb, w_ref[...])
        acc_ref[0:dc, :] += _mm_tn(co_ref[...], dpb)
        acc_ref[dc:2 * dc, :] += _mm_tn(go_ref[...], dpb)

        @pl.when(i == pl.num_programs(0) - 1)
        def _():
            dw_ref[...] = acc_ref[...].astype(BF16)

    row = lambda w: pl.BlockSpec((tm, w), lambda i: (i, 0))
    full = pl.BlockSpec((2 * dc, d), lambda i: (0, 0))
    return pl.pallas_call(
        body, name="outproj_bwd", grid=(r // tm,),
        in_specs=[row(d), row(dc), row(dc), full],
        out_specs=[row(2 * dc), full],
        out_shape=[_sds((r, 2 * dc), F32), _sds((2 * dc, d), BF16)],
        scratch_shapes=[pltpu.VMEM((2 * dc, d), F32)],
        compiler_params=_params(("arbitrary",)),
    )(dp1, co, go, w_out)


def _gla_bwd(u, dmi, sta, gup, gb, gn, bsz, nc, kc):
    r = u.shape[0]
    hv = GLA_HEADS * GLA_DV
    hk = GLA_HEADS * GLA_DK
    ns = nc // kc

    def body(qk_ref, v_ref, r_ref, gd_ref, dgo_ref, sta_ref, gup_ref, gb_ref, gn_ref,
             dqk_ref, dv_ref, dr_ref, dgd_ref, dgn_ref, dgb_ref, dgup_ref, dst_ref):
        bi = pl.program_id(0)
        t = pl.program_id(1)

        @pl.when(jnp.logical_and(bi == 0, t == 0))
        def _():
            dgn_ref[...] = jnp.zeros_like(dgn_ref)
            dgb_ref[...] = jnp.zeros_like(dgb_ref)
            dgup_ref[...] = jnp.zeros_like(dgup_ref)

        @pl.when(t == 0)
        def _():
            dst_ref[...] = jnp.zeros_like(dst_ref)

        ps, ri, ci = _gla_prep(qk_ref, gd_ref, gup_ref[...], gb_ref[...], (ns - 1 - t) * kc, kc)
        tril = ri >= ci
        items = [(j, h) for j in reversed(range(kc)) for h in range(GLA_HEADS)]
        ops = [_gla_heads(p, v_ref) for p in ps]
        op = lambda jh: ops[jh[0]][jh[1]]
        st = {jh: sta_ref[jh[0], jh[1]] for jh in items}
        stb = {jh: st[jh].astype(BF16) for jh in items}
        a = {jh: jnp.where(tril, _mm_nt(op(jh)["qe"], op(jh)["ke"]), 0.0).astype(BF16) for jh in items}
        o1 = {jh: _mm(a[jh], op(jh)["v"]) for jh in items}
        o2 = {jh: _mm_nt(op(jh)["qe"], stb[jh]) for jh in items}
        dob = {}
        dgn = jnp.zeros((1, GLA_DV), F32)
        for jh in items:
            rows, vs = ps[jh[0]]["rows"], op(jh)["vs"]
            o = o1[jh] + o2[jh]
            rr = r_ref[rows, vs]
            sr = _sigmoid(rr)
            rs = lax.rsqrt(jnp.mean(o * o, axis=-1, keepdims=True) + LN_EPS)
            y = o * rs
            dgo = dgo_ref[rows, vs]
            don = dgo * (rr * sr)
            dr_ref[rows, vs] = (dgo * (y * gn_ref[...]) * (sr * (1.0 + rr * (1.0 - sr)))).astype(BF16)
            dgn = dgn + _rowsum(don * y)
            dxn = don * gn_ref[...]
            dob[jh] = (rs * (dxn - y * jnp.mean(dxn * y, axis=-1, keepdims=True))).astype(BF16)
        da = {jh: jnp.where(tril, _mm_nt(dob[jh], op(jh)["v"]), 0.0).astype(BF16) for jh in items}
        dv1 = {jh: _mm_tn(a[jh], dob[jh]) for jh in items}
        dqe1 = {jh: _mm(da[jh], op(jh)["ke"]) for jh in items}
        dqe2 = {jh: _mm(dob[jh], stb[jh]) for jh in items}
        dke1 = {jh: _mm_tn(da[jh], op(jh)["qe"]) for jh in items}
        inc = {jh: _mm_tn(dob[jh], op(jh)["qe"]) for jh in items}
        dsts = [dst_ref[h] for h in range(GLA_HEADS)]
        dkd1, dgam1 = {}, {}
        for jh in items:
            j, h = jh
            dst = dsts[h]
            dstb = dst.astype(BF16)
            dv_ref[ps[j]["rows"], op(jh)["vs"]] = (dv1[jh] + _mm_nt(op(jh)["kd"], dstb)).astype(BF16)
            dkd1[jh] = _mm(op(jh)["v"], dstb)
            dgam1[jh] = _rowsum(dst * st[jh])
            dsts[h] = dst * ps[j]["gam"][:, op(jh)["ls"]] + inc[jh]
        for h in range(GLA_HEADS):
            dst_ref[h] = dsts[h]
        upper = (ri <= ci).astype(BF16)
        dbs, dbls = [], []
        for j in range(kc):
            p = ps[j]
            tiles = [[op((j, 2 * hp + h2)) for h2 in range(2)] for hp in range(GLA_HEADS // 2)]
            head = lambda d, hp, h2: d[j, 2 * hp + h2]
            lanes = lambda f: jnp.concatenate([f(hp) for hp in range(GLA_HEADS // 2)], axis=1)
            dqe = lanes(lambda hp: sum(jnp.where(tiles[hp][h2]["m"], head(dqe1, hp, h2) + head(dqe2, hp, h2), 0.0)
                                       for h2 in range(2)))
            dke = lanes(lambda hp: head(dke1, hp, 0) + head(dke1, hp, 1))
            dkd = lanes(lambda hp: sum(jnp.where(tiles[hp][h2]["m"], head(dkd1, hp, h2), 0.0) for h2 in range(2)))
            dgam = lanes(lambda hp: head(dgam1, hp, 0) + head(dgam1, hp, 1))
            dqk_ref[p["rows"], :hk] = (dqe * p["eb"] * QK_SCALE).astype(BF16)
            dqk_ref[p["rows"], hk:] = (dke * p["enb"] + dkd * p["ebl"]).astype(BF16)
            dkdkd = dkd * p["kd"]
            dbs.append(dqe * p["qe"] - dke * p["ke"] - dkdkd)
            dbls.append(_rowsum(dkdkd) + dgam * p["gam"])
        dlgs = _tri_mm_all(upper, dbs)
        dzb = []
        dgb = jnp.zeros((1, hk), F32)
        for j in range(kc):
            p = ps[j]
            dz = jnp.where(p["real"], (dlgs[j] + dbls[j]) * (1.0 / GLA_TAU) * _sigmoid(-p["z"]), 0.0)
            dgb = dgb + _rowsum(dz)
            dzb.append(dz.astype(BF16))
        dgup = sum(_mm_tn(ps[j]["gd"].astype(BF16), dzb[j]) for j in range(kc))
        for j in range(kc):
            dgd_ref[ps[j]["rows"], :] = _mm_nt(dzb[j], gup_ref[...]).astype(BF16)
        dgb_ref[...] += dgb
        dgup_ref[...] += dgup
        dgn_ref[...] += dgn

    rowblk = lambda col: (lambda b, t: (b * ns + ns - 1 - t, col))
    const = lambda b, t: (0, 0)
    return pl.pallas_call(
        body, name="gla_bwd", grid=(bsz, ns),
        in_specs=[pl.BlockSpec((kc * CHUNK, QK_WIDTH), rowblk(U_QK)), pl.BlockSpec((kc * CHUNK, hv), rowblk(U_V)),
                  pl.BlockSpec((kc * CHUNK, hv), rowblk(U_R)), pl.BlockSpec((kc * CHUNK, LANES), rowblk(U_GD)),
                  pl.BlockSpec((kc * CHUNK, hv), rowblk(DMI_GLA)),
                  pl.BlockSpec((kc, GLA_HEADS, LANES, LANES), lambda b, t: (b * ns + ns - 1 - t, 0, 0, 0)),
                  pl.BlockSpec((LANES, GUP_WIDTH), const), pl.BlockSpec((1, GUP_WIDTH), const), pl.BlockSpec((1, GLA_DV), const)],
        out_specs=[pl.BlockSpec((kc * CHUNK, 2 * hk), rowblk(0)), pl.BlockSpec((kc * CHUNK, hv), rowblk(0)),
                   pl.BlockSpec((kc * CHUNK, hv), rowblk(0)), pl.BlockSpec((kc * CHUNK, LANES), rowblk(0)),
                   pl.BlockSpec((1, GLA_DV), const), pl.BlockSpec((1, GUP_WIDTH), const),
                   pl.BlockSpec((LANES, GUP_WIDTH), const)],
        out_shape=[_sds((r, 2 * hk), BF16), _sds((r, hv), BF16), _sds((r, hv), BF16), _sds((r, LANES), BF16),
                   _sds((1, GLA_DV), F32), _sds((1, GUP_WIDTH), F32), _sds((LANES, GUP_WIDTH), F32)],
        scratch_shapes=[pltpu.VMEM((GLA_HEADS, LANES, LANES), F32)],
        compiler_params=_params(("arbitrary", "arbitrary")),
    )(u, u, u, u, dmi, sta, gup, gb, gn)


def _conv_bwd(u, c, dmi, w32, cg, cbe, tp, tc, dc):
    r = u.shape[0]
    hb = tc // CONV_HALO
    nhalo = r // CONV_HALO

    def dconv(cv, dco, cg_ref, cbe_ref):
        xhat, rstd = _ln(cv)
        cn = xhat * cg_ref[...] + cbe_ref[...]
        sg = _sigmoid(cn)
        dcn = dco * (sg * (1.0 + cn * (1.0 - sg)))
        return _ln_bwd(dcn * cg_ref[...], xhat, rstd), dcn, xhat

    def body(a_ref, g_ref, ah_ref, gh_ref, c_ref, dco_ref, ch_ref, dcoh_ref, w_ref, cg_ref, cbe_ref,
             du_ref, dw_ref, dcb_ref, dcg_ref, dcbe_ref, hs_ref, dcs_ref, dw8_ref):
        t = pl.program_id(0)

        @pl.when(t == 0)
        def _():
            dw8_ref[...] = jnp.zeros_like(dw8_ref)
            dcb_ref[...] = jnp.zeros_like(dcb_ref)
            dcg_ref[...] = jnp.zeros_like(dcg_ref)
            dcbe_ref[...] = jnp.zeros_like(dcbe_ref)

        first = lax.rem(t * tc, tp) == 0
        last = lax.rem((t + 1) * tc, tp) == 0
        hh = ah_ref[...] * _sigmoid(gh_ref[...])
        hs_ref[0:CONV_HALO, :] = jnp.where(first, 0.0, hh)
        hs_ref[CONV_HALO:CONV_HALO + tc, :] = a_ref[...] * _sigmoid(g_ref[...])
        dch, _, _ = dconv(ch_ref[...], dcoh_ref[...], cg_ref, cbe_ref)
        dcs_ref[tc:tc + CONV_HALO, :] = jnp.where(last, 0.0, dch)

        lrows = tc // 4

        def sub1(k, carry):
            r0 = pl.multiple_of(k * lrows, 8)
            dcv, dcn, xhat = dconv(c_ref[pl.ds(r0, lrows), :], dco_ref[pl.ds(r0, lrows), :], cg_ref, cbe_ref)
            dcs_ref[pl.ds(r0, lrows), :] = dcv
            dcb_ref[...] += _rowsum(dcv)
            dcg_ref[...] += _rowsum(dcn * xhat)
            dcbe_ref[...] += _rowsum(dcn)
            return carry

        lax.fori_loop(0, 4, sub1, 0)

        def sub2(k, carry):
            r0 = pl.multiple_of(k * CONV_SUB, CONV_SUB)
            dwin = dcs_ref[pl.ds(r0, CONV_WIN), :]
            dh = _conv_taps(dwin, lambda o: w_ref[CONV_WIDTH - 1 - o:CONV_WIDTH - o, :], 0)
            av = a_ref[pl.ds(r0, CONV_SUB), :]
            sg = _sigmoid(g_ref[pl.ds(r0, CONV_SUB), :])
            du_ref[pl.ds(r0, CONV_SUB), 0:dc] = (dh * sg).astype(BF16)
            du_ref[pl.ds(r0, CONV_SUB), dc:2 * dc] = (dh * av * sg * (1.0 - sg)).astype(BF16)
            hwin = hs_ref[pl.ds(r0, CONV_WIN), :]
            dcv = dwin[0:CONV_SUB, :]
            for rho in range(8):
                offs = [o for o in range(2, 2 + CONV_WIDTH) if o % 8 == rho]
                rolled = hwin if rho == 0 else pltpu.roll(hwin, CONV_WIN - rho, 0)
                for o in offs:
                    m8 = o - rho
                    prod = dcv * rolled[m8:m8 + CONV_SUB, :]
                    dw8_ref[8 * (o - 2):8 * (o - 1), :] += jnp.sum(prod.reshape(CONV_SUB // 8, 8, dc), axis=0)
            return carry

        lax.fori_loop(0, tc // CONV_SUB, sub2, 0)

        @pl.when(t == pl.num_programs(0) - 1)
        def _():
            dw_ref[...] = jnp.zeros_like(dw_ref)
            for j in range(CONV_WIDTH):
                dw_ref[j:j + 1, :] = _rowsum(dw8_ref[8 * j:8 * (j + 1), :])

    vec = pl.BlockSpec((1, dc), lambda t: (0, 0))
    prev = lambda col: (lambda t: (jnp.maximum(t * hb - 1, 0), col))
    nxt = lambda col: (lambda t: (jnp.minimum((t + 1) * hb, nhalo - 1), col))
    return pl.pallas_call(
        body, name="conv_bwd", grid=(r // tc,),
        in_specs=[pl.BlockSpec((tc, dc), lambda t: (t, 0)), pl.BlockSpec((tc, dc), lambda t: (t, 1)),
                  pl.BlockSpec((CONV_HALO, dc), prev(0)), pl.BlockSpec((CONV_HALO, dc), prev(1)),
                  pl.BlockSpec((tc, dc), lambda t: (t, 0)), pl.BlockSpec((tc, dc), lambda t: (t, 0)),
                  pl.BlockSpec((CONV_HALO, dc), nxt(0)), pl.BlockSpec((CONV_HALO, dc), nxt(0)),
                  pl.BlockSpec((32, dc), lambda t: (0, 0)), vec, vec],
        out_specs=[pl.BlockSpec((tc, 2 * dc), lambda t: (t, 0)), pl.BlockSpec((32, dc), lambda t: (0, 0)), vec, vec, vec],
        out_shape=[_sds((r, 2 * dc), BF16), _sds((32, dc), F32), _sds((1, dc), F32), _sds((1, dc), F32), _sds((1, dc), F32)],
        scratch_shapes=[pltpu.VMEM((CONV_HALO + tc, dc), F32), pltpu.VMEM((tc + CONV_HALO, dc), F32),
                        pltpu.VMEM((8 * 32, dc), F32)],
        compiler_params=_params(("arbitrary",)),
    )(u, u, u, u, c, dmi, c, dmi, w32, cg, cbe)


def _inproj_bwd(dp1, dus, xsrc, g_in, w_in, tp, seq, tx):
    r, d = dp1.shape
    widths = [x.shape[1] for x in dus]
    offs = [sum(widths[:k]) for k in range(len(widths))]
    n = w_in.shape[0]
    nd = len(dus)
    head = tx == 0
    rows = X_OFF if head else tx

    def body(*refs):
        dp_ref = refs[0]
        du_refs = refs[1:1 + nd]
        x_ref, g_ref, w_ref, out_ref, dg_ref, db_ref = refs[1 + nd:]
        i = pl.program_id(0)

        @pl.when(i == 0)
        def _():
            dg_ref[...] = jnp.zeros_like(dg_ref)
            db_ref[...] = jnp.zeros_like(db_ref)
            if head:
                out_ref[...] = jnp.zeros_like(out_ref)

        ds0 = ALPHA * dp_ref[...]
        for k in range(nd):
            ds0 = ds0 + _mm(du_refs[k][...], w_ref[offs[k]:offs[k] + widths[k], :])
        if head:
            ds0 = jnp.where(lax.broadcasted_iota(jnp.int32, (X_OFF, 1), 0) >= PAD_FRONT, ds0, 0.0)
        xhat, rstd = _ln(x_ref[...])
        dg_ref[...] += _rowsum(ds0 * xhat)
        db_ref[...] += _rowsum(ds0)
        dx = _ln_bwd(ds0 * g_ref[...], xhat, rstd)
        if head:
            out_ref[...] += dx[PAD_FRONT:X_OFF, :]
        else:
            out_ref[...] = dx

    if head:
        nb = tp // X_OFF
        row = lambda w: pl.BlockSpec((X_OFF, w), lambda i: (i * nb, 0))
        xspec = pl.BlockSpec((X_OFF, d), lambda i: (0, 0))
        ospec, oshape, steps = pl.BlockSpec((N_META, d), lambda i: (0, 0)), _sds((N_META, d), F32), r // tp
    else:
        start = _x_tile_row(tp, seq, tx)
        row = lambda w: pl.BlockSpec((pl.Element(tx), pl.Element(w)), lambda i: (start(i), 0))
        xspec = pl.BlockSpec((tx, d), lambda i: (i, 0))
        ospec, oshape, steps = xspec, _sds(xsrc.shape, F32), xsrc.shape[0] // tx
    vec = pl.BlockSpec((1, d), lambda i: (0, 0))
    return pl.pallas_call(
        body, name="inproj_bwd_head" if head else "inproj_bwd_x", grid=(steps,),
        in_specs=[row(d)] + [row(w) for w in widths] + [xspec, vec, pl.BlockSpec((n, d), lambda i: (0, 0))],
        out_specs=[ospec, vec, vec],
        out_shape=[oshape, _sds((1, d), F32), _sds((1, d), F32)],
        compiler_params=_params(("arbitrary",)),
    )(dp1, *dus, xsrc, g_in, w_in)


def _inproj_bwd_w(s0, dus, tm):
    r, d = s0.shape
    widths = [x.shape[1] for x in dus]
    offs = [sum(widths[:k]) for k in range(len(widths))]
    nd = len(dus)

    def body(*refs):
        s_ref = refs[0]
        du_refs = refs[1:1 + nd]
        dw_ref, acc_ref = refs[1 + nd:]
        i = pl.program_id(0)

        @pl.when(i == 0)
        def _():
            acc_ref[...] = jnp.zeros_like(acc_ref)

        for k in range(nd):
            acc_ref[offs[k]:offs[k] + widths[k], :] += _mm_tn(du_refs[k][...], s_ref[...])

        @pl.when(i == pl.num_programs(0) - 1)
        def _():
            dw_ref[...] = acc_ref[...].astype(BF16)

    row = lambda w: pl.BlockSpec((tm, w), lambda i: (i, 0))
    return pl.pallas_call(
        body, name="inproj_bwd_w", grid=(r // tm,),
        in_specs=[row(d)] + [row(w) for w in widths],
        out_specs=pl.BlockSpec((sum(widths), d), lambda i: (0, 0)),
        out_shape=_sds((sum(widths), d), BF16),
        scratch_shapes=[pltpu.VMEM((sum(widths), d), F32)],
        compiler_params=_params(("arbitrary",)),
    )(s0, *dus)


def _local_step(x, tgt, meta, ln_in_g, ln_in_b, w_in, conv_w, conv_b, conv_ln_g, conv_ln_b, gate_up, gate_bias,
                gla_norm_g, late_weights, ln1_g, ln1_b, ln2_g, ln2_b, push):
    bsz, seq, d = x.shape
    tp = X_OFF + seq
    assert tp % CHUNK == 0
    nc = tp // CHUNK
    dc = conv_b.shape[1]
    tmm = tc = _pick_tile(tp, (704, 128, 64))
    tw = _pick_tile(bsz * tp, (1408, 128, 64))
    tx = _pick_tile(seq, (512, 64))
    kc = _pick_tile(nc, (11, 3, 2, 1))
    ns = 2

    x2 = x.reshape(bsz * seq, d)
    head = jnp.pad(meta, ((PAD_FRONT, 0), (0, 0)))
    tgt_p = tgt.reshape(bsz * seq, d)
    w32 = jnp.pad(conv_w, ((0, 32 - CONV_WIDTH), (0, 0)))
    gup = jnp.pad(gate_up, ((0, LANES - GLA_RANK), (0, 0))).astype(BF16)

    s0, s0b = _ln_in_x(x2, ln_in_g, ln_in_b, tp, seq, tx)
    s0, s0b = _ln_in_head(head, ln_in_g, ln_in_b, s0, s0b, tp)
    u = _inproj_fwd(s0b, w_in, tmm)
    c, co = _conv_fwd(u, w32, conv_b, conv_ln_g, conv_ln_b, tp, tc, dc)
    go, sta = _gla_fwd(u, gup, gate_bias, gla_norm_g, bsz, nc, kc)
    w_out, w1g, w2 = late_weights
    nh = w1g.shape[0]
    p1, s1, s1b = _outproj_fwd(s0, co, go, w_out, ln1_g, ln1_b, tmm)
    hm, dp2, dpb, loss, dg2, db2 = _mlp_fwd(s1, s1b, w1g, w2, ln2_g, ln2_b, tgt_p, tp, tmm, ns)

    dh, dp1, dg1, db1 = _mlp_bwd_act(dp2, dpb, hm, w1g, w2, p1, ln1_g, tmm, ns)
    dw1, dw2 = _mlp_bwd_w(s1b, hm, dh, dpb, nh, tw, ns)
    push("ff", (dw1, dw2))
    dmi, dwo = _outproj_bwd(dp1, co, go, w_out, tmm)
    push("out", (dwo,))
    dqk, dv, dr, dgd, dgn, dgb, dgup = _gla_bwd(u, dmi, sta, gup, gate_bias, gla_norm_g, bsz, nc, kc)
    dcv, dcw, dcb, dcg, dcbe = _conv_bwd(u, c, dmi, w32, conv_ln_g, conv_ln_b, tp, tc, dc)
    dus = [dcv, dqk, dv, dr, dgd]
    dwi = _inproj_bwd_w(s0b, dus, tmm)
    push("in", (dwi,))
    gx, dgx, dbx = _inproj_bwd(dp1, dus, x2, ln_in_g, w_in, tp, seq, tx)
    dmeta, dgh, dbh = _inproj_bwd(dp1, dus, head, ln_in_g, w_in, tp, seq, 0)

    return dict(loss=loss[0, 0], grad_x=gx.reshape(bsz, seq, d), meta_tokens=dmeta, ln_in_g=dgx + dgh, ln_in_b=dbx + dbh,
                conv_w=dcw[:CONV_WIDTH], conv_b=dcb, conv_ln_g=dcg, conv_ln_b=dcbe,
                gate_up=dgup[:GLA_RANK], gate_bias=dgb, gla_norm_g=dgn, ln1_g=dg1, ln1_b=db1, ln2_g=dg2, ln2_b=db2)


def _exchange(arrays, scatter, name):
    na = len(arrays)
    npeer = N_DEV - 1

    def body(*refs):
        srcs = refs[:na]
        outs = refs[na:2 * na]
        send_sems, recv_sems, local_sems = refs[2 * na:]
        xi, yi, ci = (lax.axis_index(a) for a in MESH_AXES)
        me = 4 * xi + 2 * yi + ci
        copies = []
        for a in range(na):
            own = srcs[a].at[me] if scatter[a] else srcs[a]
            cp = pltpu.make_async_copy(own, outs[a].at[me], local_sems.at[a])
            cp.start()
            copies.append(cp)
        remote = []
        for k in range(1, N_DEV):
            px, py, pc = xi ^ (k >> 2), yi ^ ((k >> 1) & 1), ci ^ (k & 1)
            peer = 4 * px + 2 * py + pc
            for a in range(na):
                src = srcs[a].at[peer] if scatter[a] else srcs[a]
                cp = pltpu.make_async_remote_copy(
                    src_ref=src, dst_ref=outs[a].at[me],
                    send_sem=send_sems.at[a * npeer + k - 1], recv_sem=recv_sems.at[a * npeer + k - 1],
                    device_id=(px, py, pc), device_id_type=pl.DeviceIdType.MESH)
                cp.start()
                remote.append(cp)
        for cp in remote:
            cp.wait()
        for cp in copies:
            cp.wait()

    out_shape = [_sds(a.shape if scatter[i] else (N_DEV,) + a.shape, a.dtype) for i, a in enumerate(arrays)]
    anyspec = pl.BlockSpec(memory_space=pl.ANY)
    return pl.pallas_call(
        body, name=name,
        in_specs=[anyspec] * na, out_specs=[anyspec] * na, out_shape=out_shape,
        scratch_shapes=[pltpu.SemaphoreType.DMA((na * npeer,)), pltpu.SemaphoreType.DMA((na * npeer,)),
                        pltpu.SemaphoreType.DMA((na,))],
    )(*arrays)


def _peers(xi, yi, ci):
    for k in range(1, N_DEV):
        px, py, pc = xi ^ (k >> 2), yi ^ ((k >> 1) & 1), ci ^ (k & 1)
        yield (px, py, pc), 4 * px + 2 * py + pc


def _sc_exchange(arrays, scatter, name, collective_id, after=None):
    na = len(arrays)
    npeer = N_DEV - 1
    ndep = 0 if after is None else 1

    def body(*refs):
        srcs = refs[:na]
        outs = refs[na + ndep:2 * na + ndep]
        send_sems, recv_sems, own_sems = refs[2 * na + ndep:]
        xi, yi, ci = (lax.axis_index(a) for a in MESH_AXES)
        me = 4 * xi + 2 * yi + ci
        barrier = pltpu.get_barrier_semaphore()
        for pos, _ in _peers(xi, yi, ci):
            pl.semaphore_signal(barrier, inc=1, device_id=pos, device_id_type=pl.DeviceIdType.MESH)
        pl.semaphore_wait(barrier, npeer)
        own = [pltpu.make_async_copy(srcs[a].at[me] if scatter[a] else srcs[a], outs[a].at[me], own_sems.at[a])
               for a in range(na)]
        for cp in own:
            cp.start()
        remote = []
        for a in range(na):
            for k, (pos, peer) in enumerate(_peers(xi, yi, ci)):
                cp = pltpu.make_async_remote_copy(
                    src_ref=srcs[a].at[peer] if scatter[a] else srcs[a], dst_ref=outs[a].at[me],
                    send_sem=send_sems.at[a * npeer + k], recv_sem=recv_sems.at[a * npeer + k],
                    device_id=pos, device_id_type=pl.DeviceIdType.MESH)
                cp.start()
                remote.append(cp)
        for cp in own:
            cp.wait()
        for cp in remote:
            cp.wait()

    out_type = [_sds(a.shape if scatter[i] else (N_DEV,) + a.shape, a.dtype) for i, a in enumerate(arrays)]
    sent = sum(a.size * a.dtype.itemsize // (N_DEV if scatter[i] else 1) for i, a in enumerate(arrays))
    return pl.kernel(
        body, out_type=out_type, mesh=plsc.ScalarSubcoreMesh(axis_name="seq", num_cores=1), name=name,
        scratch_types=[pltpu.SemaphoreType.DMA((na * npeer,)), pltpu.SemaphoreType.DMA((na * npeer,)),
                       pltpu.SemaphoreType.DMA((na,))],
        compiler_params=pltpu.CompilerParams(collective_id=collective_id),
        cost_estimate=pl.CostEstimate(flops=0, transcendentals=0, bytes_accessed=2 * N_DEV * sent,
                                      remote_bytes_transferred=npeer * sent),
    )(*arrays, *([] if after is None else [after]))


def _sc_gather(arrays, name, collective_id, after=None):
    na = len(arrays)
    ndep = 0 if after is None else 1
    npair = N_DEV - 1

    def body(*refs):
        srcs = refs[:na]
        outs = refs[na + ndep:2 * na + ndep]
        send_sems, recv_sems, own_sems = refs[2 * na + ndep:]
        xi, yi, ci = (lax.axis_index(a) for a in MESH_AXES)
        me = 4 * xi + 2 * yi + ci
        sibling = (xi, yi, 1 - ci)
        chips = [(1 - xi, yi), (xi, 1 - yi), (1 - xi, 1 - yi)]
        barrier = pltpu.get_barrier_semaphore()
        for pos, _ in _peers(xi, yi, ci):
            pl.semaphore_signal(barrier, inc=1, device_id=pos, device_id_type=pl.DeviceIdType.MESH)
        pl.semaphore_wait(barrier, npair)

        def copy(a, k, src, slot, to):
            return pltpu.make_async_remote_copy(
                src_ref=src, dst_ref=outs[a].at[slot], send_sem=send_sems.at[a * npair + k],
                recv_sem=recv_sems.at[a * npair + k], device_id=to, device_id_type=pl.DeviceIdType.MESH)

        own = [pltpu.make_async_copy(srcs[a], outs[a].at[me], own_sems.at[a]) for a in range(na)]
        for cp in own:
            cp.start()
        sent = []
        for a in range(na):
            sent.append(copy(a, 0, srcs[a], me, sibling))
            sent += [copy(a, 1 + j, srcs[a], me, (*chip, ci)) for j, chip in enumerate(chips)]
        for cp in sent:
            cp.start()
        for j, (cx, cy) in enumerate(chips):
            slot = 4 * cx + 2 * cy + ci
            for a in range(na):
                copy(a, 1 + j, srcs[a], slot, sibling).wait_recv()
                cp = copy(a, 4 + j, outs[a].at[slot], slot, sibling)
                cp.start()
                sent.append(cp)
        for a in range(na):
            copy(a, 0, srcs[a], me, sibling).wait_recv()
            for j in range(len(chips)):
                copy(a, 4 + j, srcs[a], me, sibling).wait_recv()
        for cp in sent:
            cp.wait_send()
        for cp in own:
            cp.wait()

    out_type = [_sds((N_DEV,) + a.shape, a.dtype) for a in arrays]
    sent_bytes = sum(a.size * a.dtype.itemsize for a in arrays)
    return pl.kernel(
        body, out_type=out_type, mesh=plsc.ScalarSubcoreMesh(axis_name="seq", num_cores=1), name=name,
        scratch_types=[pltpu.SemaphoreType.DMA((na * npair,)), pltpu.SemaphoreType.DMA((na * npair,)),
                       pltpu.SemaphoreType.DMA((na,))],
        compiler_params=pltpu.CompilerParams(collective_id=collective_id),
        cost_estimate=pl.CostEstimate(flops=0, transcendentals=0, bytes_accessed=2 * N_DEV * sent_bytes,
                                      remote_bytes_transferred=npair * sent_bytes),
    )(*arrays, *([] if after is None else [after]))


def _adamw(w, g, m, v):
    m = ADAM_B1 * m + (1.0 - ADAM_B1) * g
    v = ADAM_B2 * v + (1.0 - ADAM_B2) * jnp.square(g)
    m_hat = m / (1.0 - ADAM_B1 ** ADAM_STEP)
    v_hat = v / (1.0 - ADAM_B2 ** ADAM_STEP)
    delta = -ADAM_LR * (m_hat / (jnp.sqrt(v_hat) + ADAM_EPS) + ADAM_WD * w)
    return delta, m, v


def _sum_devices(ref):
    g = ref[0].astype(F32)
    for k in range(1, N_DEV):
        g = g + ref[k].astype(F32)
    return g


def _update_big(parts, w, m, v, name):
    rows, cols = w.shape[0], w.shape[-1]
    mid = (None,) * (w.ndim - 2)
    mid0 = (0,) * (w.ndim - 2)

    def body(p_ref, w_ref, m_ref, v_ref, g_ref, d_ref, nm_ref, nv_ref):
        g = _sum_devices(p_ref)
        g_ref[...] = g
        d_ref[...], nm_ref[...], nv_ref[...] = _adamw(w_ref[...], g, m_ref[...], v_ref[...])

    if rows % 16 == 0:
        tr = _pick_tile(rows, (128, 64, 16))
        steps, blk = rows // tr, pl.BlockSpec((tr,) + mid + (cols,), lambda i: (i,) + mid0 + (0,))
        pblk = pl.BlockSpec((N_DEV, tr, cols), lambda i: (0, i, 0))
    else:
        tcol = 2 * LANES
        steps, blk = cols // tcol, pl.BlockSpec((rows,) + mid + (tcol,), lambda i: (0,) + mid0 + (i,))
        pblk = pl.BlockSpec((N_DEV, rows, tcol), lambda i: (0, 0, i))
    return pl.pallas_call(
        body, name=name, grid=(steps,),
        in_specs=[pblk, blk, blk, blk],
        out_specs=[blk] * 4, out_shape=[_sds(w.shape, F32)] * 4,
        compiler_params=_params(("parallel",)),
    )(parts, w, m, v)


_VEC_ORDER = ("ln_in_g", "ln_in_b", "conv_b", "conv_ln_g", "conv_ln_b", "gate_bias", "gla_norm_g",
              "ln1_g", "ln1_b", "ln2_g", "ln2_b")
_SHARDED_SMALL = (("meta_tokens", 0, N_META, LANES), ("conv_w", N_META, CONV_WIDTH, None), ("gate_up", N_META + 32, GLA_RANK, None))


def _update_small(parts_sh, parts_vec, wmv):
    names = [s[0] for s in _SHARDED_SMALL] + list(_VEC_ORDER)
    flat = [a for nme in names for a in wmv[nme]]
    nv = len(_VEC_ORDER)

    def body(*refs):
        sh_ref, vec_ref = refs[0], refs[1]
        ins = refs[2:2 + len(flat)]
        outs = refs[2 + len(flat):2 + len(flat) + 4 * len(names)]
        loss_ref = refs[2 + len(flat) + 4 * len(names)]
        gsh_ref, gvec_ref = refs[-2:]
        gsh_ref[...] = _sum_devices(sh_ref)
        gvec_ref[...] = _sum_devices(vec_ref)
        loss_ref[...] = gvec_ref[nv:nv + 1, :]
        for idx, nme in enumerate(names):
            w_ref, m_ref, v_ref = ins[3 * idx:3 * idx + 3]
            rows, cols = w_ref.shape[0], w_ref.shape[-1]
            at = (slice(None),) + (0,) * (len(w_ref.shape) - 2) + (slice(None),)
            if idx < len(_SHARDED_SMALL):
                r0 = _SHARDED_SMALL[idx][1]
                g = gsh_ref[r0:r0 + rows, 0:cols]
            else:
                j = idx - len(_SHARDED_SMALL)
                g = gvec_ref[j:j + 1, 0:cols]
            o = outs[4 * idx:4 * idx + 4]
            o[0][at] = g
            o[1][at], o[2][at], o[3][at] = _adamw(w_ref[at], g, m_ref[at], v_ref[at])

    out_shape = [_sds(wmv[nme][0].shape, F32) for nme in names for _ in range(4)] + [_sds((1, parts_vec.shape[2]), F32)]
    vmem = pl.BlockSpec(memory_space=pltpu.VMEM)
    res = pl.pallas_call(
        body, name="update_small", out_shape=out_shape,
        in_specs=[vmem] * (2 + len(flat)), out_specs=[vmem] * len(out_shape),
        scratch_shapes=[pltpu.VMEM(parts_sh.shape[1:], F32), pltpu.VMEM(parts_vec.shape[1:], F32)],
    )(parts_sh, parts_vec, *flat)
    return {nme: res[4 * i:4 * i + 4] for i, nme in enumerate(names)}, res[-1][0, 0]


_WEIGHTS = ("meta_tokens", "ln_in_g", "ln_in_b", "w_in", "conv_w", "conv_b", "conv_ln_g", "conv_ln_b", "gate_up",
            "gate_bias", "gla_norm_g", "w_out", "ln1_g", "ln1_b", "w_ff1", "w_ff2", "ln2_g", "ln2_b")


def kernel(x, meta_tokens, ln_in_g, ln_in_b, w_in, conv_w, conv_b, conv_ln_g, conv_ln_b, gate_up, gate_bias, gla_norm_g, w_out, ln1_g, ln1_b, w_ff1, w_ff2, ln2_g, ln2_b, loss_target, m_meta_tokens, m_ln_in_g, m_ln_in_b, m_w_in, m_conv_w, m_conv_b, m_conv_ln_g, m_conv_ln_b, m_gate_up, m_gate_bias, m_gla_norm_g, m_w_out, m_ln1_g, m_ln1_b, m_w_ff1, m_w_ff2, m_ln2_g, m_ln2_b, v_meta_tokens, v_ln_in_g, v_ln_in_b, v_w_in, v_conv_w, v_conv_b, v_conv_ln_g, v_conv_ln_b, v_gate_up, v_gate_bias, v_gla_norm_g, v_w_out, v_ln1_g, v_ln1_b, v_w_ff1, v_w_ff2, v_ln2_g, v_ln2_b):
    w = dict(meta_tokens=meta_tokens, ln_in_g=ln_in_g, ln_in_b=ln_in_b, w_in=w_in, conv_w=conv_w, conv_b=conv_b,
             conv_ln_g=conv_ln_g, conv_ln_b=conv_ln_b, gate_up=gate_up, gate_bias=gate_bias, gla_norm_g=gla_norm_g,
             w_out=w_out, ln1_g=ln1_g, ln1_b=ln1_b, w_ff1=w_ff1, w_ff2=w_ff2, ln2_g=ln2_g, ln2_b=ln2_b)
    mom = dict(meta_tokens=m_meta_tokens, ln_in_g=m_ln_in_g, ln_in_b=m_ln_in_b, w_in=m_w_in, conv_w=m_conv_w,
               conv_b=m_conv_b, conv_ln_g=m_conv_ln_g, conv_ln_b=m_conv_ln_b, gate_up=m_gate_up, gate_bias=m_gate_bias,
               gla_norm_g=m_gla_norm_g, w_out=m_w_out, ln1_g=m_ln1_g, ln1_b=m_ln1_b, w_ff1=m_w_ff1, w_ff2=m_w_ff2,
               ln2_g=m_ln2_g, ln2_b=m_ln2_b)
    var = dict(meta_tokens=v_meta_tokens, ln_in_g=v_ln_in_g, ln_in_b=v_ln_in_b, w_in=v_w_in, conv_w=v_conv_w,
               conv_b=v_conv_b, conv_ln_g=v_conv_ln_g, conv_ln_b=v_conv_ln_b, gate_up=v_gate_up, gate_bias=v_gate_bias,
               gla_norm_g=v_gla_norm_g, w_out=v_w_out, ln1_g=v_ln1_g, ln1_b=v_ln1_b, w_ff1=v_w_ff1, w_ff2=v_w_ff2,
               ln2_g=v_ln2_g, ln2_b=v_ln2_b)
    shapes = {k: a.shape for k, a in w.items()}

    def two_d(a):
        return a.reshape(1, -1) if a.ndim == 1 else a.reshape(a.shape[-2:])

    w2d = {k: two_d(a) for k, a in w.items()}
    m2d = {k: two_d(a) for k, a in mom.items()}
    v2d = {k: two_d(a) for k, a in var.items()}
    d = x.shape[-1]
    d_in = w2d["w_in"].shape[1] * N_DEV
    d_in_p = -(-d_in // LANES) * LANES

    in_wmv = [jnp.transpose(dct["w_in"], (2, 0, 1)) for dct in (w, mom, var)]
    g_in, g_meta, g_conv, g_gup = _sc_gather(
        [w2d["w_in"].T.astype(BF16), w2d["meta_tokens"], w2d["conv_w"], w2d["gate_up"]], "gather_first", 0)
    g_out, g_ff1, g_ff2 = _sc_gather(
        [w2d["w_out"].astype(BF16), w2d["w_ff1"].astype(BF16), w2d["w_ff2"].astype(BF16)], "gather_late", 1)
    w_in_full = jnp.pad(g_in.reshape(d_in, d), ((0, d_in_p - d_in), (0, 0)))
    meta_full = g_meta.transpose(1, 0, 2).reshape(N_META, d)
    conv_w_full = g_conv.transpose(1, 0, 2).reshape(CONV_WIDTH, -1)
    gate_up_full = g_gup.transpose(1, 0, 2).reshape(GLA_RANK, -1)

    late_weights = (g_out.reshape(-1, d), g_ff1, g_ff2.reshape(-1, d))
    pushed = {}

    def push(tag, grads):
        if tag == "ff":
            pushed["ff1"], pushed["ff2"] = _sc_exchange(list(grads), [True, True], "scatter_ff", 2)
        elif tag == "out":
            pushed["p_out"] = grads[0].reshape(N_DEV, -1, d)
        else:
            p_in = grads[0][:d_in].reshape(N_DEV, d_in // N_DEV, d)
            pushed["in"], pushed["out"] = _sc_exchange([p_in, pushed["p_out"]], [True, True], "scatter_rest", 3,
                                                       after=pushed["ff1"])

    res = _local_step(x, loss_target, meta_full, w2d["ln_in_g"], w2d["ln_in_b"], w_in_full, conv_w_full, w2d["conv_b"],
                      w2d["conv_ln_g"], w2d["conv_ln_b"], gate_up_full, w2d["gate_bias"], w2d["gla_norm_g"], late_weights,
                      w2d["ln1_g"], w2d["ln1_b"], w2d["ln2_g"], w2d["ln2_b"], push)

    dc = res["conv_w"].shape[1]
    hk = res["gate_up"].shape[1]
    sh_meta = res["meta_tokens"].reshape(N_META, N_DEV, LANES).transpose(1, 0, 2)
    sh_conv = jnp.pad(res["conv_w"].reshape(CONV_WIDTH, N_DEV, dc // N_DEV).transpose(1, 0, 2),
                      ((0, 0), (0, 32 - CONV_WIDTH), (0, LANES - dc // N_DEV)))
    sh_gup = jnp.pad(res["gate_up"].reshape(GLA_RANK, N_DEV, hk // N_DEV).transpose(1, 0, 2),
                     ((0, 0), (0, 0), (0, LANES - hk // N_DEV)))
    p_sh = jnp.concatenate([sh_meta, sh_conv, sh_gup], axis=1)
    p_vec = jnp.concatenate([jnp.pad(res[k], ((0, 0), (0, d - res[k].shape[1]))) for k in _VEC_ORDER]
                            + [jnp.full((1, d), res["loss"], F32), jnp.zeros((15 - len(_VEC_ORDER), d), F32)], axis=0)

    r_sh, r_vec = _exchange([p_sh, p_vec], [True, False], "scatter_small")
    r_ff1, r_ff2, r_out, r_in = pushed["ff1"], pushed["ff2"], pushed["out"], pushed["in"]

    upd = {}
    upd["w_in"] = [jnp.transpose(a, (1, 2, 0)) for a in _update_big(r_in, *in_wmv, "update_w_in")]
    upd["w_out"] = _update_big(r_out, w2d["w_out"], m2d["w_out"], v2d["w_out"], "update_w_out")
    upd["w_ff1"] = _update_big(r_ff1, w2d["w_ff1"], m2d["w_ff1"], v2d["w_ff1"], "update_w_ff1")
    upd["w_ff2"] = _update_big(r_ff2, w2d["w_ff2"], m2d["w_ff2"], v2d["w_ff2"], "update_w_ff2")
    small = [s[0] for s in _SHARDED_SMALL] + list(_VEC_ORDER)
    wmv = {k: (w2d[k], m2d[k], v2d[k]) for k in small}
    wmv["conv_w"] = tuple(jnp.transpose(dct["conv_w"], (1, 0, 2)) for dct in (w, mom, var))
    upd_small, loss = _update_small(r_sh, r_vec, wmv)
    upd.update(upd_small)

    outs = [loss, res["grad_x"]]
    for j in range(4):
        outs += [upd[k][j].reshape(shapes[k]) for k in _WEIGHTS]
    return tuple(outs)
```

```python
import jax
import jax.numpy as jnp
from jax import lax
from jax.experimental import pallas as pl
from jax.experimental.pallas import tpu as pltpu
from jax.experimental.pallas import tpu_sc as plsc

F32 = jnp.float32
BF16 = jnp.bfloat16

N_META = 16
CHUNK = 64
PAD_FRONT = (-N_META) % CHUNK
X_OFF = PAD_FRONT + N_META
CONV_WIDTH = 31
CONV_HALO = 32
CONV_SUB = 64
CONV_WIN = CONV_SUB + CONV_HALO
GLA_HEADS = 4
GLA_DK = 64
GLA_DV = 128
GLA_RANK = 16
GLA_TAU = 16.0
QK_SCALE = GLA_DK ** -0.5
LN_EPS = 1e-5
ALPHA = 2.0 ** 0.25
LANES = 128
N_DEV = 8
ADAM_LR = 0.001
ADAM_B1 = 0.9
ADAM_B2 = 0.999
ADAM_EPS = 1e-08
ADAM_WD = 0.01
ADAM_STEP = 10
VMEM_LIMIT = 56 * 1024 * 1024
MESH_AXES = ("x", "y", "c")
U_QK, U_V, U_R, U_GD = 2, 3, 4, 20
QK_WIDTH = 2 * GLA_HEADS * GLA_DK
GUP_WIDTH = GLA_HEADS * GLA_DK
DMI_GLA = 1


def _sds(shape, dtype):
    return jax.ShapeDtypeStruct(shape, dtype)


def _mm(a, b):
    return jnp.dot(a, b, preferred_element_type=F32)


def _mm_nt(a, b):
    return lax.dot_general(a, b, (((1,), (1,)), ((), ())), preferred_element_type=F32)


def _mm_tn(a, b):
    return lax.dot_general(a, b, (((0,), (0,)), ((), ())), preferred_element_type=F32)


def _sigmoid(x):
    return 1.0 / (1.0 + jnp.exp(-x))


def _log_sigmoid(z):
    return jnp.minimum(z, 0.0) - jnp.log(1.0 + jnp.exp(-jnp.abs(z)))


def _ln(x):
    mu = jnp.mean(x, axis=-1, keepdims=True)
    xc = x - mu
    var = jnp.mean(xc * xc, axis=-1, keepdims=True)
    rstd = lax.rsqrt(var + LN_EPS)
    return xc * rstd, rstd


def _ln_bwd(dyg, xhat, rstd):
    m1 = jnp.mean(dyg, axis=-1, keepdims=True)
    m2 = jnp.mean(dyg * xhat, axis=-1, keepdims=True)
    return rstd * (dyg - m1 - xhat * m2)


def _rowsum(x):
    return jnp.sum(x, axis=0, keepdims=True)


def _row_in_seq(i, tm, tp):
    base = lax.rem(i * tm, tp)
    return base + lax.broadcasted_iota(jnp.int32, (tm, 1), 0)


def _split3(x):
    hi = x.astype(BF16)
    r1 = x - hi.astype(F32)
    mid = r1.astype(BF16)
    lo = (r1 - mid.astype(F32)).astype(BF16)
    return hi, mid, lo


def _params(sem):
    return pltpu.CompilerParams(dimension_semantics=sem, vmem_limit_bytes=VMEM_LIMIT)


def _pick_tile(n, prefs):
    for t in prefs:
        if n % t == 0:
            return t
    raise ValueError(f"no tile for {n}")


ROW_BLOCKS = 2


def _row_blocks(tm, matmuls, finish):
    blocks = [slice(k * tm // ROW_BLOCKS, (k + 1) * tm // ROW_BLOCKS) for k in range(ROW_BLOCKS)]
    acc = matmuls(blocks[0])
    for prev, rows in zip(blocks, blocks[1:]):
        nxt = matmuls(rows)
        finish(prev, acc)
        acc = nxt
    finish(blocks[-1], acc)


def _x_tile_row(tp, seq, tx):
    tps = seq // tx
    return lambda i: pl.multiple_of((i // tps) * tp + X_OFF + (i % tps) * tx, CHUNK)


def _ln_in_x(x2, g, b, tp, seq, tx):
    rx, d = x2.shape
    r = rx // seq * tp
    row = _x_tile_row(tp, seq, tx)

    def body(x_ref, g_ref, b_ref, s0_ref, sb_ref):
        xhat, _ = _ln(x_ref[...])
        s = xhat * g_ref[...] + b_ref[...]
        s0_ref[...] = s
        sb_ref[...] = s.astype(BF16)

    out = pl.BlockSpec((pl.Element(tx), pl.Element(d)), lambda i: (row(i), 0))
    return pl.pallas_call(
        body, name="ln_in_x", grid=(rx // tx,),
        in_specs=[pl.BlockSpec((tx, d), lambda i: (i, 0)), pl.BlockSpec((1, d), lambda i: (0, 0)),
                  pl.BlockSpec((1, d), lambda i: (0, 0))],
        out_specs=[out, out],
        out_shape=[_sds((r, d), F32), _sds((r, d), BF16)],
        compiler_params=_params(("parallel",)),
    )(x2, g, b)


def _ln_in_head(head, g, b, s0, s0b, tp):
    r, d = s0.shape
    nb = tp // X_OFF

    def body(h_ref, g_ref, b_ref, s0_in, sb_in, s0_ref, sb_ref):
        xhat, _ = _ln(h_ref[...])
        real = lax.broadcasted_iota(jnp.int32, (X_OFF, 1), 0) >= PAD_FRONT
        s = jnp.where(real, xhat * g_ref[...] + b_ref[...], 0.0)
        s0_ref[...] = s
        sb_ref[...] = s.astype(BF16)

    anyspec = pl.BlockSpec(memory_space=pl.ANY)
    out = pl.BlockSpec((X_OFF, d), lambda i: (i * nb, 0))
    return pl.pallas_call(
        body, name="ln_in_head", grid=(r // tp,),
        in_specs=[pl.BlockSpec((X_OFF, d), lambda i: (0, 0)), pl.BlockSpec((1, d), lambda i: (0, 0)),
                  pl.BlockSpec((1, d), lambda i: (0, 0)), anyspec, anyspec],
        out_specs=[out, out],
        out_shape=[_sds((r, d), F32), _sds((r, d), BF16)],
        input_output_aliases={3: 0, 4: 1},
        compiler_params=_params(("parallel",)),
    )(head, g, b, s0, s0b)


def _inproj_fwd(s0b, w_in, tm):
    r, d = s0b.shape
    n = w_in.shape[0]

    def body(s_ref, w_ref, u_ref):
        u_ref[...] = _mm_nt(s_ref[...], w_ref[...])

    return pl.pallas_call(
        body, name="inproj_fwd", grid=(r // tm,),
        in_specs=[pl.BlockSpec((tm, d), lambda i: (i, 0)), pl.BlockSpec((n, d), lambda i: (0, 0))],
        out_specs=pl.BlockSpec((tm, n), lambda i: (i, 0)),
        out_shape=_sds((r, n), F32),
        compiler_params=_params(("parallel",)),
    )(s0b, w_in)


def _conv_taps(win, coef, lo):
    acc = None
    for rho in range(8):
        offs = [o for o in range(lo, lo + CONV_WIDTH) if o % 8 == rho]
        if not offs:
            continue
        rolled = win if rho == 0 else pltpu.roll(win, CONV_WIN - rho, 0)
        for o in offs:
            m8 = o - rho
            term = rolled[m8:m8 + CONV_SUB, :] * coef(o)
            acc = term if acc is None else acc + term
    return acc


def _conv_fwd(u, w32, cb, cg, cbe, tp, tc, dc):
    r = u.shape[0]
    hb = tc // CONV_HALO

    def body(a_ref, g_ref, ah_ref, gh_ref, w_ref, cb_ref, cg_ref, cbe_ref, c_ref, co_ref, hs_ref):
        t = pl.program_id(0)
        first = lax.rem(t * tc, tp) == 0
        hh = ah_ref[...] * _sigmoid(gh_ref[...])
        hs_ref[0:CONV_HALO, :] = jnp.where(first, 0.0, hh)
        hs_ref[CONV_HALO:CONV_HALO + tc, :] = a_ref[...] * _sigmoid(g_ref[...])

        def sub(k, carry):
            r0 = pl.multiple_of(k * CONV_SUB, CONV_SUB)
            win = hs_ref[pl.ds(r0, CONV_WIN), :]
            c = _conv_taps(win, lambda o: w_ref[o - 2:o - 1, :], 2) + cb_ref[...]
            c_ref[pl.ds(r0, CONV_SUB), :] = c
            xhat, _ = _ln(c)
            cn = xhat * cg_ref[...] + cbe_ref[...]
            co_ref[pl.ds(r0, CONV_SUB), :] = (cn * _sigmoid(cn)).astype(BF16)
            return carry

        lax.fori_loop(0, tc // CONV_SUB, sub, 0)

    vec = pl.BlockSpec((1, dc), lambda t: (0, 0))
    return pl.pallas_call(
        body, name="conv_fwd", grid=(r // tc,),
        in_specs=[pl.BlockSpec((tc, dc), lambda t: (t, 0)), pl.BlockSpec((tc, dc), lambda t: (t, 1)),
                  pl.BlockSpec((CONV_HALO, dc), lambda t: (jnp.maximum(t * hb - 1, 0), 0)),
                  pl.BlockSpec((CONV_HALO, dc), lambda t: (jnp.maximum(t * hb - 1, 0), 1)),
                  pl.BlockSpec((32, dc), lambda t: (0, 0)), vec, vec, vec],
        out_specs=[pl.BlockSpec((tc, dc), lambda t: (t, 0)), pl.BlockSpec((tc, dc), lambda t: (t, 0))],
        out_shape=[_sds((r, dc), F32), _sds((r, dc), BF16)],
        scratch_shapes=[pltpu.VMEM((CONV_HALO + tc, dc), F32)],
        compiler_params=_params(("parallel",)),
    )(u, u, u, u, w32, cb, cg, cbe)


def _tri_mm_all(tri, xs):
    parts = [_split3(x) for x in xs]
    acc = [None] * len(xs)
    for t in range(3):
        for j in range(len(xs)):
            term = _mm(tri, parts[j][t])
            acc[j] = term if t == 0 else acc[j] + term
    return acc


def _gla_prep(qk_ref, gd_ref, gup, gb, n0, kc):
    rows = [slice(j * CHUNK, (j + 1) * CHUNK) for j in range(kc)]
    ri = lax.broadcasted_iota(jnp.int32, (CHUNK, CHUNK), 0)
    ci = lax.broadcasted_iota(jnp.int32, (CHUNK, CHUNK), 1)
    low = (ri >= ci).astype(BF16)
    hk = GLA_HEADS * GLA_DK
    gds = [gd_ref[rw, :] for rw in rows]
    zs = [_mm(g.astype(BF16), gup) + gb for g in gds]
    reals = [(n0 + j) * CHUNK + lax.broadcasted_iota(jnp.int32, (CHUNK, 1), 0) >= PAD_FRONT for j in range(kc)]
    lgs = [jnp.where(reals[j], _log_sigmoid(zs[j]) * (1.0 / GLA_TAU), 0.0) for j in range(kc)]
    bs = _tri_mm_all(low, lgs)
    out = []
    for j in range(kc):
        b, bl = bs[j], _rowsum(lgs[j])
        q = qk_ref[rows[j], :hk] * QK_SCALE
        k = qk_ref[rows[j], hk:]
        eb, enb, ebl = jnp.exp(b), jnp.exp(-b), jnp.exp(bl - b)
        out.append(dict(rows=rows[j], gd=gds[j], z=zs[j], real=reals[j], eb=eb, enb=enb, ebl=ebl, gam=jnp.exp(bl),
                        qe=q * eb, ke=k * enb, kd=k * ebl))
    return out, ri, ci


def _gla_heads(p, v_ref):
    ops = []
    for h in range(GLA_HEADS):
        hp, h2 = divmod(h, 2)
        ls = slice(hp * LANES, (hp + 1) * LANES)
        m = _head_mask(h2)
        ops.append(dict(ls=ls, m=m, vs=slice(h * GLA_DV, (h + 1) * GLA_DV),
                        qe=jnp.where(m, p["qe"][:, ls], 0.0).astype(BF16),
                        kd=jnp.where(m, p["kd"][:, ls], 0.0).astype(BF16),
                        ke=p["ke"][:, ls].astype(BF16),
                        v=v_ref[p["rows"], h * GLA_DV:(h + 1) * GLA_DV].astype(BF16)))
    return ops


def _head_mask(h2):
    lane = lax.broadcasted_iota(jnp.int32, (1, LANES), 1)
    return (lane < GLA_DK) if h2 == 0 else (lane >= GLA_DK)


def _gla_fwd(u, gup, gb, gn, bsz, nc, kc):
    r = u.shape[0]
    hv = GLA_HEADS * GLA_DV
    ns = nc // kc

    def body(qk_ref, v_ref, r_ref, gd_ref, gup_ref, gb_ref, gn_ref, go_ref, sta_ref, st_ref):
        t = pl.program_id(1)

        @pl.when(t == 0)
        def _():
            st_ref[...] = jnp.zeros_like(st_ref)

        ps, ri, ci = _gla_prep(qk_ref, gd_ref, gup_ref[...], gb_ref[...], t * kc, kc)
        tril = ri >= ci
        items = [(j, h) for j in range(kc) for h in range(GLA_HEADS)]
        ops = [_gla_heads(p, v_ref) for p in ps]
        a = {jh: jnp.where(tril, _mm_nt(ops[jh[0]][jh[1]]["qe"], ops[jh[0]][jh[1]]["ke"]), 0.0).astype(BF16) for jh in items}
        oi = {jh: _mm(a[jh], ops[jh[0]][jh[1]]["v"]) for jh in items}
        inc = {jh: _mm_tn(ops[jh[0]][jh[1]]["v"], ops[jh[0]][jh[1]]["kd"]) for jh in items}
        sts = [st_ref[h] for h in range(GLA_HEADS)]
        for j, h in items:
            op, p = ops[j][h], ps[j]
            st = sts[h]
            sta_ref[j, h] = st
            o = oi[j, h] + _mm_nt(op["qe"], st.astype(BF16))
            sts[h] = st * p["gam"][:, op["ls"]] + inc[j, h]
            rs = lax.rsqrt(jnp.mean(o * o, axis=-1, keepdims=True) + LN_EPS)
            rr = r_ref[p["rows"], op["vs"]]
            go_ref[p["rows"], op["vs"]] = (o * rs * gn_ref[...] * (rr * _sigmoid(rr))).astype(BF16)
        for h in range(GLA_HEADS):
            st_ref[h] = sts[h]

    rowblk = lambda col: (lambda b, t: (b * ns + t, col))
    const = lambda b, t: (0, 0)
    return pl.pallas_call(
        body, name="gla_fwd", grid=(bsz, ns),
        in_specs=[pl.BlockSpec((kc * CHUNK, QK_WIDTH), rowblk(U_QK)), pl.BlockSpec((kc * CHUNK, hv), rowblk(U_V)),
                  pl.BlockSpec((kc * CHUNK, hv), rowblk(U_R)), pl.BlockSpec((kc * CHUNK, LANES), rowblk(U_GD)),
                  pl.BlockSpec((LANES, GUP_WIDTH), const), pl.BlockSpec((1, GUP_WIDTH), const), pl.BlockSpec((1, GLA_DV), const)],
        out_specs=[pl.BlockSpec((kc * CHUNK, hv), rowblk(0)),
                   pl.BlockSpec((kc, GLA_HEADS, LANES, LANES), lambda b, t: (b * ns + t, 0, 0, 0))],
        out_shape=[_sds((r, hv), BF16), _sds((bsz * nc, GLA_HEADS, LANES, LANES), F32)],
        scratch_shapes=[pltpu.VMEM((GLA_HEADS, LANES, LANES), F32)],
        compiler_params=_params(("parallel", "arbitrary")),
    )(u, u, u, u, gup, gb, gn)


def _outproj_fwd(s0, co, go, w_out, g1, b1, tm):
    r, d = s0.shape
    dc = co.shape[1]

    def body(s0_ref, co_ref, go_ref, w_ref, g_ref, b_ref, p1_ref, s1_ref, s1b_ref):
        nb = 4 if tm % 64 == 0 else 1
        blocks = [slice(k * (tm // nb), (k + 1) * (tm // nb)) for k in range(nb)]
        mixes = [_mm(co_ref[rows, :], w_ref[0:dc, :]) + _mm(go_ref[rows, :], w_ref[dc:2 * dc, :]) for rows in blocks]
        for rows, mix in zip(blocks, mixes):
            p1 = ALPHA * s0_ref[rows, :] + mix
            p1_ref[rows, :] = p1
            xhat, _ = _ln(p1)
            s1 = xhat * g_ref[...] + b_ref[...]
            s1_ref[rows, :] = s1
            s1b_ref[rows, :] = s1.astype(BF16)

    row = lambda w: pl.BlockSpec((tm, w), lambda i: (i, 0))
    vec = pl.BlockSpec((1, d), lambda i: (0, 0))
    return pl.pallas_call(
        body, name="outproj_fwd", grid=(r // tm,),
        in_specs=[row(d), row(dc), row(dc), pl.BlockSpec((2 * dc, d), lambda i: (0, 0)), vec, vec],
        out_specs=[row(d), row(d), row(d)],
        out_shape=[_sds((r, d), F32), _sds((r, d), F32), _sds((r, d), BF16)],
        compiler_params=_params(("parallel",)),
    )(s0, co, go, w_out, g1, b1)


def _mlp_fwd(s1, s1b, w1g, w2, g2, b2, tgt, tp, tm, ns):
    r, d = s1.shape
    nh, _, th = w1g.shape
    nj = nh // ns

    def body(s1_ref, sb_ref, w1_ref, w2_ref, g_ref, b_ref, t_ref, hm_ref, dp2_ref, dpb_ref, loss_ref, dg_ref, db_ref, acc_ref):
        i = pl.program_id(0)
        j = pl.program_id(1)

        @pl.when(jnp.logical_and(i == 0, j == 0))
        def _():
            loss_ref[...] = jnp.zeros_like(loss_ref)
            dg_ref[...] = jnp.zeros_like(dg_ref)
            db_ref[...] = jnp.zeros_like(db_ref)

        @pl.when(j == 0)
        def _():
            acc_ref[...] = jnp.zeros_like(acc_ref)

        def mlp_rows(rows):
            hs = [_mm(sb_ref[rows, :], w1_ref[s]) for s in range(ns)]
            acc = acc_ref[rows, :]
            for s in range(ns):
                hm_ref[rows, s * th:(s + 1) * th] = hs[s].astype(BF16)
                act = jnp.square(jnp.maximum(hs[s], 0.0))
                acc = acc + _mm(act.astype(BF16), w2_ref[s * th:(s + 1) * th, :])
            return acc

        @pl.when(j < nj - 1)
        def _():
            acc_ref[...] = mlp_rows(slice(None))

        @pl.when(j == nj - 1)
        def _():
            isx = _row_in_seq(i, tm, tp) >= X_OFF
            tg = t_ref[...]
            tg = jnp.where(i == 0, pltpu.roll(tg, X_OFF, 0), tg)

            def finish(rows, acc):
                p2 = ALPHA * s1_ref[rows, :] + acc
                xhat, rstd = _ln(p2)
                s2 = xhat * g_ref[...] + b_ref[...]
                err = jnp.where(isx[rows], s2 - tg[rows], 0.0)
                loss_ref[...] += 0.5 * jnp.sum(jnp.mean(err * err, axis=-1, keepdims=True))
                dy = err * (1.0 / d)
                dg_ref[...] += _rowsum(dy * xhat)
                db_ref[...] += _rowsum(dy)
                dp2 = _ln_bwd(dy * g_ref[...], xhat, rstd)
                dp2_ref[rows, :] = dp2
                dpb_ref[rows, :] = dp2.astype(BF16)

            _row_blocks(tm, mlp_rows, finish)

    row = pl.BlockSpec((tm, d), lambda i, j: (i, 0))
    vec = pl.BlockSpec((1, d), lambda i, j: (0, 0))
    tgt_row = pl.BlockSpec((pl.Element(tm), pl.Element(d)),
                           lambda i, j: (pl.multiple_of(jnp.maximum(i * tm - X_OFF * ((i * tm) // tp + 1), 0), CHUNK), 0))
    return pl.pallas_call(
        body, name="mlp_fwd", grid=(r // tm, nj),
        in_specs=[row, row, pl.BlockSpec((ns, d, th), lambda i, j: (j, 0, 0)), pl.BlockSpec((ns * th, d), lambda i, j: (j, 0)),
                  vec, vec, tgt_row],
        out_specs=[pl.BlockSpec((tm, ns * th), lambda i, j: (i, j)), row, row,
                   pl.BlockSpec((8, LANES), lambda i, j: (0, 0)), vec, vec],
        out_shape=[_sds((r, nh * th), BF16), _sds((r, d), F32), _sds((r, d), BF16), _sds((8, LANES), F32),
                   _sds((1, d), F32), _sds((1, d), F32)],
        scratch_shapes=[pltpu.VMEM((tm, d), F32)],
        compiler_params=_params(("arbitrary", "arbitrary")),
    )(s1, s1b, w1g, w2, g2, b2, tgt)


def _mlp_bwd_act(dp2, dpb, hm, w1g, w2, p1, g1, tm, ns):
    r, d = dp2.shape
    nh, _, th = w1g.shape
    nj = nh // ns

    def body(dp2_ref, dpb_ref, hm_ref, w1_ref, w2_ref, p1_ref, g_ref, dh_ref, dp1_ref, dg_ref, db_ref, acc_ref):
        i = pl.program_id(0)
        j = pl.program_id(1)

        @pl.when(jnp.logical_and(i == 0, j == 0))
        def _():
            dg_ref[...] = jnp.zeros_like(dg_ref)
            db_ref[...] = jnp.zeros_like(db_ref)

        @pl.when(j == 0)
        def _():
            acc_ref[...] = jnp.zeros_like(acc_ref)

        def mlp_rows(rows):
            dacts = [_mm_nt(dpb_ref[rows, :], w2_ref[s * th:(s + 1) * th, :]) for s in range(ns)]
            acc = acc_ref[rows, :]
            for s in range(ns):
                cols = slice(s * th, (s + 1) * th)
                dh = (dacts[s] * (2.0 * jnp.maximum(hm_ref[rows, cols].astype(F32), 0.0))).astype(BF16)
                dh_ref[rows, cols] = dh
                acc = acc + _mm_nt(dh, w1_ref[s])
            return acc

        @pl.when(j < nj - 1)
        def _():
            acc_ref[...] = mlp_rows(slice(None))

        @pl.when(j == nj - 1)
        def _():
            def finish(rows, acc):
                ds1 = ALPHA * dp2_ref[rows, :] + acc
                xhat, rstd = _ln(p1_ref[rows, :])
                dg_ref[...] += _rowsum(ds1 * xhat)
                db_ref[...] += _rowsum(ds1)
                dp1_ref[rows, :] = _ln_bwd(ds1 * g_ref[...], xhat, rstd)

            _row_blocks(tm, mlp_rows, finish)

    row = pl.BlockSpec((tm, d), lambda i, j: (i, 0))
    vec = pl.BlockSpec((1, d), lambda i, j: (0, 0))
    blk = pl.BlockSpec((tm, ns * th), lambda i, j: (i, j))
    return pl.pallas_call(
        body, name="mlp_bwd_act", grid=(r // tm, nj),
        in_specs=[row, row, blk, pl.BlockSpec((ns, d, th), lambda i, j: (j, 0, 0)),
                  pl.BlockSpec((ns * th, d), lambda i, j: (j, 0)), row, vec],
        out_specs=[blk, row, vec, vec],
        out_shape=[_sds((r, nh * th), BF16), _sds((r, d), F32), _sds((1, d), F32), _sds((1, d), F32)],
        scratch_shapes=[pltpu.VMEM((tm, d), F32)],
        compiler_params=_params(("arbitrary", "arbitrary")),
    )(dp2, dpb, hm, w1g, w2, p1, g1)


def _mlp_bwd_w(s1b, hm, dh, dpb, nh, tm, ns):
    r, d = s1b.shape
    th = hm.shape[1] // nh

    def body(s1_ref, hm_ref, dh_ref, dp2_ref, dw1_ref, dw2_ref, a1_ref, a2_ref):
        i = pl.program_id(1)

        @pl.when(i == 0)
        def _():
            a1_ref[...] = jnp.zeros_like(a1_ref)
            a2_ref[...] = jnp.zeros_like(a2_ref)

        for s in range(ns):
            a1_ref[s] += _mm_tn(s1_ref[...], dh_ref[:, s * th:(s + 1) * th])
        for s in range(ns):
            act = jnp.square(jnp.maximum(hm_ref[:, s * th:(s + 1) * th].astype(F32), 0.0)).astype(BF16)
            a2_ref[s] += _mm_tn(act, dp2_ref[...])

        @pl.when(i == pl.num_programs(1) - 1)
        def _():
            dw1_ref[...] = a1_ref[...].astype(BF16)
            dw2_ref[...] = a2_ref[...].astype(BF16)

    row = pl.BlockSpec((tm, d), lambda j, i: (i, 0))
    blk = pl.BlockSpec((tm, ns * th), lambda j, i: (i, j))
    return pl.pallas_call(
        body, name="mlp_bwd_w", grid=(nh // ns, r // tm),
        in_specs=[row, blk, blk, row],
        out_specs=[pl.BlockSpec((ns, d, th), lambda j, i: (j, 0, 0)), pl.BlockSpec((ns, th, d), lambda j, i: (j, 0, 0))],
        out_shape=[_sds((nh, d, th), BF16), _sds((nh, th, d), BF16)],
        scratch_shapes=[pltpu.VMEM((ns, d, th), F32), pltpu.VMEM((ns, th, d), F32)],
        compiler_params=_params(("parallel", "arbitrary")),
    )(s1b, hm, dh, dpb)


def _outproj_bwd(dp1, co, go, w_out, tm):
    r, d = dp1.shape
    dc = co.shape[1]

    def body(dp_ref, co_ref, go_ref, w_ref, dmi_ref, dw_ref, acc_ref):
        i = pl.program_id(0)

        @pl.when(i == 0)
        def _():
            acc_ref[...] = jnp.zeros_like(acc_ref)

        dpb = dp_ref[...].astype(BF16)
        dmi_ref[...] = _mm_nt(dpb, w_ref[...])
        acc_ref[0:dc, :] += _mm_tn(co_ref[...], dpb)
        acc_ref[dc:2 * dc, :] += _mm_tn(go_ref[...], dpb)

        @pl.when(i == pl.num_programs(0) - 1)
        def _():
            dw_ref[...] = acc_ref[...].astype(BF16)

    row = lambda w: pl.BlockSpec((tm, w), lambda i: (i, 0))
    full = pl.BlockSpec((2 * dc, d), lambda i: (0, 0))
    return pl.pallas_call(
        body, name="outproj_bwd", grid=(r // tm,),
        in_specs=[row(d), row(dc), row(dc), full],
        out_specs=[row(2 * dc), full],
        out_shape=[_sds((r, 2 * dc), F32), _sds((2 * dc, d), BF16)],
        scratch_shapes=[pltpu.VMEM((2 * dc, d), F32)],
        compiler_params=_params(("arbitrary",)),
    )(dp1, co, go, w_out)


def _gla_bwd(u, dmi, sta, gup, gb, gn, bsz, nc, kc):
    r = u.shape[0]
    hv = GLA_HEADS * GLA_DV
    hk = GLA_HEADS * GLA_DK
    ns = nc // kc

    def body(qk_ref, v_ref, r_ref, gd_ref, dgo_ref, sta_ref, gup_ref, gb_ref, gn_ref,
             dqk_ref, dv_ref, dr_ref, dgd_ref, dgn_ref, dgb_ref, dgup_ref, dst_ref):
        bi = pl.program_id(0)
        t = pl.program_id(1)

        @pl.when(jnp.logical_and(bi == 0, t == 0))
        def _():
            dgn_ref[...] = jnp.zeros_like(dgn_ref)
            dgb_ref[...] = jnp.zeros_like(dgb_ref)
            dgup_ref[...] = jnp.zeros_like(dgup_ref)

        @pl.when(t == 0)
        def _():
            dst_ref[...] = jnp.zeros_like(dst_ref)

        ps, ri, ci = _gla_prep(qk_ref, gd_ref, gup_ref[...], gb_ref[...], (ns - 1 - t) * kc, kc)
        tril = ri >= ci
        items = [(j, h) for j in reversed(range(kc)) for h in range(GLA_HEADS)]
        ops = [_gla_heads(p, v_ref) for p in ps]
        op = lambda jh: ops[jh[0]][jh[1]]
        st = {jh: sta_ref[jh[0], jh[1]] for jh in items}
        stb = {jh: st[jh].astype(BF16) for jh in items}
        a = {jh: jnp.where(tril, _mm_nt(op(jh)["qe"], op(jh)["ke"]), 0.0).astype(BF16) for jh in items}
        o1 = {jh: _mm(a[jh], op(jh)["v"]) for jh in items}
        o2 = {jh: _mm_nt(op(jh)["qe"], stb[jh]) for jh in items}
        dob = {}
        dgn = jnp.zeros((1, GLA_DV), F32)
        for jh in items:
            rows, vs = ps[jh[0]]["rows"], op(jh)["vs"]
            o = o1[jh] + o2[jh]
            rr = r_ref[rows, vs]
            sr = _sigmoid(rr)
            rs = lax.rsqrt(jnp.mean(o * o, axis=-1, keepdims=True) + LN_EPS)
            y = o * rs
            dgo = dgo_ref[rows, vs]
            don = dgo * (rr * sr)
            dr_ref[rows, vs] = (dgo * (y * gn_ref[...]) * (sr * (1.0 + rr * (1.0 - sr)))).astype(BF16)
            dgn = dgn + _rowsum(don * y)
            dxn = don * gn_ref[...]
            dob[jh] = (rs * (dxn - y * jnp.mean(dxn * y, axis=-1, keepdims=True))).astype(BF16)
        da = {jh: jnp.where(tril, _mm_nt(dob[jh], op(jh)["v"]), 0.0).astype(BF16) for jh in items}
        dv1 = {jh: _mm_tn(a[jh], dob[jh]) for jh in items}
        dqe1 = {jh: _mm(da[jh], op(jh)["ke"]) for jh in items}
        dqe2 = {jh: _mm(dob[jh], stb[jh]) for jh in items}
        dke1 = {jh: _mm_tn(da[jh], op(jh)["qe"]) for jh in items}
        inc = {jh: _mm_tn(dob[jh], op(jh)["qe"]) for jh in items}
        dsts = [dst_ref[h] for h in range(GLA_HEADS)]
        dkd1, dgam1 = {}, {}
        for jh in items:
            j, h = jh
            dst = dsts[h]
            dstb = dst.astype(BF16)
            dv_ref[ps[j]["rows"], op(jh)["vs"]] = (dv1[jh] + _mm_nt(op(jh)["kd"], dstb)).astype(BF16)
            dkd1[jh] = _mm(op(jh)["v"], dstb)
            dgam1[jh] = _rowsum(dst * st[jh])
            dsts[h] = dst * ps[j]["gam"][:, op(jh)["ls"]] + inc[jh]
        for h in range(GLA_HEADS):
            dst_ref[h] = dsts[h]
        upper = (ri <= ci).astype(BF16)
        dbs, dbls = [], []
        for j in range(kc):
            p = ps[j]
            tiles = [[op((j, 2 * hp + h2)) for h2 in range(2)] for hp in range(GLA_HEADS // 2)]
            head = lambda d, hp, h2: d[j, 2 * hp + h2]
            lanes = lambda f: jnp.concatenate([f(hp) for hp in range(GLA_HEADS // 2)], axis=1)
            dqe = lanes(lambda hp: sum(jnp.where(tiles[hp][h2]["m"], head(dqe1, hp, h2) + head(dqe2, hp, h2), 0.0)
                                       for h2 in range(2)))
            dke = lanes(lambda hp: head(dke1, hp, 0) + head(dke1, hp, 1))
            dkd = lanes(lambda hp: sum(jnp.where(tiles[hp][h2]["m"], head(dkd1, hp, h2), 0.0) for h2 in range(2)))
            dgam = lanes(lambda hp: head(dgam1, hp, 0) + head(dgam1, hp, 1))
            dqk_ref[p["rows"], :hk] = (dqe * p["eb"] * QK_SCALE).astype(BF16)
            dqk_ref[p["rows"], hk:] = (dke * p["enb"] + dkd * p["ebl"]).astype(BF16)
            dkdkd = dkd * p["kd"]
            dbs.append(dqe * p["qe"] - dke * p["ke"] - dkdkd)
            dbls.append(_rowsum(dkdkd) + dgam * p["gam"])
        dlgs = _tri_mm_all(upper, dbs)
        dzb = []
        dgb = jnp.zeros((1, hk), F32)
        for j in range(kc):
            p = ps[j]
            dz = jnp.where(p["real"], (dlgs[j] + dbls[j]) * (1.0 / GLA_TAU) * _sigmoid(-p["z"]), 0.0)
            dgb = dgb + _rowsum(dz)
            dzb.append(dz.astype(BF16))
        dgup = sum(_mm_tn(ps[j]["gd"].astype(BF16), dzb[j]) for j in range(kc))
        for j in range(kc):
            dgd_ref[ps[j]["rows"], :] = _mm_nt(dzb[j], gup_ref[...]).astype(BF16)
        dgb_ref[...] += dgb
        dgup_ref[...] += dgup
        dgn_ref[...] += dgn

    rowblk = lambda col: (lambda b, t: (b * ns + ns - 1 - t, col))
    const = lambda b, t: (0, 0)
    return pl.pallas_call(
        body, name="gla_bwd", grid=(bsz, ns),
        in_specs=[pl.BlockSpec((kc * CHUNK, QK_WIDTH), rowblk(U_QK)), pl.BlockSpec((kc * CHUNK, hv), rowblk(U_V)),
                  pl.BlockSpec((kc * CHUNK, hv), rowblk(U_R)), pl.BlockSpec((kc * CHUNK, LANES), rowblk(U_GD)),
                  pl.BlockSpec((kc * CHUNK, hv), rowblk(DMI_GLA)),
                  pl.BlockSpec((kc, GLA_HEADS, LANES, LANES), lambda b, t: (b * ns + ns - 1 - t, 0, 0, 0)),
                  pl.BlockSpec((LANES, GUP_WIDTH), const), pl.BlockSpec((1, GUP_WIDTH), const), pl.BlockSpec((1, GLA_DV), const)],
        out_specs=[pl.BlockSpec((kc * CHUNK, 2 * hk), rowblk(0)), pl.BlockSpec((kc * CHUNK, hv), rowblk(0)),
                   pl.BlockSpec((kc * CHUNK, hv), rowblk(0)), pl.BlockSpec((kc * CHUNK, LANES), rowblk(0)),
                   pl.BlockSpec((1, GLA_DV), const), pl.BlockSpec((1, GUP_WIDTH), const),
                   pl.BlockSpec((LANES, GUP_WIDTH), const)],
        out_shape=[_sds((r, 2 * hk), BF16), _sds((r, hv), BF16), _sds((r, hv), BF16), _sds((r, LANES), BF16),
                   _sds((1, GLA_DV), F32), _sds((1, GUP_WIDTH), F32), _sds((LANES, GUP_WIDTH), F32)],
        scratch_shapes=[pltpu.VMEM((GLA_HEADS, LANES, LANES), F32)],
        compiler_params=_params(("arbitrary", "arbitrary")),
    )(u, u, u, u, dmi, sta, gup, gb, gn)


def _conv_bwd(u, c, dmi, w32, cg, cbe, tp, tc, dc):
    r = u.shape[0]
    hb = tc // CONV_HALO
    nhalo = r // CONV_HALO

    def dconv(cv, dco, cg_ref, cbe_ref):
        xhat, rstd = _ln(cv)
        cn = xhat * cg_ref[...] + cbe_ref[...]
        sg = _sigmoid(cn)
        dcn = dco * (sg * (1.0 + cn * (1.0 - sg)))
        return _ln_bwd(dcn * cg_ref[...], xhat, rstd), dcn, xhat

    def body(a_ref, g_ref, ah_ref, gh_ref, c_ref, dco_ref, ch_ref, dcoh_ref, w_ref, cg_ref, cbe_ref,
             du_ref, dw_ref, dcb_ref, dcg_ref, dcbe_ref, hs_ref, dcs_ref, dw8_ref):
        t = pl.program_id(0)

        @pl.when(t == 0)
        def _():
            dw8_ref[...] = jnp.zeros_like(dw8_ref)
            dcb_ref[...] = jnp.zeros_like(dcb_ref)
            dcg_ref[...] = jnp.zeros_like(dcg_ref)
            dcbe_ref[...] = jnp.zeros_like(dcbe_ref)

        first = lax.rem(t * tc, tp) == 0
        last = lax.rem((t + 1) * tc, tp) == 0
        hh = ah_ref[...] * _sigmoid(gh_ref[...])
        hs_ref[0:CONV_HALO, :] = jnp.where(first, 0.0, hh)
        hs_ref[CONV_HALO:CONV_HALO + tc, :] = a_ref[...] * _sigmoid(g_ref[...])
        dch, _, _ = dconv(ch_ref[...], dcoh_ref[...], cg_ref, cbe_ref)
        dcs_ref[tc:tc + CONV_HALO, :] = jnp.where(last, 0.0, dch)

        lrows = tc // 4

        def sub1(k, carry):
            r0 = pl.multiple_of(k * lrows, 8)
            dcv, dcn, xhat = dconv(c_ref[pl.ds(r0, lrows), :], dco_ref[pl.ds(r0, lrows), :], cg_ref, cbe_ref)
            dcs_ref[pl.ds(r0, lrows), :] = dcv
            dcb_ref[...] += _rowsum(dcv)
            dcg_ref[...] += _rowsum(dcn * xhat)
            dcbe_ref[...] += _rowsum(dcn)
            return carry

        lax.fori_loop(0, 4, sub1, 0)

        def sub2(k, carry):
            r0 = pl.multiple_of(k * CONV_SUB, CONV_SUB)
            dwin = dcs_ref[pl.ds(r0, CONV_WIN), :]
            dh = _conv_taps(dwin, lambda o: w_ref[CONV_WIDTH - 1 - o:CONV_WIDTH - o, :], 0)
            av = a_ref[pl.ds(r0, CONV_SUB), :]
            sg = _sigmoid(g_ref[pl.ds(r0, CONV_SUB), :])
            du_ref[pl.ds(r0, CONV_SUB), 0:dc] = (dh * sg).astype(BF16)
            du_ref[pl.ds(r0, CONV_SUB), dc:2 * dc] = (dh * av * sg * (1.0 - sg)).astype(BF16)
            hwin = hs_ref[pl.ds(r0, CONV_WIN), :]
            dcv = dwin[0:CONV_SUB, :]
            for rho in range(8):
                offs = [o for o in range(2, 2 + CONV_WIDTH) if o % 8 == rho]
                rolled = hwin if rho == 0 else pltpu.roll(hwin, CONV_WIN - rho, 0)
                for o in offs:
                    m8 = o - rho
                    prod = dcv * rolled[m8:m8 + CONV_SUB, :]
                    dw8_ref[8 * (o - 2):8 * (o - 1), :] += jnp.sum(prod.reshape(CONV_SUB // 8, 8, dc), axis=0)
            return carry

        lax.fori_loop(0, tc // CONV_SUB, sub2, 0)

        @pl.when(t == pl.num_programs(0) - 1)
        def _():
            dw_ref[...] = jnp.zeros_like(dw_ref)
            for j in range(CONV_WIDTH):
                dw_ref[j:j + 1, :] = _rowsum(dw8_ref[8 * j:8 * (j + 1), :])

    vec = pl.BlockSpec((1, dc), lambda t: (0, 0))
    prev = lambda col: (lambda t: (jnp.maximum(t * hb - 1, 0), col))
    nxt = lambda col: (lambda t: (jnp.minimum((t + 1) * hb, nhalo - 1), col))
    return pl.pallas_call(
        body, name="conv_bwd", grid=(r // tc,),
        in_specs=[pl.BlockSpec((tc, dc), lambda t: (t, 0)), pl.BlockSpec((tc, dc), lambda t: (t, 1)),
                  pl.BlockSpec((CONV_HALO, dc), prev(0)), pl.BlockSpec((CONV_HALO, dc), prev(1)),
                  pl.BlockSpec((tc, dc), lambda t: (t, 0)), pl.BlockSpec((tc, dc), lambda t: (t, 0)),
                  pl.BlockSpec((CONV_HALO, dc), nxt(0)), pl.BlockSpec((CONV_HALO, dc), nxt(0)),
                  pl.BlockSpec((32, dc), lambda t: (0, 0)), vec, vec],
        out_specs=[pl.BlockSpec((tc, 2 * dc), lambda t: (t, 0)), pl.BlockSpec((32, dc), lambda t: (0, 0)), vec, vec, vec],
        out_shape=[_sds((r, 2 * dc), BF16), _sds((32, dc), F32), _sds((1, dc), F32), _sds((1, dc), F32), _sds((1, dc), F32)],
        scratch_shapes=[pltpu.VMEM((CONV_HALO + tc, dc), F32), pltpu.VMEM((tc + CONV_HALO, dc), F32),
                        pltpu.VMEM((8 * 32, dc), F32)],
        compiler_params=_params(("arbitrary",)),
    )(u, u, u, u, c, dmi, c, dmi, w32, cg, cbe)


def _inproj_bwd(dp1, dus, xsrc, g_in, w_in, tp, seq, tx):
    r, d = dp1.shape
    widths = [x.shape[1] for x in dus]
    offs = [sum(widths[:k]) for k in range(len(widths))]
    n = w_in.shape[0]
    nd = len(dus)
    head = tx == 0
    rows = X_OFF if head else tx

    def body(*refs):
        dp_ref = refs[0]
        du_refs = refs[1:1 + nd]
        x_ref, g_ref, w_ref, out_ref, dg_ref, db_ref = refs[1 + nd:]
        i = pl.program_id(0)

        @pl.when(i == 0)
        def _():
            dg_ref[...] = jnp.zeros_like(dg_ref)
            db_ref[...] = jnp.zeros_like(db_ref)
            if head:
                out_ref[...] = jnp.zeros_like(out_ref)

        ds0 = ALPHA * dp_ref[...]
        for k in range(nd):
            ds0 = ds0 + _mm(du_refs[k][...], w_ref[offs[k]:offs[k] + widths[k], :])
        if head:
            ds0 = jnp.where(lax.broadcasted_iota(jnp.int32, (X_OFF, 1), 0) >= PAD_FRONT, ds0, 0.0)
        xhat, rstd = _ln(x_ref[...])
        dg_ref[...] += _rowsum(ds0 * xhat)
        db_ref[...] += _rowsum(ds0)
        dx = _ln_bwd(ds0 * g_ref[...], xhat, rstd)
        if head:
            out_ref[...] += dx[PAD_FRONT:X_OFF, :]
        else:
            out_ref[...] = dx

    if head:
        nb = tp // X_OFF
        row = lambda w: pl.BlockSpec((X_OFF, w), lambda i: (i * nb, 0))
        xspec = pl.BlockSpec((X_OFF, d), lambda i: (0, 0))
        ospec, oshape, steps = pl.BlockSpec((N_META, d), lambda i: (0, 0)), _sds((N_META, d), F32), r // tp
    else:
        start = _x_tile_row(tp, seq, tx)
        row = lambda w: pl.BlockSpec((pl.Element(tx), pl.Element(w)), lambda i: (start(i), 0))
        xspec = pl.BlockSpec((tx, d), lambda i: (i, 0))
        ospec, oshape, steps = xspec, _sds(xsrc.shape, F32), xsrc.shape[0] // tx
    vec = pl.BlockSpec((1, d), lambda i: (0, 0))
    return pl.pallas_call(
        body, name="inproj_bwd_head" if head else "inproj_bwd_x", grid=(steps,),
        in_specs=[row(d)] + [row(w) for w in widths] + [xspec, vec, pl.BlockSpec((n, d), lambda i: (0, 0))],
        out_specs=[ospec, vec, vec],
        out_shape=[oshape, _sds((1, d), F32), _sds((1, d), F32)],
        compiler_params=_params(("arbitrary",)),
    )(dp1, *dus, xsrc, g_in, w_in)


def _inproj_bwd_w(s0, dus, tm):
    r, d = s0.shape
    widths = [x.shape[1] for x in dus]
    offs = [sum(widths[:k]) for k in range(len(widths))]
    nd = len(dus)

    def body(*refs):
        s_ref = refs[0]
        du_refs = refs[1:1 + nd]
        dw_ref, acc_ref = refs[1 + nd:]
        i = pl.program_id(0)

        @pl.when(i == 0)
        def _():
            acc_ref[...] = jnp.zeros_like(acc_ref)

        for k in range(nd):
            acc_ref[offs[k]:offs[k] + widths[k], :] += _mm_tn(du_refs[k][...], s_ref[...])

        @pl.when(i == pl.num_programs(0) - 1)
        def _():
            dw_ref[...] = acc_ref[...].astype(BF16)

    row = lambda w: pl.BlockSpec((tm, w), lambda i: (i, 0))
    return pl.pallas_call(
        body, name="inproj_bwd_w", grid=(r // tm,),
        in_specs=[row(d)] + [row(w) for w in widths],
        out_specs=pl.BlockSpec((sum(widths), d), lambda i: (0, 0)),
        out_shape=_sds((sum(widths), d), BF16),
        scratch_shapes=[pltpu.VMEM((sum(widths), d), F32)],
        compiler_params=_params(("arbitrary",)),
    )(s0, *dus)


def _local_step(x, tgt, meta, ln_in_g, ln_in_b, w_in, conv_w, conv_b, conv_ln_g, conv_ln_b, gate_up, gate_bias,
                gla_norm_g, late_weights, ln1_g, ln1_b, ln2_g, ln2_b, push):
    bsz, seq, d = x.shape
    tp = X_OFF + seq
    assert tp % CHUNK == 0
    nc = tp // CHUNK
    dc = conv_b.shape[1]
    tmm = tc = _pick_tile(tp, (704, 128, 64))
    tx = _pick_tile(seq, (512, 64))
    kc = _pick_tile(nc, (11, 3, 2, 1))
    ns = 2

    x2 = x.reshape(bsz * seq, d)
    head = jnp.pad(meta, ((PAD_FRONT, 0), (0, 0)))
    tgt_p = tgt.reshape(bsz * seq, d)
    w32 = jnp.pad(conv_w, ((0, 32 - CONV_WIDTH), (0, 0)))
    gup = jnp.pad(gate_up, ((0, LANES - GLA_RANK), (0, 0))).astype(BF16)

    s0, s0b = _ln_in_x(x2, ln_in_g, ln_in_b, tp, seq, tx)
    s0, s0b = _ln_in_head(head, ln_in_g, ln_in_b, s0, s0b, tp)
    u = _inproj_fwd(s0b, w_in, tmm)
    c, co = _conv_fwd(u, w32, conv_b, conv_ln_g, conv_ln_b, tp, tc, dc)
    go, sta = _gla_fwd(u, gup, gate_bias, gla_norm_g, bsz, nc, kc)
    w_out, w1g, w2 = late_weights
    nh = w1g.shape[0]
    p1, s1, s1b = _outproj_fwd(s0, co, go, w_out, ln1_g, ln1_b, tmm)
    hm, dp2, dpb, loss, dg2, db2 = _mlp_fwd(s1, s1b, w1g, w2, ln2_g, ln2_b, tgt_p, tp, tmm, ns)

    dh, dp1, dg1, db1 = _mlp_bwd_act(dp2, dpb, hm, w1g, w2, p1, ln1_g, tmm, ns)
    dw1, dw2 = _mlp_bwd_w(s1b, hm, dh, dpb, nh, tmm, ns)
    push("ff", (dw1, dw2))
    dmi, dwo = _outproj_bwd(dp1, co, go, w_out, tmm)
    push("out", (dwo,))
    dqk, dv, dr, dgd, dgn, dgb, dgup = _gla_bwd(u, dmi, sta, gup, gate_bias, gla_norm_g, bsz, nc, kc)
    dcv, dcw, dcb, dcg, dcbe = _conv_bwd(u, c, dmi, w32, conv_ln_g, conv_ln_b, tp, tc, dc)
    dus = [dcv, dqk, dv, dr, dgd]
    dwi = _inproj_bwd_w(s0b, dus, tmm)
    push("in", (dwi,))
    gx, dgx, dbx = _inproj_bwd(dp1, dus, x2, ln_in_g, w_in, tp, seq, tx)
    dmeta, dgh, dbh = _inproj_bwd(dp1, dus, head, ln_in_g, w_in, tp, seq, 0)

    return dict(loss=loss[0, 0], grad_x=gx.reshape(bsz, seq, d), meta_tokens=dmeta, ln_in_g=dgx + dgh, ln_in_b=dbx + dbh,
                conv_w=dcw[:CONV_WIDTH], conv_b=dcb, conv_ln_g=dcg, conv_ln_b=dcbe,
                gate_up=dgup[:GLA_RANK], gate_bias=dgb, gla_norm_g=dgn, ln1_g=dg1, ln1_b=db1, ln2_g=dg2, ln2_b=db2)


def _exchange(arrays, scatter, name):
    na = len(arrays)
    npeer = N_DEV - 1

    def body(*refs):
        srcs = refs[:na]
        outs = refs[na:2 * na]
        send_sems, recv_sems, local_sems = refs[2 * na:]
        xi, yi, ci = (lax.axis_index(a) for a in MESH_AXES)
        me = 4 * xi + 2 * yi + ci
        copies = []
        for a in range(na):
            own = srcs[a].at[me] if scatter[a] else srcs[a]
            cp = pltpu.make_async_copy(own, outs[a].at[me], local_sems.at[a])
            cp.start()
            copies.append(cp)
        remote = []
        for k in range(1, N_DEV):
            px, py, pc = xi ^ (k >> 2), yi ^ ((k >> 1) & 1), ci ^ (k & 1)
            peer = 4 * px + 2 * py + pc
            for a in range(na):
                src = srcs[a].at[peer] if scatter[a] else srcs[a]
                cp = pltpu.make_async_remote_copy(
                    src_ref=src, dst_ref=outs[a].at[me],
                    send_sem=send_sems.at[a * npeer + k - 1], recv_sem=recv_sems.at[a * npeer + k - 1],
                    device_id=(px, py, pc), device_id_type=pl.DeviceIdType.MESH)
                cp.start()
                remote.append(cp)
        for cp in remote:
            cp.wait()
        for cp in copies:
            cp.wait()

    out_shape = [_sds(a.shape if scatter[i] else (N_DEV,) + a.shape, a.dtype) for i, a in enumerate(arrays)]
    anyspec = pl.BlockSpec(memory_space=pl.ANY)
    return pl.pallas_call(
        body, name=name,
        in_specs=[anyspec] * na, out_specs=[anyspec] * na, out_shape=out_shape,
        scratch_shapes=[pltpu.SemaphoreType.DMA((na * npeer,)), pltpu.SemaphoreType.DMA((na * npeer,)),
                        pltpu.SemaphoreType.DMA((na,))],
    )(*arrays)


def _peers(xi, yi, ci):
    for k in range(1, N_DEV):
        px, py, pc = xi ^ (k >> 2), yi ^ ((k >> 1) & 1), ci ^ (k & 1)
        yield (px, py, pc), 4 * px + 2 * py + pc


def _sc_exchange(arrays, scatter, name, collective_id, after=None):
    na = len(arrays)
    npeer = N_DEV - 1
    ndep = 0 if after is None else 1

    def body(*refs):
        srcs = refs[:na]
        outs = refs[na + ndep:2 * na + ndep]
        send_sems, recv_sems, own_sems = refs[2 * na + ndep:]
        xi, yi, ci = (lax.axis_index(a) for a in MESH_AXES)
        me = 4 * xi + 2 * yi + ci
        barrier = pltpu.get_barrier_semaphore()
        for pos, _ in _peers(xi, yi, ci):
            pl.semaphore_signal(barrier, inc=1, device_id=pos, device_id_type=pl.DeviceIdType.MESH)
        pl.semaphore_wait(barrier, npeer)
        own = [pltpu.make_async_copy(srcs[a].at[me] if scatter[a] else srcs[a], outs[a].at[me], own_sems.at[a])
               for a in range(na)]
        for cp in own:
            cp.start()
        remote = []
        for a in range(na):
            for k, (pos, peer) in enumerate(_peers(xi, yi, ci)):
                cp = pltpu.make_async_remote_copy(
                    src_ref=srcs[a].at[peer] if scatter[a] else srcs[a], dst_ref=outs[a].at[me],
                    send_sem=send_sems.at[a * npeer + k], recv_sem=recv_sems.at[a * npeer + k],
                    device_id=pos, device_id_type=pl.DeviceIdType.MESH)
                cp.start()
                remote.append(cp)
        for cp in own:
            cp.wait()
        for cp in remote:
            cp.wait()

    out_type = [_sds(a.shape if scatter[i] else (N_DEV,) + a.shape, a.dtype) for i, a in enumerate(arrays)]
    sent = sum(a.size * a.dtype.itemsize // (N_DEV if scatter[i] else 1) for i, a in enumerate(arrays))
    return pl.kernel(
        body, out_type=out_type, mesh=plsc.ScalarSubcoreMesh(axis_name="seq", num_cores=1), name=name,
        scratch_types=[pltpu.SemaphoreType.DMA((na * npeer,)), pltpu.SemaphoreType.DMA((na * npeer,)),
                       pltpu.SemaphoreType.DMA((na,))],
        compiler_params=pltpu.CompilerParams(collective_id=collective_id),
        cost_estimate=pl.CostEstimate(flops=0, transcendentals=0, bytes_accessed=2 * N_DEV * sent,
                                      remote_bytes_transferred=npeer * sent),
    )(*arrays, *([] if after is None else [after]))


def _sc_gather(arrays, name, collective_id, after=None):
    na = len(arrays)
    ndep = 0 if after is None else 1
    npair = N_DEV - 1

    def body(*refs):
        srcs = refs[:na]
        outs = refs[na + ndep:2 * na + ndep]
        send_sems, recv_sems, own_sems = refs[2 * na + ndep:]
        xi, yi, ci = (lax.axis_index(a) for a in MESH_AXES)
        me = 4 * xi + 2 * yi + ci
        sibling = (xi, yi, 1 - ci)
        chips = [(1 - xi, yi), (xi, 1 - yi), (1 - xi, 1 - yi)]
        barrier = pltpu.get_barrier_semaphore()
        for pos, _ in _peers(xi, yi, ci):
            pl.semaphore_signal(barrier, inc=1, device_id=pos, device_id_type=pl.DeviceIdType.MESH)
        pl.semaphore_wait(barrier, npair)

        def copy(a, k, src, slot, to):
            return pltpu.make_async_remote_copy(
                src_ref=src, dst_ref=outs[a].at[slot], send_sem=send_sems.at[a * npair + k],
                recv_sem=recv_sems.at[a * npair + k], device_id=to, device_id_type=pl.DeviceIdType.MESH)

        own = [pltpu.make_async_copy(srcs[a], outs[a].at[me], own_sems.at[a]) for a in range(na)]
        for cp in own:
            cp.start()
        sent = []
        for a in range(na):
            sent.append(copy(a, 0, srcs[a], me, sibling))
            sent += [copy(a, 1 + j, srcs[a], me, (*chip, ci)) for j, chip in enumerate(chips)]
        for cp in sent:
            cp.start()
        for j, (cx, cy) in enumerate(chips):
            slot = 4 * cx + 2 * cy + ci
            for a in range(na):
                copy(a, 1 + j, srcs[a], slot, sibling).wait_recv()
                cp = copy(a, 4 + j, outs[a].at[slot], slot, sibling)
                cp.start()
                sent.append(cp)
        for a in range(na):
            copy(a, 0, srcs[a], me, sibling).wait_recv()
            for j in range(len(chips)):
                copy(a, 4 + j, srcs[a], me, sibling).wait_recv()
        for cp in sent:
            cp.wait_send()
        for cp in own:
            cp.wait()

    out_type = [_sds((N_DEV,) + a.shape, a.dtype) for a in arrays]
    sent_bytes = sum(a.size * a.dtype.itemsize for a in arrays)
    return pl.kernel(
        body, out_type=out_type, mesh=plsc.ScalarSubcoreMesh(axis_name="seq", num_cores=1), name=name,
        scratch_types=[pltpu.SemaphoreType.DMA((na * npair,)), pltpu.SemaphoreType.DMA((na * npair,)),
                       pltpu.SemaphoreType.DMA((na,))],
        compiler_params=pltpu.CompilerParams(collective_id=collective_id),
        cost_estimate=pl.CostEstimate(flops=0, transcendentals=0, bytes_accessed=2 * N_DEV * sent_bytes,
                                      remote_bytes_transferred=npair * sent_bytes),
    )(*arrays, *([] if after is None else [after]))


def _adamw(w, g, m, v):
    m = ADAM_B1 * m + (1.0 - ADAM_B1) * g
    v = ADAM_B2 * v + (1.0 - ADAM_B2) * jnp.square(g)
    m_hat = m / (1.0 - ADAM_B1 ** ADAM_STEP)
    v_hat = v / (1.0 - ADAM_B2 ** ADAM_STEP)
    delta = -ADAM_LR * (m_hat / (jnp.sqrt(v_hat) + ADAM_EPS) + ADAM_WD * w)
    return delta, m, v


def _sum_devices(ref):
    g = ref[0].astype(F32)
    for k in range(1, N_DEV):
        g = g + ref[k].astype(F32)
    return g


def _update_big(parts, w, m, v, name):
    rows, cols = w.shape[0], w.shape[-1]
    mid = (None,) * (w.ndim - 2)
    mid0 = (0,) * (w.ndim - 2)

    def body(p_ref, w_ref, m_ref, v_ref, g_ref, d_ref, nm_ref, nv_ref):
        g = _sum_devices(p_ref)
        g_ref[...] = g
        d_ref[...], nm_ref[...], nv_ref[...] = _adamw(w_ref[...], g, m_ref[...], v_ref[...])

    if rows % 16 == 0:
        tr = _pick_tile(rows, (128, 64, 16))
        steps, blk = rows // tr, pl.BlockSpec((tr,) + mid + (cols,), lambda i: (i,) + mid0 + (0,))
        pblk = pl.BlockSpec((N_DEV, tr, cols), lambda i: (0, i, 0))
    else:
        tcol = 2 * LANES
        steps, blk = cols // tcol, pl.BlockSpec((rows,) + mid + (tcol,), lambda i: (0,) + mid0 + (i,))
        pblk = pl.BlockSpec((N_DEV, rows, tcol), lambda i: (0, 0, i))
    return pl.pallas_call(
        body, name=name, grid=(steps,),
        in_specs=[pblk, blk, blk, blk],
        out_specs=[blk] * 4, out_shape=[_sds(w.shape, F32)] * 4,
        compiler_params=_params(("parallel",)),
    )(parts, w, m, v)


_VEC_ORDER = ("ln_in_g", "ln_in_b", "conv_b", "conv_ln_g", "conv_ln_b", "gate_bias", "gla_norm_g",
              "ln1_g", "ln1_b", "ln2_g", "ln2_b")
_SHARDED_SMALL = (("meta_tokens", 0, N_META, LANES), ("conv_w", N_META, CONV_WIDTH, None), ("gate_up", N_META + 32, GLA_RANK, None))


def _update_small(parts_sh, parts_vec, wmv):
    names = [s[0] for s in _SHARDED_SMALL] + list(_VEC_ORDER)
    flat = [a for nme in names for a in wmv[nme]]
    nv = len(_VEC_ORDER)

    def body(*refs):
        sh_ref, vec_ref = refs[0], refs[1]
        ins = refs[2:2 + len(flat)]
        outs = refs[2 + len(flat):2 + len(flat) + 4 * len(names)]
        loss_ref = refs[2 + len(flat) + 4 * len(names)]
        gsh_ref, gvec_ref = refs[-2:]
        gsh_ref[...] = _sum_devices(sh_ref)
        gvec_ref[...] = _sum_devices(vec_ref)
        loss_ref[...] = gvec_ref[nv:nv + 1, :]
        for idx, nme in enumerate(names):
            w_ref, m_ref, v_ref = ins[3 * idx:3 * idx + 3]
            rows, cols = w_ref.shape[0], w_ref.shape[-1]
            at = (slice(None),) + (0,) * (len(w_ref.shape) - 2) + (slice(None),)
            if idx < len(_SHARDED_SMALL):
                r0 = _SHARDED_SMALL[idx][1]
                g = gsh_ref[r0:r0 + rows, 0:cols]
            else:
                j = idx - len(_SHARDED_SMALL)
                g = gvec_ref[j:j + 1, 0:cols]
            o = outs[4 * idx:4 * idx + 4]
            o[0][at] = g
            o[1][at], o[2][at], o[3][at] = _adamw(w_ref[at], g, m_ref[at], v_ref[at])

    out_shape = [_sds(wmv[nme][0].shape, F32) for nme in names for _ in range(4)] + [_sds((1, parts_vec.shape[2]), F32)]
    vmem = pl.BlockSpec(memory_space=pltpu.VMEM)
    res = pl.pallas_call(
        body, name="update_small", out_shape=out_shape,
        in_specs=[vmem] * (2 + len(flat)), out_specs=[vmem] * len(out_shape),
        scratch_shapes=[pltpu.VMEM(parts_sh.shape[1:], F32), pltpu.VMEM(parts_vec.shape[1:], F32)],
    )(parts_sh, parts_vec, *flat)
    return {nme: res[4 * i:4 * i + 4] for i, nme in enumerate(names)}, res[-1][0, 0]


_WEIGHTS = ("meta_tokens", "ln_in_g", "ln_in_b", "w_in", "conv_w", "conv_b", "conv_ln_g", "conv_ln_b", "gate_up",
            "gate_bias", "gla_norm_g", "w_out", "ln1_g", "ln1_b", "w_ff1", "w_ff2", "ln2_g", "ln2_b")


def kernel(x, meta_tokens, ln_in_g, ln_in_b, w_in, conv_w, conv_b, conv_ln_g, conv_ln_b, gate_up, gate_bias, gla_norm_g, w_out, ln1_g, ln1_b, w_ff1, w_ff2, ln2_g, ln2_b, loss_target, m_meta_tokens, m_ln_in_g, m_ln_in_b, m_w_in, m_conv_w, m_conv_b, m_conv_ln_g, m_conv_ln_b, m_gate_up, m_gate_bias, m_gla_norm_g, m_w_out, m_ln1_g, m_ln1_b, m_w_ff1, m_w_ff2, m_ln2_g, m_ln2_b, v_meta_tokens, v_ln_in_g, v_ln_in_b, v_w_in, v_conv_w, v_conv_b, v_conv_ln_g, v_conv_ln_b, v_gate_up, v_gate_bias, v_gla_norm_g, v_w_out, v_ln1_g, v_ln1_b, v_w_ff1, v_w_ff2, v_ln2_g, v_ln2_b):
    w = dict(meta_tokens=meta_tokens, ln_in_g=ln_in_g, ln_in_b=ln_in_b, w_in=w_in, conv_w=conv_w, conv_b=conv_b,
             conv_ln_g=conv_ln_g, conv_ln_b=conv_ln_b, gate_up=gate_up, gate_bias=gate_bias, gla_norm_g=gla_norm_g,
             w_out=w_out, ln1_g=ln1_g, ln1_b=ln1_b, w_ff1=w_ff1, w_ff2=w_ff2, ln2_g=ln2_g, ln2_b=ln2_b)
    mom = dict(meta_tokens=m_meta_tokens, ln_in_g=m_ln_in_g, ln_in_b=m_ln_in_b, w_in=m_w_in, conv_w=m_conv_w,
               conv_b=m_conv_b, conv_ln_g=m_conv_ln_g, conv_ln_b=m_conv_ln_b, gate_up=m_gate_up, gate_bias=m_gate_bias,
               gla_norm_g=m_gla_norm_g, w_out=m_w_out, ln1_g=m_ln1_g, ln1_b=m_ln1_b, w_ff1=m_w_ff1, w_ff2=m_w_ff2,
               ln2_g=m_ln2_g, ln2_b=m_ln2_b)
    var = dict(meta_tokens=v_meta_tokens, ln_in_g=v_ln_in_g, ln_in_b=v_ln_in_b, w_in=v_w_in, conv_w=v_conv_w,
               conv_b=v_conv_b, conv_ln_g=v_conv_ln_g, conv_ln_b=v_conv_ln_b, gate_up=v_gate_up, gate_bias=v_gate_bias,
               gla_norm_g=v_gla_norm_g, w_out=v_w_out, ln1_g=v_ln1_g, ln1_b=v_ln1_b, w_ff1=v_w_ff1, w_ff2=v_w_ff2,
               ln2_g=v_ln2_g, ln2_b=v_ln2_b)
    shapes = {k: a.shape for k, a in w.items()}

    def two_d(a):
        return a.reshape(1, -1) if a.ndim == 1 else a.reshape(a.shape[-2:])

    w2d = {k: two_d(a) for k, a in w.items()}
    m2d = {k: two_d(a) for k, a in mom.items()}
    v2d = {k: two_d(a) for k, a in var.items()}
    d = x.shape[-1]
    d_in = w2d["w_in"].shape[1] * N_DEV
    d_in_p = -(-d_in // LANES) * LANES

    in_wmv = [jnp.transpose(dct["w_in"], (2, 0, 1)) for dct in (w, mom, var)]
    g_in, g_meta, g_conv, g_gup = _sc_gather(
        [w2d["w_in"].T.astype(BF16), w2d["meta_tokens"], w2d["conv_w"], w2d["gate_up"]], "gather_first", 0)
    g_out, g_ff1, g_ff2 = _sc_gather(
        [w2d["w_out"].astype(BF16), w2d["w_ff1"].astype(BF16), w2d["w_ff2"].astype(BF16)], "gather_late", 1)
    w_in_full = jnp.pad(g_in.reshape(d_in, d), ((0, d_in_p - d_in), (0, 0)))
    meta_full = g_meta.transpose(1, 0, 2).reshape(N_META, d)
    conv_w_full = g_conv.transpose(1, 0, 2).reshape(CONV_WIDTH, -1)
    gate_up_full = g_gup.transpose(1, 0, 2).reshape(GLA_RANK, -1)

    late_weights = (g_out.reshape(-1, d), g_ff1, g_ff2.reshape(-1, d))
    pushed = {}

    def push(tag, grads):
        if tag == "ff":
            pushed["ff1"], pushed["ff2"] = _sc_exchange(list(grads), [True, True], "scatter_ff", 2)
        elif tag == "out":
            pushed["p_out"] = grads[0].reshape(N_DEV, -1, d)
        else:
            p_in = grads[0][:d_in].reshape(N_DEV, d_in // N_DEV, d)
            pushed["in"], pushed["out"] = _sc_exchange([p_in, pushed["p_out"]], [True, True], "scatter_rest", 3,
                                                       after=pushed["ff1"])

    res = _local_step(x, loss_target, meta_full, w2d["ln_in_g"], w2d["ln_in_b"], w_in_full, conv_w_full, w2d["conv_b"],
                      w2d["conv_ln_g"], w2d["conv_ln_b"], gate_up_full, w2d["gate_bias"], w2d["gla_norm_g"], late_weights,
                      w2d["ln1_g"], w2d["ln1_b"], w2d["ln2_g"], w2d["ln2_b"], push)

    dc = res["conv_w"].shape[1]
    hk = res["gate_up"].shape[1]
    sh_meta = res["meta_tokens"].reshape(N_META, N_DEV, LANES).transpose(1, 0, 2)
    sh_conv = jnp.pad(res["conv_w"].reshape(CONV_WIDTH, N_DEV, dc // N_DEV).transpose(1, 0, 2),
                      ((0, 0), (0, 32 - CONV_WIDTH), (0, LANES - dc // N_DEV)))
    sh_gup = jnp.pad(res["gate_up"].reshape(GLA_RANK, N_DEV, hk // N_DEV).transpose(1, 0, 2),
                     ((0, 0), (0, 0), (0, LANES - hk // N_DEV)))
    p_sh = jnp.concatenate([sh_meta, sh_conv, sh_gup], axis=1)
    p_vec = jnp.concatenate([jnp.pad(res[k], ((0, 0), (0, d - res[k].shape[1]))) for k in _VEC_ORDER]
                            + [jnp.full((1, d), res["loss"], F32), jnp.zeros((15 - len(_VEC_ORDER), d), F32)], axis=0)

    r_sh, r_vec = _exchange([p_sh, p_vec], [True, False], "scatter_small")
    r_ff1, r_ff2, r_out, r_in = pushed["ff1"], pushed["ff2"], pushed["out"], pushed["in"]

    upd = {}
    upd["w_in"] = [jnp.transpose(a, (1, 2, 0)) for a in _update_big(r_in, *in_wmv, "update_w_in")]
    upd["w_out"] = _update_big(r_out, w2d["w_out"], m2d["w_out"], v2d["w_out"], "update_w_out")
    upd["w_ff1"] = _update_big(r_ff1, w2d["w_ff1"], m2d["w_ff1"], v2d["w_ff1"], "update_w_ff1")
    upd["w_ff2"] = _update_big(r_ff2, w2d["w_ff2"], m2d["w_ff2"], v2d["w_ff2"], "update_w_ff2")
    small = [s[0] for s in _SHARDED_SMALL] + list(_VEC_ORDER)
    wmv = {k: (w2d[k], m2d[k], v2d[k]) for k in small}
    wmv["conv_w"] = tuple(jnp.transpose(dct["conv_w"], (1, 0, 2)) for dct in (w, mom, var))
    upd_small, loss = _update_small(r_sh, r_vec, wmv)
    upd.update(upd_small)

    outs = [loss, res["grad_x"]]
    for j in range(4):
        outs += [upd[k][j].reshape(shapes[k]) for k in _WEIGHTS]
    return tuple(outs)
```

```python
import jax
import jax.numpy as jnp
from jax import lax
from jax.experimental import pallas as pl
from jax.experimental.pallas import tpu as pltpu
from jax.experimental.pallas import tpu_sc as plsc

F32 = jnp.float32
BF16 = jnp.bfloat16

N_META = 16
CHUNK = 64
PAD_FRONT = (-N_META) % CHUNK
X_OFF = PAD_FRONT + N_META
CONV_WIDTH = 31
CONV_HALO = 32
CONV_SUB = 64
CONV_WIN = CONV_SUB + CONV_HALO
GLA_HEADS = 4
GLA_DK = 64
GLA_DV = 128
GLA_RANK = 16
GLA_TAU = 16.0
QK_SCALE = GLA_DK ** -0.5
LN_EPS = 1e-5
ALPHA = 2.0 ** 0.25
LANES = 128
N_DEV = 8
ADAM_LR = 0.001
ADAM_B1 = 0.9
ADAM_B2 = 0.999
ADAM_EPS = 1e-08
ADAM_WD = 0.01
ADAM_STEP = 10
VMEM_LIMIT = 56 * 1024 * 1024
MESH_AXES = ("x", "y", "c")
U_QK, U_V, U_R, U_GD = 2, 3, 4, 20
QK_WIDTH = 2 * GLA_HEADS * GLA_DK
GUP_WIDTH = GLA_HEADS * GLA_DK
DMI_GLA = 1


def _sds(shape, dtype):
    return jax.ShapeDtypeStruct(shape, dtype)


def _mm(a, b):
    return jnp.dot(a, b, preferred_element_type=F32)


def _mm_nt(a, b):
    return lax.dot_general(a, b, (((1,), (1,)), ((), ())), preferred_element_type=F32)


def _mm_tn(a, b):
    return lax.dot_general(a, b, (((0,), (0,)), ((), ())), preferred_element_type=F32)


def _sigmoid(x):
    return 1.0 / (1.0 + jnp.exp(-x))


def _log_sigmoid(z):
    return jnp.minimum(z, 0.0) - jnp.log(1.0 + jnp.exp(-jnp.abs(z)))


def _ln(x):
    mu = jnp.mean(x, axis=-1, keepdims=True)
    xc = x - mu
    var = jnp.mean(xc * xc, axis=-1, keepdims=True)
    rstd = lax.rsqrt(var + LN_EPS)
    return xc * rstd, rstd


def _ln_bwd(dyg, xhat, rstd):
    m1 = jnp.mean(dyg, axis=-1, keepdims=True)
    m2 = jnp.mean(dyg * xhat, axis=-1, keepdims=True)
    return rstd * (dyg - m1 - xhat * m2)


def _rowsum(x):
    return jnp.sum(x, axis=0, keepdims=True)


def _row_in_seq(i, tm, tp):
    base = lax.rem(i * tm, tp)
    return base + lax.broadcasted_iota(jnp.int32, (tm, 1), 0)


def _split3(x):
    hi = x.astype(BF16)
    r1 = x - hi.astype(F32)
    mid = r1.astype(BF16)
    lo = (r1 - mid.astype(F32)).astype(BF16)
    return hi, mid, lo


def _params(sem):
    return pltpu.CompilerParams(dimension_semantics=sem, vmem_limit_bytes=VMEM_LIMIT)


def _pick_tile(n, prefs):
    for t in prefs:
        if n % t == 0:
            return t
    raise ValueError(f"no tile for {n}")


ROW_BLOCKS = 2


def _row_blocks(tm, matmuls, finish):
    blocks = [slice(k * tm // ROW_BLOCKS, (k + 1) * tm // ROW_BLOCKS) for k in range(ROW_BLOCKS)]
    acc = matmuls(blocks[0])
    for prev, rows in zip(blocks, blocks[1:]):
        nxt = matmuls(rows)
        finish(prev, acc)
        acc = nxt
    finish(blocks[-1], acc)


def _x_tile_row(tp, seq, tx):
    tps = seq // tx
    return lambda i: pl.multiple_of((i // tps) * tp + X_OFF + (i % tps) * tx, CHUNK)


def _ln_in_x(x2, g, b, tp, seq, tx):
    rx, d = x2.shape
    r = rx // seq * tp
    row = _x_tile_row(tp, seq, tx)

    def body(x_ref, g_ref, b_ref, s0_ref, sb_ref):
        xhat, _ = _ln(x_ref[...])
        s = xhat * g_ref[...] + b_ref[...]
        s0_ref[...] = s
        sb_ref[...] = s.astype(BF16)

    out = pl.BlockSpec((pl.Element(tx), pl.Element(d)), lambda i: (row(i), 0))
    return pl.pallas_call(
        body, name="ln_in_x", grid=(rx // tx,),
        in_specs=[pl.BlockSpec((tx, d), lambda i: (i, 0)), pl.BlockSpec((1, d), lambda i: (0, 0)),
                  pl.BlockSpec((1, d), lambda i: (0, 0))],
        out_specs=[out, out],
        out_shape=[_sds((r, d), F32), _sds((r, d), BF16)],
        compiler_params=_params(("parallel",)),
    )(x2, g, b)


def _ln_in_head(head, g, b, s0, s0b, tp):
    r, d = s0.shape
    nb = tp // X_OFF

    def body(h_ref, g_ref, b_ref, s0_in, sb_in, s0_ref, sb_ref):
        xhat, _ = _ln(h_ref[...])
        real = lax.broadcasted_iota(jnp.int32, (X_OFF, 1), 0) >= PAD_FRONT
        s = jnp.where(real, xhat * g_ref[...] + b_ref[...], 0.0)
        s0_ref[...] = s
        sb_ref[...] = s.astype(BF16)

    anyspec = pl.BlockSpec(memory_space=pl.ANY)
    out = pl.BlockSpec((X_OFF, d), lambda i: (i * nb, 0))
    return pl.pallas_call(
        body, name="ln_in_head", grid=(r // tp,),
        in_specs=[pl.BlockSpec((X_OFF, d), lambda i: (0, 0)), pl.BlockSpec((1, d), lambda i: (0, 0)),
                  pl.BlockSpec((1, d), lambda i: (0, 0)), anyspec, anyspec],
        out_specs=[out, out],
        out_shape=[_sds((r, d), F32), _sds((r, d), BF16)],
        input_output_aliases={3: 0, 4: 1},
        compiler_params=_params(("parallel",)),
    )(head, g, b, s0, s0b)


def _inproj_fwd(s0b, w_in, tm):
    r, d = s0b.shape
    n = w_in.shape[0]

    def body(s_ref, w_ref, u_ref):
        u_ref[...] = _mm_nt(s_ref[...], w_ref[...])

    return pl.pallas_call(
        body, name="inproj_fwd", grid=(r // tm,),
        in_specs=[pl.BlockSpec((tm, d), lambda i: (i, 0)), pl.BlockSpec((n, d), lambda i: (0, 0))],
        out_specs=pl.BlockSpec((tm, n), lambda i: (i, 0)),
        out_shape=_sds((r, n), F32),
        compiler_params=_params(("parallel",)),
    )(s0b, w_in)


def _conv_taps(win, coef, lo):
    acc = None
    for rho in range(8):
        offs = [o for o in range(lo, lo + CONV_WIDTH) if o % 8 == rho]
        if not offs:
            continue
        rolled = win if rho == 0 else pltpu.roll(win, CONV_WIN - rho, 0)
        for o in offs:
            m8 = o - rho
            term = rolled[m8:m8 + CONV_SUB, :] * coef(o)
            acc = term if acc is None else acc + term
    return acc


def _conv_fwd(u, w32, cb, cg, cbe, tp, tc, dc):
    r = u.shape[0]
    hb = tc // CONV_HALO

    def body(a_ref, g_ref, ah_ref, gh_ref, w_ref, cb_ref, cg_ref, cbe_ref, c_ref, co_ref, hs_ref):
        t = pl.program_id(0)
        first = lax.rem(t * tc, tp) == 0
        hh = ah_ref[...] * _sigmoid(gh_ref[...])
        hs_ref[0:CONV_HALO, :] = jnp.where(first, 0.0, hh)
        hs_ref[CONV_HALO:CONV_HALO + tc, :] = a_ref[...] * _sigmoid(g_ref[...])

        def sub(k, carry):
            r0 = pl.multiple_of(k * CONV_SUB, CONV_SUB)
            win = hs_ref[pl.ds(r0, CONV_WIN), :]
            c = _conv_taps(win, lambda o: w_ref[o - 2:o - 1, :], 2) + cb_ref[...]
            c_ref[pl.ds(r0, CONV_SUB), :] = c
            xhat, _ = _ln(c)
            cn = xhat * cg_ref[...] + cbe_ref[...]
            co_ref[pl.ds(r0, CONV_SUB), :] = (cn * _sigmoid(cn)).astype(BF16)
            return carry

        lax.fori_loop(0, tc // CONV_SUB, sub, 0)

    vec = pl.BlockSpec((1, dc), lambda t: (0, 0))
    return pl.pallas_call(
        body, name="conv_fwd", grid=(r // tc,),
        in_specs=[pl.BlockSpec((tc, dc), lambda t: (t, 0)), pl.BlockSpec((tc, dc), lambda t: (t, 1)),
                  pl.BlockSpec((CONV_HALO, dc), lambda t: (jnp.maximum(t * hb - 1, 0), 0)),
                  pl.BlockSpec((CONV_HALO, dc), lambda t: (jnp.maximum(t * hb - 1, 0), 1)),
                  pl.BlockSpec((32, dc), lambda t: (0, 0)), vec, vec, vec],
        out_specs=[pl.BlockSpec((tc, dc), lambda t: (t, 0)), pl.BlockSpec((tc, dc), lambda t: (t, 0))],
        out_shape=[_sds((r, dc), F32), _sds((r, dc), BF16)],
        scratch_shapes=[pltpu.VMEM((CONV_HALO + tc, dc), F32)],
        compiler_params=_params(("parallel",)),
    )(u, u, u, u, w32, cb, cg, cbe)


def _tri_mm_all(tri, xs):
    parts = [_split3(x) for x in xs]
    acc = [None] * len(xs)
    for t in range(3):
        for j in range(len(xs)):
            term = _mm(tri, parts[j][t])
            acc[j] = term if t == 0 else acc[j] + term
    return acc


def _gla_prep(qk_ref, gd_ref, gup, gb, n0, kc):
    rows = [slice(j * CHUNK, (j + 1) * CHUNK) for j in range(kc)]
    ri = lax.broadcasted_iota(jnp.int32, (CHUNK, CHUNK), 0)
    ci = lax.broadcasted_iota(jnp.int32, (CHUNK, CHUNK), 1)
    low = (ri >= ci).astype(BF16)
    hk = GLA_HEADS * GLA_DK
    gds = [gd_ref[rw, :] for rw in rows]
    zs = [_mm(g.astype(BF16), gup) + gb for g in gds]
    reals = [(n0 + j) * CHUNK + lax.broadcasted_iota(jnp.int32, (CHUNK, 1), 0) >= PAD_FRONT for j in range(kc)]
    lgs = [jnp.where(reals[j], _log_sigmoid(zs[j]) * (1.0 / GLA_TAU), 0.0) for j in range(kc)]
    bs = _tri_mm_all(low, lgs)
    out = []
    for j in range(kc):
        b, bl = bs[j], _rowsum(lgs[j])
        q = qk_ref[rows[j], :hk] * QK_SCALE
        k = qk_ref[rows[j], hk:]
        eb, enb, ebl = jnp.exp(b), jnp.exp(-b), jnp.exp(bl - b)
        out.append(dict(rows=rows[j], gd=gds[j], z=zs[j], real=reals[j], eb=eb, enb=enb, ebl=ebl, gam=jnp.exp(bl),
                        qe=q * eb, ke=k * enb, kd=k * ebl))
    return out, ri, ci


def _gla_heads(p, v_ref):
    ops = []
    for h in range(GLA_HEADS):
        hp, h2 = divmod(h, 2)
        ls = slice(hp * LANES, (hp + 1) * LANES)
        m = _head_mask(h2)
        ops.append(dict(ls=ls, m=m, vs=slice(h * GLA_DV, (h + 1) * GLA_DV),
                        qe=jnp.where(m, p["qe"][:, ls], 0.0).astype(BF16),
                        kd=jnp.where(m, p["kd"][:, ls], 0.0).astype(BF16),
                        ke=p["ke"][:, ls].astype(BF16),
                        v=v_ref[p["rows"], h * GLA_DV:(h + 1) * GLA_DV].astype(BF16)))
    return ops


def _head_mask(h2):
    lane = lax.broadcasted_iota(jnp.int32, (1, LANES), 1)
    return (lane < GLA_DK) if h2 == 0 else (lane >= GLA_DK)


def _gla_fwd(u, gup, gb, gn, bsz, nc, kc):
    r = u.shape[0]
    hv = GLA_HEADS * GLA_DV
    ns = nc // kc

    def body(qk_ref, v_ref, r_ref, gd_ref, gup_ref, gb_ref, gn_ref, go_ref, sta_ref, st_ref):
        t = pl.program_id(1)

        @pl.when(t == 0)
        def _():
            st_ref[...] = jnp.zeros_like(st_ref)

        ps, ri, ci = _gla_prep(qk_ref, gd_ref, gup_ref[...], gb_ref[...], t * kc, kc)
        tril = ri >= ci
        items = [(j, h) for j in range(kc) for h in range(GLA_HEADS)]
        ops = [_gla_heads(p, v_ref) for p in ps]
        a = {jh: jnp.where(tril, _mm_nt(ops[jh[0]][jh[1]]["qe"], ops[jh[0]][jh[1]]["ke"]), 0.0).astype(BF16) for jh in items}
        oi = {jh: _mm(a[jh], ops[jh[0]][jh[1]]["v"]) for jh in items}
        inc = {jh: _mm_tn(ops[jh[0]][jh[1]]["v"], ops[jh[0]][jh[1]]["kd"]) for jh in items}
        sts = [st_ref[h] for h in range(GLA_HEADS)]
        for j, h in items:
            op, p = ops[j][h], ps[j]
            st = sts[h]
            sta_ref[j, h] = st
            o = oi[j, h] + _mm_nt(op["qe"], st.astype(BF16))
            sts[h] = st * p["gam"][:, op["ls"]] + inc[j, h]
            rs = lax.rsqrt(jnp.mean(o * o, axis=-1, keepdims=True) + LN_EPS)
            rr = r_ref[p["rows"], op["vs"]]
            go_ref[p["rows"], op["vs"]] = (o * rs * gn_ref[...] * (rr * _sigmoid(rr))).astype(BF16)
        for h in range(GLA_HEADS):
            st_ref[h] = sts[h]

    rowblk = lambda col: (lambda b, t: (b * ns + t, col))
    const = lambda b, t: (0, 0)
    return pl.pallas_call(
        body, name="gla_fwd", grid=(bsz, ns),
        in_specs=[pl.BlockSpec((kc * CHUNK, QK_WIDTH), rowblk(U_QK)), pl.BlockSpec((kc * CHUNK, hv), rowblk(U_V)),
                  pl.BlockSpec((kc * CHUNK, hv), rowblk(U_R)), pl.BlockSpec((kc * CHUNK, LANES), rowblk(U_GD)),
                  pl.BlockSpec((LANES, GUP_WIDTH), const), pl.BlockSpec((1, GUP_WIDTH), const), pl.BlockSpec((1, GLA_DV), const)],
        out_specs=[pl.BlockSpec((kc * CHUNK, hv), rowblk(0)),
                   pl.BlockSpec((kc, GLA_HEADS, LANES, LANES), lambda b, t: (b * ns + t, 0, 0, 0))],
        out_shape=[_sds((r, hv), BF16), _sds((bsz * nc, GLA_HEADS, LANES, LANES), F32)],
        scratch_shapes=[pltpu.VMEM((GLA_HEADS, LANES, LANES), F32)],
        compiler_params=_params(("parallel", "arbitrary")),
    )(u, u, u, u, gup, gb, gn)


def _outproj_fwd(s0, co, go, w_out, g1, b1, tm):
    r, d = s0.shape
    dc = co.shape[1]

    def body(s0_ref, co_ref, go_ref, w_ref, g_ref, b_ref, p1_ref, s1_ref, s1b_ref):
        nb = 4 if tm % 64 == 0 else 1
        blocks = [slice(k * (tm // nb), (k + 1) * (tm // nb)) for k in range(nb)]
        mixes = [_mm(co_ref[rows, :], w_ref[0:dc, :]) + _mm(go_ref[rows, :], w_ref[dc:2 * dc, :]) for rows in blocks]
        for rows, mix in zip(blocks, mixes):
            p1 = ALPHA * s0_ref[rows, :] + mix
            p1_ref[rows, :] = p1
            xhat, _ = _ln(p1)
            s1 = xhat * g_ref[...] + b_ref[...]
            s1_ref[rows, :] = s1
            s1b_ref[rows, :] = s1.astype(BF16)

    row = lambda w: pl.BlockSpec((tm, w), lambda i: (i, 0))
    vec = pl.BlockSpec((1, d), lambda i: (0, 0))
    return pl.pallas_call(
        body, name="outproj_fwd", grid=(r // tm,),
        in_specs=[row(d), row(dc), row(dc), pl.BlockSpec((2 * dc, d), lambda i: (0, 0)), vec, vec],
        out_specs=[row(d), row(d), row(d)],
        out_shape=[_sds((r, d), F32), _sds((r, d), F32), _sds((r, d), BF16)],
        compiler_params=_params(("parallel",)),
    )(s0, co, go, w_out, g1, b1)


def _mlp_fwd(s1, s1b, w1g, w2, g2, b2, tgt, tp, tm, ns):
    r, d = s1.shape
    nh, _, th = w1g.shape
    nj = nh // ns

    def body(s1_ref, sb_ref, w1_ref, w2_ref, g_ref, b_ref, t_ref, hm_ref, dp2_ref, dpb_ref, loss_ref, dg_ref, db_ref, acc_ref):
        i = pl.program_id(0)
        j = pl.program_id(1)

        @pl.when(jnp.logical_and(i == 0, j == 0))
        def _():
            loss_ref[...] = jnp.zeros_like(loss_ref)
            dg_ref[...] = jnp.zeros_like(dg_ref)
            db_ref[...] = jnp.zeros_like(db_ref)

        @pl.when(j == 0)
        def _():
            acc_ref[...] = jnp.zeros_like(acc_ref)

        def mlp_rows(rows):
            hs = [_mm(sb_ref[rows, :], w1_ref[s]) for s in range(ns)]
            acc = acc_ref[rows, :]
            for s in range(ns):
                hm_ref[rows, s * th:(s + 1) * th] = hs[s].astype(BF16)
                act = jnp.square(jnp.maximum(hs[s], 0.0))
                acc = acc + _mm(act.astype(BF16), w2_ref[s * th:(s + 1) * th, :])
            return acc

        @pl.when(j < nj - 1)
        def _():
            acc_ref[...] = mlp_rows(slice(None))

        @pl.when(j == nj - 1)
        def _():
            isx = _row_in_seq(i, tm, tp) >= X_OFF
            tg = t_ref[...]
            tg = jnp.where(i == 0, pltpu.roll(tg, X_OFF, 0), tg)

            def finish(rows, acc):
                p2 = ALPHA * s1_ref[rows, :] + acc
                xhat, rstd = _ln(p2)
                s2 = xhat * g_ref[...] + b_ref[...]
                err = jnp.where(isx[rows], s2 - tg[rows], 0.0)
                loss_ref[...] += 0.5 * jnp.sum(jnp.mean(err * err, axis=-1, keepdims=True))
                dy = err * (1.0 / d)
                dg_ref[...] += _rowsum(dy * xhat)
                db_ref[...] += _rowsum(dy)
                dp2 = _ln_bwd(dy * g_ref[...], xhat, rstd)
                dp2_ref[rows, :] = dp2
                dpb_ref[rows, :] = dp2.astype(BF16)

            _row_blocks(tm, mlp_rows, finish)

    row = pl.BlockSpec((tm, d), lambda i, j: (i, 0))
    vec = pl.BlockSpec((1, d), lambda i, j: (0, 0))
    tgt_row = pl.BlockSpec((pl.Element(tm), pl.Element(d)),
                           lambda i, j: (pl.multiple_of(jnp.maximum(i * tm - X_OFF * ((i * tm) // tp + 1), 0), CHUNK), 0))
    return pl.pallas_call(
        body, name="mlp_fwd", grid=(r // tm, nj),
        in_specs=[row, row, pl.BlockSpec((ns, d, th), lambda i, j: (j, 0, 0)), pl.BlockSpec((ns * th, d), lambda i, j: (j, 0)),
                  vec, vec, tgt_row],
        out_specs=[pl.BlockSpec((tm, ns * th), lambda i, j: (i, j)), row, row,
                   pl.BlockSpec((8, LANES), lambda i, j: (0, 0)), vec, vec],
        out_shape=[_sds((r, nh * th), BF16), _sds((r, d), F32), _sds((r, d), BF16), _sds((8, LANES), F32),
                   _sds((1, d), F32), _sds((1, d), F32)],
        scratch_shapes=[pltpu.VMEM((tm, d), F32)],
        compiler_params=_params(("arbitrary", "arbitrary")),
    )(s1, s1b, w1g, w2, g2, b2, tgt)


def _mlp_bwd_act(dp2, dpb, hm, w1g, w2, p1, g1, tm, ns):
    r, d = dp2.shape
    nh, _, th = w1g.shape
    nj = nh // ns

    def body(dp2_ref, dpb_ref, hm_ref, w1_ref, w2_ref, p1_ref, g_ref, dh_ref, dp1_ref, dg_ref, db_ref, acc_ref):
        i = pl.program_id(0)
        j = pl.program_id(1)

        @pl.when(jnp.logical_and(i == 0, j == 0))
        def _():
            dg_ref[...] = jnp.zeros_like(dg_ref)
            db_ref[...] = jnp.zeros_like(db_ref)

        @pl.when(j == 0)
        def _():
            acc_ref[...] = jnp.zeros_like(acc_ref)

        def mlp_rows(rows):
            dacts = [_mm_nt(dpb_ref[rows, :], w2_ref[s * th:(s + 1) * th, :]) for s in range(ns)]
            acc = acc_ref[rows, :]
            for s in range(ns):
                cols = slice(s * th, (s + 1) * th)
                dh = (dacts[s] * (2.0 * jnp.maximum(hm_ref[rows, cols].astype(F32), 0.0))).astype(BF16)
                dh_ref[rows, cols] = dh
                acc = acc + _mm_nt(dh, w1_ref[s])
            return acc

        @pl.when(j < nj - 1)
        def _():
            acc_ref[...] = mlp_rows(slice(None))

        @pl.when(j == nj - 1)
        def _():
            def finish(rows, acc):
                ds1 = ALPHA * dp2_ref[rows, :] + acc
                xhat, rstd = _ln(p1_ref[rows, :])
                dg_ref[...] += _rowsum(ds1 * xhat)
                db_ref[...] += _rowsum(ds1)
                dp1_ref[rows, :] = _ln_bwd(ds1 * g_ref[...], xhat, rstd)

            _row_blocks(tm, mlp_rows, finish)

    row = pl.BlockSpec((tm, d), lambda i, j: (i, 0))
    vec = pl.BlockSpec((1, d), lambda i, j: (0, 0))
    blk = pl.BlockSpec((tm, ns * th), lambda i, j: (i, j))
    return pl.pallas_call(
        body, name="mlp_bwd_act", grid=(r // tm, nj),
        in_specs=[row, row, blk, pl.BlockSpec((ns, d, th), lambda i, j: (j, 0, 0)),
                  pl.BlockSpec((ns * th, d), lambda i, j: (j, 0)), row, vec],
        out_specs=[blk, row, vec, vec],
        out_shape=[_sds((r, nh * th), BF16), _sds((r, d), F32), _sds((1, d), F32), _sds((1, d), F32)],
        scratch_shapes=[pltpu.VMEM((tm, d), F32)],
        compiler_params=_params(("arbitrary", "arbitrary")),
    )(dp2, dpb, hm, w1g, w2, p1, g1)


def _mlp_bwd_w(s1b, hm, dh, dpb, nh, tm, ns):
    r, d = s1b.shape
    th = hm.shape[1] // nh

    def body(s1_ref, hm_ref, dh_ref, dp2_ref, dw1_ref, dw2_ref, a1_ref, a2_ref):
        i = pl.program_id(1)

        @pl.when(i == 0)
        def _():
            a1_ref[...] = jnp.zeros_like(a1_ref)
            a2_ref[...] = jnp.zeros_like(a2_ref)

        for s in range(ns):
            a1_ref[s] += _mm_tn(s1_ref[...], dh_ref[:, s * th:(s + 1) * th])
        for s in range(ns):
            act = jnp.square(jnp.maximum(hm_ref[:, s * th:(s + 1) * th].astype(F32), 0.0)).astype(BF16)
            a2_ref[s] += _mm_tn(act, dp2_ref[...])

        @pl.when(i == pl.num_programs(1) - 1)
        def _():
            dw1_ref[...] = a1_ref[...].astype(BF16)
            dw2_ref[...] = a2_ref[...].astype(BF16)

    row = pl.BlockSpec((tm, d), lambda j, i: (i, 0))
    blk = pl.BlockSpec((tm, ns * th), lambda j, i: (i, j))
    return pl.pallas_call(
        body, name="mlp_bwd_w", grid=(nh // ns, r // tm),
        in_specs=[row, blk, blk, row],
        out_specs=[pl.BlockSpec((ns, d, th), lambda j, i: (j, 0, 0)), pl.BlockSpec((ns, th, d), lambda j, i: (j, 0, 0))],
        out_shape=[_sds((nh, d, th), BF16), _sds((nh, th, d), BF16)],
        scratch_shapes=[pltpu.VMEM((ns, d, th), F32), pltpu.VMEM((ns, th, d), F32)],
        compiler_params=_params(("parallel", "arbitrary")),
    )(s1b, hm, dh, dpb)


def _outproj_bwd(dp1, co, go, w_out, tm):
    r, d = dp1.shape
    dc = co.shape[1]

    def body(dp_ref, co_ref, go_ref, w_ref, dmi_ref, dw_ref, acc_ref):
        i = pl.program_id(0)

        @pl.when(i == 0)
        def _():
            acc_ref[...] = jnp.zeros_like(acc_ref)

        dpb = dp_ref[...].astype(BF16)
        dmi_ref[...] = _mm_nt(dpb, w_ref[...])
        acc_ref[0:dc, :] += _mm_tn(co_ref[...], dpb)
        acc_ref[dc:2 * dc, :] += _mm_tn(go_ref[...], dpb)

        @pl.when(i == pl.num_programs(0) - 1)
        def _():
            dw_ref[...] = acc_ref[...].astype(BF16)

    row = lambda w: pl.BlockSpec((tm, w), lambda i: (i, 0))
    full = pl.BlockSpec((2 * dc, d), lambda i: (0, 0))
    return pl.pallas_call(
        body, name="outproj_bwd", grid=(r // tm,),
        in_specs=[row(d), row(dc), row(dc), full],
        out_specs=[row(2 * dc), full],
        out_shape=[_sds((r, 2 * dc), F32), _sds((2 * dc, d), BF16)],
        scratch_shapes=[pltpu.VMEM((2 * dc, d), F32)],
        compiler_params=_params(("arbitrary",)),
    )(dp1, co, go, w_out)


def _gla_bwd(u, dmi, sta, gup, gb, gn, bsz, nc, kc):
    r = u.shape[0]
    hv = GLA_HEADS * GLA_DV
    hk = GLA_HEADS * GLA_DK
    ns = nc // kc

    def body(qk_ref, v_ref, r_ref, gd_ref, dgo_ref, sta_ref, gup_ref, gb_ref, gn_ref,
             dqk_ref, dv_ref, dr_ref, dgd_ref, dgn_ref, dgb_ref, dgup_ref, dst_ref):
        bi = pl.program_id(0)
        t = pl.program_id(1)

        @pl.when(jnp.logical_and(bi == 0, t == 0))
        def _():
            dgn_ref[...] = jnp.zeros_like(dgn_ref)
            dgb_ref[...] = jnp.zeros_like(dgb_ref)
            dgup_ref[...] = jnp.zeros_like(dgup_ref)

        @pl.when(t == 0)
        def _():
            dst_ref[...] = jnp.zeros_like(dst_ref)

        ps, ri, ci = _gla_prep(qk_ref, gd_ref, gup_ref[...], gb_ref[...], (ns - 1 - t) * kc, kc)
        tril = ri >= ci
        items = [(j, h) for j in reversed(range(kc)) for h in range(GLA_HEADS)]
        ops = [_gla_heads(p, v_ref) for p in ps]
        op = lambda jh: ops[jh[0]][jh[1]]
        st = {jh: sta_ref[jh[0], jh[1]] for jh in items}
        stb = {jh: st[jh].astype(BF16) for jh in items}
        a = {jh: jnp.where(tril, _mm_nt(op(jh)["qe"], op(jh)["ke"]), 0.0).astype(BF16) for jh in items}
        o1 = {jh: _mm(a[jh], op(jh)["v"]) for jh in items}
        o2 = {jh: _mm_nt(op(jh)["qe"], stb[jh]) for jh in items}
        dob = {}
        dgn = jnp.zeros((1, GLA_DV), F32)
        for jh in items:
            rows, vs = ps[jh[0]]["rows"], op(jh)["vs"]
            o = o1[jh] + o2[jh]
            rr = r_ref[rows, vs]
            sr = _sigmoid(rr)
            rs = lax.rsqrt(jnp.mean(o * o, axis=-1, keepdims=True) + LN_EPS)
            y = o * rs
            dgo = dgo_ref[rows, vs]
            don = dgo * (rr * sr)
            dr_ref[rows, vs] = (dgo * (y * gn_ref[...]) * (sr * (1.0 + rr * (1.0 - sr)))).astype(BF16)
            dgn = dgn + _rowsum(don * y)
            dxn = don * gn_ref[...]
            dob[jh] = (rs * (dxn - y * jnp.mean(dxn * y, axis=-1, keepdims=True))).astype(BF16)
        da = {jh: jnp.where(tril, _mm_nt(dob[jh], op(jh)["v"]), 0.0).astype(BF16) for jh in items}
        dv1 = {jh: _mm_tn(a[jh], dob[jh]) for jh in items}
        dqe1 = {jh: _mm(da[jh], op(jh)["ke"]) for jh in items}
        dqe2 = {jh: _mm(dob[jh], stb[jh]) for jh in items}
        dke1 = {jh: _mm_tn(da[jh], op(jh)["qe"]) for jh in items}
        inc = {jh: _mm_tn(dob[jh], op(jh)["qe"]) for jh in items}
        dsts = [dst_ref[h] for h in range(GLA_HEADS)]
        dkd1, dgam1 = {}, {}
        for jh in items:
            j, h = jh
            dst = dsts[h]
            dstb = dst.astype(BF16)
            dv_ref[ps[j]["rows"], op(jh)["vs"]] = (dv1[jh] + _mm_nt(op(jh)["kd"], dstb)).astype(BF16)
            dkd1[jh] = _mm(op(jh)["v"], dstb)
            dgam1[jh] = _rowsum(dst * st[jh])
            dsts[h] = dst * ps[j]["gam"][:, op(jh)["ls"]] + inc[jh]
        for h in range(GLA_HEADS):
            dst_ref[h] = dsts[h]
        upper = (ri <= ci).astype(BF16)
        dbs, dbls = [], []
        for j in range(kc):
            p = ps[j]
            tiles = [[op((j, 2 * hp + h2)) for h2 in range(2)] for hp in range(GLA_HEADS // 2)]
            head = lambda d, hp, h2: d[j, 2 * hp + h2]
            lanes = lambda f: jnp.concatenate([f(hp) for hp in range(GLA_HEADS // 2)], axis=1)
            dqe = lanes(lambda hp: sum(jnp.where(tiles[hp][h2]["m"], head(dqe1, hp, h2) + head(dqe2, hp, h2), 0.0)
                                       for h2 in range(2)))
            dke = lanes(lambda hp: head(dke1, hp, 0) + head(dke1, hp, 1))
            dkd = lanes(lambda hp: sum(jnp.where(tiles[hp][h2]["m"], head(dkd1, hp, h2), 0.0) for h2 in range(2)))
            dgam = lanes(lambda hp: head(dgam1, hp, 0) + head(dgam1, hp, 1))
            dqk_ref[p["rows"], :hk] = (dqe * p["eb"] * QK_SCALE).astype(BF16)
            dqk_ref[p["rows"], hk:] = (dke * p["enb"] + dkd * p["ebl"]).astype(BF16)
            dkdkd = dkd * p["kd"]
            dbs.append(dqe * p["qe"] - dke * p["ke"] - dkdkd)
            dbls.append(_rowsum(dkdkd) + dgam * p["gam"])
        dlgs = _tri_mm_all(upper, dbs)
        dzb = []
        dgb = jnp.zeros((1, hk), F32)
        for j in range(kc):
            p = ps[j]
            dz = jnp.where(p["real"], (dlgs[j] + dbls[j]) * (1.0 / GLA_TAU) * _sigmoid(-p["z"]), 0.0)
            dgb = dgb + _rowsum(dz)
            dzb.append(dz.astype(BF16))
        dgup = sum(_mm_tn(ps[j]["gd"].astype(BF16), dzb[j]) for j in range(kc))
        for j in range(kc):
            dgd_ref[ps[j]["rows"], :] = _mm_nt(dzb[j], gup_ref[...]).astype(BF16)
        dgb_ref[...] += dgb
        dgup_ref[...] += dgup
        dgn_ref[...] += dgn

    rowblk = lambda col: (lambda b, t: (b * ns + ns - 1 - t, col))
    const = lambda b, t: (0, 0)
    return pl.pallas_call(
        body, name="gla_bwd", grid=(bsz, ns),
        in_specs=[pl.BlockSpec((kc * CHUNK, QK_WIDTH), rowblk(U_QK)), pl.BlockSpec((kc * CHUNK, hv), rowblk(U_V)),
                  pl.BlockSpec((kc * CHUNK, hv), rowblk(U_R)), pl.BlockSpec((kc * CHUNK, LANES), rowblk(U_GD)),
                  pl.BlockSpec((kc * CHUNK, hv), rowblk(DMI_GLA)),
                  pl.BlockSpec((kc, GLA_HEADS, LANES, LANES), lambda b, t: (b * ns + ns - 1 - t, 0, 0, 0)),
                  pl.BlockSpec((LANES, GUP_WIDTH), const), pl.BlockSpec((1, GUP_WIDTH), const), pl.BlockSpec((1, GLA_DV), const)],
        out_specs=[pl.BlockSpec((kc * CHUNK, 2 * hk), rowblk(0)), pl.BlockSpec((kc * CHUNK, hv), rowblk(0)),
                   pl.BlockSpec((kc * CHUNK, hv), rowblk(0)), pl.BlockSpec((kc * CHUNK, LANES), rowblk(0)),
                   pl.BlockSpec((1, GLA_DV), const), pl.BlockSpec((1, GUP_WIDTH), const),
                   pl.BlockSpec((LANES, GUP_WIDTH), const)],
        out_shape=[_sds((r, 2 * hk), BF16), _sds((r, hv), BF16), _sds((r, hv), BF16), _sds((r, LANES), BF16),
                   _sds((1, GLA_DV), F32), _sds((1, GUP_WIDTH), F32), _sds((LANES, GUP_WIDTH), F32)],
        scratch_shapes=[pltpu.VMEM((GLA_HEADS, LANES, LANES), F32)],
        compiler_params=_params(("arbitrary", "arbitrary")),
    )(u, u, u, u, dmi, sta, gup, gb, gn)


def _conv_bwd(u, c, dmi, w32, cg, cbe, tp, tc, dc):
    r = u.shape[0]
    hb = tc // CONV_HALO
    nhalo = r // CONV_HALO

    def dconv(cv, dco, cg_ref, cbe_ref):
        xhat, rstd = _ln(cv)
        cn = xhat * cg_ref[...] + cbe_ref[...]
        sg = _sigmoid(cn)
        dcn = dco * (sg * (1.0 + cn * (1.0 - sg)))
        return _ln_bwd(dcn * cg_ref[...], xhat, rstd), dcn, xhat

    def body(a_ref, g_ref, ah_ref, gh_ref, c_ref, dco_ref, ch_ref, dcoh_ref, w_ref, cg_ref, cbe_ref,
             du_ref, dw_ref, dcb_ref, dcg_ref, dcbe_ref, hs_ref, dcs_ref, dw8_ref):
        t = pl.program_id(0)

        @pl.when(t == 0)
        def _():
            dw8_ref[...] = jnp.zeros_like(dw8_ref)
            dcb_ref[...] = jnp.zeros_like(dcb_ref)
            dcg_ref[...] = jnp.zeros_like(dcg_ref)
            dcbe_ref[...] = jnp.zeros_like(dcbe_ref)

        first = lax.rem(t * tc, tp) == 0
        last = lax.rem((t + 1) * tc, tp) == 0
        hh = ah_ref[...] * _sigmoid(gh_ref[...])
        hs_ref[0:CONV_HALO, :] = jnp.where(first, 0.0, hh)
        hs_ref[CONV_HALO:CONV_HALO + tc, :] = a_ref[...] * _sigmoid(g_ref[...])
        dch, _, _ = dconv(ch_ref[...], dcoh_ref[...], cg_ref, cbe_ref)
        dcs_ref[tc:tc + CONV_HALO, :] = jnp.where(last, 0.0, dch)

        lrows = tc // 4

        def sub1(k, carry):
            r0 = pl.multiple_of(k * lrows, 8)
            dcv, dcn, xhat = dconv(c_ref[pl.ds(r0, lrows), :], dco_ref[pl.ds(r0, lrows), :], cg_ref, cbe_ref)
            dcs_ref[pl.ds(r0, lrows), :] = dcv
            dcb_ref[...] += _rowsum(dcv)
            dcg_ref[...] += _rowsum(dcn * xhat)
            dcbe_ref[...] += _rowsum(dcn)
            return carry

        lax.fori_loop(0, 4, sub1, 0)

        def sub2(k, carry):
            r0 = pl.multiple_of(k * CONV_SUB, CONV_SUB)
            dwin = dcs_ref[pl.ds(r0, CONV_WIN), :]
            dh = _conv_taps(dwin, lambda o: w_ref[CONV_WIDTH - 1 - o:CONV_WIDTH - o, :], 0)
            av = a_ref[pl.ds(r0, CONV_SUB), :]
            sg = _sigmoid(g_ref[pl.ds(r0, CONV_SUB), :])
            du_ref[pl.ds(r0, CONV_SUB), 0:dc] = (dh * sg).astype(BF16)
            du_ref[pl.ds(r0, CONV_SUB), dc:2 * dc] = (dh * av * sg * (1.0 - sg)).astype(BF16)
            hwin = hs_ref[pl.ds(r0, CONV_WIN), :]
            dcv = dwin[0:CONV_SUB, :]
            for rho in range(8):
                offs = [o for o in range(2, 2 + CONV_WIDTH) if o % 8 == rho]
                rolled = hwin if rho == 0 else pltpu.roll(hwin, CONV_WIN - rho, 0)
                for o in offs:
                    m8 = o - rho
                    prod = dcv * rolled[m8:m8 + CONV_SUB, :]
                    dw8_ref[8 * (o - 2):8 * (o - 1), :] += jnp.sum(prod.reshape(CONV_SUB // 8, 8, dc), axis=0)
            return carry

        lax.fori_loop(0, tc // CONV_SUB, sub2, 0)

        @pl.when(t == pl.num_programs(0) - 1)
        def _():
            dw_ref[...] = jnp.zeros_like(dw_ref)
            for j in range(CONV_WIDTH):
                dw_ref[j:j + 1, :] = _rowsum(dw8_ref[8 * j:8 * (j + 1), :])

    vec = pl.BlockSpec((1, dc), lambda t: (0, 0))
    prev = lambda col: (lambda t: (jnp.maximum(t * hb - 1, 0), col))
    nxt = lambda col: (lambda t: (jnp.minimum((t + 1) * hb, nhalo - 1), col))
    return pl.pallas_call(
        body, name="conv_bwd", grid=(r // tc,),
        in_specs=[pl.BlockSpec((tc, dc), lambda t: (t, 0)), pl.BlockSpec((tc, dc), lambda t: (t, 1)),
                  pl.BlockSpec((CONV_HALO, dc), prev(0)), pl.BlockSpec((CONV_HALO, dc), prev(1)),
                  pl.BlockSpec((tc, dc), lambda t: (t, 0)), pl.BlockSpec((tc, dc), lambda t: (t, 0)),
                  pl.BlockSpec((CONV_HALO, dc), nxt(0)), pl.BlockSpec((CONV_HALO, dc), nxt(0)),
                  pl.BlockSpec((32, dc), lambda t: (0, 0)), vec, vec],
        out_specs=[pl.BlockSpec((tc, 2 * dc), lambda t: (t, 0)), pl.BlockSpec((32, dc), lambda t: (0, 0)), vec, vec, vec],
        out_shape=[_sds((r, 2 * dc), BF16), _sds((32, dc), F32), _sds((1, dc), F32), _sds((1, dc), F32), _sds((1, dc), F32)],
        scratch_shapes=[pltpu.VMEM((CONV_HALO + tc, dc), F32), pltpu.VMEM((tc + CONV_HALO, dc), F32),
                        pltpu.VMEM((8 * 32, dc), F32)],
        compiler_params=_params(("arbitrary",)),
    )(u, u, u, u, c, dmi, c, dmi, w32, cg, cbe)


def _inproj_bwd(dp1, dus, xsrc, g_in, w_in, tp, seq, tx):
    r, d = dp1.shape
    widths = [x.shape[1] for x in dus]
    offs = [sum(widths[:k]) for k in range(len(widths))]
    n = w_in.shape[0]
    nd = len(dus)
    head = tx == 0
    rows = X_OFF if head else tx

    def body(*refs):
        dp_ref = refs[0]
        du_refs = refs[1:1 + nd]
        x_ref, g_ref, w_ref, out_ref, dg_ref, db_ref = refs[1 + nd:]
        i = pl.program_id(0)

        @pl.when(i == 0)
        def _():
            dg_ref[...] = jnp.zeros_like(dg_ref)
            db_ref[...] = jnp.zeros_like(db_ref)
            if head:
                out_ref[...] = jnp.zeros_like(out_ref)

        ds0 = ALPHA * dp_ref[...]
        for k in range(nd):
            ds0 = ds0 + _mm(du_refs[k][...], w_ref[offs[k]:offs[k] + widths[k], :])
        if head:
            ds0 = jnp.where(lax.broadcasted_iota(jnp.int32, (X_OFF, 1), 0) >= PAD_FRONT, ds0, 0.0)
        xhat, rstd = _ln(x_ref[...])
        dg_ref[...] += _rowsum(ds0 * xhat)
        db_ref[...] += _rowsum(ds0)
        dx = _ln_bwd(ds0 * g_ref[...], xhat, rstd)
        if head:
            out_ref[...] += dx[PAD_FRONT:X_OFF, :]
        else:
            out_ref[...] = dx

    if head:
        nb = tp // X_OFF
        row = lambda w: pl.BlockSpec((X_OFF, w), lambda i: (i * nb, 0))
        xspec = pl.BlockSpec((X_OFF, d), lambda i: (0, 0))
        ospec, oshape, steps = pl.BlockSpec((N_META, d), lambda i: (0, 0)), _sds((N_META, d), F32), r // tp
    else:
        start = _x_tile_row(tp, seq, tx)
        row = lambda w: pl.BlockSpec((pl.Element(tx), pl.Element(w)), lambda i: (start(i), 0))
        xspec = pl.BlockSpec((tx, d), lambda i: (i, 0))
        ospec, oshape, steps = xspec, _sds(xsrc.shape, F32), xsrc.shape[0] // tx
    vec = pl.BlockSpec((1, d), lambda i: (0, 0))
    return pl.pallas_call(
        body, name="inproj_bwd_head" if head else "inproj_bwd_x", grid=(steps,),
        in_specs=[row(d)] + [row(w) for w in widths] + [xspec, vec, pl.BlockSpec((n, d), lambda i: (0, 0))],
        out_specs=[ospec, vec, vec],
        out_shape=[oshape, _sds((1, d), F32), _sds((1, d), F32)],
        compiler_params=_params(("arbitrary",)),
    )(dp1, *dus, xsrc, g_in, w_in)


def _inproj_bwd_w(s0, dus, tm):
    r, d = s0.shape
    widths = [x.shape[1] for x in dus]
    offs = [sum(widths[:k]) for k in range(len(widths))]
    nd = len(dus)

    def body(*refs):
        s_ref = refs[0]
        du_refs = refs[1:1 + nd]
        dw_ref, acc_ref = refs[1 + nd:]
        i = pl.program_id(0)

        @pl.when(i == 0)
        def _():
            acc_ref[...] = jnp.zeros_like(acc_ref)

        for k in range(nd):
            acc_ref[offs[k]:offs[k] + widths[k], :] += _mm_tn(du_refs[k][...], s_ref[...])

        @pl.when(i == pl.num_programs(0) - 1)
        def _():
            dw_ref[...] = acc_ref[...].astype(BF16)

    row = lambda w: pl.BlockSpec((tm, w), lambda i: (i, 0))
    return pl.pallas_call(
        body, name="inproj_bwd_w", grid=(r // tm,),
        in_specs=[row(d)] + [row(w) for w in widths],
        out_specs=pl.BlockSpec((sum(widths), d), lambda i: (0, 0)),
        out_shape=_sds((sum(widths), d), BF16),
        scratch_shapes=[pltpu.VMEM((sum(widths), d), F32)],
        compiler_params=_params(("arbitrary",)),
    )(s0, *dus)


def _local_step(x, tgt, meta, ln_in_g, ln_in_b, w_in, conv_w, conv_b, conv_ln_g, conv_ln_b, gate_up, gate_bias,
                gla_norm_g, late_weights, ln1_g, ln1_b, ln2_g, ln2_b, push):
    bsz, seq, d = x.shape
    tp = X_OFF + seq
    assert tp % CHUNK == 0
    nc = tp // CHUNK
    dc = conv_b.shape[1]
    tmm = tc = _pick_tile(tp, (704, 128, 64))
    tx = _pick_tile(seq, (512, 64))
    kc = _pick_tile(nc, (11, 3, 2, 1))
    ns = 2

    x2 = x.reshape(bsz * seq, d)
    head = jnp.pad(meta, ((PAD_FRONT, 0), (0, 0)))
    tgt_p = tgt.reshape(bsz * seq, d)
    w32 = jnp.pad(conv_w, ((0, 32 - CONV_WIDTH), (0, 0)))
    gup = jnp.pad(gate_up, ((0, LANES - GLA_RANK), (0, 0))).astype(BF16)

    s0, s0b = _ln_in_x(x2, ln_in_g, ln_in_b, tp, seq, tx)
    s0, s0b = _ln_in_head(head, ln_in_g, ln_in_b, s0, s0b, tp)
    u = _inproj_fwd(s0b, w_in, tmm)
    c, co = _conv_fwd(u, w32, conv_b, conv_ln_g, conv_ln_b, tp, tc, dc)
    go, sta = _gla_fwd(u, gup, gate_bias, gla_norm_g, bsz, nc, kc)
    w_out, w1g, w2 = late_weights
    nh = w1g.shape[0]
    p1, s1, s1b = _outproj_fwd(s0, co, go, w_out, ln1_g, ln1_b, tmm)
    hm, dp2, dpb, loss, dg2, db2 = _mlp_fwd(s1, s1b, w1g, w2, ln2_g, ln2_b, tgt_p, tp, tmm, ns)

    dh, dp1, dg1, db1 = _mlp_bwd_act(dp2, dpb, hm, w1g, w2, p1, ln1_g, tmm, ns)
    dw1, dw2 = _mlp_bwd_w(s1b, hm, dh, dpb, nh, tmm, ns)
    push("ff", (dw1, dw2))
    dmi, dwo = _outproj_bwd(dp1, co, go, w_out, tmm)
    push("out", (dwo,))
    dqk, dv, dr, dgd, dgn, dgb, dgup = _gla_bwd(u, dmi, sta, gup, gate_bias, gla_norm_g, bsz, nc, kc)
    dcv, dcw, dcb, dcg, dcbe = _conv_bwd(u, c, dmi, w32, conv_ln_g, conv_ln_b, tp, tc, dc)
    dus = [dcv, dqk, dv, dr, dgd]
    dwi = _inproj_bwd_w(s0b, dus, tmm)
    push("in", (dwi,))
    gx, dgx, dbx = _inproj_bwd(dp1, dus, x2, ln_in_g, w_in, tp, seq, tx)
    dmeta, dgh, dbh = _inproj_bwd(dp1, dus, head, ln_in_g, w_in, tp, seq, 0)

    return dict(loss=loss[0, 0], grad_x=gx.reshape(bsz, seq, d), meta_tokens=dmeta, ln_in_g=dgx + dgh, ln_in_b=dbx + dbh,
                conv_w=dcw[:CONV_WIDTH], conv_b=dcb, conv_ln_g=dcg, conv_ln_b=dcbe,
                gate_up=dgup[:GLA_RANK], gate_bias=dgb, gla_norm_g=dgn, ln1_g=dg1, ln1_b=db1, ln2_g=dg2, ln2_b=db2)


def _exchange(arrays, scatter, name):
    na = len(arrays)
    npeer = N_DEV - 1

    def body(*refs):
        srcs = refs[:na]
        outs = refs[na:2 * na]
        send_sems, recv_sems, local_sems = refs[2 * na:]
        xi, yi, ci = (lax.axis_index(a) for a in MESH_AXES)
        me = 4 * xi + 2 * yi + ci
        copies = []
        for a in range(na):
            own = srcs[a].at[me] if scatter[a] else srcs[a]
            cp = pltpu.make_async_copy(own, outs[a].at[me], local_sems.at[a])
            cp.start()
            copies.append(cp)
        remote = []
        for k in range(1, N_DEV):
            px, py, pc = xi ^ (k >> 2), yi ^ ((k >> 1) & 1), ci ^ (k & 1)
            peer = 4 * px + 2 * py + pc
            for a in range(na):
                src = srcs[a].at[peer] if scatter[a] else srcs[a]
                cp = pltpu.make_async_remote_copy(
                    src_ref=src, dst_ref=outs[a].at[me],
                    send_sem=send_sems.at[a * npeer + k - 1], recv_sem=recv_sems.at[a * npeer + k - 1],
                    device_id=(px, py, pc), device_id_type=pl.DeviceIdType.MESH)
                cp.start()
                remote.append(cp)
        for cp in remote:
            cp.wait()
        for cp in copies:
            cp.wait()

    out_shape = [_sds(a.shape if scatter[i] else (N_DEV,) + a.shape, a.dtype) for i, a in enumerate(arrays)]
    anyspec = pl.BlockSpec(memory_space=pl.ANY)
    return pl.pallas_call(
        body, name=name,
        in_specs=[anyspec] * na, out_specs=[anyspec] * na, out_shape=out_shape,
        scratch_shapes=[pltpu.SemaphoreType.DMA((na * npeer,)), pltpu.SemaphoreType.DMA((na * npeer,)),
                        pltpu.SemaphoreType.DMA((na,))],
    )(*arrays)


def _peers(xi, yi, ci):
    for k in range(1, N_DEV):
        px, py, pc = xi ^ (k >> 2), yi ^ ((k >> 1) & 1), ci ^ (k & 1)
        yield (px, py, pc), 4 * px + 2 * py + pc


def _sc_exchange(arrays, scatter, name, collective_id, after=None):
    na = len(arrays)
    npeer = N_DEV - 1
    ndep = 0 if after is None else 1

    def body(*refs):
        srcs = refs[:na]
        outs = refs[na + ndep:2 * na + ndep]
        send_sems, recv_sems, own_sems = refs[2 * na + ndep:]
        xi, yi, ci = (lax.axis_index(a) for a in MESH_AXES)
        me = 4 * xi + 2 * yi + ci
        barrier = pltpu.get_barrier_semaphore()
        for pos, _ in _peers(xi, yi, ci):
            pl.semaphore_signal(barrier, inc=1, device_id=pos, device_id_type=pl.DeviceIdType.MESH)
        pl.semaphore_wait(barrier, npeer)
        own = [pltpu.make_async_copy(srcs[a].at[me] if scatter[a] else srcs[a], outs[a].at[me], own_sems.at[a])
               for a in range(na)]
        for cp in own:
            cp.start()
        remote = []
        for a in range(na):
            for k, (pos, peer) in enumerate(_peers(xi, yi, ci)):
                cp = pltpu.make_async_remote_copy(
                    src_ref=srcs[a].at[peer] if scatter[a] else srcs[a], dst_ref=outs[a].at[me],
                    send_sem=send_sems.at[a * npeer + k], recv_sem=recv_sems.at[a * npeer + k],
                    device_id=pos, device_id_type=pl.DeviceIdType.MESH)
                cp.start()
                remote.append(cp)
        for cp in own:
            cp.wait()
        for cp in remote:
            cp.wait()

    out_type = [_sds(a.shape if scatter[i] else (N_DEV,) + a.shape, a.dtype) for i, a in enumerate(arrays)]
    sent = sum(a.size * a.dtype.itemsize // (N_DEV if scatter[i] else 1) for i, a in enumerate(arrays))
    return pl.kernel(
        body, out_type=out_type, mesh=plsc.ScalarSubcoreMesh(axis_name="seq", num_cores=1), name=name,
        scratch_types=[pltpu.SemaphoreType.DMA((na * npeer,)), pltpu.SemaphoreType.DMA((na * npeer,)),
                       pltpu.SemaphoreType.DMA((na,))],
        compiler_params=pltpu.CompilerParams(collective_id=collective_id),
        cost_estimate=pl.CostEstimate(flops=0, transcendentals=0, bytes_accessed=2 * N_DEV * sent,
                                      remote_bytes_transferred=npeer * sent),
    )(*arrays, *([] if after is None else [after]))


def _sc_gather(arrays, name, collective_id, after=None):
    na = len(arrays)
    ndep = 0 if after is None else 1
    npair = N_DEV - 1

    def body(*refs):
        srcs = refs[:na]
        outs = refs[na + ndep:2 * na + ndep]
        send_sems, recv_sems, own_sems = refs[2 * na + ndep:]
        xi, yi, ci = (lax.axis_index(a) for a in MESH_AXES)
        me = 4 * xi + 2 * yi + ci
        sibling = (xi, yi, 1 - ci)
        chips = [(1 - xi, yi), (xi, 1 - yi), (1 - xi, 1 - yi)]
        barrier = pltpu.get_barrier_semaphore()
        for pos, _ in _peers(xi, yi, ci):
            pl.semaphore_signal(barrier, inc=1, device_id=pos, device_id_type=pl.DeviceIdType.MESH)
        pl.semaphore_wait(barrier, npair)

        def copy(a, k, src, slot, to):
            return pltpu.make_async_remote_copy(
                src_ref=src, dst_ref=outs[a].at[slot], send_sem=send_sems.at[a * npair + k],
                recv_sem=recv_sems.at[a * npair + k], device_id=to, device_id_type=pl.DeviceIdType.MESH)

        own = [pltpu.make_async_copy(srcs[a], outs[a].at[me], own_sems.at[a]) for a in range(na)]
        for cp in own:
            cp.start()
        sent = []
        for a in range(na):
            sent.append(copy(a, 0, srcs[a], me, sibling))
            sent += [copy(a, 1 + j, srcs[a], me, (*chip, ci)) for j, chip in enumerate(chips)]
        for cp in sent:
            cp.start()
        for j, (cx, cy) in enumerate(chips):
            slot = 4 * cx + 2 * cy + ci
            for a in range(na):
                copy(a, 1 + j, srcs[a], slot, sibling).wait_recv()
                cp = copy(a, 4 + j, outs[a].at[slot], slot, sibling)
                cp.start()
                sent.append(cp)
        for a in range(na):
            copy(a, 0, srcs[a], me, sibling).wait_recv()
            for j in range(len(chips)):
                copy(a, 4 + j, srcs[a], me, sibling).wait_recv()
        for cp in sent:
            cp.wait_send()
        for cp in own:
            cp.wait()

    out_type = [_sds((N_DEV,) + a.shape, a.dtype) for a in arrays]
    sent_bytes = sum(a.size * a.dtype.itemsize for a in arrays)
    return pl.kernel(
        body, out_type=out_type, mesh=plsc.ScalarSubcoreMesh(axis_name="seq", num_cores=1), name=name,
        scratch_types=[pltpu.SemaphoreType.DMA((na * npair,)), pltpu.SemaphoreType.DMA((na * npair,)),
                       pltpu.SemaphoreType.DMA((na,))],
        compiler_params=pltpu.CompilerParams(collective_id=collective_id),
        cost_estimate=pl.CostEstimate(flops=0, transcendentals=0, bytes_accessed=2 * N_DEV * sent_bytes,
                                      remote_bytes_transferred=npair * sent_bytes),
    )(*arrays, *([] if after is None else [after]))


def _adamw(w, g, m, v):
    m = ADAM_B1 * m + (1.0 - ADAM_B1) * g
    v = ADAM_B2 * v + (1.0 - ADAM_B2) * jnp.square(g)
    m_hat = m / (1.0 - ADAM_B1 ** ADAM_STEP)
    v_hat = v / (1.0 - ADAM_B2 ** ADAM_STEP)
    delta = -ADAM_LR * (m_hat / (jnp.sqrt(v_hat) + ADAM_EPS) + ADAM_WD * w)
    return delta, m, v


def _sum_devices(ref):
    g = ref[0].astype(F32)
    for k in range(1, N_DEV):
        g = g + ref[k].astype(F32)
    return g


def _update_big(parts, w, m, v, name):
    rows, cols = w.shape[0], w.shape[-1]
    mid = (None,) * (w.ndim - 2)
    mid0 = (0,) * (w.ndim - 2)

    def body(p_ref, w_ref, m_ref, v_ref, g_ref, d_ref, nm_ref, nv_ref):
        g = _sum_devices(p_ref)
        g_ref[...] = g
        d_ref[...], nm_ref[...], nv_ref[...] = _adamw(w_ref[...], g, m_ref[...], v_ref[...])

    if rows % 16 == 0:
        tr = _pick_tile(rows, (128, 64, 16))
        steps, blk = rows // tr, pl.BlockSpec((tr,) + mid + (cols,), lambda i: (i,) + mid0 + (0,))
        pblk = pl.BlockSpec((N_DEV, tr, cols), lambda i: (0, i, 0))
    else:
        tcol = 2 * LANES
        steps, blk = cols // tcol, pl.BlockSpec((rows,) + mid + (tcol,), lambda i: (0,) + mid0 + (i,))
        pblk = pl.BlockSpec((N_DEV, rows, tcol), lambda i: (0, 0, i))
    return pl.pallas_call(
        body, name=name, grid=(steps,),
        in_specs=[pblk, blk, blk, blk],
        out_specs=[blk] * 4, out_shape=[_sds(w.shape, F32)] * 4,
        compiler_params=_params(("parallel",)),
    )(parts, w, m, v)


_VEC_ORDER = ("ln_in_g", "ln_in_b", "conv_b", "conv_ln_g", "conv_ln_b", "gate_bias", "gla_norm_g",
              "ln1_g", "ln1_b", "ln2_g", "ln2_b")
_SHARDED_SMALL = (("meta_tokens", 0, N_META, LANES), ("conv_w", N_META, CONV_WIDTH, None), ("gate_up", N_META + 32, GLA_RANK, None))


def _update_small(parts_sh, parts_vec, wmv):
    names = [s[0] for s in _SHARDED_SMALL] + list(_VEC_ORDER)
    flat = [a for nme in names for a in wmv[nme]]
    nv = len(_VEC_ORDER)

    def body(*refs):
        sh_ref, vec_ref = refs[0], refs[1]
        ins = refs[2:2 + len(flat)]
        outs = refs[2 + len(flat):2 + len(flat) + 4 * len(names)]
        loss_ref = refs[2 + len(flat) + 4 * len(names)]
        gsh_ref, gvec_ref = refs[-2:]
        gsh_ref[...] = _sum_devices(sh_ref)
        gvec_ref[...] = _sum_devices(vec_ref)
        loss_ref[...] = gvec_ref[nv:nv + 1, :]
        for idx, nme in enumerate(names):
            w_ref, m_ref, v_ref = ins[3 * idx:3 * idx + 3]
            rows, cols = w_ref.shape[0], w_ref.shape[-1]
            at = (slice(None),) + (0,) * (len(w_ref.shape) - 2) + (slice(None),)
            if idx < len(_SHARDED_SMALL):
                r0 = _SHARDED_SMALL[idx][1]
                g = gsh_ref[r0:r0 + rows, 0:cols]
            else:
                j = idx - len(_SHARDED_SMALL)
                g = gvec_ref[j:j + 1, 0:cols]
            o = outs[4 * idx:4 * idx + 4]
            o[0][at] = g
            o[1][at], o[2][at], o[3][at] = _adamw(w_ref[at], g, m_ref[at], v_ref[at])

    out_shape = [_sds(wmv[nme][0].shape, F32) for nme in names for _ in range(4)] + [_sds((1, parts_vec.shape[2]), F32)]
    vmem = pl.BlockSpec(memory_space=pltpu.VMEM)
    res = pl.pallas_call(
        body, name="update_small", out_shape=out_shape,
        in_specs=[vmem] * (2 + len(flat)), out_specs=[vmem] * len(out_shape),
        scratch_shapes=[pltpu.VMEM(parts_sh.shape[1:], F32), pltpu.VMEM(parts_vec.shape[1:], F32)],
    )(parts_sh, parts_vec, *flat)
    return {nme: res[4 * i:4 * i + 4] for i, nme in enumerate(names)}, res[-1][0, 0]


_WEIGHTS = ("meta_tokens", "ln_in_g", "ln_in_b", "w_in", "conv_w", "conv_b", "conv_ln_g", "conv_ln_b", "gate_up",
            "gate_bias", "gla_norm_g", "w_out", "ln1_g", "ln1_b", "w_ff1", "w_ff2", "ln2_g", "ln2_b")


def kernel(x, meta_tokens, ln_in_g, ln_in_b, w_in, conv_w, conv_b, conv_ln_g, conv_ln_b, gate_up, gate_bias, gla_norm_g, w_out, ln1_g, ln1_b, w_ff1, w_ff2, ln2_g, ln2_b, loss_target, m_meta_tokens, m_ln_in_g, m_ln_in_b, m_w_in, m_conv_w, m_conv_b, m_conv_ln_g, m_conv_ln_b, m_gate_up, m_gate_bias, m_gla_norm_g, m_w_out, m_ln1_g, m_ln1_b, m_w_ff1, m_w_ff2, m_ln2_g, m_ln2_b, v_meta_tokens, v_ln_in_g, v_ln_in_b, v_w_in, v_conv_w, v_conv_b, v_conv_ln_g, v_conv_ln_b, v_gate_up, v_gate_bias, v_gla_norm_g, v_w_out, v_ln1_g, v_ln1_b, v_w_ff1, v_w_ff2, v_ln2_g, v_ln2_b):
    w = dict(meta_tokens=meta_tokens, ln_in_g=ln_in_g, ln_in_b=ln_in_b, w_in=w_in, conv_w=conv_w, conv_b=conv_b,
             conv_ln_g=conv_ln_g, conv_ln_b=conv_ln_b, gate_up=gate_up, gate_bias=gate_bias, gla_norm_g=gla_norm_g,
             w_out=w_out, ln1_g=ln1_g, ln1_b=ln1_b, w_ff1=w_ff1, w_ff2=w_ff2, ln2_g=ln2_g, ln2_b=ln2_b)
    mom = dict(meta_tokens=m_meta_tokens, ln_in_g=m_ln_in_g, ln_in_b=m_ln_in_b, w_in=m_w_in, conv_w=m_conv_w,
               conv_b=m_conv_b, conv_ln_g=m_conv_ln_g, conv_ln_b=m_conv_ln_b, gate_up=m_gate_up, gate_bias=m_gate_bias,
               gla_norm_g=m_gla_norm_g, w_out=m_w_out, ln1_g=m_ln1_g, ln1_b=m_ln1_b, w_ff1=m_w_ff1, w_ff2=m_w_ff2,
               ln2_g=m_ln2_g, ln2_b=m_ln2_b)
    var = dict(meta_tokens=v_meta_tokens, ln_in_g=v_ln_in_g, ln_in_b=v_ln_in_b, w_in=v_w_in, conv_w=v_conv_w,
               conv_b=v_conv_b, conv_ln_g=v_conv_ln_g, conv_ln_b=v_conv_ln_b, gate_up=v_gate_up, gate_bias=v_gate_bias,
               gla_norm_g=v_gla_norm_g, w_out=v_w_out, ln1_g=v_ln1_g, ln1_b=v_ln1_b, w_ff1=v_w_ff1, w_ff2=v_w_ff2,
               ln2_g=v_ln2_g, ln2_b=v_ln2_b)
    shapes = {k: a.shape for k, a in w.items()}

    def two_d(a):
        return a.reshape(1, -1) if a.ndim == 1 else a.reshape(a.shape[-2:])

    w2d = {k: two_d(a) for k, a in w.items()}
    m2d = {k: two_d(a) for k, a in mom.items()}
    v2d = {k: two_d(a) for k, a in var.items()}
    d = x.shape[-1]
    d_in = w2d["w_in"].shape[1] * N_DEV
    d_in_p = -(-d_in // LANES) * LANES

    in_wmv = [jnp.transpose(dct["w_in"], (2, 0, 1)) for dct in (w, mom, var)]
    g_in, g_meta, g_conv, g_gup = _sc_gather(
        [w2d["w_in"].T.astype(BF16), w2d["meta_tokens"], w2d["conv_w"], w2d["gate_up"]], "gather_first", 0)
    g_out, g_ff1, g_ff2 = _sc_gather(
        [w2d["w_out"].astype(BF16), w2d["w_ff1"].astype(BF16), w2d["w_ff2"].astype(BF16)], "gather_late", 1)
    w_in_full = jnp.pad(g_in.reshape(d_in, d), ((0, d_in_p - d_in), (0, 0)))
    meta_full = g_meta.transpose(1, 0, 2).reshape(N_META, d)
    conv_w_full = g_conv.transpose(1, 0, 2).reshape(CONV_WIDTH, -1)
    gate_up_full = g_gup.transpose(1, 0, 2).reshape(GLA_RANK, -1)

    late_weights = (g_out.reshape(-1, d), g_ff1, g_ff2.reshape(-1, d))
    pushed = {}

    def push(tag, grads):
        if tag == "ff":
            pushed["ff1"], pushed["ff2"] = _sc_exchange(list(grads), [True, True], "scatter_ff", 2)
        elif tag == "out":
            pushed["p_out"] = grads[0].reshape(N_DEV, -1, d)
        else:
            p_in = grads[0][:d_in].reshape(N_DEV, d_in // N_DEV, d)
            pushed["in"], pushed["out"] = _sc_exchange([p_in, pushed["p_out"]], [True, True], "scatter_rest", 3,
                                                       after=pushed["ff1"])

    res = _local_step(x, loss_target, meta_full, w2d["ln_in_g"], w2d["ln_in_b"], w_in_full, conv_w_full, w2d["conv_b"],
                      w2d["conv_ln_g"], w2d["conv_ln_b"], gate_up_full, w2d["gate_bias"], w2d["gla_norm_g"], late_weights,
                      w2d["ln1_g"], w2d["ln1_b"], w2d["ln2_g"], w2d["ln2_b"], push)

    dc = res["conv_w"].shape[1]
    hk = res["gate_up"].shape[1]
    sh_meta = res["meta_tokens"].reshape(N_META, N_DEV, LANES).transpose(1, 0, 2)
    sh_conv = jnp.pad(res["conv_w"].reshape(CONV_WIDTH, N_DEV, dc // N_DEV).transpose(1, 0, 2),
                      ((0, 0), (0, 32 - CONV_WIDTH), (0, LANES - dc // N_DEV)))
    sh_gup = jnp.pad(res["gate_up"].reshape(GLA_RANK, N_DEV, hk // N_DEV).transpose(1, 0, 2),
                     ((0, 0), (0, 0), (0, LANES - hk // N_DEV)))
    p_sh = jnp.concatenate([sh_meta, sh_conv, sh_gup], axis=1)
    p_vec = jnp.concatenate([jnp.pad(res[k], ((0, 0), (0, d - res[k].shape[1]))) for k in _VEC_ORDER]
                            + [jnp.full((1, d), res["loss"], F32), jnp.zeros((15 - len(_VEC_ORDER), d), F32)], axis=0)

    r_ff1, r_ff2, r_out, r_in = pushed["ff1"], pushed["ff2"], pushed["out"], pushed["in"]
    upd = {}
    upd["w_ff1"] = _update_big(r_ff1, w2d["w_ff1"], m2d["w_ff1"], v2d["w_ff1"], "update_w_ff1")
    upd["w_ff2"] = _update_big(r_ff2, w2d["w_ff2"], m2d["w_ff2"], v2d["w_ff2"], "update_w_ff2")
    p_sh, p_vec, upd["w_ff1"], upd["w_ff2"] = lax.optimization_barrier((p_sh, p_vec, upd["w_ff1"], upd["w_ff2"]))
    r_sh, r_vec = _exchange([p_sh, p_vec], [True, False], "scatter_small")

    upd["w_in"] = [jnp.transpose(a, (1, 2, 0)) for a in _update_big(r_in, *in_wmv, "update_w_in")]
    upd["w_out"] = _update_big(r_out, w2d["w_out"], m2d["w_out"], v2d["w_out"], "update_w_out")
    small = [s[0] for s in _SHARDED_SMALL] + list(_VEC_ORDER)
    wmv = {k: (w2d[k], m2d[k], v2d[k]) for k in small}
    wmv["conv_w"] = tuple(jnp.transpose(dct["conv_w"], (1, 0, 2)) for dct in (w, mom, var))
    upd_small, loss = _update_small(r_sh, r_vec, wmv)
    upd.update(upd_small)

    outs = [loss, res["grad_x"]]
    for j in range(4):
        outs += [upd[k][j].reshape(shapes[k]) for k in _WEIGHTS]
    return tuple(outs)
```

```python
import jax
import jax.numpy as jnp
from jax import lax
from jax.experimental import pallas as pl
from jax.experimental.pallas import tpu as pltpu
from jax.experimental.pallas import tpu_sc as plsc

F32 = jnp.float32
BF16 = jnp.bfloat16

N_META = 16
CHUNK = 64
PAD_FRONT = (-N_META) % CHUNK
X_OFF = PAD_FRONT + N_META
CONV_WIDTH = 31
CONV_HALO = 32
CONV_SUB = 64
CONV_WIN = CONV_SUB + CONV_HALO
GLA_HEADS = 4
GLA_DK = 64
GLA_DV = 128
GLA_RANK = 16
GLA_TAU = 16.0
QK_SCALE = GLA_DK ** -0.5
LN_EPS = 1e-5
ALPHA = 2.0 ** 0.25
LANES = 128
N_DEV = 8
ADAM_LR = 0.001
ADAM_B1 = 0.9
ADAM_B2 = 0.999
ADAM_EPS = 1e-08
ADAM_WD = 0.01
ADAM_STEP = 10
VMEM_LIMIT = 56 * 1024 * 1024
MESH_AXES = ("x", "y", "c")
U_QK, U_V, U_R, U_GD = 2, 3, 4, 20
QK_WIDTH = 2 * GLA_HEADS * GLA_DK
GUP_WIDTH = GLA_HEADS * GLA_DK
DMI_GLA = 1


def _sds(shape, dtype):
    return jax.ShapeDtypeStruct(shape, dtype)


def _mm(a, b):
    return jnp.dot(a, b, preferred_element_type=F32)


def _mm_nt(a, b):
    return lax.dot_general(a, b, (((1,), (1,)), ((), ())), preferred_element_type=F32)


def _mm_tn(a, b):
    return lax.dot_general(a, b, (((0,), (0,)), ((), ())), preferred_element_type=F32)


def _sigmoid(x):
    return 1.0 / (1.0 + jnp.exp(-x))


def _log_sigmoid(z):
    return jnp.minimum(z, 0.0) - jnp.log(1.0 + jnp.exp(-jnp.abs(z)))


def _ln(x):
    mu = jnp.mean(x, axis=-1, keepdims=True)
    xc = x - mu
    var = jnp.mean(xc * xc, axis=-1, keepdims=True)
    rstd = lax.rsqrt(var + LN_EPS)
    return xc * rstd, rstd


def _ln_bwd(dyg, xhat, rstd):
    m1 = jnp.mean(dyg, axis=-1, keepdims=True)
    m2 = jnp.mean(dyg * xhat, axis=-1, keepdims=True)
    return rstd * (dyg - m1 - xhat * m2)


def _rowsum(x):
    return jnp.sum(x, axis=0, keepdims=True)


def _row_in_seq(i, tm, tp):
    base = lax.rem(i * tm, tp)
    return base + lax.broadcasted_iota(jnp.int32, (tm, 1), 0)


def _split3(x):
    hi = x.astype(BF16)
    r1 = x - hi.astype(F32)
    mid = r1.astype(BF16)
    lo = (r1 - mid.astype(F32)).astype(BF16)
    return hi, mid, lo


def _params(sem):
    return pltpu.CompilerParams(dimension_semantics=sem, vmem_limit_bytes=VMEM_LIMIT)


def _pick_tile(n, prefs):
    for t in prefs:
        if n % t == 0:
            return t
    raise ValueError(f"no tile for {n}")


ROW_BLOCKS = 2


def _row_blocks(tm, matmuls, finish):
    blocks = [slice(k * tm // ROW_BLOCKS, (k + 1) * tm // ROW_BLOCKS) for k in range(ROW_BLOCKS)]
    acc = matmuls(blocks[0])
    for prev, rows in zip(blocks, blocks[1:]):
        nxt = matmuls(rows)
        finish(prev, acc)
        acc = nxt
    finish(blocks[-1], acc)


def _x_tile_row(tp, seq, tx):
    tps = seq // tx
    return lambda i: pl.multiple_of((i // tps) * tp + X_OFF + (i % tps) * tx, CHUNK)


def _ln_in_x(x2, g, b, tp, seq, tx):
    rx, d = x2.shape
    r = rx // seq * tp
    row = _x_tile_row(tp, seq, tx)

    def body(x_ref, g_ref, b_ref, s0_ref, sb_ref):
        xhat, _ = _ln(x_ref[...])
        s = xhat * g_ref[...] + b_ref[...]
        s0_ref[...] = s
        sb_ref[...] = s.astype(BF16)

    out = pl.BlockSpec((pl.Element(tx), pl.Element(d)), lambda i: (row(i), 0))
    return pl.pallas_call(
        body, name="ln_in_x", grid=(rx // tx,),
        in_specs=[pl.BlockSpec((tx, d), lambda i: (i, 0)), pl.BlockSpec((1, d), lambda i: (0, 0)),
                  pl.BlockSpec((1, d), lambda i: (0, 0))],
        out_specs=[out, out],
        out_shape=[_sds((r, d), F32), _sds((r, d), BF16)],
        compiler_params=_params(("parallel",)),
    )(x2, g, b)


def _ln_in_head(head, g, b, s0, s0b, tp):
    r, d = s0.shape
    nb = tp // X_OFF

    def body(h_ref, g_ref, b_ref, s0_in, sb_in, s0_ref, sb_ref):
        xhat, _ = _ln(h_ref[...])
        real = lax.broadcasted_iota(jnp.int32, (X_OFF, 1), 0) >= PAD_FRONT
        s = jnp.where(real, xhat * g_ref[...] + b_ref[...], 0.0)
        s0_ref[...] = s
        sb_ref[...] = s.astype(BF16)

    anyspec = pl.BlockSpec(memory_space=pl.ANY)
    out = pl.BlockSpec((X_OFF, d), lambda i: (i * nb, 0))
    return pl.pallas_call(
        body, name="ln_in_head", grid=(r // tp,),
        in_specs=[pl.BlockSpec((X_OFF, d), lambda i: (0, 0)), pl.BlockSpec((1, d), lambda i: (0, 0)),
                  pl.BlockSpec((1, d), lambda i: (0, 0)), anyspec, anyspec],
        out_specs=[out, out],
        out_shape=[_sds((r, d), F32), _sds((r, d), BF16)],
        input_output_aliases={3: 0, 4: 1},
        compiler_params=_params(("parallel",)),
    )(head, g, b, s0, s0b)


def _inproj_fwd(s0b, w_in, tm):
    r, d = s0b.shape
    n = w_in.shape[0]

    def body(s_ref, w_ref, u_ref):
        u_ref[...] = _mm_nt(s_ref[...], w_ref[...])

    return pl.pallas_call(
        body, name="inproj_fwd", grid=(r // tm,),
        in_specs=[pl.BlockSpec((tm, d), lambda i: (i, 0)), pl.BlockSpec((n, d), lambda i: (0, 0))],
        out_specs=pl.BlockSpec((tm, n), lambda i: (i, 0)),
        out_shape=_sds((r, n), F32),
        compiler_params=_params(("parallel",)),
    )(s0b, w_in)


def _conv_taps(win, coef, lo):
    acc = None
    for rho in range(8):
        offs = [o for o in range(lo, lo + CONV_WIDTH) if o % 8 == rho]
        if not offs:
            continue
        rolled = win if rho == 0 else pltpu.roll(win, CONV_WIN - rho, 0)
        for o in offs:
            m8 = o - rho
            term = rolled[m8:m8 + CONV_SUB, :] * coef(o)
            acc = term if acc is None else acc + term
    return acc


def _conv_fwd(u, w32, cb, cg, cbe, tp, tc, dc):
    r = u.shape[0]
    hb = tc // CONV_HALO

    def body(a_ref, g_ref, ah_ref, gh_ref, w_ref, cb_ref, cg_ref, cbe_ref, c_ref, co_ref, hs_ref):
        t = pl.program_id(0)
        first = lax.rem(t * tc, tp) == 0
        hh = ah_ref[...] * _sigmoid(gh_ref[...])
        hs_ref[0:CONV_HALO, :] = jnp.where(first, 0.0, hh)
        hs_ref[CONV_HALO:CONV_HALO + tc, :] = a_ref[...] * _sigmoid(g_ref[...])

        def sub(k, carry):
            r0 = pl.multiple_of(k * CONV_SUB, CONV_SUB)
            win = hs_ref[pl.ds(r0, CONV_WIN), :]
            c = _conv_taps(win, lambda o: w_ref[o - 2:o - 1, :], 2) + cb_ref[...]
            c_ref[pl.ds(r0, CONV_SUB), :] = c
            xhat, _ = _ln(c)
            cn = xhat * cg_ref[...] + cbe_ref[...]
            co_ref[pl.ds(r0, CONV_SUB), :] = (cn * _sigmoid(cn)).astype(BF16)
            return carry

        lax.fori_loop(0, tc // CONV_SUB, sub, 0)

    vec = pl.BlockSpec((1, dc), lambda t: (0, 0))
    return pl.pallas_call(
        body, name="conv_fwd", grid=(r // tc,),
        in_specs=[pl.BlockSpec((tc, dc), lambda t: (t, 0)), pl.BlockSpec((tc, dc), lambda t: (t, 1)),
                  pl.BlockSpec((CONV_HALO, dc), lambda t: (jnp.maximum(t * hb - 1, 0), 0)),
                  pl.BlockSpec((CONV_HALO, dc), lambda t: (jnp.maximum(t * hb - 1, 0), 1)),
                  pl.BlockSpec((32, dc), lambda t: (0, 0)), vec, vec, vec],
        out_specs=[pl.BlockSpec((tc, dc), lambda t: (t, 0)), pl.BlockSpec((tc, dc), lambda t: (t, 0))],
        out_shape=[_sds((r, dc), F32), _sds((r, dc), BF16)],
        scratch_shapes=[pltpu.VMEM((CONV_HALO + tc, dc), F32)],
        compiler_params=_params(("parallel",)),
    )(u, u, u, u, w32, cb, cg, cbe)


def _tri_mm_all(tri, xs):
    parts = [_split3(x) for x in xs]
    acc = [None] * len(xs)
    for t in range(3):
        for j in range(len(xs)):
            term = _mm(tri, parts[j][t])
            acc[j] = term if t == 0 else acc[j] + term
    return acc


def _gla_prep(qk_ref, gd_ref, gup, gb, n0, kc):
    rows = [slice(j * CHUNK, (j + 1) * CHUNK) for j in range(kc)]
    ri = lax.broadcasted_iota(jnp.int32, (CHUNK, CHUNK), 0)
    ci = lax.broadcasted_iota(jnp.int32, (CHUNK, CHUNK), 1)
    low = (ri >= ci).astype(BF16)
    hk = GLA_HEADS * GLA_DK
    gds = [gd_ref[rw, :] for rw in rows]
    zs = [_mm(g.astype(BF16), gup) + gb for g in gds]
    reals = [(n0 + j) * CHUNK + lax.broadcasted_iota(jnp.int32, (CHUNK, 1), 0) >= PAD_FRONT for j in range(kc)]
    lgs = [jnp.where(reals[j], _log_sigmoid(zs[j]) * (1.0 / GLA_TAU), 0.0) for j in range(kc)]
    bs = _tri_mm_all(low, lgs)
    out = []
    for j in range(kc):
        b, bl = bs[j], _rowsum(lgs[j])
        q = qk_ref[rows[j], :hk] * QK_SCALE
        k = qk_ref[rows[j], hk:]
        eb, enb, ebl = jnp.exp(b), jnp.exp(-b), jnp.exp(bl - b)
        out.append(dict(rows=rows[j], gd=gds[j], z=zs[j], real=reals[j], eb=eb, enb=enb, ebl=ebl, gam=jnp.exp(bl),
                        qe=q * eb, ke=k * enb, kd=k * ebl))
    return out, ri, ci


def _gla_heads(p, v_ref):
    ops = []
    for h in range(GLA_HEADS):
        hp, h2 = divmod(h, 2)
        ls = slice(hp * LANES, (hp + 1) * LANES)
        m = _head_mask(h2)
        ops.append(dict(ls=ls, m=m, vs=slice(h * GLA_DV, (h + 1) * GLA_DV),
                        qe=jnp.where(m, p["qe"][:, ls], 0.0).astype(BF16),
                        kd=jnp.where(m, p["kd"][:, ls], 0.0).astype(BF16),
                        ke=p["ke"][:, ls].astype(BF16),
                        v=v_ref[p["rows"], h * GLA_DV:(h + 1) * GLA_DV].astype(BF16)))
    return ops


def _head_mask(h2):
    lane = lax.broadcasted_iota(jnp.int32, (1, LANES), 1)
    return (lane < GLA_DK) if h2 == 0 else (lane >= GLA_DK)


def _gla_fwd(u, gup, gb, gn, bsz, nc, kc):
    r = u.shape[0]
    hv = GLA_HEADS * GLA_DV
    ns = nc // kc

    def body(qk_ref, v_ref, r_ref, gd_ref, gup_ref, gb_ref, gn_ref, go_ref, sta_ref, st_ref):
        t = pl.program_id(1)

        @pl.when(t == 0)
        def _():
            st_ref[...] = jnp.zeros_like(st_ref)

        ps, ri, ci = _gla_prep(qk_ref, gd_ref, gup_ref[...], gb_ref[...], t * kc, kc)
        tril = ri >= ci
        items = [(j, h) for j in range(kc) for h in range(GLA_HEADS)]
        ops = [_gla_heads(p, v_ref) for p in ps]
        a = {jh: jnp.where(tril, _mm_nt(ops[jh[0]][jh[1]]["qe"], ops[jh[0]][jh[1]]["ke"]), 0.0).astype(BF16) for jh in items}
        oi = {jh: _mm(a[jh], ops[jh[0]][jh[1]]["v"]) for jh in items}
        inc = {jh: _mm_tn(ops[jh[0]][jh[1]]["v"], ops[jh[0]][jh[1]]["kd"]) for jh in items}
        sts = [st_ref[h] for h in range(GLA_HEADS)]
        for j, h in items:
            op, p = ops[j][h], ps[j]
            st = sts[h]
            sta_ref[j, h] = st
            o = oi[j, h] + _mm_nt(op["qe"], st.astype(BF16))
            sts[h] = st * p["gam"][:, op["ls"]] + inc[j, h]
            rs = lax.rsqrt(jnp.mean(o * o, axis=-1, keepdims=True) + LN_EPS)
            rr = r_ref[p["rows"], op["vs"]]
            go_ref[p["rows"], op["vs"]] = (o * rs * gn_ref[...] * (rr * _sigmoid(rr))).astype(BF16)
        for h in range(GLA_HEADS):
            st_ref[h] = sts[h]

    rowblk = lambda col: (lambda b, t: (b * ns + t, col))
    const = lambda b, t: (0, 0)
    return pl.pallas_call(
        body, name="gla_fwd", grid=(bsz, ns),
        in_specs=[pl.BlockSpec((kc * CHUNK, QK_WIDTH), rowblk(U_QK)), pl.BlockSpec((kc * CHUNK, hv), rowblk(U_V)),
                  pl.BlockSpec((kc * CHUNK, hv), rowblk(U_R)), pl.BlockSpec((kc * CHUNK, LANES), rowblk(U_GD)),
                  pl.BlockSpec((LANES, GUP_WIDTH), const), pl.BlockSpec((1, GUP_WIDTH), const), pl.BlockSpec((1, GLA_DV), const)],
        out_specs=[pl.BlockSpec((kc * CHUNK, hv), rowblk(0)),
                   pl.BlockSpec((kc, GLA_HEADS, LANES, LANES), lambda b, t: (b * ns + t, 0, 0, 0))],
        out_shape=[_sds((r, hv), BF16), _sds((bsz * nc, GLA_HEADS, LANES, LANES), F32)],
        scratch_shapes=[pltpu.VMEM((GLA_HEADS, LANES, LANES), F32)],
        compiler_params=_params(("parallel", "arbitrary")),
    )(u, u, u, u, gup, gb, gn)


def _outproj_fwd(s0, co, go, w_out, g1, b1, tm):
    r, d = s0.shape
    dc = co.shape[1]

    def body(s0_ref, co_ref, go_ref, w_ref, g_ref, b_ref, p1_ref, s1_ref, s1b_ref):
        nb = 4 if tm % 64 == 0 else 1
        blocks = [slice(k * (tm // nb), (k + 1) * (tm // nb)) for k in range(nb)]
        mixes = [_mm(co_ref[rows, :], w_ref[0:dc, :]) + _mm(go_ref[rows, :], w_ref[dc:2 * dc, :]) for rows in blocks]
        for rows, mix in zip(blocks, mixes):
            p1 = ALPHA * s0_ref[rows, :] + mix
            p1_ref[rows, :] = p1
            xhat, _ = _ln(p1)
            s1 = xhat * g_ref[...] + b_ref[...]
            s1_ref[rows, :] = s1
            s1b_ref[rows, :] = s1.astype(BF16)

    row = lambda w: pl.BlockSpec((tm, w), lambda i: (i, 0))
    vec = pl.BlockSpec((1, d), lambda i: (0, 0))
    return pl.pallas_call(
        body, name="outproj_fwd", grid=(r // tm,),
        in_specs=[row(d), row(dc), row(dc), pl.BlockSpec((2 * dc, d), lambda i: (0, 0)), vec, vec],
        out_specs=[row(d), row(d), row(d)],
        out_shape=[_sds((r, d), F32), _sds((r, d), F32), _sds((r, d), BF16)],
        compiler_params=_params(("parallel",)),
    )(s0, co, go, w_out, g1, b1)


def _mlp_fwd_resident(s1, s1b, w1g, w2, g2, b2, tgt, tp, tm):
    r, d = s1.shape
    nh, _, th = w1g.shape

    def body(s1_ref, sb_ref, w1_ref, w2_ref, g_ref, b_ref, t_ref, hm_ref, dp2_ref, dpb_ref, loss_ref, dg_ref, db_ref):
        i = pl.program_id(0)

        @pl.when(i == 0)
        def _():
            loss_ref[...] = jnp.zeros_like(loss_ref)
            dg_ref[...] = jnp.zeros_like(dg_ref)
            db_ref[...] = jnp.zeros_like(db_ref)

        def mlp_rows(rows):
            x = sb_ref[rows, :]
            acc = None
            h_next = _mm(x, w1_ref[0])
            for s in range(nh):
                h = h_next
                if s + 1 < nh:
                    h_next = _mm(x, w1_ref[s + 1])
                hm_ref[rows, s * th:(s + 1) * th] = h.astype(BF16)
                act = jnp.square(jnp.maximum(h, 0.0))
                p = _mm(act.astype(BF16), w2_ref[s * th:(s + 1) * th, :])
                acc = p if acc is None else acc + p
            return acc

        isx = _row_in_seq(i, tm, tp) >= X_OFF
        tg = t_ref[...]
        tg = jnp.where(i == 0, pltpu.roll(tg, X_OFF, 0), tg)

        def finish(rows, acc):
            p2 = ALPHA * s1_ref[rows, :] + acc
            xhat, rstd = _ln(p2)
            s2 = xhat * g_ref[...] + b_ref[...]
            err = jnp.where(isx[rows], s2 - tg[rows], 0.0)
            loss_ref[...] += 0.5 * jnp.sum(jnp.mean(err * err, axis=-1, keepdims=True))
            dy = err * (1.0 / d)
            dg_ref[...] += _rowsum(dy * xhat)
            db_ref[...] += _rowsum(dy)
            dp2 = _ln_bwd(dy * g_ref[...], xhat, rstd)
            dp2_ref[rows, :] = dp2
            dpb_ref[rows, :] = dp2.astype(BF16)

        _row_blocks(tm, mlp_rows, finish)

    row = pl.BlockSpec((tm, d), lambda i: (i, 0))
    vec = pl.BlockSpec((1, d), lambda i: (0, 0))
    held = pl.Buffered(1)
    tgt_row = pl.BlockSpec((pl.Element(tm), pl.Element(d)),
                           lambda i: (pl.multiple_of(jnp.maximum(i * tm - X_OFF * ((i * tm) // tp + 1), 0), 8), 0))
    return pl.pallas_call(
        body, name="mlp_fwd", grid=(r // tm,),
        in_specs=[row, row, pl.BlockSpec((nh, d, th), lambda i: (0, 0, 0), pipeline_mode=held),
                  pl.BlockSpec((nh * th, d), lambda i: (0, 0), pipeline_mode=held), vec, vec, tgt_row],
        out_specs=[pl.BlockSpec((tm, nh * th), lambda i: (i, 0)), row, row,
                   pl.BlockSpec((8, LANES), lambda i: (0, 0)), vec, vec],
        out_shape=[_sds((r, nh * th), BF16), _sds((r, d), F32), _sds((r, d), BF16), _sds((8, LANES), F32),
                   _sds((1, d), F32), _sds((1, d), F32)],
        compiler_params=_params(("arbitrary",)),
    )(s1, s1b, w1g, w2, g2, b2, tgt)


def _mlp_fwd(s1, s1b, w1g, w2, g2, b2, tgt, tp, tm, ns):
    r, d = s1.shape
    nh, _, th = w1g.shape
    nj = nh // ns

    def body(s1_ref, sb_ref, w1_ref, w2_ref, g_ref, b_ref, t_ref, hm_ref, dp2_ref, dpb_ref, loss_ref, dg_ref, db_ref, acc_ref):
        i = pl.program_id(0)
        j = pl.program_id(1)

        @pl.when(jnp.logical_and(i == 0, j == 0))
        def _():
            loss_ref[...] = jnp.zeros_like(loss_ref)
            dg_ref[...] = jnp.zeros_like(dg_ref)
            db_ref[...] = jnp.zeros_like(db_ref)

        @pl.when(j == 0)
        def _():
            acc_ref[...] = jnp.zeros_like(acc_ref)

        def mlp_rows(rows):
            hs = [_mm(sb_ref[rows, :], w1_ref[s]) for s in range(ns)]
            acc = acc_ref[rows, :]
            for s in range(ns):
                hm_ref[rows, s * th:(s + 1) * th] = hs[s].astype(BF16)
                act = jnp.square(jnp.maximum(hs[s], 0.0))
                acc = acc + _mm(act.astype(BF16), w2_ref[s * th:(s + 1) * th, :])
            return acc

        @pl.when(j < nj - 1)
        def _():
            acc_ref[...] = mlp_rows(slice(None))

        @pl.when(j == nj - 1)
        def _():
            isx = _row_in_seq(i, tm, tp) >= X_OFF
            tg = t_ref[...]
            tg = jnp.where(i == 0, pltpu.roll(tg, X_OFF, 0), tg)

            def finish(rows, acc):
                p2 = ALPHA * s1_ref[rows, :] + acc
                xhat, rstd = _ln(p2)
                s2 = xhat * g_ref[...] + b_ref[...]
                err = jnp.where(isx[rows], s2 - tg[rows], 0.0)
                loss_ref[...] += 0.5 * jnp.sum(jnp.mean(err * err, axis=-1, keepdims=True))
                dy = err * (1.0 / d)
                dg_ref[...] += _rowsum(dy * xhat)
                db_ref[...] += _rowsum(dy)
                dp2 = _ln_bwd(dy * g_ref[...], xhat, rstd)
                dp2_ref[rows, :] = dp2
                dpb_ref[rows, :] = dp2.astype(BF16)

            _row_blocks(tm, mlp_rows, finish)

    row = pl.BlockSpec((tm, d), lambda i, j: (i, 0))
    vec = pl.BlockSpec((1, d), lambda i, j: (0, 0))
    tgt_row = pl.BlockSpec((pl.Element(tm), pl.Element(d)),
                           lambda i, j: (pl.multiple_of(jnp.maximum(i * tm - X_OFF * ((i * tm) // tp + 1), 0), CHUNK), 0))
    return pl.pallas_call(
        body, name="mlp_fwd", grid=(r // tm, nj),
        in_specs=[row, row, pl.BlockSpec((ns, d, th), lambda i, j: (j, 0, 0)), pl.BlockSpec((ns * th, d), lambda i, j: (j, 0)),
                  vec, vec, tgt_row],
        out_specs=[pl.BlockSpec((tm, ns * th), lambda i, j: (i, j)), row, row,
                   pl.BlockSpec((8, LANES), lambda i, j: (0, 0)), vec, vec],
        out_shape=[_sds((r, nh * th), BF16), _sds((r, d), F32), _sds((r, d), BF16), _sds((8, LANES), F32),
                   _sds((1, d), F32), _sds((1, d), F32)],
        scratch_shapes=[pltpu.VMEM((tm, d), F32)],
        compiler_params=_params(("arbitrary", "arbitrary")),
    )(s1, s1b, w1g, w2, g2, b2, tgt)


def _mlp_bwd_act(dp2, dpb, hm, w1g, w2, p1, g1, tm, ns):
    r, d = dp2.shape
    nh, _, th = w1g.shape
    nj = nh // ns

    def body(dp2_ref, dpb_ref, hm_ref, w1_ref, w2_ref, p1_ref, g_ref, dh_ref, dp1_ref, dg_ref, db_ref, acc_ref):
        i = pl.program_id(0)
        j = pl.program_id(1)

        @pl.when(jnp.logical_and(i == 0, j == 0))
        def _():
            dg_ref[...] = jnp.zeros_like(dg_ref)
            db_ref[...] = jnp.zeros_like(db_ref)

        @pl.when(j == 0)
        def _():
            acc_ref[...] = jnp.zeros_like(acc_ref)

        def mlp_rows(rows):
            dacts = [_mm_nt(dpb_ref[rows, :], w2_ref[s * th:(s + 1) * th, :]) for s in range(ns)]
            acc = acc_ref[rows, :]
            for s in range(ns):
                cols = slice(s * th, (s + 1) * th)
                dh = (dacts[s] * (2.0 * jnp.maximum(hm_ref[rows, cols].astype(F32), 0.0))).astype(BF16)
                dh_ref[rows, cols] = dh
                acc = acc + _mm_nt(dh, w1_ref[s])
            return acc

        @pl.when(j < nj - 1)
        def _():
            acc_ref[...] = mlp_rows(slice(None))

        @pl.when(j == nj - 1)
        def _():
            def finish(rows, acc):
                ds1 = ALPHA * dp2_ref[rows, :] + acc
                xhat, rstd = _ln(p1_ref[rows, :])
                dg_ref[...] += _rowsum(ds1 * xhat)
                db_ref[...] += _rowsum(ds1)
                dp1_ref[rows, :] = _ln_bwd(ds1 * g_ref[...], xhat, rstd)

            _row_blocks(tm, mlp_rows, finish)

    row = pl.BlockSpec((tm, d), lambda i, j: (i, 0))
    vec = pl.BlockSpec((1, d), lambda i, j: (0, 0))
    blk = pl.BlockSpec((tm, ns * th), lambda i, j: (i, j))
    return pl.pallas_call(
        body, name="mlp_bwd_act", grid=(r // tm, nj),
        in_specs=[row, row, blk, pl.BlockSpec((ns, d, th), lambda i, j: (j, 0, 0)),
                  pl.BlockSpec((ns * th, d), lambda i, j: (j, 0)), row, vec],
        out_specs=[blk, row, vec, vec],
        out_shape=[_sds((r, nh * th), BF16), _sds((r, d), F32), _sds((1, d), F32), _sds((1, d), F32)],
        scratch_shapes=[pltpu.VMEM((tm, d), F32)],
        compiler_params=_params(("arbitrary", "arbitrary")),
    )(dp2, dpb, hm, w1g, w2, p1, g1)


def _mlp_bwd_w(s1b, hm, dh, dpb, nh, tm, ns):
    r, d = s1b.shape
    th = hm.shape[1] // nh

    def body(s1_ref, hm_ref, dh_ref, dp2_ref, dw1_ref, dw2_ref, a1_ref, a2_ref):
        i = pl.program_id(1)

        @pl.when(i == 0)
        def _():
            a1_ref[...] = jnp.zeros_like(a1_ref)
            a2_ref[...] = jnp.zeros_like(a2_ref)

        for s in range(ns):
            a1_ref[s] += _mm_tn(s1_ref[...], dh_ref[:, s * th:(s + 1) * th])
        for s in range(ns):
            act = jnp.square(jnp.maximum(hm_ref[:, s * th:(s + 1) * th].astype(F32), 0.0)).astype(BF16)
            a2_ref[s] += _mm_tn(act, dp2_ref[...])

        @pl.when(i == pl.num_programs(1) - 1)
        def _():
            dw1_ref[...] = a1_ref[...].astype(BF16)
            dw2_ref[...] = a2_ref[...].astype(BF16)

    row = pl.BlockSpec((tm, d), lambda j, i: (i, 0))
    blk = pl.BlockSpec((tm, ns * th), lambda j, i: (i, j))
    return pl.pallas_call(
        body, name="mlp_bwd_w", grid=(nh // ns, r // tm),
        in_specs=[row, blk, blk, row],
        out_specs=[pl.BlockSpec((ns, d, th), lambda j, i: (j, 0, 0)), pl.BlockSpec((ns, th, d), lambda j, i: (j, 0, 0))],
        out_shape=[_sds((nh, d, th), BF16), _sds((nh, th, d), BF16)],
        scratch_shapes=[pltpu.VMEM((ns, d, th), F32), pltpu.VMEM((ns, th, d), F32)],
        compiler_params=_params(("parallel", "arbitrary")),
    )(s1b, hm, dh, dpb)


def _outproj_bwd(dp1, co, go, w_out, tm):
    r, d = dp1.shape
    dc = co.shape[1]

    def body(dp_ref, co_ref, go_ref, w_ref, dmi_ref, dw_ref, acc_ref):
        i = pl.program_id(0)

        @pl.when(i == 0)
        def _():
            acc_ref[...] = jnp.zeros_like(acc_ref)

        dpb = dp_ref[...].astype(BF16)
        dmi_ref[...] = _mm_nt(dpb, w_ref[...])
        acc_ref[0:dc, :] += _mm_tn(co_ref[...], dpb)
        acc_ref[dc:2 * dc, :] += _mm_tn(go_ref[...], dpb)

        @pl.when(i == pl.num_programs(0) - 1)
        def _():
            dw_ref[...] = acc_ref[...].astype(BF16)

    row = lambda w: pl.BlockSpec((tm, w), lambda i: (i, 0))
    full = pl.BlockSpec((2 * dc, d), lambda i: (0, 0))
    return pl.pallas_call(
        body, name="outproj_bwd", grid=(r // tm,),
        in_specs=[row(d), row(dc), row(dc), full],
        out_specs=[row(2 * dc), full],
        out_shape=[_sds((r, 2 * dc), F32), _sds((2 * dc, d), BF16)],
        scratch_shapes=[pltpu.VMEM((2 * dc, d), F32)],
        compiler_params=_params(("arbitrary",)),
    )(dp1, co, go, w_out)


def _gla_bwd(u, dmi, sta, gup, gb, gn, bsz, nc, kc):
    r = u.shape[0]
    hv = GLA_HEADS * GLA_DV
    hk = GLA_HEADS * GLA_DK
    ns = nc // kc

    def body(qk_ref, v_ref, r_ref, gd_ref, dgo_ref, sta_ref, gup_ref, gb_ref, gn_ref,
             dqk_ref, dv_ref, dr_ref, dgd_ref, dgn_ref, dgb_ref, dgup_ref, dst_ref):
        bi = pl.program_id(0)
        t = pl.program_id(1)

        @pl.when(jnp.logical_and(bi == 0, t == 0))
        def _():
            dgn_ref[...] = jnp.zeros_like(dgn_ref)
            dgb_ref[...] = jnp.zeros_like(dgb_ref)
            dgup_ref[...] = jnp.zeros_like(dgup_ref)

        @pl.when(t == 0)
        def _():
            dst_ref[...] = jnp.zeros_like(dst_ref)

        ps, ri, ci = _gla_prep(qk_ref, gd_ref, gup_ref[...], gb_ref[...], (ns - 1 - t) * kc, kc)
        tril = ri >= ci
        items = [(j, h) for j in reversed(range(kc)) for h in range(GLA_HEADS)]
        ops = [_gla_heads(p, v_ref) for p in ps]
        op = lambda jh: ops[jh[0]][jh[1]]
        st = {jh: sta_ref[jh[0], jh[1]] for jh in items}
        stb = {jh: st[jh].astype(BF16) for jh in items}
        a = {jh: jnp.where(tril, _mm_nt(op(jh)["qe"], op(jh)["ke"]), 0.0).astype(BF16) for jh in items}
        o1 = {jh: _mm(a[jh], op(jh)["v"]) for jh in items}
        o2 = {jh: _mm_nt(op(jh)["qe"], stb[jh]) for jh in items}
        dob = {}
        dgn = jnp.zeros((1, GLA_DV), F32)
        for jh in items:
            rows, vs = ps[jh[0]]["rows"], op(jh)["vs"]
            o = o1[jh] + o2[jh]
            rr = r_ref[rows, vs]
            sr = _sigmoid(rr)
            rs = lax.rsqrt(jnp.mean(o * o, axis=-1, keepdims=True) + LN_EPS)
            y = o * rs
            dgo = dgo_ref[rows, vs]
            don = dgo * (rr * sr)
            dr_ref[rows, vs] = (dgo * (y * gn_ref[...]) * (sr * (1.0 + rr * (1.0 - sr)))).astype(BF16)
            dgn = dgn + _rowsum(don * y)
            dxn = don * gn_ref[...]
            dob[jh] = (rs * (dxn - y * jnp.mean(dxn * y, axis=-1, keepdims=True))).astype(BF16)
        da = {jh: jnp.where(tril, _mm_nt(dob[jh], op(jh)["v"]), 0.0).astype(BF16) for jh in items}
        dv1 = {jh: _mm_tn(a[jh], dob[jh]) for jh in items}
        dqe1 = {jh: _mm(da[jh], op(jh)["ke"]) for jh in items}
        dqe2 = {jh: _mm(dob[jh], stb[jh]) for jh in items}
        dke1 = {jh: _mm_tn(da[jh], op(jh)["qe"]) for jh in items}
        inc = {jh: _mm_tn(dob[jh], op(jh)["qe"]) for jh in items}
        dsts = [dst_ref[h] for h in range(GLA_HEADS)]
        dkd1, dgam1 = {}, {}
        for jh in items:
            j, h = jh
            dst = dsts[h]
            dstb = dst.astype(BF16)
            dv_ref[ps[j]["rows"], op(jh)["vs"]] = (dv1[jh] + _mm_nt(op(jh)["kd"], dstb)).astype(BF16)
            dkd1[jh] = _mm(op(jh)["v"], dstb)
            dgam1[jh] = _rowsum(dst * st[jh])
            dsts[h] = dst * ps[j]["gam"][:, op(jh)["ls"]] + inc[jh]
        for h in range(GLA_HEADS):
            dst_ref[h] = dsts[h]
        upper = (ri <= ci).astype(BF16)
        dbs, dbls = [], []
        for j in range(kc):
            p = ps[j]
            tiles = [[op((j, 2 * hp + h2)) for h2 in range(2)] for hp in range(GLA_HEADS // 2)]
            head = lambda d, hp, h2: d[j, 2 * hp + h2]
            lanes = lambda f: jnp.concatenate([f(hp) for hp in range(GLA_HEADS // 2)], axis=1)
            dqe = lanes(lambda hp: sum(jnp.where(tiles[hp][h2]["m"], head(dqe1, hp, h2) + head(dqe2, hp, h2), 0.0)
                                       for h2 in range(2)))
            dke = lanes(lambda hp: head(dke1, hp, 0) + head(dke1, hp, 1))
            dkd = lanes(lambda hp: sum(jnp.where(tiles[hp][h2]["m"], head(dkd1, hp, h2), 0.0) for h2 in range(2)))
            dgam = lanes(lambda hp: head(dgam1, hp, 0) + head(dgam1, hp, 1))
            dqk_ref[p["rows"], :hk] = (dqe * p["eb"] * QK_SCALE).astype(BF16)
            dqk_ref[p["rows"], hk:] = (dke * p["enb"] + dkd * p["ebl"]).astype(BF16)
            dkdkd = dkd * p["kd"]
            dbs.append(dqe * p["qe"] - dke * p["ke"] - dkdkd)
            dbls.append(_rowsum(dkdkd) + dgam * p["gam"])
        dlgs = _tri_mm_all(upper, dbs)
        dzb = []
        dgb = jnp.zeros((1, hk), F32)
        for j in range(kc):
            p = ps[j]
            dz = jnp.where(p["real"], (dlgs[j] + dbls[j]) * (1.0 / GLA_TAU) * _sigmoid(-p["z"]), 0.0)
            dgb = dgb + _rowsum(dz)
            dzb.append(dz.astype(BF16))
        dgup = sum(_mm_tn(ps[j]["gd"].astype(BF16), dzb[j]) for j in range(kc))
        for j in range(kc):
            dgd_ref[ps[j]["rows"], :] = _mm_nt(dzb[j], gup_ref[...]).astype(BF16)
        dgb_ref[...] += dgb
        dgup_ref[...] += dgup
        dgn_ref[...] += dgn

    rowblk = lambda col: (lambda b, t: (b * ns + ns - 1 - t, col))
    const = lambda b, t: (0, 0)
    return pl.pallas_call(
        body, name="gla_bwd", grid=(bsz, ns),
        in_specs=[pl.BlockSpec((kc * CHUNK, QK_WIDTH), rowblk(U_QK)), pl.BlockSpec((kc * CHUNK, hv), rowblk(U_V)),
                  pl.BlockSpec((kc * CHUNK, hv), rowblk(U_R)), pl.BlockSpec((kc * CHUNK, LANES), rowblk(U_GD)),
                  pl.BlockSpec((kc * CHUNK, hv), rowblk(DMI_GLA)),
                  pl.BlockSpec((kc, GLA_HEADS, LANES, LANES), lambda b, t: (b * ns + ns - 1 - t, 0, 0, 0)),
                  pl.BlockSpec((LANES, GUP_WIDTH), const), pl.BlockSpec((1, GUP_WIDTH), const), pl.BlockSpec((1, GLA_DV), const)],
        out_specs=[pl.BlockSpec((kc * CHUNK, 2 * hk), rowblk(0)), pl.BlockSpec((kc * CHUNK, hv), rowblk(0)),
                   pl.BlockSpec((kc * CHUNK, hv), rowblk(0)), pl.BlockSpec((kc * CHUNK, LANES), rowblk(0)),
                   pl.BlockSpec((1, GLA_DV), const), pl.BlockSpec((1, GUP_WIDTH), const),
                   pl.BlockSpec((LANES, GUP_WIDTH), const)],
        out_shape=[_sds((r, 2 * hk), BF16), _sds((r, hv), BF16), _sds((r, hv), BF16), _sds((r, LANES), BF16),
                   _sds((1, GLA_DV), F32), _sds((1, GUP_WIDTH), F32), _sds((LANES, GUP_WIDTH), F32)],
        scratch_shapes=[pltpu.VMEM((GLA_HEADS, LANES, LANES), F32)],
        compiler_params=_params(("arbitrary", "arbitrary")),
    )(u, u, u, u, dmi, sta, gup, gb, gn)


def _conv_bwd(u, c, dmi, w32, cg, cbe, tp, tc, dc):
    r = u.shape[0]
    hb = tc // CONV_HALO
    nhalo = r // CONV_HALO

    def dconv(cv, dco, cg_ref, cbe_ref):
        xhat, rstd = _ln(cv)
        cn = xhat * cg_ref[...] + cbe_ref[...]
        sg = _sigmoid(cn)
        dcn = dco * (sg * (1.0 + cn * (1.0 - sg)))
        return _ln_bwd(dcn * cg_ref[...], xhat, rstd), dcn, xhat

    def body(a_ref, g_ref, ah_ref, gh_ref, c_ref, dco_ref, ch_ref, dcoh_ref, w_ref, cg_ref, cbe_ref,
             du_ref, dw_ref, dcb_ref, dcg_ref, dcbe_ref, hs_ref, dcs_ref, dw8_ref):
        t = pl.program_id(0)

        @pl.when(t == 0)
        def _():
            dw8_ref[...] = jnp.zeros_like(dw8_ref)
            dcb_ref[...] = jnp.zeros_like(dcb_ref)
            dcg_ref[...] = jnp.zeros_like(dcg_ref)
            dcbe_ref[...] = jnp.zeros_like(dcbe_ref)

        first = lax.rem(t * tc, tp) == 0
        last = lax.rem((t + 1) * tc, tp) == 0
        hh = ah_ref[...] * _sigmoid(gh_ref[...])
        hs_ref[0:CONV_HALO, :] = jnp.where(first, 0.0, hh)
        hs_ref[CONV_HALO:CONV_HALO + tc, :] = a_ref[...] * _sigmoid(g_ref[...])
        dch, _, _ = dconv(ch_ref[...], dcoh_ref[...], cg_ref, cbe_ref)
        dcs_ref[tc:tc + CONV_HALO, :] = jnp.where(last, 0.0, dch)

        lrows = tc // 4

        def sub1(k, carry):
            r0 = pl.multiple_of(k * lrows, 8)
            dcv, dcn, xhat = dconv(c_ref[pl.ds(r0, lrows), :], dco_ref[pl.ds(r0, lrows), :], cg_ref, cbe_ref)
            dcs_ref[pl.ds(r0, lrows), :] = dcv
            dcb_ref[...] += _rowsum(dcv)
            dcg_ref[...] += _rowsum(dcn * xhat)
            dcbe_ref[...] += _rowsum(dcn)
            return carry

        lax.fori_loop(0, 4, sub1, 0)

        def sub2(k, carry):
            r0 = pl.multiple_of(k * CONV_SUB, CONV_SUB)
            dwin = dcs_ref[pl.ds(r0, CONV_WIN), :]
            dh = _conv_taps(dwin, lambda o: w_ref[CONV_WIDTH - 1 - o:CONV_WIDTH - o, :], 0)
            av = a_ref[pl.ds(r0, CONV_SUB), :]
            sg = _sigmoid(g_ref[pl.ds(r0, CONV_SUB), :])
            du_ref[pl.ds(r0, CONV_SUB), 0:dc] = (dh * sg).astype(BF16)
            du_ref[pl.ds(r0, CONV_SUB), dc:2 * dc] = (dh * av * sg * (1.0 - sg)).astype(BF16)
            hwin = hs_ref[pl.ds(r0, CONV_WIN), :]
            dcv = dwin[0:CONV_SUB, :]
            for rho in range(8):
                offs = [o for o in range(2, 2 + CONV_WIDTH) if o % 8 == rho]
                rolled = hwin if rho == 0 else pltpu.roll(hwin, CONV_WIN - rho, 0)
                for o in offs:
                    m8 = o - rho
                    prod = dcv * rolled[m8:m8 + CONV_SUB, :]
                    dw8_ref[8 * (o - 2):8 * (o - 1), :] += jnp.sum(prod.reshape(CONV_SUB // 8, 8, dc), axis=0)
            return carry

        lax.fori_loop(0, tc // CONV_SUB, sub2, 0)

        @pl.when(t == pl.num_programs(0) - 1)
        def _():
            dw_ref[...] = jnp.zeros_like(dw_ref)
            for j in range(CONV_WIDTH):
                dw_ref[j:j + 1, :] = _rowsum(dw8_ref[8 * j:8 * (j + 1), :])

    vec = pl.BlockSpec((1, dc), lambda t: (0, 0))
    prev = lambda col: (lambda t: (jnp.maximum(t * hb - 1, 0), col))
    nxt = lambda col: (lambda t: (jnp.minimum((t + 1) * hb, nhalo - 1), col))
    return pl.pallas_call(
        body, name="conv_bwd", grid=(r // tc,),
        in_specs=[pl.BlockSpec((tc, dc), lambda t: (t, 0)), pl.BlockSpec((tc, dc), lambda t: (t, 1)),
                  pl.BlockSpec((CONV_HALO, dc), prev(0)), pl.BlockSpec((CONV_HALO, dc), prev(1)),
                  pl.BlockSpec((tc, dc), lambda t: (t, 0)), pl.BlockSpec((tc, dc), lambda t: (t, 0)),
                  pl.BlockSpec((CONV_HALO, dc), nxt(0)), pl.BlockSpec((CONV_HALO, dc), nxt(0)),
                  pl.BlockSpec((32, dc), lambda t: (0, 0)), vec, vec],
        out_specs=[pl.BlockSpec((tc, 2 * dc), lambda t: (t, 0)), pl.BlockSpec((32, dc), lambda t: (0, 0)), vec, vec, vec],
        out_shape=[_sds((r, 2 * dc), BF16), _sds((32, dc), F32), _sds((1, dc), F32), _sds((1, dc), F32), _sds((1, dc), F32)],
        scratch_shapes=[pltpu.VMEM((CONV_HALO + tc, dc), F32), pltpu.VMEM((tc + CONV_HALO, dc), F32),
                        pltpu.VMEM((8 * 32, dc), F32)],
        compiler_params=_params(("arbitrary",)),
    )(u, u, u, u, c, dmi, c, dmi, w32, cg, cbe)


def _inproj_bwd(dp1, dus, xsrc, g_in, w_in, tp, seq, tx):
    r, d = dp1.shape
    widths = [x.shape[1] for x in dus]
    offs = [sum(widths[:k]) for k in range(len(widths))]
    n = w_in.shape[0]
    nd = len(dus)
    head = tx == 0
    rows = X_OFF if head else tx

    def body(*refs):
        dp_ref = refs[0]
        du_refs = refs[1:1 + nd]
        x_ref, g_ref, w_ref, out_ref, dg_ref, db_ref = refs[1 + nd:]
        i = pl.program_id(0)

        @pl.when(i == 0)
        def _():
            dg_ref[...] = jnp.zeros_like(dg_ref)
            db_ref[...] = jnp.zeros_like(db_ref)
            if head:
                out_ref[...] = jnp.zeros_like(out_ref)

        ds0 = ALPHA * dp_ref[...]
        for k in range(nd):
            ds0 = ds0 + _mm(du_refs[k][...], w_ref[offs[k]:offs[k] + widths[k], :])
        if head:
            ds0 = jnp.where(lax.broadcasted_iota(jnp.int32, (X_OFF, 1), 0) >= PAD_FRONT, ds0, 0.0)
        xhat, rstd = _ln(x_ref[...])
        dg_ref[...] += _rowsum(ds0 * xhat)
        db_ref[...] += _rowsum(ds0)
        dx = _ln_bwd(ds0 * g_ref[...], xhat, rstd)
        if head:
            out_ref[...] += dx[PAD_FRONT:X_OFF, :]
        else:
            out_ref[...] = dx

    if head:
        nb = tp // X_OFF
        row = lambda w: pl.BlockSpec((X_OFF, w), lambda i: (i * nb, 0))
        xspec = pl.BlockSpec((X_OFF, d), lambda i: (0, 0))
        ospec, oshape, steps = pl.BlockSpec((N_META, d), lambda i: (0, 0)), _sds((N_META, d), F32), r // tp
    else:
        start = _x_tile_row(tp, seq, tx)
        row = lambda w: pl.BlockSpec((pl.Element(tx), pl.Element(w)), lambda i: (start(i), 0))
        xspec = pl.BlockSpec((tx, d), lambda i: (i, 0))
        ospec, oshape, steps = xspec, _sds(xsrc.shape, F32), xsrc.shape[0] // tx
    vec = pl.BlockSpec((1, d), lambda i: (0, 0))
    return pl.pallas_call(
        body, name="inproj_bwd_head" if head else "inproj_bwd_x", grid=(steps,),
        in_specs=[row(d)] + [row(w) for w in widths] + [xspec, vec, pl.BlockSpec((n, d), lambda i: (0, 0))],
        out_specs=[ospec, vec, vec],
        out_shape=[oshape, _sds((1, d), F32), _sds((1, d), F32)],
        compiler_params=_params(("arbitrary",)),
    )(dp1, *dus, xsrc, g_in, w_in)


def _inproj_bwd_w(s0, dus, tm):
    r, d = s0.shape
    widths = [x.shape[1] for x in dus]
    offs = [sum(widths[:k]) for k in range(len(widths))]
    nd = len(dus)

    def body(*refs):
        s_ref = refs[0]
        du_refs = refs[1:1 + nd]
        dw_ref, acc_ref = refs[1 + nd:]
        i = pl.program_id(0)

        @pl.when(i == 0)
        def _():
            acc_ref[...] = jnp.zeros_like(acc_ref)

        for k in range(nd):
            acc_ref[offs[k]:offs[k] + widths[k], :] += _mm_tn(du_refs[k][...], s_ref[...])

        @pl.when(i == pl.num_programs(0) - 1)
        def _():
            dw_ref[...] = acc_ref[...].astype(BF16)

    row = lambda w: pl.BlockSpec((tm, w), lambda i: (i, 0))
    return pl.pallas_call(
        body, name="inproj_bwd_w", grid=(r // tm,),
        in_specs=[row(d)] + [row(w) for w in widths],
        out_specs=pl.BlockSpec((sum(widths), d), lambda i: (0, 0)),
        out_shape=_sds((sum(widths), d), BF16),
        scratch_shapes=[pltpu.VMEM((sum(widths), d), F32)],
        compiler_params=_params(("arbitrary",)),
    )(s0, *dus)


def _local_step(x, tgt, meta, ln_in_g, ln_in_b, w_in, conv_w, conv_b, conv_ln_g, conv_ln_b, gate_up, gate_bias,
                gla_norm_g, late_weights, ln1_g, ln1_b, ln2_g, ln2_b, push):
    bsz, seq, d = x.shape
    tp = X_OFF + seq
    assert tp % CHUNK == 0
    nc = tp // CHUNK
    dc = conv_b.shape[1]
    tmm = tc = _pick_tile(tp, (704, 128, 64))
    tx = _pick_tile(seq, (512, 64))
    kc = _pick_tile(nc, (11, 3, 2, 1))
    ns = 2

    x2 = x.reshape(bsz * seq, d)
    head = jnp.pad(meta, ((PAD_FRONT, 0), (0, 0)))
    tgt_p = tgt.reshape(bsz * seq, d)
    w32 = jnp.pad(conv_w, ((0, 32 - CONV_WIDTH), (0, 0)))
    gup = jnp.pad(gate_up, ((0, LANES - GLA_RANK), (0, 0))).astype(BF16)

    s0, s0b = _ln_in_x(x2, ln_in_g, ln_in_b, tp, seq, tx)
    s0, s0b = _ln_in_head(head, ln_in_g, ln_in_b, s0, s0b, tp)
    u = _inproj_fwd(s0b, w_in, tmm)
    c, co = _conv_fwd(u, w32, conv_b, conv_ln_g, conv_ln_b, tp, tc, dc)
    go, sta = _gla_fwd(u, gup, gate_bias, gla_norm_g, bsz, nc, kc)
    w_out, w1g, w2 = late_weights
    nh = w1g.shape[0]
    p1, s1, s1b = _outproj_fwd(s0, co, go, w_out, ln1_g, ln1_b, tmm)
    hm, dp2, dpb, loss, dg2, db2 = _mlp_fwd_resident(s1, s1b, w1g, w2, ln2_g, ln2_b, tgt_p, tp,
                                                     _pick_tile(tp, (352, 128, 64)))

    dh, dp1, dg1, db1 = _mlp_bwd_act(dp2, dpb, hm, w1g, w2, p1, ln1_g, tmm, ns)
    dw1, dw2 = _mlp_bwd_w(s1b, hm, dh, dpb, nh, tmm, ns)
    push("ff", (dw1, dw2))
    dmi, dwo = _outproj_bwd(dp1, co, go, w_out, tmm)
    push("out", (dwo,))
    dqk, dv, dr, dgd, dgn, dgb, dgup = _gla_bwd(u, dmi, sta, gup, gate_bias, gla_norm_g, bsz, nc, kc)
    dcv, dcw, dcb, dcg, dcbe = _conv_bwd(u, c, dmi, w32, conv_ln_g, conv_ln_b, tp, tc, dc)
    dus = [dcv, dqk, dv, dr, dgd]
    dwi = _inproj_bwd_w(s0b, dus, tmm)
    push("in", (dwi,))
    gx, dgx, dbx = _inproj_bwd(dp1, dus, x2, ln_in_g, w_in, tp, seq, tx)
    dmeta, dgh, dbh = _inproj_bwd(dp1, dus, head, ln_in_g, w_in, tp, seq, 0)

    return dict(loss=loss[0, 0], grad_x=gx.reshape(bsz, seq, d), meta_tokens=dmeta, ln_in_g=dgx + dgh, ln_in_b=dbx + dbh,
                conv_w=dcw[:CONV_WIDTH], conv_b=dcb, conv_ln_g=dcg, conv_ln_b=dcbe,
                gate_up=dgup[:GLA_RANK], gate_bias=dgb, gla_norm_g=dgn, ln1_g=dg1, ln1_b=db1, ln2_g=dg2, ln2_b=db2)


def _exchange(arrays, scatter, name):
    na = len(arrays)
    npeer = N_DEV - 1

    def body(*refs):
        srcs = refs[:na]
        outs = refs[na:2 * na]
        send_sems, recv_sems, local_sems = refs[2 * na:]
        xi, yi, ci = (lax.axis_index(a) for a in MESH_AXES)
        me = 4 * xi + 2 * yi + ci
        copies = []
        for a in range(na):
            own = srcs[a].at[me] if scatter[a] else srcs[a]
            cp = pltpu.make_async_copy(own, outs[a].at[me], local_sems.at[a])
            cp.start()
            copies.append(cp)
        remote = []
        for k in range(1, N_DEV):
            px, py, pc = xi ^ (k >> 2), yi ^ ((k >> 1) & 1), ci ^ (k & 1)
            peer = 4 * px + 2 * py + pc
            for a in range(na):
                src = srcs[a].at[peer] if scatter[a] else srcs[a]
                cp = pltpu.make_async_remote_copy(
                    src_ref=src, dst_ref=outs[a].at[me],
                    send_sem=send_sems.at[a * npeer + k - 1], recv_sem=recv_sems.at[a * npeer + k - 1],
                    device_id=(px, py, pc), device_id_type=pl.DeviceIdType.MESH)
                cp.start()
                remote.append(cp)
        for cp in remote:
            cp.wait()
        for cp in copies:
            cp.wait()

    out_shape = [_sds(a.shape if scatter[i] else (N_DEV,) + a.shape, a.dtype) for i, a in enumerate(arrays)]
    anyspec = pl.BlockSpec(memory_space=pl.ANY)
    return pl.pallas_call(
        body, name=name,
        in_specs=[anyspec] * na, out_specs=[anyspec] * na, out_shape=out_shape,
        scratch_shapes=[pltpu.SemaphoreType.DMA((na * npeer,)), pltpu.SemaphoreType.DMA((na * npeer,)),
                        pltpu.SemaphoreType.DMA((na,))],
    )(*arrays)


def _peers(xi, yi, ci):
    for k in range(1, N_DEV):
        px, py, pc = xi ^ (k >> 2), yi ^ ((k >> 1) & 1), ci ^ (k & 1)
        yield (px, py, pc), 4 * px + 2 * py + pc


def _sc_exchange(arrays, scatter, name, collective_id, after=None):
    na = len(arrays)
    npeer = N_DEV - 1
    ndep = 0 if after is None else 1

    def body(*refs):
        srcs = refs[:na]
        outs = refs[na + ndep:2 * na + ndep]
        send_sems, recv_sems, own_sems = refs[2 * na + ndep:]
        xi, yi, ci = (lax.axis_index(a) for a in MESH_AXES)
        me = 4 * xi + 2 * yi + ci
        barrier = pltpu.get_barrier_semaphore()
        for pos, _ in _peers(xi, yi, ci):
            pl.semaphore_signal(barrier, inc=1, device_id=pos, device_id_type=pl.DeviceIdType.MESH)
        pl.semaphore_wait(barrier, npeer)
        own = [pltpu.make_async_copy(srcs[a].at[me] if scatter[a] else srcs[a], outs[a].at[me], own_sems.at[a])
               for a in range(na)]
        for cp in own:
            cp.start()
        remote = []
        for a in range(na):
            for k, (pos, peer) in enumerate(_peers(xi, yi, ci)):
                cp = pltpu.make_async_remote_copy(
                    src_ref=srcs[a].at[peer] if scatter[a] else srcs[a], dst_ref=outs[a].at[me],
                    send_sem=send_sems.at[a * npeer + k], recv_sem=recv_sems.at[a * npeer + k],
                    device_id=pos, device_id_type=pl.DeviceIdType.MESH)
                cp.start()
                remote.append(cp)
        for cp in own:
            cp.wait()
        for cp in remote:
            cp.wait()

    out_type = [_sds(a.shape if scatter[i] else (N_DEV,) + a.shape, a.dtype) for i, a in enumerate(arrays)]
    sent = sum(a.size * a.dtype.itemsize // (N_DEV if scatter[i] else 1) for i, a in enumerate(arrays))
    return pl.kernel(
        body, out_type=out_type, mesh=plsc.ScalarSubcoreMesh(axis_name="seq", num_cores=1), name=name,
        scratch_types=[pltpu.SemaphoreType.DMA((na * npeer,)), pltpu.SemaphoreType.DMA((na * npeer,)),
                       pltpu.SemaphoreType.DMA((na,))],
        compiler_params=pltpu.CompilerParams(collective_id=collective_id),
        cost_estimate=pl.CostEstimate(flops=0, transcendentals=0, bytes_accessed=2 * N_DEV * sent,
                                      remote_bytes_transferred=npeer * sent),
    )(*arrays, *([] if after is None else [after]))


def _sc_gather(arrays, name, collective_id, after=None):
    na = len(arrays)
    ndep = 0 if after is None else 1
    npair = N_DEV - 1

    def body(*refs):
        srcs = refs[:na]
        outs = refs[na + ndep:2 * na + ndep]
        send_sems, recv_sems, own_sems = refs[2 * na + ndep:]
        xi, yi, ci = (lax.axis_index(a) for a in MESH_AXES)
        me = 4 * xi + 2 * yi + ci
        sibling = (xi, yi, 1 - ci)
        chips = [(1 - xi, yi), (xi, 1 - yi), (1 - xi, 1 - yi)]
        barrier = pltpu.get_barrier_semaphore()
        for pos, _ in _peers(xi, yi, ci):
            pl.semaphore_signal(barrier, inc=1, device_id=pos, device_id_type=pl.DeviceIdType.MESH)
        pl.semaphore_wait(barrier, npair)

        def copy(a, k, src, slot, to):
            return pltpu.make_async_remote_copy(
                src_ref=src, dst_ref=outs[a].at[slot], send_sem=send_sems.at[a * npair + k],
                recv_sem=recv_sems.at[a * npair + k], device_id=to, device_id_type=pl.DeviceIdType.MESH)

        own = [pltpu.make_async_copy(srcs[a], outs[a].at[me], own_sems.at[a]) for a in range(na)]
        for cp in own:
            cp.start()
        sent = []
        for a in range(na):
            sent.append(copy(a, 0, srcs[a], me, sibling))
            sent += [copy(a, 1 + j, srcs[a], me, (*chip, ci)) for j, chip in enumerate(chips)]
        for cp in sent:
            cp.start()
        for j, (cx, cy) in enumerate(chips):
            slot = 4 * cx + 2 * cy + ci
            for a in range(na):
                copy(a, 1 + j, srcs[a], slot, sibling).wait_recv()
                cp = copy(a, 4 + j, outs[a].at[slot], slot, sibling)
                cp.start()
                sent.append(cp)
        for a in range(na):
            copy(a, 0, srcs[a], me, sibling).wait_recv()
            for j in range(len(chips)):
                copy(a, 4 + j, srcs[a], me, sibling).wait_recv()
        for cp in sent:
            cp.wait_send()
        for cp in own:
            cp.wait()

    out_type = [_sds((N_DEV,) + a.shape, a.dtype) for a in arrays]
    sent_bytes = sum(a.size * a.dtype.itemsize for a in arrays)
    return pl.kernel(
        body, out_type=out_type, mesh=plsc.ScalarSubcoreMesh(axis_name="seq", num_cores=1), name=name,
        scratch_types=[pltpu.SemaphoreType.DMA((na * npair,)), pltpu.SemaphoreType.DMA((na * npair,)),
                       pltpu.SemaphoreType.DMA((na,))],
        compiler_params=pltpu.CompilerParams(collective_id=collective_id),
        cost_estimate=pl.CostEstimate(flops=0, transcendentals=0, bytes_accessed=2 * N_DEV * sent_bytes,
                                      remote_bytes_transferred=npair * sent_bytes),
    )(*arrays, *([] if after is None else [after]))


def _adamw(w, g, m, v):
    m = ADAM_B1 * m + (1.0 - ADAM_B1) * g
    v = ADAM_B2 * v + (1.0 - ADAM_B2) * jnp.square(g)
    m_hat = m / (1.0 - ADAM_B1 ** ADAM_STEP)
    v_hat = v / (1.0 - ADAM_B2 ** ADAM_STEP)
    delta = -ADAM_LR * (m_hat / (jnp.sqrt(v_hat) + ADAM_EPS) + ADAM_WD * w)
    return delta, m, v


def _sum_devices(ref):
    g = ref[0].astype(F32)
    for k in range(1, N_DEV):
        g = g + ref[k].astype(F32)
    return g


def _update_big(parts, w, m, v, name):
    rows, cols = w.shape[0], w.shape[-1]
    mid = (None,) * (w.ndim - 2)
    mid0 = (0,) * (w.ndim - 2)

    def body(p_ref, w_ref, m_ref, v_ref, g_ref, d_ref, nm_ref, nv_ref):
        g = _sum_devices(p_ref)
        g_ref[...] = g
        d_ref[...], nm_ref[...], nv_ref[...] = _adamw(w_ref[...], g, m_ref[...], v_ref[...])

    if rows % 16 == 0:
        tr = _pick_tile(rows, (128, 64, 16))
        steps, blk = rows // tr, pl.BlockSpec((tr,) + mid + (cols,), lambda i: (i,) + mid0 + (0,))
        pblk = pl.BlockSpec((N_DEV, tr, cols), lambda i: (0, i, 0))
    else:
        tcol = 2 * LANES
        steps, blk = cols // tcol, pl.BlockSpec((rows,) + mid + (tcol,), lambda i: (0,) + mid0 + (i,))
        pblk = pl.BlockSpec((N_DEV, rows, tcol), lambda i: (0, 0, i))
    return pl.pallas_call(
        body, name=name, grid=(steps,),
        in_specs=[pblk, blk, blk, blk],
        out_specs=[blk] * 4, out_shape=[_sds(w.shape, F32)] * 4,
        compiler_params=_params(("parallel",)),
    )(parts, w, m, v)


_VEC_ORDER = ("ln_in_g", "ln_in_b", "conv_b", "conv_ln_g", "conv_ln_b", "gate_bias", "gla_norm_g",
              "ln1_g", "ln1_b", "ln2_g", "ln2_b")
_SHARDED_SMALL = (("meta_tokens", 0, N_META, LANES), ("conv_w", N_META, CONV_WIDTH, None), ("gate_up", N_META + 32, GLA_RANK, None))


def _update_small(parts_sh, parts_vec, wmv):
    names = [s[0] for s in _SHARDED_SMALL] + list(_VEC_ORDER)
    flat = [a for nme in names for a in wmv[nme]]
    nv = len(_VEC_ORDER)

    def body(*refs):
        sh_ref, vec_ref = refs[0], refs[1]
        ins = refs[2:2 + len(flat)]
        outs = refs[2 + len(flat):2 + len(flat) + 4 * len(names)]
        loss_ref = refs[2 + len(flat) + 4 * len(names)]
        gsh_ref, gvec_ref = refs[-2:]
        gsh_ref[...] = _sum_devices(sh_ref)
        gvec_ref[...] = _sum_devices(vec_ref)
        loss_ref[...] = gvec_ref[nv:nv + 1, :]
        for idx, nme in enumerate(names):
            w_ref, m_ref, v_ref = ins[3 * idx:3 * idx + 3]
            rows, cols = w_ref.shape[0], w_ref.shape[-1]
            at = (slice(None),) + (0,) * (len(w_ref.shape) - 2) + (slice(None),)
            if idx < len(_SHARDED_SMALL):
                r0 = _SHARDED_SMALL[idx][1]
                g = gsh_ref[r0:r0 + rows, 0:cols]
            else:
                j = idx - len(_SHARDED_SMALL)
                g = gvec_ref[j:j + 1, 0:cols]
            o = outs[4 * idx:4 * idx + 4]
            o[0][at] = g
            o[1][at], o[2][at], o[3][at] = _adamw(w_ref[at], g, m_ref[at], v_ref[at])

    out_shape = [_sds(wmv[nme][0].shape, F32) for nme in names for _ in range(4)] + [_sds((1, parts_vec.shape[2]), F32)]
    vmem = pl.BlockSpec(memory_space=pltpu.VMEM)
    res = pl.pallas_call(
        body, name="update_small", out_shape=out_shape,
        in_specs=[vmem] * (2 + len(flat)), out_specs=[vmem] * len(out_shape),
        scratch_shapes=[pltpu.VMEM(parts_sh.shape[1:], F32), pltpu.VMEM(parts_vec.shape[1:], F32)],
    )(parts_sh, parts_vec, *flat)
    return {nme: res[4 * i:4 * i + 4] for i, nme in enumerate(names)}, res[-1][0, 0]


_WEIGHTS = ("meta_tokens", "ln_in_g", "ln_in_b", "w_in", "conv_w", "conv_b", "conv_ln_g", "conv_ln_b", "gate_up",
            "gate_bias", "gla_norm_g", "w_out", "ln1_g", "ln1_b", "w_ff1", "w_ff2", "ln2_g", "ln2_b")


def kernel(x, meta_tokens, ln_in_g, ln_in_b, w_in, conv_w, conv_b, conv_ln_g, conv_ln_b, gate_up, gate_bias, gla_norm_g, w_out, ln1_g, ln1_b, w_ff1, w_ff2, ln2_g, ln2_b, loss_target, m_meta_tokens, m_ln_in_g, m_ln_in_b, m_w_in, m_conv_w, m_conv_b, m_conv_ln_g, m_conv_ln_b, m_gate_up, m_gate_bias, m_gla_norm_g, m_w_out, m_ln1_g, m_ln1_b, m_w_ff1, m_w_ff2, m_ln2_g, m_ln2_b, v_meta_tokens, v_ln_in_g, v_ln_in_b, v_w_in, v_conv_w, v_conv_b, v_conv_ln_g, v_conv_ln_b, v_gate_up, v_gate_bias, v_gla_norm_g, v_w_out, v_ln1_g, v_ln1_b, v_w_ff1, v_w_ff2, v_ln2_g, v_ln2_b):
    w = dict(meta_tokens=meta_tokens, ln_in_g=ln_in_g, ln_in_b=ln_in_b, w_in=w_in, conv_w=conv_w, conv_b=conv_b,
             conv_ln_g=conv_ln_g, conv_ln_b=conv_ln_b, gate_up=gate_up, gate_bias=gate_bias, gla_norm_g=gla_norm_g,
             w_out=w_out, ln1_g=ln1_g, ln1_b=ln1_b, w_ff1=w_ff1, w_ff2=w_ff2, ln2_g=ln2_g, ln2_b=ln2_b)
    mom = dict(meta_tokens=m_meta_tokens, ln_in_g=m_ln_in_g, ln_in_b=m_ln_in_b, w_in=m_w_in, conv_w=m_conv_w,
               conv_b=m_conv_b, conv_ln_g=m_conv_ln_g, conv_ln_b=m_conv_ln_b, gate_up=m_gate_up, gate_bias=m_gate_bias,
               gla_norm_g=m_gla_norm_g, w_out=m_w_out, ln1_g=m_ln1_g, ln1_b=m_ln1_b, w_ff1=m_w_ff1, w_ff2=m_w_ff2,
               ln2_g=m_ln2_g, ln2_b=m_ln2_b)
    var = dict(meta_tokens=v_meta_tokens, ln_in_g=v_ln_in_g, ln_in_b=v_ln_in_b, w_in=v_w_in, conv_w=v_conv_w,
               conv_b=v_conv_b, conv_ln_g=v_conv_ln_g, conv_ln_b=v_conv_ln_b, gate_up=v_gate_up, gate_bias=v_gate_bias,
               gla_norm_g=v_gla_norm_g, w_out=v_w_out, ln1_g=v_ln1_g, ln1_b=v_ln1_b, w_ff1=v_w_ff1, w_ff2=v_w_ff2,
               ln2_g=v_ln2_g, ln2_b=v_ln2_b)
    shapes = {k: a.shape for k, a in w.items()}

    def two_d(a):
        return a.reshape(1, -1) if a.ndim == 1 else a.reshape(a.shape[-2:])

    w2d = {k: two_d(a) for k, a in w.items()}
    m2d = {k: two_d(a) for k, a in mom.items()}
    v2d = {k: two_d(a) for k, a in var.items()}
    d = x.shape[-1]
    d_in = w2d["w_in"].shape[1] * N_DEV
    d_in_p = -(-d_in // LANES) * LANES

    in_wmv = [jnp.transpose(dct["w_in"], (2, 0, 1)) for dct in (w, mom, var)]
    g_in, g_meta, g_conv, g_gup = _sc_gather(
        [w2d["w_in"].T.astype(BF16), w2d["meta_tokens"], w2d["conv_w"], w2d["gate_up"]], "gather_first", 0)
    g_out, g_ff1, g_ff2 = _sc_gather(
        [w2d["w_out"].astype(BF16), w2d["w_ff1"].astype(BF16), w2d["w_ff2"].astype(BF16)], "gather_late", 1)
    w_in_full = jnp.pad(g_in.reshape(d_in, d), ((0, d_in_p - d_in), (0, 0)))
    meta_full = g_meta.transpose(1, 0, 2).reshape(N_META, d)
    conv_w_full = g_conv.transpose(1, 0, 2).reshape(CONV_WIDTH, -1)
    gate_up_full = g_gup.transpose(1, 0, 2).reshape(GLA_RANK, -1)

    late_weights = (g_out.reshape(-1, d), g_ff1, g_ff2.reshape(-1, d))
    pushed = {}

    def push(tag, grads):
        if tag == "ff":
            pushed["ff1"], pushed["ff2"] = _sc_exchange(list(grads), [True, True], "scatter_ff", 2)
        elif tag == "out":
            pushed["p_out"] = grads[0].reshape(N_DEV, -1, d)
        else:
            p_in = grads[0][:d_in].reshape(N_DEV, d_in // N_DEV, d)
            pushed["in"], pushed["out"] = _sc_exchange([p_in, pushed["p_out"]], [True, True], "scatter_rest", 3,
                                                       after=pushed["ff1"])

    res = _local_step(x, loss_target, meta_full, w2d["ln_in_g"], w2d["ln_in_b"], w_in_full, conv_w_full, w2d["conv_b"],
                      w2d["conv_ln_g"], w2d["conv_ln_b"], gate_up_full, w2d["gate_bias"], w2d["gla_norm_g"], late_weights,
                      w2d["ln1_g"], w2d["ln1_b"], w2d["ln2_g"], w2d["ln2_b"], push)

    dc = res["conv_w"].shape[1]
    hk = res["gate_up"].shape[1]
    sh_meta = res["meta_tokens"].reshape(N_META, N_DEV, LANES).transpose(1, 0, 2)
    sh_conv = jnp.pad(res["conv_w"].reshape(CONV_WIDTH, N_DEV, dc // N_DEV).transpose(1, 0, 2),
                      ((0, 0), (0, 32 - CONV_WIDTH), (0, LANES - dc // N_DEV)))
    sh_gup = jnp.pad(res["gate_up"].reshape(GLA_RANK, N_DEV, hk // N_DEV).transpose(1, 0, 2),
                     ((0, 0), (0, 0), (0, LANES - hk // N_DEV)))
    p_sh = jnp.concatenate([sh_meta, sh_conv, sh_gup], axis=1)
    p_vec = jnp.concatenate([jnp.pad(res[k], ((0, 0), (0, d - res[k].shape[1]))) for k in _VEC_ORDER]
                            + [jnp.full((1, d), res["loss"], F32), jnp.zeros((15 - len(_VEC_ORDER), d), F32)], axis=0)

    r_ff1, r_ff2, r_out, r_in = pushed["ff1"], pushed["ff2"], pushed["out"], pushed["in"]
    upd = {}
    upd["w_ff1"] = _update_big(r_ff1, w2d["w_ff1"], m2d["w_ff1"], v2d["w_ff1"], "update_w_ff1")
    upd["w_ff2"] = _update_big(r_ff2, w2d["w_ff2"], m2d["w_ff2"], v2d["w_ff2"], "update_w_ff2")
    p_sh, p_vec, upd["w_ff1"], upd["w_ff2"] = lax.optimization_barrier((p_sh, p_vec, upd["w_ff1"], upd["w_ff2"]))
    r_sh, r_vec = _exchange([p_sh, p_vec], [True, False], "scatter_small")

    upd["w_in"] = [jnp.transpose(a, (1, 2, 0)) for a in _update_big(r_in, *in_wmv, "update_w_in")]
    upd["w_out"] = _update_big(r_out, w2d["w_out"], m2d["w_out"], v2d["w_out"], "update_w_out")
    small = [s[0] for s in _SHARDED_SMALL] + list(_VEC_ORDER)
    wmv = {k: (w2d[k], m2d[k], v2d[k]) for k in small}
    wmv["conv_w"] = tuple(jnp.transpose(dct["conv_w"], (1, 0, 2)) for dct in (w, mom, var))
    upd_small, loss = _update_small(r_sh, r_vec, wmv)
    upd.update(upd_small)

    outs = [loss, res["grad_x"]]
    for j in range(4):
        outs += [upd[k][j].reshape(shapes[k]) for k in _WEIGHTS]
    return tuple(outs)
```

```python
import jax
import jax.numpy as jnp
from jax import lax
from jax.experimental import pallas as pl
from jax.experimental.pallas import tpu as pltpu
from jax.experimental.pallas import tpu_sc as plsc

F32 = jnp.float32
BF16 = jnp.bfloat16

N_META = 16
CHUNK = 64
PAD_FRONT = (-N_META) % CHUNK
X_OFF = PAD_FRONT + N_META
CONV_WIDTH = 31
CONV_HALO = 32
CONV_SUB = 64
CONV_WIN = CONV_SUB + CONV_HALO
GLA_HEADS = 4
GLA_DK = 64
GLA_DV = 128
GLA_RANK = 16
GLA_TAU = 16.0
QK_SCALE = GLA_DK ** -0.5
LN_EPS = 1e-5
ALPHA = 2.0 ** 0.25
LANES = 128
N_DEV = 8
ADAM_LR = 0.001
ADAM_B1 = 0.9
ADAM_B2 = 0.999
ADAM_EPS = 1e-08
ADAM_WD = 0.01
ADAM_STEP = 10
VMEM_LIMIT = 56 * 1024 * 1024
MESH_AXES = ("x", "y", "c")
U_QK, U_V, U_R, U_GD = 2, 3, 4, 20
QK_WIDTH = 2 * GLA_HEADS * GLA_DK
GUP_WIDTH = GLA_HEADS * GLA_DK
DMI_GLA = 1


def _sds(shape, dtype):
    return jax.ShapeDtypeStruct(shape, dtype)


def _mm(a, b):
    return jnp.dot(a, b, preferred_element_type=F32)


def _mm_nt(a, b):
    return lax.dot_general(a, b, (((1,), (1,)), ((), ())), preferred_element_type=F32)


def _mm_tn(a, b):
    return lax.dot_general(a, b, (((0,), (0,)), ((), ())), preferred_element_type=F32)


def _sigmoid(x):
    return 1.0 / (1.0 + jnp.exp(-x))


def _log_sigmoid(z):
    return jnp.minimum(z, 0.0) - jnp.log(1.0 + jnp.exp(-jnp.abs(z)))


def _ln(x):
    mu = jnp.mean(x, axis=-1, keepdims=True)
    xc = x - mu
    var = jnp.mean(xc * xc, axis=-1, keepdims=True)
    rstd = lax.rsqrt(var + LN_EPS)
    return xc * rstd, rstd


def _ln_bwd(dyg, xhat, rstd):
    m1 = jnp.mean(dyg, axis=-1, keepdims=True)
    m2 = jnp.mean(dyg * xhat, axis=-1, keepdims=True)
    return rstd * (dyg - m1 - xhat * m2)


def _rowsum(x):
    return jnp.sum(x, axis=0, keepdims=True)


def _row_in_seq(i, tm, tp):
    base = lax.rem(i * tm, tp)
    return base + lax.broadcasted_iota(jnp.int32, (tm, 1), 0)


def _split3(x):
    hi = x.astype(BF16)
    r1 = x - hi.astype(F32)
    mid = r1.astype(BF16)
    lo = (r1 - mid.astype(F32)).astype(BF16)
    return hi, mid, lo


def _params(sem):
    return pltpu.CompilerParams(dimension_semantics=sem, vmem_limit_bytes=VMEM_LIMIT)


def _pick_tile(n, prefs):
    for t in prefs:
        if n % t == 0:
            return t
    raise ValueError(f"no tile for {n}")


ROW_BLOCKS = 2


def _row_blocks(tm, matmuls, finish):
    blocks = [slice(k * tm // ROW_BLOCKS, (k + 1) * tm // ROW_BLOCKS) for k in range(ROW_BLOCKS)]
    acc = matmuls(blocks[0])
    for prev, rows in zip(blocks, blocks[1:]):
        nxt = matmuls(rows)
        finish(prev, acc)
        acc = nxt
    finish(blocks[-1], acc)


def _x_tile_row(tp, seq, tx):
    tps = seq // tx
    return lambda i: pl.multiple_of((i // tps) * tp + X_OFF + (i % tps) * tx, CHUNK)


def _ln_in_x(x2, g, b, tp, seq, tx):
    rx, d = x2.shape
    r = rx // seq * tp
    row = _x_tile_row(tp, seq, tx)

    def body(x_ref, g_ref, b_ref, s0_ref, sb_ref):
        xhat, _ = _ln(x_ref[...])
        s = xhat * g_ref[...] + b_ref[...]
        s0_ref[...] = s
        sb_ref[...] = s.astype(BF16)

    out = pl.BlockSpec((pl.Element(tx), pl.Element(d)), lambda i: (row(i), 0))
    return pl.pallas_call(
        body, name="ln_in_x", grid=(rx // tx,),
        in_specs=[pl.BlockSpec((tx, d), lambda i: (i, 0)), pl.BlockSpec((1, d), lambda i: (0, 0)),
                  pl.BlockSpec((1, d), lambda i: (0, 0))],
        out_specs=[out, out],
        out_shape=[_sds((r, d), F32), _sds((r, d), BF16)],
        compiler_params=_params(("parallel",)),
    )(x2, g, b)


def _ln_in_head(head, g, b, s0, s0b, tp):
    r, d = s0.shape
    nb = tp // X_OFF

    def body(h_ref, g_ref, b_ref, s0_in, sb_in, s0_ref, sb_ref):
        xhat, _ = _ln(h_ref[...])
        real = lax.broadcasted_iota(jnp.int32, (X_OFF, 1), 0) >= PAD_FRONT
        s = jnp.where(real, xhat * g_ref[...] + b_ref[...], 0.0)
        s0_ref[...] = s
        sb_ref[...] = s.astype(BF16)

    anyspec = pl.BlockSpec(memory_space=pl.ANY)
    out = pl.BlockSpec((X_OFF, d), lambda i: (i * nb, 0))
    return pl.pallas_call(
        body, name="ln_in_head", grid=(r // tp,),
        in_specs=[pl.BlockSpec((X_OFF, d), lambda i: (0, 0)), pl.BlockSpec((1, d), lambda i: (0, 0)),
                  pl.BlockSpec((1, d), lambda i: (0, 0)), anyspec, anyspec],
        out_specs=[out, out],
        out_shape=[_sds((r, d), F32), _sds((r, d), BF16)],
        input_output_aliases={3: 0, 4: 1},
        compiler_params=_params(("parallel",)),
    )(head, g, b, s0, s0b)


def _inproj_fwd(s0b, w_in, tm):
    r, d = s0b.shape
    n = w_in.shape[0]

    def body(s_ref, w_ref, u_ref):
        u_ref[...] = _mm_nt(s_ref[...], w_ref[...])

    return pl.pallas_call(
        body, name="inproj_fwd", grid=(r // tm,),
        in_specs=[pl.BlockSpec((tm, d), lambda i: (i, 0)), pl.BlockSpec((n, d), lambda i: (0, 0))],
        out_specs=pl.BlockSpec((tm, n), lambda i: (i, 0)),
        out_shape=_sds((r, n), F32),
        compiler_params=_params(("parallel",)),
    )(s0b, w_in)


def _conv_taps(win, coef, lo):
    acc = None
    for rho in range(8):
        offs = [o for o in range(lo, lo + CONV_WIDTH) if o % 8 == rho]
        if not offs:
            continue
        rolled = win if rho == 0 else pltpu.roll(win, CONV_WIN - rho, 0)
        for o in offs:
            m8 = o - rho
            term = rolled[m8:m8 + CONV_SUB, :] * coef(o)
            acc = term if acc is None else acc + term
    return acc


def _conv_fwd(u, w32, cb, cg, cbe, tp, tc, dc):
    r = u.shape[0]
    hb = tc // CONV_HALO

    def body(a_ref, g_ref, ah_ref, gh_ref, w_ref, cb_ref, cg_ref, cbe_ref, c_ref, co_ref, hs_ref):
        t = pl.program_id(0)
        first = lax.rem(t * tc, tp) == 0
        hh = ah_ref[...] * _sigmoid(gh_ref[...])
        hs_ref[0:CONV_HALO, :] = jnp.where(first, 0.0, hh)
        hs_ref[CONV_HALO:CONV_HALO + tc, :] = a_ref[...] * _sigmoid(g_ref[...])

        def sub(k, carry):
            r0 = pl.multiple_of(k * CONV_SUB, CONV_SUB)
            win = hs_ref[pl.ds(r0, CONV_WIN), :]
            c = _conv_taps(win, lambda o: w_ref[o - 2:o - 1, :], 2) + cb_ref[...]
            c_ref[pl.ds(r0, CONV_SUB), :] = c
            xhat, _ = _ln(c)
            cn = xhat * cg_ref[...] + cbe_ref[...]
            co_ref[pl.ds(r0, CONV_SUB), :] = (cn * _sigmoid(cn)).astype(BF16)
            return carry

        lax.fori_loop(0, tc // CONV_SUB, sub, 0)

    vec = pl.BlockSpec((1, dc), lambda t: (0, 0))
    return pl.pallas_call(
        body, name="conv_fwd", grid=(r // tc,),
        in_specs=[pl.BlockSpec((tc, dc), lambda t: (t, 0)), pl.BlockSpec((tc, dc), lambda t: (t, 1)),
                  pl.BlockSpec((CONV_HALO, dc), lambda t: (jnp.maximum(t * hb - 1, 0), 0)),
                  pl.BlockSpec((CONV_HALO, dc), lambda t: (jnp.maximum(t * hb - 1, 0), 1)),
                  pl.BlockSpec((32, dc), lambda t: (0, 0)), vec, vec, vec],
        out_specs=[pl.BlockSpec((tc, dc), lambda t: (t, 0)), pl.BlockSpec((tc, dc), lambda t: (t, 0))],
        out_shape=[_sds((r, dc), F32), _sds((r, dc), BF16)],
        scratch_shapes=[pltpu.VMEM((CONV_HALO + tc, dc), F32)],
        compiler_params=_params(("parallel",)),
    )(u, u, u, u, w32, cb, cg, cbe)


def _tri_mm_all(tri, xs):
    parts = [_split3(x) for x in xs]
    acc = [None] * len(xs)
    for t in range(3):
        for j in range(len(xs)):
            term = _mm(tri, parts[j][t])
            acc[j] = term if t == 0 else acc[j] + term
    return acc


def _gla_prep(qk_ref, gd_ref, gup, gb, n0, kc):
    rows = [slice(j * CHUNK, (j + 1) * CHUNK) for j in range(kc)]
    ri = lax.broadcasted_iota(jnp.int32, (CHUNK, CHUNK), 0)
    ci = lax.broadcasted_iota(jnp.int32, (CHUNK, CHUNK), 1)
    low = (ri >= ci).astype(BF16)
    hk = GLA_HEADS * GLA_DK
    gds = [gd_ref[rw, :] for rw in rows]
    zs = [_mm(g.astype(BF16), gup) + gb for g in gds]
    reals = [(n0 + j) * CHUNK + lax.broadcasted_iota(jnp.int32, (CHUNK, 1), 0) >= PAD_FRONT for j in range(kc)]
    lgs = [jnp.where(reals[j], _log_sigmoid(zs[j]) * (1.0 / GLA_TAU), 0.0) for j in range(kc)]
    bs = _tri_mm_all(low, lgs)
    out = []
    for j in range(kc):
        b, bl = bs[j], _rowsum(lgs[j])
        q = qk_ref[rows[j], :hk] * QK_SCALE
        k = qk_ref[rows[j], hk:]
        eb, enb, ebl = jnp.exp(b), jnp.exp(-b), jnp.exp(bl - b)
        out.append(dict(rows=rows[j], gd=gds[j], z=zs[j], real=reals[j], eb=eb, enb=enb, ebl=ebl, gam=jnp.exp(bl),
                        qe=q * eb, ke=k * enb, kd=k * ebl))
    return out, ri, ci


def _gla_heads(p, v_ref):
    ops = []
    for h in range(GLA_HEADS):
        hp, h2 = divmod(h, 2)
        ls = slice(hp * LANES, (hp + 1) * LANES)
        m = _head_mask(h2)
        ops.append(dict(ls=ls, m=m, vs=slice(h * GLA_DV, (h + 1) * GLA_DV),
                        qe=jnp.where(m, p["qe"][:, ls], 0.0).astype(BF16),
                        kd=jnp.where(m, p["kd"][:, ls], 0.0).astype(BF16),
                        ke=p["ke"][:, ls].astype(BF16),
                        v=v_ref[p["rows"], h * GLA_DV:(h + 1) * GLA_DV].astype(BF16)))
    return ops


def _head_mask(h2):
    lane = lax.broadcasted_iota(jnp.int32, (1, LANES), 1)
    return (lane < GLA_DK) if h2 == 0 else (lane >= GLA_DK)


def _gla_fwd(u, gup, gb, gn, bsz, nc, kc):
    r = u.shape[0]
    hv = GLA_HEADS * GLA_DV
    ns = nc // kc

    def body(qk_ref, v_ref, r_ref, gd_ref, gup_ref, gb_ref, gn_ref, go_ref, sta_ref, st_ref):
        t = pl.program_id(1)

        @pl.when(t == 0)
        def _():
            st_ref[...] = jnp.zeros_like(st_ref)

        ps, ri, ci = _gla_prep(qk_ref, gd_ref, gup_ref[...], gb_ref[...], t * kc, kc)
        tril = ri >= ci
        items = [(j, h) for j in range(kc) for h in range(GLA_HEADS)]
        ops = [_gla_heads(p, v_ref) for p in ps]
        a = {jh: jnp.where(tril, _mm_nt(ops[jh[0]][jh[1]]["qe"], ops[jh[0]][jh[1]]["ke"]), 0.0).astype(BF16) for jh in items}
        oi = {jh: _mm(a[jh], ops[jh[0]][jh[1]]["v"]) for jh in items}
        inc = {jh: _mm_tn(ops[jh[0]][jh[1]]["v"], ops[jh[0]][jh[1]]["kd"]) for jh in items}
        sts = [st_ref[h] for h in range(GLA_HEADS)]
        for j, h in items:
            op, p = ops[j][h], ps[j]
            st = sts[h]
            sta_ref[j, h] = st
            o = oi[j, h] + _mm_nt(op["qe"], st.astype(BF16))
            sts[h] = st * p["gam"][:, op["ls"]] + inc[j, h]
            rs = lax.rsqrt(jnp.mean(o * o, axis=-1, keepdims=True) + LN_EPS)
            rr = r_ref[p["rows"], op["vs"]]
            go_ref[p["rows"], op["vs"]] = (o * rs * gn_ref[...] * (rr * _sigmoid(rr))).astype(BF16)
        for h in range(GLA_HEADS):
            st_ref[h] = sts[h]

    rowblk = lambda col: (lambda b, t: (b * ns + t, col))
    const = lambda b, t: (0, 0)
    return pl.pallas_call(
        body, name="gla_fwd", grid=(bsz, ns),
        in_specs=[pl.BlockSpec((kc * CHUNK, QK_WIDTH), rowblk(U_QK)), pl.BlockSpec((kc * CHUNK, hv), rowblk(U_V)),
                  pl.BlockSpec((kc * CHUNK, hv), rowblk(U_R)), pl.BlockSpec((kc * CHUNK, LANES), rowblk(U_GD)),
                  pl.BlockSpec((LANES, GUP_WIDTH), const), pl.BlockSpec((1, GUP_WIDTH), const), pl.BlockSpec((1, GLA_DV), const)],
        out_specs=[pl.BlockSpec((kc * CHUNK, hv), rowblk(0)),
                   pl.BlockSpec((kc, GLA_HEADS, LANES, LANES), lambda b, t: (b * ns + t, 0, 0, 0))],
        out_shape=[_sds((r, hv), BF16), _sds((bsz * nc, GLA_HEADS, LANES, LANES), F32)],
        scratch_shapes=[pltpu.VMEM((GLA_HEADS, LANES, LANES), F32)],
        compiler_params=_params(("parallel", "arbitrary")),
    )(u, u, u, u, gup, gb, gn)


def _outproj_fwd(s0, co, go, w_out, g1, b1, tm):
    r, d = s0.shape
    dc = co.shape[1]

    def body(s0_ref, co_ref, go_ref, w_ref, g_ref, b_ref, p1_ref, s1_ref, s1b_ref):
        nb = 4 if tm % 64 == 0 else 1
        blocks = [slice(k * (tm // nb), (k + 1) * (tm // nb)) for k in range(nb)]
        mixes = [_mm(co_ref[rows, :], w_ref[0:dc, :]) + _mm(go_ref[rows, :], w_ref[dc:2 * dc, :]) for rows in blocks]
        for rows, mix in zip(blocks, mixes):
            p1 = ALPHA * s0_ref[rows, :] + mix
            p1_ref[rows, :] = p1
            xhat, _ = _ln(p1)
            s1 = xhat * g_ref[...] + b_ref[...]
            s1_ref[rows, :] = s1
            s1b_ref[rows, :] = s1.astype(BF16)

    row = lambda w: pl.BlockSpec((tm, w), lambda i: (i, 0))
    vec = pl.BlockSpec((1, d), lambda i: (0, 0))
    return pl.pallas_call(
        body, name="outproj_fwd", grid=(r // tm,),
        in_specs=[row(d), row(dc), row(dc), pl.BlockSpec((2 * dc, d), lambda i: (0, 0)), vec, vec],
        out_specs=[row(d), row(d), row(d)],
        out_shape=[_sds((r, d), F32), _sds((r, d), F32), _sds((r, d), BF16)],
        compiler_params=_params(("parallel",)),
    )(s0, co, go, w_out, g1, b1)


def _mlp_fwd_resident(s1, s1b, w1g, w2, g2, b2, tgt, tp, tm):
    r, d = s1.shape
    nh, _, th = w1g.shape

    def body(s1_ref, sb_ref, w1_ref, w2_ref, g_ref, b_ref, t_ref, hm_ref, dp2_ref, dpb_ref, loss_ref, dg_ref, db_ref):
        i = pl.program_id(0)

        @pl.when(i == 0)
        def _():
            loss_ref[...] = jnp.zeros_like(loss_ref)
            dg_ref[...] = jnp.zeros_like(dg_ref)
            db_ref[...] = jnp.zeros_like(db_ref)

        def mlp_rows(rows):
            x = sb_ref[rows, :]
            acc = None
            h_next = _mm(x, w1_ref[0])
            for s in range(nh):
                h = h_next
                if s + 1 < nh:
                    h_next = _mm(x, w1_ref[s + 1])
                hm_ref[rows, s * th:(s + 1) * th] = h.astype(BF16)
                act = jnp.square(jnp.maximum(h, 0.0))
                p = _mm(act.astype(BF16), w2_ref[s * th:(s + 1) * th, :])
                acc = p if acc is None else acc + p
            return acc

        isx = _row_in_seq(i, tm, tp) >= X_OFF
        tg = t_ref[...]
        tg = jnp.where(i == 0, pltpu.roll(tg, X_OFF, 0), tg)

        def finish(rows, acc):
            p2 = ALPHA * s1_ref[rows, :] + acc
            xhat, rstd = _ln(p2)
            s2 = xhat * g_ref[...] + b_ref[...]
            err = jnp.where(isx[rows], s2 - tg[rows], 0.0)
            loss_ref[...] += 0.5 * jnp.sum(jnp.mean(err * err, axis=-1, keepdims=True))
            dy = err * (1.0 / d)
            dg_ref[...] += _rowsum(dy * xhat)
            db_ref[...] += _rowsum(dy)
            dp2 = _ln_bwd(dy * g_ref[...], xhat, rstd)
            dp2_ref[rows, :] = dp2
            dpb_ref[rows, :] = dp2.astype(BF16)

        _row_blocks(tm, mlp_rows, finish)

    row = pl.BlockSpec((tm, d), lambda i: (i, 0))
    vec = pl.BlockSpec((1, d), lambda i: (0, 0))
    held = pl.Buffered(1)
    tgt_row = pl.BlockSpec((pl.Element(tm), pl.Element(d)),
                           lambda i: (pl.multiple_of(jnp.maximum(i * tm - X_OFF * ((i * tm) // tp + 1), 0), 8), 0))
    return pl.pallas_call(
        body, name="mlp_fwd", grid=(r // tm,),
        in_specs=[row, row, pl.BlockSpec((nh, d, th), lambda i: (0, 0, 0), pipeline_mode=held),
                  pl.BlockSpec((nh * th, d), lambda i: (0, 0), pipeline_mode=held), vec, vec, tgt_row],
        out_specs=[pl.BlockSpec((tm, nh * th), lambda i: (i, 0)), row, row,
                   pl.BlockSpec((8, LANES), lambda i: (0, 0)), vec, vec],
        out_shape=[_sds((r, nh * th), BF16), _sds((r, d), F32), _sds((r, d), BF16), _sds((8, LANES), F32),
                   _sds((1, d), F32), _sds((1, d), F32)],
        compiler_params=_params(("arbitrary",)),
    )(s1, s1b, w1g, w2, g2, b2, tgt)


def _mlp_fwd(s1, s1b, w1g, w2, g2, b2, tgt, tp, tm, ns):
    r, d = s1.shape
    nh, _, th = w1g.shape
    nj = nh // ns

    def body(s1_ref, sb_ref, w1_ref, w2_ref, g_ref, b_ref, t_ref, hm_ref, dp2_ref, dpb_ref, loss_ref, dg_ref, db_ref, acc_ref):
        i = pl.program_id(0)
        j = pl.program_id(1)

        @pl.when(jnp.logical_and(i == 0, j == 0))
        def _():
            loss_ref[...] = jnp.zeros_like(loss_ref)
            dg_ref[...] = jnp.zeros_like(dg_ref)
            db_ref[...] = jnp.zeros_like(db_ref)

        @pl.when(j == 0)
        def _():
            acc_ref[...] = jnp.zeros_like(acc_ref)

        def mlp_rows(rows):
            hs = [_mm(sb_ref[rows, :], w1_ref[s]) for s in range(ns)]
            acc = acc_ref[rows, :]
            for s in range(ns):
                hm_ref[rows, s * th:(s + 1) * th] = hs[s].astype(BF16)
                act = jnp.square(jnp.maximum(hs[s], 0.0))
                acc = acc + _mm(act.astype(BF16), w2_ref[s * th:(s + 1) * th, :])
            return acc

        @pl.when(j < nj - 1)
        def _():
            acc_ref[...] = mlp_rows(slice(None))

        @pl.when(j == nj - 1)
        def _():
            isx = _row_in_seq(i, tm, tp) >= X_OFF
            tg = t_ref[...]
            tg = jnp.where(i == 0, pltpu.roll(tg, X_OFF, 0), tg)

            def finish(rows, acc):
                p2 = ALPHA * s1_ref[rows, :] + acc
                xhat, rstd = _ln(p2)
                s2 = xhat * g_ref[...] + b_ref[...]
                err = jnp.where(isx[rows], s2 - tg[rows], 0.0)
                loss_ref[...] += 0.5 * jnp.sum(jnp.mean(err * err, axis=-1, keepdims=True))
                dy = err * (1.0 / d)
                dg_ref[...] += _rowsum(dy * xhat)
                db_ref[...] += _rowsum(dy)
                dp2 = _ln_bwd(dy * g_ref[...], xhat, rstd)
                dp2_ref[rows, :] = dp2
                dpb_ref[rows, :] = dp2.astype(BF16)

            _row_blocks(tm, mlp_rows, finish)

    row = pl.BlockSpec((tm, d), lambda i, j: (i, 0))
    vec = pl.BlockSpec((1, d), lambda i, j: (0, 0))
    tgt_row = pl.BlockSpec((pl.Element(tm), pl.Element(d)),
                           lambda i, j: (pl.multiple_of(jnp.maximum(i * tm - X_OFF * ((i * tm) // tp + 1), 0), CHUNK), 0))
    return pl.pallas_call(
        body, name="mlp_fwd", grid=(r // tm, nj),
        in_specs=[row, row, pl.BlockSpec((ns, d, th), lambda i, j: (j, 0, 0)), pl.BlockSpec((ns * th, d), lambda i, j: (j, 0)),
                  vec, vec, tgt_row],
        out_specs=[pl.BlockSpec((tm, ns * th), lambda i, j: (i, j)), row, row,
                   pl.BlockSpec((8, LANES), lambda i, j: (0, 0)), vec, vec],
        out_shape=[_sds((r, nh * th), BF16), _sds((r, d), F32), _sds((r, d), BF16), _sds((8, LANES), F32),
                   _sds((1, d), F32), _sds((1, d), F32)],
        scratch_shapes=[pltpu.VMEM((tm, d), F32)],
        compiler_params=_params(("arbitrary", "arbitrary")),
    )(s1, s1b, w1g, w2, g2, b2, tgt)


def _mlp_bwd_act_resident(dp2, dpb, hm, w1g, w2, p1, g1, tm):
    r, d = dp2.shape
    nh, _, th = w1g.shape

    def body(dp2_ref, dpb_ref, hm_ref, w1_ref, w2_ref, p1_ref, g_ref, dh_ref, dp1_ref, dg_ref, db_ref):
        i = pl.program_id(0)

        @pl.when(i == 0)
        def _():
            dg_ref[...] = jnp.zeros_like(dg_ref)
            db_ref[...] = jnp.zeros_like(db_ref)

        def mlp_rows(rows):
            x = dpb_ref[rows, :]
            acc = None
            da_next = _mm_nt(x, w2_ref[0:th, :])
            for s in range(nh):
                dact = da_next
                if s + 1 < nh:
                    da_next = _mm_nt(x, w2_ref[(s + 1) * th:(s + 2) * th, :])
                cols = slice(s * th, (s + 1) * th)
                dh = (dact * (2.0 * jnp.maximum(hm_ref[rows, cols].astype(F32), 0.0))).astype(BF16)
                dh_ref[rows, cols] = dh
                p = _mm_nt(dh, w1_ref[s])
                acc = p if acc is None else acc + p
            return acc

        def finish(rows, acc):
            ds1 = ALPHA * dp2_ref[rows, :] + acc
            xhat, rstd = _ln(p1_ref[rows, :])
            dg_ref[...] += _rowsum(ds1 * xhat)
            db_ref[...] += _rowsum(ds1)
            dp1_ref[rows, :] = _ln_bwd(ds1 * g_ref[...], xhat, rstd)

        _row_blocks(tm, mlp_rows, finish)

    row = pl.BlockSpec((tm, d), lambda i: (i, 0))
    vec = pl.BlockSpec((1, d), lambda i: (0, 0))
    blk = pl.BlockSpec((tm, nh * th), lambda i: (i, 0))
    held = pl.Buffered(1)
    return pl.pallas_call(
        body, name="mlp_bwd_act", grid=(r // tm,),
        in_specs=[row, row, blk, pl.BlockSpec((nh, d, th), lambda i: (0, 0, 0), pipeline_mode=held),
                  pl.BlockSpec((nh * th, d), lambda i: (0, 0), pipeline_mode=held), row, vec],
        out_specs=[blk, row, vec, vec],
        out_shape=[_sds((r, nh * th), BF16), _sds((r, d), F32), _sds((1, d), F32), _sds((1, d), F32)],
        compiler_params=_params(("arbitrary",)),
    )(dp2, dpb, hm, w1g, w2, p1, g1)


def _mlp_bwd_act(dp2, dpb, hm, w1g, w2, p1, g1, tm, ns):
    r, d = dp2.shape
    nh, _, th = w1g.shape
    nj = nh // ns

    def body(dp2_ref, dpb_ref, hm_ref, w1_ref, w2_ref, p1_ref, g_ref, dh_ref, dp1_ref, dg_ref, db_ref, acc_ref):
        i = pl.program_id(0)
        j = pl.program_id(1)

        @pl.when(jnp.logical_and(i == 0, j == 0))
        def _():
            dg_ref[...] = jnp.zeros_like(dg_ref)
            db_ref[...] = jnp.zeros_like(db_ref)

        @pl.when(j == 0)
        def _():
            acc_ref[...] = jnp.zeros_like(acc_ref)

        def mlp_rows(rows):
            dacts = [_mm_nt(dpb_ref[rows, :], w2_ref[s * th:(s + 1) * th, :]) for s in range(ns)]
            acc = acc_ref[rows, :]
            for s in range(ns):
                cols = slice(s * th, (s + 1) * th)
                dh = (dacts[s] * (2.0 * jnp.maximum(hm_ref[rows, cols].astype(F32), 0.0))).astype(BF16)
                dh_ref[rows, cols] = dh
                acc = acc + _mm_nt(dh, w1_ref[s])
            return acc

        @pl.when(j < nj - 1)
        def _():
            acc_ref[...] = mlp_rows(slice(None))

        @pl.when(j == nj - 1)
        def _():
            def finish(rows, acc):
                ds1 = ALPHA * dp2_ref[rows, :] + acc
                xhat, rstd = _ln(p1_ref[rows, :])
                dg_ref[...] += _rowsum(ds1 * xhat)
                db_ref[...] += _rowsum(ds1)
                dp1_ref[rows, :] = _ln_bwd(ds1 * g_ref[...], xhat, rstd)

            _row_blocks(tm, mlp_rows, finish)

    row = pl.BlockSpec((tm, d), lambda i, j: (i, 0))
    vec = pl.BlockSpec((1, d), lambda i, j: (0, 0))
    blk = pl.BlockSpec((tm, ns * th), lambda i, j: (i, j))
    return pl.pallas_call(
        body, name="mlp_bwd_act", grid=(r // tm, nj),
        in_specs=[row, row, blk, pl.BlockSpec((ns, d, th), lambda i, j: (j, 0, 0)),
                  pl.BlockSpec((ns * th, d), lambda i, j: (j, 0)), row, vec],
        out_specs=[blk, row, vec, vec],
        out_shape=[_sds((r, nh * th), BF16), _sds((r, d), F32), _sds((1, d), F32), _sds((1, d), F32)],
        scratch_shapes=[pltpu.VMEM((tm, d), F32)],
        compiler_params=_params(("arbitrary", "arbitrary")),
    )(dp2, dpb, hm, w1g, w2, p1, g1)


def _mlp_bwd_w(s1b, hm, dh, dpb, nh, tm, ns):
    r, d = s1b.shape
    th = hm.shape[1] // nh

    def body(s1_ref, hm_ref, dh_ref, dp2_ref, dw1_ref, dw2_ref, a1_ref, a2_ref):
        i = pl.program_id(1)

        @pl.when(i == 0)
        def _():
            a1_ref[...] = jnp.zeros_like(a1_ref)
            a2_ref[...] = jnp.zeros_like(a2_ref)

        for s in range(ns):
            a1_ref[s] += _mm_tn(s1_ref[...], dh_ref[:, s * th:(s + 1) * th])
        for s in range(ns):
            act = jnp.square(jnp.maximum(hm_ref[:, s * th:(s + 1) * th].astype(F32), 0.0)).astype(BF16)
            a2_ref[s] += _mm_tn(act, dp2_ref[...])

        @pl.when(i == pl.num_programs(1) - 1)
        def _():
            dw1_ref[...] = a1_ref[...].astype(BF16)
            dw2_ref[...] = a2_ref[...].astype(BF16)

    row = pl.BlockSpec((tm, d), lambda j, i: (i, 0))
    blk = pl.BlockSpec((tm, ns * th), lambda j, i: (i, j))
    return pl.pallas_call(
        body, name="mlp_bwd_w", grid=(nh // ns, r // tm),
        in_specs=[row, blk, blk, row],
        out_specs=[pl.BlockSpec((ns, d, th), lambda j, i: (j, 0, 0)), pl.BlockSpec((ns, th, d), lambda j, i: (j, 0, 0))],
        out_shape=[_sds((nh, d, th), BF16), _sds((nh, th, d), BF16)],
        scratch_shapes=[pltpu.VMEM((ns, d, th), F32), pltpu.VMEM((ns, th, d), F32)],
        compiler_params=_params(("parallel", "arbitrary")),
    )(s1b, hm, dh, dpb)


def _outproj_bwd(dp1, co, go, w_out, tm):
    r, d = dp1.shape
    dc = co.shape[1]

    def body(dp_ref, co_ref, go_ref, w_ref, dmi_ref, dw_ref, acc_ref):
        i = pl.program_id(0)

        @pl.when(i == 0)
        def _():
            acc_ref[...] = jnp.zeros_like(acc_ref)

        dpb = dp_ref[...].astype(BF16)
        dmi_ref[...] = _mm_nt(dpb, w_ref[...])
        acc_ref[0:dc, :] += _mm_tn(co_ref[...], dpb)
        acc_ref[dc:2 * dc, :] += _mm_tn(go_ref[...], dpb)

        @pl.when(i == pl.num_programs(0) - 1)
        def _():
            dw_ref[...] = acc_ref[...].astype(BF16)

    row = lambda w: pl.BlockSpec((tm, w), lambda i: (i, 0))
    full = pl.BlockSpec((2 * dc, d), lambda i: (0, 0))
    return pl.pallas_call(
        body, name="outproj_bwd", grid=(r // tm,),
        in_specs=[row(d), row(dc), row(dc), full],
        out_specs=[row(2 * dc), full],
        out_shape=[_sds((r, 2 * dc), F32), _sds((2 * dc, d), BF16)],
        scratch_shapes=[pltpu.VMEM((2 * dc, d), F32)],
        compiler_params=_params(("arbitrary",)),
    )(dp1, co, go, w_out)


def _gla_bwd(u, dmi, sta, gup, gb, gn, bsz, nc, kc):
    r = u.shape[0]
    hv = GLA_HEADS * GLA_DV
    hk = GLA_HEADS * GLA_DK
    ns = nc // kc

    def body(qk_ref, v_ref, r_ref, gd_ref, dgo_ref, sta_ref, gup_ref, gb_ref, gn_ref,
             dqk_ref, dv_ref, dr_ref, dgd_ref, dgn_ref, dgb_ref, dgup_ref, dst_ref):
        bi = pl.program_id(0)
        t = pl.program_id(1)

        @pl.when(jnp.logical_and(bi == 0, t == 0))
        def _():
            dgn_ref[...] = jnp.zeros_like(dgn_ref)
            dgb_ref[...] = jnp.zeros_like(dgb_ref)
            dgup_ref[...] = jnp.zeros_like(dgup_ref)

        @pl.when(t == 0)
        def _():
            dst_ref[...] = jnp.zeros_like(dst_ref)

        ps, ri, ci = _gla_prep(qk_ref, gd_ref, gup_ref[...], gb_ref[...], (ns - 1 - t) * kc, kc)
        tril = ri >= ci
        items = [(j, h) for j in reversed(range(kc)) for h in range(GLA_HEADS)]
        ops = [_gla_heads(p, v_ref) for p in ps]
        op = lambda jh: ops[jh[0]][jh[1]]
        st = {jh: sta_ref[jh[0], jh[1]] for jh in items}
        stb = {jh: st[jh].astype(BF16) for jh in items}
        a = {jh: jnp.where(tril, _mm_nt(op(jh)["qe"], op(jh)["ke"]), 0.0).astype(BF16) for jh in items}
        o1 = {jh: _mm(a[jh], op(jh)["v"]) for jh in items}
        o2 = {jh: _mm_nt(op(jh)["qe"], stb[jh]) for jh in items}
        dob = {}
        dgn = jnp.zeros((1, GLA_DV), F32)
        for jh in items:
            rows, vs = ps[jh[0]]["rows"], op(jh)["vs"]
            o = o1[jh] + o2[jh]
            rr = r_ref[rows, vs]
            sr = _sigmoid(rr)
            rs = lax.rsqrt(jnp.mean(o * o, axis=-1, keepdims=True) + LN_EPS)
            y = o * rs
            dgo = dgo_ref[rows, vs]
            don = dgo * (rr * sr)
            dr_ref[rows, vs] = (dgo * (y * gn_ref[...]) * (sr * (1.0 + rr * (1.0 - sr)))).astype(BF16)
            dgn = dgn + _rowsum(don * y)
            dxn = don * gn_ref[...]
            dob[jh] = (rs * (dxn - y * jnp.mean(dxn * y, axis=-1, keepdims=True))).astype(BF16)
        da = {jh: jnp.where(tril, _mm_nt(dob[jh], op(jh)["v"]), 0.0).astype(BF16) for jh in items}
        dv1 = {jh: _mm_tn(a[jh], dob[jh]) for jh in items}
        dqe1 = {jh: _mm(da[jh], op(jh)["ke"]) for jh in items}
        dqe2 = {jh: _mm(dob[jh], stb[jh]) for jh in items}
        dke1 = {jh: _mm_tn(da[jh], op(jh)["qe"]) for jh in items}
        inc = {jh: _mm_tn(dob[jh], op(jh)["qe"]) for jh in items}
        dsts = [dst_ref[h] for h in range(GLA_HEADS)]
        dkd1, dgam1 = {}, {}
        for jh in items:
            j, h = jh
            dst = dsts[h]
            dstb = dst.astype(BF16)
            dv_ref[ps[j]["rows"], op(jh)["vs"]] = (dv1[jh] + _mm_nt(op(jh)["kd"], dstb)).astype(BF16)
            dkd1[jh] = _mm(op(jh)["v"], dstb)
            dgam1[jh] = _rowsum(dst * st[jh])
            dsts[h] = dst * ps[j]["gam"][:, op(jh)["ls"]] + inc[jh]
        for h in range(GLA_HEADS):
            dst_ref[h] = dsts[h]
        upper = (ri <= ci).astype(BF16)
        dbs, dbls = [], []
        for j in range(kc):
            p = ps[j]
            tiles = [[op((j, 2 * hp + h2)) for h2 in range(2)] for hp in range(GLA_HEADS // 2)]
            head = lambda d, hp, h2: d[j, 2 * hp + h2]
            lanes = lambda f: jnp.concatenate([f(hp) for hp in range(GLA_HEADS // 2)], axis=1)
            dqe = lanes(lambda hp: sum(jnp.where(tiles[hp][h2]["m"], head(dqe1, hp, h2) + head(dqe2, hp, h2), 0.0)
                                       for h2 in range(2)))
            dke = lanes(lambda hp: head(dke1, hp, 0) + head(dke1, hp, 1))
            dkd = lanes(lambda hp: sum(jnp.where(tiles[hp][h2]["m"], head(dkd1, hp, h2), 0.0) for h2 in range(2)))
            dgam = lanes(lambda hp: head(dgam1, hp, 0) + head(dgam1, hp, 1))
            dqk_ref[p["rows"], :hk] = (dqe * p["eb"] * QK_SCALE).astype(BF16)
            dqk_ref[p["rows"], hk:] = (dke * p["enb"] + dkd * p["ebl"]).astype(BF16)
            dkdkd = dkd * p["kd"]
            dbs.append(dqe * p["qe"] - dke * p["ke"] - dkdkd)
            dbls.append(_rowsum(dkdkd) + dgam * p["gam"])
        dlgs = _tri_mm_all(upper, dbs)
        dzb = []
        dgb = jnp.zeros((1, hk), F32)
        for j in range(kc):
            p = ps[j]
            dz = jnp.where(p["real"], (dlgs[j] + dbls[j]) * (1.0 / GLA_TAU) * _sigmoid(-p["z"]), 0.0)
            dgb = dgb + _rowsum(dz)
            dzb.append(dz.astype(BF16))
        dgup = sum(_mm_tn(ps[j]["gd"].astype(BF16), dzb[j]) for j in range(kc))
        for j in range(kc):
            dgd_ref[ps[j]["rows"], :] = _mm_nt(dzb[j], gup_ref[...]).astype(BF16)
        dgb_ref[...] += dgb
        dgup_ref[...] += dgup
        dgn_ref[...] += dgn

    rowblk = lambda col: (lambda b, t: (b * ns + ns - 1 - t, col))
    const = lambda b, t: (0, 0)
    return pl.pallas_call(
        body, name="gla_bwd", grid=(bsz, ns),
        in_specs=[pl.BlockSpec((kc * CHUNK, QK_WIDTH), rowblk(U_QK)), pl.BlockSpec((kc * CHUNK, hv), rowblk(U_V)),
                  pl.BlockSpec((kc * CHUNK, hv), rowblk(U_R)), pl.BlockSpec((kc * CHUNK, LANES), rowblk(U_GD)),
                  pl.BlockSpec((kc * CHUNK, hv), rowblk(DMI_GLA)),
                  pl.BlockSpec((kc, GLA_HEADS, LANES, LANES), lambda b, t: (b * ns + ns - 1 - t, 0, 0, 0)),
                  pl.BlockSpec((LANES, GUP_WIDTH), const), pl.BlockSpec((1, GUP_WIDTH), const), pl.BlockSpec((1, GLA_DV), const)],
        out_specs=[pl.BlockSpec((kc * CHUNK, 2 * hk), rowblk(0)), pl.BlockSpec((kc * CHUNK, hv), rowblk(0)),
                   pl.BlockSpec((kc * CHUNK, hv), rowblk(0)), pl.BlockSpec((kc * CHUNK, LANES), rowblk(0)),
                   pl.BlockSpec((1, GLA_DV), const), pl.BlockSpec((1, GUP_WIDTH), const),
                   pl.BlockSpec((LANES, GUP_WIDTH), const)],
        out_shape=[_sds((r, 2 * hk), BF16), _sds((r, hv), BF16), _sds((r, hv), BF16), _sds((r, LANES), BF16),
                   _sds((1, GLA_DV), F32), _sds((1, GUP_WIDTH), F32), _sds((LANES, GUP_WIDTH), F32)],
        scratch_shapes=[pltpu.VMEM((GLA_HEADS, LANES, LANES), F32)],
        compiler_params=_params(("arbitrary", "arbitrary")),
    )(u, u, u, u, dmi, sta, gup, gb, gn)


def _conv_bwd(u, c, dmi, w32, cg, cbe, tp, tc, dc):
    r = u.shape[0]
    hb = tc // CONV_HALO
    nhalo = r // CONV_HALO

    def dconv(cv, dco, cg_ref, cbe_ref):
        xhat, rstd = _ln(cv)
        cn = xhat * cg_ref[...] + cbe_ref[...]
        sg = _sigmoid(cn)
        dcn = dco * (sg * (1.0 + cn * (1.0 - sg)))
        return _ln_bwd(dcn * cg_ref[...], xhat, rstd), dcn, xhat

    def body(a_ref, g_ref, ah_ref, gh_ref, c_ref, dco_ref, ch_ref, dcoh_ref, w_ref, cg_ref, cbe_ref,
             du_ref, dw_ref, dcb_ref, dcg_ref, dcbe_ref, hs_ref, dcs_ref, dw8_ref):
        t = pl.program_id(0)

        @pl.when(t == 0)
        def _():
            dw8_ref[...] = jnp.zeros_like(dw8_ref)
            dcb_ref[...] = jnp.zeros_like(dcb_ref)
            dcg_ref[...] = jnp.zeros_like(dcg_ref)
            dcbe_ref[...] = jnp.zeros_like(dcbe_ref)

        first = lax.rem(t * tc, tp) == 0
        last = lax.rem((t + 1) * tc, tp) == 0
        hh = ah_ref[...] * _sigmoid(gh_ref[...])
        hs_ref[0:CONV_HALO, :] = jnp.where(first, 0.0, hh)
        hs_ref[CONV_HALO:CONV_HALO + tc, :] = a_ref[...] * _sigmoid(g_ref[...])
        dch, _, _ = dconv(ch_ref[...], dcoh_ref[...], cg_ref, cbe_ref)
        dcs_ref[tc:tc + CONV_HALO, :] = jnp.where(last, 0.0, dch)

        lrows = tc // 4

        def sub1(k, carry):
            r0 = pl.multiple_of(k * lrows, 8)
            dcv, dcn, xhat = dconv(c_ref[pl.ds(r0, lrows), :], dco_ref[pl.ds(r0, lrows), :], cg_ref, cbe_ref)
            dcs_ref[pl.ds(r0, lrows), :] = dcv
            dcb_ref[...] += _rowsum(dcv)
            dcg_ref[...] += _rowsum(dcn * xhat)
            dcbe_ref[...] += _rowsum(dcn)
            return carry

        lax.fori_loop(0, 4, sub1, 0)

        def sub2(k, carry):
            r0 = pl.multiple_of(k * CONV_SUB, CONV_SUB)
            dwin = dcs_ref[pl.ds(r0, CONV_WIN), :]
            dh = _conv_taps(dwin, lambda o: w_ref[CONV_WIDTH - 1 - o:CONV_WIDTH - o, :], 0)
            av = a_ref[pl.ds(r0, CONV_SUB), :]
            sg = _sigmoid(g_ref[pl.ds(r0, CONV_SUB), :])
            du_ref[pl.ds(r0, CONV_SUB), 0:dc] = (dh * sg).astype(BF16)
            du_ref[pl.ds(r0, CONV_SUB), dc:2 * dc] = (dh * av * sg * (1.0 - sg)).astype(BF16)
            hwin = hs_ref[pl.ds(r0, CONV_WIN), :]
            dcv = dwin[0:CONV_SUB, :]
            for rho in range(8):
                offs = [o for o in range(2, 2 + CONV_WIDTH) if o % 8 == rho]
                rolled = hwin if rho == 0 else pltpu.roll(hwin, CONV_WIN - rho, 0)
                for o in offs:
                    m8 = o - rho
                    prod = dcv * rolled[m8:m8 + CONV_SUB, :]
                    dw8_ref[8 * (o - 2):8 * (o - 1), :] += jnp.sum(prod.reshape(CONV_SUB // 8, 8, dc), axis=0)
            return carry

        lax.fori_loop(0, tc // CONV_SUB, sub2, 0)

        @pl.when(t == pl.num_programs(0) - 1)
        def _():
            dw_ref[...] = jnp.zeros_like(dw_ref)
            for j in range(CONV_WIDTH):
                dw_ref[j:j + 1, :] = _rowsum(dw8_ref[8 * j:8 * (j + 1), :])

    vec = pl.BlockSpec((1, dc), lambda t: (0, 0))
    prev = lambda col: (lambda t: (jnp.maximum(t * hb - 1, 0), col))
    nxt = lambda col: (lambda t: (jnp.minimum((t + 1) * hb, nhalo - 1), col))
    return pl.pallas_call(
        body, name="conv_bwd", grid=(r // tc,),
        in_specs=[pl.BlockSpec((tc, dc), lambda t: (t, 0)), pl.BlockSpec((tc, dc), lambda t: (t, 1)),
                  pl.BlockSpec((CONV_HALO, dc), prev(0)), pl.BlockSpec((CONV_HALO, dc), prev(1)),
                  pl.BlockSpec((tc, dc), lambda t: (t, 0)), pl.BlockSpec((tc, dc), lambda t: (t, 0)),
                  pl.BlockSpec((CONV_HALO, dc), nxt(0)), pl.BlockSpec((CONV_HALO, dc), nxt(0)),
                  pl.BlockSpec((32, dc), lambda t: (0, 0)), vec, vec],
        out_specs=[pl.BlockSpec((tc, 2 * dc), lambda t: (t, 0)), pl.BlockSpec((32, dc), lambda t: (0, 0)), vec, vec, vec],
        out_shape=[_sds((r, 2 * dc), BF16), _sds((32, dc), F32), _sds((1, dc), F32), _sds((1, dc), F32), _sds((1, dc), F32)],
        scratch_shapes=[pltpu.VMEM((CONV_HALO + tc, dc), F32), pltpu.VMEM((tc + CONV_HALO, dc), F32),
                        pltpu.VMEM((8 * 32, dc), F32)],
        compiler_params=_params(("arbitrary",)),
    )(u, u, u, u, c, dmi, c, dmi, w32, cg, cbe)


def _inproj_bwd(dp1, dus, xsrc, g_in, w_in, tp, seq, tx):
    r, d = dp1.shape
    widths = [x.shape[1] for x in dus]
    offs = [sum(widths[:k]) for k in range(len(widths))]
    n = w_in.shape[0]
    nd = len(dus)
    head = tx == 0
    rows = X_OFF if head else tx

    def body(*refs):
        dp_ref = refs[0]
        du_refs = refs[1:1 + nd]
        x_ref, g_ref, w_ref, out_ref, dg_ref, db_ref = refs[1 + nd:]
        i = pl.program_id(0)

        @pl.when(i == 0)
        def _():
            dg_ref[...] = jnp.zeros_like(dg_ref)
            db_ref[...] = jnp.zeros_like(db_ref)
            if head:
                out_ref[...] = jnp.zeros_like(out_ref)

        ds0 = ALPHA * dp_ref[...]
        for k in range(nd):
            ds0 = ds0 + _mm(du_refs[k][...], w_ref[offs[k]:offs[k] + widths[k], :])
        if head:
            ds0 = jnp.where(lax.broadcasted_iota(jnp.int32, (X_OFF, 1), 0) >= PAD_FRONT, ds0, 0.0)
        xhat, rstd = _ln(x_ref[...])
        dg_ref[...] += _rowsum(ds0 * xhat)
        db_ref[...] += _rowsum(ds0)
        dx = _ln_bwd(ds0 * g_ref[...], xhat, rstd)
        if head:
            out_ref[...] += dx[PAD_FRONT:X_OFF, :]
        else:
            out_ref[...] = dx

    if head:
        nb = tp // X_OFF
        row = lambda w: pl.BlockSpec((X_OFF, w), lambda i: (i * nb, 0))
        xspec = pl.BlockSpec((X_OFF, d), lambda i: (0, 0))
        ospec, oshape, steps = pl.BlockSpec((N_META, d), lambda i: (0, 0)), _sds((N_META, d), F32), r // tp
    else:
        start = _x_tile_row(tp, seq, tx)
        row = lambda w: pl.BlockSpec((pl.Element(tx), pl.Element(w)), lambda i: (start(i), 0))
        xspec = pl.BlockSpec((tx, d), lambda i: (i, 0))
        ospec, oshape, steps = xspec, _sds(xsrc.shape, F32), xsrc.shape[0] // tx
    vec = pl.BlockSpec((1, d), lambda i: (0, 0))
    return pl.pallas_call(
        body, name="inproj_bwd_head" if head else "inproj_bwd_x", grid=(steps,),
        in_specs=[row(d)] + [row(w) for w in widths] + [xspec, vec, pl.BlockSpec((n, d), lambda i: (0, 0))],
        out_specs=[ospec, vec, vec],
        out_shape=[oshape, _sds((1, d), F32), _sds((1, d), F32)],
        compiler_params=_params(("arbitrary",)),
    )(dp1, *dus, xsrc, g_in, w_in)


def _inproj_bwd_w(s0, dus, tm):
    r, d = s0.shape
    widths = [x.shape[1] for x in dus]
    offs = [sum(widths[:k]) for k in range(len(widths))]
    nd = len(dus)

    def body(*refs):
        s_ref = refs[0]
        du_refs = refs[1:1 + nd]
        dw_ref, acc_ref = refs[1 + nd:]
        i = pl.program_id(0)

        @pl.when(i == 0)
        def _():
            acc_ref[...] = jnp.zeros_like(acc_ref)

        for k in range(nd):
            acc_ref[offs[k]:offs[k] + widths[k], :] += _mm_tn(du_refs[k][...], s_ref[...])

        @pl.when(i == pl.num_programs(0) - 1)
        def _():
            dw_ref[...] = acc_ref[...].astype(BF16)

    row = lambda w: pl.BlockSpec((tm, w), lambda i: (i, 0))
    return pl.pallas_call(
        body, name="inproj_bwd_w", grid=(r // tm,),
        in_specs=[row(d)] + [row(w) for w in widths],
        out_specs=pl.BlockSpec((sum(widths), d), lambda i: (0, 0)),
        out_shape=_sds((sum(widths), d), BF16),
        scratch_shapes=[pltpu.VMEM((sum(widths), d), F32)],
        compiler_params=_params(("arbitrary",)),
    )(s0, *dus)


def _local_step(x, tgt, meta, ln_in_g, ln_in_b, w_in, conv_w, conv_b, conv_ln_g, conv_ln_b, gate_up, gate_bias,
                gla_norm_g, late_weights, ln1_g, ln1_b, ln2_g, ln2_b, push):
    bsz, seq, d = x.shape
    tp = X_OFF + seq
    assert tp % CHUNK == 0
    nc = tp // CHUNK
    dc = conv_b.shape[1]
    tmm = tc = _pick_tile(tp, (704, 128, 64))
    tx = _pick_tile(seq, (512, 64))
    kc = _pick_tile(nc, (11, 3, 2, 1))
    ns = 2

    x2 = x.reshape(bsz * seq, d)
    head = jnp.pad(meta, ((PAD_FRONT, 0), (0, 0)))
    tgt_p = tgt.reshape(bsz * seq, d)
    w32 = jnp.pad(conv_w, ((0, 32 - CONV_WIDTH), (0, 0)))
    gup = jnp.pad(gate_up, ((0, LANES - GLA_RANK), (0, 0))).astype(BF16)

    s0, s0b = _ln_in_x(x2, ln_in_g, ln_in_b, tp, seq, tx)
    s0, s0b = _ln_in_head(head, ln_in_g, ln_in_b, s0, s0b, tp)
    u = _inproj_fwd(s0b, w_in, tmm)
    c, co = _conv_fwd(u, w32, conv_b, conv_ln_g, conv_ln_b, tp, tc, dc)
    go, sta = _gla_fwd(u, gup, gate_bias, gla_norm_g, bsz, nc, kc)
    w_out, w1g, w2 = late_weights
    nh = w1g.shape[0]
    p1, s1, s1b = _outproj_fwd(s0, co, go, w_out, ln1_g, ln1_b, tmm)
    hm, dp2, dpb, loss, dg2, db2 = _mlp_fwd_resident(s1, s1b, w1g, w2, ln2_g, ln2_b, tgt_p, tp,
                                                     _pick_tile(tp, (352, 128, 64)))

    dh, dp1, dg1, db1 = _mlp_bwd_act_resident(dp2, dpb, hm, w1g, w2, p1, ln1_g, _pick_tile(tp, (352, 128, 64)))
    dw1, dw2 = _mlp_bwd_w(s1b, hm, dh, dpb, nh, tmm, ns)
    push("ff", (dw1, dw2))
    dmi, dwo = _outproj_bwd(dp1, co, go, w_out, tmm)
    push("out", (dwo,))
    dqk, dv, dr, dgd, dgn, dgb, dgup = _gla_bwd(u, dmi, sta, gup, gate_bias, gla_norm_g, bsz, nc, kc)
    dcv, dcw, dcb, dcg, dcbe = _conv_bwd(u, c, dmi, w32, conv_ln_g, conv_ln_b, tp, tc, dc)
    dus = [dcv, dqk, dv, dr, dgd]
    dwi = _inproj_bwd_w(s0b, dus, tmm)
    push("in", (dwi,))
    gx, dgx, dbx = _inproj_bwd(dp1, dus, x2, ln_in_g, w_in, tp, seq, tx)
    dmeta, dgh, dbh = _inproj_bwd(dp1, dus, head, ln_in_g, w_in, tp, seq, 0)

    return dict(loss=loss[0, 0], grad_x=gx.reshape(bsz, seq, d), meta_tokens=dmeta, ln_in_g=dgx + dgh, ln_in_b=dbx + dbh,
                conv_w=dcw[:CONV_WIDTH], conv_b=dcb, conv_ln_g=dcg, conv_ln_b=dcbe,
                gate_up=dgup[:GLA_RANK], gate_bias=dgb, gla_norm_g=dgn, ln1_g=dg1, ln1_b=db1, ln2_g=dg2, ln2_b=db2)


def _exchange(arrays, scatter, name):
    na = len(arrays)
    npeer = N_DEV - 1

    def body(*refs):
        srcs = refs[:na]
        outs = refs[na:2 * na]
        send_sems, recv_sems, local_sems = refs[2 * na:]
        xi, yi, ci = (lax.axis_index(a) for a in MESH_AXES)
        me = 4 * xi + 2 * yi + ci
        copies = []
        for a in range(na):
            own = srcs[a].at[me] if scatter[a] else srcs[a]
            cp = pltpu.make_async_copy(own, outs[a].at[me], local_sems.at[a])
            cp.start()
            copies.append(cp)
        remote = []
        for k in range(1, N_DEV):
            px, py, pc = xi ^ (k >> 2), yi ^ ((k >> 1) & 1), ci ^ (k & 1)
            peer = 4 * px + 2 * py + pc
            for a in range(na):
                src = srcs[a].at[peer] if scatter[a] else srcs[a]
                cp = pltpu.make_async_remote_copy(
                    src_ref=src, dst_ref=outs[a].at[me],
                    send_sem=send_sems.at[a * npeer + k - 1], recv_sem=recv_sems.at[a * npeer + k - 1],
                    device_id=(px, py, pc), device_id_type=pl.DeviceIdType.MESH)
                cp.start()
                remote.append(cp)
        for cp in remote:
            cp.wait()
        for cp in copies:
            cp.wait()

    out_shape = [_sds(a.shape if scatter[i] else (N_DEV,) + a.shape, a.dtype) for i, a in enumerate(arrays)]
    anyspec = pl.BlockSpec(memory_space=pl.ANY)
    return pl.pallas_call(
        body, name=name,
        in_specs=[anyspec] * na, out_specs=[anyspec] * na, out_shape=out_shape,
        scratch_shapes=[pltpu.SemaphoreType.DMA((na * npeer,)), pltpu.SemaphoreType.DMA((na * npeer,)),
                        pltpu.SemaphoreType.DMA((na,))],
    )(*arrays)


def _peers(xi, yi, ci):
    for k in range(1, N_DEV):
        px, py, pc = xi ^ (k >> 2), yi ^ ((k >> 1) & 1), ci ^ (k & 1)
        yield (px, py, pc), 4 * px + 2 * py + pc


def _sc_exchange(arrays, scatter, name, collective_id, after=None):
    na = len(arrays)
    npeer = N_DEV - 1
    ndep = 0 if after is None else 1

    def body(*refs):
        srcs = refs[:na]
        outs = refs[na + ndep:2 * na + ndep]
        send_sems, recv_sems, own_sems = refs[2 * na + ndep:]
        xi, yi, ci = (lax.axis_index(a) for a in MESH_AXES)
        me = 4 * xi + 2 * yi + ci
        barrier = pltpu.get_barrier_semaphore()
        for pos, _ in _peers(xi, yi, ci):
            pl.semaphore_signal(barrier, inc=1, device_id=pos, device_id_type=pl.DeviceIdType.MESH)
        pl.semaphore_wait(barrier, npeer)
        own = [pltpu.make_async_copy(srcs[a].at[me] if scatter[a] else srcs[a], outs[a].at[me], own_sems.at[a])
               for a in range(na)]
        for cp in own:
            cp.start()
        remote = []
        for a in range(na):
            for k, (pos, peer) in enumerate(_peers(xi, yi, ci)):
                cp = pltpu.make_async_remote_copy(
                    src_ref=srcs[a].at[peer] if scatter[a] else srcs[a], dst_ref=outs[a].at[me],
                    send_sem=send_sems.at[a * npeer + k], recv_sem=recv_sems.at[a * npeer + k],
                    device_id=pos, device_id_type=pl.DeviceIdType.MESH)
                cp.start()
                remote.append(cp)
        for cp in own:
            cp.wait()
        for cp in remote:
            cp.wait()

    out_type = [_sds(a.shape if scatter[i] else (N_DEV,) + a.shape, a.dtype) for i, a in enumerate(arrays)]
    sent = sum(a.size * a.dtype.itemsize // (N_DEV if scatter[i] else 1) for i, a in enumerate(arrays))
    return pl.kernel(
        body, out_type=out_type, mesh=plsc.ScalarSubcoreMesh(axis_name="seq", num_cores=1), name=name,
        scratch_types=[pltpu.SemaphoreType.DMA((na * npeer,)), pltpu.SemaphoreType.DMA((na * npeer,)),
                       pltpu.SemaphoreType.DMA((na,))],
        compiler_params=pltpu.CompilerParams(collective_id=collective_id),
        cost_estimate=pl.CostEstimate(flops=0, transcendentals=0, bytes_accessed=2 * N_DEV * sent,
                                      remote_bytes_transferred=npeer * sent),
    )(*arrays, *([] if after is None else [after]))


def _sc_gather(arrays, name, collective_id, after=None):
    na = len(arrays)
    ndep = 0 if after is None else 1
    npair = N_DEV - 1

    def body(*refs):
        srcs = refs[:na]
        outs = refs[na + ndep:2 * na + ndep]
        send_sems, recv_sems, own_sems = refs[2 * na + ndep:]
        xi, yi, ci = (lax.axis_index(a) for a in MESH_AXES)
        me = 4 * xi + 2 * yi + ci
        sibling = (xi, yi, 1 - ci)
        chips = [(1 - xi, yi), (xi, 1 - yi), (1 - xi, 1 - yi)]
        barrier = pltpu.get_barrier_semaphore()
        for pos, _ in _peers(xi, yi, ci):
            pl.semaphore_signal(barrier, inc=1, device_id=pos, device_id_type=pl.DeviceIdType.MESH)
        pl.semaphore_wait(barrier, npair)

        def copy(a, k, src, slot, to):
            return pltpu.make_async_remote_copy(
                src_ref=src, dst_ref=outs[a].at[slot], send_sem=send_sems.at[a * npair + k],
                recv_sem=recv_sems.at[a * npair + k], device_id=to, device_id_type=pl.DeviceIdType.MESH)

        own = [pltpu.make_async_copy(srcs[a], outs[a].at[me], own_sems.at[a]) for a in range(na)]
        for cp in own:
            cp.start()
        sent = []
        for a in range(na):
            sent.append(copy(a, 0, srcs[a], me, sibling))
            sent += [copy(a, 1 + j, srcs[a], me, (*chip, ci)) for j, chip in enumerate(chips)]
        for cp in sent:
            cp.start()
        for j, (cx, cy) in enumerate(chips):
            slot = 4 * cx + 2 * cy + ci
            for a in range(na):
                copy(a, 1 + j, srcs[a], slot, sibling).wait_recv()
                cp = copy(a, 4 + j, outs[a].at[slot], slot, sibling)
                cp.start()
                sent.append(cp)
        for a in range(na):
            copy(a, 0, srcs[a], me, sibling).wait_recv()
            for j in range(len(chips)):
                copy(a, 4 + j, srcs[a], me, sibling).wait_recv()
        for cp in sent:
            cp.wait_send()
        for cp in own:
            cp.wait()

    out_type = [_sds((N_DEV,) + a.shape, a.dtype) for a in arrays]
    sent_bytes = sum(a.size * a.dtype.itemsize for a in arrays)
    return pl.kernel(
        body, out_type=out_type, mesh=plsc.ScalarSubcoreMesh(axis_name="seq", num_cores=1), name=name,
        scratch_types=[pltpu.SemaphoreType.DMA((na * npair,)), pltpu.SemaphoreType.DMA((na * npair,)),
                       pltpu.SemaphoreType.DMA((na,))],
        compiler_params=pltpu.CompilerParams(collective_id=collective_id),
        cost_estimate=pl.CostEstimate(flops=0, transcendentals=0, bytes_accessed=2 * N_DEV * sent_bytes,
                                      remote_bytes_transferred=npair * sent_bytes),
    )(*arrays, *([] if after is None else [after]))


def _adamw(w, g, m, v):
    m = ADAM_B1 * m + (1.0 - ADAM_B1) * g
    v = ADAM_B2 * v + (1.0 - ADAM_B2) * jnp.square(g)
    m_hat = m / (1.0 - ADAM_B1 ** ADAM_STEP)
    v_hat = v / (1.0 - ADAM_B2 ** ADAM_STEP)
    delta = -ADAM_LR * (m_hat / (jnp.sqrt(v_hat) + ADAM_EPS) + ADAM_WD * w)
    return delta, m, v


def _sum_devices(ref):
    g = ref[0].astype(F32)
    for k in range(1, N_DEV):
        g = g + ref[k].astype(F32)
    return g


def _update_big(parts, w, m, v, name):
    rows, cols = w.shape[0], w.shape[-1]
    mid = (None,) * (w.ndim - 2)
    mid0 = (0,) * (w.ndim - 2)

    def body(p_ref, w_ref, m_ref, v_ref, g_ref, d_ref, nm_ref, nv_ref):
        g = _sum_devices(p_ref)
        g_ref[...] = g
        d_ref[...], nm_ref[...], nv_ref[...] = _adamw(w_ref[...], g, m_ref[...], v_ref[...])

    if rows % 16 == 0:
        tr = _pick_tile(rows, (128, 64, 16))
        steps, blk = rows // tr, pl.BlockSpec((tr,) + mid + (cols,), lambda i: (i,) + mid0 + (0,))
        pblk = pl.BlockSpec((N_DEV, tr, cols), lambda i: (0, i, 0))
    else:
        tcol = 2 * LANES
        steps, blk = cols // tcol, pl.BlockSpec((rows,) + mid + (tcol,), lambda i: (0,) + mid0 + (i,))
        pblk = pl.BlockSpec((N_DEV, rows, tcol), lambda i: (0, 0, i))
    return pl.pallas_call(
        body, name=name, grid=(steps,),
        in_specs=[pblk, blk, blk, blk],
        out_specs=[blk] * 4, out_shape=[_sds(w.shape, F32)] * 4,
        compiler_params=_params(("parallel",)),
    )(parts, w, m, v)


_VEC_ORDER = ("ln_in_g", "ln_in_b", "conv_b", "conv_ln_g", "conv_ln_b", "gate_bias", "gla_norm_g",
              "ln1_g", "ln1_b", "ln2_g", "ln2_b")
_SHARDED_SMALL = (("meta_tokens", 0, N_META, LANES), ("conv_w", N_META, CONV_WIDTH, None), ("gate_up", N_META + 32, GLA_RANK, None))


def _update_small(parts_sh, parts_vec, wmv):
    names = [s[0] for s in _SHARDED_SMALL] + list(_VEC_ORDER)
    flat = [a for nme in names for a in wmv[nme]]
    nv = len(_VEC_ORDER)

    def body(*refs):
        sh_ref, vec_ref = refs[0], refs[1]
        ins = refs[2:2 + len(flat)]
        outs = refs[2 + len(flat):2 + len(flat) + 4 * len(names)]
        loss_ref = refs[2 + len(flat) + 4 * len(names)]
        gsh_ref, gvec_ref = refs[-2:]
        gsh_ref[...] = _sum_devices(sh_ref)
        gvec_ref[...] = _sum_devices(vec_ref)
        loss_ref[...] = gvec_ref[nv:nv + 1, :]
        for idx, nme in enumerate(names):
            w_ref, m_ref, v_ref = ins[3 * idx:3 * idx + 3]
            rows, cols = w_ref.shape[0], w_ref.shape[-1]
            at = (slice(None),) + (0,) * (len(w_ref.shape) - 2) + (slice(None),)
            if idx < len(_SHARDED_SMALL):
                r0 = _SHARDED_SMALL[idx][1]
                g = gsh_ref[r0:r0 + rows, 0:cols]
            else:
                j = idx - len(_SHARDED_SMALL)
                g = gvec_ref[j:j + 1, 0:cols]
            o = outs[4 * idx:4 * idx + 4]
            o[0][at] = g
            o[1][at], o[2][at], o[3][at] = _adamw(w_ref[at], g, m_ref[at], v_ref[at])

    out_shape = [_sds(wmv[nme][0].shape, F32) for nme in names for _ in range(4)] + [_sds((1, parts_vec.shape[2]), F32)]
    vmem = pl.BlockSpec(memory_space=pltpu.VMEM)
    res = pl.pallas_call(
        body, name="update_small", out_shape=out_shape,
        in_specs=[vmem] * (2 + len(flat)), out_specs=[vmem] * len(out_shape),
        scratch_shapes=[pltpu.VMEM(parts_sh.shape[1:], F32), pltpu.VMEM(parts_vec.shape[1:], F32)],
    )(parts_sh, parts_vec, *flat)
    return {nme: res[4 * i:4 * i + 4] for i, nme in enumerate(names)}, res[-1][0, 0]


_WEIGHTS = ("meta_tokens", "ln_in_g", "ln_in_b", "w_in", "conv_w", "conv_b", "conv_ln_g", "conv_ln_b", "gate_up",
            "gate_bias", "gla_norm_g", "w_out", "ln1_g", "ln1_b", "w_ff1", "w_ff2", "ln2_g", "ln2_b")


def kernel(x, meta_tokens, ln_in_g, ln_in_b, w_in, conv_w, conv_b, conv_ln_g, conv_ln_b, gate_up, gate_bias, gla_norm_g, w_out, ln1_g, ln1_b, w_ff1, w_ff2, ln2_g, ln2_b, loss_target, m_meta_tokens, m_ln_in_g, m_ln_in_b, m_w_in, m_conv_w, m_conv_b, m_conv_ln_g, m_conv_ln_b, m_gate_up, m_gate_bias, m_gla_norm_g, m_w_out, m_ln1_g, m_ln1_b, m_w_ff1, m_w_ff2, m_ln2_g, m_ln2_b, v_meta_tokens, v_ln_in_g, v_ln_in_b, v_w_in, v_conv_w, v_conv_b, v_conv_ln_g, v_conv_ln_b, v_gate_up, v_gate_bias, v_gla_norm_g, v_w_out, v_ln1_g, v_ln1_b, v_w_ff1, v_w_ff2, v_ln2_g, v_ln2_b):
    w = dict(meta_tokens=meta_tokens, ln_in_g=ln_in_g, ln_in_b=ln_in_b, w_in=w_in, conv_w=conv_w, conv_b=conv_b,
             conv_ln_g=conv_ln_g, conv_ln_b=conv_ln_b, gate_up=gate_up, gate_bias=gate_bias, gla_norm_g=gla_norm_g,
             w_out=w_out, ln1_g=ln1_g, ln1_b=ln1_b, w_ff1=w_ff1, w_ff2=w_ff2, ln2_g=ln2_g, ln2_b=ln2_b)
    mom = dict(meta_tokens=m_meta_tokens, ln_in_g=m_ln_in_g, ln_in_b=m_ln_in_b, w_in=m_w_in, conv_w=m_conv_w,
               conv_b=m_conv_b, conv_ln_g=m_conv_ln_g, conv_ln_b=m_conv_ln_b, gate_up=m_gate_up, gate_bias=m_gate_bias,
               gla_norm_g=m_gla_norm_g, w_out=m_w_out, ln1_g=m_ln1_g, ln1_b=m_ln1_b, w_ff1=m_w_ff1, w_ff2=m_w_ff2,
               ln2_g=m_ln2_g, ln2_b=m_ln2_b)
    var = dict(meta_tokens=v_meta_tokens, ln_in_g=v_ln_in_g, ln_in_b=v_ln_in_b, w_in=v_w_in, conv_w=v_conv_w,
               conv_b=v_conv_b, conv_ln_g=v_conv_ln_g, conv_ln_b=v_conv_ln_b, gate_up=v_gate_up, gate_bias=v_gate_bias,
               gla_norm_g=v_gla_norm_g, w_out=v_w_out, ln1_g=v_ln1_g, ln1_b=v_ln1_b, w_ff1=v_w_ff1, w_ff2=v_w_ff2,
               ln2_g=v_ln2_g, ln2_b=v_ln2_b)
    shapes = {k: a.shape for k, a in w.items()}

    def two_d(a):
        return a.reshape(1, -1) if a.ndim == 1 else a.reshape(a.shape[-2:])

    w2d = {k: two_d(a) for k, a in w.items()}
    m2d = {k: two_d(a) for k, a in mom.items()}
    v2d = {k: two_d(a) for k, a in var.items()}
    d = x.shape[-1]
    d_in = w2d["w_in"].shape[1] * N_DEV
    d_in_p = -(-d_in // LANES) * LANES

    in_wmv = [jnp.transpose(dct["w_in"], (2, 0, 1)) for dct in (w, mom, var)]
    g_in, g_meta, g_conv, g_gup = _sc_gather(
        [w2d["w_in"].T.astype(BF16), w2d["meta_tokens"], w2d["conv_w"], w2d["gate_up"]], "gather_first", 0)
    g_out, g_ff1, g_ff2 = _sc_gather(
        [w2d["w_out"].astype(BF16), w2d["w_ff1"].astype(BF16), w2d["w_ff2"].astype(BF16)], "gather_late", 1)
    w_in_full = jnp.pad(g_in.reshape(d_in, d), ((0, d_in_p - d_in), (0, 0)))
    meta_full = g_meta.transpose(1, 0, 2).reshape(N_META, d)
    conv_w_full = g_conv.transpose(1, 0, 2).reshape(CONV_WIDTH, -1)
    gate_up_full = g_gup.transpose(1, 0, 2).reshape(GLA_RANK, -1)

    late_weights = (g_out.reshape(-1, d), g_ff1, g_ff2.reshape(-1, d))
    pushed = {}

    def push(tag, grads):
        if tag == "ff":
            pushed["ff1"], pushed["ff2"] = _sc_exchange(list(grads), [True, True], "scatter_ff", 2)
        elif tag == "out":
            pushed["p_out"] = grads[0].reshape(N_DEV, -1, d)
        else:
            p_in = grads[0][:d_in].reshape(N_DEV, d_in // N_DEV, d)
            pushed["in"], pushed["out"] = _sc_exchange([p_in, pushed["p_out"]], [True, True], "scatter_rest", 3,
                                                       after=pushed["ff1"])

    res = _local_step(x, loss_target, meta_full, w2d["ln_in_g"], w2d["ln_in_b"], w_in_full, conv_w_full, w2d["conv_b"],
                      w2d["conv_ln_g"], w2d["conv_ln_b"], gate_up_full, w2d["gate_bias"], w2d["gla_norm_g"], late_weights,
                      w2d["ln1_g"], w2d["ln1_b"], w2d["ln2_g"], w2d["ln2_b"], push)

    dc = res["conv_w"].shape[1]
    hk = res["gate_up"].shape[1]
    sh_meta = res["meta_tokens"].reshape(N_META, N_DEV, LANES).transpose(1, 0, 2)
    sh_conv = jnp.pad(res["conv_w"].reshape(CONV_WIDTH, N_DEV, dc // N_DEV).transpose(1, 0, 2),
                      ((0, 0), (0, 32 - CONV_WIDTH), (0, LANES - dc // N_DEV)))
    sh_gup = jnp.pad(res["gate_up"].reshape(GLA_RANK, N_DEV, hk // N_DEV).transpose(1, 0, 2),
                     ((0, 0), (0, 0), (0, LANES - hk // N_DEV)))
    p_sh = jnp.concatenate([sh_meta, sh_conv, sh_gup], axis=1)
    p_vec = jnp.concatenate([jnp.pad(res[k], ((0, 0), (0, d - res[k].shape[1]))) for k in _VEC_ORDER]
                            + [jnp.full((1, d), res["loss"], F32), jnp.zeros((15 - len(_VEC_ORDER), d), F32)], axis=0)

    r_ff1, r_ff2, r_out, r_in = pushed["ff1"], pushed["ff2"], pushed["out"], pushed["in"]
    upd = {}
    upd["w_ff1"] = _update_big(r_ff1, w2d["w_ff1"], m2d["w_ff1"], v2d["w_ff1"], "update_w_ff1")
    upd["w_ff2"] = _update_big(r_ff2, w2d["w_ff2"], m2d["w_ff2"], v2d["w_ff2"], "update_w_ff2")
    p_sh, p_vec, upd["w_ff1"], upd["w_ff2"] = lax.optimization_barrier((p_sh, p_vec, upd["w_ff1"], upd["w_ff2"]))
    r_sh, r_vec = _exchange([p_sh, p_vec], [True, False], "scatter_small")

    upd["w_in"] = [jnp.transpose(a, (1, 2, 0)) for a in _update_big(r_in, *in_wmv, "update_w_in")]
    upd["w_out"] = _update_big(r_out, w2d["w_out"], m2d["w_out"], v2d["w_out"], "update_w_out")
    small = [s[0] for s in _SHARDED_SMALL] + list(_VEC_ORDER)
    wmv = {k: (w2d[k], m2d[k], v2d[k]) for k in small}
    wmv["conv_w"] = tuple(jnp.transpose(dct["conv_w"], (1, 0, 2)) for dct in (w, mom, var))
    upd_small, loss = _update_small(r_sh, r_vec, wmv)
    upd.update(upd_small)

    outs = [loss, res["grad_x"]]
    for j in range(4):
        outs += [upd[k][j].reshape(shapes[k]) for k in _WEIGHTS]
    return tuple(outs)
```
